```python
import math
import jax
import jax.numpy as jnp
from jax import lax
import numpy as np

D_MODEL = 1024
BATCH = 8
SEQ = 2048
DEPTH = 2

N_EVEN = (DEPTH + 1) // 2
N_ODD = DEPTH // 2
CHUNK = 64
NORM_EPS = 1e-6
RW_HEADS = 8
RW_HEAD_DIM = 64
RW_DIM = RW_HEADS * RW_HEAD_DIM
R_DECAY = 64
R_AAA = 64
R_GATE = 128
RW_IN = 3 * RW_DIM + R_DECAY + R_AAA + R_GATE
RW_LN_EPS = 64e-5
HG_HEADS = 8
HG_KEY_DIM = 64
HG_VAL_DIM = 64
HG_KDIM = HG_HEADS * HG_KEY_DIM
HG_VDIM = HG_HEADS * HG_VAL_DIM
HG_IN = 2 * HG_KDIM + 2 * HG_VDIM
EVEN_IN = RW_IN + HG_IN
EVEN_MIX = RW_DIM + HG_VDIM
GD_HEADS = 4
GD_HEAD_DIM = 128
GD_DIM = GD_HEADS * GD_HEAD_DIM
CONV_K = 4
GD_IN = 4 * GD_DIM + 2 * GD_HEADS
ML_HEADS = 4
ML_HEAD_DIM = 128
ML_DIM = ML_HEADS * ML_HEAD_DIM
ML_IN = 4 * ML_DIM + 2 * ML_HEADS
ODD_IN = GD_IN + ML_IN
ODD_MIX = GD_DIM + ML_DIM
N_GROUPS = 4
EXPERTS_PER_GROUP = 8
N_EXPERTS = N_GROUPS * EXPERTS_PER_GROUP
TOP_K_FINE = 2
D_EXPERT = 256

kernel_name = 'hybrid_rwkv7_hgrn2_gdn_mlstm_hmoe'


def _rms(x, g, eps=NORM_EPS):
    xf = x.astype(jnp.float32)
    y = xf * lax.rsqrt(jnp.mean(xf * xf, axis=-1, keepdims=True) + eps)
    return (y * g.astype(jnp.float32)).astype(x.dtype)


def _l2n(t):
    return t * lax.rsqrt(jnp.sum(t * t, axis=-1, keepdims=True) + 1e-6)


def _heads(t, h):
    return t.reshape(t.shape[:-1] + (h, t.shape[-1] // h))


def _shift(t):
    return jnp.pad(t[:, :-1], ((0, 0), (1, 0), (0, 0)))


def _to_chunks(t):
    b, s, h = t.shape[:3]
    t = t.reshape((b, s // CHUNK, CHUNK, h) + t.shape[3:])
    return jnp.moveaxis(t, (1, 3), (0, 2))


def _from_chunks(t):
    t = jnp.moveaxis(t, (0, 2), (1, 3))
    return t.reshape((t.shape[0], t.shape[1] * t.shape[2]) + t.shape[3:])


def _causal_conv(t, w):
    k = w.shape[0]
    n = t.shape[1]
    tp = jnp.pad(t, ((0, 0), (k - 1, 0), (0, 0)))
    y = tp[:, 0:n] * w[0]
    for j in range(1, k):
        y = y + tp[:, j:j + n] * w[j]
    return y


def _rwkv7_mix(p, mu, w0, w2, a0, a2, g2, k_k, k_a, r_k, ln_w, ln_b):
    bsz, n, _ = p.shape
    p = p + mu * (_shift(p) - p)
    r = p[..., :RW_DIM]
    k = p[..., RW_DIM:2 * RW_DIM]
    v = p[..., 2 * RW_DIM:3 * RW_DIM]
    o1 = 3 * RW_DIM
    wl = p[..., o1:o1 + R_DECAY]
    al = p[..., o1 + R_DECAY:o1 + R_DECAY + R_AAA]
    gl = p[..., o1 + R_DECAY + R_AAA:]
    w = -jax.nn.softplus(-(w0 + jnp.tanh(wl) @ w2)) - 0.5
    decay = jnp.exp(-jnp.exp(w))
    a = jax.nn.sigmoid(a0 + al @ a2)
    g = jax.nn.sigmoid(gl) @ g2
    kk = _l2n(_heads(k * k_k, RW_HEADS))
    k = k * (1.0 + (a - 1.0) * k_a)
    rh, kh, vh, ah, wh = (_heads(t, RW_HEADS) for t in (r, k, v, a, decay))

    def step(s, inp):
        r_t, w_t, k_t, v_t, kk_t, a_t = inp
        sa = jnp.einsum('bhvk,bhk->bhv', s, -kk_t)
        s = s * w_t[:, :, None, :] + sa[..., None] * (kk_t * a_t)[:, :, None, :] + v_t[..., None] * k_t[:, :, None, :]
        return s, jnp.einsum('bhvk,bhk->bhv', s, r_t)

    tm = lambda t: jnp.moveaxis(t, 1, 0)
    s0 = jnp.zeros((bsz, RW_HEADS, RW_HEAD_DIM, RW_HEAD_DIM), jnp.float32)
    _, y = lax.scan(step, s0, (tm(rh), tm(wh), tm(kh), tm(vh), tm(kk), tm(ah)))
    y = jnp.moveaxis(y, 0, 1)
    mean = jnp.mean(y, axis=-1, keepdims=True)
    var = jnp.mean(jnp.square(y - mean), axis=-1, keepdims=True)
    y = ((y - mean) * lax.rsqrt(var + RW_LN_EPS)).reshape(bsz, n, RW_DIM) * ln_w + ln_b
    bonus = jnp.sum(rh * kh * r_k, axis=-1, keepdims=True) * vh
    return (y + bonus.reshape(bsz, n, RW_DIM)) * g


def _chunk_gla(q, k, v, logf):
    q, k, v, logf = (_to_chunks(t) for t in (q, k, v, logf))
    bcum = jnp.cumsum(logf, axis=-2)
    incl = jnp.tril(jnp.ones((CHUNK, CHUNK), dtype=bool))[:, :, None]

    def step(s, inp):
        qc, kc, vc, bc = inp
        b_last = bc[..., -1, :]
        o_inter = jnp.einsum('bhtd,bhdv->bhtv', qc * jnp.exp(bc), s)
        dec = jnp.exp(jnp.where(incl, bc[..., :, None, :] - bc[..., None, :, :], -jnp.inf))
        attn = jnp.einsum('bhtd,bhsd,bhtsd->bhts', qc, kc, dec)
        o = o_inter + attn @ vc
        s = jnp.exp(b_last)[..., None] * s + jnp.einsum('bhsd,bhsv->bhdv', kc * jnp.exp(b_last[..., None, :] - bc), vc)
        return s, o

    bsz, h = q.shape[1], q.shape[2]
    s0 = jnp.zeros((bsz, h, q.shape[-1], v.shape[-1]), jnp.float32)
    _, o = lax.scan(step, s0, (q, k, v, bcum))
    return _from_chunks(o)


def _hgrn2_mix(p, lb, norm_g):
    bsz, n, _ = p.shape
    q = jax.nn.silu(p[..., :HG_KDIM])
    f = p[..., HG_KDIM:2 * HG_KDIM]
    i = p[..., 2 * HG_KDIM:2 * HG_KDIM + HG_VDIM]
    g = p[..., 2 * HG_KDIM + HG_VDIM:]
    fg = lb + (1.0 - lb) * jax.nn.sigmoid(f)
    o = _chunk_gla(_heads(q, HG_HEADS), _heads(1.0 - fg, HG_HEADS), _heads(i, HG_HEADS), _heads(jnp.log(fg), HG_HEADS))
    o = _rms(o, norm_g).reshape(bsz, n, HG_VDIM)
    return o * jax.nn.silu(g)


def _chunk_gated_delta(q, k, v, g, beta):
    q, k, v, g, beta = (_to_chunks(t) for t in (q, k, v, g, beta))
    gam = jnp.cumsum(g, axis=-1)
    incl = jnp.tril(jnp.ones((CHUNK, CHUNK), dtype=bool))
    strict = jnp.tril(jnp.ones((CHUNK, CHUNK), dtype=bool), -1)
    decay = jnp.exp(jnp.where(incl, gam[..., :, None] - gam[..., None, :], -jnp.inf))
    k_beta = k * beta[..., None]
    m = jnp.where(strict, jnp.einsum('...td,...sd->...ts', k_beta, k) * decay, 0.0)
    eye = jnp.eye(CHUNK, dtype=m.dtype)
    tinv = lax.linalg.triangular_solve(eye + m, jnp.broadcast_to(eye, m.shape), left_side=True, lower=True, unit_diagonal=True)
    u = tinv @ (v * beta[..., None])
    w = tinv @ (k_beta * jnp.exp(gam)[..., None])
    attn = jnp.einsum('...td,...sd->...ts', q, k) * decay
    q_dec = q * jnp.exp(gam)[..., None]
    k_dec = k * jnp.exp(gam[..., -1:] - gam)[..., None]
    g_last = jnp.exp(gam[..., -1])

    def step(s, inp):
        u_c, w_c, a_c, qd, kd, gl = inp
        v_new = u_c - w_c @ s
        o = qd @ s + a_c @ v_new
        s = s * gl[..., None, None] + jnp.einsum('bhsd,bhsv->bhdv', kd, v_new)
        return s, o

    s0 = jnp.zeros((q.shape[1], q.shape[2], q.shape[-1], v.shape[-1]), jnp.float32)
    _, o = lax.scan(step, s0, (u, w, attn, q_dec, k_dec, g_last))
    return _from_chunks(o)


def _gdn_mix(p, conv_w, a_log, dt_bias, norm_g):
    bsz, n, _ = p.shape
    qkv = jax.nn.silu(_causal_conv(p[..., :3 * GD_DIM], conv_w))
    q = _l2n(_heads(qkv[..., :GD_DIM], GD_HEADS)) * GD_HEAD_DIM ** -0.5
    k = _l2n(_heads(qkv[..., GD_DIM:2 * GD_DIM], GD_HEADS))
    v = _heads(qkv[..., 2 * GD_DIM:], GD_HEADS)
    z = p[..., 3 * GD_DIM:4 * GD_DIM]
    beta = jax.nn.sigmoid(p[..., 4 * GD_DIM:4 * GD_DIM + GD_HEADS])
    g = -jnp.exp(a_log) * jax.nn.softplus(p[..., 4 * GD_DIM + GD_HEADS:] + dt_bias)
    o = _chunk_gated_delta(q, k, v, g, beta)
    o = _rms(o, norm_g).reshape(bsz, n, GD_DIM)
    return o * jax.nn.silu(z)


def _chunk_mlstm(q, k, v, ig, logf):
    q, k, v, ig, logf = (_to_chunks(t) for t in (q, k, v, ig, logf))
    bcum = jnp.cumsum(logf, axis=-1)
    incl = jnp.tril(jnp.ones((CHUNK, CHUNK), dtype=bool))

    def step(carry, inp):
        c, nvec, m = carry
        qc, kc, vc, igc, bc = inp
        b_last = bc[..., -1]
        d_log = jnp.where(incl, bc[..., :, None] - bc[..., None, :] + igc[..., None, :], -jnp.inf)
        inter_log = bc + m[..., None]
        m_t = jnp.maximum(inter_log, jnp.max(d_log, axis=-1))
        d_w = jnp.exp(d_log - m_t[..., None])
        inter_w = jnp.exp(inter_log - m_t)
        sc = jnp.einsum('bhtd,bhsd->bhts', qc, kc) * d_w
        num = inter_w[..., None] * jnp.einsum('bhtd,bhdv->bhtv', qc, c) + sc @ vc
        den = inter_w * jnp.einsum('bhtd,bhd->bht', qc, nvec) + jnp.sum(sc, axis=-1)
        h = num / jnp.maximum(jnp.abs(den), jnp.exp(-m_t))[..., None]
        upd_log = b_last[..., None] - bc + igc
        m_new = jnp.maximum(b_last + m, jnp.max(upd_log, axis=-1))
        w_s = jnp.exp(upd_log - m_new[..., None])
        dec = jnp.exp(b_last + m - m_new)
        c = dec[..., None, None] * c + jnp.einsum('bhs,bhsd,bhsv->bhdv', w_s, kc, vc)
        nvec = dec[..., None] * nvec + jnp.einsum('bhs,bhsd->bhd', w_s, kc)
        return (c, nvec, m_new), h

    bsz, h = q.shape[1], q.shape[2]
    c0 = jnp.zeros((bsz, h, q.shape[-1], v.shape[-1]), jnp.float32)
    n0 = jnp.zeros((bsz, h, q.shape[-1]), jnp.float32)
    m0 = jnp.zeros((bsz, h), jnp.float32)
    _, o = lax.scan(step, (c0, n0, m0), (q, k, v, ig, bcum))
    return _from_chunks(o)


def _mlstm_mix(p, i_bias, f_bias, norm_g):
    bsz, n, _ = p.shape
    q = _heads(p[..., :ML_DIM], ML_HEADS)
    k = _heads(p[..., ML_DIM:2 * ML_DIM], ML_HEADS) * ML_HEAD_DIM ** -0.5
    v = _heads(p[..., 2 * ML_DIM:3 * ML_DIM], ML_HEADS)
    o_gate = jax.nn.sigmoid(p[..., 3 * ML_DIM:4 * ML_DIM])
    ig = p[..., 4 * ML_DIM:4 * ML_DIM + ML_HEADS] + i_bias
    logf = jax.nn.log_sigmoid(p[..., 4 * ML_DIM + ML_HEADS:] + f_bias)
    h = _chunk_mlstm(q, k, v, ig, logf)
    h = _rms(h, norm_g.reshape(ML_HEADS, ML_HEAD_DIM)).reshape(bsz, n, ML_DIM)
    return o_gate * h


def _hmoe(x, w_group, b_group, w_router, b_router, w_gate, w_up, w_down):
    bsz, n, d = x.shape
    xt = x.reshape(-1, d)
    ntok = xt.shape[0]
    grp_logits = (xt @ w_group).astype(jnp.float32) + b_group.astype(jnp.float32)
    p_grp = jax.nn.softmax(grp_logits, axis=-1)
    p_top, g_idx = lax.top_k(p_grp, 1)
    fine = ((xt @ w_router).astype(jnp.float32) + b_router.astype(jnp.float32)).reshape(ntok, N_GROUPS, EXPERTS_PER_GROUP)
    fine_sel = fine[jnp.arange(ntok), g_idx[:, 0]]
    f_val, f_idx = lax.top_k(fine_sel, TOP_K_FINE)
    gate = p_top * jax.nn.softmax(f_val, axis=-1)
    eid = g_idx * EXPERTS_PER_GROUP + f_idx
    combine = jnp.sum(jax.nn.one_hot(eid, N_EXPERTS, dtype=jnp.float32) * gate[..., None], axis=1)
    y = jnp.zeros((ntok, d), jnp.float32)
    for grp in range(N_GROUPS):
        sl = slice(grp * EXPERTS_PER_GROUP, (grp + 1) * EXPERTS_PER_GROUP)
        hid = jax.nn.silu(jnp.einsum('nd,edf->nef', xt, w_gate[sl])) * jnp.einsum('nd,edf->nef', xt, w_up[sl])
        hid = hid * combine[:, sl, None].astype(hid.dtype)
        y = y + jnp.einsum('nef,efd->nd', hid, w_down[sl]).astype(jnp.float32)
    return y.reshape(bsz, n, d)


def setup_inputs(seed: int = 0) -> dict:
    key = jax.random.key(seed)
    keys = iter(jax.random.split(key, 48))
    f32 = jnp.float32

    def nrm(shape, scale):
        return jax.random.normal(next(keys), shape, f32) * scale

    def unif(shape, lo, hi):
        return jax.random.uniform(next(keys), shape, f32, lo, hi)

    ne, no = N_EVEN, N_ODD
    x = nrm((BATCH, SEQ, D_MODEL), 1.0)
    norm_mix = 1.0 + nrm((DEPTH, D_MODEL), 0.05)
    norm_ffn = 1.0 + nrm((DEPTH, D_MODEL), 0.05)
    norm_final = 1.0 + nrm((D_MODEL,), 0.05)
    ev_w_in = nrm((ne, D_MODEL, EVEN_IN), D_MODEL ** -0.5)
    ev_mu = unif((ne, RW_IN), 0.0, 1.0)
    rw_w0 = -1.0 + nrm((ne, RW_DIM), 0.5)
    rw_w2 = nrm((ne, R_DECAY, RW_DIM), 0.5 * R_DECAY ** -0.5)
    rw_a0 = nrm((ne, RW_DIM), 0.1)
    rw_a2 = nrm((ne, R_AAA, RW_DIM), 0.5 * R_AAA ** -0.5)
    rw_g2 = nrm((ne, R_GATE, RW_DIM), R_GATE ** -0.5)
    rw_k_k = 0.85 + nrm((ne, RW_DIM), 0.05)
    rw_k_a = 1.0 + nrm((ne, RW_DIM), 0.05)
    rw_r_k = nrm((ne, RW_HEADS, RW_HEAD_DIM), 0.1)
    rw_ln_w = 1.0 + nrm((ne, RW_DIM), 0.05)
    rw_ln_b = nrm((ne, RW_DIM), 0.02)
    hg_lb_logits = nrm((ne + 1, HG_KDIM), 0.5)
    hg_norm = 1.0 + nrm((ne, HG_VAL_DIM), 0.05)
    ev_w_out = nrm((ne, EVEN_MIX, D_MODEL), EVEN_MIX ** -0.5)
    od_w_in = nrm((no, D_MODEL, ODD_IN), D_MODEL ** -0.5)
    gd_conv = nrm((no, CONV_K, 3 * GD_DIM), CONV_K ** -0.5)
    gd_a_log = jnp.log(unif((no, GD_HEADS), 1.0, 16.0))
    dt = jnp.exp(unif((no, GD_HEADS), math.log(1e-3), math.log(1e-1)))
    gd_dt_bias = dt + jnp.log(-jnp.expm1(-dt))
    gd_norm = 1.0 + nrm((no, GD_HEAD_DIM), 0.05)
    ml_i_bias = nrm((no, ML_HEADS), 0.1)
    ml_f_bias = jnp.linspace(3.0, 6.0, ML_HEADS, dtype=f32) + nrm((no, ML_HEADS), 0.1)
    ml_norm = 1.0 + nrm((no, ML_DIM), 0.05)
    od_w_out = nrm((no, ODD_MIX, D_MODEL), ODD_MIX ** -0.5)
    moe_w_group = nrm((DEPTH, D_MODEL, N_GROUPS), D_MODEL ** -0.5)
    moe_b_group = nrm((DEPTH, N_GROUPS), 0.01)
    moe_w_router = nrm((DEPTH, D_MODEL, N_EXPERTS), D_MODEL ** -0.5)
    moe_b_router = nrm((DEPTH, N_EXPERTS), 0.01)
    moe_w_gate = nrm((DEPTH, N_EXPERTS, D_MODEL, D_EXPERT), D_MODEL ** -0.5)
    moe_w_up = nrm((DEPTH, N_EXPERTS, D_MODEL, D_EXPERT), D_MODEL ** -0.5)
    moe_w_down = nrm((DEPTH, N_EXPERTS, D_EXPERT, D_MODEL), D_EXPERT ** -0.5)
    return {'x': x, 'norm_mix': norm_mix, 'norm_ffn': norm_ffn, 'norm_final': norm_final,
            'ev_w_in': ev_w_in, 'ev_mu': ev_mu, 'rw_w0': rw_w0, 'rw_w2': rw_w2, 'rw_a0': rw_a0,
            'rw_a2': rw_a2, 'rw_g2': rw_g2, 'rw_k_k': rw_k_k, 'rw_k_a': rw_k_a, 'rw_r_k': rw_r_k,
            'rw_ln_w': rw_ln_w, 'rw_ln_b': rw_ln_b, 'hg_lb_logits': hg_lb_logits, 'hg_norm': hg_norm,
            'ev_w_out': ev_w_out, 'od_w_in': od_w_in, 'gd_conv': gd_conv, 'gd_a_log': gd_a_log,
            'gd_dt_bias': gd_dt_bias, 'gd_norm': gd_norm, 'ml_i_bias': ml_i_bias, 'ml_f_bias': ml_f_bias,
            'ml_norm': ml_norm, 'od_w_out': od_w_out, 'moe_w_group': moe_w_group, 'moe_b_group': moe_b_group,
            'moe_w_router': moe_w_router, 'moe_b_router': moe_b_router, 'moe_w_gate': moe_w_gate,
            'moe_w_up': moe_w_up, 'moe_w_down': moe_w_down}


def reference(x, norm_mix, norm_ffn, norm_final, ev_w_in, ev_mu, rw_w0, rw_w2, rw_a0, rw_a2, rw_g2,
              rw_k_k, rw_k_a, rw_r_k, rw_ln_w, rw_ln_b, hg_lb_logits, hg_norm, ev_w_out, od_w_in,
              gd_conv, gd_a_log, gd_dt_bias, gd_norm, ml_i_bias, ml_f_bias, ml_norm, od_w_out,
              moe_w_group, moe_b_group, moe_w_router, moe_b_router, moe_w_gate, moe_w_up, moe_w_down):
    lb_table = jnp.cumsum(jax.nn.softmax(hg_lb_logits.astype(jnp.float32), axis=0), axis=0)
    h = x
    for layer in range(DEPTH):
        hn = _rms(h, norm_mix[layer])
        j = layer // 2
        if layer % 2 == 0:
            p = jnp.matmul(hn, ev_w_in[j]).astype(jnp.float32)
            ya = _rwkv7_mix(p[..., :RW_IN], ev_mu[j], rw_w0[j], rw_w2[j], rw_a0[j], rw_a2[j], rw_g2[j],
                            rw_k_k[j], rw_k_a[j], rw_r_k[j], rw_ln_w[j], rw_ln_b[j])
            yb = _hgrn2_mix(p[..., RW_IN:], lb_table[j], hg_norm[j])
            mix = jnp.matmul(jnp.concatenate([ya, yb], axis=-1), ev_w_out[j])
        else:
            p = jnp.matmul(hn, od_w_in[j]).astype(jnp.float32)
            yc = _gdn_mix(p[..., :GD_IN], gd_conv[j], gd_a_log[j], gd_dt_bias[j], gd_norm[j])
            yd = _mlstm_mix(p[..., GD_IN:], ml_i_bias[j], ml_f_bias[j], ml_norm[j])
            mix = jnp.matmul(jnp.concatenate([yc, yd], axis=-1), od_w_out[j])
        h = h + mix.astype(h.dtype)
        ffn = _hmoe(_rms(h, norm_ffn[layer]), moe_w_group[layer], moe_b_group[layer], moe_w_router[layer],
                    moe_b_router[layer], moe_w_gate[layer], moe_w_up[layer], moe_w_down[layer])
        h = h + ffn.astype(h.dtype)
    return _rms(h, norm_final)
```

```python
import functools
import math

import jax
import jax.numpy as jnp
from jax import lax
from jax.experimental import pallas as pl
from jax.experimental.pallas import tpu as pltpu

F32 = jnp.float32
BF16 = jnp.bfloat16
I32 = jnp.int32
HIGHEST = lax.Precision.HIGHEST

D_MODEL = 1024
NORM_EPS = 1e-6
RW_HEADS, RW_HEAD_DIM = 8, 64
RW_DIM = RW_HEADS * RW_HEAD_DIM
R_DECAY, R_AAA, R_GATE = 64, 64, 128
RW_IN = 3 * RW_DIM + R_DECAY + R_AAA + R_GATE
RW_LN_EPS = 64e-5
HG_HEADS, HG_DIM = 8, 64
HG_KDIM = HG_HEADS * HG_DIM
HG_IN = 4 * HG_KDIM
GD_HEADS, GD_HEAD_DIM = 4, 128
GD_DIM = GD_HEADS * GD_HEAD_DIM
CONV_K = 4
GD_MAIN = 4 * GD_DIM
GD_IN = GD_MAIN + 2 * GD_HEADS
ML_HEADS, ML_HEAD_DIM = 4, 128
ML_DIM = ML_HEADS * ML_HEAD_DIM
ML_MAIN = 4 * ML_DIM
N_GROUPS, EXPERTS_PER_GROUP = 4, 8
N_EXPERTS = N_GROUPS * EXPERTS_PER_GROUP
D_EXPERT = 256

LANES = 128
VMEM_LIMIT_BYTES = 48 * 1024 * 1024

PROJ_TM = 256
CHUNK = 64
SUB = 16
ML_CHUNK = 128
MOE_TM = 256
NEG = -1e30


def _cparams(*sem):
    return pltpu.CompilerParams(dimension_semantics=sem, vmem_limit_bytes=VMEM_LIMIT_BYTES)


def _mm(a, b):
    return jnp.dot(a.astype(BF16), b.astype(BF16), preferred_element_type=F32)


def _mm_nt(a, b):
    return lax.dot_general(a.astype(BF16), b.astype(BF16), (((1,), (1,)), ((), ())), preferred_element_type=F32)


def _mm_tn(a, b):
    return lax.dot_general(a.astype(BF16), b.astype(BF16), (((0,), (0,)), ((), ())), preferred_element_type=F32)


def _mm_hi(a, b):
    return jnp.dot(a, b, precision=HIGHEST, preferred_element_type=F32)


def _mm_split(x, ones_bf16):
    hi = x.astype(BF16)
    lo = (x - hi.astype(F32)).astype(BF16)
    return (jnp.dot(hi, ones_bf16, preferred_element_type=F32) + jnp.dot(lo, ones_bf16, preferred_element_type=F32))


def _iota2(shape):
    return lax.broadcasted_iota(I32, shape, 0), lax.broadcasted_iota(I32, shape, 1)


def _rms_rows(x, g, eps=NORM_EPS):
    return x * lax.rsqrt(jnp.mean(x * x, axis=-1, keepdims=True) + eps) * g


def _seg_ones(width, seg):
    r, c = _iota2((width, width))
    return jnp.where((r // seg) == (c // seg), 1.0, 0.0).astype(BF16)


def _tri_inv(m, n):
    r, c = _iota2((n, n))
    same = (r // SUB) == (c // SUB)
    eye = jnp.where(r == c, 1.0, 0.0).astype(F32)
    d = jnp.where(same, m, 0.0)
    off = m - d
    x = eye - d
    p = _mm_hi(d, d)
    x = x + _mm_hi(x, p)
    p = _mm_hi(p, p)
    x = x + _mm_hi(x, p)
    p = _mm_hi(p, p)
    x = x + _mm_hi(x, p)
    e = _mm_hi(x, off)
    ime = eye - e
    y = ime + _mm_hi(ime, _mm_hi(e, e))
    return _mm_hi(y, x)


def _rms_proj_kernel(x_ref, g_ref, *refs, n_out):
    y = _rms_rows(x_ref[...], g_ref[...]).astype(BF16)
    for w_ref, o_ref in zip(refs[:n_out], refs[n_out:]):
        o_ref[...] = jnp.dot(y, w_ref[...], preferred_element_type=F32)


def _rms_proj(x, g, ws):
    n, d = x.shape
    tm = min(PROJ_TM, n)
    in_specs = [pl.BlockSpec((tm, d), lambda i: (i, 0)), pl.BlockSpec((1, d), lambda i: (0, 0))]
    in_specs += [pl.BlockSpec(w.shape, lambda i: (0, 0)) for w in ws]
    return pl.pallas_call(
        functools.partial(_rms_proj_kernel, n_out=len(ws)),
        grid=(n // tm,),
        in_specs=in_specs,
        out_specs=[pl.BlockSpec((tm, w.shape[1]), lambda i: (i, 0)) for w in ws],
        out_shape=[jax.ShapeDtypeStruct((n, w.shape[1]), F32) for w in ws],
        compiler_params=_cparams("parallel"),
        name="rms_proj",
    )(x, g.reshape(1, d), *ws)


def _out_proj_kernel(h_ref, ya_ref, yb_ref, wa_ref, wb_ref, o_ref):
    o_ref[...] = (h_ref[...] + _mm(ya_ref[...], wa_ref[...]) + _mm(yb_ref[...], wb_ref[...]))


def _out_proj(h, ya, yb, w_out):
    n, d = h.shape
    da, db = ya.shape[1], yb.shape[1]
    tm = min(PROJ_TM, n)
    wa = w_out[:da].astype(BF16)
    wb = w_out[da:].astype(BF16)
    return pl.pallas_call(
        _out_proj_kernel,
        grid=(n // tm,),
        in_specs=[pl.BlockSpec((tm, d), lambda i: (i, 0)), pl.BlockSpec((tm, da), lambda i: (i, 0)),
                  pl.BlockSpec((tm, db), lambda i: (i, 0)), pl.BlockSpec((da, d), lambda i: (0, 0)),
                  pl.BlockSpec((db, d), lambda i: (0, 0))],
        out_specs=pl.BlockSpec((tm, d), lambda i: (i, 0)),
        out_shape=jax.ShapeDtypeStruct((n, d), F32),
        compiler_params=_cparams("parallel"),
        name="out_proj",
    )(h, ya, yb, wa, wb)


def _rwkv7_kernel(p_ref, mu_ref, w0_ref, w2_ref, a0_ref, a2_ref, g2_ref, kk_ref, ka_ref, rk_ref,
                  lnw_ref, lnb_ref, o_ref, prev_ref, zt_ref):
    L = CHUNK
    c = pl.program_id(1)

    @pl.when(c == 0)
    def _():
        prev_ref[...] = jnp.zeros_like(prev_ref)
        zt_ref[...] = jnp.zeros_like(zt_ref)

    x = p_ref[0]
    row = lax.broadcasted_iota(I32, x.shape, 0)
    xs = jnp.where(row == 0, prev_ref[7:8, :], pltpu.roll(x, 1, 0))
    prev_ref[...] = x[L - 8:L, :]
    pm = x + mu_ref[...] * (xs - x)
    r_all = pm[:, 0:RW_DIM]
    k_all = pm[:, RW_DIM:2 * RW_DIM]
    v_all = pm[:, 2 * RW_DIM:3 * RW_DIM]
    wa = pm[:, 3 * RW_DIM:3 * RW_DIM + LANES]
    gl = pm[:, 3 * RW_DIM + LANES:]
    wlog = -jax.nn.softplus(-(w0_ref[...] + _mm(jnp.tanh(wa), w2_ref[...]))) - 0.5
    ld = -jnp.exp(wlog)
    a_all = jax.nn.sigmoid(a0_ref[...] + _mm(wa, a2_ref[...]))
    g_all = _mm(jax.nn.sigmoid(gl), g2_ref[...])

    tr, tc = _iota2((L, L))
    tril_incl = jnp.where(tc <= tr, 1.0, 0.0).astype(F32)
    strict = tc < tr
    incl = tc <= tr
    cs_all = _mm_hi(tril_incl, ld)
    seg = _seg_ones(LANES, RW_HEAD_DIM)
    lane = lax.broadcasted_iota(I32, (L, LANES), 1)
    hm = (lane < RW_HEAD_DIM, lane >= RW_HEAD_DIM)
    br, bc = _iota2((LANES, LANES))
    bd = (br // RW_HEAD_DIM) == (bc // RW_HEAD_DIM)

    for p in range(RW_HEADS // 2):
        sl = slice(p * LANES, (p + 1) * LANES)
        r, k, v, a, g = r_all[:, sl], k_all[:, sl], v_all[:, sl], a_all[:, sl], g_all[:, sl]
        cs, ldp = cs_all[:, sl], ld[:, sl]
        kkr = k * kk_ref[:, sl]
        kk = kkr * lax.rsqrt(_mm_split(kkr * kkr, seg) + 1e-6)
        k2 = k * (1.0 + (a - 1.0) * ka_ref[:, sl])
        b = kk * a
        cs_last = cs[L - 1:L, :]
        e_neg = jnp.exp(-cs)
        e_rem = jnp.exp(cs_last - cs)
        ahat = kk * jnp.exp(cs - ldp)
        bhat = b * e_neg
        khat = k2 * e_neg
        rhat = r * jnp.exp(cs)
        btil = b * e_rem
        ktil = k2 * e_rem
        zero = jnp.zeros_like(v)
        am = [jnp.where(hm[h], ahat, zero) for h in range(2)]
        rm = [jnp.where(hm[h], rhat, zero) for h in range(2)]
        vm = [jnp.where(hm[h], v, zero) for h in range(2)]
        lhs = jnp.concatenate(am + rm, axis=0)
        gb = _mm_nt(lhs, bhat)
        gk = _mm_nt(lhs, khat)
        tinv = [_tri_inv(jnp.where(strict, gb[h * L:(h + 1) * L], 0.0), L) for h in range(2)]
        xk = sum(_mm(jnp.where(strict, gk[h * L:(h + 1) * L], 0.0), vm[h]) for h in range(2))
        w = sum(_mm(tinv[h], am[h]) for h in range(2))
        u0 = -sum(_mm(tinv[h], jnp.where(hm[h], xk, zero)) for h in range(2))
        zt = zt_ref[p]
        u = u0 - _mm_nt(w, zt)
        y = _mm_nt(rhat, zt)
        for h in range(2):
            y = y + _mm(jnp.where(incl, gb[(2 + h) * L:(3 + h) * L], 0.0), jnp.where(hm[h], u, zero))
            y = y + _mm(jnp.where(incl, gk[(2 + h) * L:(3 + h) * L], 0.0), vm[h])
        zt_ref[p] = zt * jnp.exp(cs_last) + jnp.where(bd, _mm_tn(u, btil) + _mm_tn(v, ktil), 0.0)

        mean = _mm_split(y, seg) * (1.0 / RW_HEAD_DIM)
        yc = y - mean
        var = _mm_split(yc * yc, seg) * (1.0 / RW_HEAD_DIM)
        yn = yc * lax.rsqrt(var + RW_LN_EPS) * lnw_ref[:, sl] + lnb_ref[:, sl]
        bonus = _mm_split(r * k2 * rk_ref[:, sl], seg) * v
        o_ref[0, :, sl] = (yn + bonus) * g


def _rwkv7(p, mu, w0, w2, a0, a2, g2, k_k, k_a, r_k, ln_w, ln_b):
    bsz, t, _ = p.shape
    row = lambda z: z.reshape(1, -1).astype(F32)
    w2p = jnp.concatenate([w2, jnp.zeros_like(w2)], axis=0).astype(BF16)
    a2p = jnp.concatenate([jnp.zeros_like(a2), a2], axis=0).astype(BF16)
    params = [row(mu), row(w0), w2p, row(a0), a2p, g2.astype(BF16), row(k_k), row(k_a), row(r_k), row(ln_w), row(ln_b)]
    full = lambda z: pl.BlockSpec(z.shape, lambda b, c: (0, 0))
    return pl.pallas_call(
        _rwkv7_kernel,
        grid=(bsz, t // CHUNK),
        in_specs=[pl.BlockSpec((1, CHUNK, RW_IN), lambda b, c: (b, c, 0))] + [full(z) for z in params],
        out_specs=pl.BlockSpec((1, CHUNK, RW_DIM), lambda b, c: (b, c, 0)),
        out_shape=jax.ShapeDtypeStruct((bsz, t, RW_DIM), F32),
        scratch_shapes=[pltpu.VMEM((8, RW_IN), F32), pltpu.VMEM((RW_HEADS // 2, LANES, LANES), F32)],
        compiler_params=_cparams("arbitrary", "arbitrary"),
        name="rwkv7_mix",
    )(p, *params)


def _hgrn2_kernel(p_ref, lb_ref, ng_ref, o_ref, st_ref):
    L = CHUNK
    c = pl.program_id(1)

    @pl.when(c == 0)
    def _():
        st_ref[...] = jnp.zeros_like(st_ref)

    x = p_ref[0]
    lb = lb_ref[...]
    q_all = jax.nn.silu(x[:, 0:HG_KDIM])
    fg = lb + (1.0 - lb) * jax.nn.sigmoid(x[:, HG_KDIM:2 * HG_KDIM])
    k_all = 1.0 - fg
    logf = jnp.log(fg)
    v_all = x[:, 2 * HG_KDIM:3 * HG_KDIM]
    gate = x[:, 3 * HG_KDIM:]
    tr, tc = _iota2((L, L))
    blk_tril = jnp.where((tc <= tr) & ((tr // SUB) == (tc // SUB)), 1.0, 0.0).astype(F32)
    bc_all = _mm_hi(blk_tril, logf)
    seg = _seg_ones(LANES, HG_DIM)
    br, bcc = _iota2((LANES, LANES))
    bd = (br // HG_DIM) == (bcc // HG_DIM)
    t3 = lax.broadcasted_iota(I32, (SUB, SUB, 1), 0)
    s3 = lax.broadcasted_iota(I32, (SUB, SUB, 1), 1)
    causal3 = s3 <= t3

    for p in range(HG_HEADS // 2):
        sl = slice(p * LANES, (p + 1) * LANES)
        st = st_ref[p]
        outs = []
        for j in range(L // SUB):
            rs = slice(j * SUB, (j + 1) * SUB)
            q, k, v, bc = q_all[rs, sl], k_all[rs, sl], v_all[rs, sl], bc_all[rs, sl]
            bend = bc[SUB - 1:SUB, :]
            o = _mm_nt(q * jnp.exp(bc), st)
            diff = jnp.where(causal3, bc[:, None, :] - bc[None, :, :], NEG)
            pr = (jnp.exp(diff) * q[:, None, :] * k[None, :, :]).reshape(SUB * SUB, LANES)
            score = _mm_split(pr, seg).reshape(SUB, SUB, LANES)
            o = o + jnp.sum(score * v[None, :, :], axis=1)
            st = st * jnp.exp(bend) + jnp.where(bd, _mm_tn(v, k * jnp.exp(bend - bc)), 0.0)
            outs.append(o)
        st_ref[p] = st
        o = jnp.concatenate(outs, axis=0)
        ms = _mm_split(o * o, seg) * (1.0 / HG_DIM)
        o_ref[0, :, sl] = o * lax.rsqrt(ms + NORM_EPS) * ng_ref[:, sl] * jax.nn.silu(gate[:, sl])


def _hgrn2(p, lb, norm_g):
    bsz, t, _ = p.shape
    lb = lb.reshape(1, HG_KDIM).astype(F32)
    ng = jnp.tile(norm_g.astype(F32), HG_HEADS).reshape(1, HG_KDIM)
    return pl.pallas_call(
        _hgrn2_kernel,
        grid=(bsz, t // CHUNK),
        in_specs=[pl.BlockSpec((1, CHUNK, HG_IN), lambda b, c: (b, c, 0)),
                  pl.BlockSpec((1, HG_KDIM), lambda b, c: (0, 0)), pl.BlockSpec((1, HG_KDIM), lambda b, c: (0, 0))],
        out_specs=pl.BlockSpec((1, CHUNK, HG_KDIM), lambda b, c: (b, c, 0)),
        out_shape=jax.ShapeDtypeStruct((bsz, t, HG_KDIM), F32),
        scratch_shapes=[pltpu.VMEM((HG_HEADS // 2, LANES, LANES), F32)],
        compiler_params=_cparams("arbitrary", "arbitrary"),
        name="hgrn2_mix",
    )(p, lb, ng)


def _gdn_kernel(p_ref, pg_ref, cw_ref, gb_ref, nal_ref, nalc_ref, ng_ref, o_ref, prev_ref, s_ref):
    L = CHUNK
    c = pl.program_id(1)

    @pl.when(c == 0)
    def _():
        prev_ref[...] = jnp.zeros_like(prev_ref)
        s_ref[...] = jnp.zeros_like(s_ref)

    x = p_ref[0]
    xq = x[:, 0:3 * GD_DIM]
    xprev = prev_ref[...]
    row = lax.broadcasted_iota(I32, xq.shape, 0)
    conv = xq * cw_ref[CONV_K - 1:CONV_K, :]
    for j in range(1, CONV_K):
        shifted = jnp.where(row >= j, pltpu.roll(xq, j, 0), pltpu.roll(xprev, j, 0))
        conv = conv + shifted * cw_ref[CONV_K - 1 - j:CONV_K - j, :]
    prev_ref[...] = xq
    qkv = jax.nn.silu(conv)
    z = x[:, 3 * GD_DIM:]

    gt = pg_ref[0] + gb_ref[...]
    beta_all = jax.nn.sigmoid(gt)
    g_all = nal_ref[...] * jax.nn.softplus(gt)
    gt_t = gt.T
    g_t = nalc_ref[...] * jax.nn.softplus(gt_t)
    tr, tc = _iota2((L, L))
    tril_incl = jnp.where(tc <= tr, 1.0, 0.0).astype(F32)
    triu_incl = jnp.where(tr <= tc, 1.0, 0.0).astype(F32)
    gam_all = _mm_hi(tril_incl, g_all)
    gam_t = _mm_hi(g_t, triu_incl)
    strict = tc < tr
    incl = tc <= tr

    for h in range(GD_HEADS):
        sl = slice(h * LANES, (h + 1) * LANES)
        q = qkv[:, sl]
        k = qkv[:, GD_DIM + h * LANES:GD_DIM + (h + 1) * LANES]
        v = qkv[:, 2 * GD_DIM + h * LANES:2 * GD_DIM + (h + 1) * LANES]
        q = q * lax.rsqrt(jnp.sum(q * q, axis=-1, keepdims=True) + 1e-6) * (GD_HEAD_DIM ** -0.5)
        k = k * lax.rsqrt(jnp.sum(k * k, axis=-1, keepdims=True) + 1e-6)
        beta = beta_all[:, h:h + 1]
        gam = gam_all[:, GD_HEADS + h:GD_HEADS + h + 1]
        gam_row = gam_t[GD_HEADS + h:GD_HEADS + h + 1, :]
        gam_last = gam[L - 1:L, :]
        decay = jnp.exp(jnp.where(incl, gam - gam_row, NEG))
        kb = k * beta
        gram = _mm_nt(jnp.concatenate([kb, q], axis=0), k)
        tinv = _tri_inv(jnp.where(strict, gram[0:L] * decay, 0.0), L)
        u = _mm(tinv, v * beta)
        w = _mm(tinv, kb * jnp.exp(gam))
        attn = gram[L:2 * L] * decay
        s = s_ref[h]
        v_new = u - _mm(w, s)
        o = _mm(q * jnp.exp(gam), s) + _mm(attn, v_new)
        s_ref[h] = s * jnp.exp(gam_last) + _mm_tn(k * jnp.exp(gam_last - gam), v_new)
        o = _rms_rows(o, ng_ref[...])
        o_ref[0, :, sl] = o * jax.nn.silu(z[:, sl])


def _gate_row(lo, vals):
    return jnp.zeros((1, LANES), F32).at[0, lo:lo + vals.shape[0]].set(vals.astype(F32))


def _gdn(p, pg, conv_w, a_log, dt_bias, norm_g):
    bsz, t, _ = p.shape
    gbias = _gate_row(GD_HEADS, dt_bias)
    nal = _gate_row(GD_HEADS, -jnp.exp(a_log))
    nal_col = nal.reshape(LANES, 1)
    ng = norm_g.reshape(1, GD_HEAD_DIM).astype(F32)
    full = lambda z: pl.BlockSpec(z.shape, lambda b, c: (0, 0))
    return pl.pallas_call(
        _gdn_kernel,
        grid=(bsz, t // CHUNK),
        in_specs=[pl.BlockSpec((1, CHUNK, GD_MAIN), lambda b, c: (b, c, 0)),
                  pl.BlockSpec((1, CHUNK, LANES), lambda b, c: (b, c, 0)),
                  full(conv_w), full(gbias), full(nal), full(nal_col), full(ng)],
        out_specs=pl.BlockSpec((1, CHUNK, GD_DIM), lambda b, c: (b, c, 0)),
        out_shape=jax.ShapeDtypeStruct((bsz, t, GD_DIM), F32),
        scratch_shapes=[pltpu.VMEM((CHUNK, 3 * GD_DIM), F32), pltpu.VMEM((GD_HEADS, LANES, LANES), F32)],
        compiler_params=_cparams("arbitrary", "arbitrary"),
        name="gdn_mix",
    )(p, pg, conv_w.astype(F32), gbias, nal, nal_col, ng)


def _mlstm_kernel(p_ref, pg_ref, gb_ref, ng_ref, o_ref, c_ref, n_ref, m_ref):
    L = ML_CHUNK
    ci = pl.program_id(1)

    @pl.when(ci == 0)
    def _():
        c_ref[...] = jnp.zeros_like(c_ref)
        n_ref[...] = jnp.zeros_like(n_ref)
        m_ref[...] = jnp.zeros_like(m_ref)

    x = p_ref[0]
    gt = pg_ref[0] + gb_ref[...]
    logf = jax.nn.log_sigmoid(gt)
    gt_t = gt.T
    logf_t = jax.nn.log_sigmoid(gt_t)
    tr, tc = _iota2((L, L))
    tril_incl = jnp.where(tc <= tr, 1.0, 0.0).astype(F32)
    triu_incl = jnp.where(tr <= tc, 1.0, 0.0).astype(F32)
    bc_all = _mm_hi(tril_incl, logf)
    bc_t = _mm_hi(logf_t, triu_incl)
    incl = tc <= tr
    i_lo, f_lo = 2 * GD_HEADS, 2 * GD_HEADS + ML_HEADS

    for h in range(ML_HEADS):
        sl = slice(h * LANES, (h + 1) * LANES)
        q = x[:, sl]
        k = x[:, ML_DIM + h * LANES:ML_DIM + (h + 1) * LANES] * (ML_HEAD_DIM ** -0.5)
        v = x[:, 2 * ML_DIM + h * LANES:2 * ML_DIM + (h + 1) * LANES]
        og = jax.nn.sigmoid(x[:, 3 * ML_DIM + h * LANES:3 * ML_DIM + (h + 1) * LANES])
        bc = bc_all[:, f_lo + h:f_lo + h + 1]
        bc_row = bc_t[f_lo + h:f_lo + h + 1, :]
        ig = gt[:, i_lo + h:i_lo + h + 1]
        ig_row = gt_t[i_lo + h:i_lo + h + 1, :]
        m_prev = m_ref[h][0:1, 0:1]
        b_last = bc[L - 1:L, :]

        d_log = jnp.where(incl, bc - bc_row + ig_row, NEG)
        inter_log = bc + m_prev
        m_t = jnp.maximum(inter_log, jnp.max(d_log, axis=-1, keepdims=True))
        d_w = jnp.exp(d_log - m_t)
        inter_w = jnp.exp(inter_log - m_t)
        sc = _mm_nt(q, k) * d_w
        cmat = c_ref[h]
        nvec = n_ref[h][0:1, :]
        num = inter_w * _mm(q, cmat) + _mm(sc, v)
        den = inter_w * jnp.sum(q * nvec, axis=-1, keepdims=True) + jnp.sum(sc, axis=-1, keepdims=True)
        hh = num / jnp.maximum(jnp.abs(den), jnp.exp(-m_t))

        upd_log = b_last - bc + ig
        m_new = jnp.maximum(b_last + m_prev, jnp.max(upd_log, axis=0, keepdims=True))
        w_s = jnp.exp(upd_log - m_new)
        dec = jnp.exp(b_last + m_prev - m_new)
        wk = w_s * k
        c_ref[h] = dec * cmat + _mm_tn(wk, v)
        n_ref[h] = jnp.broadcast_to(dec * nvec + jnp.sum(wk, axis=0, keepdims=True), (8, LANES))
        m_ref[h] = jnp.broadcast_to(m_new, (8, LANES))

        o_ref[0, :, sl] = og * _rms_rows(hh, ng_ref[:, sl])


def _mlstm(p, pg, i_bias, f_bias, norm_g):
    bsz, t, _ = p.shape
    gbias = _gate_row(2 * GD_HEADS, jnp.concatenate([i_bias, f_bias]))
    ng = norm_g.reshape(1, ML_DIM).astype(F32)
    lc = min(ML_CHUNK, t)
    full = lambda z: pl.BlockSpec(z.shape, lambda b, c: (0, 0))
    return pl.pallas_call(
        _mlstm_kernel,
        grid=(bsz, t // lc),
        in_specs=[pl.BlockSpec((1, lc, ML_MAIN), lambda b, c: (b, c, 0)),
                  pl.BlockSpec((1, lc, LANES), lambda b, c: (b, c, 0)), full(gbias), full(ng)],
        out_specs=pl.BlockSpec((1, lc, ML_DIM), lambda b, c: (b, c, 0)),
        out_shape=jax.ShapeDtypeStruct((bsz, t, ML_DIM), F32),
        scratch_shapes=[pltpu.VMEM((ML_HEADS, LANES, LANES), F32), pltpu.VMEM((ML_HEADS, 8, LANES), F32),
                        pltpu.VMEM((ML_HEADS, 8, LANES), F32)],
        compiler_params=_cparams("arbitrary", "arbitrary"),
        name="mlstm_mix",
    )(p, pg, gbias, ng)


def _router_kernel(h_ref, g_ref, w_ref, b_ref, ri_ref, rg_ref, cnt_ref, off_ref):
    tm = h_ref.shape[0]
    i = pl.program_id(0)

    @pl.when(i == 0)
    def _():
        off_ref[...] = jnp.zeros_like(off_ref)

    xn = _rms_rows(h_ref[...], g_ref[...])
    logits = _mm_hi(xn, w_ref[...]) + b_ref[...]
    lane = lax.broadcasted_iota(I32, logits.shape, 1)
    big = jnp.int32(1 << 20)
    is_grp = (lane >= N_EXPERTS) & (lane < N_EXPERTS + N_GROUPS)
    lg = jnp.where(is_grp, logits, NEG)
    gmax = jnp.max(lg, axis=-1, keepdims=True)
    p_top = 1.0 / jnp.sum(jnp.exp(lg - gmax), axis=-1, keepdims=True)
    g_idx = jnp.min(jnp.where(lg == gmax, lane, big), axis=-1, keepdims=True) - N_EXPERTS
    valid = (lane < N_EXPERTS) & ((lane // EXPERTS_PER_GROUP) == g_idx)
    v1 = jnp.where(valid, logits, NEG)
    m1 = jnp.max(v1, axis=-1, keepdims=True)
    i1 = jnp.min(jnp.where(v1 == m1, lane, big), axis=-1, keepdims=True)
    v2 = jnp.where(lane == i1, NEG, v1)
    m2 = jnp.max(v2, axis=-1, keepdims=True)
    i2 = jnp.min(jnp.where(v2 == m2, lane, big), axis=-1, keepdims=True)
    e21 = jnp.exp(m2 - m1)
    gate1 = p_top / (1.0 + e21)
    gate2 = p_top * e21 / (1.0 + e21)

    sel1 = lane == i1
    sel2 = lane == i2
    onehot = jnp.where(sel1 | sel2, 1.0, 0.0)
    tr, tc = _iota2((tm, tm))
    before = jnp.where(tc < tr, 1.0, 0.0).astype(BF16)
    prefix = jnp.dot(before, onehot.astype(BF16), preferred_element_type=F32) + off_ref[0:1, :]
    rank1 = jnp.sum(jnp.where(sel1, prefix, 0.0), axis=-1, keepdims=True).astype(I32)
    rank2 = jnp.sum(jnp.where(sel2, prefix, 0.0), axis=-1, keepdims=True).astype(I32)
    total = off_ref[0:1, :] + jnp.sum(onehot, axis=0, keepdims=True)
    off_ref[...] = jnp.broadcast_to(total, off_ref.shape)
    cnt_ref[...] = jnp.broadcast_to(total, cnt_ref.shape).astype(I32)

    ri_ref[...] = jnp.where(lane == 0, i1, jnp.where(lane == 1, i2, jnp.where(lane == 2, rank1, jnp.where(lane == 3, rank2, 0))))
    rg_ref[...] = jnp.where(lane == 0, gate1, jnp.where(lane == 1, gate2, 0.0))


def _router(h, g, w_cat, b_cat):
    n, d = h.shape
    tm = min(PROJ_TM, n)
    return pl.pallas_call(
        _router_kernel,
        grid=(n // tm,),
        in_specs=[pl.BlockSpec((tm, d), lambda i: (i, 0)), pl.BlockSpec((1, d), lambda i: (0, 0)),
                  pl.BlockSpec((d, LANES), lambda i: (0, 0)), pl.BlockSpec((1, LANES), lambda i: (0, 0))],
        out_specs=[pl.BlockSpec((tm, LANES), lambda i: (i, 0)), pl.BlockSpec((tm, LANES), lambda i: (i, 0)),
                   pl.BlockSpec((8, LANES), lambda i: (0, 0))],
        out_shape=[jax.ShapeDtypeStruct((n, LANES), I32), jax.ShapeDtypeStruct((n, LANES), F32),
                   jax.ShapeDtypeStruct((8, LANES), I32)],
        scratch_shapes=[pltpu.VMEM((8, LANES), F32)],
        compiler_params=_cparams("arbitrary"),
        name="moe_router",
    )(h, g.reshape(1, d), w_cat, b_cat)


def _expert_kernel(te_ref, src_ref, h_hbm, g_ref, wg_ref, wu_ref, wd_ref, o_ref, xbuf, sem):
    del te_ref
    tm = xbuf.shape[0]
    base = pl.program_id(0) * tm

    def row_copy(j):
        return pltpu.make_async_copy(h_hbm.at[pl.ds(src_ref[base + j], 1), :], xbuf.at[pl.ds(j, 1), :], sem.at[0])

    def start(j, carry):
        row_copy(j).start()
        return carry

    def wait(j, carry):
        row_copy(j).wait()
        return carry

    lax.fori_loop(0, tm, start, 0)
    lax.fori_loop(0, tm, wait, 0)
    xn = _rms_rows(xbuf[...], g_ref[...]).astype(BF16)
    gate = jnp.dot(xn, wg_ref[0], preferred_element_type=F32)
    up = jnp.dot(xn, wu_ref[0], preferred_element_type=F32)
    hid = (jax.nn.silu(gate) * up).astype(BF16)
    o_ref[...] = jnp.dot(hid, wd_ref[0], preferred_element_type=F32)


def _experts(h, g, tile_expert, src, w_gate, w_up, w_down):
    n, d = h.shape
    n_tiles = tile_expert.shape[0]
    tm = src.shape[0] // n_tiles
    de = w_gate.shape[-1]
    grid_spec = pltpu.PrefetchScalarGridSpec(
        num_scalar_prefetch=2,
        grid=(n_tiles,),
        in_specs=[pl.BlockSpec(memory_space=pl.ANY),
                  pl.BlockSpec((1, d), lambda i, te, s: (0, 0)),
                  pl.BlockSpec((1, d, de), lambda i, te, s: (te[i], 0, 0)),
                  pl.BlockSpec((1, d, de), lambda i, te, s: (te[i], 0, 0)),
                  pl.BlockSpec((1, de, d), lambda i, te, s: (te[i], 0, 0))],
        out_specs=pl.BlockSpec((tm, d), lambda i, te, s: (i, 0)),
        scratch_shapes=[pltpu.VMEM((tm, d), F32), pltpu.SemaphoreType.DMA((1,))],
    )
    return pl.pallas_call(
        _expert_kernel,
        grid_spec=grid_spec,
        out_shape=jax.ShapeDtypeStruct((n_tiles * tm, d), F32),
        compiler_params=_cparams("arbitrary"),
        name="moe_experts",
    )(tile_expert, src, h, g.reshape(1, d), w_gate, w_up, w_down)


def _combine_kernel(d1_ref, d2_ref, ys_hbm, h_ref, rg_ref, gf_ref, o_ref, y1, y2, sem, *, final_norm):
    tm = h_ref.shape[0]
    base = pl.program_id(0) * tm

    def copies(j):
        return (pltpu.make_async_copy(ys_hbm.at[pl.ds(d1_ref[base + j], 1), :], y1.at[pl.ds(j, 1), :], sem.at[0]),
                pltpu.make_async_copy(ys_hbm.at[pl.ds(d2_ref[base + j], 1), :], y2.at[pl.ds(j, 1), :], sem.at[1]))

    def start(j, carry):
        for cp in copies(j):
            cp.start()
        return carry

    def wait(j, carry):
        for cp in copies(j):
            cp.wait()
        return carry

    lax.fori_loop(0, tm, start, 0)
    lax.fori_loop(0, tm, wait, 0)
    rg = rg_ref[...]
    out = h_ref[...] + rg[:, 0:1] * y1[...] + rg[:, 1:2] * y2[...]
    if final_norm:
        out = _rms_rows(out, gf_ref[...])
    o_ref[...] = out


def _combine(h, ys, d1, d2, rg, g_final, final_norm):
    n, d = h.shape
    tm = min(PROJ_TM, n)
    grid_spec = pltpu.PrefetchScalarGridSpec(
        num_scalar_prefetch=2,
        grid=(n // tm,),
        in_specs=[pl.BlockSpec(memory_space=pl.ANY),
                  pl.BlockSpec((tm, d), lambda i, a, b: (i, 0)),
                  pl.BlockSpec((tm, LANES), lambda i, a, b: (i, 0)),
                  pl.BlockSpec((1, d), lambda i, a, b: (0, 0))],
        out_specs=pl.BlockSpec((tm, d), lambda i, a, b: (i, 0)),
        scratch_shapes=[pltpu.VMEM((tm, d), F32), pltpu.VMEM((tm, d), F32), pltpu.SemaphoreType.DMA((2,))],
    )
    return pl.pallas_call(
        functools.partial(_combine_kernel, final_norm=final_norm),
        grid_spec=grid_spec,
        out_shape=jax.ShapeDtypeStruct((n, d), F32),
        compiler_params=_cparams("arbitrary"),
        name="moe_combine",
    )(d1, d2, ys, h, rg, g_final.reshape(1, d))


def _hmoe_residual(h, norm_g, w_group, b_group, w_router, b_router, w_gate, w_up, w_down, g_final, final_norm):
    n, d = h.shape
    pad = LANES - N_EXPERTS - N_GROUPS
    w_cat = jnp.concatenate([w_router, w_group, jnp.zeros((d, pad), F32)], axis=1)
    b_cat = jnp.concatenate([b_router, b_group, jnp.zeros((pad,), F32)]).reshape(1, LANES)
    ri, rg, cnt = _router(h, norm_g, w_cat, b_cat)

    counts = cnt[0, :N_EXPERTS]
    n_tiles = (2 * n) // MOE_TM + N_EXPERTS
    tiles_per = (counts + MOE_TM - 1) // MOE_TM
    tile_end = jnp.cumsum(tiles_per)
    seg_start = (tile_end - tiles_per) * MOE_TM
    d1 = seg_start[ri[:, 0]] + ri[:, 2]
    d2 = seg_start[ri[:, 1]] + ri[:, 3]
    tile_ids = jnp.arange(n_tiles, dtype=I32)
    tile_expert = jnp.minimum(jnp.sum((tile_end[None, :] <= tile_ids[:, None]).astype(I32), axis=1), N_EXPERTS - 1)
    tok = jnp.arange(n, dtype=I32)
    src = jnp.zeros((n_tiles * MOE_TM,), I32).at[d1].set(tok).at[d2].set(tok)

    ys = _experts(h, norm_g, tile_expert, src, w_gate.astype(BF16), w_up.astype(BF16), w_down.astype(BF16))
    return _combine(h, ys, d1, d2, rg, g_final, final_norm)


def kernel(x, norm_mix, norm_ffn, norm_final, ev_w_in, ev_mu, rw_w0, rw_w2, rw_a0, rw_a2, rw_g2, rw_k_k, rw_k_a, rw_r_k, rw_ln_w, rw_ln_b, hg_lb_logits, hg_norm, ev_w_out, od_w_in, gd_conv, gd_a_log, gd_dt_bias, gd_norm, ml_i_bias, ml_f_bias, ml_norm, od_w_out, moe_w_group, moe_b_group, moe_w_router, moe_b_router, moe_w_gate, moe_w_up, moe_w_down):
    bsz, t, d = x.shape
    n = bsz * t
    depth = norm_mix.shape[0]
    lb_table = jnp.cumsum(jax.nn.softmax(hg_lb_logits.astype(F32), axis=0), axis=0)
    h = x.reshape(n, d)
    for layer in range(depth):
        j = layer // 2
        if layer % 2 == 0:
            w_in = ev_w_in[j].astype(BF16)
            p_rw, p_hg = _rms_proj(h, norm_mix[layer], [w_in[:, :RW_IN], w_in[:, RW_IN:]])
            ya = _rwkv7(p_rw.reshape(bsz, t, RW_IN), ev_mu[j], rw_w0[j], rw_w2[j], rw_a0[j], rw_a2[j], rw_g2[j],
                        rw_k_k[j], rw_k_a[j], rw_r_k[j], rw_ln_w[j], rw_ln_b[j])
            yb = _hgrn2(p_hg.reshape(bsz, t, HG_IN), lb_table[j], hg_norm[j])
            h = _out_proj(h, ya.reshape(n, RW_DIM), yb.reshape(n, HG_KDIM), ev_w_out[j])
        else:
            w_in = od_w_in[j]
            w_gates = jnp.concatenate([w_in[:, GD_MAIN:GD_IN], w_in[:, GD_IN + ML_MAIN:],
                                       jnp.zeros((d, LANES - 2 * GD_HEADS - 2 * ML_HEADS), F32)], axis=1)
            p_gd, p_ml, p_gt = _rms_proj(h, norm_mix[layer], [w_in[:, :GD_MAIN].astype(BF16),
                                                              w_in[:, GD_IN:GD_IN + ML_MAIN].astype(BF16),
                                                              w_gates.astype(BF16)])
            p_gt = p_gt.reshape(bsz, t, LANES)
            yc = _gdn(p_gd.reshape(bsz, t, GD_MAIN), p_gt, gd_conv[j], gd_a_log[j], gd_dt_bias[j], gd_norm[j])
            yd = _mlstm(p_ml.reshape(bsz, t, ML_MAIN), p_gt, ml_i_bias[j], ml_f_bias[j], ml_norm[j])
            h = _out_proj(h, yc.reshape(n, GD_DIM), yd.reshape(n, ML_DIM), od_w_out[j])
        h = _hmoe_residual(h, norm_ffn[layer], moe_w_group[layer], moe_b_group[layer], moe_w_router[layer],
                           moe_b_router[layer], moe_w_gate[layer], moe_w_up[layer], moe_w_down[layer],
                           norm_final, final_norm=(layer == depth - 1))
    return h.reshape(bsz, t, d)
```

```python
import functools
import math

import jax
import jax.numpy as jnp
from jax import lax
from jax.experimental import pallas as pl
from jax.experimental.pallas import tpu as pltpu

F32 = jnp.float32
BF16 = jnp.bfloat16
I32 = jnp.int32
HIGHEST = lax.Precision.HIGHEST

D_MODEL = 1024
NORM_EPS = 1e-6
RW_HEADS, RW_HEAD_DIM = 8, 64
RW_DIM = RW_HEADS * RW_HEAD_DIM
R_DECAY, R_AAA, R_GATE = 64, 64, 128
RW_IN = 3 * RW_DIM + R_DECAY + R_AAA + R_GATE
RW_LN_EPS = 64e-5
HG_HEADS, HG_DIM = 8, 64
HG_KDIM = HG_HEADS * HG_DIM
HG_IN = 4 * HG_KDIM
GD_HEADS, GD_HEAD_DIM = 4, 128
GD_DIM = GD_HEADS * GD_HEAD_DIM
CONV_K = 4
GD_MAIN = 4 * GD_DIM
GD_IN = GD_MAIN + 2 * GD_HEADS
ML_HEADS, ML_HEAD_DIM = 4, 128
ML_DIM = ML_HEADS * ML_HEAD_DIM
ML_MAIN = 4 * ML_DIM
N_GROUPS, EXPERTS_PER_GROUP = 4, 8
N_EXPERTS = N_GROUPS * EXPERTS_PER_GROUP
D_EXPERT = 256

LANES = 128
VMEM_LIMIT_BYTES = 48 * 1024 * 1024

PROJ_TM = 256
CHUNK = 64
SUB = 16
ML_CHUNK = 128
MOE_TM = 256
GATHER_UNROLL = 8
NEG = -1e30


def _cparams(*sem):
    return pltpu.CompilerParams(dimension_semantics=sem, vmem_limit_bytes=VMEM_LIMIT_BYTES)


def _mm(a, b):
    return jnp.dot(a.astype(BF16), b.astype(BF16), preferred_element_type=F32)


def _mm_nt(a, b):
    return lax.dot_general(a.astype(BF16), b.astype(BF16), (((1,), (1,)), ((), ())), preferred_element_type=F32)


def _mm_tn(a, b):
    return lax.dot_general(a.astype(BF16), b.astype(BF16), (((0,), (0,)), ((), ())), preferred_element_type=F32)


def _mm_hi(a, b):
    return jnp.dot(a, b, precision=HIGHEST, preferred_element_type=F32)


def _mm_split(x, ones_bf16):
    hi = x.astype(BF16)
    lo = (x - hi.astype(F32)).astype(BF16)
    return (jnp.dot(hi, ones_bf16, preferred_element_type=F32) + jnp.dot(lo, ones_bf16, preferred_element_type=F32))


def _iota2(shape):
    return lax.broadcasted_iota(I32, shape, 0), lax.broadcasted_iota(I32, shape, 1)


def _rms_rows(x, g, eps=NORM_EPS):
    return x * lax.rsqrt(jnp.mean(x * x, axis=-1, keepdims=True) + eps) * g


def _seg_ones(width, seg):
    r, c = _iota2((width, width))
    return jnp.where((r // seg) == (c // seg), 1.0, 0.0).astype(BF16)


def _split3(x):
    x1 = x.astype(BF16)
    r1 = x - x1.astype(F32)
    x2 = r1.astype(BF16)
    return x1, x2, (r1 - x2.astype(F32)).astype(BF16)


def _cumsum_rows(tri_bf16, x):
    return sum(jnp.dot(tri_bf16, t, preferred_element_type=F32) for t in _split3(x))


def _cumsum_cols(x, tri_bf16):
    return sum(jnp.dot(t, tri_bf16, preferred_element_type=F32) for t in _split3(x))


def _mm3(a, b):
    ah = a.astype(BF16)
    al = (a - ah.astype(F32)).astype(BF16)
    bh = b.astype(BF16)
    bl = (b - bh.astype(F32)).astype(BF16)
    dot = lambda x, y: jnp.dot(x, y, preferred_element_type=F32)
    return dot(ah, bh) + dot(ah, bl) + dot(al, bh)


def _tri_inv_multi(ms, n, chain):
    assert chain // SUB <= 4
    r, c = _iota2((n, n))
    same = (r // SUB) == (c // SUB)
    eye = jnp.where(r == c, 1.0, 0.0).astype(F32)
    ds = [jnp.where(same, m, 0.0) for m in ms]
    offs = [m - d for m, d in zip(ms, ds)]
    xs = [eye - d for d in ds]
    ps = ds
    for _ in range(3):
        ps = [_mm3(p, p) for p in ps]
        xs = [x + _mm3(x, p) for x, p in zip(xs, ps)]
    es = [_mm3(x, o) for x, o in zip(xs, offs)]
    imes = [eye - e for e in es]
    e2s = [_mm3(e, e) for e in es]
    ys = [i + _mm3(i, e2) for i, e2 in zip(imes, e2s)]
    return [_mm3(y, x) for y, x in zip(ys, xs)]


def _rms_proj_kernel(x_ref, g_ref, *refs, n_out):
    y = _rms_rows(x_ref[...], g_ref[...]).astype(BF16)
    for w_ref, o_ref in zip(refs[:n_out], refs[n_out:]):
        o_ref[...] = jnp.dot(y, w_ref[...], preferred_element_type=F32)


def _rms_proj(x, g, ws):
    n, d = x.shape
    tm = min(PROJ_TM, n)
    in_specs = [pl.BlockSpec((tm, d), lambda i: (i, 0)), pl.BlockSpec((1, d), lambda i: (0, 0))]
    in_specs += [pl.BlockSpec(w.shape, lambda i: (0, 0)) for w in ws]
    return pl.pallas_call(
        functools.partial(_rms_proj_kernel, n_out=len(ws)),
        grid=(n // tm,),
        in_specs=in_specs,
        out_specs=[pl.BlockSpec((tm, w.shape[1]), lambda i: (i, 0)) for w in ws],
        out_shape=[jax.ShapeDtypeStruct((n, w.shape[1]), F32) for w in ws],
        compiler_params=_cparams("parallel"),
        name="rms_proj",
    )(x, g.reshape(1, d), *ws)


def _out_proj_kernel(h_ref, ya_ref, yb_ref, wa_ref, wb_ref, o_ref):
    o_ref[...] = (h_ref[...] + _mm(ya_ref[...], wa_ref[...]) + _mm(yb_ref[...], wb_ref[...]))


def _out_proj(h, ya, yb, w_out):
    n, d = h.shape
    da, db = ya.shape[1], yb.shape[1]
    tm = min(PROJ_TM, n)
    wa = w_out[:da].astype(BF16)
    wb = w_out[da:].astype(BF16)
    return pl.pallas_call(
        _out_proj_kernel,
        grid=(n // tm,),
        in_specs=[pl.BlockSpec((tm, d), lambda i: (i, 0)), pl.BlockSpec((tm, da), lambda i: (i, 0)),
                  pl.BlockSpec((tm, db), lambda i: (i, 0)), pl.BlockSpec((da, d), lambda i: (0, 0)),
                  pl.BlockSpec((db, d), lambda i: (0, 0))],
        out_specs=pl.BlockSpec((tm, d), lambda i: (i, 0)),
        out_shape=jax.ShapeDtypeStruct((n, d), F32),
        compiler_params=_cparams("parallel"),
        name="out_proj",
    )(h, ya, yb, wa, wb)


def _rwkv7_kernel(p_ref, mu_ref, w0_ref, w2_ref, a0_ref, a2_ref, g2_ref, kk_ref, ka_ref, rk_ref,
                  lnw_ref, lnb_ref, o_ref, prev_ref, zt_ref):
    L = CHUNK
    npair = RW_HEADS // 2
    c = pl.program_id(1)

    @pl.when(c == 0)
    def _():
        prev_ref[...] = jnp.zeros_like(prev_ref)
        zt_ref[...] = jnp.zeros_like(zt_ref)

    x = p_ref[0]
    row = lax.broadcasted_iota(I32, x.shape, 0)
    xs = jnp.where(row == 0, prev_ref[7:8, :], pltpu.roll(x, 1, 0))
    prev_ref[...] = x[L - 8:L, :]
    pm = x + mu_ref[...] * (xs - x)
    r_all = pm[:, 0:RW_DIM]
    k_all = pm[:, RW_DIM:2 * RW_DIM]
    v_all = pm[:, 2 * RW_DIM:3 * RW_DIM]
    wa = pm[:, 3 * RW_DIM:3 * RW_DIM + LANES]
    gl = pm[:, 3 * RW_DIM + LANES:]
    wlog = -jax.nn.softplus(-(w0_ref[...] + _mm(jnp.tanh(wa), w2_ref[...]))) - 0.5
    ld = -jnp.exp(wlog)
    a_all = jax.nn.sigmoid(a0_ref[...] + _mm(wa, a2_ref[...]))
    g_all = _mm(jax.nn.sigmoid(gl), g2_ref[...])

    tr, tc = _iota2((L, L))
    cs_all = _cumsum_rows(jnp.where(tc <= tr, 1.0, 0.0).astype(BF16), ld)
    seg = _seg_ones(LANES, RW_HEAD_DIM)
    lane = lax.broadcasted_iota(I32, (L, LANES), 1)
    hm = (lane < RW_HEAD_DIM, lane >= RW_HEAD_DIM)
    br, bc = _iota2((2 * L, 2 * L))
    bd = (br // L) == (bc // L)
    bd_strict = bd & (bc < br)
    bd_incl = bd & (bc <= br)
    fold = lambda z: z[0:L] + z[L:2 * L]
    both = lambda z: jnp.concatenate([jnp.where(hm[0], z, 0.0), jnp.where(hm[1], z, 0.0)], axis=0)

    pairs = []
    for p in range(npair):
        sl = slice(p * LANES, (p + 1) * LANES)
        r, k, v, a = r_all[:, sl], k_all[:, sl], v_all[:, sl], a_all[:, sl]
        cs, ldp = cs_all[:, sl], ld[:, sl]
        kkr = k * kk_ref[:, sl]
        kk = kkr * lax.rsqrt(_mm_split(kkr * kkr, seg) + 1e-6)
        k2 = k * (1.0 + (a - 1.0) * ka_ref[:, sl])
        b = kk * a
        cs_last = cs[L - 1:L, :]
        e_neg = jnp.exp(-cs)
        e_rem = jnp.exp(cs_last - cs)
        bhat = b * e_neg
        khat = k2 * e_neg
        pairs.append(dict(sl=sl, r=r, v=v, k2=k2, rhat=r * jnp.exp(cs), gam_last=jnp.exp(cs_last),
                          btil=b * e_rem, ktil=k2 * e_rem, a2=both(kk * jnp.exp(cs - ldp)), v2=both(v),
                          rhs4=jnp.concatenate([bhat, bhat, khat, khat], axis=0)))
    for q in pairs:
        lhs = jnp.concatenate([q["a2"], both(q["rhat"])], axis=0)
        q["g"] = _mm_nt(lhs, q["rhs4"])
    tinvs = _tri_inv_multi([jnp.where(bd_strict, q["g"][0:2 * L, 0:2 * L], 0.0) for q in pairs], 2 * L, L)
    for q, tinv in zip(pairs, tinvs):
        q["tinv"] = tinv
        q["x2"] = _mm(jnp.where(bd_strict, q["g"][0:2 * L, 2 * L:4 * L], 0.0), q["v2"])
    for q in pairs:
        q["w"] = fold(_mm(q["tinv"], q["a2"]))
        q["u0"] = -fold(_mm(q["tinv"], q["x2"]))
        q["y0"] = fold(_mm(jnp.where(bd_incl, q["g"][2 * L:4 * L, 2 * L:4 * L], 0.0), q["v2"]))

    zts = [zt_ref[p] for p in range(npair)]
    us = [q["u0"] - _mm_nt(q["w"], zt) for q, zt in zip(pairs, zts)]
    ys = [q["y0"] + _mm_nt(q["rhat"], zt)
          + fold(_mm(jnp.where(bd_incl, q["g"][2 * L:4 * L, 0:2 * L], 0.0), both(u)))
          for q, zt, u in zip(pairs, zts, us)]
    hr, hc = _iota2((LANES, LANES))
    head_bd = (hr // RW_HEAD_DIM) == (hc // RW_HEAD_DIM)
    for p, (q, zt, u) in enumerate(zip(pairs, zts, us)):
        zt_ref[p] = zt * q["gam_last"] + jnp.where(head_bd, _mm_tn(u, q["btil"]) + _mm_tn(q["v"], q["ktil"]), 0.0)

    for q, y in zip(pairs, ys):
        sl = q["sl"]
        mean = _mm_split(y, seg) * (1.0 / RW_HEAD_DIM)
        yc = y - mean
        var = _mm_split(yc * yc, seg) * (1.0 / RW_HEAD_DIM)
        yn = yc * lax.rsqrt(var + RW_LN_EPS) * lnw_ref[:, sl] + lnb_ref[:, sl]
        bonus = _mm_split(q["r"] * q["k2"] * rk_ref[:, sl], seg) * q["v"]
        o_ref[0, :, sl] = (yn + bonus) * g_all[:, sl]


def _rwkv7(p, mu, w0, w2, a0, a2, g2, k_k, k_a, r_k, ln_w, ln_b):
    bsz, t, _ = p.shape
    row = lambda z: z.reshape(1, -1).astype(F32)
    w2p = jnp.concatenate([w2, jnp.zeros_like(w2)], axis=0).astype(BF16)
    a2p = jnp.concatenate([jnp.zeros_like(a2), a2], axis=0).astype(BF16)
    params = [row(mu), row(w0), w2p, row(a0), a2p, g2.astype(BF16), row(k_k), row(k_a), row(r_k), row(ln_w), row(ln_b)]
    full = lambda z: pl.BlockSpec(z.shape, lambda b, c: (0, 0))
    return pl.pallas_call(
        _rwkv7_kernel,
        grid=(bsz, t // CHUNK),
        in_specs=[pl.BlockSpec((1, CHUNK, RW_IN), lambda b, c: (b, c, 0))] + [full(z) for z in params],
        out_specs=pl.BlockSpec((1, CHUNK, RW_DIM), lambda b, c: (b, c, 0)),
        out_shape=jax.ShapeDtypeStruct((bsz, t, RW_DIM), F32),
        scratch_shapes=[pltpu.VMEM((8, RW_IN), F32), pltpu.VMEM((RW_HEADS // 2, LANES, LANES), F32)],
        compiler_params=_cparams("arbitrary", "arbitrary"),
        name="rwkv7_mix",
    )(p, *params)


def _hgrn2_kernel(p_ref, lb_ref, ng_ref, o_ref, st_ref):
    L = CHUNK
    c = pl.program_id(1)

    @pl.when(c == 0)
    def _():
        st_ref[...] = jnp.zeros_like(st_ref)

    x = p_ref[0]
    lb = lb_ref[...]
    q_all = jax.nn.silu(x[:, 0:HG_KDIM])
    fg = lb + (1.0 - lb) * jax.nn.sigmoid(x[:, HG_KDIM:2 * HG_KDIM])
    k_all = 1.0 - fg
    logf = jnp.log(fg)
    v_all = x[:, 2 * HG_KDIM:3 * HG_KDIM]
    gate = x[:, 3 * HG_KDIM:]
    tr, tc = _iota2((L, L))
    blk_tril = jnp.where((tc <= tr) & ((tr // SUB) == (tc // SUB)), 1.0, 0.0).astype(BF16)
    bc_all = _cumsum_rows(blk_tril, logf)
    seg = _seg_ones(LANES, HG_DIM)
    br, bcc = _iota2((LANES, LANES))
    bd = (br // HG_DIM) == (bcc // HG_DIM)
    t3 = lax.broadcasted_iota(I32, (SUB, SUB, 1), 0)
    s3 = lax.broadcasted_iota(I32, (SUB, SUB, 1), 1)
    causal3 = s3 <= t3

    for p in range(HG_HEADS // 2):
        sl = slice(p * LANES, (p + 1) * LANES)
        st = st_ref[p]
        outs = []
        for j in range(L // SUB):
            rs = slice(j * SUB, (j + 1) * SUB)
            q, k, v, bc = q_all[rs, sl], k_all[rs, sl], v_all[rs, sl], bc_all[rs, sl]
            bend = bc[SUB - 1:SUB, :]
            o = _mm_nt(q * jnp.exp(bc), st)
            diff = jnp.where(causal3, bc[:, None, :] - bc[None, :, :], NEG)
            pr = (jnp.exp(diff) * q[:, None, :] * k[None, :, :]).reshape(SUB * SUB, LANES)
            score = _mm_split(pr, seg).reshape(SUB, SUB, LANES)
            o = o + jnp.sum(score * v[None, :, :], axis=1)
            st = st * jnp.exp(bend) + jnp.where(bd, _mm_tn(v, k * jnp.exp(bend - bc)), 0.0)
            outs.append(o)
        st_ref[p] = st
        o = jnp.concatenate(outs, axis=0)
        ms = _mm_split(o * o, seg) * (1.0 / HG_DIM)
        o_ref[0, :, sl] = o * lax.rsqrt(ms + NORM_EPS) * ng_ref[:, sl] * jax.nn.silu(gate[:, sl])


def _hgrn2(p, lb, norm_g):
    bsz, t, _ = p.shape
    lb = lb.reshape(1, HG_KDIM).astype(F32)
    ng = jnp.tile(norm_g.astype(F32), HG_HEADS).reshape(1, HG_KDIM)
    return pl.pallas_call(
        _hgrn2_kernel,
        grid=(bsz, t // CHUNK),
        in_specs=[pl.BlockSpec((1, CHUNK, HG_IN), lambda b, c: (b, c, 0)),
                  pl.BlockSpec((1, HG_KDIM), lambda b, c: (0, 0)), pl.BlockSpec((1, HG_KDIM), lambda b, c: (0, 0))],
        out_specs=pl.BlockSpec((1, CHUNK, HG_KDIM), lambda b, c: (b, c, 0)),
        out_shape=jax.ShapeDtypeStruct((bsz, t, HG_KDIM), F32),
        scratch_shapes=[pltpu.VMEM((HG_HEADS // 2, LANES, LANES), F32)],
        compiler_params=_cparams("arbitrary", "arbitrary"),
        name="hgrn2_mix",
    )(p, lb, ng)


def _gdn_kernel(p_ref, pg_ref, cw_ref, gb_ref, nal_ref, nalc_ref, ng_ref, o_ref, prev_ref, s_ref):
    L = CHUNK
    c = pl.program_id(1)

    @pl.when(c == 0)
    def _():
        prev_ref[...] = jnp.zeros_like(prev_ref)
        s_ref[...] = jnp.zeros_like(s_ref)

    x = p_ref[0]
    xq = x[:, 0:3 * GD_DIM]
    xprev = prev_ref[...]
    row = lax.broadcasted_iota(I32, xq.shape, 0)
    conv = xq * cw_ref[CONV_K - 1:CONV_K, :]
    for j in range(1, CONV_K):
        shifted = jnp.where(row >= j, pltpu.roll(xq, j, 0), pltpu.roll(xprev, j, 0))
        conv = conv + shifted * cw_ref[CONV_K - 1 - j:CONV_K - j, :]
    prev_ref[...] = xq
    qkv = jax.nn.silu(conv)
    z = x[:, 3 * GD_DIM:]

    gt = pg_ref[0] + gb_ref[...]
    beta_all = jax.nn.sigmoid(gt)
    g_all = nal_ref[...] * jax.nn.softplus(gt)
    gt_t = gt.T
    g_t = nalc_ref[...] * jax.nn.softplus(gt_t)
    tr, tc = _iota2((L, L))
    gam_all = _cumsum_rows(jnp.where(tc <= tr, 1.0, 0.0).astype(BF16), g_all)
    ur, uc = _iota2((L, 2 * L))
    triu2 = jnp.where(ur <= (uc % L), 1.0, 0.0).astype(BF16)
    gam_t2 = _cumsum_cols(g_t, triu2)
    incl = tc <= tr
    br, bc = _iota2((2 * L, 2 * L))
    bd = (br // L) == (bc // L)
    bd_strict = bd & (bc < br)
    bd_incl = bd & (bc <= br)
    lane2 = lax.broadcasted_iota(I32, (1, 2 * L), 1)
    zero = jnp.zeros((L, LANES), F32)

    heads = []
    for h in range(GD_HEADS):
        q = qkv[:, h * LANES:(h + 1) * LANES]
        k = qkv[:, GD_DIM + h * LANES:GD_DIM + (h + 1) * LANES]
        v = qkv[:, 2 * GD_DIM + h * LANES:2 * GD_DIM + (h + 1) * LANES]
        q = q * lax.rsqrt(jnp.sum(q * q, axis=-1, keepdims=True) + 1e-6) * (GD_HEAD_DIM ** -0.5)
        k = k * lax.rsqrt(jnp.sum(k * k, axis=-1, keepdims=True) + 1e-6)
        beta = beta_all[:, h:h + 1]
        gam = gam_all[:, GD_HEADS + h:GD_HEADS + h + 1]
        gam_row2 = gam_t2[GD_HEADS + h:GD_HEADS + h + 1, :]
        gam_last = gam[L - 1:L, :]
        kb = k * beta
        heads.append(dict(q=q, k=k, kb=kb, vb=v * beta, gam=gam, gam_row2=gam_row2, gam_last=gam_last,
                          kg=kb * jnp.exp(gam), qg=q * jnp.exp(gam), kd=k * jnp.exp(gam_last - gam),
                          decay=jnp.exp(jnp.where(incl, gam - gam_row2[:, 0:L], NEG))))
    ms = []
    for pr in range(GD_HEADS // 2):
        h0, h1 = heads[2 * pr], heads[2 * pr + 1]
        lhs = jnp.concatenate([jnp.concatenate([h0["kb"], zero], axis=1),
                               jnp.concatenate([zero, h1["kb"]], axis=1)], axis=0)
        rhs = jnp.concatenate([jnp.concatenate([h0["k"], zero], axis=1),
                               jnp.concatenate([zero, h1["k"]], axis=1)], axis=0)
        gam_col = jnp.concatenate([h0["gam"], h1["gam"]], axis=0)
        gam_row = jnp.where(lane2 < L, h0["gam_row2"], h1["gam_row2"])
        decay2 = jnp.exp(jnp.where(bd_incl, gam_col - gam_row, NEG))
        ms.append(jnp.where(bd_strict, _mm_nt(lhs, rhs) * decay2, 0.0))
    tinvs = _tri_inv_multi(ms, 2 * L, L)
    for pr, tinv in enumerate(tinvs):
        h0, h1 = heads[2 * pr], heads[2 * pr + 1]
        rhs = jnp.concatenate([jnp.concatenate([h0["vb"], h0["kg"]], axis=1),
                               jnp.concatenate([h1["vb"], h1["kg"]], axis=1)], axis=0)
        uw = _mm(tinv, rhs)
        h0["u"], h0["w"] = uw[0:L, 0:LANES], uw[0:L, LANES:]
        h1["u"], h1["w"] = uw[L:2 * L, 0:LANES], uw[L:2 * L, LANES:]
    for hd in heads:
        hd["attn"] = _mm_nt(hd["q"], hd["k"]) * hd["decay"]

    ss = [s_ref[h] for h in range(GD_HEADS)]
    v_news = [hd["u"] - _mm(hd["w"], s) for hd, s in zip(heads, ss)]
    os_ = [_mm(hd["qg"], s) + _mm(hd["attn"], vn) for hd, s, vn in zip(heads, ss, v_news)]
    for h, (hd, s, vn) in enumerate(zip(heads, ss, v_news)):
        s_ref[h] = s * jnp.exp(hd["gam_last"]) + _mm_tn(hd["kd"], vn)
    for h, o in enumerate(os_):
        sl = slice(h * LANES, (h + 1) * LANES)
        o_ref[0, :, sl] = _rms_rows(o, ng_ref[...]) * jax.nn.silu(z[:, sl])


def _gate_row(lo, vals):
    return jnp.zeros((1, LANES), F32).at[0, lo:lo + vals.shape[0]].set(vals.astype(F32))


def _gdn(p, pg, conv_w, a_log, dt_bias, norm_g):
    bsz, t, _ = p.shape
    gbias = _gate_row(GD_HEADS, dt_bias)
    nal = _gate_row(GD_HEADS, -jnp.exp(a_log))
    nal_col = nal.reshape(LANES, 1)
    ng = norm_g.reshape(1, GD_HEAD_DIM).astype(F32)
    full = lambda z: pl.BlockSpec(z.shape, lambda b, c: (0, 0))
    return pl.pallas_call(
        _gdn_kernel,
        grid=(bsz, t // CHUNK),
        in_specs=[pl.BlockSpec((1, CHUNK, GD_MAIN), lambda b, c: (b, c, 0)),
                  pl.BlockSpec((1, CHUNK, LANES), lambda b, c: (b, c, 0)),
                  full(conv_w), full(gbias), full(nal), full(nal_col), full(ng)],
        out_specs=pl.BlockSpec((1, CHUNK, GD_DIM), lambda b, c: (b, c, 0)),
        out_shape=jax.ShapeDtypeStruct((bsz, t, GD_DIM), F32),
        scratch_shapes=[pltpu.VMEM((CHUNK, 3 * GD_DIM), F32), pltpu.VMEM((GD_HEADS, LANES, LANES), F32)],
        compiler_params=_cparams("arbitrary", "arbitrary"),
        name="gdn_mix",
    )(p, pg, conv_w.astype(F32), gbias, nal, nal_col, ng)


def _mlstm_kernel(p_ref, pg_ref, gb_ref, ng_ref, o_ref, c_ref, n_ref, m_ref):
    L = ML_CHUNK
    ci = pl.program_id(1)

    @pl.when(ci == 0)
    def _():
        c_ref[...] = jnp.zeros_like(c_ref)
        n_ref[...] = jnp.zeros_like(n_ref)
        m_ref[...] = jnp.zeros_like(m_ref)

    x = p_ref[0]
    gt = pg_ref[0] + gb_ref[...]
    logf = jax.nn.log_sigmoid(gt)
    gt_t = gt.T
    logf_t = jax.nn.log_sigmoid(gt_t)
    tr, tc = _iota2((L, L))
    bc_all = _cumsum_rows(jnp.where(tc <= tr, 1.0, 0.0).astype(BF16), logf)
    bc_t = _cumsum_cols(logf_t, jnp.where(tr <= tc, 1.0, 0.0).astype(BF16))
    incl = tc <= tr
    i_lo, f_lo = 2 * GD_HEADS, 2 * GD_HEADS + ML_HEADS

    for h in range(ML_HEADS):
        sl = slice(h * LANES, (h + 1) * LANES)
        q = x[:, sl]
        k = x[:, ML_DIM + h * LANES:ML_DIM + (h + 1) * LANES] * (ML_HEAD_DIM ** -0.5)
        v = x[:, 2 * ML_DIM + h * LANES:2 * ML_DIM + (h + 1) * LANES]
        og = jax.nn.sigmoid(x[:, 3 * ML_DIM + h * LANES:3 * ML_DIM + (h + 1) * LANES])
        bc = bc_all[:, f_lo + h:f_lo + h + 1]
        bc_row = bc_t[f_lo + h:f_lo + h + 1, :]
        ig = gt[:, i_lo + h:i_lo + h + 1]
        ig_row = gt_t[i_lo + h:i_lo + h + 1, :]
        m_prev = m_ref[h][0:1, 0:1]
        b_last = bc[L - 1:L, :]

        d_log = jnp.where(incl, bc - bc_row + ig_row, NEG)
        inter_log = bc + m_prev
        m_t = jnp.maximum(inter_log, jnp.max(d_log, axis=-1, keepdims=True))
        d_w = jnp.exp(d_log - m_t)
        inter_w = jnp.exp(inter_log - m_t)
        sc = _mm_nt(q, k) * d_w
        cmat = c_ref[h]
        nvec = n_ref[h][0:1, :]
        num = inter_w * _mm(q, cmat) + _mm(sc, v)
        den = inter_w * jnp.sum(q * nvec, axis=-1, keepdims=True) + jnp.sum(sc, axis=-1, keepdims=True)
        hh = num / jnp.maximum(jnp.abs(den), jnp.exp(-m_t))

        upd_log = b_last - bc + ig
        m_new = jnp.maximum(b_last + m_prev, jnp.max(upd_log, axis=0, keepdims=True))
        w_s = jnp.exp(upd_log - m_new)
        dec = jnp.exp(b_last + m_prev - m_new)
        wk = w_s * k
        c_ref[h] = dec * cmat + _mm_tn(wk, v)
        n_ref[h] = jnp.broadcast_to(dec * nvec + jnp.sum(wk, axis=0, keepdims=True), (8, LANES))
        m_ref[h] = jnp.broadcast_to(m_new, (8, LANES))

        o_ref[0, :, sl] = og * _rms_rows(hh, ng_ref[:, sl])


def _mlstm(p, pg, i_bias, f_bias, norm_g):
    bsz, t, _ = p.shape
    gbias = _gate_row(2 * GD_HEADS, jnp.concatenate([i_bias, f_bias]))
    ng = norm_g.reshape(1, ML_DIM).astype(F32)
    lc = min(ML_CHUNK, t)
    full = lambda z: pl.BlockSpec(z.shape, lambda b, c: (0, 0))
    return pl.pallas_call(
        _mlstm_kernel,
        grid=(bsz, t // lc),
        in_specs=[pl.BlockSpec((1, lc, ML_MAIN), lambda b, c: (b, c, 0)),
                  pl.BlockSpec((1, lc, LANES), lambda b, c: (b, c, 0)), full(gbias), full(ng)],
        out_specs=pl.BlockSpec((1, lc, ML_DIM), lambda b, c: (b, c, 0)),
        out_shape=jax.ShapeDtypeStruct((bsz, t, ML_DIM), F32),
        scratch_shapes=[pltpu.VMEM((ML_HEADS, LANES, LANES), F32), pltpu.VMEM((ML_HEADS, 8, LANES), F32),
                        pltpu.VMEM((ML_HEADS, 8, LANES), F32)],
        compiler_params=_cparams("arbitrary", "arbitrary"),
        name="mlstm_mix",
    )(p, pg, gbias, ng)


def _router_kernel(h_ref, g_ref, w_ref, b_ref, ri_ref, rg_ref, cnt_ref, off_ref):
    tm = h_ref.shape[0]
    i = pl.program_id(0)

    @pl.when(i == 0)
    def _():
        off_ref[...] = jnp.zeros_like(off_ref)

    xn = _rms_rows(h_ref[...], g_ref[...])
    logits = _mm_hi(xn, w_ref[...]) + b_ref[...]
    lane = lax.broadcasted_iota(I32, logits.shape, 1)
    big = jnp.int32(1 << 20)
    is_grp = (lane >= N_EXPERTS) & (lane < N_EXPERTS + N_GROUPS)
    lg = jnp.where(is_grp, logits, NEG)
    gmax = jnp.max(lg, axis=-1, keepdims=True)
    p_top = 1.0 / jnp.sum(jnp.exp(lg - gmax), axis=-1, keepdims=True)
    g_idx = jnp.min(jnp.where(lg == gmax, lane, big), axis=-1, keepdims=True) - N_EXPERTS
    valid = (lane < N_EXPERTS) & ((lane // EXPERTS_PER_GROUP) == g_idx)
    v1 = jnp.where(valid, logits, NEG)
    m1 = jnp.max(v1, axis=-1, keepdims=True)
    i1 = jnp.min(jnp.where(v1 == m1, lane, big), axis=-1, keepdims=True)
    v2 = jnp.where(lane == i1, NEG, v1)
    m2 = jnp.max(v2, axis=-1, keepdims=True)
    i2 = jnp.min(jnp.where(v2 == m2, lane, big), axis=-1, keepdims=True)
    e21 = jnp.exp(m2 - m1)
    gate1 = p_top / (1.0 + e21)
    gate2 = p_top * e21 / (1.0 + e21)

    sel1 = lane == i1
    sel2 = lane == i2
    onehot = jnp.where(sel1 | sel2, 1.0, 0.0)
    tr, tc = _iota2((tm, tm))
    before = jnp.where(tc < tr, 1.0, 0.0).astype(BF16)
    prefix = jnp.dot(before, onehot.astype(BF16), preferred_element_type=F32) + off_ref[0:1, :]
    rank1 = jnp.sum(jnp.where(sel1, prefix, 0.0), axis=-1, keepdims=True).astype(I32)
    rank2 = jnp.sum(jnp.where(sel2, prefix, 0.0), axis=-1, keepdims=True).astype(I32)
    total = off_ref[0:1, :] + jnp.sum(onehot, axis=0, keepdims=True)
    off_ref[...] = jnp.broadcast_to(total, off_ref.shape)
    cnt_ref[...] = jnp.broadcast_to(total, cnt_ref.shape).astype(I32)

    ri_ref[...] = jnp.where(lane == 0, i1, jnp.where(lane == 1, i2, jnp.where(lane == 2, rank1, jnp.where(lane == 3, rank2, 0))))
    rg_ref[...] = jnp.where(lane == 0, gate1, jnp.where(lane == 1, gate2, 0.0))


def _router(h, g, w_cat, b_cat):
    n, d = h.shape
    tm = min(PROJ_TM, n)
    return pl.pallas_call(
        _router_kernel,
        grid=(n // tm,),
        in_specs=[pl.BlockSpec((tm, d), lambda i: (i, 0)), pl.BlockSpec((1, d), lambda i: (0, 0)),
                  pl.BlockSpec((d, LANES), lambda i: (0, 0)), pl.BlockSpec((1, LANES), lambda i: (0, 0))],
        out_specs=[pl.BlockSpec((tm, LANES), lambda i: (i, 0)), pl.BlockSpec((tm, LANES), lambda i: (i, 0)),
                   pl.BlockSpec((8, LANES), lambda i: (0, 0))],
        out_shape=[jax.ShapeDtypeStruct((n, LANES), I32), jax.ShapeDtypeStruct((n, LANES), F32),
                   jax.ShapeDtypeStruct((8, LANES), I32)],
        scratch_shapes=[pltpu.VMEM((8, LANES), F32)],
        compiler_params=_cparams("arbitrary"),
        name="moe_router",
    )(h, g.reshape(1, d), w_cat, b_cat)


def _gather_rows(src_hbm, idx_ref, base, dst, sem):
    def body(j, carry):
        pltpu.make_async_copy(src_hbm.at[pl.ds(idx_ref[base + j], 1), :], dst.at[pl.ds(j, 1), :], sem).start()
        return carry

    lax.fori_loop(0, dst.shape[0], body, 0, unroll=GATHER_UNROLL)


def _wait_rows(src_hbm, dst, sem):
    pltpu.make_async_copy(src_hbm.at[pl.ds(0, dst.shape[0]), :], dst, sem).wait()


def _expert_kernel(te_ref, nt_ref, src_ref, h_hbm, g_ref, wg_ref, wu_ref, wd_ref, o_ref, xbuf, sem):
    del te_ref
    tm = xbuf.shape[1]
    i = pl.program_id(0)
    n_used = nt_ref[0]
    slot = i % 2

    @pl.when(i == 0)
    def _():
        _gather_rows(h_hbm, src_ref, 0, xbuf.at[0], sem.at[0])

    @pl.when(i + 1 < n_used)
    def _():
        _gather_rows(h_hbm, src_ref, (i + 1) * tm, xbuf.at[1 - slot], sem.at[1 - slot])

    @pl.when(i < n_used)
    def _():
        _wait_rows(h_hbm, xbuf.at[slot], sem.at[slot])
        xn = _rms_rows(xbuf[slot], g_ref[...]).astype(BF16)
        gate = jnp.dot(xn, wg_ref[0], preferred_element_type=F32)
        up = jnp.dot(xn, wu_ref[0], preferred_element_type=F32)
        hid = (jax.nn.silu(gate) * up).astype(BF16)
        o_ref[...] = jnp.dot(hid, wd_ref[0], preferred_element_type=F32)

    @pl.when(i >= n_used)
    def _():
        o_ref[...] = jnp.zeros_like(o_ref)


def _experts(h, g, tile_expert, n_used, src, w_gate, w_up, w_down):
    n, d = h.shape
    n_tiles = tile_expert.shape[0]
    tm = src.shape[0] // n_tiles
    de = w_gate.shape[-1]
    grid_spec = pltpu.PrefetchScalarGridSpec(
        num_scalar_prefetch=3,
        grid=(n_tiles,),
        in_specs=[pl.BlockSpec(memory_space=pl.ANY),
                  pl.BlockSpec((1, d), lambda i, te, nt, s: (0, 0)),
                  pl.BlockSpec((1, d, de), lambda i, te, nt, s: (te[i], 0, 0)),
                  pl.BlockSpec((1, d, de), lambda i, te, nt, s: (te[i], 0, 0)),
                  pl.BlockSpec((1, de, d), lambda i, te, nt, s: (te[i], 0, 0))],
        out_specs=pl.BlockSpec((tm, d), lambda i, te, nt, s: (i, 0)),
        scratch_shapes=[pltpu.VMEM((2, tm, d), F32), pltpu.SemaphoreType.DMA((2,))],
    )
    return pl.pallas_call(
        _expert_kernel,
        grid_spec=grid_spec,
        out_shape=jax.ShapeDtypeStruct((n_tiles * tm, d), F32),
        compiler_params=_cparams("arbitrary"),
        name="moe_experts",
    )(tile_expert, n_used, src, h, g.reshape(1, d), w_gate, w_up, w_down)


def _combine_kernel(d1_ref, d2_ref, ys_hbm, h_ref, rg_ref, gf_ref, o_ref, y1, y2, sem, *, final_norm):
    tm = h_ref.shape[0]
    i = pl.program_id(0)
    slot = i % 2

    def gather(tile, s):
        _gather_rows(ys_hbm, d1_ref, tile * tm, y1.at[s], sem.at[0, s])
        _gather_rows(ys_hbm, d2_ref, tile * tm, y2.at[s], sem.at[1, s])

    @pl.when(i == 0)
    def _():
        gather(0, 0)

    @pl.when(i + 1 < pl.num_programs(0))
    def _():
        gather(i + 1, 1 - slot)

    _wait_rows(ys_hbm, y1.at[slot], sem.at[0, slot])
    _wait_rows(ys_hbm, y2.at[slot], sem.at[1, slot])
    rg = rg_ref[...]
    out = h_ref[...] + rg[:, 0:1] * y1[slot] + rg[:, 1:2] * y2[slot]
    if final_norm:
        out = _rms_rows(out, gf_ref[...])
    o_ref[...] = out


def _combine(h, ys, d1, d2, rg, g_final, final_norm):
    n, d = h.shape
    tm = min(PROJ_TM, n)
    grid_spec = pltpu.PrefetchScalarGridSpec(
        num_scalar_prefetch=2,
        grid=(n // tm,),
        in_specs=[pl.BlockSpec(memory_space=pl.ANY),
                  pl.BlockSpec((tm, d), lambda i, a, b: (i, 0)),
                  pl.BlockSpec((tm, LANES), lambda i, a, b: (i, 0)),
                  pl.BlockSpec((1, d), lambda i, a, b: (0, 0))],
        out_specs=pl.BlockSpec((tm, d), lambda i, a, b: (i, 0)),
        scratch_shapes=[pltpu.VMEM((2, tm, d), F32), pltpu.VMEM((2, tm, d), F32), pltpu.SemaphoreType.DMA((2, 2))],
    )
    return pl.pallas_call(
        functools.partial(_combine_kernel, final_norm=final_norm),
        grid_spec=grid_spec,
        out_shape=jax.ShapeDtypeStruct((n, d), F32),
        compiler_params=_cparams("arbitrary"),
        name="moe_combine",
    )(d1, d2, ys, h, rg, g_final.reshape(1, d))


def _hmoe_residual(h, norm_g, w_group, b_group, w_router, b_router, w_gate, w_up, w_down, g_final, final_norm):
    n, d = h.shape
    pad = LANES - N_EXPERTS - N_GROUPS
    w_cat = jnp.concatenate([w_router, w_group, jnp.zeros((d, pad), F32)], axis=1)
    b_cat = jnp.concatenate([b_router, b_group, jnp.zeros((pad,), F32)]).reshape(1, LANES)
    ri, rg, cnt = _router(h, norm_g, w_cat, b_cat)

    counts = cnt[0, :N_EXPERTS]
    n_tiles = (2 * n) // MOE_TM + N_EXPERTS
    tiles_per = (counts + MOE_TM - 1) // MOE_TM
    tile_end = jnp.cumsum(tiles_per)
    seg_start = (tile_end - tiles_per) * MOE_TM
    d1 = seg_start[ri[:, 0]] + ri[:, 2]
    d2 = seg_start[ri[:, 1]] + ri[:, 3]
    tile_ids = jnp.arange(n_tiles, dtype=I32)
    tile_expert = jnp.minimum(jnp.sum((tile_end[None, :] <= tile_ids[:, None]).astype(I32), axis=1), N_EXPERTS - 1)
    tok = jnp.arange(n, dtype=I32)
    src = jnp.zeros((n_tiles * MOE_TM,), I32).at[d1].set(tok).at[d2].set(tok)

    n_used = tile_end[N_EXPERTS - 1:].astype(I32)
    ys = _experts(h, norm_g, tile_expert, n_used, src, w_gate.astype(BF16), w_up.astype(BF16), w_down.astype(BF16))
    return _combine(h, ys, d1, d2, rg, g_final, final_norm)


def kernel(x, norm_mix, norm_ffn, norm_final, ev_w_in, ev_mu, rw_w0, rw_w2, rw_a0, rw_a2, rw_g2, rw_k_k, rw_k_a, rw_r_k, rw_ln_w, rw_ln_b, hg_lb_logits, hg_norm, ev_w_out, od_w_in, gd_conv, gd_a_log, gd_dt_bias, gd_norm, ml_i_bias, ml_f_bias, ml_norm, od_w_out, moe_w_group, moe_b_group, moe_w_router, moe_b_router, moe_w_gate, moe_w_up, moe_w_down):
    bsz, t, d = x.shape
    n = bsz * t
    depth = norm_mix.shape[0]
    lb_table = jnp.cumsum(jax.nn.softmax(hg_lb_logits.astype(F32), axis=0), axis=0)
    h = x.reshape(n, d)
    for layer in range(depth):
        j = layer // 2
        if layer % 2 == 0:
            w_in = ev_w_in[j].astype(BF16)
            p_rw, p_hg = _rms_proj(h, norm_mix[layer], [w_in[:, :RW_IN], w_in[:, RW_IN:]])
            ya = _rwkv7(p_rw.reshape(bsz, t, RW_IN), ev_mu[j], rw_w0[j], rw_w2[j], rw_a0[j], rw_a2[j], rw_g2[j],
                        rw_k_k[j], rw_k_a[j], rw_r_k[j], rw_ln_w[j], rw_ln_b[j])
            yb = _hgrn2(p_hg.reshape(bsz, t, HG_IN), lb_table[j], hg_norm[j])
            h = _out_proj(h, ya.reshape(n, RW_DIM), yb.reshape(n, HG_KDIM), ev_w_out[j])
        else:
            w_in = od_w_in[j]
            w_gates = jnp.concatenate([w_in[:, GD_MAIN:GD_IN], w_in[:, GD_IN + ML_MAIN:],
                                       jnp.zeros((d, LANES - 2 * GD_HEADS - 2 * ML_HEADS), F32)], axis=1)
            p_gd, p_ml, p_gt = _rms_proj(h, norm_mix[layer], [w_in[:, :GD_MAIN].astype(BF16),
                                                              w_in[:, GD_IN:GD_IN + ML_MAIN].astype(BF16),
                                                              w_gates.astype(BF16)])
            p_gt = p_gt.reshape(bsz, t, LANES)
            yc = _gdn(p_gd.reshape(bsz, t, GD_MAIN), p_gt, gd_conv[j], gd_a_log[j], gd_dt_bias[j], gd_norm[j])
            yd = _mlstm(p_ml.reshape(bsz, t, ML_MAIN), p_gt, ml_i_bias[j], ml_f_bias[j], ml_norm[j])
            h = _out_proj(h, yc.reshape(n, GD_DIM), yd.reshape(n, ML_DIM), od_w_out[j])
        h = _hmoe_residual(h, norm_ffn[layer], moe_w_group[layer], moe_b_group[layer], moe_w_router[layer],
                           moe_b_router[layer], moe_w_gate[layer], moe_w_up[layer], moe_w_down[layer],
                           norm_final, final_norm=(layer == depth - 1))
    return h.reshape(bsz, t, d)
```

```python
import functools
import math

import jax
import jax.numpy as jnp
from jax import lax
from jax.experimental import pallas as pl
from jax.experimental.pallas import tpu as pltpu

F32 = jnp.float32
BF16 = jnp.bfloat16
I32 = jnp.int32
HIGHEST = lax.Precision.HIGHEST

D_MODEL = 1024
NORM_EPS = 1e-6
RW_HEADS, RW_HEAD_DIM = 8, 64
RW_DIM = RW_HEADS * RW_HEAD_DIM
R_DECAY, R_AAA, R_GATE = 64, 64, 128
RW_IN = 3 * RW_DIM + R_DECAY + R_AAA + R_GATE
RW_LN_EPS = 64e-5
HG_HEADS, HG_DIM = 8, 64
HG_KDIM = HG_HEADS * HG_DIM
HG_IN = 4 * HG_KDIM
GD_HEADS, GD_HEAD_DIM = 4, 128
GD_DIM = GD_HEADS * GD_HEAD_DIM
CONV_K = 4
GD_MAIN = 4 * GD_DIM
GD_IN = GD_MAIN + 2 * GD_HEADS
ML_HEADS, ML_HEAD_DIM = 4, 128
ML_DIM = ML_HEADS * ML_HEAD_DIM
ML_MAIN = 4 * ML_DIM
N_GROUPS, EXPERTS_PER_GROUP = 4, 8
N_EXPERTS = N_GROUPS * EXPERTS_PER_GROUP
D_EXPERT = 256

LANES = 128
VMEM_LIMIT_BYTES = 48 * 1024 * 1024

PROJ_TM = 256
CHUNK = 64
SUB = 16
RW_NCH = 4
GD_NCH = 4
ML_CHUNK = 128
MOE_TM = 256
GATHER_UNROLL = 8
NEG = -1e30


def _cparams(*sem):
    return pltpu.CompilerParams(dimension_semantics=sem, vmem_limit_bytes=VMEM_LIMIT_BYTES)


def _mm(a, b):
    return jnp.dot(a.astype(BF16), b.astype(BF16), preferred_element_type=F32)


def _mm_nt(a, b):
    return lax.dot_general(a.astype(BF16), b.astype(BF16), (((1,), (1,)), ((), ())), preferred_element_type=F32)


def _mm_tn(a, b):
    return lax.dot_general(a.astype(BF16), b.astype(BF16), (((0,), (0,)), ((), ())), preferred_element_type=F32)


def _mm_hi(a, b):
    return jnp.dot(a, b, precision=HIGHEST, preferred_element_type=F32)


def _mm_split(x, ones_bf16):
    hi = x.astype(BF16)
    lo = (x - hi.astype(F32)).astype(BF16)
    return (jnp.dot(hi, ones_bf16, preferred_element_type=F32) + jnp.dot(lo, ones_bf16, preferred_element_type=F32))


def _iota2(shape):
    return lax.broadcasted_iota(I32, shape, 0), lax.broadcasted_iota(I32, shape, 1)


def _rms_rows(x, g, eps=NORM_EPS):
    return x * lax.rsqrt(jnp.mean(x * x, axis=-1, keepdims=True) + eps) * g


def _seg_ones(width, seg):
    r, c = _iota2((width, width))
    return jnp.where((r // seg) == (c // seg), 1.0, 0.0).astype(BF16)


def _split3(x):
    x1 = x.astype(BF16)
    r1 = x - x1.astype(F32)
    x2 = r1.astype(BF16)
    return x1, x2, (r1 - x2.astype(F32)).astype(BF16)


def _cumsum_rows(tri_bf16, x):
    return sum(jnp.dot(tri_bf16, t, preferred_element_type=F32) for t in _split3(x))


def _cumsum_cols(x, tri_bf16):
    return sum(jnp.dot(t, tri_bf16, preferred_element_type=F32) for t in _split3(x))


def _mm3(a, b):
    ah = a.astype(BF16)
    al = (a - ah.astype(F32)).astype(BF16)
    bh = b.astype(BF16)
    bl = (b - bh.astype(F32)).astype(BF16)
    dot = lambda x, y: jnp.dot(x, y, preferred_element_type=F32)
    return dot(ah, bh) + dot(ah, bl) + dot(al, bh)


def _tri_inv_multi(ms, n, chain):
    assert chain // SUB <= 4
    r, c = _iota2((n, n))
    same = (r // SUB) == (c // SUB)
    eye = jnp.where(r == c, 1.0, 0.0).astype(F32)
    ds = [jnp.where(same, m, 0.0) for m in ms]
    offs = [m - d for m, d in zip(ms, ds)]
    xs = [eye - d for d in ds]
    ps = ds
    for _ in range(3):
        ps = [_mm(p, p) for p in ps]
        xs = [x + _mm(x, p) for x, p in zip(xs, ps)]
    es = [_mm(x, o) for x, o in zip(xs, offs)]
    imes = [eye - e for e in es]
    e2s = [_mm(e, e) for e in es]
    ys = [i + _mm(i, e2) for i, e2 in zip(imes, e2s)]
    xs = [_mm(y, x) for y, x in zip(ys, xs)]
    res = [eye - x - _mm3(m, x) for m, x in zip(ms, xs)]
    return [x + _mm(x, rr) for x, rr in zip(xs, res)]


def _rms_proj_kernel(x_ref, g_ref, *refs, n_out):
    y = _rms_rows(x_ref[...], g_ref[...]).astype(BF16)
    for w_ref, o_ref in zip(refs[:n_out], refs[n_out:]):
        o_ref[...] = jnp.dot(y, w_ref[...], preferred_element_type=F32)


def _rms_proj(x, g, ws):
    n, d = x.shape
    tm = min(PROJ_TM, n)
    in_specs = [pl.BlockSpec((tm, d), lambda i: (i, 0)), pl.BlockSpec((1, d), lambda i: (0, 0))]
    in_specs += [pl.BlockSpec(w.shape, lambda i: (0, 0)) for w in ws]
    return pl.pallas_call(
        functools.partial(_rms_proj_kernel, n_out=len(ws)),
        grid=(n // tm,),
        in_specs=in_specs,
        out_specs=[pl.BlockSpec((tm, w.shape[1]), lambda i: (i, 0)) for w in ws],
        out_shape=[jax.ShapeDtypeStruct((n, w.shape[1]), F32) for w in ws],
        compiler_params=_cparams("parallel"),
        name="rms_proj",
    )(x, g.reshape(1, d), *ws)


ROW_TILE = D_MODEL // LANES


def _store_row_tiles(ref, x):
    for j in range(ROW_TILE):
        ref[pl.ds(j, x.shape[0], stride=ROW_TILE), :] = x[:, j * LANES:(j + 1) * LANES]


def _load_row_tiles(ref, rows):
    return jnp.concatenate([ref[pl.ds(j, rows, stride=ROW_TILE), :] for j in range(ROW_TILE)], axis=1)


def _out_proj_kernel(h_ref, ya_ref, yb_ref, wa_ref, wb_ref, o_ref, ot_ref):
    out = h_ref[...] + _mm(ya_ref[...], wa_ref[...]) + _mm(yb_ref[...], wb_ref[...])
    o_ref[...] = out
    _store_row_tiles(ot_ref, out)


def _out_proj(h, ya, yb, w_out):
    n, d = h.shape
    da, db = ya.shape[1], yb.shape[1]
    tm = min(PROJ_TM, n)
    wa = w_out[:da].astype(BF16)
    wb = w_out[da:].astype(BF16)
    return pl.pallas_call(
        _out_proj_kernel,
        grid=(n // tm,),
        in_specs=[pl.BlockSpec((tm, d), lambda i: (i, 0)), pl.BlockSpec((tm, da), lambda i: (i, 0)),
                  pl.BlockSpec((tm, db), lambda i: (i, 0)), pl.BlockSpec((da, d), lambda i: (0, 0)),
                  pl.BlockSpec((db, d), lambda i: (0, 0))],
        out_specs=[pl.BlockSpec((tm, d), lambda i: (i, 0)), pl.BlockSpec((tm * ROW_TILE, LANES), lambda i: (i, 0))],
        out_shape=[jax.ShapeDtypeStruct((n, d), F32), jax.ShapeDtypeStruct((n * ROW_TILE, LANES), F32)],
        compiler_params=_cparams("parallel"),
        name="out_proj",
    )(h, ya, yb, wa, wb)


def _rwkv7_kernel(p_ref, mu_ref, w0_ref, w2_ref, a0_ref, a2_ref, g2_ref, kk_ref, ka_ref, rk_ref,
                  lnw_ref, lnb_ref, o_ref, prev_ref, zt_ref):
    L = CHUNK
    npair = RW_HEADS // 2
    c = pl.program_id(1)

    @pl.when(c == 0)
    def _():
        prev_ref[...] = jnp.zeros_like(prev_ref)
        zt_ref[...] = jnp.zeros_like(zt_ref)

    x = p_ref[0]
    R = x.shape[0]
    nch = R // L
    row = lax.broadcasted_iota(I32, x.shape, 0)
    xs = jnp.where(row == 0, prev_ref[7:8, :], pltpu.roll(x, 1, 0))
    prev_ref[...] = x[R - 8:R, :]
    pm = x + mu_ref[...] * (xs - x)
    r_all = pm[:, 0:RW_DIM]
    k_all = pm[:, RW_DIM:2 * RW_DIM]
    v_all = pm[:, 2 * RW_DIM:3 * RW_DIM]
    wa = pm[:, 3 * RW_DIM:3 * RW_DIM + LANES]
    gl = pm[:, 3 * RW_DIM + LANES:]
    wlog = -jax.nn.softplus(-(w0_ref[...] + _mm(jnp.tanh(wa), w2_ref[...]))) - 0.5
    ld = -jnp.exp(wlog)
    a_all = jax.nn.sigmoid(a0_ref[...] + _mm(wa, a2_ref[...]))
    g_all = _mm(jax.nn.sigmoid(gl), g2_ref[...])

    tr, tc = _iota2((R, R))
    chunk_tril = jnp.where((tc <= tr) & ((tr // L) == (tc // L)), 1.0, 0.0).astype(BF16)
    cs_all = _cumsum_rows(chunk_tril, ld)
    seg = _seg_ones(LANES, RW_HEAD_DIM)
    lane = lax.broadcasted_iota(I32, (L, LANES), 1)
    hm = (lane < RW_HEAD_DIM, lane >= RW_HEAD_DIM)
    br, bc = _iota2((2 * L, 2 * L))
    bd = (br // L) == (bc // L)
    bd_strict = bd & (bc < br)
    bd_incl = bd & (bc <= br)
    fold = lambda z: z[0:L] + z[L:2 * L]
    both = lambda z: jnp.concatenate([jnp.where(hm[0], z, 0.0), jnp.where(hm[1], z, 0.0)], axis=0)

    units = []
    for ci in range(nch):
        rs = slice(ci * L, (ci + 1) * L)
        for p in range(npair):
            sl = slice(p * LANES, (p + 1) * LANES)
            r, k, v, a = r_all[rs, sl], k_all[rs, sl], v_all[rs, sl], a_all[rs, sl]
            cs, ldp = cs_all[rs, sl], ld[rs, sl]
            kkr = k * kk_ref[:, sl]
            kk = kkr * lax.rsqrt(_mm_split(kkr * kkr, seg) + 1e-6)
            k2 = k * (1.0 + (a - 1.0) * ka_ref[:, sl])
            b = kk * a
            cs_last = cs[L - 1:L, :]
            e_neg = jnp.exp(-cs)
            e_rem = jnp.exp(cs_last - cs)
            bhat = b * e_neg
            khat = k2 * e_neg
            units.append(dict(p=p, rs=rs, sl=sl, r=r, v=v, k2=k2, rhat=r * jnp.exp(cs), gam_last=jnp.exp(cs_last),
                              btil=b * e_rem, ktil=k2 * e_rem, a2=both(kk * jnp.exp(cs - ldp)), v2=both(v),
                              rhs4=jnp.concatenate([bhat, bhat, khat, khat], axis=0)))
    for q in units:
        lhs = jnp.concatenate([q["a2"], both(q["rhat"])], axis=0)
        q["g"] = _mm_nt(lhs, q["rhs4"])
    tinvs = _tri_inv_multi([jnp.where(bd_strict, q["g"][0:2 * L, 0:2 * L], 0.0) for q in units], 2 * L, L)
    for q, tinv in zip(units, tinvs):
        q["tinv"] = tinv
        q["x2"] = _mm(jnp.where(bd_strict, q["g"][0:2 * L, 2 * L:4 * L], 0.0), q["v2"])
    for q in units:
        uw = _mm(q["tinv"], jnp.concatenate([q["x2"], q["a2"]], axis=1))
        q["u0"] = -fold(uw[:, 0:LANES])
        q["w"] = fold(uw[:, LANES:])
        q["y0"] = fold(_mm(jnp.where(bd_incl, q["g"][2 * L:4 * L, 2 * L:4 * L], 0.0), q["v2"]))
    for q in units:
        rb = jnp.where(bd_incl, q["g"][2 * L:4 * L, 0:2 * L], 0.0)
        ruw = _mm(rb, jnp.concatenate([both(q["u0"]), both(q["w"])], axis=1))
        q["yc"] = q["y0"] + fold(ruw[:, 0:LANES])
        q["ry"] = q["rhat"] - fold(ruw[:, LANES:])
        q["c1"] = _mm_tn(jnp.concatenate([q["u0"], q["v"]], axis=0), jnp.concatenate([q["btil"], q["ktil"]], axis=0))
        q["c2"] = _mm_tn(q["w"], q["btil"])

    hr, hc = _iota2((LANES, LANES))
    head_bd = (hr // RW_HEAD_DIM) == (hc // RW_HEAD_DIM)
    zts = [zt_ref[p] for p in range(npair)]
    for ci in range(nch):
        qs = units[ci * npair:(ci + 1) * npair]
        for q, zt in zip(qs, zts):
            q["y"] = q["yc"] + _mm_nt(q["ry"], zt)
        zts = [zt * q["gam_last"] + jnp.where(head_bd, q["c1"] - _mm(zt, q["c2"]), 0.0) for q, zt in zip(qs, zts)]
    for p in range(npair):
        zt_ref[p] = zts[p]

    for q in units:
        sl, rs, y = q["sl"], q["rs"], q["y"]
        mean = _mm_split(y, seg) * (1.0 / RW_HEAD_DIM)
        yc = y - mean
        var = _mm_split(yc * yc, seg) * (1.0 / RW_HEAD_DIM)
        yn = yc * lax.rsqrt(var + RW_LN_EPS) * lnw_ref[:, sl] + lnb_ref[:, sl]
        bonus = _mm_split(q["r"] * q["k2"] * rk_ref[:, sl], seg) * q["v"]
        o_ref[0, rs, sl] = (yn + bonus) * g_all[rs, sl]


def _rwkv7(p, mu, w0, w2, a0, a2, g2, k_k, k_a, r_k, ln_w, ln_b):
    bsz, t, _ = p.shape
    row = lambda z: z.reshape(1, -1).astype(F32)
    w2p = jnp.concatenate([w2, jnp.zeros_like(w2)], axis=0).astype(BF16)
    a2p = jnp.concatenate([jnp.zeros_like(a2), a2], axis=0).astype(BF16)
    params = [row(mu), row(w0), w2p, row(a0), a2p, g2.astype(BF16), row(k_k), row(k_a), row(r_k), row(ln_w), row(ln_b)]
    full = lambda z: pl.BlockSpec(z.shape, lambda b, c: (0, 0))
    rows = RW_NCH * CHUNK
    return pl.pallas_call(
        _rwkv7_kernel,
        grid=(bsz, t // rows),
        in_specs=[pl.BlockSpec((1, rows, RW_IN), lambda b, c: (b, c, 0))] + [full(z) for z in params],
        out_specs=pl.BlockSpec((1, rows, RW_DIM), lambda b, c: (b, c, 0)),
        out_shape=jax.ShapeDtypeStruct((bsz, t, RW_DIM), F32),
        scratch_shapes=[pltpu.VMEM((8, RW_IN), F32), pltpu.VMEM((RW_HEADS // 2, LANES, LANES), F32)],
        compiler_params=_cparams("arbitrary", "arbitrary"),
        name="rwkv7_mix",
    )(p, *params)


def _hgrn2_kernel(p_ref, lb_ref, ng_ref, o_ref, st_ref):
    L = CHUNK
    c = pl.program_id(1)

    @pl.when(c == 0)
    def _():
        st_ref[...] = jnp.zeros_like(st_ref)

    x = p_ref[0]
    lb = lb_ref[...]
    q_all = jax.nn.silu(x[:, 0:HG_KDIM])
    fg = lb + (1.0 - lb) * jax.nn.sigmoid(x[:, HG_KDIM:2 * HG_KDIM])
    k_all = 1.0 - fg
    logf = jnp.log(fg)
    v_all = x[:, 2 * HG_KDIM:3 * HG_KDIM]
    gate = x[:, 3 * HG_KDIM:]
    tr, tc = _iota2((L, L))
    blk_tril = jnp.where((tc <= tr) & ((tr // SUB) == (tc // SUB)), 1.0, 0.0).astype(BF16)
    bc_all = _cumsum_rows(blk_tril, logf)
    seg = _seg_ones(LANES, HG_DIM)
    br, bcc = _iota2((LANES, LANES))
    bd = (br // HG_DIM) == (bcc // HG_DIM)
    half = SUB // 2
    t_lo = lax.broadcasted_iota(I32, (SUB, LANES), 0)
    t_hi = lax.broadcasted_iota(I32, (half, LANES), 0) + half

    for p in range(HG_HEADS // 2):
        sl = slice(p * LANES, (p + 1) * LANES)
        st = st_ref[p]
        outs = []
        for j in range(L // SUB):
            rs = slice(j * SUB, (j + 1) * SUB)
            q, k, v, bc = q_all[rs, sl], k_all[rs, sl], v_all[rs, sl], bc_all[rs, sl]
            bend = bc[SUB - 1:SUB, :]
            o = _mm_nt(q * jnp.exp(bc), st)
            q_hi, bc_hi = q[half:], bc[half:]
            parts = []
            for s in range(SUB):
                if s < half:
                    diff = jnp.where(t_lo >= s, bc - bc[s:s + 1, :], NEG)
                    parts.append(jnp.exp(diff) * q * k[s:s + 1, :])
                else:
                    diff = jnp.where(t_hi >= s, bc_hi - bc[s:s + 1, :], NEG)
                    parts.append(jnp.exp(diff) * q_hi * k[s:s + 1, :])
            score = _mm(jnp.concatenate(parts, axis=0), seg)
            acc_lo = jnp.zeros((half, LANES), F32)
            acc_hi = jnp.zeros((half, LANES), F32)
            off = 0
            for s in range(SUB):
                vs = v[s:s + 1, :]
                if s < half:
                    acc_lo = acc_lo + score[off:off + half] * vs
                    acc_hi = acc_hi + score[off + half:off + SUB] * vs
                    off += SUB
                else:
                    acc_hi = acc_hi + score[off:off + half] * vs
                    off += half
            o = o + jnp.concatenate([acc_lo, acc_hi], axis=0)
            st = st * jnp.exp(bend) + jnp.where(bd, _mm_tn(v, k * jnp.exp(bend - bc)), 0.0)
            outs.append(o)
        st_ref[p] = st
        o = jnp.concatenate(outs, axis=0)
        ms = _mm_split(o * o, seg) * (1.0 / HG_DIM)
        o_ref[0, :, sl] = o * lax.rsqrt(ms + NORM_EPS) * ng_ref[:, sl] * jax.nn.silu(gate[:, sl])


def _hgrn2(p, lb, norm_g):
    bsz, t, _ = p.shape
    lb = lb.reshape(1, HG_KDIM).astype(F32)
    ng = jnp.tile(norm_g.astype(F32), HG_HEADS).reshape(1, HG_KDIM)
    return pl.pallas_call(
        _hgrn2_kernel,
        grid=(bsz, t // CHUNK),
        in_specs=[pl.BlockSpec((1, CHUNK, HG_IN), lambda b, c: (b, c, 0)),
                  pl.BlockSpec((1, HG_KDIM), lambda b, c: (0, 0)), pl.BlockSpec((1, HG_KDIM), lambda b, c: (0, 0))],
        out_specs=pl.BlockSpec((1, CHUNK, HG_KDIM), lambda b, c: (b, c, 0)),
        out_shape=jax.ShapeDtypeStruct((bsz, t, HG_KDIM), F32),
        scratch_shapes=[pltpu.VMEM((HG_HEADS // 2, LANES, LANES), F32)],
        compiler_params=_cparams("arbitrary", "arbitrary"),
        name="hgrn2_mix",
    )(p, lb, ng)


def _gdn_kernel(p_ref, pg_ref, cw_ref, gb_ref, nal_ref, nalc_ref, ng_ref, o_ref, prev_ref, s_ref):
    L = CHUNK
    c = pl.program_id(1)

    @pl.when(c == 0)
    def _():
        prev_ref[...] = jnp.zeros_like(prev_ref)
        s_ref[...] = jnp.zeros_like(s_ref)

    x = p_ref[0]
    R = x.shape[0]
    nch = R // L
    xq = x[:, 0:3 * GD_DIM]
    xcat = jnp.concatenate([prev_ref[...], xq], axis=0)
    conv = xq * cw_ref[CONV_K - 1:CONV_K, :]
    for j in range(1, CONV_K):
        conv = conv + xcat[8 - j:8 - j + R, :] * cw_ref[CONV_K - 1 - j:CONV_K - j, :]
    prev_ref[...] = xq[R - 8:R, :]
    qkv = jax.nn.silu(conv)
    z = x[:, 3 * GD_DIM:]

    gt = pg_ref[0] + gb_ref[...]
    beta_all = jax.nn.sigmoid(gt)
    g_all = nal_ref[...] * jax.nn.softplus(gt)
    g_t = nalc_ref[...] * jax.nn.softplus(gt.T)
    tr, tc = _iota2((R, R))
    chunk_tril = jnp.where((tc <= tr) & ((tr // L) == (tc // L)), 1.0, 0.0).astype(BF16)
    gam_all = _cumsum_rows(chunk_tril, g_all)
    ur, uc = _iota2((R, 2 * R))
    triu2 = jnp.where(((ur // L) == (uc // (2 * L))) & ((ur % L) <= (uc % L)), 1.0, 0.0).astype(BF16)
    gam_t2 = _cumsum_cols(g_t, triu2)
    lr, lc = _iota2((L, L))
    incl = lc <= lr
    br, bc = _iota2((2 * L, 2 * L))
    bd = (br // L) == (bc // L)
    bd_strict = bd & (bc < br)
    bd_incl = bd & (bc <= br)
    lane2 = lax.broadcasted_iota(I32, (1, 2 * L), 1)
    zero = jnp.zeros((L, LANES), F32)

    units = []
    for ci in range(nch):
        rs = slice(ci * L, (ci + 1) * L)
        for h in range(GD_HEADS):
            q = qkv[rs, h * LANES:(h + 1) * LANES]
            k = qkv[rs, GD_DIM + h * LANES:GD_DIM + (h + 1) * LANES]
            v = qkv[rs, 2 * GD_DIM + h * LANES:2 * GD_DIM + (h + 1) * LANES]
            q = q * lax.rsqrt(jnp.sum(q * q, axis=-1, keepdims=True) + 1e-6) * (GD_HEAD_DIM ** -0.5)
            k = k * lax.rsqrt(jnp.sum(k * k, axis=-1, keepdims=True) + 1e-6)
            beta = beta_all[rs, h:h + 1]
            gam = gam_all[rs, GD_HEADS + h:GD_HEADS + h + 1]
            gam_row2 = gam_t2[GD_HEADS + h:GD_HEADS + h + 1, ci * 2 * L:(ci + 1) * 2 * L]
            gam_last = gam[L - 1:L, :]
            kb = k * beta
            units.append(dict(h=h, rs=rs, q=q, k=k, kb=kb, vb=v * beta, gam=gam, gam_row2=gam_row2,
                              gam_last=gam_last, kg=kb * jnp.exp(gam), qg=q * jnp.exp(gam),
                              kd=k * jnp.exp(gam_last - gam),
                              decay=jnp.exp(jnp.where(incl, gam - gam_row2[:, 0:L], NEG))))
    pairs = [(units[i], units[i + 1]) for i in range(0, len(units), 2)]
    ms = []
    for h0, h1 in pairs:
        lhs = jnp.concatenate([jnp.concatenate([h0["kb"], zero], axis=1),
                               jnp.concatenate([zero, h1["kb"]], axis=1)], axis=0)
        rhs = jnp.concatenate([jnp.concatenate([h0["k"], zero], axis=1),
                               jnp.concatenate([zero, h1["k"]], axis=1)], axis=0)
        gam_col = jnp.concatenate([h0["gam"], h1["gam"]], axis=0)
        gam_row = jnp.where(lane2 < L, h0["gam_row2"], h1["gam_row2"])
        decay2 = jnp.exp(jnp.where(bd_incl, gam_col - gam_row, NEG))
        ms.append(jnp.where(bd_strict, _mm_nt(lhs, rhs) * decay2, 0.0))
    tinvs = _tri_inv_multi(ms, 2 * L, L)
    for (h0, h1), tinv in zip(pairs, tinvs):
        rhs = jnp.concatenate([jnp.concatenate([h0["vb"], h0["kg"]], axis=1),
                               jnp.concatenate([h1["vb"], h1["kg"]], axis=1)], axis=0)
        uw = _mm(tinv, rhs)
        h0["u"], h0["w"] = uw[0:L, 0:LANES], uw[0:L, LANES:]
        h1["u"], h1["w"] = uw[L:2 * L, 0:LANES], uw[L:2 * L, LANES:]
    for hd in units:
        attn = _mm_nt(hd["q"], hd["k"]) * hd["decay"]
        uw = jnp.concatenate([hd["u"], hd["w"]], axis=1)
        auw = _mm(attn, uw)
        hd["o0"] = auw[:, 0:LANES]
        hd["qs"] = hd["qg"] - auw[:, LANES:]
        cc = _mm_tn(hd["kd"], uw)
        hd["c1"], hd["c2"] = cc[:, 0:LANES], cc[:, LANES:]

    ss = [s_ref[h] for h in range(GD_HEADS)]
    for ci in range(nch):
        hds = units[ci * GD_HEADS:(ci + 1) * GD_HEADS]
        for hd, s in zip(hds, ss):
            hd["o"] = hd["o0"] + _mm(hd["qs"], s)
        ss = [s * jnp.exp(hd["gam_last"]) + hd["c1"] - _mm(hd["c2"], s) for hd, s in zip(hds, ss)]
    for h in range(GD_HEADS):
        s_ref[h] = ss[h]
    for hd in units:
        sl = slice(hd["h"] * LANES, (hd["h"] + 1) * LANES)
        o_ref[0, hd["rs"], sl] = _rms_rows(hd["o"], ng_ref[...]) * jax.nn.silu(z[hd["rs"], sl])


def _gate_row(lo, vals):
    return jnp.zeros((1, LANES), F32).at[0, lo:lo + vals.shape[0]].set(vals.astype(F32))


def _gdn(p, pg, conv_w, a_log, dt_bias, norm_g):
    bsz, t, _ = p.shape
    gbias = _gate_row(GD_HEADS, dt_bias)
    nal = _gate_row(GD_HEADS, -jnp.exp(a_log))
    nal_col = nal.reshape(LANES, 1)
    ng = norm_g.reshape(1, GD_HEAD_DIM).astype(F32)
    full = lambda z: pl.BlockSpec(z.shape, lambda b, c: (0, 0))
    rows = GD_NCH * CHUNK
    return pl.pallas_call(
        _gdn_kernel,
        grid=(bsz, t // rows),
        in_specs=[pl.BlockSpec((1, rows, GD_MAIN), lambda b, c: (b, c, 0)),
                  pl.BlockSpec((1, rows, LANES), lambda b, c: (b, c, 0)),
                  full(conv_w), full(gbias), full(nal), full(nal_col), full(ng)],
        out_specs=pl.BlockSpec((1, rows, GD_DIM), lambda b, c: (b, c, 0)),
        out_shape=jax.ShapeDtypeStruct((bsz, t, GD_DIM), F32),
        scratch_shapes=[pltpu.VMEM((8, 3 * GD_DIM), F32), pltpu.VMEM((GD_HEADS, LANES, LANES), F32)],
        compiler_params=_cparams("arbitrary", "arbitrary"),
        name="gdn_mix",
    )(p, pg, conv_w.astype(F32), gbias, nal, nal_col, ng)


def _mlstm_kernel(p_ref, pg_ref, gb_ref, ng_ref, o_ref, c_ref, n_ref, m_ref):
    L = ML_CHUNK
    ci = pl.program_id(1)

    @pl.when(ci == 0)
    def _():
        c_ref[...] = jnp.zeros_like(c_ref)
        n_ref[...] = jnp.zeros_like(n_ref)
        m_ref[...] = jnp.zeros_like(m_ref)

    x = p_ref[0]
    gt = pg_ref[0] + gb_ref[...]
    logf = jax.nn.log_sigmoid(gt)
    gt_t = gt.T
    logf_t = jax.nn.log_sigmoid(gt_t)
    tr, tc = _iota2((L, L))
    bc_all = _cumsum_rows(jnp.where(tc <= tr, 1.0, 0.0).astype(BF16), logf)
    bc_t = _cumsum_cols(logf_t, jnp.where(tr <= tc, 1.0, 0.0).astype(BF16))
    incl = tc <= tr
    i_lo, f_lo = 2 * GD_HEADS, 2 * GD_HEADS + ML_HEADS

    for h in range(ML_HEADS):
        sl = slice(h * LANES, (h + 1) * LANES)
        q = x[:, sl]
        k = x[:, ML_DIM + h * LANES:ML_DIM + (h + 1) * LANES] * (ML_HEAD_DIM ** -0.5)
        v = x[:, 2 * ML_DIM + h * LANES:2 * ML_DIM + (h + 1) * LANES]
        og = jax.nn.sigmoid(x[:, 3 * ML_DIM + h * LANES:3 * ML_DIM + (h + 1) * LANES])
        bc = bc_all[:, f_lo + h:f_lo + h + 1]
        bc_row = bc_t[f_lo + h:f_lo + h + 1, :]
        ig = gt[:, i_lo + h:i_lo + h + 1]
        ig_row = gt_t[i_lo + h:i_lo + h + 1, :]
        m_prev = m_ref[h][0:1, 0:1]
        b_last = bc[L - 1:L, :]

        d_log = jnp.where(incl, bc - bc_row + ig_row, NEG)
        inter_log = bc + m_prev
        m_t = jnp.maximum(inter_log, jnp.max(d_log, axis=-1, keepdims=True))
        d_w = jnp.exp(d_log - m_t)
        inter_w = jnp.exp(inter_log - m_t)
        sc = _mm_nt(q, k) * d_w
        cmat = c_ref[h]
        nvec = n_ref[h][0:1, :]
        num = inter_w * _mm(q, cmat) + _mm(sc, v)
        den = inter_w * jnp.sum(q * nvec, axis=-1, keepdims=True) + jnp.sum(sc, axis=-1, keepdims=True)
        hh = num / jnp.maximum(jnp.abs(den), jnp.exp(-m_t))

        upd_log = b_last - bc + ig
        m_new = jnp.maximum(b_last + m_prev, jnp.max(upd_log, axis=0, keepdims=True))
        w_s = jnp.exp(upd_log - m_new)
        dec = jnp.exp(b_last + m_prev - m_new)
        wk = w_s * k
        c_ref[h] = dec * cmat + _mm_tn(wk, v)
        n_ref[h] = jnp.broadcast_to(dec * nvec + jnp.sum(wk, axis=0, keepdims=True), (8, LANES))
        m_ref[h] = jnp.broadcast_to(m_new, (8, LANES))

        o_ref[0, :, sl] = og * _rms_rows(hh, ng_ref[:, sl])


def _mlstm(p, pg, i_bias, f_bias, norm_g):
    bsz, t, _ = p.shape
    gbias = _gate_row(2 * GD_HEADS, jnp.concatenate([i_bias, f_bias]))
    ng = norm_g.reshape(1, ML_DIM).astype(F32)
    lc = min(ML_CHUNK, t)
    full = lambda z: pl.BlockSpec(z.shape, lambda b, c: (0, 0))
    return pl.pallas_call(
        _mlstm_kernel,
        grid=(bsz, t // lc),
        in_specs=[pl.BlockSpec((1, lc, ML_MAIN), lambda b, c: (b, c, 0)),
                  pl.BlockSpec((1, lc, LANES), lambda b, c: (b, c, 0)), full(gbias), full(ng)],
        out_specs=pl.BlockSpec((1, lc, ML_DIM), lambda b, c: (b, c, 0)),
        out_shape=jax.ShapeDtypeStruct((bsz, t, ML_DIM), F32),
        scratch_shapes=[pltpu.VMEM((ML_HEADS, LANES, LANES), F32), pltpu.VMEM((ML_HEADS, 8, LANES), F32),
                        pltpu.VMEM((ML_HEADS, 8, LANES), F32)],
        compiler_params=_cparams("arbitrary", "arbitrary"),
        name="mlstm_mix",
    )(p, pg, gbias, ng)


def _router_kernel(h_ref, g_ref, wh_ref, wl_ref, b_ref, ri_ref, rg_ref, cnt_ref, off_ref):
    tm = h_ref.shape[0]
    i = pl.program_id(0)

    @pl.when(i == 0)
    def _():
        off_ref[...] = jnp.zeros_like(off_ref)

    xn = _rms_rows(h_ref[...], g_ref[...])
    xh = xn.astype(BF16)
    xl = (xn - xh.astype(F32)).astype(BF16)
    dot = lambda a, b: jnp.dot(a, b, preferred_element_type=F32)
    logits = dot(xh, wh_ref[...]) + dot(xh, wl_ref[...]) + dot(xl, wh_ref[...]) + b_ref[...]
    lane = lax.broadcasted_iota(I32, logits.shape, 1)
    big = jnp.int32(1 << 20)
    is_grp = (lane >= N_EXPERTS) & (lane < N_EXPERTS + N_GROUPS)
    lg = jnp.where(is_grp, logits, NEG)
    gmax = jnp.max(lg, axis=-1, keepdims=True)
    p_top = 1.0 / jnp.sum(jnp.exp(lg - gmax), axis=-1, keepdims=True)
    g_idx = jnp.min(jnp.where(lg == gmax, lane, big), axis=-1, keepdims=True) - N_EXPERTS
    valid = (lane < N_EXPERTS) & ((lane // EXPERTS_PER_GROUP) == g_idx)
    v1 = jnp.where(valid, logits, NEG)
    m1 = jnp.max(v1, axis=-1, keepdims=True)
    i1 = jnp.min(jnp.where(v1 == m1, lane, big), axis=-1, keepdims=True)
    v2 = jnp.where(lane == i1, NEG, v1)
    m2 = jnp.max(v2, axis=-1, keepdims=True)
    i2 = jnp.min(jnp.where(v2 == m2, lane, big), axis=-1, keepdims=True)
    e21 = jnp.exp(m2 - m1)
    gate1 = p_top / (1.0 + e21)
    gate2 = p_top * e21 / (1.0 + e21)

    sel1 = lane == i1
    sel2 = lane == i2
    onehot = jnp.where(sel1 | sel2, 1.0, 0.0)
    tr, tc = _iota2((tm, tm))
    before = jnp.where(tc < tr, 1.0, 0.0).astype(BF16)
    prefix = jnp.dot(before, onehot.astype(BF16), preferred_element_type=F32) + off_ref[0:1, :]
    rank1 = jnp.sum(jnp.where(sel1, prefix, 0.0), axis=-1, keepdims=True).astype(I32)
    rank2 = jnp.sum(jnp.where(sel2, prefix, 0.0), axis=-1, keepdims=True).astype(I32)
    total = off_ref[0:1, :] + jnp.sum(onehot, axis=0, keepdims=True)
    off_ref[...] = jnp.broadcast_to(total, off_ref.shape)
    cnt_ref[...] = jnp.broadcast_to(total, cnt_ref.shape).astype(I32)

    ri_ref[...] = jnp.where(lane == 0, i1, jnp.where(lane == 1, i2, jnp.where(lane == 2, rank1, jnp.where(lane == 3, rank2, 0))))
    rg_ref[...] = jnp.where(lane == 0, gate1, jnp.where(lane == 1, gate2, 0.0))


def _router(h, g, w_cat, b_cat):
    n, d = h.shape
    tm = min(PROJ_TM, n)
    w_hi = w_cat.astype(BF16)
    w_lo = (w_cat - w_hi.astype(F32)).astype(BF16)
    return pl.pallas_call(
        _router_kernel,
        grid=(n // tm,),
        in_specs=[pl.BlockSpec((tm, d), lambda i: (i, 0)), pl.BlockSpec((1, d), lambda i: (0, 0)),
                  pl.BlockSpec((d, LANES), lambda i: (0, 0)), pl.BlockSpec((d, LANES), lambda i: (0, 0)),
                  pl.BlockSpec((1, LANES), lambda i: (0, 0))],
        out_specs=[pl.BlockSpec((tm, LANES), lambda i: (i, 0)), pl.BlockSpec((tm, LANES), lambda i: (i, 0)),
                   pl.BlockSpec((8, LANES), lambda i: (0, 0))],
        out_shape=[jax.ShapeDtypeStruct((n, LANES), I32), jax.ShapeDtypeStruct((n, LANES), F32),
                   jax.ShapeDtypeStruct((8, LANES), I32)],
        scratch_shapes=[pltpu.VMEM((8, LANES), F32)],
        compiler_params=_cparams("arbitrary"),
        name="moe_router",
    )(h, g.reshape(1, d), w_hi, w_lo, b_cat)


def _dest_kernel(ri_ref, ss_ref, o_ref):
    ri = ri_ref[...]
    lane = lax.broadcasted_iota(I32, ri.shape, 1)
    ss = ss_ref[...].astype(F32)
    pick = lambda e: jnp.sum(jnp.where(lane == e, ss, 0.0), axis=-1, keepdims=True)
    d1 = pick(ri[:, 0:1]) + ri[:, 2:3].astype(F32)
    d2 = pick(ri[:, 1:2]) + ri[:, 3:4].astype(F32)
    both = jnp.where(lane == 0, d1, jnp.where(lane == 1, d2, 0.0))
    o_ref[...] = both.T[0:8, :].astype(I32)


def _dest(ri, seg_start):
    n = ri.shape[0]
    tm = min(PROJ_TM, n)
    ss = jnp.zeros((1, LANES), I32).at[0, :N_EXPERTS].set(seg_start)
    return pl.pallas_call(
        _dest_kernel,
        grid=(n // tm,),
        in_specs=[pl.BlockSpec((tm, LANES), lambda i: (i, 0)), pl.BlockSpec((1, LANES), lambda i: (0, 0))],
        out_specs=pl.BlockSpec((8, tm), lambda i: (0, i)),
        out_shape=jax.ShapeDtypeStruct((8, n), I32),
        compiler_params=_cparams("parallel"),
        name="moe_dest",
    )(ri, ss)


def _gather_rows(src_hbm, idx_ref, base, dst, sem):
    def body(j, carry):
        r = pl.multiple_of(idx_ref[base + j] * ROW_TILE, ROW_TILE)
        pltpu.make_async_copy(src_hbm.at[pl.ds(r, ROW_TILE), :],
                              dst.at[pl.ds(pl.multiple_of(j * ROW_TILE, ROW_TILE), ROW_TILE), :], sem).start()
        return carry

    lax.fori_loop(0, dst.shape[0] // ROW_TILE, body, 0, unroll=GATHER_UNROLL)


def _wait_rows(src_hbm, dst, sem):
    pltpu.make_async_copy(src_hbm.at[pl.ds(0, dst.shape[0]), :], dst, sem).wait()


def _expert_kernel(te_ref, nt_ref, src_ref, h_hbm, g_ref, wg_ref, wu_ref, wd_ref, o_ref, xbuf, sem):
    del te_ref
    tm = xbuf.shape[1] // ROW_TILE
    i = pl.program_id(0)
    n_used = nt_ref[0]
    slot = i % 2

    @pl.when(i == 0)
    def _():
        _gather_rows(h_hbm, src_ref, 0, xbuf.at[0], sem.at[0])

    @pl.when(i + 1 < n_used)
    def _():
        _gather_rows(h_hbm, src_ref, (i + 1) * tm, xbuf.at[1 - slot], sem.at[1 - slot])

    @pl.when(i < n_used)
    def _():
        _wait_rows(h_hbm, xbuf.at[slot], sem.at[slot])
        xn = _rms_rows(_load_row_tiles(xbuf.at[slot], tm), g_ref[...]).astype(BF16)
        gate = jnp.dot(xn, wg_ref[0], preferred_element_type=F32)
        up = jnp.dot(xn, wu_ref[0], preferred_element_type=F32)
        hid = (jax.nn.silu(gate) * up).astype(BF16)
        _store_row_tiles(o_ref, jnp.dot(hid, wd_ref[0], preferred_element_type=F32))

    @pl.when(i >= n_used)
    def _():
        o_ref[...] = jnp.zeros_like(o_ref)


def _experts(h_tiles, g, tile_expert, n_used, src, w_gate, w_up, w_down):
    d = D_MODEL
    n_tiles = tile_expert.shape[0]
    tm = src.shape[0] // n_tiles
    de = w_gate.shape[-1]
    grid_spec = pltpu.PrefetchScalarGridSpec(
        num_scalar_prefetch=3,
        grid=(n_tiles,),
        in_specs=[pl.BlockSpec(memory_space=pl.ANY),
                  pl.BlockSpec((1, d), lambda i, te, nt, s: (0, 0)),
                  pl.BlockSpec((1, d, de), lambda i, te, nt, s: (te[i], 0, 0)),
                  pl.BlockSpec((1, d, de), lambda i, te, nt, s: (te[i], 0, 0)),
                  pl.BlockSpec((1, de, d), lambda i, te, nt, s: (te[i], 0, 0))],
        out_specs=pl.BlockSpec((tm * ROW_TILE, LANES), lambda i, te, nt, s: (i, 0)),
        scratch_shapes=[pltpu.VMEM((2, tm * ROW_TILE, LANES), F32), pltpu.SemaphoreType.DMA((2,))],
    )
    return pl.pallas_call(
        _expert_kernel,
        grid_spec=grid_spec,
        out_shape=jax.ShapeDtypeStruct((n_tiles * tm * ROW_TILE, LANES), F32),
        compiler_params=_cparams("arbitrary"),
        name="moe_experts",
    )(tile_expert, n_used, src, h_tiles, g.reshape(1, d), w_gate, w_up, w_down)


def _combine_kernel(d1_ref, d2_ref, ys_hbm, h_ref, rg_ref, gf_ref, o_ref, y1, y2, sem, *, final_norm):
    tm = h_ref.shape[0]
    i = pl.program_id(0)
    slot = i % 2

    def gather(tile, s):
        _gather_rows(ys_hbm, d1_ref, tile * tm, y1.at[s], sem.at[0, s])
        _gather_rows(ys_hbm, d2_ref, tile * tm, y2.at[s], sem.at[1, s])

    @pl.when(i == 0)
    def _():
        gather(0, 0)

    @pl.when(i + 1 < pl.num_programs(0))
    def _():
        gather(i + 1, 1 - slot)

    _wait_rows(ys_hbm, y1.at[slot], sem.at[0, slot])
    _wait_rows(ys_hbm, y2.at[slot], sem.at[1, slot])
    rg = rg_ref[...]
    out = (h_ref[...] + rg[:, 0:1] * _load_row_tiles(y1.at[slot], tm)
           + rg[:, 1:2] * _load_row_tiles(y2.at[slot], tm))
    if final_norm:
        out = _rms_rows(out, gf_ref[...])
    o_ref[...] = out


def _combine(h, ys, d1, d2, rg, g_final, final_norm):
    n, d = h.shape
    tm = min(PROJ_TM, n)
    grid_spec = pltpu.PrefetchScalarGridSpec(
        num_scalar_prefetch=2,
        grid=(n // tm,),
        in_specs=[pl.BlockSpec(memory_space=pl.ANY),
                  pl.BlockSpec((tm, d), lambda i, a, b: (i, 0)),
                  pl.BlockSpec((tm, LANES), lambda i, a, b: (i, 0)),
                  pl.BlockSpec((1, d), lambda i, a, b: (0, 0))],
        out_specs=pl.BlockSpec((tm, d), lambda i, a, b: (i, 0)),
        scratch_shapes=[pltpu.VMEM((2, tm * ROW_TILE, LANES), F32), pltpu.VMEM((2, tm * ROW_TILE, LANES), F32),
                        pltpu.SemaphoreType.DMA((2, 2))],
    )
    return pl.pallas_call(
        functools.partial(_combine_kernel, final_norm=final_norm),
        grid_spec=grid_spec,
        out_shape=jax.ShapeDtypeStruct((n, d), F32),
        compiler_params=_cparams("arbitrary"),
        name="moe_combine",
    )(d1, d2, ys, h, rg, g_final.reshape(1, d))


def _hmoe_residual(h, h_tiles, norm_g, w_group, b_group, w_router, b_router, w_gate, w_up, w_down, g_final, final_norm):
    n, d = h.shape
    pad = LANES - N_EXPERTS - N_GROUPS
    w_cat = jnp.concatenate([w_router, w_group, jnp.zeros((d, pad), F32)], axis=1)
    b_cat = jnp.concatenate([b_router, b_group, jnp.zeros((pad,), F32)]).reshape(1, LANES)
    ri, rg, cnt = _router(h, norm_g, w_cat, b_cat)

    counts = cnt[0, :N_EXPERTS]
    n_tiles = (2 * n) // MOE_TM + N_EXPERTS
    tiles_per = (counts + MOE_TM - 1) // MOE_TM
    tile_end = jnp.cumsum(tiles_per)
    seg_start = (tile_end - tiles_per) * MOE_TM
    dest = _dest(ri, seg_start.astype(I32))
    d1, d2 = dest[0], dest[1]
    tile_ids = jnp.arange(n_tiles, dtype=I32)
    tile_expert = jnp.minimum(jnp.sum((tile_end[None, :] <= tile_ids[:, None]).astype(I32), axis=1), N_EXPERTS - 1)
    tok = jnp.arange(n, dtype=I32)
    src = jnp.zeros((n_tiles * MOE_TM,), I32).at[dest[0:2].reshape(-1)].set(jnp.concatenate([tok, tok]))

    n_used = tile_end[N_EXPERTS - 1:].astype(I32)
    ys = _experts(h_tiles, norm_g, tile_expert, n_used, src, w_gate.astype(BF16), w_up.astype(BF16), w_down.astype(BF16))
    return _combine(h, ys, d1, d2, rg, g_final, final_norm)


def kernel(x, norm_mix, norm_ffn, norm_final, ev_w_in, ev_mu, rw_w0, rw_w2, rw_a0, rw_a2, rw_g2, rw_k_k, rw_k_a, rw_r_k, rw_ln_w, rw_ln_b, hg_lb_logits, hg_norm, ev_w_out, od_w_in, gd_conv, gd_a_log, gd_dt_bias, gd_norm, ml_i_bias, ml_f_bias, ml_norm, od_w_out, moe_w_group, moe_b_group, moe_w_router, moe_b_router, moe_w_gate, moe_w_up, moe_w_down):
    bsz, t, d = x.shape
    n = bsz * t
    depth = norm_mix.shape[0]
    lb_table = jnp.cumsum(jax.nn.softmax(hg_lb_logits.astype(F32), axis=0), axis=0)
    h = x.reshape(n, d)
    for layer in range(depth):
        j = layer // 2
        if layer % 2 == 0:
            w_in = ev_w_in[j].astype(BF16)
            p_rw, p_hg = _rms_proj(h, norm_mix[layer], [w_in[:, :RW_IN], w_in[:, RW_IN:]])
            ya = _rwkv7(p_rw.reshape(bsz, t, RW_IN), ev_mu[j], rw_w0[j], rw_w2[j], rw_a0[j], rw_a2[j], rw_g2[j],
                        rw_k_k[j], rw_k_a[j], rw_r_k[j], rw_ln_w[j], rw_ln_b[j])
            yb = _hgrn2(p_hg.reshape(bsz, t, HG_IN), lb_table[j], hg_norm[j])
            h, h_tiles = _out_proj(h, ya.reshape(n, RW_DIM), yb.reshape(n, HG_KDIM), ev_w_out[j])
        else:
            w_in = od_w_in[j]
            w_gates = jnp.concatenate([w_in[:, GD_MAIN:GD_IN], w_in[:, GD_IN + ML_MAIN:],
                                       jnp.zeros((d, LANES - 2 * GD_HEADS - 2 * ML_HEADS), F32)], axis=1)
            p_gd, p_ml, p_gt = _rms_proj(h, norm_mix[layer], [w_in[:, :GD_MAIN].astype(BF16),
                                                              w_in[:, GD_IN:GD_IN + ML_MAIN].astype(BF16),
                                                              w_gates.astype(BF16)])
            p_gt = p_gt.reshape(bsz, t, LANES)
            yc = _gdn(p_gd.reshape(bsz, t, GD_MAIN), p_gt, gd_conv[j], gd_a_log[j], gd_dt_bias[j], gd_norm[j])
            yd = _mlstm(p_ml.reshape(bsz, t, ML_MAIN), p_gt, ml_i_bias[j], ml_f_bias[j], ml_norm[j])
            h, h_tiles = _out_proj(h, yc.reshape(n, GD_DIM), yd.reshape(n, ML_DIM), od_w_out[j])
        h = _hmoe_residual(h, h_tiles, norm_ffn[layer], moe_w_group[layer], moe_b_group[layer], moe_w_router[layer],
                           moe_b_router[layer], moe_w_gate[layer], moe_w_up[layer], moe_w_down[layer],
                           norm_final, final_norm=(layer == depth - 1))
    return h.reshape(bsz, t, d)
```

```python
import functools
import math

import jax
import jax.numpy as jnp
from jax import lax
from jax.experimental import pallas as pl
from jax.experimental.pallas import tpu as pltpu

F32 = jnp.float32
BF16 = jnp.bfloat16
I32 = jnp.int32
HIGHEST = lax.Precision.HIGHEST

D_MODEL = 1024
NORM_EPS = 1e-6
RW_HEADS, RW_HEAD_DIM = 8, 64
RW_DIM = RW_HEADS * RW_HEAD_DIM
R_DECAY, R_AAA, R_GATE = 64, 64, 128
RW_IN = 3 * RW_DIM + R_DECAY + R_AAA + R_GATE
RW_LN_EPS = 64e-5
HG_HEADS, HG_DIM = 8, 64
HG_KDIM = HG_HEADS * HG_DIM
HG_IN = 4 * HG_KDIM
GD_HEADS, GD_HEAD_DIM = 4, 128
GD_DIM = GD_HEADS * GD_HEAD_DIM
CONV_K = 4
GD_MAIN = 4 * GD_DIM
GD_IN = GD_MAIN + 2 * GD_HEADS
ML_HEADS, ML_HEAD_DIM = 4, 128
ML_DIM = ML_HEADS * ML_HEAD_DIM
ML_MAIN = 4 * ML_DIM
N_GROUPS, EXPERTS_PER_GROUP = 4, 8
N_EXPERTS = N_GROUPS * EXPERTS_PER_GROUP
D_EXPERT = 256

LANES = 128
VMEM_LIMIT_BYTES = 48 * 1024 * 1024

PROJ_TM = 256
CHUNK = 64
SUB = 16
RW_NCH = 4
GD_NCH = 4
HG_ROWS = 128
ML_CHUNK = 128
MOE_TM = 256
GATHER_UNROLL = 8
NEG = -1e30


def _cparams(*sem):
    return pltpu.CompilerParams(dimension_semantics=sem, vmem_limit_bytes=VMEM_LIMIT_BYTES)


def _mm(a, b):
    return jnp.dot(a.astype(BF16), b.astype(BF16), preferred_element_type=F32)


def _mm_nt(a, b):
    return lax.dot_general(a.astype(BF16), b.astype(BF16), (((1,), (1,)), ((), ())), preferred_element_type=F32)


def _mm_tn(a, b):
    return lax.dot_general(a.astype(BF16), b.astype(BF16), (((0,), (0,)), ((), ())), preferred_element_type=F32)


def _mm_hi(a, b):
    return jnp.dot(a, b, precision=HIGHEST, preferred_element_type=F32)


def _mm_split(x, ones_bf16):
    hi = x.astype(BF16)
    lo = (x - hi.astype(F32)).astype(BF16)
    return (jnp.dot(hi, ones_bf16, preferred_element_type=F32) + jnp.dot(lo, ones_bf16, preferred_element_type=F32))


def _iota2(shape):
    return lax.broadcasted_iota(I32, shape, 0), lax.broadcasted_iota(I32, shape, 1)


def _rms_rows(x, g, eps=NORM_EPS):
    return x * lax.rsqrt(jnp.mean(x * x, axis=-1, keepdims=True) + eps) * g


def _seg_ones(width, seg):
    r, c = _iota2((width, width))
    return jnp.where((r // seg) == (c // seg), 1.0, 0.0).astype(BF16)


def _split3(x):
    x1 = x.astype(BF16)
    r1 = x - x1.astype(F32)
    x2 = r1.astype(BF16)
    return x1, x2, (r1 - x2.astype(F32)).astype(BF16)


def _cumsum_rows(tri_bf16, x):
    return sum(jnp.dot(tri_bf16, t, preferred_element_type=F32) for t in _split3(x))


def _cumsum_cols(x, tri_bf16):
    return sum(jnp.dot(t, tri_bf16, preferred_element_type=F32) for t in _split3(x))


def _mm3(a, b):
    ah = a.astype(BF16)
    al = (a - ah.astype(F32)).astype(BF16)
    bh = b.astype(BF16)
    bl = (b - bh.astype(F32)).astype(BF16)
    dot = lambda x, y: jnp.dot(x, y, preferred_element_type=F32)
    return dot(ah, bh) + dot(ah, bl) + dot(al, bh)


def _tri_inv_multi(ms, n, chain):
    assert chain // SUB <= 4
    r, c = _iota2((n, n))
    same = (r // SUB) == (c // SUB)
    eye = jnp.where(r == c, 1.0, 0.0).astype(F32)
    ds = [jnp.where(same, m, 0.0) for m in ms]
    offs = [m - d for m, d in zip(ms, ds)]
    xs = [eye - d for d in ds]
    ps = ds
    for _ in range(3):
        ps = [_mm(p, p) for p in ps]
        xs = [x + _mm(x, p) for x, p in zip(xs, ps)]
    es = [_mm(x, o) for x, o in zip(xs, offs)]
    imes = [eye - e for e in es]
    e2s = [_mm(e, e) for e in es]
    ys = [i + _mm(i, e2) for i, e2 in zip(imes, e2s)]
    xs = [_mm(y, x) for y, x in zip(ys, xs)]
    res = [eye - x - _mm3(m, x) for m, x in zip(ms, xs)]
    return [x + _mm(x, rr) for x, rr in zip(xs, res)]


def _rms_proj_kernel(x_ref, g_ref, *refs, n_out):
    y = _rms_rows(x_ref[...], g_ref[...]).astype(BF16)
    for w_ref, o_ref in zip(refs[:n_out], refs[n_out:]):
        o_ref[...] = jnp.dot(y, w_ref[...], preferred_element_type=F32)


def _rms_proj(x, g, ws):
    n, d = x.shape
    tm = min(PROJ_TM, n)
    in_specs = [pl.BlockSpec((tm, d), lambda i: (i, 0)), pl.BlockSpec((1, d), lambda i: (0, 0))]
    in_specs += [pl.BlockSpec(w.shape, lambda i: (0, 0)) for w in ws]
    return pl.pallas_call(
        functools.partial(_rms_proj_kernel, n_out=len(ws)),
        grid=(n // tm,),
        in_specs=in_specs,
        out_specs=[pl.BlockSpec((tm, w.shape[1]), lambda i: (i, 0)) for w in ws],
        out_shape=[jax.ShapeDtypeStruct((n, w.shape[1]), F32) for w in ws],
        compiler_params=_cparams("parallel"),
        name="rms_proj",
    )(x, g.reshape(1, d), *ws)


ROW_TILE = D_MODEL // LANES


def _store_row_tiles(ref, x):
    for j in range(ROW_TILE):
        ref[pl.ds(j, x.shape[0], stride=ROW_TILE), :] = x[:, j * LANES:(j + 1) * LANES]


def _load_row_tiles(ref, rows):
    return jnp.concatenate([ref[pl.ds(j, rows, stride=ROW_TILE), :] for j in range(ROW_TILE)], axis=1)


def _rwkv7_kernel(p_ref, mu_ref, w0_ref, w2_ref, a0_ref, a2_ref, g2_ref, kk_ref, ka_ref, rk_ref,
                  lnw_ref, lnb_ref, o_ref, prev_ref, zt_ref):
    L = CHUNK
    npair = RW_HEADS // 2
    c = pl.program_id(1)

    @pl.when(c == 0)
    def _():
        prev_ref[...] = jnp.zeros_like(prev_ref)
        zt_ref[...] = jnp.zeros_like(zt_ref)

    x = p_ref[0]
    R = x.shape[0]
    nch = R // L
    row = lax.broadcasted_iota(I32, x.shape, 0)
    xs = jnp.where(row == 0, prev_ref[7:8, :], pltpu.roll(x, 1, 0))
    prev_ref[...] = x[R - 8:R, :]
    pm = x + mu_ref[...] * (xs - x)
    r_all = pm[:, 0:RW_DIM]
    k_all = pm[:, RW_DIM:2 * RW_DIM]
    v_all = pm[:, 2 * RW_DIM:3 * RW_DIM]
    wa = pm[:, 3 * RW_DIM:3 * RW_DIM + LANES]
    gl = pm[:, 3 * RW_DIM + LANES:]
    wlog = -jax.nn.softplus(-(w0_ref[...] + _mm(jnp.tanh(wa), w2_ref[...]))) - 0.5
    ld = -jnp.exp(wlog)
    a_all = jax.nn.sigmoid(a0_ref[...] + _mm(wa, a2_ref[...]))
    g_all = _mm(jax.nn.sigmoid(gl), g2_ref[...])

    tr, tc = _iota2((R, R))
    chunk_tril = jnp.where((tc <= tr) & ((tr // L) == (tc // L)), 1.0, 0.0).astype(BF16)
    cs_all = _cumsum_rows(chunk_tril, ld)
    seg = _seg_ones(LANES, RW_HEAD_DIM)
    lane = lax.broadcasted_iota(I32, (L, LANES), 1)
    hm = (lane < RW_HEAD_DIM, lane >= RW_HEAD_DIM)
    br, bc = _iota2((2 * L, 2 * L))
    bd = (br // L) == (bc // L)
    bd_strict = bd & (bc < br)
    bd_incl = bd & (bc <= br)
    fold = lambda z: z[0:L] + z[L:2 * L]
    both = lambda z: jnp.concatenate([jnp.where(hm[0], z, 0.0), jnp.where(hm[1], z, 0.0)], axis=0)

    units = []
    for ci in range(nch):
        rs = slice(ci * L, (ci + 1) * L)
        for p in range(npair):
            sl = slice(p * LANES, (p + 1) * LANES)
            r, k, v, a = r_all[rs, sl], k_all[rs, sl], v_all[rs, sl], a_all[rs, sl]
            cs, ldp = cs_all[rs, sl], ld[rs, sl]
            kkr = k * kk_ref[:, sl]
            kk = kkr * lax.rsqrt(_mm_split(kkr * kkr, seg) + 1e-6)
            k2 = k * (1.0 + (a - 1.0) * ka_ref[:, sl])
            b = kk * a
            cs_last = cs[L - 1:L, :]
            e_neg = jnp.exp(-cs)
            e_rem = jnp.exp(cs_last - cs)
            bhat = b * e_neg
            khat = k2 * e_neg
            units.append(dict(p=p, rs=rs, sl=sl, r=r, v=v, k2=k2, rhat=r * jnp.exp(cs), gam_last=jnp.exp(cs_last),
                              btil=b * e_rem, ktil=k2 * e_rem, a2=both(kk * jnp.exp(cs - ldp)), v2=both(v),
                              rhs4=jnp.concatenate([bhat, bhat, khat, khat], axis=0)))
    for q in units:
        lhs = jnp.concatenate([q["a2"], both(q["rhat"])], axis=0)
        q["g"] = _mm_nt(lhs, q["rhs4"])
    tinvs = _tri_inv_multi([jnp.where(bd_strict, q["g"][0:2 * L, 0:2 * L], 0.0) for q in units], 2 * L, L)
    for q, tinv in zip(units, tinvs):
        q["tinv"] = tinv
        q["x2"] = _mm(jnp.where(bd_strict, q["g"][0:2 * L, 2 * L:4 * L], 0.0), q["v2"])
    for q in units:
        uw = _mm(q["tinv"], jnp.concatenate([q["x2"], q["a2"]], axis=1))
        q["u0"] = -fold(uw[:, 0:LANES])
        q["w"] = fold(uw[:, LANES:])
        q["y0"] = fold(_mm(jnp.where(bd_incl, q["g"][2 * L:4 * L, 2 * L:4 * L], 0.0), q["v2"]))
    for q in units:
        rb = jnp.where(bd_incl, q["g"][2 * L:4 * L, 0:2 * L], 0.0)
        ruw = _mm(rb, jnp.concatenate([both(q["u0"]), both(q["w"])], axis=1))
        q["yc"] = q["y0"] + fold(ruw[:, 0:LANES])
        q["ry"] = q["rhat"] - fold(ruw[:, LANES:])
        q["c1"] = _mm_tn(jnp.concatenate([q["u0"], q["v"]], axis=0), jnp.concatenate([q["btil"], q["ktil"]], axis=0))
        q["c2"] = _mm_tn(q["w"], q["btil"])

    hr, hc = _iota2((LANES, LANES))
    head_bd = (hr // RW_HEAD_DIM) == (hc // RW_HEAD_DIM)
    zts = [zt_ref[p] for p in range(npair)]
    for ci in range(nch):
        qs = units[ci * npair:(ci + 1) * npair]
        for q, zt in zip(qs, zts):
            q["y"] = q["yc"] + _mm_nt(q["ry"], zt)
        zts = [zt * q["gam_last"] + jnp.where(head_bd, q["c1"] - _mm(zt, q["c2"]), 0.0) for q, zt in zip(qs, zts)]
    for p in range(npair):
        zt_ref[p] = zts[p]

    for q in units:
        sl, rs, y = q["sl"], q["rs"], q["y"]
        mean = _mm_split(y, seg) * (1.0 / RW_HEAD_DIM)
        yc = y - mean
        var = _mm_split(yc * yc, seg) * (1.0 / RW_HEAD_DIM)
        yn = yc * lax.rsqrt(var + RW_LN_EPS) * lnw_ref[:, sl] + lnb_ref[:, sl]
        bonus = _mm_split(q["r"] * q["k2"] * rk_ref[:, sl], seg) * q["v"]
        o_ref[0, rs, sl] = (yn + bonus) * g_all[rs, sl]


def _rwkv7(p, mu, w0, w2, a0, a2, g2, k_k, k_a, r_k, ln_w, ln_b):
    bsz, t, _ = p.shape
    row = lambda z: z.reshape(1, -1).astype(F32)
    w2p = jnp.concatenate([w2, jnp.zeros_like(w2)], axis=0).astype(BF16)
    a2p = jnp.concatenate([jnp.zeros_like(a2), a2], axis=0).astype(BF16)
    params = [row(mu), row(w0), w2p, row(a0), a2p, g2.astype(BF16), row(k_k), row(k_a), row(r_k), row(ln_w), row(ln_b)]
    full = lambda z: pl.BlockSpec(z.shape, lambda b, c: (0, 0))
    rows = RW_NCH * CHUNK
    return pl.pallas_call(
        _rwkv7_kernel,
        grid=(bsz, t // rows),
        in_specs=[pl.BlockSpec((1, rows, RW_IN), lambda b, c: (b, c, 0))] + [full(z) for z in params],
        out_specs=pl.BlockSpec((1, rows, RW_DIM), lambda b, c: (b, c, 0)),
        out_shape=jax.ShapeDtypeStruct((bsz, t, RW_DIM), F32),
        scratch_shapes=[pltpu.VMEM((8, RW_IN), F32), pltpu.VMEM((RW_HEADS // 2, LANES, LANES), F32)],
        compiler_params=_cparams("arbitrary", "arbitrary"),
        name="rwkv7_mix",
    )(p, *params)


def _hgrn2_kernel(p_ref, lb_ref, ng_ref, o_ref, st_ref):
    L = HG_ROWS
    c = pl.program_id(1)

    @pl.when(c == 0)
    def _():
        st_ref[...] = jnp.zeros_like(st_ref)

    x = p_ref[0]
    lb = lb_ref[...]
    q_all = jax.nn.silu(x[:, 0:HG_KDIM])
    fg = lb + (1.0 - lb) * jax.nn.sigmoid(x[:, HG_KDIM:2 * HG_KDIM])
    k_all = 1.0 - fg
    logf = jnp.log(fg)
    v_all = x[:, 2 * HG_KDIM:3 * HG_KDIM]
    gate = x[:, 3 * HG_KDIM:]
    tr, tc = _iota2((L, L))
    blk_tril = jnp.where((tc <= tr) & ((tr // SUB) == (tc // SUB)), 1.0, 0.0).astype(BF16)
    bc_all = _cumsum_rows(blk_tril, logf)
    seg = _seg_ones(LANES, HG_DIM)
    br, bcc = _iota2((LANES, LANES))
    bd = (br // HG_DIM) == (bcc // HG_DIM)
    half = SUB // 2
    t_lo = lax.broadcasted_iota(I32, (SUB, LANES), 0)
    t_hi = lax.broadcasted_iota(I32, (half, LANES), 0) + half

    nsub = L // SUB
    npair = HG_HEADS // 2
    rows_per_unit = half * SUB + half * half
    sub_of_row = lax.broadcasted_iota(I32, (L, LANES), 0) // SUB

    units = []
    parts = []
    for p in range(npair):
        sl = slice(p * LANES, (p + 1) * LANES)
        for j in range(nsub):
            rs = slice(j * SUB, (j + 1) * SUB)
            q, k, v, bc = q_all[rs, sl], k_all[rs, sl], v_all[rs, sl], bc_all[rs, sl]
            q_hi, bc_hi = q[half:], bc[half:]
            for s in range(SUB):
                if s < half:
                    diff = jnp.where(t_lo >= s, bc - bc[s:s + 1, :], NEG)
                    parts.append(jnp.exp(diff) * q * k[s:s + 1, :])
                else:
                    diff = jnp.where(t_hi >= s, bc_hi - bc[s:s + 1, :], NEG)
                    parts.append(jnp.exp(diff) * q_hi * k[s:s + 1, :])
            bend = bc[SUB - 1:SUB, :]
            units.append(dict(v=v, qt=q * jnp.exp(bc), dec=jnp.exp(bend), kd=k * jnp.exp(bend - bc)))
    score_all = _mm(jnp.concatenate(parts, axis=0), seg)
    for ui, un in enumerate(units):
        score = score_all[ui * rows_per_unit:(ui + 1) * rows_per_unit]
        v = un["v"]
        acc_lo = jnp.zeros((half, LANES), F32)
        acc_hi = jnp.zeros((half, LANES), F32)
        off = 0
        for s in range(SUB):
            vs = v[s:s + 1, :]
            if s < half:
                acc_lo = acc_lo + score[off:off + half] * vs
                acc_hi = acc_hi + score[off + half:off + SUB] * vs
                off += SUB
            else:
                acc_hi = acc_hi + score[off:off + half] * vs
                off += half
        un["intra"] = jnp.concatenate([acc_lo, acc_hi], axis=0)
    for p in range(npair):
        us = units[p * nsub:(p + 1) * nsub]
        v_pair = jnp.concatenate([un["v"] for un in us], axis=0)
        kd_pair = jnp.concatenate([un["kd"] for un in us], axis=0)
        kd_wide = jnp.concatenate([jnp.where(sub_of_row == j, kd_pair, 0.0) for j in range(nsub)], axis=1)
        upd = _mm_tn(v_pair, kd_wide)
        for j, un in enumerate(us):
            un["upd"] = jnp.where(bd, upd[:, j * LANES:(j + 1) * LANES], 0.0)

    for p in range(npair):
        sl = slice(p * LANES, (p + 1) * LANES)
        st = st_ref[p]
        outs = []
        for un in units[p * nsub:(p + 1) * nsub]:
            outs.append(un["intra"] + _mm_nt(un["qt"], st))
            st = st * un["dec"] + un["upd"]
        st_ref[p] = st
        o = jnp.concatenate(outs, axis=0)
        ms = _mm_split(o * o, seg) * (1.0 / HG_DIM)
        o_ref[0, :, sl] = o * lax.rsqrt(ms + NORM_EPS) * ng_ref[:, sl] * jax.nn.silu(gate[:, sl])


def _hgrn2(p, lb, norm_g):
    bsz, t, _ = p.shape
    lb = lb.reshape(1, HG_KDIM).astype(F32)
    ng = jnp.tile(norm_g.astype(F32), HG_HEADS).reshape(1, HG_KDIM)
    return pl.pallas_call(
        _hgrn2_kernel,
        grid=(bsz, t // HG_ROWS),
        in_specs=[pl.BlockSpec((1, HG_ROWS, HG_IN), lambda b, c: (b, c, 0)),
                  pl.BlockSpec((1, HG_KDIM), lambda b, c: (0, 0)), pl.BlockSpec((1, HG_KDIM), lambda b, c: (0, 0))],
        out_specs=pl.BlockSpec((1, HG_ROWS, HG_KDIM), lambda b, c: (b, c, 0)),
        out_shape=jax.ShapeDtypeStruct((bsz, t, HG_KDIM), F32),
        scratch_shapes=[pltpu.VMEM((HG_HEADS // 2, LANES, LANES), F32)],
        compiler_params=_cparams("arbitrary", "arbitrary"),
        name="hgrn2_mix",
    )(p, lb, ng)


def _gdn_kernel(p_ref, pg_ref, cw_ref, gb_ref, nal_ref, nalc_ref, ng_ref, o_ref, prev_ref, s_ref):
    L = CHUNK
    c = pl.program_id(1)

    @pl.when(c == 0)
    def _():
        prev_ref[...] = jnp.zeros_like(prev_ref)
        s_ref[...] = jnp.zeros_like(s_ref)

    x = p_ref[0]
    R = x.shape[0]
    nch = R // L
    xq = x[:, 0:3 * GD_DIM]
    xcat = jnp.concatenate([prev_ref[...], xq], axis=0)
    conv = xq * cw_ref[CONV_K - 1:CONV_K, :]
    for j in range(1, CONV_K):
        conv = conv + xcat[8 - j:8 - j + R, :] * cw_ref[CONV_K - 1 - j:CONV_K - j, :]
    prev_ref[...] = xq[R - 8:R, :]
    qkv = jax.nn.silu(conv)
    z = x[:, 3 * GD_DIM:]

    gt = pg_ref[0] + gb_ref[...]
    beta_all = jax.nn.sigmoid(gt)
    g_all = nal_ref[...] * jax.nn.softplus(gt)
    g_t = nalc_ref[...] * jax.nn.softplus(gt.T)
    tr, tc = _iota2((R, R))
    chunk_tril = jnp.where((tc <= tr) & ((tr // L) == (tc // L)), 1.0, 0.0).astype(BF16)
    gam_all = _cumsum_rows(chunk_tril, g_all)
    ur, uc = _iota2((R, 2 * R))
    triu2 = jnp.where(((ur // L) == (uc // (2 * L))) & ((ur % L) <= (uc % L)), 1.0, 0.0).astype(BF16)
    gam_t2 = _cumsum_cols(g_t, triu2)
    lr, lc = _iota2((L, L))
    incl = lc <= lr
    br, bc = _iota2((2 * L, 2 * L))
    bd = (br // L) == (bc // L)
    bd_strict = bd & (bc < br)
    bd_incl = bd & (bc <= br)
    lane2 = lax.broadcasted_iota(I32, (1, 2 * L), 1)
    zero = jnp.zeros((L, LANES), F32)

    units = []
    for ci in range(nch):
        rs = slice(ci * L, (ci + 1) * L)
        for h in range(GD_HEADS):
            q = qkv[rs, h * LANES:(h + 1) * LANES]
            k = qkv[rs, GD_DIM + h * LANES:GD_DIM + (h + 1) * LANES]
            v = qkv[rs, 2 * GD_DIM + h * LANES:2 * GD_DIM + (h + 1) * LANES]
            q = q * lax.rsqrt(jnp.sum(q * q, axis=-1, keepdims=True) + 1e-6) * (GD_HEAD_DIM ** -0.5)
            k = k * lax.rsqrt(jnp.sum(k * k, axis=-1, keepdims=True) + 1e-6)
            beta = beta_all[rs, h:h + 1]
            gam = gam_all[rs, GD_HEADS + h:GD_HEADS + h + 1]
            gam_row2 = gam_t2[GD_HEADS + h:GD_HEADS + h + 1, ci * 2 * L:(ci + 1) * 2 * L]
            gam_last = gam[L - 1:L, :]
            kb = k * beta
            units.append(dict(h=h, rs=rs, q=q, k=k, kb=kb, vb=v * beta, gam=gam, gam_row2=gam_row2,
                              gam_last=gam_last, kg=kb * jnp.exp(gam), qg=q * jnp.exp(gam),
                              kd=k * jnp.exp(gam_last - gam),
                              decay=jnp.exp(jnp.where(incl, gam - gam_row2[:, 0:L], NEG))))
    pairs = [(units[i], units[i + 1]) for i in range(0, len(units), 2)]
    ms = []
    for h0, h1 in pairs:
        lhs = jnp.concatenate([jnp.concatenate([h0["kb"], zero], axis=1),
                               jnp.concatenate([zero, h1["kb"]], axis=1)], axis=0)
        rhs = jnp.concatenate([jnp.concatenate([h0["k"], zero], axis=1),
                               jnp.concatenate([zero, h1["k"]], axis=1)], axis=0)
        gam_col = jnp.concatenate([h0["gam"], h1["gam"]], axis=0)
        gam_row = jnp.where(lane2 < L, h0["gam_row2"], h1["gam_row2"])
        decay2 = jnp.exp(jnp.where(bd_incl, gam_col - gam_row, NEG))
        ms.append(jnp.where(bd_strict, _mm_nt(lhs, rhs) * decay2, 0.0))
    tinvs = _tri_inv_multi(ms, 2 * L, L)
    for (h0, h1), tinv in zip(pairs, tinvs):
        rhs = jnp.concatenate([jnp.concatenate([h0["vb"], h0["kg"]], axis=1),
                               jnp.concatenate([h1["vb"], h1["kg"]], axis=1)], axis=0)
        uw = _mm(tinv, rhs)
        h0["u"], h0["w"] = uw[0:L, 0:LANES], uw[0:L, LANES:]
        h1["u"], h1["w"] = uw[L:2 * L, 0:LANES], uw[L:2 * L, LANES:]
    for hd in units:
        attn = _mm_nt(hd["q"], hd["k"]) * hd["decay"]
        uw = jnp.concatenate([hd["u"], hd["w"]], axis=1)
        auw = _mm(attn, uw)
        hd["o0"] = auw[:, 0:LANES]
        hd["qs"] = hd["qg"] - auw[:, LANES:]
        cc = _mm_tn(hd["kd"], uw)
        hd["c1"], hd["c2"] = cc[:, 0:LANES], cc[:, LANES:]

    ss = [s_ref[h] for h in range(GD_HEADS)]
    for ci in range(nch):
        hds = units[ci * GD_HEADS:(ci + 1) * GD_HEADS]
        for hd, s in zip(hds, ss):
            hd["o"] = hd["o0"] + _mm(hd["qs"], s)
        ss = [s * jnp.exp(hd["gam_last"]) + hd["c1"] - _mm(hd["c2"], s) for hd, s in zip(hds, ss)]
    for h in range(GD_HEADS):
        s_ref[h] = ss[h]
    for hd in units:
        sl = slice(hd["h"] * LANES, (hd["h"] + 1) * LANES)
        o_ref[0, hd["rs"], sl] = _rms_rows(hd["o"], ng_ref[...]) * jax.nn.silu(z[hd["rs"], sl])


def _gate_row(lo, vals):
    return jnp.zeros((1, LANES), F32).at[0, lo:lo + vals.shape[0]].set(vals.astype(F32))


def _gdn(p, pg, conv_w, a_log, dt_bias, norm_g):
    bsz, t, _ = p.shape
    gbias = _gate_row(GD_HEADS, dt_bias)
    nal = _gate_row(GD_HEADS, -jnp.exp(a_log))
    nal_col = nal.reshape(LANES, 1)
    ng = norm_g.reshape(1, GD_HEAD_DIM).astype(F32)
    full = lambda z: pl.BlockSpec(z.shape, lambda b, c: (0, 0))
    rows = GD_NCH * CHUNK
    return pl.pallas_call(
        _gdn_kernel,
        grid=(bsz, t // rows),
        in_specs=[pl.BlockSpec((1, rows, GD_MAIN), lambda b, c: (b, c, 0)),
                  pl.BlockSpec((1, rows, LANES), lambda b, c: (b, c, 0)),
                  full(conv_w), full(gbias), full(nal), full(nal_col), full(ng)],
        out_specs=pl.BlockSpec((1, rows, GD_DIM), lambda b, c: (b, c, 0)),
        out_shape=jax.ShapeDtypeStruct((bsz, t, GD_DIM), F32),
        scratch_shapes=[pltpu.VMEM((8, 3 * GD_DIM), F32), pltpu.VMEM((GD_HEADS, LANES, LANES), F32)],
        compiler_params=_cparams("arbitrary", "arbitrary"),
        name="gdn_mix",
    )(p, pg, conv_w.astype(F32), gbias, nal, nal_col, ng)


def _mlstm_kernel(p_ref, pg_ref, gb_ref, ng_ref, o_ref, c_ref, n_ref, m_ref):
    L = ML_CHUNK
    ci = pl.program_id(1)

    @pl.when(ci == 0)
    def _():
        c_ref[...] = jnp.zeros_like(c_ref)
        n_ref[...] = jnp.zeros_like(n_ref)
        m_ref[...] = jnp.zeros_like(m_ref)

    x = p_ref[0]
    gt = pg_ref[0] + gb_ref[...]
    logf = jax.nn.log_sigmoid(gt)
    gt_t = gt.T
    logf_t = jax.nn.log_sigmoid(gt_t)
    tr, tc = _iota2((L, L))
    bc_all = _cumsum_rows(jnp.where(tc <= tr, 1.0, 0.0).astype(BF16), logf)
    bc_t = _cumsum_cols(logf_t, jnp.where(tr <= tc, 1.0, 0.0).astype(BF16))
    incl = tc <= tr
    i_lo, f_lo = 2 * GD_HEADS, 2 * GD_HEADS + ML_HEADS

    for h in range(ML_HEADS):
        sl = slice(h * LANES, (h + 1) * LANES)
        q = x[:, sl]
        k = x[:, ML_DIM + h * LANES:ML_DIM + (h + 1) * LANES] * (ML_HEAD_DIM ** -0.5)
        v = x[:, 2 * ML_DIM + h * LANES:2 * ML_DIM + (h + 1) * LANES]
        og = jax.nn.sigmoid(x[:, 3 * ML_DIM + h * LANES:3 * ML_DIM + (h + 1) * LANES])
        bc = bc_all[:, f_lo + h:f_lo + h + 1]
        bc_row = bc_t[f_lo + h:f_lo + h + 1, :]
        ig = gt[:, i_lo + h:i_lo + h + 1]
        ig_row = gt_t[i_lo + h:i_lo + h + 1, :]
        m_prev = m_ref[h][0:1, 0:1]
        b_last = bc[L - 1:L, :]

        d_log = jnp.where(incl, bc - bc_row + ig_row, NEG)
        inter_log = bc + m_prev
        m_t = jnp.maximum(inter_log, jnp.max(d_log, axis=-1, keepdims=True))
        d_w = jnp.exp(d_log - m_t)
        inter_w = jnp.exp(inter_log - m_t)
        sc = _mm_nt(q, k) * d_w
        cmat = c_ref[h]
        nvec = n_ref[h][0:1, :]
        num = inter_w * _mm(q, cmat) + _mm(sc, v)
        den = inter_w * jnp.sum(q * nvec, axis=-1, keepdims=True) + jnp.sum(sc, axis=-1, keepdims=True)
        hh = num / jnp.maximum(jnp.abs(den), jnp.exp(-m_t))

        upd_log = b_last - bc + ig
        m_new = jnp.maximum(b_last + m_prev, jnp.max(upd_log, axis=0, keepdims=True))
        w_s = jnp.exp(upd_log - m_new)
        dec = jnp.exp(b_last + m_prev - m_new)
        wk = w_s * k
        c_ref[h] = dec * cmat + _mm_tn(wk, v)
        n_ref[h] = jnp.broadcast_to(dec * nvec + jnp.sum(wk, axis=0, keepdims=True), (8, LANES))
        m_ref[h] = jnp.broadcast_to(m_new, (8, LANES))

        o_ref[0, :, sl] = og * _rms_rows(hh, ng_ref[:, sl])


def _mlstm(p, pg, i_bias, f_bias, norm_g):
    bsz, t, _ = p.shape
    gbias = _gate_row(2 * GD_HEADS, jnp.concatenate([i_bias, f_bias]))
    ng = norm_g.reshape(1, ML_DIM).astype(F32)
    lc = min(ML_CHUNK, t)
    full = lambda z: pl.BlockSpec(z.shape, lambda b, c: (0, 0))
    return pl.pallas_call(
        _mlstm_kernel,
        grid=(bsz, t // lc),
        in_specs=[pl.BlockSpec((1, lc, ML_MAIN), lambda b, c: (b, c, 0)),
                  pl.BlockSpec((1, lc, LANES), lambda b, c: (b, c, 0)), full(gbias), full(ng)],
        out_specs=pl.BlockSpec((1, lc, ML_DIM), lambda b, c: (b, c, 0)),
        out_shape=jax.ShapeDtypeStruct((bsz, t, ML_DIM), F32),
        scratch_shapes=[pltpu.VMEM((ML_HEADS, LANES, LANES), F32), pltpu.VMEM((ML_HEADS, 8, LANES), F32),
                        pltpu.VMEM((ML_HEADS, 8, LANES), F32)],
        compiler_params=_cparams("arbitrary", "arbitrary"),
        name="mlstm_mix",
    )(p, pg, gbias, ng)


def _route(h, g_ref, wh_ref, wl_ref, b_ref, ri_ref, rg_ref, cnt_ref, off_ref):
    tm = h.shape[0]

    @pl.when(pl.program_id(0) == 0)
    def _():
        off_ref[...] = jnp.zeros_like(off_ref)

    xn = _rms_rows(h, g_ref[...])
    xh = xn.astype(BF16)
    xl = (xn - xh.astype(F32)).astype(BF16)
    dot = lambda a, b: jnp.dot(a, b, preferred_element_type=F32)
    logits = dot(xh, wh_ref[...]) + dot(xh, wl_ref[...]) + dot(xl, wh_ref[...]) + b_ref[...]
    lane = lax.broadcasted_iota(I32, logits.shape, 1)
    big = jnp.int32(1 << 20)
    is_grp = (lane >= N_EXPERTS) & (lane < N_EXPERTS + N_GROUPS)
    lg = jnp.where(is_grp, logits, NEG)
    gmax = jnp.max(lg, axis=-1, keepdims=True)
    p_top = 1.0 / jnp.sum(jnp.exp(lg - gmax), axis=-1, keepdims=True)
    g_idx = jnp.min(jnp.where(lg == gmax, lane, big), axis=-1, keepdims=True) - N_EXPERTS
    valid = (lane < N_EXPERTS) & ((lane // EXPERTS_PER_GROUP) == g_idx)
    v1 = jnp.where(valid, logits, NEG)
    m1 = jnp.max(v1, axis=-1, keepdims=True)
    i1 = jnp.min(jnp.where(v1 == m1, lane, big), axis=-1, keepdims=True)
    v2 = jnp.where(lane == i1, NEG, v1)
    m2 = jnp.max(v2, axis=-1, keepdims=True)
    i2 = jnp.min(jnp.where(v2 == m2, lane, big), axis=-1, keepdims=True)
    e21 = jnp.exp(m2 - m1)
    gate1 = p_top / (1.0 + e21)
    gate2 = p_top * e21 / (1.0 + e21)

    sel1 = lane == i1
    sel2 = lane == i2
    onehot = jnp.where(sel1 | sel2, 1.0, 0.0)
    tr, tc = _iota2((tm, tm))
    before = jnp.where(tc < tr, 1.0, 0.0).astype(BF16)
    prefix = jnp.dot(before, onehot.astype(BF16), preferred_element_type=F32) + off_ref[0:1, :]
    rank1 = jnp.sum(jnp.where(sel1, prefix, 0.0), axis=-1, keepdims=True).astype(I32)
    rank2 = jnp.sum(jnp.where(sel2, prefix, 0.0), axis=-1, keepdims=True).astype(I32)
    total = off_ref[0:1, :] + jnp.sum(onehot, axis=0, keepdims=True)
    off_ref[...] = jnp.broadcast_to(total, off_ref.shape)
    cnt_ref[...] = jnp.broadcast_to(total, cnt_ref.shape).astype(I32)

    ri_ref[...] = jnp.where(lane == 0, i1, jnp.where(lane == 1, i2, jnp.where(lane == 2, rank1, jnp.where(lane == 3, rank2, 0))))
    rg_ref[...] = jnp.where(lane == 0, gate1, jnp.where(lane == 1, gate2, 0.0))


def _out_proj_route_kernel(h_ref, ya_ref, yb_ref, wa_ref, wb_ref, g_ref, wh_ref, wl_ref, b_ref,
                           o_ref, ot_ref, ri_ref, rg_ref, cnt_ref, off_ref):
    out = h_ref[...] + _mm(ya_ref[...], wa_ref[...]) + _mm(yb_ref[...], wb_ref[...])
    o_ref[...] = out
    _store_row_tiles(ot_ref, out)
    _route(out, g_ref, wh_ref, wl_ref, b_ref, ri_ref, rg_ref, cnt_ref, off_ref)


def _out_proj_route(h, ya, yb, w_out, g_ffn, w_group, b_group, w_router, b_router):
    n, d = h.shape
    da, db = ya.shape[1], yb.shape[1]
    tm = min(PROJ_TM, n)
    wa = w_out[:da].astype(BF16)
    wb = w_out[da:].astype(BF16)
    pad = LANES - N_EXPERTS - N_GROUPS
    w_cat = jnp.concatenate([w_router, w_group, jnp.zeros((d, pad), F32)], axis=1)
    b_cat = jnp.concatenate([b_router, b_group, jnp.zeros((pad,), F32)]).reshape(1, LANES)
    w_hi = w_cat.astype(BF16)
    w_lo = (w_cat - w_hi.astype(F32)).astype(BF16)
    row = lambda i: (i, 0)
    fixed = lambda i: (0, 0)
    return pl.pallas_call(
        _out_proj_route_kernel,
        grid=(n // tm,),
        in_specs=[pl.BlockSpec((tm, d), row), pl.BlockSpec((tm, da), row), pl.BlockSpec((tm, db), row),
                  pl.BlockSpec((da, d), fixed), pl.BlockSpec((db, d), fixed), pl.BlockSpec((1, d), fixed),
                  pl.BlockSpec((d, LANES), fixed), pl.BlockSpec((d, LANES), fixed), pl.BlockSpec((1, LANES), fixed)],
        out_specs=[pl.BlockSpec((tm, d), row), pl.BlockSpec((tm * ROW_TILE, LANES), row),
                   pl.BlockSpec((tm, LANES), row), pl.BlockSpec((tm, LANES), row), pl.BlockSpec((8, LANES), fixed)],
        out_shape=[jax.ShapeDtypeStruct((n, d), F32), jax.ShapeDtypeStruct((n * ROW_TILE, LANES), F32),
                   jax.ShapeDtypeStruct((n, LANES), I32), jax.ShapeDtypeStruct((n, LANES), F32),
                   jax.ShapeDtypeStruct((8, LANES), I32)],
        scratch_shapes=[pltpu.VMEM((8, LANES), F32)],
        compiler_params=_cparams("arbitrary"),
        name="out_proj_route",
    )(h, ya, yb, wa, wb, g_ffn.reshape(1, d), w_hi, w_lo, b_cat)


def _dest_kernel(ri_ref, ss_ref, o_ref):
    ri = ri_ref[...]
    lane = lax.broadcasted_iota(I32, ri.shape, 1)
    ss = ss_ref[...].astype(F32)
    pick = lambda e: jnp.sum(jnp.where(lane == e, ss, 0.0), axis=-1, keepdims=True)
    d1 = pick(ri[:, 0:1]) + ri[:, 2:3].astype(F32)
    d2 = pick(ri[:, 1:2]) + ri[:, 3:4].astype(F32)
    both = jnp.where(lane == 0, d1, jnp.where(lane == 1, d2, 0.0))
    o_ref[...] = both.T[0:8, :].astype(I32)


def _dest(ri, seg_start):
    n = ri.shape[0]
    tm = min(PROJ_TM, n)
    ss = jnp.zeros((1, LANES), I32).at[0, :N_EXPERTS].set(seg_start)
    return pl.pallas_call(
        _dest_kernel,
        grid=(n // tm,),
        in_specs=[pl.BlockSpec((tm, LANES), lambda i: (i, 0)), pl.BlockSpec((1, LANES), lambda i: (0, 0))],
        out_specs=pl.BlockSpec((8, tm), lambda i: (0, i)),
        out_shape=jax.ShapeDtypeStruct((8, n), I32),
        compiler_params=_cparams("parallel"),
        name="moe_dest",
    )(ri, ss)


def _invert_kernel(d1_ref, d2_ref, o_ref):
    def clear(i, carry):
        o_ref[i] = 0
        return carry

    def fill(t, carry):
        o_ref[d1_ref[t]] = t
        o_ref[d2_ref[t]] = t
        return carry

    lax.fori_loop(0, o_ref.shape[0], clear, 0, unroll=GATHER_UNROLL)
    lax.fori_loop(0, d1_ref.shape[0], fill, 0, unroll=GATHER_UNROLL)


def _invert(d1, d2, n_rows):
    grid_spec = pltpu.PrefetchScalarGridSpec(
        num_scalar_prefetch=2, grid=(1,), in_specs=[],
        out_specs=pl.BlockSpec(memory_space=pltpu.SMEM))
    return pl.pallas_call(
        _invert_kernel,
        grid_spec=grid_spec,
        out_shape=jax.ShapeDtypeStruct((n_rows,), I32),
        compiler_params=_cparams("arbitrary"),
        name="moe_invert",
    )(d1, d2)


def _gather_rows(src_hbm, idx_ref, base, dst, sem):
    def body(g, carry):
        for u in range(GATHER_UNROLL):
            j = g * GATHER_UNROLL + u
            r = pl.multiple_of(idx_ref[base + j] * ROW_TILE, ROW_TILE)
            pltpu.make_async_copy(src_hbm.at[pl.ds(r, ROW_TILE), :],
                                  dst.at[pl.ds(pl.multiple_of(j * ROW_TILE, ROW_TILE), ROW_TILE), :],
                                  sem).start(priority=u % 2)
        return carry

    lax.fori_loop(0, dst.shape[0] // (ROW_TILE * GATHER_UNROLL), body, 0)


def _wait_rows(src_hbm, dst, sem):
    pltpu.make_async_copy(src_hbm.at[pl.ds(0, dst.shape[0]), :], dst, sem).wait()


def _expert_kernel(te_ref, nt_ref, src_ref, h_hbm, g_ref, wg_ref, wu_ref, wd_ref, o_ref, xbuf, sem):
    del te_ref
    tm = xbuf.shape[1] // ROW_TILE
    i = pl.program_id(0)
    n_used = nt_ref[0]
    slot = i % 2

    @pl.when(i == 0)
    def _():
        _gather_rows(h_hbm, src_ref, 0, xbuf.at[0], sem.at[0])

    @pl.when(i + 1 < n_used)
    def _():
        _gather_rows(h_hbm, src_ref, (i + 1) * tm, xbuf.at[1 - slot], sem.at[1 - slot])

    @pl.when(i < n_used)
    def _():
        _wait_rows(h_hbm, xbuf.at[slot], sem.at[slot])
        xn = _rms_rows(_load_row_tiles(xbuf.at[slot], tm), g_ref[...])
        hid = jax.nn.silu(_mm(xn, wg_ref[0])) * _mm(xn, wu_ref[0])
        _store_row_tiles(o_ref, _mm(hid, wd_ref[0]))

    @pl.when(i >= n_used)
    def _():
        o_ref[...] = jnp.zeros_like(o_ref)


def _experts(h_tiles, g, tile_expert, n_used, src, w_gate, w_up, w_down):
    d = D_MODEL
    n_tiles = tile_expert.shape[0]
    tm = src.shape[0] // n_tiles
    de = w_gate.shape[-1]
    grid_spec = pltpu.PrefetchScalarGridSpec(
        num_scalar_prefetch=3,
        grid=(n_tiles,),
        in_specs=[pl.BlockSpec(memory_space=pl.ANY),
                  pl.BlockSpec((1, d), lambda i, te, nt, s: (0, 0)),
                  pl.BlockSpec((1, d, de), lambda i, te, nt, s: (te[i], 0, 0)),
                  pl.BlockSpec((1, d, de), lambda i, te, nt, s: (te[i], 0, 0)),
                  pl.BlockSpec((1, de, d), lambda i, te, nt, s: (te[i], 0, 0))],
        out_specs=pl.BlockSpec((tm * ROW_TILE, LANES), lambda i, te, nt, s: (i, 0)),
        scratch_shapes=[pltpu.VMEM((2, tm * ROW_TILE, LANES), F32), pltpu.SemaphoreType.DMA((2,))],
    )
    return pl.pallas_call(
        _expert_kernel,
        grid_spec=grid_spec,
        out_shape=jax.ShapeDtypeStruct((n_tiles * tm * ROW_TILE, LANES), F32),
        compiler_params=_cparams("arbitrary"),
        name="moe_experts",
    )(tile_expert, n_used, src, h_tiles, g.reshape(1, d), w_gate, w_up, w_down)


def _combine_kernel(d1_ref, d2_ref, ys_hbm, h_ref, rg_ref, gf_ref, o_ref, y1, y2, sem, *, final_norm):
    tm = h_ref.shape[0]
    i = pl.program_id(0)
    slot = i % 2

    def gather(tile, s):
        _gather_rows(ys_hbm, d1_ref, tile * tm, y1.at[s], sem.at[0, s])
        _gather_rows(ys_hbm, d2_ref, tile * tm, y2.at[s], sem.at[1, s])

    @pl.when(i == 0)
    def _():
        gather(0, 0)

    @pl.when(i + 1 < pl.num_programs(0))
    def _():
        gather(i + 1, 1 - slot)

    _wait_rows(ys_hbm, y1.at[slot], sem.at[0, slot])
    _wait_rows(ys_hbm, y2.at[slot], sem.at[1, slot])
    rg = rg_ref[...]
    out = (h_ref[...] + rg[:, 0:1] * _load_row_tiles(y1.at[slot], tm)
           + rg[:, 1:2] * _load_row_tiles(y2.at[slot], tm))
    if final_norm:
        out = _rms_rows(out, gf_ref[...])
    o_ref[...] = out


def _combine(h, ys, d1, d2, rg, g_final, final_norm):
    n, d = h.shape
    tm = min(PROJ_TM, n)
    grid_spec = pltpu.PrefetchScalarGridSpec(
        num_scalar_prefetch=2,
        grid=(n // tm,),
        in_specs=[pl.BlockSpec(memory_space=pl.ANY),
                  pl.BlockSpec((tm, d), lambda i, a, b: (i, 0)),
                  pl.BlockSpec((tm, LANES), lambda i, a, b: (i, 0)),
                  pl.BlockSpec((1, d), lambda i, a, b: (0, 0))],
        out_specs=pl.BlockSpec((tm, d), lambda i, a, b: (i, 0)),
        scratch_shapes=[pltpu.VMEM((2, tm * ROW_TILE, LANES), F32), pltpu.VMEM((2, tm * ROW_TILE, LANES), F32),
                        pltpu.SemaphoreType.DMA((2, 2))],
    )
    return pl.pallas_call(
        functools.partial(_combine_kernel, final_norm=final_norm),
        grid_spec=grid_spec,
        out_shape=jax.ShapeDtypeStruct((n, d), F32),
        compiler_params=_cparams("arbitrary"),
        name="moe_combine",
    )(d1, d2, ys, h, rg, g_final.reshape(1, d))


def _hmoe_residual(h, h_tiles, ri, rg, cnt, norm_g, w_gate, w_up, w_down, g_final, final_norm):
    n, d = h.shape

    counts = cnt[0, :N_EXPERTS]
    n_tiles = (2 * n) // MOE_TM + N_EXPERTS
    tiles_per = (counts + MOE_TM - 1) // MOE_TM
    tile_end = jnp.cumsum(tiles_per)
    seg_start = (tile_end - tiles_per) * MOE_TM
    dest = _dest(ri, seg_start.astype(I32))
    d1, d2 = dest[0], dest[1]
    tile_ids = jnp.arange(n_tiles, dtype=I32)
    tile_expert = jnp.minimum(jnp.sum((tile_end[None, :] <= tile_ids[:, None]).astype(I32), axis=1), N_EXPERTS - 1)
    src = _invert(d1, d2, n_tiles * MOE_TM)

    n_used = tile_end[N_EXPERTS - 1:].astype(I32)
    ys = _experts(h_tiles, norm_g, tile_expert, n_used, src, w_gate, w_up, w_down)
    return _combine(h, ys, d1, d2, rg, g_final, final_norm)


def kernel(x, norm_mix, norm_ffn, norm_final, ev_w_in, ev_mu, rw_w0, rw_w2, rw_a0, rw_a2, rw_g2, rw_k_k, rw_k_a, rw_r_k, rw_ln_w, rw_ln_b, hg_lb_logits, hg_norm, ev_w_out, od_w_in, gd_conv, gd_a_log, gd_dt_bias, gd_norm, ml_i_bias, ml_f_bias, ml_norm, od_w_out, moe_w_group, moe_b_group, moe_w_router, moe_b_router, moe_w_gate, moe_w_up, moe_w_down):
    bsz, t, d = x.shape
    n = bsz * t
    depth = norm_mix.shape[0]
    lb_table = jnp.cumsum(jax.nn.softmax(hg_lb_logits.astype(F32), axis=0), axis=0)
    h = x.reshape(n, d)
    for layer in range(depth):
        j = layer // 2
        if layer % 2 == 0:
            w_in = ev_w_in[j].astype(BF16)
            p_rw, p_hg = _rms_proj(h, norm_mix[layer], [w_in[:, :RW_IN], w_in[:, RW_IN:]])
            ya = _rwkv7(p_rw.reshape(bsz, t, RW_IN), ev_mu[j], rw_w0[j], rw_w2[j], rw_a0[j], rw_a2[j], rw_g2[j],
                        rw_k_k[j], rw_k_a[j], rw_r_k[j], rw_ln_w[j], rw_ln_b[j])
            yb = _hgrn2(p_hg.reshape(bsz, t, HG_IN), lb_table[j], hg_norm[j])
            mix_a, mix_b, w_out = ya.reshape(n, RW_DIM), yb.reshape(n, HG_KDIM), ev_w_out[j]
        else:
            w_in = od_w_in[j]
            w_gates = jnp.concatenate([w_in[:, GD_MAIN:GD_IN], w_in[:, GD_IN + ML_MAIN:],
                                       jnp.zeros((d, LANES - 2 * GD_HEADS - 2 * ML_HEADS), F32)], axis=1)
            p_gd, p_ml, p_gt = _rms_proj(h, norm_mix[layer], [w_in[:, :GD_MAIN].astype(BF16),
                                                              w_in[:, GD_IN:GD_IN + ML_MAIN].astype(BF16),
                                                              w_gates.astype(BF16)])
            p_gt = p_gt.reshape(bsz, t, LANES)
            yc = _gdn(p_gd.reshape(bsz, t, GD_MAIN), p_gt, gd_conv[j], gd_a_log[j], gd_dt_bias[j], gd_norm[j])
            yd = _mlstm(p_ml.reshape(bsz, t, ML_MAIN), p_gt, ml_i_bias[j], ml_f_bias[j], ml_norm[j])
            mix_a, mix_b, w_out = yc.reshape(n, GD_DIM), yd.reshape(n, ML_DIM), od_w_out[j]
        h, h_tiles, ri, rg, cnt = _out_proj_route(h, mix_a, mix_b, w_out, norm_ffn[layer], moe_w_group[layer],
                                                  moe_b_group[layer], moe_w_router[layer], moe_b_router[layer])
        h = _hmoe_residual(h, h_tiles, ri, rg, cnt, norm_ffn[layer], moe_w_gate[layer], moe_w_up[layer],
                           moe_w_down[layer], norm_final, final_norm=(layer == depth - 1))
    return h.reshape(bsz, t, d)
```

```python
import functools
import math

import jax
import jax.numpy as jnp
from jax import lax
from jax.experimental import pallas as pl
from jax.experimental.pallas import tpu as pltpu

F32 = jnp.float32
BF16 = jnp.bfloat16
I32 = jnp.int32
HIGHEST = lax.Precision.HIGHEST

D_MODEL = 1024
NORM_EPS = 1e-6
RW_HEADS, RW_HEAD_DIM = 8, 64
RW_DIM = RW_HEADS * RW_HEAD_DIM
R_DECAY, R_AAA, R_GATE = 64, 64, 128
RW_IN = 3 * RW_DIM + R_DECAY + R_AAA + R_GATE
RW_LN_EPS = 64e-5
HG_HEADS, HG_DIM = 8, 64
HG_KDIM = HG_HEADS * HG_DIM
HG_IN = 4 * HG_KDIM
GD_HEADS, GD_HEAD_DIM = 4, 128
GD_DIM = GD_HEADS * GD_HEAD_DIM
CONV_K = 4
GD_MAIN = 4 * GD_DIM
GD_IN = GD_MAIN + 2 * GD_HEADS
ML_HEADS, ML_HEAD_DIM = 4, 128
ML_DIM = ML_HEADS * ML_HEAD_DIM
ML_MAIN = 4 * ML_DIM
N_GROUPS, EXPERTS_PER_GROUP = 4, 8
N_EXPERTS = N_GROUPS * EXPERTS_PER_GROUP
D_EXPERT = 256

LANES = 128
VMEM_LIMIT_BYTES = 48 * 1024 * 1024

PROJ_TM = 256
CHUNK = 64
SUB = 16
RW_NCH = 4
GD_NCH = 4
HG_ROWS = 128
ML_CHUNK = 128
MOE_TM = 256
GATHER_UNROLL = 8
GATHER_SEMS = 4
NEG = -1e30


def _cparams(*sem):
    return pltpu.CompilerParams(dimension_semantics=sem, vmem_limit_bytes=VMEM_LIMIT_BYTES)


def _mm(a, b):
    return jnp.dot(a.astype(BF16), b.astype(BF16), preferred_element_type=F32)


def _mm_nt(a, b):
    return lax.dot_general(a.astype(BF16), b.astype(BF16), (((1,), (1,)), ((), ())), preferred_element_type=F32)


def _mm_tn(a, b):
    return lax.dot_general(a.astype(BF16), b.astype(BF16), (((0,), (0,)), ((), ())), preferred_element_type=F32)


def _mm_hi(a, b):
    return jnp.dot(a, b, precision=HIGHEST, preferred_element_type=F32)


def _mm_split(x, ones_bf16):
    hi = x.astype(BF16)
    lo = (x - hi.astype(F32)).astype(BF16)
    return (jnp.dot(hi, ones_bf16, preferred_element_type=F32) + jnp.dot(lo, ones_bf16, preferred_element_type=F32))


def _iota2(shape):
    return lax.broadcasted_iota(I32, shape, 0), lax.broadcasted_iota(I32, shape, 1)


def _rms_rows(x, g, eps=NORM_EPS):
    return x * lax.rsqrt(jnp.mean(x * x, axis=-1, keepdims=True) + eps) * g


def _seg_ones(width, seg):
    r, c = _iota2((width, width))
    return jnp.where((r // seg) == (c // seg), 1.0, 0.0).astype(BF16)


def _split3(x):
    x1 = x.astype(BF16)
    r1 = x - x1.astype(F32)
    x2 = r1.astype(BF16)
    return x1, x2, (r1 - x2.astype(F32)).astype(BF16)


def _cumsum_rows(tri_bf16, x):
    return sum(jnp.dot(tri_bf16, t, preferred_element_type=F32) for t in _split3(x))


def _cumsum_cols(x, tri_bf16):
    return sum(jnp.dot(t, tri_bf16, preferred_element_type=F32) for t in _split3(x))


def _mm3(a, b):
    ah = a.astype(BF16)
    al = (a - ah.astype(F32)).astype(BF16)
    bh = b.astype(BF16)
    bl = (b - bh.astype(F32)).astype(BF16)
    dot = lambda x, y: jnp.dot(x, y, preferred_element_type=F32)
    return dot(ah, bh) + dot(ah, bl) + dot(al, bh)


def _tri_inv_multi(ms, n, chain):
    assert chain // SUB <= 4
    r, c = _iota2((n, n))
    same = (r // SUB) == (c // SUB)
    eye = jnp.where(r == c, 1.0, 0.0).astype(F32)
    ds = [jnp.where(same, m, 0.0) for m in ms]
    offs = [m - d for m, d in zip(ms, ds)]
    xs = [eye - d for d in ds]
    ps = ds
    for _ in range(3):
        ps = [_mm(p, p) for p in ps]
        xs = [x + _mm(x, p) for x, p in zip(xs, ps)]
    es = [_mm(x, o) for x, o in zip(xs, offs)]
    imes = [eye - e for e in es]
    e2s = [_mm(e, e) for e in es]
    ys = [i + _mm(i, e2) for i, e2 in zip(imes, e2s)]
    xs = [_mm(y, x) for y, x in zip(ys, xs)]
    res = [eye - x - _mm3(m, x) for m, x in zip(ms, xs)]
    return [x + _mm(x, rr) for x, rr in zip(xs, res)]


def _rms_proj_kernel(x_ref, g_ref, *refs, n_out):
    y = _rms_rows(x_ref[...], g_ref[...]).astype(BF16)
    for w_ref, o_ref in zip(refs[:n_out], refs[n_out:]):
        o_ref[...] = jnp.dot(y, w_ref[...], preferred_element_type=F32)


def _rms_proj(x, g, ws):
    n, d = x.shape
    tm = min(PROJ_TM, n)
    in_specs = [pl.BlockSpec((tm, d), lambda i: (i, 0)), pl.BlockSpec((1, d), lambda i: (0, 0))]
    in_specs += [pl.BlockSpec(w.shape, lambda i: (0, 0)) for w in ws]
    return pl.pallas_call(
        functools.partial(_rms_proj_kernel, n_out=len(ws)),
        grid=(n // tm,),
        in_specs=in_specs,
        out_specs=[pl.BlockSpec((tm, w.shape[1]), lambda i: (i, 0)) for w in ws],
        out_shape=[jax.ShapeDtypeStruct((n, w.shape[1]), F32) for w in ws],
        compiler_params=_cparams("parallel"),
        name="rms_proj",
    )(x, g.reshape(1, d), *ws)


ROW_TILE = D_MODEL // LANES


def _store_row_tiles(ref, x):
    for j in range(ROW_TILE):
        ref[pl.ds(j, x.shape[0], stride=ROW_TILE), :] = x[:, j * LANES:(j + 1) * LANES]


def _load_row_tiles(ref, rows):
    return jnp.concatenate([ref[pl.ds(j, rows, stride=ROW_TILE), :] for j in range(ROW_TILE)], axis=1)


def _rwkv7_kernel(p_ref, mu_ref, w0_ref, w2_ref, a0_ref, a2_ref, g2_ref, kk_ref, ka_ref, rk_ref,
                  lnw_ref, lnb_ref, o_ref, prev_ref, zt_ref):
    L = CHUNK
    npair = RW_HEADS // 2
    c = pl.program_id(1)

    @pl.when(c == 0)
    def _():
        prev_ref[...] = jnp.zeros_like(prev_ref)
        zt_ref[...] = jnp.zeros_like(zt_ref)

    x = p_ref[0]
    R = x.shape[0]
    nch = R // L
    row = lax.broadcasted_iota(I32, x.shape, 0)
    xs = jnp.where(row == 0, prev_ref[7:8, :], pltpu.roll(x, 1, 0))
    prev_ref[...] = x[R - 8:R, :]
    pm = x + mu_ref[...] * (xs - x)
    r_all = pm[:, 0:RW_DIM]
    k_all = pm[:, RW_DIM:2 * RW_DIM]
    v_all = pm[:, 2 * RW_DIM:3 * RW_DIM]
    wa = pm[:, 3 * RW_DIM:3 * RW_DIM + LANES]
    gl = pm[:, 3 * RW_DIM + LANES:]
    wlog = -jax.nn.softplus(-(w0_ref[...] + _mm(jnp.tanh(wa), w2_ref[...]))) - 0.5
    ld = -jnp.exp(wlog)
    a_all = jax.nn.sigmoid(a0_ref[...] + _mm(wa, a2_ref[...]))
    g_all = _mm(jax.nn.sigmoid(gl), g2_ref[...])

    tr, tc = _iota2((R, R))
    chunk_tril = jnp.where((tc <= tr) & ((tr // L) == (tc // L)), 1.0, 0.0).astype(BF16)
    cs_all = _cumsum_rows(chunk_tril, ld)
    seg = _seg_ones(LANES, RW_HEAD_DIM)
    lane = lax.broadcasted_iota(I32, (L, LANES), 1)
    hm = (lane < RW_HEAD_DIM, lane >= RW_HEAD_DIM)
    br, bc = _iota2((2 * L, 2 * L))
    bd = (br // L) == (bc // L)
    bd_strict = bd & (bc < br)
    bd_incl = bd & (bc <= br)
    fold = lambda z: z[0:L] + z[L:2 * L]
    both = lambda z: jnp.concatenate([jnp.where(hm[0], z, 0.0), jnp.where(hm[1], z, 0.0)], axis=0)

    units = []
    for ci in range(nch):
        rs = slice(ci * L, (ci + 1) * L)
        for p in range(npair):
            sl = slice(p * LANES, (p + 1) * LANES)
            r, k, v, a = r_all[rs, sl], k_all[rs, sl], v_all[rs, sl], a_all[rs, sl]
            cs, ldp = cs_all[rs, sl], ld[rs, sl]
            kkr = k * kk_ref[:, sl]
            kk = kkr * lax.rsqrt(_mm_split(kkr * kkr, seg) + 1e-6)
            k2 = k * (1.0 + (a - 1.0) * ka_ref[:, sl])
            b = kk * a
            cs_last = cs[L - 1:L, :]
            e_neg = jnp.exp(-cs)
            e_rem = jnp.exp(cs_last - cs)
            bhat = b * e_neg
            khat = k2 * e_neg
            units.append(dict(p=p, rs=rs, sl=sl, r=r, v=v, k2=k2, rhat=r * jnp.exp(cs), gam_last=jnp.exp(cs_last),
                              btil=b * e_rem, ktil=k2 * e_rem, a2=both(kk * jnp.exp(cs - ldp)), v2=both(v),
                              rhs4=jnp.concatenate([bhat, bhat, khat, khat], axis=0)))
    for q in units:
        lhs = jnp.concatenate([q["a2"], both(q["rhat"])], axis=0)
        q["g"] = _mm_nt(lhs, q["rhs4"])
    tinvs = _tri_inv_multi([jnp.where(bd_strict, q["g"][0:2 * L, 0:2 * L], 0.0) for q in units], 2 * L, L)
    for q, tinv in zip(units, tinvs):
        q["tinv"] = tinv
        q["x2"] = _mm(jnp.where(bd_strict, q["g"][0:2 * L, 2 * L:4 * L], 0.0), q["v2"])
    for q in units:
        uw = _mm(q["tinv"], jnp.concatenate([q["x2"], q["a2"]], axis=1))
        q["u0"] = -fold(uw[:, 0:LANES])
        q["w"] = fold(uw[:, LANES:])
        q["y0"] = fold(_mm(jnp.where(bd_incl, q["g"][2 * L:4 * L, 2 * L:4 * L], 0.0), q["v2"]))
    for q in units:
        rb = jnp.where(bd_incl, q["g"][2 * L:4 * L, 0:2 * L], 0.0)
        ruw = _mm(rb, jnp.concatenate([both(q["u0"]), both(q["w"])], axis=1))
        q["yc"] = q["y0"] + fold(ruw[:, 0:LANES])
        q["ry"] = q["rhat"] - fold(ruw[:, LANES:])
        q["c1"] = _mm_tn(jnp.concatenate([q["u0"], q["v"]], axis=0), jnp.concatenate([q["btil"], q["ktil"]], axis=0))
        q["c2"] = _mm_tn(q["w"], q["btil"])

    hr, hc = _iota2((LANES, LANES))
    head_bd = (hr // RW_HEAD_DIM) == (hc // RW_HEAD_DIM)
    zts = [zt_ref[p] for p in range(npair)]
    for ci in range(nch):
        qs = units[ci * npair:(ci + 1) * npair]
        for q, zt in zip(qs, zts):
            q["y"] = q["yc"] + _mm_nt(q["ry"], zt)
        zts = [zt * q["gam_last"] + jnp.where(head_bd, q["c1"] - _mm(zt, q["c2"]), 0.0) for q, zt in zip(qs, zts)]
    for p in range(npair):
        zt_ref[p] = zts[p]

    for q in units:
        sl, rs, y = q["sl"], q["rs"], q["y"]
        mean = _mm_split(y, seg) * (1.0 / RW_HEAD_DIM)
        yc = y - mean
        var = _mm_split(yc * yc, seg) * (1.0 / RW_HEAD_DIM)
        yn = yc * lax.rsqrt(var + RW_LN_EPS) * lnw_ref[:, sl] + lnb_ref[:, sl]
        bonus = _mm_split(q["r"] * q["k2"] * rk_ref[:, sl], seg) * q["v"]
        o_ref[0, rs, sl] = (yn + bonus) * g_all[rs, sl]


def _rwkv7(p, mu, w0, w2, a0, a2, g2, k_k, k_a, r_k, ln_w, ln_b):
    bsz, t, _ = p.shape
    row = lambda z: z.reshape(1, -1).astype(F32)
    w2p = jnp.concatenate([w2, jnp.zeros_like(w2)], axis=0).astype(BF16)
    a2p = jnp.concatenate([jnp.zeros_like(a2), a2], axis=0).astype(BF16)
    params = [row(mu), row(w0), w2p, row(a0), a2p, g2.astype(BF16), row(k_k), row(k_a), row(r_k), row(ln_w), row(ln_b)]
    full = lambda z: pl.BlockSpec(z.shape, lambda b, c: (0, 0))
    rows = RW_NCH * CHUNK
    return pl.pallas_call(
        _rwkv7_kernel,
        grid=(bsz, t // rows),
        in_specs=[pl.BlockSpec((1, rows, RW_IN), lambda b, c: (b, c, 0))] + [full(z) for z in params],
        out_specs=pl.BlockSpec((1, rows, RW_DIM), lambda b, c: (b, c, 0)),
        out_shape=jax.ShapeDtypeStruct((bsz, t, RW_DIM), F32),
        scratch_shapes=[pltpu.VMEM((8, RW_IN), F32), pltpu.VMEM((RW_HEADS // 2, LANES, LANES), F32)],
        compiler_params=_cparams("arbitrary", "arbitrary"),
        name="rwkv7_mix",
    )(p, *params)


def _hgrn2_kernel(p_ref, lb_ref, ng_ref, o_ref, st_ref):
    L = HG_ROWS
    c = pl.program_id(1)

    @pl.when(c == 0)
    def _():
        st_ref[...] = jnp.zeros_like(st_ref)

    x = p_ref[0]
    lb = lb_ref[...]
    q_all = jax.nn.silu(x[:, 0:HG_KDIM])
    fg = lb + (1.0 - lb) * jax.nn.sigmoid(x[:, HG_KDIM:2 * HG_KDIM])
    k_all = 1.0 - fg
    logf = jnp.log(fg)
    v_all = x[:, 2 * HG_KDIM:3 * HG_KDIM]
    gate = x[:, 3 * HG_KDIM:]
    tr, tc = _iota2((L, L))
    blk_tril = jnp.where((tc <= tr) & ((tr // SUB) == (tc // SUB)), 1.0, 0.0).astype(BF16)
    bc_all = _cumsum_rows(blk_tril, logf)
    seg = _seg_ones(LANES, HG_DIM)
    br, bcc = _iota2((LANES, LANES))
    bd = (br // HG_DIM) == (bcc // HG_DIM)
    half = SUB // 2
    t_lo = lax.broadcasted_iota(I32, (SUB, LANES), 0)
    t_hi = lax.broadcasted_iota(I32, (half, LANES), 0) + half

    nsub = L // SUB
    npair = HG_HEADS // 2
    rows_per_unit = half * SUB + half * half
    sub_of_row = lax.broadcasted_iota(I32, (L, LANES), 0) // SUB

    units = []
    parts = []
    for p in range(npair):
        sl = slice(p * LANES, (p + 1) * LANES)
        for j in range(nsub):
            rs = slice(j * SUB, (j + 1) * SUB)
            q, k, v, bc = q_all[rs, sl], k_all[rs, sl], v_all[rs, sl], bc_all[rs, sl]
            q_hi, bc_hi = q[half:], bc[half:]
            for s in range(SUB):
                if s < half:
                    diff = jnp.where(t_lo >= s, bc - bc[s:s + 1, :], NEG)
                    parts.append(jnp.exp(diff) * q * k[s:s + 1, :])
                else:
                    diff = jnp.where(t_hi >= s, bc_hi - bc[s:s + 1, :], NEG)
                    parts.append(jnp.exp(diff) * q_hi * k[s:s + 1, :])
            bend = bc[SUB - 1:SUB, :]
            units.append(dict(v=v, qt=q * jnp.exp(bc), dec=jnp.exp(bend), kd=k * jnp.exp(bend - bc)))
    score_all = _mm(jnp.concatenate(parts, axis=0), seg)
    for ui, un in enumerate(units):
        score = score_all[ui * rows_per_unit:(ui + 1) * rows_per_unit]
        v = un["v"]
        acc_lo = jnp.zeros((half, LANES), F32)
        acc_hi = jnp.zeros((half, LANES), F32)
        off = 0
        for s in range(SUB):
            vs = v[s:s + 1, :]
            if s < half:
                acc_lo = acc_lo + score[off:off + half] * vs
                acc_hi = acc_hi + score[off + half:off + SUB] * vs
                off += SUB
            else:
                acc_hi = acc_hi + score[off:off + half] * vs
                off += half
        un["intra"] = jnp.concatenate([acc_lo, acc_hi], axis=0)
    for p in range(npair):
        us = units[p * nsub:(p + 1) * nsub]
        v_pair = jnp.concatenate([un["v"] for un in us], axis=0)
        kd_pair = jnp.concatenate([un["kd"] for un in us], axis=0)
        kd_wide = jnp.concatenate([jnp.where(sub_of_row == j, kd_pair, 0.0) for j in range(nsub)], axis=1)
        upd = _mm_tn(v_pair, kd_wide)
        for j, un in enumerate(us):
            un["upd"] = jnp.where(bd, upd[:, j * LANES:(j + 1) * LANES], 0.0)

    for p in range(npair):
        sl = slice(p * LANES, (p + 1) * LANES)
        st = st_ref[p]
        outs = []
        for un in units[p * nsub:(p + 1) * nsub]:
            outs.append(un["intra"] + _mm_nt(un["qt"], st))
            st = st * un["dec"] + un["upd"]
        st_ref[p] = st
        o = jnp.concatenate(outs, axis=0)
        ms = _mm_split(o * o, seg) * (1.0 / HG_DIM)
        o_ref[0, :, sl] = o * lax.rsqrt(ms + NORM_EPS) * ng_ref[:, sl] * jax.nn.silu(gate[:, sl])


def _hgrn2(p, lb, norm_g):
    bsz, t, _ = p.shape
    lb = lb.reshape(1, HG_KDIM).astype(F32)
    ng = jnp.tile(norm_g.astype(F32), HG_HEADS).reshape(1, HG_KDIM)
    return pl.pallas_call(
        _hgrn2_kernel,
        grid=(bsz, t // HG_ROWS),
        in_specs=[pl.BlockSpec((1, HG_ROWS, HG_IN), lambda b, c: (b, c, 0)),
                  pl.BlockSpec((1, HG_KDIM), lambda b, c: (0, 0)), pl.BlockSpec((1, HG_KDIM), lambda b, c: (0, 0))],
        out_specs=pl.BlockSpec((1, HG_ROWS, HG_KDIM), lambda b, c: (b, c, 0)),
        out_shape=jax.ShapeDtypeStruct((bsz, t, HG_KDIM), F32),
        scratch_shapes=[pltpu.VMEM((HG_HEADS // 2, LANES, LANES), F32)],
        compiler_params=_cparams("arbitrary", "arbitrary"),
        name="hgrn2_mix",
    )(p, lb, ng)


def _gdn_kernel(p_ref, pg_ref, cw_ref, gb_ref, nal_ref, nalc_ref, ng_ref, o_ref, prev_ref, s_ref):
    L = CHUNK
    c = pl.program_id(1)

    @pl.when(c == 0)
    def _():
        prev_ref[...] = jnp.zeros_like(prev_ref)
        s_ref[...] = jnp.zeros_like(s_ref)

    x = p_ref[0]
    R = x.shape[0]
    nch = R // L
    xq = x[:, 0:3 * GD_DIM]
    xcat = jnp.concatenate([prev_ref[...], xq], axis=0)
    conv = xq * cw_ref[CONV_K - 1:CONV_K, :]
    for j in range(1, CONV_K):
        conv = conv + xcat[8 - j:8 - j + R, :] * cw_ref[CONV_K - 1 - j:CONV_K - j, :]
    prev_ref[...] = xq[R - 8:R, :]
    qkv = jax.nn.silu(conv)
    z = x[:, 3 * GD_DIM:]

    gt = pg_ref[0] + gb_ref[...]
    beta_all = jax.nn.sigmoid(gt)
    g_all = nal_ref[...] * jax.nn.softplus(gt)
    g_t = nalc_ref[...] * jax.nn.softplus(gt.T)
    tr, tc = _iota2((R, R))
    chunk_tril = jnp.where((tc <= tr) & ((tr // L) == (tc // L)), 1.0, 0.0).astype(BF16)
    gam_all = _cumsum_rows(chunk_tril, g_all)
    ur, uc = _iota2((R, 2 * R))
    triu2 = jnp.where(((ur // L) == (uc // (2 * L))) & ((ur % L) <= (uc % L)), 1.0, 0.0).astype(BF16)
    gam_t2 = _cumsum_cols(g_t, triu2)
    lr, lc = _iota2((L, L))
    incl = lc <= lr
    br, bc = _iota2((2 * L, 2 * L))
    bd = (br // L) == (bc // L)
    bd_strict = bd & (bc < br)
    bd_incl = bd & (bc <= br)
    lane2 = lax.broadcasted_iota(I32, (1, 2 * L), 1)
    zero = jnp.zeros((L, LANES), F32)

    units = []
    for ci in range(nch):
        rs = slice(ci * L, (ci + 1) * L)
        for h in range(GD_HEADS):
            q = qkv[rs, h * LANES:(h + 1) * LANES]
            k = qkv[rs, GD_DIM + h * LANES:GD_DIM + (h + 1) * LANES]
            v = qkv[rs, 2 * GD_DIM + h * LANES:2 * GD_DIM + (h + 1) * LANES]
            q = q * lax.rsqrt(jnp.sum(q * q, axis=-1, keepdims=True) + 1e-6) * (GD_HEAD_DIM ** -0.5)
            k = k * lax.rsqrt(jnp.sum(k * k, axis=-1, keepdims=True) + 1e-6)
            beta = beta_all[rs, h:h + 1]
            gam = gam_all[rs, GD_HEADS + h:GD_HEADS + h + 1]
            gam_row2 = gam_t2[GD_HEADS + h:GD_HEADS + h + 1, ci * 2 * L:(ci + 1) * 2 * L]
            gam_last = gam[L - 1:L, :]
            kb = k * beta
            units.append(dict(h=h, rs=rs, q=q, k=k, kb=kb, vb=v * beta, gam=gam, gam_row2=gam_row2,
                              gam_last=gam_last, kg=kb * jnp.exp(gam), qg=q * jnp.exp(gam),
                              kd=k * jnp.exp(gam_last - gam),
                              decay=jnp.exp(jnp.where(incl, gam - gam_row2[:, 0:L], NEG))))
    pairs = [(units[i], units[i + 1]) for i in range(0, len(units), 2)]
    ms = []
    for h0, h1 in pairs:
        lhs = jnp.concatenate([jnp.concatenate([h0["kb"], zero], axis=1),
                               jnp.concatenate([zero, h1["kb"]], axis=1)], axis=0)
        rhs = jnp.concatenate([jnp.concatenate([h0["k"], zero], axis=1),
                               jnp.concatenate([zero, h1["k"]], axis=1)], axis=0)
        gam_col = jnp.concatenate([h0["gam"], h1["gam"]], axis=0)
        gam_row = jnp.where(lane2 < L, h0["gam_row2"], h1["gam_row2"])
        decay2 = jnp.exp(jnp.where(bd_incl, gam_col - gam_row, NEG))
        ms.append(jnp.where(bd_strict, _mm_nt(lhs, rhs) * decay2, 0.0))
    tinvs = _tri_inv_multi(ms, 2 * L, L)
    for (h0, h1), tinv in zip(pairs, tinvs):
        rhs = jnp.concatenate([jnp.concatenate([h0["vb"], h0["kg"]], axis=1),
                               jnp.concatenate([h1["vb"], h1["kg"]], axis=1)], axis=0)
        uw = _mm(tinv, rhs)
        h0["u"], h0["w"] = uw[0:L, 0:LANES], uw[0:L, LANES:]
        h1["u"], h1["w"] = uw[L:2 * L, 0:LANES], uw[L:2 * L, LANES:]
    for hd in units:
        attn = _mm_nt(hd["q"], hd["k"]) * hd["decay"]
        uw = jnp.concatenate([hd["u"], hd["w"]], axis=1)
        auw = _mm(attn, uw)
        hd["o0"] = auw[:, 0:LANES]
        hd["qs"] = hd["qg"] - auw[:, LANES:]
        cc = _mm_tn(hd["kd"], uw)
        hd["c1"], hd["c2"] = cc[:, 0:LANES], cc[:, LANES:]

    ss = [s_ref[h] for h in range(GD_HEADS)]
    for ci in range(nch):
        hds = units[ci * GD_HEADS:(ci + 1) * GD_HEADS]
        for hd, s in zip(hds, ss):
            hd["o"] = hd["o0"] + _mm(hd["qs"], s)
        ss = [s * jnp.exp(hd["gam_last"]) + hd["c1"] - _mm(hd["c2"], s) for hd, s in zip(hds, ss)]
    for h in range(GD_HEADS):
        s_ref[h] = ss[h]
    for hd in units:
        sl = slice(hd["h"] * LANES, (hd["h"] + 1) * LANES)
        o_ref[0, hd["rs"], sl] = _rms_rows(hd["o"], ng_ref[...]) * jax.nn.silu(z[hd["rs"], sl])


def _gate_row(lo, vals):
    return jnp.zeros((1, LANES), F32).at[0, lo:lo + vals.shape[0]].set(vals.astype(F32))


def _gdn(p, pg, conv_w, a_log, dt_bias, norm_g):
    bsz, t, _ = p.shape
    gbias = _gate_row(GD_HEADS, dt_bias)
    nal = _gate_row(GD_HEADS, -jnp.exp(a_log))
    nal_col = nal.reshape(LANES, 1)
    ng = norm_g.reshape(1, GD_HEAD_DIM).astype(F32)
    full = lambda z: pl.BlockSpec(z.shape, lambda b, c: (0, 0))
    rows = GD_NCH * CHUNK
    return pl.pallas_call(
        _gdn_kernel,
        grid=(bsz, t // rows),
        in_specs=[pl.BlockSpec((1, rows, GD_MAIN), lambda b, c: (b, c, 0)),
                  pl.BlockSpec((1, rows, LANES), lambda b, c: (b, c, 0)),
                  full(conv_w), full(gbias), full(nal), full(nal_col), full(ng)],
        out_specs=pl.BlockSpec((1, rows, GD_DIM), lambda b, c: (b, c, 0)),
        out_shape=jax.ShapeDtypeStruct((bsz, t, GD_DIM), F32),
        scratch_shapes=[pltpu.VMEM((8, 3 * GD_DIM), F32), pltpu.VMEM((GD_HEADS, LANES, LANES), F32)],
        compiler_params=_cparams("arbitrary", "arbitrary"),
        name="gdn_mix",
    )(p, pg, conv_w.astype(F32), gbias, nal, nal_col, ng)


def _mlstm_kernel(p_ref, pg_ref, gb_ref, ng_ref, o_ref, c_ref, n_ref, m_ref):
    L = ML_CHUNK
    ci = pl.program_id(1)

    @pl.when(ci == 0)
    def _():
        c_ref[...] = jnp.zeros_like(c_ref)
        n_ref[...] = jnp.zeros_like(n_ref)
        m_ref[...] = jnp.zeros_like(m_ref)

    x = p_ref[0]
    gt = pg_ref[0] + gb_ref[...]
    logf = jax.nn.log_sigmoid(gt)
    gt_t = gt.T
    logf_t = jax.nn.log_sigmoid(gt_t)
    tr, tc = _iota2((L, L))
    bc_all = _cumsum_rows(jnp.where(tc <= tr, 1.0, 0.0).astype(BF16), logf)
    bc_t = _cumsum_cols(logf_t, jnp.where(tr <= tc, 1.0, 0.0).astype(BF16))
    incl = tc <= tr
    i_lo, f_lo = 2 * GD_HEADS, 2 * GD_HEADS + ML_HEADS

    heads = []
    for h in range(ML_HEADS):
        sl = slice(h * LANES, (h + 1) * LANES)
        q = x[:, sl]
        k = x[:, ML_DIM + h * LANES:ML_DIM + (h + 1) * LANES] * (ML_HEAD_DIM ** -0.5)
        v = x[:, 2 * ML_DIM + h * LANES:2 * ML_DIM + (h + 1) * LANES]
        bc = bc_all[:, f_lo + h:f_lo + h + 1]
        bc_row = bc_t[f_lo + h:f_lo + h + 1, :]
        ig = gt[:, i_lo + h:i_lo + h + 1]
        ig_row = gt_t[i_lo + h:i_lo + h + 1, :]
        m_prev = m_ref[h][0:1, 0:1]
        b_last = bc[L - 1:L, :]
        d_log = jnp.where(incl, bc - bc_row + ig_row, NEG)
        inter_log = bc + m_prev
        m_t = jnp.maximum(inter_log, jnp.max(d_log, axis=-1, keepdims=True))
        upd_log = b_last - bc + ig
        m_new = jnp.maximum(b_last + m_prev, jnp.max(upd_log, axis=0, keepdims=True))
        wk = jnp.exp(upd_log - m_new) * k
        heads.append(dict(sl=sl, q=q, v=v, m_t=m_t, m_new=m_new, wk=wk, inter_w=jnp.exp(inter_log - m_t),
                          dec=jnp.exp(b_last + m_prev - m_new), sc=_mm_nt(q, k) * jnp.exp(d_log - m_t)))
    for hd in heads:
        hd["scv"] = _mm(hd["sc"], hd["v"])
        hd["kv"] = _mm_tn(hd["wk"], hd["v"])
    for h, hd in enumerate(heads):
        cmat = c_ref[h]
        nvec = n_ref[h][0:1, :]
        q, sc, inter_w, dec = hd["q"], hd["sc"], hd["inter_w"], hd["dec"]
        num = inter_w * _mm(q, cmat) + hd["scv"]
        den = inter_w * jnp.sum(q * nvec, axis=-1, keepdims=True) + jnp.sum(sc, axis=-1, keepdims=True)
        hd["hh"] = num / jnp.maximum(jnp.abs(den), jnp.exp(-hd["m_t"]))
        c_ref[h] = dec * cmat + hd["kv"]
        n_ref[h] = jnp.broadcast_to(dec * nvec + jnp.sum(hd["wk"], axis=0, keepdims=True), (8, LANES))
        m_ref[h] = jnp.broadcast_to(hd["m_new"], (8, LANES))
    for h, hd in enumerate(heads):
        sl = hd["sl"]
        og = jax.nn.sigmoid(x[:, 3 * ML_DIM + h * LANES:3 * ML_DIM + (h + 1) * LANES])
        o_ref[0, :, sl] = og * _rms_rows(hd["hh"], ng_ref[:, sl])


def _mlstm(p, pg, i_bias, f_bias, norm_g):
    bsz, t, _ = p.shape
    gbias = _gate_row(2 * GD_HEADS, jnp.concatenate([i_bias, f_bias]))
    ng = norm_g.reshape(1, ML_DIM).astype(F32)
    lc = min(ML_CHUNK, t)
    full = lambda z: pl.BlockSpec(z.shape, lambda b, c: (0, 0))
    return pl.pallas_call(
        _mlstm_kernel,
        grid=(bsz, t // lc),
        in_specs=[pl.BlockSpec((1, lc, ML_MAIN), lambda b, c: (b, c, 0)),
                  pl.BlockSpec((1, lc, LANES), lambda b, c: (b, c, 0)), full(gbias), full(ng)],
        out_specs=pl.BlockSpec((1, lc, ML_DIM), lambda b, c: (b, c, 0)),
        out_shape=jax.ShapeDtypeStruct((bsz, t, ML_DIM), F32),
        scratch_shapes=[pltpu.VMEM((ML_HEADS, LANES, LANES), F32), pltpu.VMEM((ML_HEADS, 8, LANES), F32),
                        pltpu.VMEM((ML_HEADS, 8, LANES), F32)],
        compiler_params=_cparams("arbitrary", "arbitrary"),
        name="mlstm_mix",
    )(p, pg, gbias, ng)


def _route(h, g_ref, wh_ref, wl_ref, b_ref, ri_ref, rg_ref, cnt_ref, off_ref):
    tm = h.shape[0]

    @pl.when(pl.program_id(0) == 0)
    def _():
        off_ref[...] = jnp.zeros_like(off_ref)

    xn = _rms_rows(h, g_ref[...])
    xh = xn.astype(BF16)
    xl = (xn - xh.astype(F32)).astype(BF16)
    dot = lambda a, b: jnp.dot(a, b, preferred_element_type=F32)
    logits = dot(xh, wh_ref[...]) + dot(xh, wl_ref[...]) + dot(xl, wh_ref[...]) + b_ref[...]
    lane = lax.broadcasted_iota(I32, logits.shape, 1)
    big = jnp.int32(1 << 20)
    is_grp = (lane >= N_EXPERTS) & (lane < N_EXPERTS + N_GROUPS)
    lg = jnp.where(is_grp, logits, NEG)
    gmax = jnp.max(lg, axis=-1, keepdims=True)
    p_top = 1.0 / jnp.sum(jnp.exp(lg - gmax), axis=-1, keepdims=True)
    g_idx = jnp.min(jnp.where(lg == gmax, lane, big), axis=-1, keepdims=True) - N_EXPERTS
    valid = (lane < N_EXPERTS) & ((lane // EXPERTS_PER_GROUP) == g_idx)
    v1 = jnp.where(valid, logits, NEG)
    m1 = jnp.max(v1, axis=-1, keepdims=True)
    i1 = jnp.min(jnp.where(v1 == m1, lane, big), axis=-1, keepdims=True)
    v2 = jnp.where(lane == i1, NEG, v1)
    m2 = jnp.max(v2, axis=-1, keepdims=True)
    i2 = jnp.min(jnp.where(v2 == m2, lane, big), axis=-1, keepdims=True)
    e21 = jnp.exp(m2 - m1)
    gate1 = p_top / (1.0 + e21)
    gate2 = p_top * e21 / (1.0 + e21)

    sel1 = lane == i1
    sel2 = lane == i2
    onehot = jnp.where(sel1 | sel2, 1.0, 0.0)
    tr, tc = _iota2((tm, tm))
    before = jnp.where(tc < tr, 1.0, 0.0).astype(BF16)
    prefix = jnp.dot(before, onehot.astype(BF16), preferred_element_type=F32) + off_ref[0:1, :]
    rank1 = jnp.sum(jnp.where(sel1, prefix, 0.0), axis=-1, keepdims=True).astype(I32)
    rank2 = jnp.sum(jnp.where(sel2, prefix, 0.0), axis=-1, keepdims=True).astype(I32)
    total = off_ref[0:1, :] + jnp.sum(onehot, axis=0, keepdims=True)
    off_ref[...] = jnp.broadcast_to(total, off_ref.shape)
    cnt_ref[...] = jnp.broadcast_to(total, cnt_ref.shape).astype(I32)

    ri_ref[...] = jnp.where(lane == 0, i1, jnp.where(lane == 1, i2, jnp.where(lane == 2, rank1, jnp.where(lane == 3, rank2, 0))))
    rg_ref[...] = jnp.where(lane == 0, gate1, jnp.where(lane == 1, gate2, 0.0))


def _out_proj_route_kernel(h_ref, ya_ref, yb_ref, wa_ref, wb_ref, g_ref, wh_ref, wl_ref, b_ref,
                           o_ref, ot_ref, ri_ref, rg_ref, cnt_ref, off_ref):
    out = h_ref[...] + _mm(ya_ref[...], wa_ref[...]) + _mm(yb_ref[...], wb_ref[...])
    o_ref[...] = out
    _store_row_tiles(ot_ref, out)
    _route(out, g_ref, wh_ref, wl_ref, b_ref, ri_ref, rg_ref, cnt_ref, off_ref)


def _out_proj_route(h, ya, yb, w_out, g_ffn, w_group, b_group, w_router, b_router):
    n, d = h.shape
    da, db = ya.shape[1], yb.shape[1]
    tm = min(PROJ_TM, n)
    wa = w_out[:da].astype(BF16)
    wb = w_out[da:].astype(BF16)
    pad = LANES - N_EXPERTS - N_GROUPS
    w_cat = jnp.concatenate([w_router, w_group, jnp.zeros((d, pad), F32)], axis=1)
    b_cat = jnp.concatenate([b_router, b_group, jnp.zeros((pad,), F32)]).reshape(1, LANES)
    w_hi = w_cat.astype(BF16)
    w_lo = (w_cat - w_hi.astype(F32)).astype(BF16)
    row = lambda i: (i, 0)
    fixed = lambda i: (0, 0)
    return pl.pallas_call(
        _out_proj_route_kernel,
        grid=(n // tm,),
        in_specs=[pl.BlockSpec((tm, d), row), pl.BlockSpec((tm, da), row), pl.BlockSpec((tm, db), row),
                  pl.BlockSpec((da, d), fixed), pl.BlockSpec((db, d), fixed), pl.BlockSpec((1, d), fixed),
                  pl.BlockSpec((d, LANES), fixed), pl.BlockSpec((d, LANES), fixed), pl.BlockSpec((1, LANES), fixed)],
        out_specs=[pl.BlockSpec((tm, d), row), pl.BlockSpec((tm * ROW_TILE, LANES), row),
                   pl.BlockSpec((tm, LANES), row), pl.BlockSpec((tm, LANES), row), pl.BlockSpec((8, LANES), fixed)],
        out_shape=[jax.ShapeDtypeStruct((n, d), F32), jax.ShapeDtypeStruct((n * ROW_TILE, LANES), F32),
                   jax.ShapeDtypeStruct((n, LANES), I32), jax.ShapeDtypeStruct((n, LANES), F32),
                   jax.ShapeDtypeStruct((8, LANES), I32)],
        scratch_shapes=[pltpu.VMEM((8, LANES), F32)],
        compiler_params=_cparams("arbitrary"),
        name="out_proj_route",
    )(h, ya, yb, wa, wb, g_ffn.reshape(1, d), w_hi, w_lo, b_cat)


def _dest_kernel(ri_ref, ss_ref, o_ref):
    ri = ri_ref[...]
    lane = lax.broadcasted_iota(I32, ri.shape, 1)
    ss = ss_ref[...].astype(F32)
    pick = lambda e: jnp.sum(jnp.where(lane == e, ss, 0.0), axis=-1, keepdims=True)
    d1 = pick(ri[:, 0:1]) + ri[:, 2:3].astype(F32)
    d2 = pick(ri[:, 1:2]) + ri[:, 3:4].astype(F32)
    both = jnp.where(lane == 0, d1, jnp.where(lane == 1, d2, 0.0))
    o_ref[...] = both.T[0:8, :].astype(I32)


def _dest(ri, seg_start):
    n = ri.shape[0]
    tm = min(PROJ_TM, n)
    ss = jnp.zeros((1, LANES), I32).at[0, :N_EXPERTS].set(seg_start)
    return pl.pallas_call(
        _dest_kernel,
        grid=(n // tm,),
        in_specs=[pl.BlockSpec((tm, LANES), lambda i: (i, 0)), pl.BlockSpec((1, LANES), lambda i: (0, 0))],
        out_specs=pl.BlockSpec((8, tm), lambda i: (0, i)),
        out_shape=jax.ShapeDtypeStruct((8, n), I32),
        compiler_params=_cparams("parallel"),
        name="moe_dest",
    )(ri, ss)


def _invert_kernel(d1_ref, d2_ref, o_ref):
    def clear(i, carry):
        o_ref[i] = 0
        return carry

    def fill(t, carry):
        o_ref[d1_ref[t]] = t
        o_ref[d2_ref[t]] = t
        return carry

    lax.fori_loop(0, o_ref.shape[0], clear, 0, unroll=GATHER_UNROLL)
    lax.fori_loop(0, d1_ref.shape[0], fill, 0, unroll=GATHER_UNROLL)


def _invert(d1, d2, n_rows):
    grid_spec = pltpu.PrefetchScalarGridSpec(
        num_scalar_prefetch=2, grid=(1,), in_specs=[],
        out_specs=pl.BlockSpec(memory_space=pltpu.SMEM))
    return pl.pallas_call(
        _invert_kernel,
        grid_spec=grid_spec,
        out_shape=jax.ShapeDtypeStruct((n_rows,), I32),
        compiler_params=_cparams("arbitrary"),
        name="moe_invert",
    )(d1, d2)


def _gather_rows(src_hbm, idx_ref, base, dst, sems):
    def body(g, carry):
        for u in range(GATHER_UNROLL):
            j = g * GATHER_UNROLL + u
            r = pl.multiple_of(idx_ref[base + j] * ROW_TILE, ROW_TILE)
            pltpu.make_async_copy(src_hbm.at[pl.ds(r, ROW_TILE), :],
                                  dst.at[pl.ds(pl.multiple_of(j * ROW_TILE, ROW_TILE), ROW_TILE), :],
                                  sems.at[u % GATHER_SEMS]).start(priority=(u // GATHER_SEMS) % 2)
        return carry

    lax.fori_loop(0, dst.shape[0] // (ROW_TILE * GATHER_UNROLL), body, 0)


def _wait_rows(src_hbm, dst, sems):
    share = dst.shape[0] // GATHER_SEMS
    for k in range(GATHER_SEMS):
        pltpu.make_async_copy(src_hbm.at[pl.ds(0, share), :], dst.at[pl.ds(0, share), :], sems.at[k]).wait()


def _expert_kernel(te_ref, nt_ref, src_ref, h_hbm, g_ref, wg_ref, wu_ref, wd_ref, o_ref, xbuf, sem):
    del te_ref
    tm = xbuf.shape[1] // ROW_TILE
    i = pl.program_id(0)
    n_used = nt_ref[0]
    slot = i % 2

    @pl.when(i == 0)
    def _():
        _gather_rows(h_hbm, src_ref, 0, xbuf.at[0], sem.at[0])

    @pl.when(i + 1 < n_used)
    def _():
        _gather_rows(h_hbm, src_ref, (i + 1) * tm, xbuf.at[1 - slot], sem.at[1 - slot])

    @pl.when(i < n_used)
    def _():
        _wait_rows(h_hbm, xbuf.at[slot], sem.at[slot])
        xn = _rms_rows(_load_row_tiles(xbuf.at[slot], tm), g_ref[...])
        hid = jax.nn.silu(_mm(xn, wg_ref[0])) * _mm(xn, wu_ref[0])
        _store_row_tiles(o_ref, _mm(hid, wd_ref[0]))

    @pl.when(i >= n_used)
    def _():
        o_ref[...] = jnp.zeros_like(o_ref)


def _experts(h_tiles, g, tile_expert, n_used, src, w_gate, w_up, w_down):
    d = D_MODEL
    n_tiles = tile_expert.shape[0]
    tm = src.shape[0] // n_tiles
    de = w_gate.shape[-1]
    grid_spec = pltpu.PrefetchScalarGridSpec(
        num_scalar_prefetch=3,
        grid=(n_tiles,),
        in_specs=[pl.BlockSpec(memory_space=pl.ANY),
                  pl.BlockSpec((1, d), lambda i, te, nt, s: (0, 0)),
                  pl.BlockSpec((1, d, de), lambda i, te, nt, s: (te[i], 0, 0)),
                  pl.BlockSpec((1, d, de), lambda i, te, nt, s: (te[i], 0, 0)),
                  pl.BlockSpec((1, de, d), lambda i, te, nt, s: (te[i], 0, 0))],
        out_specs=pl.BlockSpec((tm * ROW_TILE, LANES), lambda i, te, nt, s: (i, 0)),
        scratch_shapes=[pltpu.VMEM((2, tm * ROW_TILE, LANES), F32), pltpu.SemaphoreType.DMA((2, GATHER_SEMS))],
    )
    return pl.pallas_call(
        _expert_kernel,
        grid_spec=grid_spec,
        out_shape=jax.ShapeDtypeStruct((n_tiles * tm * ROW_TILE, LANES), F32),
        compiler_params=_cparams("arbitrary"),
        name="moe_experts",
    )(tile_expert, n_used, src, h_tiles, g.reshape(1, d), w_gate, w_up, w_down)


def _combine_kernel(d1_ref, d2_ref, ys_hbm, h_ref, rg_ref, gf_ref, o_ref, y1, y2, sem, *, final_norm):
    tm = h_ref.shape[0]
    i = pl.program_id(0)
    slot = i % 2

    def gather(tile, s):
        _gather_rows(ys_hbm, d1_ref, tile * tm, y1.at[s], sem.at[0, s])
        _gather_rows(ys_hbm, d2_ref, tile * tm, y2.at[s], sem.at[1, s])

    @pl.when(i == 0)
    def _():
        gather(0, 0)

    @pl.when(i + 1 < pl.num_programs(0))
    def _():
        gather(i + 1, 1 - slot)

    _wait_rows(ys_hbm, y1.at[slot], sem.at[0, slot])
    _wait_rows(ys_hbm, y2.at[slot], sem.at[1, slot])
    rg = rg_ref[...]
    out = (h_ref[...] + rg[:, 0:1] * _load_row_tiles(y1.at[slot], tm)
           + rg[:, 1:2] * _load_row_tiles(y2.at[slot], tm))
    if final_norm:
        out = _rms_rows(out, gf_ref[...])
    o_ref[...] = out


def _combine(h, ys, d1, d2, rg, g_final, final_norm):
    n, d = h.shape
    tm = min(PROJ_TM, n)
    grid_spec = pltpu.PrefetchScalarGridSpec(
        num_scalar_prefetch=2,
        grid=(n // tm,),
        in_specs=[pl.BlockSpec(memory_space=pl.ANY),
                  pl.BlockSpec((tm, d), lambda i, a, b: (i, 0)),
                  pl.BlockSpec((tm, LANES), lambda i, a, b: (i, 0)),
                  pl.BlockSpec((1, d), lambda i, a, b: (0, 0))],
        out_specs=pl.BlockSpec((tm, d), lambda i, a, b: (i, 0)),
        scratch_shapes=[pltpu.VMEM((2, tm * ROW_TILE, LANES), F32), pltpu.VMEM((2, tm * ROW_TILE, LANES), F32),
                        pltpu.SemaphoreType.DMA((2, 2, GATHER_SEMS))],
    )
    return pl.pallas_call(
        functools.partial(_combine_kernel, final_norm=final_norm),
        grid_spec=grid_spec,
        out_shape=jax.ShapeDtypeStruct((n, d), F32),
        compiler_params=_cparams("arbitrary"),
        name="moe_combine",
    )(d1, d2, ys, h, rg, g_final.reshape(1, d))


def _hmoe_residual(h, h_tiles, ri, rg, cnt, norm_g, w_gate, w_up, w_down, g_final, final_norm):
    n, d = h.shape

    counts = cnt[0, :N_EXPERTS]
    n_tiles = (2 * n) // MOE_TM + N_EXPERTS
    tiles_per = (counts + MOE_TM - 1) // MOE_TM
    tile_end = jnp.cumsum(tiles_per)
    seg_start = (tile_end - tiles_per) * MOE_TM
    dest = _dest(ri, seg_start.astype(I32))
    d1, d2 = dest[0], dest[1]
    tile_ids = jnp.arange(n_tiles, dtype=I32)
    tile_expert = jnp.minimum(jnp.sum((tile_end[None, :] <= tile_ids[:, None]).astype(I32), axis=1), N_EXPERTS - 1)
    src = _invert(d1, d2, n_tiles * MOE_TM)

    n_used = tile_end[N_EXPERTS - 1:].astype(I32)
    ys = _experts(h_tiles, norm_g, tile_expert, n_used, src, w_gate, w_up, w_down)
    return _combine(h, ys, d1, d2, rg, g_final, final_norm)


def kernel(x, norm_mix, norm_ffn, norm_final, ev_w_in, ev_mu, rw_w0, rw_w2, rw_a0, rw_a2, rw_g2, rw_k_k, rw_k_a, rw_r_k, rw_ln_w, rw_ln_b, hg_lb_logits, hg_norm, ev_w_out, od_w_in, gd_conv, gd_a_log, gd_dt_bias, gd_norm, ml_i_bias, ml_f_bias, ml_norm, od_w_out, moe_w_group, moe_b_group, moe_w_router, moe_b_router, moe_w_gate, moe_w_up, moe_w_down):
    bsz, t, d = x.shape
    n = bsz * t
    depth = norm_mix.shape[0]
    lb_table = jnp.cumsum(jax.nn.softmax(hg_lb_logits.astype(F32), axis=0), axis=0)
    h = x.reshape(n, d)
    for layer in range(depth):
        j = layer // 2
        if layer % 2 == 0:
            w_in = ev_w_in[j].astype(BF16)
            p_rw, p_hg = _rms_proj(h, norm_mix[layer], [w_in[:, :RW_IN], w_in[:, RW_IN:]])
            ya = _rwkv7(p_rw.reshape(bsz, t, RW_IN), ev_mu[j], rw_w0[j], rw_w2[j], rw_a0[j], rw_a2[j], rw_g2[j],
                        rw_k_k[j], rw_k_a[j], rw_r_k[j], rw_ln_w[j], rw_ln_b[j])
            yb = _hgrn2(p_hg.reshape(bsz, t, HG_IN), lb_table[j], hg_norm[j])
            mix_a, mix_b, w_out = ya.reshape(n, RW_DIM), yb.reshape(n, HG_KDIM), ev_w_out[j]
        else:
            w_in = od_w_in[j]
            w_gates = jnp.concatenate([w_in[:, GD_MAIN:GD_IN], w_in[:, GD_IN + ML_MAIN:],
                                       jnp.zeros((d, LANES - 2 * GD_HEADS - 2 * ML_HEADS), F32)], axis=1)
            p_gd, p_ml, p_gt = _rms_proj(h, norm_mix[layer], [w_in[:, :GD_MAIN].astype(BF16),
                                                              w_in[:, GD_IN:GD_IN + ML_MAIN].astype(BF16),
                                                              w_gates.astype(BF16)])
            p_gt = p_gt.reshape(bsz, t, LANES)
            yc = _gdn(p_gd.reshape(bsz, t, GD_MAIN), p_gt, gd_conv[j], gd_a_log[j], gd_dt_bias[j], gd_norm[j])
            yd = _mlstm(p_ml.reshape(bsz, t, ML_MAIN), p_gt, ml_i_bias[j], ml_f_bias[j], ml_norm[j])
            mix_a, mix_b, w_out = yc.reshape(n, GD_DIM), yd.reshape(n, ML_DIM), od_w_out[j]
        h, h_tiles, ri, rg, cnt = _out_proj_route(h, mix_a, mix_b, w_out, norm_ffn[layer], moe_w_group[layer],
                                                  moe_b_group[layer], moe_w_router[layer], moe_b_router[layer])
        h = _hmoe_residual(h, h_tiles, ri, rg, cnt, norm_ffn[layer], moe_w_gate[layer], moe_w_up[layer],
                           moe_w_down[layer], norm_final, final_norm=(layer == depth - 1))
    return h.reshape(bsz, t, d)
```

```python
import functools
import math

import jax
import jax.numpy as jnp
from jax import lax
from jax.experimental import pallas as pl
from jax.experimental.pallas import tpu as pltpu

F32 = jnp.float32
BF16 = jnp.bfloat16
I32 = jnp.int32
HIGHEST = lax.Precision.HIGHEST

D_MODEL = 1024
NORM_EPS = 1e-6
RW_HEADS, RW_HEAD_DIM = 8, 64
RW_DIM = RW_HEADS * RW_HEAD_DIM
R_DECAY, R_AAA, R_GATE = 64, 64, 128
RW_IN = 3 * RW_DIM + R_DECAY + R_AAA + R_GATE
RW_LN_EPS = 64e-5
HG_HEADS, HG_DIM = 8, 64
HG_KDIM = HG_HEADS * HG_DIM
HG_IN = 4 * HG_KDIM
GD_HEADS, GD_HEAD_DIM = 4, 128
GD_DIM = GD_HEADS * GD_HEAD_DIM
CONV_K = 4
GD_MAIN = 4 * GD_DIM
GD_IN = GD_MAIN + 2 * GD_HEADS
ML_HEADS, ML_HEAD_DIM = 4, 128
ML_DIM = ML_HEADS * ML_HEAD_DIM
ML_MAIN = 4 * ML_DIM
N_GROUPS, EXPERTS_PER_GROUP = 4, 8
N_EXPERTS = N_GROUPS * EXPERTS_PER_GROUP
D_EXPERT = 256

LANES = 128
VMEM_LIMIT_BYTES = 48 * 1024 * 1024

PROJ_TM = 256
CHUNK = 64
SUB = 16
RW_NCH = 4
GD_NCH = 4
HG_ROWS = 128
ML_CHUNK = 128
MOE_TM = 256
EXPERT_BUFS = 3
GATHER_UNROLL = 8
NEG = -1e30


def _cparams(*sem):
    return pltpu.CompilerParams(dimension_semantics=sem, vmem_limit_bytes=VMEM_LIMIT_BYTES)


def _mm(a, b):
    return jnp.dot(a.astype(BF16), b.astype(BF16), preferred_element_type=F32)


def _mm_nt(a, b):
    return lax.dot_general(a.astype(BF16), b.astype(BF16), (((1,), (1,)), ((), ())), preferred_element_type=F32)


def _mm_tn(a, b):
    return lax.dot_general(a.astype(BF16), b.astype(BF16), (((0,), (0,)), ((), ())), preferred_element_type=F32)


def _mm_hi(a, b):
    return jnp.dot(a, b, precision=HIGHEST, preferred_element_type=F32)


def _mm_split(x, ones_bf16):
    hi = x.astype(BF16)
    lo = (x - hi.astype(F32)).astype(BF16)
    return (jnp.dot(hi, ones_bf16, preferred_element_type=F32) + jnp.dot(lo, ones_bf16, preferred_element_type=F32))


def _iota2(shape):
    return lax.broadcasted_iota(I32, shape, 0), lax.broadcasted_iota(I32, shape, 1)


def _rms_rows(x, g, eps=NORM_EPS):
    return x * lax.rsqrt(jnp.mean(x * x, axis=-1, keepdims=True) + eps) * g


def _seg_ones(width, seg):
    r, c = _iota2((width, width))
    return jnp.where((r // seg) == (c // seg), 1.0, 0.0).astype(BF16)


def _split3(x):
    x1 = x.astype(BF16)
    r1 = x - x1.astype(F32)
    x2 = r1.astype(BF16)
    return x1, x2, (r1 - x2.astype(F32)).astype(BF16)


def _cumsum_rows(tri_bf16, x):
    return sum(jnp.dot(tri_bf16, t, preferred_element_type=F32) for t in _split3(x))


def _cumsum_cols(x, tri_bf16):
    return sum(jnp.dot(t, tri_bf16, preferred_element_type=F32) for t in _split3(x))


def _mm3(a, b):
    ah = a.astype(BF16)
    al = (a - ah.astype(F32)).astype(BF16)
    bh = b.astype(BF16)
    bl = (b - bh.astype(F32)).astype(BF16)
    dot = lambda x, y: jnp.dot(x, y, preferred_element_type=F32)
    return dot(ah, bh) + dot(ah, bl) + dot(al, bh)


def _tri_inv_multi(ms, n, chain):
    assert chain // SUB <= 4
    r, c = _iota2((n, n))
    same = (r // SUB) == (c // SUB)
    eye = jnp.where(r == c, 1.0, 0.0).astype(F32)
    ds = [jnp.where(same, m, 0.0) for m in ms]
    offs = [m - d for m, d in zip(ms, ds)]
    xs = [eye - d for d in ds]
    ps = ds
    for _ in range(3):
        ps = [_mm(p, p) for p in ps]
        xs = [x + _mm(x, p) for x, p in zip(xs, ps)]
    es = [_mm(x, o) for x, o in zip(xs, offs)]
    imes = [eye - e for e in es]
    e2s = [_mm(e, e) for e in es]
    ys = [i + _mm(i, e2) for i, e2 in zip(imes, e2s)]
    xs = [_mm(y, x) for y, x in zip(ys, xs)]
    res = [eye - x - _mm3(m, x) for m, x in zip(ms, xs)]
    return [x + _mm(x, rr) for x, rr in zip(xs, res)]


def _rms_proj_kernel(x_ref, g_ref, *refs, n_out):
    y = _rms_rows(x_ref[...], g_ref[...]).astype(BF16)
    for w_ref, o_ref in zip(refs[:n_out], refs[n_out:]):
        o_ref[...] = jnp.dot(y, w_ref[...], preferred_element_type=F32)


def _rms_proj(x, g, ws):
    n, d = x.shape
    tm = min(PROJ_TM, n)
    in_specs = [pl.BlockSpec((tm, d), lambda i: (i, 0)), pl.BlockSpec((1, d), lambda i: (0, 0))]
    in_specs += [pl.BlockSpec(w.shape, lambda i: (0, 0)) for w in ws]
    return pl.pallas_call(
        functools.partial(_rms_proj_kernel, n_out=len(ws)),
        grid=(n // tm,),
        in_specs=in_specs,
        out_specs=[pl.BlockSpec((tm, w.shape[1]), lambda i: (i, 0)) for w in ws],
        out_shape=[jax.ShapeDtypeStruct((n, w.shape[1]), F32) for w in ws],
        compiler_params=_cparams("parallel"),
        name="rms_proj",
    )(x, g.reshape(1, d), *ws)


ROW_TILE = D_MODEL // LANES


def _store_row_tiles(ref, x):
    for j in range(ROW_TILE):
        ref[pl.ds(j, x.shape[0], stride=ROW_TILE), :] = x[:, j * LANES:(j + 1) * LANES]


def _load_row_tiles(ref, rows):
    return jnp.concatenate([ref[pl.ds(j, rows, stride=ROW_TILE), :] for j in range(ROW_TILE)], axis=1)


def _rwkv7_kernel(p_ref, mu_ref, w0_ref, w2_ref, a0_ref, a2_ref, g2_ref, kk_ref, ka_ref, rk_ref,
                  lnw_ref, lnb_ref, o_ref, prev_ref, zt_ref):
    L = CHUNK
    npair = RW_HEADS // 2
    c = pl.program_id(1)

    @pl.when(c == 0)
    def _():
        prev_ref[...] = jnp.zeros_like(prev_ref)
        zt_ref[...] = jnp.zeros_like(zt_ref)

    x = p_ref[0]
    R = x.shape[0]
    nch = R // L
    row = lax.broadcasted_iota(I32, x.shape, 0)
    xs = jnp.where(row == 0, prev_ref[7:8, :], pltpu.roll(x, 1, 0))
    prev_ref[...] = x[R - 8:R, :]
    pm = x + mu_ref[...] * (xs - x)
    r_all = pm[:, 0:RW_DIM]
    k_all = pm[:, RW_DIM:2 * RW_DIM]
    v_all = pm[:, 2 * RW_DIM:3 * RW_DIM]
    wa = pm[:, 3 * RW_DIM:3 * RW_DIM + LANES]
    gl = pm[:, 3 * RW_DIM + LANES:]
    wlog = -jax.nn.softplus(-(w0_ref[...] + _mm(jnp.tanh(wa), w2_ref[...]))) - 0.5
    ld = -jnp.exp(wlog)
    a_all = jax.nn.sigmoid(a0_ref[...] + _mm(wa, a2_ref[...]))
    g_all = _mm(jax.nn.sigmoid(gl), g2_ref[...])

    tr, tc = _iota2((R, R))
    chunk_tril = jnp.where((tc <= tr) & ((tr // L) == (tc // L)), 1.0, 0.0).astype(BF16)
    cs_all = _cumsum_rows(chunk_tril, ld)
    seg = _seg_ones(LANES, RW_HEAD_DIM)
    lane = lax.broadcasted_iota(I32, (L, LANES), 1)
    hm = (lane < RW_HEAD_DIM, lane >= RW_HEAD_DIM)
    br, bc = _iota2((2 * L, 2 * L))
    bd = (br // L) == (bc // L)
    bd_strict = bd & (bc < br)
    bd_incl = bd & (bc <= br)
    fold = lambda z: z[0:L] + z[L:2 * L]
    both = lambda z: jnp.concatenate([jnp.where(hm[0], z, 0.0), jnp.where(hm[1], z, 0.0)], axis=0)

    units = []
    for ci in range(nch):
        rs = slice(ci * L, (ci + 1) * L)
        for p in range(npair):
            sl = slice(p * LANES, (p + 1) * LANES)
            r, k, v, a = r_all[rs, sl], k_all[rs, sl], v_all[rs, sl], a_all[rs, sl]
            cs, ldp = cs_all[rs, sl], ld[rs, sl]
            kkr = k * kk_ref[:, sl]
            kk = kkr * lax.rsqrt(_mm_split(kkr * kkr, seg) + 1e-6)
            k2 = k * (1.0 + (a - 1.0) * ka_ref[:, sl])
            b = kk * a
            cs_last = cs[L - 1:L, :]
            e_neg = jnp.exp(-cs)
            e_rem = jnp.exp(cs_last - cs)
            bhat = b * e_neg
            khat = k2 * e_neg
            units.append(dict(p=p, rs=rs, sl=sl, r=r, v=v, k2=k2, rhat=r * jnp.exp(cs), gam_last=jnp.exp(cs_last),
                              btil=b * e_rem, ktil=k2 * e_rem, a2=both(kk * jnp.exp(cs - ldp)), v2=both(v),
                              rhs4=jnp.concatenate([bhat, bhat, khat, khat], axis=0)))
    for q in units:
        lhs = jnp.concatenate([q["a2"], both(q["rhat"])], axis=0)
        q["g"] = _mm_nt(lhs, q["rhs4"])
    tinvs = _tri_inv_multi([jnp.where(bd_strict, q["g"][0:2 * L, 0:2 * L], 0.0) for q in units], 2 * L, L)
    for q, tinv in zip(units, tinvs):
        q["tinv"] = tinv
        q["x2"] = _mm(jnp.where(bd_strict, q["g"][0:2 * L, 2 * L:4 * L], 0.0), q["v2"])
    for q in units:
        uw = _mm(q["tinv"], jnp.concatenate([q["x2"], q["a2"]], axis=1))
        q["u0"] = -fold(uw[:, 0:LANES])
        q["w"] = fold(uw[:, LANES:])
        q["y0"] = fold(_mm(jnp.where(bd_incl, q["g"][2 * L:4 * L, 2 * L:4 * L], 0.0), q["v2"]))
    for q in units:
        rb = jnp.where(bd_incl, q["g"][2 * L:4 * L, 0:2 * L], 0.0)
        ruw = _mm(rb, jnp.concatenate([both(q["u0"]), both(q["w"])], axis=1))
        q["yc"] = q["y0"] + fold(ruw[:, 0:LANES])
        q["ry"] = q["rhat"] - fold(ruw[:, LANES:])
        q["c1"] = _mm_tn(jnp.concatenate([q["u0"], q["v"]], axis=0), jnp.concatenate([q["btil"], q["ktil"]], axis=0))
        q["c2"] = _mm_tn(q["w"], q["btil"])

    hr, hc = _iota2((LANES, LANES))
    head_bd = (hr // RW_HEAD_DIM) == (hc // RW_HEAD_DIM)
    zts = [zt_ref[p] for p in range(npair)]
    for ci in range(nch):
        qs = units[ci * npair:(ci + 1) * npair]
        for q, zt in zip(qs, zts):
            q["y"] = q["yc"] + _mm_nt(q["ry"], zt)
        zts = [zt * q["gam_last"] + jnp.where(head_bd, q["c1"] - _mm(zt, q["c2"]), 0.0) for q, zt in zip(qs, zts)]
    for p in range(npair):
        zt_ref[p] = zts[p]

    for q in units:
        sl, rs, y = q["sl"], q["rs"], q["y"]
        mean = _mm_split(y, seg) * (1.0 / RW_HEAD_DIM)
        yc = y - mean
        var = _mm_split(yc * yc, seg) * (1.0 / RW_HEAD_DIM)
        yn = yc * lax.rsqrt(var + RW_LN_EPS) * lnw_ref[:, sl] + lnb_ref[:, sl]
        bonus = _mm_split(q["r"] * q["k2"] * rk_ref[:, sl], seg) * q["v"]
        o_ref[0, rs, sl] = (yn + bonus) * g_all[rs, sl]


def _rwkv7(p, mu, w0, w2, a0, a2, g2, k_k, k_a, r_k, ln_w, ln_b):
    bsz, t, _ = p.shape
    row = lambda z: z.reshape(1, -1).astype(F32)
    w2p = jnp.concatenate([w2, jnp.zeros_like(w2)], axis=0).astype(BF16)
    a2p = jnp.concatenate([jnp.zeros_like(a2), a2], axis=0).astype(BF16)
    params = [row(mu), row(w0), w2p, row(a0), a2p, g2.astype(BF16), row(k_k), row(k_a), row(r_k), row(ln_w), row(ln_b)]
    full = lambda z: pl.BlockSpec(z.shape, lambda b, c: (0, 0))
    rows = RW_NCH * CHUNK
    return pl.pallas_call(
        _rwkv7_kernel,
        grid=(bsz, t // rows),
        in_specs=[pl.BlockSpec((1, rows, RW_IN), lambda b, c: (b, c, 0))] + [full(z) for z in params],
        out_specs=pl.BlockSpec((1, rows, RW_DIM), lambda b, c: (b, c, 0)),
        out_shape=jax.ShapeDtypeStruct((bsz, t, RW_DIM), F32),
        scratch_shapes=[pltpu.VMEM((8, RW_IN), F32), pltpu.VMEM((RW_HEADS // 2, LANES, LANES), F32)],
        compiler_params=_cparams("arbitrary", "arbitrary"),
        name="rwkv7_mix",
    )(p, *params)


def _hgrn2_kernel(p_ref, lb_ref, ng_ref, o_ref, st_ref):
    L = HG_ROWS
    c = pl.program_id(1)

    @pl.when(c == 0)
    def _():
        st_ref[...] = jnp.zeros_like(st_ref)

    x = p_ref[0]
    lb = lb_ref[...]
    q_all = jax.nn.silu(x[:, 0:HG_KDIM])
    fg = lb + (1.0 - lb) * jax.nn.sigmoid(x[:, HG_KDIM:2 * HG_KDIM])
    k_all = 1.0 - fg
    logf = jnp.log(fg)
    v_all = x[:, 2 * HG_KDIM:3 * HG_KDIM]
    gate = x[:, 3 * HG_KDIM:]
    tr, tc = _iota2((L, L))
    blk_tril = jnp.where((tc <= tr) & ((tr // SUB) == (tc // SUB)), 1.0, 0.0).astype(BF16)
    bc_all = _cumsum_rows(blk_tril, logf)
    seg = _seg_ones(LANES, HG_DIM)
    br, bcc = _iota2((LANES, LANES))
    bd = (br // HG_DIM) == (bcc // HG_DIM)
    half = SUB // 2
    t_lo = lax.broadcasted_iota(I32, (SUB, LANES), 0)
    t_hi = lax.broadcasted_iota(I32, (half, LANES), 0) + half

    nsub = L // SUB
    npair = HG_HEADS // 2
    rows_per_unit = half * SUB + half * half
    sub_of_row = lax.broadcasted_iota(I32, (L, LANES), 0) // SUB

    units = []
    parts = []
    for p in range(npair):
        sl = slice(p * LANES, (p + 1) * LANES)
        for j in range(nsub):
            rs = slice(j * SUB, (j + 1) * SUB)
            q, k, v, bc = q_all[rs, sl], k_all[rs, sl], v_all[rs, sl], bc_all[rs, sl]
            q_hi, bc_hi = q[half:], bc[half:]
            for s in range(SUB):
                if s < half:
                    diff = jnp.where(t_lo >= s, bc - bc[s:s + 1, :], NEG)
                    parts.append(jnp.exp(diff) * q * k[s:s + 1, :])
                else:
                    diff = jnp.where(t_hi >= s, bc_hi - bc[s:s + 1, :], NEG)
                    parts.append(jnp.exp(diff) * q_hi * k[s:s + 1, :])
            bend = bc[SUB - 1:SUB, :]
            units.append(dict(v=v, qt=q * jnp.exp(bc), dec=jnp.exp(bend), kd=k * jnp.exp(bend - bc)))
    score_all = _mm(jnp.concatenate(parts, axis=0), seg)
    for ui, un in enumerate(units):
        score = score_all[ui * rows_per_unit:(ui + 1) * rows_per_unit]
        v = un["v"]
        acc_lo = jnp.zeros((half, LANES), F32)
        acc_hi = jnp.zeros((half, LANES), F32)
        off = 0
        for s in range(SUB):
            vs = v[s:s + 1, :]
            if s < half:
                acc_lo = acc_lo + score[off:off + half] * vs
                acc_hi = acc_hi + score[off + half:off + SUB] * vs
                off += SUB
            else:
                acc_hi = acc_hi + score[off:off + half] * vs
                off += half
        un["intra"] = jnp.concatenate([acc_lo, acc_hi], axis=0)
    for p in range(npair):
        us = units[p * nsub:(p + 1) * nsub]
        v_pair = jnp.concatenate([un["v"] for un in us], axis=0)
        kd_pair = jnp.concatenate([un["kd"] for un in us], axis=0)
        kd_wide = jnp.concatenate([jnp.where(sub_of_row == j, kd_pair, 0.0) for j in range(nsub)], axis=1)
        upd = _mm_tn(v_pair, kd_wide)
        for j, un in enumerate(us):
            un["upd"] = jnp.where(bd, upd[:, j * LANES:(j + 1) * LANES], 0.0)

    for p in range(npair):
        sl = slice(p * LANES, (p + 1) * LANES)
        st = st_ref[p]
        outs = []
        for un in units[p * nsub:(p + 1) * nsub]:
            outs.append(un["intra"] + _mm_nt(un["qt"], st))
            st = st * un["dec"] + un["upd"]
        st_ref[p] = st
        o = jnp.concatenate(outs, axis=0)
        ms = _mm_split(o * o, seg) * (1.0 / HG_DIM)
        o_ref[0, :, sl] = o * lax.rsqrt(ms + NORM_EPS) * ng_ref[:, sl] * jax.nn.silu(gate[:, sl])


def _hgrn2(p, lb, norm_g):
    bsz, t, _ = p.shape
    lb = lb.reshape(1, HG_KDIM).astype(F32)
    ng = jnp.tile(norm_g.astype(F32), HG_HEADS).reshape(1, HG_KDIM)
    return pl.pallas_call(
        _hgrn2_kernel,
        grid=(bsz, t // HG_ROWS),
        in_specs=[pl.BlockSpec((1, HG_ROWS, HG_IN), lambda b, c: (b, c, 0)),
                  pl.BlockSpec((1, HG_KDIM), lambda b, c: (0, 0)), pl.BlockSpec((1, HG_KDIM), lambda b, c: (0, 0))],
        out_specs=pl.BlockSpec((1, HG_ROWS, HG_KDIM), lambda b, c: (b, c, 0)),
        out_shape=jax.ShapeDtypeStruct((bsz, t, HG_KDIM), F32),
        scratch_shapes=[pltpu.VMEM((HG_HEADS // 2, LANES, LANES), F32)],
        compiler_params=_cparams("arbitrary", "arbitrary"),
        name="hgrn2_mix",
    )(p, lb, ng)


def _gdn_kernel(p_ref, pg_ref, cw_ref, gb_ref, nal_ref, nalc_ref, ng_ref, o_ref, prev_ref, s_ref):
    L = CHUNK
    c = pl.program_id(1)

    @pl.when(c == 0)
    def _():
        prev_ref[...] = jnp.zeros_like(prev_ref)
        s_ref[...] = jnp.zeros_like(s_ref)

    x = p_ref[0]
    R = x.shape[0]
    nch = R // L
    xq = x[:, 0:3 * GD_DIM]
    xcat = jnp.concatenate([prev_ref[...], xq], axis=0)
    conv = xq * cw_ref[CONV_K - 1:CONV_K, :]
    for j in range(1, CONV_K):
        conv = conv + xcat[8 - j:8 - j + R, :] * cw_ref[CONV_K - 1 - j:CONV_K - j, :]
    prev_ref[...] = xq[R - 8:R, :]
    qkv = jax.nn.silu(conv)
    z = x[:, 3 * GD_DIM:]

    gt = pg_ref[0] + gb_ref[...]
    beta_all = jax.nn.sigmoid(gt)
    g_all = nal_ref[...] * jax.nn.softplus(gt)
    g_t = nalc_ref[...] * jax.nn.softplus(gt.T)
    tr, tc = _iota2((R, R))
    chunk_tril = jnp.where((tc <= tr) & ((tr // L) == (tc // L)), 1.0, 0.0).astype(BF16)
    gam_all = _cumsum_rows(chunk_tril, g_all)
    ur, uc = _iota2((R, 2 * R))
    triu2 = jnp.where(((ur // L) == (uc // (2 * L))) & ((ur % L) <= (uc % L)), 1.0, 0.0).astype(BF16)
    gam_t2 = _cumsum_cols(g_t, triu2)
    lr, lc = _iota2((L, L))
    incl = lc <= lr
    br, bc = _iota2((2 * L, 2 * L))
    bd = (br // L) == (bc // L)
    bd_strict = bd & (bc < br)
    bd_incl = bd & (bc <= br)
    lane2 = lax.broadcasted_iota(I32, (1, 2 * L), 1)
    zero = jnp.zeros((L, LANES), F32)

    units = []
    for ci in range(nch):
        rs = slice(ci * L, (ci + 1) * L)
        for h in range(GD_HEADS):
            q = qkv[rs, h * LANES:(h + 1) * LANES]
            k = qkv[rs, GD_DIM + h * LANES:GD_DIM + (h + 1) * LANES]
            v = qkv[rs, 2 * GD_DIM + h * LANES:2 * GD_DIM + (h + 1) * LANES]
            q = q * lax.rsqrt(jnp.sum(q * q, axis=-1, keepdims=True) + 1e-6) * (GD_HEAD_DIM ** -0.5)
            k = k * lax.rsqrt(jnp.sum(k * k, axis=-1, keepdims=True) + 1e-6)
            beta = beta_all[rs, h:h + 1]
            gam = gam_all[rs, GD_HEADS + h:GD_HEADS + h + 1]
            gam_row2 = gam_t2[GD_HEADS + h:GD_HEADS + h + 1, ci * 2 * L:(ci + 1) * 2 * L]
            gam_last = gam[L - 1:L, :]
            kb = k * beta
            units.append(dict(h=h, rs=rs, q=q, k=k, kb=kb, vb=v * beta, gam=gam, gam_row2=gam_row2,
                              gam_last=gam_last, kg=kb * jnp.exp(gam), qg=q * jnp.exp(gam),
                              kd=k * jnp.exp(gam_last - gam),
                              decay=jnp.exp(jnp.where(incl, gam - gam_row2[:, 0:L], NEG))))
    pairs = [(units[i], units[i + 1]) for i in range(0, len(units), 2)]
    ms = []
    for h0, h1 in pairs:
        lhs = jnp.concatenate([jnp.concatenate([h0["kb"], zero], axis=1),
                               jnp.concatenate([zero, h1["kb"]], axis=1)], axis=0)
        rhs = jnp.concatenate([jnp.concatenate([h0["k"], zero], axis=1),
                               jnp.concatenate([zero, h1["k"]], axis=1)], axis=0)
        gam_col = jnp.concatenate([h0["gam"], h1["gam"]], axis=0)
        gam_row = jnp.where(lane2 < L, h0["gam_row2"], h1["gam_row2"])
        decay2 = jnp.exp(jnp.where(bd_incl, gam_col - gam_row, NEG))
        ms.append(jnp.where(bd_strict, _mm_nt(lhs, rhs) * decay2, 0.0))
    tinvs = _tri_inv_multi(ms, 2 * L, L)
    for (h0, h1), tinv in zip(pairs, tinvs):
        rhs = jnp.concatenate([jnp.concatenate([h0["vb"], h0["kg"]], axis=1),
                               jnp.concatenate([h1["vb"], h1["kg"]], axis=1)], axis=0)
        uw = _mm(tinv, rhs)
        h0["u"], h0["w"] = uw[0:L, 0:LANES], uw[0:L, LANES:]
        h1["u"], h1["w"] = uw[L:2 * L, 0:LANES], uw[L:2 * L, LANES:]
    for hd in units:
        attn = _mm_nt(hd["q"], hd["k"]) * hd["decay"]
        uw = jnp.concatenate([hd["u"], hd["w"]], axis=1)
        auw = _mm(attn, uw)
        hd["o0"] = auw[:, 0:LANES]
        hd["qs"] = hd["qg"] - auw[:, LANES:]
        cc = _mm_tn(hd["kd"], uw)
        hd["c1"], hd["c2"] = cc[:, 0:LANES], cc[:, LANES:]

    ss = [s_ref[h] for h in range(GD_HEADS)]
    for ci in range(nch):
        hds = units[ci * GD_HEADS:(ci + 1) * GD_HEADS]
        for hd, s in zip(hds, ss):
            hd["o"] = hd["o0"] + _mm(hd["qs"], s)
        ss = [s * jnp.exp(hd["gam_last"]) + hd["c1"] - _mm(hd["c2"], s) for hd, s in zip(hds, ss)]
    for h in range(GD_HEADS):
        s_ref[h] = ss[h]
    for hd in units:
        sl = slice(hd["h"] * LANES, (hd["h"] + 1) * LANES)
        o_ref[0, hd["rs"], sl] = _rms_rows(hd["o"], ng_ref[...]) * jax.nn.silu(z[hd["rs"], sl])


def _gate_row(lo, vals):
    return jnp.zeros((1, LANES), F32).at[0, lo:lo + vals.shape[0]].set(vals.astype(F32))


def _gdn(p, pg, conv_w, a_log, dt_bias, norm_g):
    bsz, t, _ = p.shape
    gbias = _gate_row(GD_HEADS, dt_bias)
    nal = _gate_row(GD_HEADS, -jnp.exp(a_log))
    nal_col = nal.reshape(LANES, 1)
    ng = norm_g.reshape(1, GD_HEAD_DIM).astype(F32)
    full = lambda z: pl.BlockSpec(z.shape, lambda b, c: (0, 0))
    rows = GD_NCH * CHUNK
    return pl.pallas_call(
        _gdn_kernel,
        grid=(bsz, t // rows),
        in_specs=[pl.BlockSpec((1, rows, GD_MAIN), lambda b, c: (b, c, 0)),
                  pl.BlockSpec((1, rows, LANES), lambda b, c: (b, c, 0)),
                  full(conv_w), full(gbias), full(nal), full(nal_col), full(ng)],
        out_specs=pl.BlockSpec((1, rows, GD_DIM), lambda b, c: (b, c, 0)),
        out_shape=jax.ShapeDtypeStruct((bsz, t, GD_DIM), F32),
        scratch_shapes=[pltpu.VMEM((8, 3 * GD_DIM), F32), pltpu.VMEM((GD_HEADS, LANES, LANES), F32)],
        compiler_params=_cparams("arbitrary", "arbitrary"),
        name="gdn_mix",
    )(p, pg, conv_w.astype(F32), gbias, nal, nal_col, ng)


def _mlstm_kernel(p_ref, pg_ref, gb_ref, ng_ref, o_ref, c_ref, n_ref, m_ref):
    L = ML_CHUNK
    ci = pl.program_id(1)

    @pl.when(ci == 0)
    def _():
        c_ref[...] = jnp.zeros_like(c_ref)
        n_ref[...] = jnp.zeros_like(n_ref)
        m_ref[...] = jnp.zeros_like(m_ref)

    x = p_ref[0]
    gt = pg_ref[0] + gb_ref[...]
    logf = jax.nn.log_sigmoid(gt)
    gt_t = gt.T
    logf_t = jax.nn.log_sigmoid(gt_t)
    tr, tc = _iota2((L, L))
    bc_all = _cumsum_rows(jnp.where(tc <= tr, 1.0, 0.0).astype(BF16), logf)
    bc_t = _cumsum_cols(logf_t, jnp.where(tr <= tc, 1.0, 0.0).astype(BF16))
    incl = tc <= tr
    i_lo, f_lo = 2 * GD_HEADS, 2 * GD_HEADS + ML_HEADS

    heads = []
    for h in range(ML_HEADS):
        sl = slice(h * LANES, (h + 1) * LANES)
        q = x[:, sl]
        k = x[:, ML_DIM + h * LANES:ML_DIM + (h + 1) * LANES] * (ML_HEAD_DIM ** -0.5)
        v = x[:, 2 * ML_DIM + h * LANES:2 * ML_DIM + (h + 1) * LANES]
        bc = bc_all[:, f_lo + h:f_lo + h + 1]
        bc_row = bc_t[f_lo + h:f_lo + h + 1, :]
        ig = gt[:, i_lo + h:i_lo + h + 1]
        ig_row = gt_t[i_lo + h:i_lo + h + 1, :]
        m_prev = m_ref[h][0:1, 0:1]
        b_last = bc[L - 1:L, :]
        d_log = jnp.where(incl, bc - bc_row + ig_row, NEG)
        inter_log = bc + m_prev
        m_t = jnp.maximum(inter_log, jnp.max(d_log, axis=-1, keepdims=True))
        upd_log = b_last - bc + ig
        m_new = jnp.maximum(b_last + m_prev, jnp.max(upd_log, axis=0, keepdims=True))
        wk = jnp.exp(upd_log - m_new) * k
        heads.append(dict(sl=sl, q=q, v=v, m_t=m_t, m_new=m_new, wk=wk, inter_w=jnp.exp(inter_log - m_t),
                          dec=jnp.exp(b_last + m_prev - m_new), sc=_mm_nt(q, k) * jnp.exp(d_log - m_t)))
    for hd in heads:
        hd["scv"] = _mm(hd["sc"], hd["v"])
        hd["kv"] = _mm_tn(hd["wk"], hd["v"])
    for h, hd in enumerate(heads):
        cmat = c_ref[h]
        nvec = n_ref[h][0:1, :]
        q, sc, inter_w, dec = hd["q"], hd["sc"], hd["inter_w"], hd["dec"]
        num = inter_w * _mm(q, cmat) + hd["scv"]
        den = inter_w * jnp.sum(q * nvec, axis=-1, keepdims=True) + jnp.sum(sc, axis=-1, keepdims=True)
        hd["hh"] = num / jnp.maximum(jnp.abs(den), jnp.exp(-hd["m_t"]))
        c_ref[h] = dec * cmat + hd["kv"]
        n_ref[h] = jnp.broadcast_to(dec * nvec + jnp.sum(hd["wk"], axis=0, keepdims=True), (8, LANES))
        m_ref[h] = jnp.broadcast_to(hd["m_new"], (8, LANES))
    for h, hd in enumerate(heads):
        sl = hd["sl"]
        og = jax.nn.sigmoid(x[:, 3 * ML_DIM + h * LANES:3 * ML_DIM + (h + 1) * LANES])
        o_ref[0, :, sl] = og * _rms_rows(hd["hh"], ng_ref[:, sl])


def _mlstm(p, pg, i_bias, f_bias, norm_g):
    bsz, t, _ = p.shape
    gbias = _gate_row(2 * GD_HEADS, jnp.concatenate([i_bias, f_bias]))
    ng = norm_g.reshape(1, ML_DIM).astype(F32)
    lc = min(ML_CHUNK, t)
    full = lambda z: pl.BlockSpec(z.shape, lambda b, c: (0, 0))
    return pl.pallas_call(
        _mlstm_kernel,
        grid=(bsz, t // lc),
        in_specs=[pl.BlockSpec((1, lc, ML_MAIN), lambda b, c: (b, c, 0)),
                  pl.BlockSpec((1, lc, LANES), lambda b, c: (b, c, 0)), full(gbias), full(ng)],
        out_specs=pl.BlockSpec((1, lc, ML_DIM), lambda b, c: (b, c, 0)),
        out_shape=jax.ShapeDtypeStruct((bsz, t, ML_DIM), F32),
        scratch_shapes=[pltpu.VMEM((ML_HEADS, LANES, LANES), F32), pltpu.VMEM((ML_HEADS, 8, LANES), F32),
                        pltpu.VMEM((ML_HEADS, 8, LANES), F32)],
        compiler_params=_cparams("arbitrary", "arbitrary"),
        name="mlstm_mix",
    )(p, pg, gbias, ng)


def _route(h, g_ref, wh_ref, wl_ref, b_ref, ri_ref, rg_ref, cnt_ref, off_ref):
    tm = h.shape[0]

    @pl.when(pl.program_id(0) == 0)
    def _():
        off_ref[...] = jnp.zeros_like(off_ref)

    xn = _rms_rows(h, g_ref[...])
    xh = xn.astype(BF16)
    xl = (xn - xh.astype(F32)).astype(BF16)
    dot = lambda a, b: jnp.dot(a, b, preferred_element_type=F32)
    logits = dot(xh, wh_ref[...]) + dot(xh, wl_ref[...]) + dot(xl, wh_ref[...]) + b_ref[...]
    lane = lax.broadcasted_iota(I32, logits.shape, 1)
    big = jnp.int32(1 << 20)
    is_grp = (lane >= N_EXPERTS) & (lane < N_EXPERTS + N_GROUPS)
    lg = jnp.where(is_grp, logits, NEG)
    gmax = jnp.max(lg, axis=-1, keepdims=True)
    p_top = 1.0 / jnp.sum(jnp.exp(lg - gmax), axis=-1, keepdims=True)
    g_idx = jnp.min(jnp.where(lg == gmax, lane, big), axis=-1, keepdims=True) - N_EXPERTS
    valid = (lane < N_EXPERTS) & ((lane // EXPERTS_PER_GROUP) == g_idx)
    v1 = jnp.where(valid, logits, NEG)
    m1 = jnp.max(v1, axis=-1, keepdims=True)
    i1 = jnp.min(jnp.where(v1 == m1, lane, big), axis=-1, keepdims=True)
    v2 = jnp.where(lane == i1, NEG, v1)
    m2 = jnp.max(v2, axis=-1, keepdims=True)
    i2 = jnp.min(jnp.where(v2 == m2, lane, big), axis=-1, keepdims=True)
    e21 = jnp.exp(m2 - m1)
    gate1 = p_top / (1.0 + e21)
    gate2 = p_top * e21 / (1.0 + e21)

    sel1 = lane == i1
    sel2 = lane == i2
    onehot = jnp.where(sel1 | sel2, 1.0, 0.0)
    tr, tc = _iota2((tm, tm))
    before = jnp.where(tc < tr, 1.0, 0.0).astype(BF16)
    prefix = jnp.dot(before, onehot.astype(BF16), preferred_element_type=F32) + off_ref[0:1, :]
    rank1 = jnp.sum(jnp.where(sel1, prefix, 0.0), axis=-1, keepdims=True).astype(I32)
    rank2 = jnp.sum(jnp.where(sel2, prefix, 0.0), axis=-1, keepdims=True).astype(I32)
    total = off_ref[0:1, :] + jnp.sum(onehot, axis=0, keepdims=True)
    off_ref[...] = jnp.broadcast_to(total, off_ref.shape)
    cnt_ref[...] = jnp.broadcast_to(total, cnt_ref.shape).astype(I32)

    ri_ref[...] = jnp.where(lane == 0, i1, jnp.where(lane == 1, i2, jnp.where(lane == 2, rank1, jnp.where(lane == 3, rank2, 0))))
    rg_ref[...] = jnp.where(lane == 0, gate1, jnp.where(lane == 1, gate2, 0.0))
    return xn


def _out_proj_route_kernel(h_ref, ya_ref, yb_ref, wa_ref, wb_ref, g_ref, wh_ref, wl_ref, b_ref,
                           o_ref, xt_ref, ri_ref, rg_ref, cnt_ref, off_ref):
    out = h_ref[...] + _mm(ya_ref[...], wa_ref[...]) + _mm(yb_ref[...], wb_ref[...])
    o_ref[...] = out
    xn = _route(out, g_ref, wh_ref, wl_ref, b_ref, ri_ref, rg_ref, cnt_ref, off_ref)
    _store_row_tiles(xt_ref, xn)


def _out_proj_route(h, ya, yb, w_out, g_ffn, w_group, b_group, w_router, b_router):
    n, d = h.shape
    da, db = ya.shape[1], yb.shape[1]
    tm = min(PROJ_TM, n)
    wa = w_out[:da].astype(BF16)
    wb = w_out[da:].astype(BF16)
    pad = LANES - N_EXPERTS - N_GROUPS
    w_cat = jnp.concatenate([w_router, w_group, jnp.zeros((d, pad), F32)], axis=1)
    b_cat = jnp.concatenate([b_router, b_group, jnp.zeros((pad,), F32)]).reshape(1, LANES)
    w_hi = w_cat.astype(BF16)
    w_lo = (w_cat - w_hi.astype(F32)).astype(BF16)
    row = lambda i: (i, 0)
    fixed = lambda i: (0, 0)
    return pl.pallas_call(
        _out_proj_route_kernel,
        grid=(n // tm,),
        in_specs=[pl.BlockSpec((tm, d), row), pl.BlockSpec((tm, da), row), pl.BlockSpec((tm, db), row),
                  pl.BlockSpec((da, d), fixed), pl.BlockSpec((db, d), fixed), pl.BlockSpec((1, d), fixed),
                  pl.BlockSpec((d, LANES), fixed), pl.BlockSpec((d, LANES), fixed), pl.BlockSpec((1, LANES), fixed)],
        out_specs=[pl.BlockSpec((tm, d), row), pl.BlockSpec((tm * ROW_TILE, LANES), row),
                   pl.BlockSpec((tm, LANES), row), pl.BlockSpec((tm, LANES), row), pl.BlockSpec((8, LANES), fixed)],
        out_shape=[jax.ShapeDtypeStruct((n, d), F32), jax.ShapeDtypeStruct((n * ROW_TILE, LANES), F32),
                   jax.ShapeDtypeStruct((n, LANES), I32), jax.ShapeDtypeStruct((n, LANES), F32),
                   jax.ShapeDtypeStruct((8, LANES), I32)],
        scratch_shapes=[pltpu.VMEM((8, LANES), F32)],
        compiler_params=_cparams("arbitrary"),
        name="out_proj_route",
    )(h, ya, yb, wa, wb, g_ffn.reshape(1, d), w_hi, w_lo, b_cat)


def _dest_kernel(ri_ref, ss_ref, o_ref):
    ri = ri_ref[...]
    lane = lax.broadcasted_iota(I32, ri.shape, 1)
    ss = ss_ref[...].astype(F32)
    pick = lambda e: jnp.sum(jnp.where(lane == e, ss, 0.0), axis=-1, keepdims=True)
    d1 = pick(ri[:, 0:1]) + ri[:, 2:3].astype(F32)
    d2 = pick(ri[:, 1:2]) + ri[:, 3:4].astype(F32)
    both = jnp.where(lane == 0, d1, jnp.where(lane == 1, d2, 0.0))
    o_ref[...] = both.T[0:8, :].astype(I32)


def _dest(ri, seg_start):
    n = ri.shape[0]
    tm = min(PROJ_TM, n)
    ss = jnp.zeros((1, LANES), I32).at[0, :N_EXPERTS].set(seg_start)
    return pl.pallas_call(
        _dest_kernel,
        grid=(n // tm,),
        in_specs=[pl.BlockSpec((tm, LANES), lambda i: (i, 0)), pl.BlockSpec((1, LANES), lambda i: (0, 0))],
        out_specs=pl.BlockSpec((8, tm), lambda i: (0, i)),
        out_shape=jax.ShapeDtypeStruct((8, n), I32),
        compiler_params=_cparams("parallel"),
        name="moe_dest",
    )(ri, ss)


def _invert_kernel(d1_ref, d2_ref, o_ref):
    def clear(i, carry):
        o_ref[i] = 0
        return carry

    def fill(t, carry):
        o_ref[d1_ref[t]] = t
        o_ref[d2_ref[t]] = t
        return carry

    lax.fori_loop(0, o_ref.shape[0], clear, 0, unroll=GATHER_UNROLL)
    lax.fori_loop(0, d1_ref.shape[0], fill, 0, unroll=GATHER_UNROLL)


def _invert(d1, d2, n_rows):
    grid_spec = pltpu.PrefetchScalarGridSpec(
        num_scalar_prefetch=2, grid=(1,), in_specs=[],
        out_specs=pl.BlockSpec(memory_space=pltpu.SMEM))
    return pl.pallas_call(
        _invert_kernel,
        grid_spec=grid_spec,
        out_shape=jax.ShapeDtypeStruct((n_rows,), I32),
        compiler_params=_cparams("arbitrary"),
        name="moe_invert",
    )(d1, d2)


def _gather_rows(src_hbm, idx_ref, base, dst, sem):
    def body(j, carry):
        r = pl.multiple_of(idx_ref[base + j] * ROW_TILE, ROW_TILE)
        pltpu.make_async_copy(src_hbm.at[pl.ds(r, ROW_TILE), :],
                              dst.at[pl.ds(pl.multiple_of(j * ROW_TILE, ROW_TILE), ROW_TILE), :], sem).start()
        return carry

    lax.fori_loop(0, dst.shape[0] // ROW_TILE, body, 0, unroll=GATHER_UNROLL)


def _wait_rows(src_hbm, dst, sem):
    pltpu.make_async_copy(src_hbm.at[pl.ds(0, dst.shape[0]), :], dst, sem).wait()


def _expert_kernel(te_ref, nt_ref, src_ref, x_hbm, wg_ref, wu_ref, wd_ref, o_ref, xbuf, wgb, wub, wdb, sem):
    tm = xbuf.shape[1] // ROW_TILE
    i = pl.program_id(0)
    n_used = nt_ref[0]
    slot = i % EXPERT_BUFS

    @pl.when(i == 0)
    def _():
        for t in range(EXPERT_BUFS - 1):
            @pl.when(t < n_used)
            def _():
                _gather_rows(x_hbm, src_ref, t * tm, xbuf.at[t], sem.at[t])

    ahead = i + EXPERT_BUFS - 1

    @pl.when(ahead < n_used)
    def _():
        _gather_rows(x_hbm, src_ref, ahead * tm, xbuf.at[ahead % EXPERT_BUFS], sem.at[ahead % EXPERT_BUFS])

    @pl.when((i == 0) | (te_ref[i] != te_ref[jnp.maximum(i - 1, 0)]))
    def _():
        wgb[...] = wg_ref[0].astype(BF16)
        wub[...] = wu_ref[0].astype(BF16)
        wdb[...] = wd_ref[0].astype(BF16)

    @pl.when(i < n_used)
    def _():
        _wait_rows(x_hbm, xbuf.at[slot], sem.at[slot])
        xn = _load_row_tiles(xbuf.at[slot], tm).astype(BF16)
        gate = jnp.dot(xn, wgb[...], preferred_element_type=F32)
        up = jnp.dot(xn, wub[...], preferred_element_type=F32)
        hid = (jax.nn.silu(gate) * up).astype(BF16)
        _store_row_tiles(o_ref, jnp.dot(hid, wdb[...], preferred_element_type=F32))

    @pl.when(i >= n_used)
    def _():
        o_ref[...] = jnp.zeros_like(o_ref)


def _experts(xn_tiles, tile_expert, n_used, src, w_gate, w_up, w_down):
    d = D_MODEL
    n_tiles = tile_expert.shape[0]
    tm = src.shape[0] // n_tiles
    de = w_gate.shape[-1]
    grid_spec = pltpu.PrefetchScalarGridSpec(
        num_scalar_prefetch=3,
        grid=(n_tiles,),
        in_specs=[pl.BlockSpec(memory_space=pl.ANY),
                  pl.BlockSpec((1, d, de), lambda i, te, nt, s: (te[i], 0, 0)),
                  pl.BlockSpec((1, d, de), lambda i, te, nt, s: (te[i], 0, 0)),
                  pl.BlockSpec((1, de, d), lambda i, te, nt, s: (te[i], 0, 0))],
        out_specs=pl.BlockSpec((tm * ROW_TILE, LANES), lambda i, te, nt, s: (i, 0)),
        scratch_shapes=[pltpu.VMEM((EXPERT_BUFS, tm * ROW_TILE, LANES), F32), pltpu.VMEM((d, de), BF16),
                        pltpu.VMEM((d, de), BF16), pltpu.VMEM((de, d), BF16),
                        pltpu.SemaphoreType.DMA((EXPERT_BUFS,))],
    )
    return pl.pallas_call(
        _expert_kernel,
        grid_spec=grid_spec,
        out_shape=jax.ShapeDtypeStruct((n_tiles * tm * ROW_TILE, LANES), F32),
        compiler_params=_cparams("arbitrary"),
        name="moe_experts",
    )(tile_expert, n_used, src, xn_tiles, w_gate, w_up, w_down)


def _combine_kernel(d1_ref, d2_ref, ys_hbm, h_ref, rg_ref, gf_ref, o_ref, y1, y2, sem, *, final_norm):
    tm = h_ref.shape[0]
    i = pl.program_id(0)
    slot = i % 2

    def gather(tile, s):
        _gather_rows(ys_hbm, d1_ref, tile * tm, y1.at[s], sem.at[0, s])
        _gather_rows(ys_hbm, d2_ref, tile * tm, y2.at[s], sem.at[1, s])

    @pl.when(i == 0)
    def _():
        gather(0, 0)

    @pl.when(i + 1 < pl.num_programs(0))
    def _():
        gather(i + 1, 1 - slot)

    _wait_rows(ys_hbm, y1.at[slot], sem.at[0, slot])
    _wait_rows(ys_hbm, y2.at[slot], sem.at[1, slot])
    rg = rg_ref[...]
    out = (h_ref[...] + rg[:, 0:1] * _load_row_tiles(y1.at[slot], tm)
           + rg[:, 1:2] * _load_row_tiles(y2.at[slot], tm))
    if final_norm:
        out = _rms_rows(out, gf_ref[...])
    o_ref[...] = out


def _combine(h, ys, d1, d2, rg, g_final, final_norm):
    n, d = h.shape
    tm = min(PROJ_TM, n)
    grid_spec = pltpu.PrefetchScalarGridSpec(
        num_scalar_prefetch=2,
        grid=(n // tm,),
        in_specs=[pl.BlockSpec(memory_space=pl.ANY),
                  pl.BlockSpec((tm, d), lambda i, a, b: (i, 0)),
                  pl.BlockSpec((tm, LANES), lambda i, a, b: (i, 0)),
                  pl.BlockSpec((1, d), lambda i, a, b: (0, 0))],
        out_specs=pl.BlockSpec((tm, d), lambda i, a, b: (i, 0)),
        scratch_shapes=[pltpu.VMEM((2, tm * ROW_TILE, LANES), F32), pltpu.VMEM((2, tm * ROW_TILE, LANES), F32),
                        pltpu.SemaphoreType.DMA((2, 2))],
    )
    return pl.pallas_call(
        functools.partial(_combine_kernel, final_norm=final_norm),
        grid_spec=grid_spec,
        out_shape=jax.ShapeDtypeStruct((n, d), F32),
        compiler_params=_cparams("arbitrary"),
        name="moe_combine",
    )(d1, d2, ys, h, rg, g_final.reshape(1, d))


def _hmoe_residual(h, xn_tiles, ri, rg, cnt, w_gate, w_up, w_down, g_final, final_norm):
    n, d = h.shape

    counts = cnt[0, :N_EXPERTS]
    n_tiles = (2 * n) // MOE_TM + N_EXPERTS
    tiles_per = (counts + MOE_TM - 1) // MOE_TM
    tile_end = jnp.cumsum(tiles_per)
    seg_start = (tile_end - tiles_per) * MOE_TM
    dest = _dest(ri, seg_start.astype(I32))
    d1, d2 = dest[0], dest[1]
    tile_ids = jnp.arange(n_tiles, dtype=I32)
    tile_expert = jnp.minimum(jnp.sum((tile_end[None, :] <= tile_ids[:, None]).astype(I32), axis=1), N_EXPERTS - 1)
    src = _invert(d1, d2, n_tiles * MOE_TM)

    n_used = tile_end[N_EXPERTS - 1:].astype(I32)
    ys = _experts(xn_tiles, tile_expert, n_used, src, w_gate, w_up, w_down)
    return _combine(h, ys, d1, d2, rg, g_final, final_norm)


def kernel(x, norm_mix, norm_ffn, norm_final, ev_w_in, ev_mu, rw_w0, rw_w2, rw_a0, rw_a2, rw_g2, rw_k_k, rw_k_a, rw_r_k, rw_ln_w, rw_ln_b, hg_lb_logits, hg_norm, ev_w_out, od_w_in, gd_conv, gd_a_log, gd_dt_bias, gd_norm, ml_i_bias, ml_f_bias, ml_norm, od_w_out, moe_w_group, moe_b_group, moe_w_router, moe_b_router, moe_w_gate, moe_w_up, moe_w_down):
    bsz, t, d = x.shape
    n = bsz * t
    depth = norm_mix.shape[0]
    lb_table = jnp.cumsum(jax.nn.softmax(hg_lb_logits.astype(F32), axis=0), axis=0)
    h = x.reshape(n, d)
    for layer in range(depth):
        j = layer // 2
        if layer % 2 == 0:
            w_in = ev_w_in[j].astype(BF16)
            p_rw, p_hg = _rms_proj(h, norm_mix[layer], [w_in[:, :RW_IN], w_in[:, RW_IN:]])
            ya = _rwkv7(p_rw.reshape(bsz, t, RW_IN), ev_mu[j], rw_w0[j], rw_w2[j], rw_a0[j], rw_a2[j], rw_g2[j],
                        rw_k_k[j], rw_k_a[j], rw_r_k[j], rw_ln_w[j], rw_ln_b[j])
            yb = _hgrn2(p_hg.reshape(bsz, t, HG_IN), lb_table[j], hg_norm[j])
            mix_a, mix_b, w_out = ya.reshape(n, RW_DIM), yb.reshape(n, HG_KDIM), ev_w_out[j]
        else:
            w_in = od_w_in[j]
            w_gates = jnp.concatenate([w_in[:, GD_MAIN:GD_IN], w_in[:, GD_IN + ML_MAIN:],
                                       jnp.zeros((d, LANES - 2 * GD_HEADS - 2 * ML_HEADS), F32)], axis=1)
            p_gd, p_ml, p_gt = _rms_proj(h, norm_mix[layer], [w_in[:, :GD_MAIN].astype(BF16),
                                                              w_in[:, GD_IN:GD_IN + ML_MAIN].astype(BF16),
                                                              w_gates.astype(BF16)])
            p_gt = p_gt.reshape(bsz, t, LANES)
            yc = _gdn(p_gd.reshape(bsz, t, GD_MAIN), p_gt, gd_conv[j], gd_a_log[j], gd_dt_bias[j], gd_norm[j])
            yd = _mlstm(p_ml.reshape(bsz, t, ML_MAIN), p_gt, ml_i_bias[j], ml_f_bias[j], ml_norm[j])
            mix_a, mix_b, w_out = yc.reshape(n, GD_DIM), yd.reshape(n, ML_DIM), od_w_out[j]
        h, xn_tiles, ri, rg, cnt = _out_proj_route(h, mix_a, mix_b, w_out, norm_ffn[layer], moe_w_group[layer],
                                                   moe_b_group[layer], moe_w_router[layer], moe_b_router[layer])
        h = _hmoe_residual(h, xn_tiles, ri, rg, cnt, moe_w_gate[layer], moe_w_up[layer], moe_w_down[layer],
                           norm_final, final_norm=(layer == depth - 1))
    return h.reshape(bsz, t, d)
```

```python
import functools
import math

import jax
import jax.numpy as jnp
from jax import lax
from jax.experimental import pallas as pl
from jax.experimental.pallas import tpu as pltpu

F32 = jnp.float32
BF16 = jnp.bfloat16
I32 = jnp.int32
HIGHEST = lax.Precision.HIGHEST

D_MODEL = 1024
NORM_EPS = 1e-6
RW_HEADS, RW_HEAD_DIM = 8, 64
RW_DIM = RW_HEADS * RW_HEAD_DIM
R_DECAY, R_AAA, R_GATE = 64, 64, 128
RW_IN = 3 * RW_DIM + R_DECAY + R_AAA + R_GATE
RW_LN_EPS = 64e-5
HG_HEADS, HG_DIM = 8, 64
HG_KDIM = HG_HEADS * HG_DIM
HG_IN = 4 * HG_KDIM
GD_HEADS, GD_HEAD_DIM = 4, 128
GD_DIM = GD_HEADS * GD_HEAD_DIM
CONV_K = 4
GD_MAIN = 4 * GD_DIM
GD_IN = GD_MAIN + 2 * GD_HEADS
ML_HEADS, ML_HEAD_DIM = 4, 128
ML_DIM = ML_HEADS * ML_HEAD_DIM
ML_MAIN = 4 * ML_DIM
N_GROUPS, EXPERTS_PER_GROUP = 4, 8
N_EXPERTS = N_GROUPS * EXPERTS_PER_GROUP
D_EXPERT = 256

LANES = 128
VMEM_LIMIT_BYTES = 48 * 1024 * 1024

PROJ_TM = 256
CHUNK = 64
SUB = 16
RW_NCH = 4
GD_NCH = 4
HG_ROWS = 128
ML_CHUNK = 128
MOE_TM = 256
EXPERT_BUFS = 3
GATHER_UNROLL = 8
GATHER_PRIORITY = 1
NEG = -1e30


def _cparams(*sem):
    return pltpu.CompilerParams(dimension_semantics=sem, vmem_limit_bytes=VMEM_LIMIT_BYTES)


def _mm(a, b):
    return jnp.dot(a.astype(BF16), b.astype(BF16), preferred_element_type=F32)


def _mm_nt(a, b):
    return lax.dot_general(a.astype(BF16), b.astype(BF16), (((1,), (1,)), ((), ())), preferred_element_type=F32)


def _mm_tn(a, b):
    return lax.dot_general(a.astype(BF16), b.astype(BF16), (((0,), (0,)), ((), ())), preferred_element_type=F32)


def _mm_hi(a, b):
    return jnp.dot(a, b, precision=HIGHEST, preferred_element_type=F32)


def _mm_split(x, ones_bf16):
    hi = x.astype(BF16)
    lo = (x - hi.astype(F32)).astype(BF16)
    return (jnp.dot(hi, ones_bf16, preferred_element_type=F32) + jnp.dot(lo, ones_bf16, preferred_element_type=F32))


def _iota2(shape):
    return lax.broadcasted_iota(I32, shape, 0), lax.broadcasted_iota(I32, shape, 1)


def _rms_rows(x, g, eps=NORM_EPS):
    return x * lax.rsqrt(jnp.mean(x * x, axis=-1, keepdims=True) + eps) * g


def _seg_ones(width, seg):
    r, c = _iota2((width, width))
    return jnp.where((r // seg) == (c // seg), 1.0, 0.0).astype(BF16)


def _split3(x):
    x1 = x.astype(BF16)
    r1 = x - x1.astype(F32)
    x2 = r1.astype(BF16)
    return x1, x2, (r1 - x2.astype(F32)).astype(BF16)


def _cumsum_rows(tri_bf16, x):
    return sum(jnp.dot(tri_bf16, t, preferred_element_type=F32) for t in _split3(x))


def _cumsum_cols(x, tri_bf16):
    return sum(jnp.dot(t, tri_bf16, preferred_element_type=F32) for t in _split3(x))


def _mm3(a, b):
    ah = a.astype(BF16)
    al = (a - ah.astype(F32)).astype(BF16)
    bh = b.astype(BF16)
    bl = (b - bh.astype(F32)).astype(BF16)
    dot = lambda x, y: jnp.dot(x, y, preferred_element_type=F32)
    return dot(ah, bh) + dot(ah, bl) + dot(al, bh)


def _tri_inv_multi(ms, n, chain):
    assert chain // SUB <= 4
    r, c = _iota2((n, n))
    same = (r // SUB) == (c // SUB)
    eye = jnp.where(r == c, 1.0, 0.0).astype(F32)
    ds = [jnp.where(same, m, 0.0) for m in ms]
    offs = [m - d for m, d in zip(ms, ds)]
    xs = [eye - d for d in ds]
    ps = ds
    for _ in range(3):
        ps = [_mm(p, p) for p in ps]
        xs = [x + _mm(x, p) for x, p in zip(xs, ps)]
    es = [_mm(x, o) for x, o in zip(xs, offs)]
    imes = [eye - e for e in es]
    e2s = [_mm(e, e) for e in es]
    ys = [i + _mm(i, e2) for i, e2 in zip(imes, e2s)]
    xs = [_mm(y, x) for y, x in zip(ys, xs)]
    res = [eye - x - _mm3(m, x) for m, x in zip(ms, xs)]
    return [x + _mm(x, rr) for x, rr in zip(xs, res)]


def _rms_proj_kernel(x_ref, g_ref, *refs, n_out):
    y = _rms_rows(x_ref[...], g_ref[...]).astype(BF16)
    for w_ref, o_ref in zip(refs[:n_out], refs[n_out:]):
        o_ref[...] = jnp.dot(y, w_ref[...], preferred_element_type=F32)


def _rms_proj(x, g, ws):
    n, d = x.shape
    tm = min(PROJ_TM, n)
    in_specs = [pl.BlockSpec((tm, d), lambda i: (i, 0)), pl.BlockSpec((1, d), lambda i: (0, 0))]
    in_specs += [pl.BlockSpec(w.shape, lambda i: (0, 0)) for w in ws]
    return pl.pallas_call(
        functools.partial(_rms_proj_kernel, n_out=len(ws)),
        grid=(n // tm,),
        in_specs=in_specs,
        out_specs=[pl.BlockSpec((tm, w.shape[1]), lambda i: (i, 0)) for w in ws],
        out_shape=[jax.ShapeDtypeStruct((n, w.shape[1]), F32) for w in ws],
        compiler_params=_cparams("parallel"),
        name="rms_proj",
    )(x, g.reshape(1, d), *ws)


ROW_TILE = D_MODEL // LANES


def _store_row_tiles(ref, x):
    for j in range(ROW_TILE):
        ref[pl.ds(j, x.shape[0], stride=ROW_TILE), :] = x[:, j * LANES:(j + 1) * LANES]


def _load_row_tiles(ref, rows):
    return jnp.concatenate([ref[pl.ds(j, rows, stride=ROW_TILE), :] for j in range(ROW_TILE)], axis=1)


def _rwkv7_kernel(p_ref, mu_ref, w0_ref, w2_ref, a0_ref, a2_ref, g2_ref, kk_ref, ka_ref, rk_ref,
                  lnw_ref, lnb_ref, o_ref, prev_ref, zt_ref):
    L = CHUNK
    npair = RW_HEADS // 2
    c = pl.program_id(1)

    @pl.when(c == 0)
    def _():
        prev_ref[...] = jnp.zeros_like(prev_ref)
        zt_ref[...] = jnp.zeros_like(zt_ref)

    x = p_ref[0]
    R = x.shape[0]
    nch = R // L
    row = lax.broadcasted_iota(I32, x.shape, 0)
    xs = jnp.where(row == 0, prev_ref[7:8, :], pltpu.roll(x, 1, 0))
    prev_ref[...] = x[R - 8:R, :]
    pm = x + mu_ref[...] * (xs - x)
    r_all = pm[:, 0:RW_DIM]
    k_all = pm[:, RW_DIM:2 * RW_DIM]
    v_all = pm[:, 2 * RW_DIM:3 * RW_DIM]
    wa = pm[:, 3 * RW_DIM:3 * RW_DIM + LANES]
    gl = pm[:, 3 * RW_DIM + LANES:]
    wlog = -jax.nn.softplus(-(w0_ref[...] + _mm(jnp.tanh(wa), w2_ref[...]))) - 0.5
    ld = -jnp.exp(wlog)
    a_all = jax.nn.sigmoid(a0_ref[...] + _mm(wa, a2_ref[...]))
    g_all = _mm(jax.nn.sigmoid(gl), g2_ref[...])

    tr, tc = _iota2((R, R))
    chunk_tril = jnp.where((tc <= tr) & ((tr // L) == (tc // L)), 1.0, 0.0).astype(BF16)
    cs_all = _cumsum_rows(chunk_tril, ld)
    seg = _seg_ones(LANES, RW_HEAD_DIM)
    lane = lax.broadcasted_iota(I32, (L, LANES), 1)
    hm = (lane < RW_HEAD_DIM, lane >= RW_HEAD_DIM)
    br, bc = _iota2((2 * L, 2 * L))
    bd = (br // L) == (bc // L)
    bd_strict = bd & (bc < br)
    bd_incl = bd & (bc <= br)
    fold = lambda z: z[0:L] + z[L:2 * L]
    both = lambda z: jnp.concatenate([jnp.where(hm[0], z, 0.0), jnp.where(hm[1], z, 0.0)], axis=0)

    units = []
    for ci in range(nch):
        rs = slice(ci * L, (ci + 1) * L)
        for p in range(npair):
            sl = slice(p * LANES, (p + 1) * LANES)
            r, k, v, a = r_all[rs, sl], k_all[rs, sl], v_all[rs, sl], a_all[rs, sl]
            cs, ldp = cs_all[rs, sl], ld[rs, sl]
            kkr = k * kk_ref[:, sl]
            kk = kkr * lax.rsqrt(_mm_split(kkr * kkr, seg) + 1e-6)
            k2 = k * (1.0 + (a - 1.0) * ka_ref[:, sl])
            b = kk * a
            cs_last = cs[L - 1:L, :]
            e_neg = jnp.exp(-cs)
            e_rem = jnp.exp(cs_last - cs)
            bhat = b * e_neg
            khat = k2 * e_neg
            units.append(dict(p=p, rs=rs, sl=sl, r=r, v=v, k2=k2, rhat=r * jnp.exp(cs), gam_last=jnp.exp(cs_last),
                              btil=b * e_rem, ktil=k2 * e_rem, a2=both(kk * jnp.exp(cs - ldp)), v2=both(v),
                              rhs4=jnp.concatenate([bhat, bhat, khat, khat], axis=0)))
    for q in units:
        lhs = jnp.concatenate([q["a2"], both(q["rhat"])], axis=0)
        q["g"] = _mm_nt(lhs, q["rhs4"])
    tinvs = _tri_inv_multi([jnp.where(bd_strict, q["g"][0:2 * L, 0:2 * L], 0.0) for q in units], 2 * L, L)
    for q, tinv in zip(units, tinvs):
        q["tinv"] = tinv
        q["x2"] = _mm(jnp.where(bd_strict, q["g"][0:2 * L, 2 * L:4 * L], 0.0), q["v2"])
    for q in units:
        uw = _mm(q["tinv"], jnp.concatenate([q["x2"], q["a2"]], axis=1))
        q["u0"] = -fold(uw[:, 0:LANES])
        q["w"] = fold(uw[:, LANES:])
        q["y0"] = fold(_mm(jnp.where(bd_incl, q["g"][2 * L:4 * L, 2 * L:4 * L], 0.0), q["v2"]))
    for q in units:
        rb = jnp.where(bd_incl, q["g"][2 * L:4 * L, 0:2 * L], 0.0)
        ruw = _mm(rb, jnp.concatenate([both(q["u0"]), both(q["w"])], axis=1))
        q["yc"] = q["y0"] + fold(ruw[:, 0:LANES])
        q["ry"] = q["rhat"] - fold(ruw[:, LANES:])
        q["c1"] = _mm_tn(jnp.concatenate([q["u0"], q["v"]], axis=0), jnp.concatenate([q["btil"], q["ktil"]], axis=0))
        q["c2"] = _mm_tn(q["w"], q["btil"])

    hr, hc = _iota2((LANES, LANES))
    head_bd = (hr // RW_HEAD_DIM) == (hc // RW_HEAD_DIM)
    zts = [zt_ref[p] for p in range(npair)]
    for ci in range(nch):
        qs = units[ci * npair:(ci + 1) * npair]
        for q, zt in zip(qs, zts):
            q["y"] = q["yc"] + _mm_nt(q["ry"], zt)
        zts = [zt * q["gam_last"] + jnp.where(head_bd, q["c1"] - _mm(zt, q["c2"]), 0.0) for q, zt in zip(qs, zts)]
    for p in range(npair):
        zt_ref[p] = zts[p]

    for q in units:
        sl, rs, y = q["sl"], q["rs"], q["y"]
        mean = _mm_split(y, seg) * (1.0 / RW_HEAD_DIM)
        yc = y - mean
        var = _mm_split(yc * yc, seg) * (1.0 / RW_HEAD_DIM)
        yn = yc * lax.rsqrt(var + RW_LN_EPS) * lnw_ref[:, sl] + lnb_ref[:, sl]
        bonus = _mm_split(q["r"] * q["k2"] * rk_ref[:, sl], seg) * q["v"]
        o_ref[0, rs, sl] = (yn + bonus) * g_all[rs, sl]


def _rwkv7(p, mu, w0, w2, a0, a2, g2, k_k, k_a, r_k, ln_w, ln_b):
    bsz, t, _ = p.shape
    row = lambda z: z.reshape(1, -1).astype(F32)
    w2p = jnp.concatenate([w2, jnp.zeros_like(w2)], axis=0).astype(BF16)
    a2p = jnp.concatenate([jnp.zeros_like(a2), a2], axis=0).astype(BF16)
    params = [row(mu), row(w0), w2p, row(a0), a2p, g2.astype(BF16), row(k_k), row(k_a), row(r_k), row(ln_w), row(ln_b)]
    full = lambda z: pl.BlockSpec(z.shape, lambda b, c: (0, 0))
    rows = RW_NCH * CHUNK
    return pl.pallas_call(
        _rwkv7_kernel,
        grid=(bsz, t // rows),
        in_specs=[pl.BlockSpec((1, rows, RW_IN), lambda b, c: (b, c, 0))] + [full(z) for z in params],
        out_specs=pl.BlockSpec((1, rows, RW_DIM), lambda b, c: (b, c, 0)),
        out_shape=jax.ShapeDtypeStruct((bsz, t, RW_DIM), F32),
        scratch_shapes=[pltpu.VMEM((8, RW_IN), F32), pltpu.VMEM((RW_HEADS // 2, LANES, LANES), F32)],
        compiler_params=_cparams("arbitrary", "arbitrary"),
        name="rwkv7_mix",
    )(p, *params)


def _hgrn2_kernel(p_ref, lb_ref, ng_ref, o_ref, st_ref):
    L = HG_ROWS
    c = pl.program_id(1)

    @pl.when(c == 0)
    def _():
        st_ref[...] = jnp.zeros_like(st_ref)

    x = p_ref[0]
    lb = lb_ref[...]
    q_all = jax.nn.silu(x[:, 0:HG_KDIM])
    fg = lb + (1.0 - lb) * jax.nn.sigmoid(x[:, HG_KDIM:2 * HG_KDIM])
    k_all = 1.0 - fg
    logf = jnp.log(fg)
    v_all = x[:, 2 * HG_KDIM:3 * HG_KDIM]
    gate = x[:, 3 * HG_KDIM:]
    tr, tc = _iota2((L, L))
    blk_tril = jnp.where((tc <= tr) & ((tr // SUB) == (tc // SUB)), 1.0, 0.0).astype(BF16)
    bc_all = _cumsum_rows(blk_tril, logf)
    seg = _seg_ones(LANES, HG_DIM)
    br, bcc = _iota2((LANES, LANES))
    bd = (br // HG_DIM) == (bcc // HG_DIM)
    half = SUB // 2
    t_lo = lax.broadcasted_iota(I32, (SUB, LANES), 0)
    t_hi = lax.broadcasted_iota(I32, (half, LANES), 0) + half

    nsub = L // SUB
    npair = HG_HEADS // 2
    rows_per_unit = half * SUB + half * half
    sub_of_row = lax.broadcasted_iota(I32, (L, LANES), 0) // SUB

    units = []
    parts = []
    for p in range(npair):
        sl = slice(p * LANES, (p + 1) * LANES)
        for j in range(nsub):
            rs = slice(j * SUB, (j + 1) * SUB)
            q, k, v, bc = q_all[rs, sl], k_all[rs, sl], v_all[rs, sl], bc_all[rs, sl]
            q_hi, bc_hi = q[half:], bc[half:]
            for s in range(SUB):
                if s < half:
                    diff = jnp.where(t_lo >= s, bc - bc[s:s + 1, :], NEG)
                    parts.append(jnp.exp(diff) * q * k[s:s + 1, :])
                else:
                    diff = jnp.where(t_hi >= s, bc_hi - bc[s:s + 1, :], NEG)
                    parts.append(jnp.exp(diff) * q_hi * k[s:s + 1, :])
            bend = bc[SUB - 1:SUB, :]
            units.append(dict(v=v, qt=q * jnp.exp(bc), dec=jnp.exp(bend), kd=k * jnp.exp(bend - bc)))
    score_all = _mm(jnp.concatenate(parts, axis=0), seg)
    for ui, un in enumerate(units):
        score = score_all[ui * rows_per_unit:(ui + 1) * rows_per_unit]
        v = un["v"]
        acc_lo = jnp.zeros((half, LANES), F32)
        acc_hi = jnp.zeros((half, LANES), F32)
        off = 0
        for s in range(SUB):
            vs = v[s:s + 1, :]
            if s < half:
                acc_lo = acc_lo + score[off:off + half] * vs
                acc_hi = acc_hi + score[off + half:off + SUB] * vs
                off += SUB
            else:
                acc_hi = acc_hi + score[off:off + half] * vs
                off += half
        un["intra"] = jnp.concatenate([acc_lo, acc_hi], axis=0)
    for p in range(npair):
        us = units[p * nsub:(p + 1) * nsub]
        v_pair = jnp.concatenate([un["v"] for un in us], axis=0)
        kd_pair = jnp.concatenate([un["kd"] for un in us], axis=0)
        kd_wide = jnp.concatenate([jnp.where(sub_of_row == j, kd_pair, 0.0) for j in range(nsub)], axis=1)
        upd = _mm_tn(v_pair, kd_wide)
        for j, un in enumerate(us):
            un["upd"] = jnp.where(bd, upd[:, j * LANES:(j + 1) * LANES], 0.0)

    for p in range(npair):
        sl = slice(p * LANES, (p + 1) * LANES)
        st = st_ref[p]
        outs = []
        for un in units[p * nsub:(p + 1) * nsub]:
            outs.append(un["intra"] + _mm_nt(un["qt"], st))
            st = st * un["dec"] + un["upd"]
        st_ref[p] = st
        o = jnp.concatenate(outs, axis=0)
        ms = _mm_split(o * o, seg) * (1.0 / HG_DIM)
        o_ref[0, :, sl] = o * lax.rsqrt(ms + NORM_EPS) * ng_ref[:, sl] * jax.nn.silu(gate[:, sl])


def _hgrn2(p, lb, norm_g):
    bsz, t, _ = p.shape
    lb = lb.reshape(1, HG_KDIM).astype(F32)
    ng = jnp.tile(norm_g.astype(F32), HG_HEADS).reshape(1, HG_KDIM)
    return pl.pallas_call(
        _hgrn2_kernel,
        grid=(bsz, t // HG_ROWS),
        in_specs=[pl.BlockSpec((1, HG_ROWS, HG_IN), lambda b, c: (b, c, 0)),
                  pl.BlockSpec((1, HG_KDIM), lambda b, c: (0, 0)), pl.BlockSpec((1, HG_KDIM), lambda b, c: (0, 0))],
        out_specs=pl.BlockSpec((1, HG_ROWS, HG_KDIM), lambda b, c: (b, c, 0)),
        out_shape=jax.ShapeDtypeStruct((bsz, t, HG_KDIM), F32),
        scratch_shapes=[pltpu.VMEM((HG_HEADS // 2, LANES, LANES), F32)],
        compiler_params=_cparams("arbitrary", "arbitrary"),
        name="hgrn2_mix",
    )(p, lb, ng)


def _gdn_kernel(p_ref, pg_ref, cw_ref, gb_ref, nal_ref, nalc_ref, ng_ref, o_ref, prev_ref, s_ref):
    L = CHUNK
    c = pl.program_id(1)

    @pl.when(c == 0)
    def _():
        prev_ref[...] = jnp.zeros_like(prev_ref)
        s_ref[...] = jnp.zeros_like(s_ref)

    x = p_ref[0]
    R = x.shape[0]
    nch = R // L
    xq = x[:, 0:3 * GD_DIM]
    xcat = jnp.concatenate([prev_ref[...], xq], axis=0)
    conv = xq * cw_ref[CONV_K - 1:CONV_K, :]
    for j in range(1, CONV_K):
        conv = conv + xcat[8 - j:8 - j + R, :] * cw_ref[CONV_K - 1 - j:CONV_K - j, :]
    prev_ref[...] = xq[R - 8:R, :]
    qkv = jax.nn.silu(conv)
    z = x[:, 3 * GD_DIM:]

    gt = pg_ref[0] + gb_ref[...]
    beta_all = jax.nn.sigmoid(gt)
    g_all = nal_ref[...] * jax.nn.softplus(gt)
    g_t = nalc_ref[...] * jax.nn.softplus(gt.T)
    tr, tc = _iota2((R, R))
    chunk_tril = jnp.where((tc <= tr) & ((tr // L) == (tc // L)), 1.0, 0.0).astype(BF16)
    gam_all = _cumsum_rows(chunk_tril, g_all)
    ur, uc = _iota2((R, 2 * R))
    triu2 = jnp.where(((ur // L) == (uc // (2 * L))) & ((ur % L) <= (uc % L)), 1.0, 0.0).astype(BF16)
    gam_t2 = _cumsum_cols(g_t, triu2)
    lr, lc = _iota2((L, L))
    incl = lc <= lr
    br, bc = _iota2((2 * L, 2 * L))
    bd = (br // L) == (bc // L)
    bd_strict = bd & (bc < br)
    bd_incl = bd & (bc <= br)
    lane2 = lax.broadcasted_iota(I32, (1, 2 * L), 1)
    zero = jnp.zeros((L, LANES), F32)

    units = []
    for ci in range(nch):
        rs = slice(ci * L, (ci + 1) * L)
        for h in range(GD_HEADS):
            q = qkv[rs, h * LANES:(h + 1) * LANES]
            k = qkv[rs, GD_DIM + h * LANES:GD_DIM + (h + 1) * LANES]
            v = qkv[rs, 2 * GD_DIM + h * LANES:2 * GD_DIM + (h + 1) * LANES]
            q = q * lax.rsqrt(jnp.sum(q * q, axis=-1, keepdims=True) + 1e-6) * (GD_HEAD_DIM ** -0.5)
            k = k * lax.rsqrt(jnp.sum(k * k, axis=-1, keepdims=True) + 1e-6)
            beta = beta_all[rs, h:h + 1]
            gam = gam_all[rs, GD_HEADS + h:GD_HEADS + h + 1]
            gam_row2 = gam_t2[GD_HEADS + h:GD_HEADS + h + 1, ci * 2 * L:(ci + 1) * 2 * L]
            gam_last = gam[L - 1:L, :]
            kb = k * beta
            units.append(dict(h=h, rs=rs, q=q, k=k, kb=kb, vb=v * beta, gam=gam, gam_row2=gam_row2,
                              gam_last=gam_last, kg=kb * jnp.exp(gam), qg=q * jnp.exp(gam),
                              kd=k * jnp.exp(gam_last - gam),
                              decay=jnp.exp(jnp.where(incl, gam - gam_row2[:, 0:L], NEG))))
    pairs = [(units[i], units[i + 1]) for i in range(0, len(units), 2)]
    ms = []
    for h0, h1 in pairs:
        lhs = jnp.concatenate([jnp.concatenate([h0["kb"], zero], axis=1),
                               jnp.concatenate([zero, h1["kb"]], axis=1)], axis=0)
        rhs = jnp.concatenate([jnp.concatenate([h0["k"], zero], axis=1),
                               jnp.concatenate([zero, h1["k"]], axis=1)], axis=0)
        gam_col = jnp.concatenate([h0["gam"], h1["gam"]], axis=0)
        gam_row = jnp.where(lane2 < L, h0["gam_row2"], h1["gam_row2"])
        decay2 = jnp.exp(jnp.where(bd_incl, gam_col - gam_row, NEG))
        ms.append(jnp.where(bd_strict, _mm_nt(lhs, rhs) * decay2, 0.0))
    tinvs = _tri_inv_multi(ms, 2 * L, L)
    for (h0, h1), tinv in zip(pairs, tinvs):
        rhs = jnp.concatenate([jnp.concatenate([h0["vb"], h0["kg"]], axis=1),
                               jnp.concatenate([h1["vb"], h1["kg"]], axis=1)], axis=0)
        uw = _mm(tinv, rhs)
        h0["u"], h0["w"] = uw[0:L, 0:LANES], uw[0:L, LANES:]
        h1["u"], h1["w"] = uw[L:2 * L, 0:LANES], uw[L:2 * L, LANES:]
    for hd in units:
        attn = _mm_nt(hd["q"], hd["k"]) * hd["decay"]
        uw = jnp.concatenate([hd["u"], hd["w"]], axis=1)
        auw = _mm(attn, uw)
        hd["o0"] = auw[:, 0:LANES]
        hd["qs"] = hd["qg"] - auw[:, LANES:]
        cc = _mm_tn(hd["kd"], uw)
        hd["c1"], hd["c2"] = cc[:, 0:LANES], cc[:, LANES:]

    ss = [s_ref[h] for h in range(GD_HEADS)]
    for ci in range(nch):
        hds = units[ci * GD_HEADS:(ci + 1) * GD_HEADS]
        for hd, s in zip(hds, ss):
            hd["o"] = hd["o0"] + _mm(hd["qs"], s)
        ss = [s * jnp.exp(hd["gam_last"]) + hd["c1"] - _mm(hd["c2"], s) for hd, s in zip(hds, ss)]
    for h in range(GD_HEADS):
        s_ref[h] = ss[h]
    for hd in units:
        sl = slice(hd["h"] * LANES, (hd["h"] + 1) * LANES)
        o_ref[0, hd["rs"], sl] = _rms_rows(hd["o"], ng_ref[...]) * jax.nn.silu(z[hd["rs"], sl])


def _gate_row(lo, vals):
    return jnp.zeros((1, LANES), F32).at[0, lo:lo + vals.shape[0]].set(vals.astype(F32))


def _gdn(p, pg, conv_w, a_log, dt_bias, norm_g):
    bsz, t, _ = p.shape
    gbias = _gate_row(GD_HEADS, dt_bias)
    nal = _gate_row(GD_HEADS, -jnp.exp(a_log))
    nal_col = nal.reshape(LANES, 1)
    ng = norm_g.reshape(1, GD_HEAD_DIM).astype(F32)
    full = lambda z: pl.BlockSpec(z.shape, lambda b, c: (0, 0))
    rows = GD_NCH * CHUNK
    return pl.pallas_call(
        _gdn_kernel,
        grid=(bsz, t // rows),
        in_specs=[pl.BlockSpec((1, rows, GD_MAIN), lambda b, c: (b, c, 0)),
                  pl.BlockSpec((1, rows, LANES), lambda b, c: (b, c, 0)),
                  full(conv_w), full(gbias), full(nal), full(nal_col), full(ng)],
        out_specs=pl.BlockSpec((1, rows, GD_DIM), lambda b, c: (b, c, 0)),
        out_shape=jax.ShapeDtypeStruct((bsz, t, GD_DIM), F32),
        scratch_shapes=[pltpu.VMEM((8, 3 * GD_DIM), F32), pltpu.VMEM((GD_HEADS, LANES, LANES), F32)],
        compiler_params=_cparams("arbitrary", "arbitrary"),
        name="gdn_mix",
    )(p, pg, conv_w.astype(F32), gbias, nal, nal_col, ng)


def _mlstm_kernel(p_ref, pg_ref, gb_ref, ng_ref, o_ref, c_ref, n_ref, m_ref):
    L = ML_CHUNK
    ci = pl.program_id(1)

    @pl.when(ci == 0)
    def _():
        c_ref[...] = jnp.zeros_like(c_ref)
        n_ref[...] = jnp.zeros_like(n_ref)
        m_ref[...] = jnp.zeros_like(m_ref)

    x = p_ref[0]
    gt = pg_ref[0] + gb_ref[...]
    logf = jax.nn.log_sigmoid(gt)
    gt_t = gt.T
    logf_t = jax.nn.log_sigmoid(gt_t)
    tr, tc = _iota2((L, L))
    bc_all = _cumsum_rows(jnp.where(tc <= tr, 1.0, 0.0).astype(BF16), logf)
    bc_t = _cumsum_cols(logf_t, jnp.where(tr <= tc, 1.0, 0.0).astype(BF16))
    incl = tc <= tr
    i_lo, f_lo = 2 * GD_HEADS, 2 * GD_HEADS + ML_HEADS

    heads = []
    for h in range(ML_HEADS):
        sl = slice(h * LANES, (h + 1) * LANES)
        q = x[:, sl]
        k = x[:, ML_DIM + h * LANES:ML_DIM + (h + 1) * LANES] * (ML_HEAD_DIM ** -0.5)
        v = x[:, 2 * ML_DIM + h * LANES:2 * ML_DIM + (h + 1) * LANES]
        bc = bc_all[:, f_lo + h:f_lo + h + 1]
        bc_row = bc_t[f_lo + h:f_lo + h + 1, :]
        ig = gt[:, i_lo + h:i_lo + h + 1]
        ig_row = gt_t[i_lo + h:i_lo + h + 1, :]
        m_prev = m_ref[h][0:1, 0:1]
        b_last = bc[L - 1:L, :]
        d_log = jnp.where(incl, bc - bc_row + ig_row, NEG)
        inter_log = bc + m_prev
        m_t = jnp.maximum(inter_log, jnp.max(d_log, axis=-1, keepdims=True))
        upd_log = b_last - bc + ig
        m_new = jnp.maximum(b_last + m_prev, jnp.max(upd_log, axis=0, keepdims=True))
        wk = jnp.exp(upd_log - m_new) * k
        heads.append(dict(sl=sl, q=q, v=v, m_t=m_t, m_new=m_new, wk=wk, inter_w=jnp.exp(inter_log - m_t),
                          dec=jnp.exp(b_last + m_prev - m_new), sc=_mm_nt(q, k) * jnp.exp(d_log - m_t)))
    for hd in heads:
        hd["scv"] = _mm(hd["sc"], hd["v"])
        hd["kv"] = _mm_tn(hd["wk"], hd["v"])
    for h, hd in enumerate(heads):
        cmat = c_ref[h]
        nvec = n_ref[h][0:1, :]
        q, sc, inter_w, dec = hd["q"], hd["sc"], hd["inter_w"], hd["dec"]
        num = inter_w * _mm(q, cmat) + hd["scv"]
        den = inter_w * jnp.sum(q * nvec, axis=-1, keepdims=True) + jnp.sum(sc, axis=-1, keepdims=True)
        hd["hh"] = num / jnp.maximum(jnp.abs(den), jnp.exp(-hd["m_t"]))
        c_ref[h] = dec * cmat + hd["kv"]
        n_ref[h] = jnp.broadcast_to(dec * nvec + jnp.sum(hd["wk"], axis=0, keepdims=True), (8, LANES))
        m_ref[h] = jnp.broadcast_to(hd["m_new"], (8, LANES))
    for h, hd in enumerate(heads):
        sl = hd["sl"]
        og = jax.nn.sigmoid(x[:, 3 * ML_DIM + h * LANES:3 * ML_DIM + (h + 1) * LANES])
        o_ref[0, :, sl] = og * _rms_rows(hd["hh"], ng_ref[:, sl])


def _mlstm(p, pg, i_bias, f_bias, norm_g):
    bsz, t, _ = p.shape
    gbias = _gate_row(2 * GD_HEADS, jnp.concatenate([i_bias, f_bias]))
    ng = norm_g.reshape(1, ML_DIM).astype(F32)
    lc = min(ML_CHUNK, t)
    full = lambda z: pl.BlockSpec(z.shape, lambda b, c: (0, 0))
    return pl.pallas_call(
        _mlstm_kernel,
        grid=(bsz, t // lc),
        in_specs=[pl.BlockSpec((1, lc, ML_MAIN), lambda b, c: (b, c, 0)),
                  pl.BlockSpec((1, lc, LANES), lambda b, c: (b, c, 0)), full(gbias), full(ng)],
        out_specs=pl.BlockSpec((1, lc, ML_DIM), lambda b, c: (b, c, 0)),
        out_shape=jax.ShapeDtypeStruct((bsz, t, ML_DIM), F32),
        scratch_shapes=[pltpu.VMEM((ML_HEADS, LANES, LANES), F32), pltpu.VMEM((ML_HEADS, 8, LANES), F32),
                        pltpu.VMEM((ML_HEADS, 8, LANES), F32)],
        compiler_params=_cparams("arbitrary", "arbitrary"),
        name="mlstm_mix",
    )(p, pg, gbias, ng)


def _route(h, g_ref, wh_ref, wl_ref, b_ref, ri_ref, rg_ref, cnt_ref, off_ref):
    tm = h.shape[0]

    @pl.when(pl.program_id(0) == 0)
    def _():
        off_ref[...] = jnp.zeros_like(off_ref)

    xn = _rms_rows(h, g_ref[...])
    xh = xn.astype(BF16)
    xl = (xn - xh.astype(F32)).astype(BF16)
    dot = lambda a, b: jnp.dot(a, b, preferred_element_type=F32)
    logits = dot(xh, wh_ref[...]) + dot(xh, wl_ref[...]) + dot(xl, wh_ref[...]) + b_ref[...]
    lane = lax.broadcasted_iota(I32, logits.shape, 1)
    big = jnp.int32(1 << 20)
    is_grp = (lane >= N_EXPERTS) & (lane < N_EXPERTS + N_GROUPS)
    lg = jnp.where(is_grp, logits, NEG)
    gmax = jnp.max(lg, axis=-1, keepdims=True)
    p_top = 1.0 / jnp.sum(jnp.exp(lg - gmax), axis=-1, keepdims=True)
    g_idx = jnp.min(jnp.where(lg == gmax, lane, big), axis=-1, keepdims=True) - N_EXPERTS
    valid = (lane < N_EXPERTS) & ((lane // EXPERTS_PER_GROUP) == g_idx)
    v1 = jnp.where(valid, logits, NEG)
    m1 = jnp.max(v1, axis=-1, keepdims=True)
    i1 = jnp.min(jnp.where(v1 == m1, lane, big), axis=-1, keepdims=True)
    v2 = jnp.where(lane == i1, NEG, v1)
    m2 = jnp.max(v2, axis=-1, keepdims=True)
    i2 = jnp.min(jnp.where(v2 == m2, lane, big), axis=-1, keepdims=True)
    e21 = jnp.exp(m2 - m1)
    gate1 = p_top / (1.0 + e21)
    gate2 = p_top * e21 / (1.0 + e21)

    sel1 = lane == i1
    sel2 = lane == i2
    onehot = jnp.where(sel1 | sel2, 1.0, 0.0)
    tr, tc = _iota2((tm, tm))
    before = jnp.where(tc < tr, 1.0, 0.0).astype(BF16)
    prefix = jnp.dot(before, onehot.astype(BF16), preferred_element_type=F32) + off_ref[0:1, :]
    rank1 = jnp.sum(jnp.where(sel1, prefix, 0.0), axis=-1, keepdims=True).astype(I32)
    rank2 = jnp.sum(jnp.where(sel2, prefix, 0.0), axis=-1, keepdims=True).astype(I32)
    total = off_ref[0:1, :] + jnp.sum(onehot, axis=0, keepdims=True)
    off_ref[...] = jnp.broadcast_to(total, off_ref.shape)
    cnt_ref[...] = jnp.broadcast_to(total, cnt_ref.shape).astype(I32)

    ri_ref[...] = jnp.where(lane == 0, i1, jnp.where(lane == 1, i2, jnp.where(lane == 2, rank1, jnp.where(lane == 3, rank2, 0))))
    rg_ref[...] = jnp.where(lane == 0, gate1, jnp.where(lane == 1, gate2, 0.0))
    return xn


def _out_proj_route_kernel(h_ref, ya_ref, yb_ref, wa_ref, wb_ref, g_ref, wh_ref, wl_ref, b_ref,
                           o_ref, xt_ref, ri_ref, rg_ref, cnt_ref, off_ref):
    out = h_ref[...] + _mm(ya_ref[...], wa_ref[...]) + _mm(yb_ref[...], wb_ref[...])
    o_ref[...] = out
    xn = _route(out, g_ref, wh_ref, wl_ref, b_ref, ri_ref, rg_ref, cnt_ref, off_ref)
    _store_row_tiles(xt_ref, xn)


def _out_proj_route(h, ya, yb, w_out, g_ffn, w_group, b_group, w_router, b_router):
    n, d = h.shape
    da, db = ya.shape[1], yb.shape[1]
    tm = min(PROJ_TM, n)
    wa = w_out[:da].astype(BF16)
    wb = w_out[da:].astype(BF16)
    pad = LANES - N_EXPERTS - N_GROUPS
    w_cat = jnp.concatenate([w_router, w_group, jnp.zeros((d, pad), F32)], axis=1)
    b_cat = jnp.concatenate([b_router, b_group, jnp.zeros((pad,), F32)]).reshape(1, LANES)
    w_hi = w_cat.astype(BF16)
    w_lo = (w_cat - w_hi.astype(F32)).astype(BF16)
    row = lambda i: (i, 0)
    fixed = lambda i: (0, 0)
    return pl.pallas_call(
        _out_proj_route_kernel,
        grid=(n // tm,),
        in_specs=[pl.BlockSpec((tm, d), row), pl.BlockSpec((tm, da), row), pl.BlockSpec((tm, db), row),
                  pl.BlockSpec((da, d), fixed), pl.BlockSpec((db, d), fixed), pl.BlockSpec((1, d), fixed),
                  pl.BlockSpec((d, LANES), fixed), pl.BlockSpec((d, LANES), fixed), pl.BlockSpec((1, LANES), fixed)],
        out_specs=[pl.BlockSpec((tm, d), row), pl.BlockSpec((tm * ROW_TILE, LANES), row),
                   pl.BlockSpec((tm, LANES), row), pl.BlockSpec((tm, LANES), row), pl.BlockSpec((8, LANES), fixed)],
        out_shape=[jax.ShapeDtypeStruct((n, d), F32), jax.ShapeDtypeStruct((n * ROW_TILE, LANES), F32),
                   jax.ShapeDtypeStruct((n, LANES), I32), jax.ShapeDtypeStruct((n, LANES), F32),
                   jax.ShapeDtypeStruct((8, LANES), I32)],
        scratch_shapes=[pltpu.VMEM((8, LANES), F32)],
        compiler_params=_cparams("arbitrary"),
        name="out_proj_route",
    )(h, ya, yb, wa, wb, g_ffn.reshape(1, d), w_hi, w_lo, b_cat)


def _dest_kernel(ri_ref, ss_ref, o_ref):
    ri = ri_ref[...]
    lane = lax.broadcasted_iota(I32, ri.shape, 1)
    ss = ss_ref[...].astype(F32)
    pick = lambda e: jnp.sum(jnp.where(lane == e, ss, 0.0), axis=-1, keepdims=True)
    d1 = pick(ri[:, 0:1]) + ri[:, 2:3].astype(F32)
    d2 = pick(ri[:, 1:2]) + ri[:, 3:4].astype(F32)
    both = jnp.where(lane == 0, d1, jnp.where(lane == 1, d2, 0.0))
    o_ref[...] = both.T[0:8, :].astype(I32)


def _dest(ri, seg_start):
    n = ri.shape[0]
    tm = min(PROJ_TM, n)
    ss = jnp.zeros((1, LANES), I32).at[0, :N_EXPERTS].set(seg_start)
    return pl.pallas_call(
        _dest_kernel,
        grid=(n // tm,),
        in_specs=[pl.BlockSpec((tm, LANES), lambda i: (i, 0)), pl.BlockSpec((1, LANES), lambda i: (0, 0))],
        out_specs=pl.BlockSpec((8, tm), lambda i: (0, i)),
        out_shape=jax.ShapeDtypeStruct((8, n), I32),
        compiler_params=_cparams("parallel"),
        name="moe_dest",
    )(ri, ss)


def _invert_kernel(d1_ref, d2_ref, o_ref):
    def clear(i, carry):
        o_ref[i] = 0
        return carry

    def fill(t, carry):
        o_ref[d1_ref[t]] = t
        o_ref[d2_ref[t]] = t
        return carry

    lax.fori_loop(0, o_ref.shape[0], clear, 0, unroll=GATHER_UNROLL)
    lax.fori_loop(0, d1_ref.shape[0], fill, 0, unroll=GATHER_UNROLL)


def _invert(d1, d2, n_rows):
    grid_spec = pltpu.PrefetchScalarGridSpec(
        num_scalar_prefetch=2, grid=(1,), in_specs=[],
        out_specs=pl.BlockSpec(memory_space=pltpu.SMEM))
    return pl.pallas_call(
        _invert_kernel,
        grid_spec=grid_spec,
        out_shape=jax.ShapeDtypeStruct((n_rows,), I32),
        compiler_params=_cparams("arbitrary"),
        name="moe_invert",
    )(d1, d2)


def _gather_rows(src_hbm, idx_ref, base, dst, sem):
    def body(j, carry):
        r = pl.multiple_of(idx_ref[base + j] * ROW_TILE, ROW_TILE)
        pltpu.make_async_copy(src_hbm.at[pl.ds(r, ROW_TILE), :],
                              dst.at[pl.ds(pl.multiple_of(j * ROW_TILE, ROW_TILE), ROW_TILE), :],
                              sem).start(priority=GATHER_PRIORITY)
        return carry

    lax.fori_loop(0, dst.shape[0] // ROW_TILE, body, 0, unroll=GATHER_UNROLL)


def _wait_rows(src_hbm, dst, sem):
    pltpu.make_async_copy(src_hbm.at[pl.ds(0, dst.shape[0]), :], dst, sem).wait()


def _expert_kernel(te_ref, nt_ref, src_ref, x_hbm, wg_ref, wu_ref, wd_ref, o_ref, xbuf, wgb, wub, wdb, sem):
    tm = xbuf.shape[1] // ROW_TILE
    i = pl.program_id(0)
    n_used = nt_ref[0]
    slot = i % EXPERT_BUFS

    @pl.when(i == 0)
    def _():
        for t in range(EXPERT_BUFS - 1):
            @pl.when(t < n_used)
            def _():
                _gather_rows(x_hbm, src_ref, t * tm, xbuf.at[t], sem.at[t])

    ahead = i + EXPERT_BUFS - 1

    @pl.when(ahead < n_used)
    def _():
        _gather_rows(x_hbm, src_ref, ahead * tm, xbuf.at[ahead % EXPERT_BUFS], sem.at[ahead % EXPERT_BUFS])

    @pl.when((i == 0) | (te_ref[i] != te_ref[jnp.maximum(i - 1, 0)]))
    def _():
        wgb[...] = wg_ref[0].astype(BF16)
        wub[...] = wu_ref[0].astype(BF16)
        wdb[...] = wd_ref[0].astype(BF16)

    @pl.when(i < n_used)
    def _():
        _wait_rows(x_hbm, xbuf.at[slot], sem.at[slot])
        xn = _load_row_tiles(xbuf.at[slot], tm).astype(BF16)
        gate = jnp.dot(xn, wgb[...], preferred_element_type=F32)
        up = jnp.dot(xn, wub[...], preferred_element_type=F32)
        hid = (jax.nn.silu(gate) * up).astype(BF16)
        _store_row_tiles(o_ref, jnp.dot(hid, wdb[...], preferred_element_type=F32))

    @pl.when(i >= n_used)
    def _():
        o_ref[...] = jnp.zeros_like(o_ref)


def _experts(xn_tiles, tile_expert, n_used, src, w_gate, w_up, w_down):
    d = D_MODEL
    n_tiles = tile_expert.shape[0]
    tm = src.shape[0] // n_tiles
    de = w_gate.shape[-1]
    grid_spec = pltpu.PrefetchScalarGridSpec(
        num_scalar_prefetch=3,
        grid=(n_tiles,),
        in_specs=[pl.BlockSpec(memory_space=pl.ANY),
                  pl.BlockSpec((1, d, de), lambda i, te, nt, s: (te[i], 0, 0)),
                  pl.BlockSpec((1, d, de), lambda i, te, nt, s: (te[i], 0, 0)),
                  pl.BlockSpec((1, de, d), lambda i, te, nt, s: (te[i], 0, 0))],
        out_specs=pl.BlockSpec((tm * ROW_TILE, LANES), lambda i, te, nt, s: (i, 0)),
        scratch_shapes=[pltpu.VMEM((EXPERT_BUFS, tm * ROW_TILE, LANES), F32), pltpu.VMEM((d, de), BF16),
                        pltpu.VMEM((d, de), BF16), pltpu.VMEM((de, d), BF16),
                        pltpu.SemaphoreType.DMA((EXPERT_BUFS,))],
    )
    return pl.pallas_call(
        _expert_kernel,
        grid_spec=grid_spec,
        out_shape=jax.ShapeDtypeStruct((n_tiles * tm * ROW_TILE, LANES), F32),
        compiler_params=_cparams("arbitrary"),
        name="moe_experts",
    )(tile_expert, n_used, src, xn_tiles, w_gate, w_up, w_down)


def _combine_kernel(d1_ref, d2_ref, ys_hbm, h_ref, rg_ref, gf_ref, o_ref, y1, y2, sem, *, final_norm):
    tm = h_ref.shape[0]
    i = pl.program_id(0)
    slot = i % 2

    def gather(tile, s):
        _gather_rows(ys_hbm, d1_ref, tile * tm, y1.at[s], sem.at[0, s])
        _gather_rows(ys_hbm, d2_ref, tile * tm, y2.at[s], sem.at[1, s])

    @pl.when(i == 0)
    def _():
        gather(0, 0)

    @pl.when(i + 1 < pl.num_programs(0))
    def _():
        gather(i + 1, 1 - slot)

    _wait_rows(ys_hbm, y1.at[slot], sem.at[0, slot])
    _wait_rows(ys_hbm, y2.at[slot], sem.at[1, slot])
    rg = rg_ref[...]
    out = (h_ref[...] + rg[:, 0:1] * _load_row_tiles(y1.at[slot], tm)
           + rg[:, 1:2] * _load_row_tiles(y2.at[slot], tm))
    if final_norm:
        out = _rms_rows(out, gf_ref[...])
    o_ref[...] = out


def _combine(h, ys, d1, d2, rg, g_final, final_norm):
    n, d = h.shape
    tm = min(PROJ_TM, n)
    grid_spec = pltpu.PrefetchScalarGridSpec(
        num_scalar_prefetch=2,
        grid=(n // tm,),
        in_specs=[pl.BlockSpec(memory_space=pl.ANY),
                  pl.BlockSpec((tm, d), lambda i, a, b: (i, 0)),
                  pl.BlockSpec((tm, LANES), lambda i, a, b: (i, 0)),
                  pl.BlockSpec((1, d), lambda i, a, b: (0, 0))],
        out_specs=pl.BlockSpec((tm, d), lambda i, a, b: (i, 0)),
        scratch_shapes=[pltpu.VMEM((2, tm * ROW_TILE, LANES), F32), pltpu.VMEM((2, tm * ROW_TILE, LANES), F32),
                        pltpu.SemaphoreType.DMA((2, 2))],
    )
    return pl.pallas_call(
        functools.partial(_combine_kernel, final_norm=final_norm),
        grid_spec=grid_spec,
        out_shape=jax.ShapeDtypeStruct((n, d), F32),
        compiler_params=_cparams("arbitrary"),
        name="moe_combine",
    )(d1, d2, ys, h, rg, g_final.reshape(1, d))


def _hmoe_residual(h, xn_tiles, ri, rg, cnt, w_gate, w_up, w_down, g_final, final_norm):
    n, d = h.shape

    counts = cnt[0, :N_EXPERTS]
    n_tiles = (2 * n) // MOE_TM + N_EXPERTS
    tiles_per = (counts + MOE_TM - 1) // MOE_TM
    tile_end = jnp.cumsum(tiles_per)
    seg_start = (tile_end - tiles_per) * MOE_TM
    dest = _dest(ri, seg_start.astype(I32))
    d1, d2 = dest[0], dest[1]
    tile_ids = jnp.arange(n_tiles, dtype=I32)
    tile_expert = jnp.minimum(jnp.sum((tile_end[None, :] <= tile_ids[:, None]).astype(I32), axis=1), N_EXPERTS - 1)
    src = _invert(d1, d2, n_tiles * MOE_TM)

    n_used = tile_end[N_EXPERTS - 1:].astype(I32)
    ys = _experts(xn_tiles, tile_expert, n_used, src, w_gate, w_up, w_down)
    return _combine(h, ys, d1, d2, rg, g_final, final_norm)


def kernel(x, norm_mix, norm_ffn, norm_final, ev_w_in, ev_mu, rw_w0, rw_w2, rw_a0, rw_a2, rw_g2, rw_k_k, rw_k_a, rw_r_k, rw_ln_w, rw_ln_b, hg_lb_logits, hg_norm, ev_w_out, od_w_in, gd_conv, gd_a_log, gd_dt_bias, gd_norm, ml_i_bias, ml_f_bias, ml_norm, od_w_out, moe_w_group, moe_b_group, moe_w_router, moe_b_router, moe_w_gate, moe_w_up, moe_w_down):
    bsz, t, d = x.shape
    n = bsz * t
    depth = norm_mix.shape[0]
    lb_table = jnp.cumsum(jax.nn.softmax(hg_lb_logits.astype(F32), axis=0), axis=0)
    h = x.reshape(n, d)
    for layer in range(depth):
        j = layer // 2
        if layer % 2 == 0:
            w_in = ev_w_in[j].astype(BF16)
            p_rw, p_hg = _rms_proj(h, norm_mix[layer], [w_in[:, :RW_IN], w_in[:, RW_IN:]])
            ya = _rwkv7(p_rw.reshape(bsz, t, RW_IN), ev_mu[j], rw_w0[j], rw_w2[j], rw_a0[j], rw_a2[j], rw_g2[j],
                        rw_k_k[j], rw_k_a[j], rw_r_k[j], rw_ln_w[j], rw_ln_b[j])
            yb = _hgrn2(p_hg.reshape(bsz, t, HG_IN), lb_table[j], hg_norm[j])
            mix_a, mix_b, w_out = ya.reshape(n, RW_DIM), yb.reshape(n, HG_KDIM), ev_w_out[j]
        else:
            w_in = od_w_in[j]
            w_gates = jnp.concatenate([w_in[:, GD_MAIN:GD_IN], w_in[:, GD_IN + ML_MAIN:],
                                       jnp.zeros((d, LANES - 2 * GD_HEADS - 2 * ML_HEADS), F32)], axis=1)
            p_gd, p_ml, p_gt = _rms_proj(h, norm_mix[layer], [w_in[:, :GD_MAIN].astype(BF16),
                                                              w_in[:, GD_IN:GD_IN + ML_MAIN].astype(BF16),
                                                              w_gates.astype(BF16)])
            p_gt = p_gt.reshape(bsz, t, LANES)
            yc = _gdn(p_gd.reshape(bsz, t, GD_MAIN), p_gt, gd_conv[j], gd_a_log[j], gd_dt_bias[j], gd_norm[j])
            yd = _mlstm(p_ml.reshape(bsz, t, ML_MAIN), p_gt, ml_i_bias[j], ml_f_bias[j], ml_norm[j])
            mix_a, mix_b, w_out = yc.reshape(n, GD_DIM), yd.reshape(n, ML_DIM), od_w_out[j]
        h, xn_tiles, ri, rg, cnt = _out_proj_route(h, mix_a, mix_b, w_out, norm_ffn[layer], moe_w_group[layer],
                                                   moe_b_group[layer], moe_w_router[layer], moe_b_router[layer])
        h = _hmoe_residual(h, xn_tiles, ri, rg, cnt, moe_w_gate[layer], moe_w_up[layer], moe_w_down[layer],
                           norm_final, final_norm=(layer == depth - 1))
    return h.reshape(bsz, t, d)
```

```python
import functools
import math

import jax
import jax.numpy as jnp
from jax import lax
from jax.experimental import pallas as pl
from jax.experimental.pallas import tpu as pltpu

F32 = jnp.float32
BF16 = jnp.bfloat16
I32 = jnp.int32
HIGHEST = lax.Precision.HIGHEST

D_MODEL = 1024
NORM_EPS = 1e-6
RW_HEADS, RW_HEAD_DIM = 8, 64
RW_DIM = RW_HEADS * RW_HEAD_DIM
R_DECAY, R_AAA, R_GATE = 64, 64, 128
RW_IN = 3 * RW_DIM + R_DECAY + R_AAA + R_GATE
RW_LN_EPS = 64e-5
HG_HEADS, HG_DIM = 8, 64
HG_KDIM = HG_HEADS * HG_DIM
HG_IN = 4 * HG_KDIM
GD_HEADS, GD_HEAD_DIM = 4, 128
GD_DIM = GD_HEADS * GD_HEAD_DIM
CONV_K = 4
GD_MAIN = 4 * GD_DIM
GD_IN = GD_MAIN + 2 * GD_HEADS
ML_HEADS, ML_HEAD_DIM = 4, 128
ML_DIM = ML_HEADS * ML_HEAD_DIM
ML_MAIN = 4 * ML_DIM
N_GROUPS, EXPERTS_PER_GROUP = 4, 8
N_EXPERTS = N_GROUPS * EXPERTS_PER_GROUP
D_EXPERT = 256

LANES = 128
VMEM_LIMIT_BYTES = 48 * 1024 * 1024

PROJ_TM = 256
CHUNK = 64
SUB = 16
RW_NCH = 4
GD_NCH = 4
HG_ROWS = 128
ML_CHUNK = 128
MOE_TM = 512
EXPERT_BUFS = 3
GATHER_UNROLL = 8
NEG = -1e30


def _cparams(*sem):
    return pltpu.CompilerParams(dimension_semantics=sem, vmem_limit_bytes=VMEM_LIMIT_BYTES)


def _mm(a, b):
    return jnp.dot(a.astype(BF16), b.astype(BF16), preferred_element_type=F32)


def _mm_nt(a, b):
    return lax.dot_general(a.astype(BF16), b.astype(BF16), (((1,), (1,)), ((), ())), preferred_element_type=F32)


def _mm_tn(a, b):
    return lax.dot_general(a.astype(BF16), b.astype(BF16), (((0,), (0,)), ((), ())), preferred_element_type=F32)


def _mm_hi(a, b):
    return jnp.dot(a, b, precision=HIGHEST, preferred_element_type=F32)


def _mm_split(x, ones_bf16):
    hi = x.astype(BF16)
    lo = (x - hi.astype(F32)).astype(BF16)
    return (jnp.dot(hi, ones_bf16, preferred_element_type=F32) + jnp.dot(lo, ones_bf16, preferred_element_type=F32))


def _iota2(shape):
    return lax.broadcasted_iota(I32, shape, 0), lax.broadcasted_iota(I32, shape, 1)


def _rms_rows(x, g, eps=NORM_EPS):
    return x * lax.rsqrt(jnp.mean(x * x, axis=-1, keepdims=True) + eps) * g


def _seg_ones(width, seg):
    r, c = _iota2((width, width))
    return jnp.where((r // seg) == (c // seg), 1.0, 0.0).astype(BF16)


def _split3(x):
    x1 = x.astype(BF16)
    r1 = x - x1.astype(F32)
    x2 = r1.astype(BF16)
    return x1, x2, (r1 - x2.astype(F32)).astype(BF16)


def _cumsum_rows(tri_bf16, x):
    return sum(jnp.dot(tri_bf16, t, preferred_element_type=F32) for t in _split3(x))


def _cumsum_cols(x, tri_bf16):
    return sum(jnp.dot(t, tri_bf16, preferred_element_type=F32) for t in _split3(x))


def _mm3(a, b):
    ah = a.astype(BF16)
    al = (a - ah.astype(F32)).astype(BF16)
    bh = b.astype(BF16)
    bl = (b - bh.astype(F32)).astype(BF16)
    dot = lambda x, y: jnp.dot(x, y, preferred_element_type=F32)
    return dot(ah, bh) + dot(ah, bl) + dot(al, bh)


def _tri_inv_multi(ms, n, chain):
    assert chain // SUB <= 4
    r, c = _iota2((n, n))
    same = (r // SUB) == (c // SUB)
    eye = jnp.where(r == c, 1.0, 0.0).astype(F32)
    ds = [jnp.where(same, m, 0.0) for m in ms]
    offs = [m - d for m, d in zip(ms, ds)]
    xs = [eye - d for d in ds]
    ps = ds
    for _ in range(3):
        ps = [_mm(p, p) for p in ps]
        xs = [x + _mm(x, p) for x, p in zip(xs, ps)]
    es = [_mm(x, o) for x, o in zip(xs, offs)]
    imes = [eye - e for e in es]
    e2s = [_mm(e, e) for e in es]
    ys = [i + _mm(i, e2) for i, e2 in zip(imes, e2s)]
    xs = [_mm(y, x) for y, x in zip(ys, xs)]
    res = [eye - x - _mm3(m, x) for m, x in zip(ms, xs)]
    return [x + _mm(x, rr) for x, rr in zip(xs, res)]


def _rms_proj_kernel(x_ref, g_ref, *refs, n_out):
    y = _rms_rows(x_ref[...], g_ref[...]).astype(BF16)
    for w_ref, o_ref in zip(refs[:n_out], refs[n_out:]):
        o_ref[...] = jnp.dot(y, w_ref[...], preferred_element_type=F32)


def _rms_proj(x, g, ws):
    n, d = x.shape
    tm = min(PROJ_TM, n)
    in_specs = [pl.BlockSpec((tm, d), lambda i: (i, 0)), pl.BlockSpec((1, d), lambda i: (0, 0))]
    in_specs += [pl.BlockSpec(w.shape, lambda i: (0, 0)) for w in ws]
    return pl.pallas_call(
        functools.partial(_rms_proj_kernel, n_out=len(ws)),
        grid=(n // tm,),
        in_specs=in_specs,
        out_specs=[pl.BlockSpec((tm, w.shape[1]), lambda i: (i, 0)) for w in ws],
        out_shape=[jax.ShapeDtypeStruct((n, w.shape[1]), F32) for w in ws],
        compiler_params=_cparams("parallel"),
        name="rms_proj",
    )(x, g.reshape(1, d), *ws)


ROW_TILE = D_MODEL // LANES


def _store_row_tiles(ref, x):
    for j in range(ROW_TILE):
        ref[pl.ds(j, x.shape[0], stride=ROW_TILE), :] = x[:, j * LANES:(j + 1) * LANES]


def _load_row_tiles(ref, rows):
    return jnp.concatenate([ref[pl.ds(j, rows, stride=ROW_TILE), :] for j in range(ROW_TILE)], axis=1)


def _rwkv7_kernel(p_ref, mu_ref, w0_ref, w2_ref, a0_ref, a2_ref, g2_ref, kk_ref, ka_ref, rk_ref,
                  lnw_ref, lnb_ref, o_ref, prev_ref, zt_ref):
    L = CHUNK
    npair = RW_HEADS // 2
    c = pl.program_id(1)

    @pl.when(c == 0)
    def _():
        prev_ref[...] = jnp.zeros_like(prev_ref)
        zt_ref[...] = jnp.zeros_like(zt_ref)

    x = p_ref[0]
    R = x.shape[0]
    nch = R // L
    row = lax.broadcasted_iota(I32, x.shape, 0)
    xs = jnp.where(row == 0, prev_ref[7:8, :], pltpu.roll(x, 1, 0))
    prev_ref[...] = x[R - 8:R, :]
    pm = x + mu_ref[...] * (xs - x)
    r_all = pm[:, 0:RW_DIM]
    k_all = pm[:, RW_DIM:2 * RW_DIM]
    v_all = pm[:, 2 * RW_DIM:3 * RW_DIM]
    wa = pm[:, 3 * RW_DIM:3 * RW_DIM + LANES]
    gl = pm[:, 3 * RW_DIM + LANES:]
    wlog = -jax.nn.softplus(-(w0_ref[...] + _mm(jnp.tanh(wa), w2_ref[...]))) - 0.5
    ld = -jnp.exp(wlog)
    a_all = jax.nn.sigmoid(a0_ref[...] + _mm(wa, a2_ref[...]))
    g_all = _mm(jax.nn.sigmoid(gl), g2_ref[...])

    tr, tc = _iota2((R, R))
    chunk_tril = jnp.where((tc <= tr) & ((tr // L) == (tc // L)), 1.0, 0.0).astype(BF16)
    cs_all = _cumsum_rows(chunk_tril, ld)
    seg = _seg_ones(LANES, RW_HEAD_DIM)
    lane = lax.broadcasted_iota(I32, (L, LANES), 1)
    hm = (lane < RW_HEAD_DIM, lane >= RW_HEAD_DIM)
    br, bc = _iota2((2 * L, 2 * L))
    bd = (br // L) == (bc // L)
    bd_strict = bd & (bc < br)
    bd_incl = bd & (bc <= br)
    fold = lambda z: z[0:L] + z[L:2 * L]
    both = lambda z: jnp.concatenate([jnp.where(hm[0], z, 0.0), jnp.where(hm[1], z, 0.0)], axis=0)

    units = []
    for ci in range(nch):
        rs = slice(ci * L, (ci + 1) * L)
        for p in range(npair):
            sl = slice(p * LANES, (p + 1) * LANES)
            r, k, v, a = r_all[rs, sl], k_all[rs, sl], v_all[rs, sl], a_all[rs, sl]
            cs, ldp = cs_all[rs, sl], ld[rs, sl]
            kkr = k * kk_ref[:, sl]
            kk = kkr * lax.rsqrt(_mm_split(kkr * kkr, seg) + 1e-6)
            k2 = k * (1.0 + (a - 1.0) * ka_ref[:, sl])
            b = kk * a
            cs_last = cs[L - 1:L, :]
            e_neg = jnp.exp(-cs)
            e_rem = jnp.exp(cs_last - cs)
            bhat = b * e_neg
            khat = k2 * e_neg
            units.append(dict(p=p, rs=rs, sl=sl, r=r, v=v, k2=k2, rhat=r * jnp.exp(cs), gam_last=jnp.exp(cs_last),
                              btil=b * e_rem, ktil=k2 * e_rem, a2=both(kk * jnp.exp(cs - ldp)), v2=both(v),
                              rhs4=jnp.concatenate([bhat, bhat, khat, khat], axis=0)))
    for q in units:
        lhs = jnp.concatenate([q["a2"], both(q["rhat"])], axis=0)
        q["g"] = _mm_nt(lhs, q["rhs4"])
    tinvs = _tri_inv_multi([jnp.where(bd_strict, q["g"][0:2 * L, 0:2 * L], 0.0) for q in units], 2 * L, L)
    for q, tinv in zip(units, tinvs):
        q["tinv"] = tinv
        q["x2"] = _mm(jnp.where(bd_strict, q["g"][0:2 * L, 2 * L:4 * L], 0.0), q["v2"])
    for q in units:
        uw = _mm(q["tinv"], jnp.concatenate([q["x2"], q["a2"]], axis=1))
        q["u0"] = -fold(uw[:, 0:LANES])
        q["w"] = fold(uw[:, LANES:])
        q["y0"] = fold(_mm(jnp.where(bd_incl, q["g"][2 * L:4 * L, 2 * L:4 * L], 0.0), q["v2"]))
    for q in units:
        rb = jnp.where(bd_incl, q["g"][2 * L:4 * L, 0:2 * L], 0.0)
        ruw = _mm(rb, jnp.concatenate([both(q["u0"]), both(q["w"])], axis=1))
        q["yc"] = q["y0"] + fold(ruw[:, 0:LANES])
        q["ry"] = q["rhat"] - fold(ruw[:, LANES:])
        q["c1"] = _mm_tn(jnp.concatenate([q["u0"], q["v"]], axis=0), jnp.concatenate([q["btil"], q["ktil"]], axis=0))
        q["c2"] = _mm_tn(q["w"], q["btil"])

    hr, hc = _iota2((LANES, LANES))
    head_bd = (hr // RW_HEAD_DIM) == (hc // RW_HEAD_DIM)
    zts = [zt_ref[p] for p in range(npair)]
    for ci in range(nch):
        qs = units[ci * npair:(ci + 1) * npair]
        for q, zt in zip(qs, zts):
            q["y"] = q["yc"] + _mm_nt(q["ry"], zt)
        zts = [zt * q["gam_last"] + jnp.where(head_bd, q["c1"] - _mm(zt, q["c2"]), 0.0) for q, zt in zip(qs, zts)]
    for p in range(npair):
        zt_ref[p] = zts[p]

    for q in units:
        sl, rs, y = q["sl"], q["rs"], q["y"]
        mean = _mm_split(y, seg) * (1.0 / RW_HEAD_DIM)
        yc = y - mean
        var = _mm_split(yc * yc, seg) * (1.0 / RW_HEAD_DIM)
        yn = yc * lax.rsqrt(var + RW_LN_EPS) * lnw_ref[:, sl] + lnb_ref[:, sl]
        bonus = _mm_split(q["r"] * q["k2"] * rk_ref[:, sl], seg) * q["v"]
        o_ref[0, rs, sl] = (yn + bonus) * g_all[rs, sl]


def _rwkv7(p, mu, w0, w2, a0, a2, g2, k_k, k_a, r_k, ln_w, ln_b):
    bsz, t, _ = p.shape
    row = lambda z: z.reshape(1, -1).astype(F32)
    w2p = jnp.concatenate([w2, jnp.zeros_like(w2)], axis=0).astype(BF16)
    a2p = jnp.concatenate([jnp.zeros_like(a2), a2], axis=0).astype(BF16)
    params = [row(mu), row(w0), w2p, row(a0), a2p, g2.astype(BF16), row(k_k), row(k_a), row(r_k), row(ln_w), row(ln_b)]
    full = lambda z: pl.BlockSpec(z.shape, lambda b, c: (0, 0))
    rows = RW_NCH * CHUNK
    return pl.pallas_call(
        _rwkv7_kernel,
        grid=(bsz, t // rows),
        in_specs=[pl.BlockSpec((1, rows, RW_IN), lambda b, c: (b, c, 0))] + [full(z) for z in params],
        out_specs=pl.BlockSpec((1, rows, RW_DIM), lambda b, c: (b, c, 0)),
        out_shape=jax.ShapeDtypeStruct((bsz, t, RW_DIM), F32),
        scratch_shapes=[pltpu.VMEM((8, RW_IN), F32), pltpu.VMEM((RW_HEADS // 2, LANES, LANES), F32)],
        compiler_params=_cparams("arbitrary", "arbitrary"),
        name="rwkv7_mix",
    )(p, *params)


def _hgrn2_kernel(p_ref, lb_ref, ng_ref, o_ref, st_ref):
    L = HG_ROWS
    c = pl.program_id(1)

    @pl.when(c == 0)
    def _():
        st_ref[...] = jnp.zeros_like(st_ref)

    x = p_ref[0]
    lb = lb_ref[...]
    q_all = jax.nn.silu(x[:, 0:HG_KDIM])
    fg = lb + (1.0 - lb) * jax.nn.sigmoid(x[:, HG_KDIM:2 * HG_KDIM])
    k_all = 1.0 - fg
    logf = jnp.log(fg)
    v_all = x[:, 2 * HG_KDIM:3 * HG_KDIM]
    gate = x[:, 3 * HG_KDIM:]
    tr, tc = _iota2((L, L))
    blk_tril = jnp.where((tc <= tr) & ((tr // SUB) == (tc // SUB)), 1.0, 0.0).astype(BF16)
    bc_all = _cumsum_rows(blk_tril, logf)
    seg = _seg_ones(LANES, HG_DIM)
    br, bcc = _iota2((LANES, LANES))
    bd = (br // HG_DIM) == (bcc // HG_DIM)
    half = SUB // 2
    t_lo = lax.broadcasted_iota(I32, (SUB, LANES), 0)
    t_hi = lax.broadcasted_iota(I32, (half, LANES), 0) + half

    nsub = L // SUB
    npair = HG_HEADS // 2
    rows_per_unit = half * SUB + half * half
    sub_of_row = lax.broadcasted_iota(I32, (L, LANES), 0) // SUB

    units = []
    parts = []
    for p in range(npair):
        sl = slice(p * LANES, (p + 1) * LANES)
        for j in range(nsub):
            rs = slice(j * SUB, (j + 1) * SUB)
            q, k, v, bc = q_all[rs, sl], k_all[rs, sl], v_all[rs, sl], bc_all[rs, sl]
            q_hi, bc_hi = q[half:], bc[half:]
            for s in range(SUB):
                if s < half:
                    diff = jnp.where(t_lo >= s, bc - bc[s:s + 1, :], NEG)
                    parts.append(jnp.exp(diff) * q * k[s:s + 1, :])
                else:
                    diff = jnp.where(t_hi >= s, bc_hi - bc[s:s + 1, :], NEG)
                    parts.append(jnp.exp(diff) * q_hi * k[s:s + 1, :])
            bend = bc[SUB - 1:SUB, :]
            units.append(dict(v=v, qt=q * jnp.exp(bc), dec=jnp.exp(bend), kd=k * jnp.exp(bend - bc)))
    score_all = _mm(jnp.concatenate(parts, axis=0), seg)
    for ui, un in enumerate(units):
        score = score_all[ui * rows_per_unit:(ui + 1) * rows_per_unit]
        v = un["v"]
        acc_lo = jnp.zeros((half, LANES), F32)
        acc_hi = jnp.zeros((half, LANES), F32)
        off = 0
        for s in range(SUB):
            vs = v[s:s + 1, :]
            if s < half:
                acc_lo = acc_lo + score[off:off + half] * vs
                acc_hi = acc_hi + score[off + half:off + SUB] * vs
                off += SUB
            else:
                acc_hi = acc_hi + score[off:off + half] * vs
                off += half
        un["intra"] = jnp.concatenate([acc_lo, acc_hi], axis=0)
    for p in range(npair):
        us = units[p * nsub:(p + 1) * nsub]
        v_pair = jnp.concatenate([un["v"] for un in us], axis=0)
        kd_pair = jnp.concatenate([un["kd"] for un in us], axis=0)
        kd_wide = jnp.concatenate([jnp.where(sub_of_row == j, kd_pair, 0.0) for j in range(nsub)], axis=1)
        upd = _mm_tn(v_pair, kd_wide)
        for j, un in enumerate(us):
            un["upd"] = jnp.where(bd, upd[:, j * LANES:(j + 1) * LANES], 0.0)

    for p in range(npair):
        sl = slice(p * LANES, (p + 1) * LANES)
        st = st_ref[p]
        outs = []
        for un in units[p * nsub:(p + 1) * nsub]:
            outs.append(un["intra"] + _mm_nt(un["qt"], st))
            st = st * un["dec"] + un["upd"]
        st_ref[p] = st
        o = jnp.concatenate(outs, axis=0)
        ms = _mm_split(o * o, seg) * (1.0 / HG_DIM)
        o_ref[0, :, sl] = o * lax.rsqrt(ms + NORM_EPS) * ng_ref[:, sl] * jax.nn.silu(gate[:, sl])


def _hgrn2(p, lb, norm_g):
    bsz, t, _ = p.shape
    lb = lb.reshape(1, HG_KDIM).astype(F32)
    ng = jnp.tile(norm_g.astype(F32), HG_HEADS).reshape(1, HG_KDIM)
    return pl.pallas_call(
        _hgrn2_kernel,
        grid=(bsz, t // HG_ROWS),
        in_specs=[pl.BlockSpec((1, HG_ROWS, HG_IN), lambda b, c: (b, c, 0)),
                  pl.BlockSpec((1, HG_KDIM), lambda b, c: (0, 0)), pl.BlockSpec((1, HG_KDIM), lambda b, c: (0, 0))],
        out_specs=pl.BlockSpec((1, HG_ROWS, HG_KDIM), lambda b, c: (b, c, 0)),
        out_shape=jax.ShapeDtypeStruct((bsz, t, HG_KDIM), F32),
        scratch_shapes=[pltpu.VMEM((HG_HEADS // 2, LANES, LANES), F32)],
        compiler_params=_cparams("arbitrary", "arbitrary"),
        name="hgrn2_mix",
    )(p, lb, ng)


def _gdn_kernel(p_ref, pg_ref, cw_ref, gb_ref, nal_ref, nalc_ref, ng_ref, o_ref, prev_ref, s_ref):
    L = CHUNK
    c = pl.program_id(1)

    @pl.when(c == 0)
    def _():
        prev_ref[...] = jnp.zeros_like(prev_ref)
        s_ref[...] = jnp.zeros_like(s_ref)

    x = p_ref[0]
    R = x.shape[0]
    nch = R // L
    xq = x[:, 0:3 * GD_DIM]
    xcat = jnp.concatenate([prev_ref[...], xq], axis=0)
    conv = xq * cw_ref[CONV_K - 1:CONV_K, :]
    for j in range(1, CONV_K):
        conv = conv + xcat[8 - j:8 - j + R, :] * cw_ref[CONV_K - 1 - j:CONV_K - j, :]
    prev_ref[...] = xq[R - 8:R, :]
    qkv = jax.nn.silu(conv)
    z = x[:, 3 * GD_DIM:]

    gt = pg_ref[0] + gb_ref[...]
    beta_all = jax.nn.sigmoid(gt)
    g_all = nal_ref[...] * jax.nn.softplus(gt)
    g_t = nalc_ref[...] * jax.nn.softplus(gt.T)
    tr, tc = _iota2((R, R))
    chunk_tril = jnp.where((tc <= tr) & ((tr // L) == (tc // L)), 1.0, 0.0).astype(BF16)
    gam_all = _cumsum_rows(chunk_tril, g_all)
    ur, uc = _iota2((R, 2 * R))
    triu2 = jnp.where(((ur // L) == (uc // (2 * L))) & ((ur % L) <= (uc % L)), 1.0, 0.0).astype(BF16)
    gam_t2 = _cumsum_cols(g_t, triu2)
    lr, lc = _iota2((L, L))
    incl = lc <= lr
    br, bc = _iota2((2 * L, 2 * L))
    bd = (br // L) == (bc // L)
    bd_strict = bd & (bc < br)
    bd_incl = bd & (bc <= br)
    lane2 = lax.broadcasted_iota(I32, (1, 2 * L), 1)
    zero = jnp.zeros((L, LANES), F32)

    units = []
    for ci in range(nch):
        rs = slice(ci * L, (ci + 1) * L)
        for h in range(GD_HEADS):
            q = qkv[rs, h * LANES:(h + 1) * LANES]
            k = qkv[rs, GD_DIM + h * LANES:GD_DIM + (h + 1) * LANES]
            v = qkv[rs, 2 * GD_DIM + h * LANES:2 * GD_DIM + (h + 1) * LANES]
            q = q * lax.rsqrt(jnp.sum(q * q, axis=-1, keepdims=True) + 1e-6) * (GD_HEAD_DIM ** -0.5)
            k = k * lax.rsqrt(jnp.sum(k * k, axis=-1, keepdims=True) + 1e-6)
            beta = beta_all[rs, h:h + 1]
            gam = gam_all[rs, GD_HEADS + h:GD_HEADS + h + 1]
            gam_row2 = gam_t2[GD_HEADS + h:GD_HEADS + h + 1, ci * 2 * L:(ci + 1) * 2 * L]
            gam_last = gam[L - 1:L, :]
            kb = k * beta
            units.append(dict(h=h, rs=rs, q=q, k=k, kb=kb, vb=v * beta, gam=gam, gam_row2=gam_row2,
                              gam_last=gam_last, kg=kb * jnp.exp(gam), qg=q * jnp.exp(gam),
                              kd=k * jnp.exp(gam_last - gam),
                              decay=jnp.exp(jnp.where(incl, gam - gam_row2[:, 0:L], NEG))))
    pairs = [(units[i], units[i + 1]) for i in range(0, len(units), 2)]
    ms = []
    for h0, h1 in pairs:
        lhs = jnp.concatenate([jnp.concatenate([h0["kb"], zero], axis=1),
                               jnp.concatenate([zero, h1["kb"]], axis=1)], axis=0)
        rhs = jnp.concatenate([jnp.concatenate([h0["k"], zero], axis=1),
                               jnp.concatenate([zero, h1["k"]], axis=1)], axis=0)
        gam_col = jnp.concatenate([h0["gam"], h1["gam"]], axis=0)
        gam_row = jnp.where(lane2 < L, h0["gam_row2"], h1["gam_row2"])
        decay2 = jnp.exp(jnp.where(bd_incl, gam_col - gam_row, NEG))
        ms.append(jnp.where(bd_strict, _mm_nt(lhs, rhs) * decay2, 0.0))
    tinvs = _tri_inv_multi(ms, 2 * L, L)
    for (h0, h1), tinv in zip(pairs, tinvs):
        rhs = jnp.concatenate([jnp.concatenate([h0["vb"], h0["kg"]], axis=1),
                               jnp.concatenate([h1["vb"], h1["kg"]], axis=1)], axis=0)
        uw = _mm(tinv, rhs)
        h0["u"], h0["w"] = uw[0:L, 0:LANES], uw[0:L, LANES:]
        h1["u"], h1["w"] = uw[L:2 * L, 0:LANES], uw[L:2 * L, LANES:]
    for hd in units:
        attn = _mm_nt(hd["q"], hd["k"]) * hd["decay"]
        uw = jnp.concatenate([hd["u"], hd["w"]], axis=1)
        auw = _mm(attn, uw)
        hd["o0"] = auw[:, 0:LANES]
        hd["qs"] = hd["qg"] - auw[:, LANES:]
        cc = _mm_tn(hd["kd"], uw)
        hd["c1"], hd["c2"] = cc[:, 0:LANES], cc[:, LANES:]

    ss = [s_ref[h] for h in range(GD_HEADS)]
    for ci in range(nch):
        hds = units[ci * GD_HEADS:(ci + 1) * GD_HEADS]
        for hd, s in zip(hds, ss):
            hd["o"] = hd["o0"] + _mm(hd["qs"], s)
        ss = [s * jnp.exp(hd["gam_last"]) + hd["c1"] - _mm(hd["c2"], s) for hd, s in zip(hds, ss)]
    for h in range(GD_HEADS):
        s_ref[h] = ss[h]
    for hd in units:
        sl = slice(hd["h"] * LANES, (hd["h"] + 1) * LANES)
        o_ref[0, hd["rs"], sl] = _rms_rows(hd["o"], ng_ref[...]) * jax.nn.silu(z[hd["rs"], sl])


def _gate_row(lo, vals):
    return jnp.zeros((1, LANES), F32).at[0, lo:lo + vals.shape[0]].set(vals.astype(F32))


def _gdn(p, pg, conv_w, a_log, dt_bias, norm_g):
    bsz, t, _ = p.shape
    gbias = _gate_row(GD_HEADS, dt_bias)
    nal = _gate_row(GD_HEADS, -jnp.exp(a_log))
    nal_col = nal.reshape(LANES, 1)
    ng = norm_g.reshape(1, GD_HEAD_DIM).astype(F32)
    full = lambda z: pl.BlockSpec(z.shape, lambda b, c: (0, 0))
    rows = GD_NCH * CHUNK
    return pl.pallas_call(
        _gdn_kernel,
        grid=(bsz, t // rows),
        in_specs=[pl.BlockSpec((1, rows, GD_MAIN), lambda b, c: (b, c, 0)),
                  pl.BlockSpec((1, rows, LANES), lambda b, c: (b, c, 0)),
                  full(conv_w), full(gbias), full(nal), full(nal_col), full(ng)],
        out_specs=pl.BlockSpec((1, rows, GD_DIM), lambda b, c: (b, c, 0)),
        out_shape=jax.ShapeDtypeStruct((bsz, t, GD_DIM), F32),
        scratch_shapes=[pltpu.VMEM((8, 3 * GD_DIM), F32), pltpu.VMEM((GD_HEADS, LANES, LANES), F32)],
        compiler_params=_cparams("arbitrary", "arbitrary"),
        name="gdn_mix",
    )(p, pg, conv_w.astype(F32), gbias, nal, nal_col, ng)


def _mlstm_kernel(p_ref, pg_ref, gb_ref, ng_ref, o_ref, c_ref, n_ref, m_ref):
    L = ML_CHUNK
    ci = pl.program_id(1)

    @pl.when(ci == 0)
    def _():
        c_ref[...] = jnp.zeros_like(c_ref)
        n_ref[...] = jnp.zeros_like(n_ref)
        m_ref[...] = jnp.zeros_like(m_ref)

    x = p_ref[0]
    gt = pg_ref[0] + gb_ref[...]
    logf = jax.nn.log_sigmoid(gt)
    gt_t = gt.T
    logf_t = jax.nn.log_sigmoid(gt_t)
    tr, tc = _iota2((L, L))
    bc_all = _cumsum_rows(jnp.where(tc <= tr, 1.0, 0.0).astype(BF16), logf)
    bc_t = _cumsum_cols(logf_t, jnp.where(tr <= tc, 1.0, 0.0).astype(BF16))
    incl = tc <= tr
    i_lo, f_lo = 2 * GD_HEADS, 2 * GD_HEADS + ML_HEADS

    heads = []
    for h in range(ML_HEADS):
        sl = slice(h * LANES, (h + 1) * LANES)
        q = x[:, sl]
        k = x[:, ML_DIM + h * LANES:ML_DIM + (h + 1) * LANES] * (ML_HEAD_DIM ** -0.5)
        v = x[:, 2 * ML_DIM + h * LANES:2 * ML_DIM + (h + 1) * LANES]
        bc = bc_all[:, f_lo + h:f_lo + h + 1]
        bc_row = bc_t[f_lo + h:f_lo + h + 1, :]
        ig = gt[:, i_lo + h:i_lo + h + 1]
        ig_row = gt_t[i_lo + h:i_lo + h + 1, :]
        m_prev = m_ref[h][0:1, 0:1]
        b_last = bc[L - 1:L, :]
        d_log = jnp.where(incl, bc - bc_row + ig_row, NEG)
        inter_log = bc + m_prev
        m_t = jnp.maximum(inter_log, jnp.max(d_log, axis=-1, keepdims=True))
        upd_log = b_last - bc + ig
        m_new = jnp.maximum(b_last + m_prev, jnp.max(upd_log, axis=0, keepdims=True))
        wk = jnp.exp(upd_log - m_new) * k
        heads.append(dict(sl=sl, q=q, v=v, m_t=m_t, m_new=m_new, wk=wk, inter_w=jnp.exp(inter_log - m_t),
                          dec=jnp.exp(b_last + m_prev - m_new), sc=_mm_nt(q, k) * jnp.exp(d_log - m_t)))
    for hd in heads:
        hd["scv"] = _mm(hd["sc"], hd["v"])
        hd["kv"] = _mm_tn(hd["wk"], hd["v"])
    for h, hd in enumerate(heads):
        cmat = c_ref[h]
        nvec = n_ref[h][0:1, :]
        q, sc, inter_w, dec = hd["q"], hd["sc"], hd["inter_w"], hd["dec"]
        num = inter_w * _mm(q, cmat) + hd["scv"]
        den = inter_w * jnp.sum(q * nvec, axis=-1, keepdims=True) + jnp.sum(sc, axis=-1, keepdims=True)
        hd["hh"] = num / jnp.maximum(jnp.abs(den), jnp.exp(-hd["m_t"]))
        c_ref[h] = dec * cmat + hd["kv"]
        n_ref[h] = jnp.broadcast_to(dec * nvec + jnp.sum(hd["wk"], axis=0, keepdims=True), (8, LANES))
        m_ref[h] = jnp.broadcast_to(hd["m_new"], (8, LANES))
    for h, hd in enumerate(heads):
        sl = hd["sl"]
        og = jax.nn.sigmoid(x[:, 3 * ML_DIM + h * LANES:3 * ML_DIM + (h + 1) * LANES])
        o_ref[0, :, sl] = og * _rms_rows(hd["hh"], ng_ref[:, sl])


def _mlstm(p, pg, i_bias, f_bias, norm_g):
    bsz, t, _ = p.shape
    gbias = _gate_row(2 * GD_HEADS, jnp.concatenate([i_bias, f_bias]))
    ng = norm_g.reshape(1, ML_DIM).astype(F32)
    lc = min(ML_CHUNK, t)
    full = lambda z: pl.BlockSpec(z.shape, lambda b, c: (0, 0))
    return pl.pallas_call(
        _mlstm_kernel,
        grid=(bsz, t // lc),
        in_specs=[pl.BlockSpec((1, lc, ML_MAIN), lambda b, c: (b, c, 0)),
                  pl.BlockSpec((1, lc, LANES), lambda b, c: (b, c, 0)), full(gbias), full(ng)],
        out_specs=pl.BlockSpec((1, lc, ML_DIM), lambda b, c: (b, c, 0)),
        out_shape=jax.ShapeDtypeStruct((bsz, t, ML_DIM), F32),
        scratch_shapes=[pltpu.VMEM((ML_HEADS, LANES, LANES), F32), pltpu.VMEM((ML_HEADS, 8, LANES), F32),
                        pltpu.VMEM((ML_HEADS, 8, LANES), F32)],
        compiler_params=_cparams("arbitrary", "arbitrary"),
        name="mlstm_mix",
    )(p, pg, gbias, ng)


def _route(h, g_ref, wh_ref, wl_ref, b_ref, ri_ref, rg_ref, cnt_ref, off_ref):
    tm = h.shape[0]

    @pl.when(pl.program_id(0) == 0)
    def _():
        off_ref[...] = jnp.zeros_like(off_ref)

    xn = _rms_rows(h, g_ref[...])
    xh = xn.astype(BF16)
    xl = (xn - xh.astype(F32)).astype(BF16)
    dot = lambda a, b: jnp.dot(a, b, preferred_element_type=F32)
    logits = dot(xh, wh_ref[...]) + dot(xh, wl_ref[...]) + dot(xl, wh_ref[...]) + b_ref[...]
    lane = lax.broadcasted_iota(I32, logits.shape, 1)
    big = jnp.int32(1 << 20)
    is_grp = (lane >= N_EXPERTS) & (lane < N_EXPERTS + N_GROUPS)
    lg = jnp.where(is_grp, logits, NEG)
    gmax = jnp.max(lg, axis=-1, keepdims=True)
    p_top = 1.0 / jnp.sum(jnp.exp(lg - gmax), axis=-1, keepdims=True)
    g_idx = jnp.min(jnp.where(lg == gmax, lane, big), axis=-1, keepdims=True) - N_EXPERTS
    valid = (lane < N_EXPERTS) & ((lane // EXPERTS_PER_GROUP) == g_idx)
    v1 = jnp.where(valid, logits, NEG)
    m1 = jnp.max(v1, axis=-1, keepdims=True)
    i1 = jnp.min(jnp.where(v1 == m1, lane, big), axis=-1, keepdims=True)
    v2 = jnp.where(lane == i1, NEG, v1)
    m2 = jnp.max(v2, axis=-1, keepdims=True)
    i2 = jnp.min(jnp.where(v2 == m2, lane, big), axis=-1, keepdims=True)
    e21 = jnp.exp(m2 - m1)
    gate1 = p_top / (1.0 + e21)
    gate2 = p_top * e21 / (1.0 + e21)

    sel1 = lane == i1
    sel2 = lane == i2
    onehot = jnp.where(sel1 | sel2, 1.0, 0.0)
    tr, tc = _iota2((tm, tm))
    before = jnp.where(tc < tr, 1.0, 0.0).astype(BF16)
    prefix = jnp.dot(before, onehot.astype(BF16), preferred_element_type=F32) + off_ref[0:1, :]
    rank1 = jnp.sum(jnp.where(sel1, prefix, 0.0), axis=-1, keepdims=True).astype(I32)
    rank2 = jnp.sum(jnp.where(sel2, prefix, 0.0), axis=-1, keepdims=True).astype(I32)
    total = off_ref[0:1, :] + jnp.sum(onehot, axis=0, keepdims=True)
    off_ref[...] = jnp.broadcast_to(total, off_ref.shape)
    cnt_ref[...] = jnp.broadcast_to(total, cnt_ref.shape).astype(I32)

    ri_ref[...] = jnp.where(lane == 0, i1, jnp.where(lane == 1, i2, jnp.where(lane == 2, rank1, jnp.where(lane == 3, rank2, 0))))
    rg_ref[...] = jnp.where(lane == 0, gate1, jnp.where(lane == 1, gate2, 0.0))
    return xn


def _out_proj_route_kernel(h_ref, ya_ref, yb_ref, wa_ref, wb_ref, g_ref, wh_ref, wl_ref, b_ref,
                           o_ref, xt_ref, ri_ref, rg_ref, cnt_ref, off_ref):
    out = h_ref[...] + _mm(ya_ref[...], wa_ref[...]) + _mm(yb_ref[...], wb_ref[...])
    o_ref[...] = out
    xn = _route(out, g_ref, wh_ref, wl_ref, b_ref, ri_ref, rg_ref, cnt_ref, off_ref)
    _store_row_tiles(xt_ref, xn)


def _out_proj_route(h, ya, yb, w_out, g_ffn, w_group, b_group, w_router, b_router):
    n, d = h.shape
    da, db = ya.shape[1], yb.shape[1]
    tm = min(PROJ_TM, n)
    wa = w_out[:da].astype(BF16)
    wb = w_out[da:].astype(BF16)
    pad = LANES - N_EXPERTS - N_GROUPS
    w_cat = jnp.concatenate([w_router, w_group, jnp.zeros((d, pad), F32)], axis=1)
    b_cat = jnp.concatenate([b_router, b_group, jnp.zeros((pad,), F32)]).reshape(1, LANES)
    w_hi = w_cat.astype(BF16)
    w_lo = (w_cat - w_hi.astype(F32)).astype(BF16)
    row = lambda i: (i, 0)
    fixed = lambda i: (0, 0)
    return pl.pallas_call(
        _out_proj_route_kernel,
        grid=(n // tm,),
        in_specs=[pl.BlockSpec((tm, d), row), pl.BlockSpec((tm, da), row), pl.BlockSpec((tm, db), row),
                  pl.BlockSpec((da, d), fixed), pl.BlockSpec((db, d), fixed), pl.BlockSpec((1, d), fixed),
                  pl.BlockSpec((d, LANES), fixed), pl.BlockSpec((d, LANES), fixed), pl.BlockSpec((1, LANES), fixed)],
        out_specs=[pl.BlockSpec((tm, d), row), pl.BlockSpec((tm * ROW_TILE, LANES), row),
                   pl.BlockSpec((tm, LANES), row), pl.BlockSpec((tm, LANES), row), pl.BlockSpec((8, LANES), fixed)],
        out_shape=[jax.ShapeDtypeStruct((n, d), F32), jax.ShapeDtypeStruct((n * ROW_TILE, LANES), F32),
                   jax.ShapeDtypeStruct((n, LANES), I32), jax.ShapeDtypeStruct((n, LANES), F32),
                   jax.ShapeDtypeStruct((8, LANES), I32)],
        scratch_shapes=[pltpu.VMEM((8, LANES), F32)],
        compiler_params=_cparams("arbitrary"),
        name="out_proj_route",
    )(h, ya, yb, wa, wb, g_ffn.reshape(1, d), w_hi, w_lo, b_cat)


def _dest_kernel(ri_ref, ss_ref, o_ref):
    ri = ri_ref[...]
    lane = lax.broadcasted_iota(I32, ri.shape, 1)
    ss = ss_ref[...].astype(F32)
    pick = lambda e: jnp.sum(jnp.where(lane == e, ss, 0.0), axis=-1, keepdims=True)
    d1 = pick(ri[:, 0:1]) + ri[:, 2:3].astype(F32)
    d2 = pick(ri[:, 1:2]) + ri[:, 3:4].astype(F32)
    both = jnp.where(lane == 0, d1, jnp.where(lane == 1, d2, 0.0))
    o_ref[...] = both.T[0:8, :].astype(I32)


def _dest(ri, seg_start):
    n = ri.shape[0]
    tm = min(PROJ_TM, n)
    ss = jnp.zeros((1, LANES), I32).at[0, :N_EXPERTS].set(seg_start)
    return pl.pallas_call(
        _dest_kernel,
        grid=(n // tm,),
        in_specs=[pl.BlockSpec((tm, LANES), lambda i: (i, 0)), pl.BlockSpec((1, LANES), lambda i: (0, 0))],
        out_specs=pl.BlockSpec((8, tm), lambda i: (0, i)),
        out_shape=jax.ShapeDtypeStruct((8, n), I32),
        compiler_params=_cparams("parallel"),
        name="moe_dest",
    )(ri, ss)


def _invert_kernel(d1_ref, d2_ref, o_ref):
    def clear(i, carry):
        o_ref[i] = 0
        return carry

    def fill(t, carry):
        o_ref[d1_ref[t]] = t
        o_ref[d2_ref[t]] = t
        return carry

    lax.fori_loop(0, o_ref.shape[0], clear, 0, unroll=GATHER_UNROLL)
    lax.fori_loop(0, d1_ref.shape[0], fill, 0, unroll=GATHER_UNROLL)


def _invert(d1, d2, n_rows):
    grid_spec = pltpu.PrefetchScalarGridSpec(
        num_scalar_prefetch=2, grid=(1,), in_specs=[],
        out_specs=pl.BlockSpec(memory_space=pltpu.SMEM))
    return pl.pallas_call(
        _invert_kernel,
        grid_spec=grid_spec,
        out_shape=jax.ShapeDtypeStruct((n_rows,), I32),
        compiler_params=_cparams("arbitrary"),
        name="moe_invert",
    )(d1, d2)


def _gather_rows(src_hbm, idx_ref, base, dst, sem):
    def body(j, carry):
        r = pl.multiple_of(idx_ref[base + j] * ROW_TILE, ROW_TILE)
        pltpu.make_async_copy(src_hbm.at[pl.ds(r, ROW_TILE), :],
                              dst.at[pl.ds(pl.multiple_of(j * ROW_TILE, ROW_TILE), ROW_TILE), :],
                              sem).start()
        return carry

    lax.fori_loop(0, dst.shape[0] // ROW_TILE, body, 0, unroll=GATHER_UNROLL)


def _wait_rows(src_hbm, dst, sem):
    pltpu.make_async_copy(src_hbm.at[pl.ds(0, dst.shape[0]), :], dst, sem).wait()


def _expert_kernel(te_ref, nt_ref, src_ref, x_hbm, wg_ref, wu_ref, wd_ref, o_ref, xbuf, wgb, wub, wdb, sem):
    tm = xbuf.shape[1] // ROW_TILE
    i = pl.program_id(0)
    n_used = nt_ref[0]
    slot = i % EXPERT_BUFS

    @pl.when(i == 0)
    def _():
        for t in range(EXPERT_BUFS - 1):
            @pl.when(t < n_used)
            def _():
                _gather_rows(x_hbm, src_ref, t * tm, xbuf.at[t], sem.at[t])

    ahead = i + EXPERT_BUFS - 1

    @pl.when(ahead < n_used)
    def _():
        _gather_rows(x_hbm, src_ref, ahead * tm, xbuf.at[ahead % EXPERT_BUFS], sem.at[ahead % EXPERT_BUFS])

    @pl.when((i == 0) | (te_ref[i] != te_ref[jnp.maximum(i - 1, 0)]))
    def _():
        wgb[...] = wg_ref[0].astype(BF16)
        wub[...] = wu_ref[0].astype(BF16)
        wdb[...] = wd_ref[0].astype(BF16)

    @pl.when(i < n_used)
    def _():
        _wait_rows(x_hbm, xbuf.at[slot], sem.at[slot])
        xn = _load_row_tiles(xbuf.at[slot], tm).astype(BF16)
        gate = jnp.dot(xn, wgb[...], preferred_element_type=F32)
        up = jnp.dot(xn, wub[...], preferred_element_type=F32)
        hid = (jax.nn.silu(gate) * up).astype(BF16)
        _store_row_tiles(o_ref, jnp.dot(hid, wdb[...], preferred_element_type=F32))

    @pl.when(i >= n_used)
    def _():
        o_ref[...] = jnp.zeros_like(o_ref)


def _experts(xn_tiles, tile_expert, n_used, src, w_gate, w_up, w_down):
    d = D_MODEL
    n_tiles = tile_expert.shape[0]
    tm = src.shape[0] // n_tiles
    de = w_gate.shape[-1]
    grid_spec = pltpu.PrefetchScalarGridSpec(
        num_scalar_prefetch=3,
        grid=(n_tiles,),
        in_specs=[pl.BlockSpec(memory_space=pl.ANY),
                  pl.BlockSpec((1, d, de), lambda i, te, nt, s: (te[i], 0, 0)),
                  pl.BlockSpec((1, d, de), lambda i, te, nt, s: (te[i], 0, 0)),
                  pl.BlockSpec((1, de, d), lambda i, te, nt, s: (te[i], 0, 0))],
        out_specs=pl.BlockSpec((tm * ROW_TILE, LANES), lambda i, te, nt, s: (i, 0)),
        scratch_shapes=[pltpu.VMEM((EXPERT_BUFS, tm * ROW_TILE, LANES), F32), pltpu.VMEM((d, de), BF16),
                        pltpu.VMEM((d, de), BF16), pltpu.VMEM((de, d), BF16),
                        pltpu.SemaphoreType.DMA((EXPERT_BUFS,))],
    )
    return pl.pallas_call(
        _expert_kernel,
        grid_spec=grid_spec,
        out_shape=jax.ShapeDtypeStruct((n_tiles * tm * ROW_TILE, LANES), F32),
        compiler_params=_cparams("arbitrary"),
        name="moe_experts",
    )(tile_expert, n_used, src, xn_tiles, w_gate, w_up, w_down)


def _combine_kernel(d1_ref, d2_ref, ys_hbm, h_ref, rg_ref, gf_ref, o_ref, y1, y2, sem, *, final_norm):
    tm = h_ref.shape[0]
    i = pl.program_id(0)
    slot = i % 2

    def gather(tile, s):
        _gather_rows(ys_hbm, d1_ref, tile * tm, y1.at[s], sem.at[0, s])
        _gather_rows(ys_hbm, d2_ref, tile * tm, y2.at[s], sem.at[1, s])

    @pl.when(i == 0)
    def _():
        gather(0, 0)

    @pl.when(i + 1 < pl.num_programs(0))
    def _():
        gather(i + 1, 1 - slot)

    _wait_rows(ys_hbm, y1.at[slot], sem.at[0, slot])
    _wait_rows(ys_hbm, y2.at[slot], sem.at[1, slot])
    rg = rg_ref[...]
    out = (h_ref[...] + rg[:, 0:1] * _load_row_tiles(y1.at[slot], tm)
           + rg[:, 1:2] * _load_row_tiles(y2.at[slot], tm))
    if final_norm:
        out = _rms_rows(out, gf_ref[...])
    o_ref[...] = out


def _combine(h, ys, d1, d2, rg, g_final, final_norm):
    n, d = h.shape
    tm = min(PROJ_TM, n)
    grid_spec = pltpu.PrefetchScalarGridSpec(
        num_scalar_prefetch=2,
        grid=(n // tm,),
        in_specs=[pl.BlockSpec(memory_space=pl.ANY),
                  pl.BlockSpec((tm, d), lambda i, a, b: (i, 0)),
                  pl.BlockSpec((tm, LANES), lambda i, a, b: (i, 0)),
                  pl.BlockSpec((1, d), lambda i, a, b: (0, 0))],
        out_specs=pl.BlockSpec((tm, d), lambda i, a, b: (i, 0)),
        scratch_shapes=[pltpu.VMEM((2, tm * ROW_TILE, LANES), F32), pltpu.VMEM((2, tm * ROW_TILE, LANES), F32),
                        pltpu.SemaphoreType.DMA((2, 2))],
    )
    return pl.pallas_call(
        functools.partial(_combine_kernel, final_norm=final_norm),
        grid_spec=grid_spec,
        out_shape=jax.ShapeDtypeStruct((n, d), F32),
        compiler_params=_cparams("arbitrary"),
        name="moe_combine",
    )(d1, d2, ys, h, rg, g_final.reshape(1, d))


def _hmoe_residual(h, xn_tiles, ri, rg, cnt, w_gate, w_up, w_down, g_final, final_norm):
    n, d = h.shape

    counts = cnt[0, :N_EXPERTS]
    n_tiles = (2 * n) // MOE_TM + N_EXPERTS
    tiles_per = (counts + MOE_TM - 1) // MOE_TM
    tile_end = jnp.cumsum(tiles_per)
    seg_start = (tile_end - tiles_per) * MOE_TM
    dest = _dest(ri, seg_start.astype(I32))
    d1, d2 = dest[0], dest[1]
    tile_ids = jnp.arange(n_tiles, dtype=I32)
    tile_expert = jnp.minimum(jnp.sum((tile_end[None, :] <= tile_ids[:, None]).astype(I32), axis=1), N_EXPERTS - 1)
    src = _invert(d1, d2, n_tiles * MOE_TM)

    n_used = tile_end[N_EXPERTS - 1:].astype(I32)
    ys = _experts(xn_tiles, tile_expert, n_used, src, w_gate, w_up, w_down)
    return _combine(h, ys, d1, d2, rg, g_final, final_norm)


def kernel(x, norm_mix, norm_ffn, norm_final, ev_w_in, ev_mu, rw_w0, rw_w2, rw_a0, rw_a2, rw_g2, rw_k_k, rw_k_a, rw_r_k, rw_ln_w, rw_ln_b, hg_lb_logits, hg_norm, ev_w_out, od_w_in, gd_conv, gd_a_log, gd_dt_bias, gd_norm, ml_i_bias, ml_f_bias, ml_norm, od_w_out, moe_w_group, moe_b_group, moe_w_router, moe_b_router, moe_w_gate, moe_w_up, moe_w_down):
    bsz, t, d = x.shape
    n = bsz * t
    depth = norm_mix.shape[0]
    lb_table = jnp.cumsum(jax.nn.softmax(hg_lb_logits.astype(F32), axis=0), axis=0)
    h = x.reshape(n, d)
    for layer in range(depth):
        j = layer // 2
        if layer % 2 == 0:
            w_in = ev_w_in[j].astype(BF16)
            p_rw, p_hg = _rms_proj(h, norm_mix[layer], [w_in[:, :RW_IN], w_in[:, RW_IN:]])
            ya = _rwkv7(p_rw.reshape(bsz, t, RW_IN), ev_mu[j], rw_w0[j], rw_w2[j], rw_a0[j], rw_a2[j], rw_g2[j],
                        rw_k_k[j], rw_k_a[j], rw_r_k[j], rw_ln_w[j], rw_ln_b[j])
            yb = _hgrn2(p_hg.reshape(bsz, t, HG_IN), lb_table[j], hg_norm[j])
            mix_a, mix_b, w_out = ya.reshape(n, RW_DIM), yb.reshape(n, HG_KDIM), ev_w_out[j]
        else:
            w_in = od_w_in[j]
            w_gates = jnp.concatenate([w_in[:, GD_MAIN:GD_IN], w_in[:, GD_IN + ML_MAIN:],
                                       jnp.zeros((d, LANES - 2 * GD_HEADS - 2 * ML_HEADS), F32)], axis=1)
            p_gd, p_ml, p_gt = _rms_proj(h, norm_mix[layer], [w_in[:, :GD_MAIN].astype(BF16),
                                                              w_in[:, GD_IN:GD_IN + ML_MAIN].astype(BF16),
                                                              w_gates.astype(BF16)])
            p_gt = p_gt.reshape(bsz, t, LANES)
            yc = _gdn(p_gd.reshape(bsz, t, GD_MAIN), p_gt, gd_conv[j], gd_a_log[j], gd_dt_bias[j], gd_norm[j])
            yd = _mlstm(p_ml.reshape(bsz, t, ML_MAIN), p_gt, ml_i_bias[j], ml_f_bias[j], ml_norm[j])
            mix_a, mix_b, w_out = yc.reshape(n, GD_DIM), yd.reshape(n, ML_DIM), od_w_out[j]
        h, xn_tiles, ri, rg, cnt = _out_proj_route(h, mix_a, mix_b, w_out, norm_ffn[layer], moe_w_group[layer],
                                                   moe_b_group[layer], moe_w_router[layer], moe_b_router[layer])
        h = _hmoe_residual(h, xn_tiles, ri, rg, cnt, moe_w_gate[layer], moe_w_up[layer], moe_w_down[layer],
                           norm_final, final_norm=(layer == depth - 1))
    return h.reshape(bsz, t, d)
```

```python
import functools
import math

import jax
import jax.numpy as jnp
from jax import lax
from jax.experimental import pallas as pl
from jax.experimental.pallas import tpu as pltpu

F32 = jnp.float32
BF16 = jnp.bfloat16
I32 = jnp.int32
HIGHEST = lax.Precision.HIGHEST

D_MODEL = 1024
NORM_EPS = 1e-6
RW_HEADS, RW_HEAD_DIM = 8, 64
RW_DIM = RW_HEADS * RW_HEAD_DIM
R_DECAY, R_AAA, R_GATE = 64, 64, 128
RW_IN = 3 * RW_DIM + R_DECAY + R_AAA + R_GATE
RW_LN_EPS = 64e-5
HG_HEADS, HG_DIM = 8, 64
HG_KDIM = HG_HEADS * HG_DIM
HG_IN = 4 * HG_KDIM
GD_HEADS, GD_HEAD_DIM = 4, 128
GD_DIM = GD_HEADS * GD_HEAD_DIM
CONV_K = 4
GD_MAIN = 4 * GD_DIM
GD_IN = GD_MAIN + 2 * GD_HEADS
ML_HEADS, ML_HEAD_DIM = 4, 128
ML_DIM = ML_HEADS * ML_HEAD_DIM
ML_MAIN = 4 * ML_DIM
N_GROUPS, EXPERTS_PER_GROUP = 4, 8
N_EXPERTS = N_GROUPS * EXPERTS_PER_GROUP
D_EXPERT = 256

LANES = 128
VMEM_LIMIT_BYTES = 48 * 1024 * 1024

PROJ_TM = 256
PACK_TM = 128
CHUNK = 64
SUB = 16
RW_NCH = 4
GD_NCH = 4
HG_ROWS = 128
ML_CHUNK = 128
MOE_TM = 256
ROUTE_ROWS = 40
EXPERT_BUFS = 3
GATHER_UNROLL = 8
NEG = -1e30


def _cparams(*sem):
    return pltpu.CompilerParams(dimension_semantics=sem, vmem_limit_bytes=VMEM_LIMIT_BYTES)


def _mm(a, b):
    return jnp.dot(a.astype(BF16), b.astype(BF16), preferred_element_type=F32)


def _mm_nt(a, b):
    return lax.dot_general(a.astype(BF16), b.astype(BF16), (((1,), (1,)), ((), ())), preferred_element_type=F32)


def _mm_tn(a, b):
    return lax.dot_general(a.astype(BF16), b.astype(BF16), (((0,), (0,)), ((), ())), preferred_element_type=F32)


def _mm_hi(a, b):
    return jnp.dot(a, b, precision=HIGHEST, preferred_element_type=F32)


def _mm_split(x, ones_bf16):
    hi = x.astype(BF16)
    lo = (x - hi.astype(F32)).astype(BF16)
    return (jnp.dot(hi, ones_bf16, preferred_element_type=F32) + jnp.dot(lo, ones_bf16, preferred_element_type=F32))


def _iota2(shape):
    return lax.broadcasted_iota(I32, shape, 0), lax.broadcasted_iota(I32, shape, 1)


def _rms_rows(x, g, eps=NORM_EPS):
    return x * lax.rsqrt(jnp.mean(x * x, axis=-1, keepdims=True) + eps) * g


def _seg_ones(width, seg):
    r, c = _iota2((width, width))
    return jnp.where((r // seg) == (c // seg), 1.0, 0.0).astype(BF16)


def _split3(x):
    x1 = x.astype(BF16)
    r1 = x - x1.astype(F32)
    x2 = r1.astype(BF16)
    return x1, x2, (r1 - x2.astype(F32)).astype(BF16)


def _cumsum_rows(tri_bf16, x):
    return sum(jnp.dot(tri_bf16, t, preferred_element_type=F32) for t in _split3(x))


def _cumsum_cols(x, tri_bf16):
    return sum(jnp.dot(t, tri_bf16, preferred_element_type=F32) for t in _split3(x))


def _mm3(a, b):
    ah = a.astype(BF16)
    al = (a - ah.astype(F32)).astype(BF16)
    bh = b.astype(BF16)
    bl = (b - bh.astype(F32)).astype(BF16)
    dot = lambda x, y: jnp.dot(x, y, preferred_element_type=F32)
    return dot(ah, bh) + dot(ah, bl) + dot(al, bh)


def _tri_inv_multi(ms, n, chain):
    assert chain // SUB <= 4
    r, c = _iota2((n, n))
    same = (r // SUB) == (c // SUB)
    eye = jnp.where(r == c, 1.0, 0.0).astype(F32)
    ds = [jnp.where(same, m, 0.0) for m in ms]
    offs = [m - d for m, d in zip(ms, ds)]
    xs = [eye - d for d in ds]
    ps = ds
    for _ in range(3):
        ps = [_mm(p, p) for p in ps]
        xs = [x + _mm(x, p) for x, p in zip(xs, ps)]
    es = [_mm(x, o) for x, o in zip(xs, offs)]
    imes = [eye - e for e in es]
    e2s = [_mm(e, e) for e in es]
    ys = [i + _mm(i, e2) for i, e2 in zip(imes, e2s)]
    xs = [_mm(y, x) for y, x in zip(ys, xs)]
    res = [eye - x - _mm3(m, x) for m, x in zip(ms, xs)]
    return [x + _mm(x, rr) for x, rr in zip(xs, res)]


def _pack_weights_kernel(w_ref, *o_refs, ranges, gate_ranges):
    w = w_ref[...]
    for (lo, hi), o_ref in zip(ranges, o_refs):
        o_ref[...] = w[:, lo:hi].astype(BF16)
    if gate_ranges:
        cols = [w[:, lo:hi] for lo, hi in gate_ranges]
        used = sum(hi - lo for lo, hi in gate_ranges)
        cols.append(jnp.zeros((w.shape[0], LANES - used), F32))
        o_refs[-1][...] = jnp.concatenate(cols, axis=1).astype(BF16)


def _pack_weights(w, ranges, gate_ranges=()):
    rows, cols = w.shape
    tm = PACK_TM
    widths = [hi - lo for lo, hi in ranges] + ([LANES] if gate_ranges else [])
    return pl.pallas_call(
        functools.partial(_pack_weights_kernel, ranges=tuple(ranges), gate_ranges=tuple(gate_ranges)),
        grid=(rows // tm,),
        in_specs=[pl.BlockSpec((tm, cols), lambda i: (i, 0))],
        out_specs=[pl.BlockSpec((tm, wd), lambda i: (i, 0)) for wd in widths],
        out_shape=[jax.ShapeDtypeStruct((rows, wd), BF16) for wd in widths],
        compiler_params=_cparams("parallel"),
        name="pack_weights",
    )(w)


def _rms_proj_kernel(x_ref, g_ref, *refs, n_out):
    y = _rms_rows(x_ref[...], g_ref[...]).astype(BF16)
    for w_ref, o_ref in zip(refs[:n_out], refs[n_out:]):
        o_ref[...] = jnp.dot(y, w_ref[...], preferred_element_type=F32)


def _rms_proj(x, g, ws):
    n, d = x.shape
    tm = min(PROJ_TM, n)
    in_specs = [pl.BlockSpec((tm, d), lambda i: (i, 0)), pl.BlockSpec((1, d), lambda i: (0, 0))]
    in_specs += [pl.BlockSpec(w.shape, lambda i: (0, 0)) for w in ws]
    return pl.pallas_call(
        functools.partial(_rms_proj_kernel, n_out=len(ws)),
        grid=(n // tm,),
        in_specs=in_specs,
        out_specs=[pl.BlockSpec((tm, w.shape[1]), lambda i: (i, 0)) for w in ws],
        out_shape=[jax.ShapeDtypeStruct((n, w.shape[1]), F32) for w in ws],
        compiler_params=_cparams("parallel"),
        name="rms_proj",
    )(x, g.reshape(1, d), *ws)


ROW_TILE = D_MODEL // LANES


def _store_row_tiles(ref, x):
    for j in range(ROW_TILE):
        ref[pl.ds(j, x.shape[0], stride=ROW_TILE), :] = x[:, j * LANES:(j + 1) * LANES]


def _load_row_tiles(ref, rows):
    return jnp.concatenate([ref[pl.ds(j, rows, stride=ROW_TILE), :] for j in range(ROW_TILE)], axis=1)


def _rwkv7_kernel(p_ref, mu_ref, w0_ref, w2_ref, a0_ref, a2_ref, g2_ref, kk_ref, ka_ref, rk_ref,
                  lnw_ref, lnb_ref, o_ref, prev_ref, zt_ref):
    L = CHUNK
    npair = RW_HEADS // 2
    c = pl.program_id(1)

    @pl.when(c == 0)
    def _():
        prev_ref[...] = jnp.zeros_like(prev_ref)
        zt_ref[...] = jnp.zeros_like(zt_ref)

    x = p_ref[0]
    R = x.shape[0]
    nch = R // L
    row = lax.broadcasted_iota(I32, x.shape, 0)
    xs = jnp.where(row == 0, prev_ref[7:8, :], pltpu.roll(x, 1, 0))
    prev_ref[...] = x[R - 8:R, :]
    pm = x + mu_ref[...] * (xs - x)
    r_all = pm[:, 0:RW_DIM]
    k_all = pm[:, RW_DIM:2 * RW_DIM]
    v_all = pm[:, 2 * RW_DIM:3 * RW_DIM]
    wa = pm[:, 3 * RW_DIM:3 * RW_DIM + LANES]
    gl = pm[:, 3 * RW_DIM + LANES:]
    wlog = -jax.nn.softplus(-(w0_ref[...] + _mm(jnp.tanh(wa), w2_ref[...]))) - 0.5
    ld = -jnp.exp(wlog)
    a_all = jax.nn.sigmoid(a0_ref[...] + _mm(wa, a2_ref[...]))
    g_all = _mm(jax.nn.sigmoid(gl), g2_ref[...])

    tr, tc = _iota2((R, R))
    chunk_tril = jnp.where((tc <= tr) & ((tr // L) == (tc // L)), 1.0, 0.0).astype(BF16)
    cs_all = _cumsum_rows(chunk_tril, ld)
    seg = _seg_ones(LANES, RW_HEAD_DIM)
    lane = lax.broadcasted_iota(I32, (L, LANES), 1)
    hm = (lane < RW_HEAD_DIM, lane >= RW_HEAD_DIM)
    br, bc = _iota2((2 * L, 2 * L))
    bd = (br // L) == (bc // L)
    bd_strict = bd & (bc < br)
    bd_incl = bd & (bc <= br)
    fold = lambda z: z[0:L] + z[L:2 * L]
    both = lambda z: jnp.concatenate([jnp.where(hm[0], z, 0.0), jnp.where(hm[1], z, 0.0)], axis=0)

    units = []
    for ci in range(nch):
        rs = slice(ci * L, (ci + 1) * L)
        for p in range(npair):
            sl = slice(p * LANES, (p + 1) * LANES)
            r, k, v, a = r_all[rs, sl], k_all[rs, sl], v_all[rs, sl], a_all[rs, sl]
            cs, ldp = cs_all[rs, sl], ld[rs, sl]
            kkr = k * kk_ref[:, sl]
            kk = kkr * lax.rsqrt(_mm_split(kkr * kkr, seg) + 1e-6)
            k2 = k * (1.0 + (a - 1.0) * ka_ref[:, sl])
            b = kk * a
            cs_last = cs[L - 1:L, :]
            e_neg = jnp.exp(-cs)
            e_rem = jnp.exp(cs_last - cs)
            bhat = b * e_neg
            khat = k2 * e_neg
            units.append(dict(p=p, rs=rs, sl=sl, r=r, v=v, k2=k2, rhat=r * jnp.exp(cs), gam_last=jnp.exp(cs_last),
                              btil=b * e_rem, ktil=k2 * e_rem, a2=both(kk * jnp.exp(cs - ldp)), v2=both(v),
                              rhs4=jnp.concatenate([bhat, bhat, khat, khat], axis=0)))
    for q in units:
        lhs = jnp.concatenate([q["a2"], both(q["rhat"])], axis=0)
        q["g"] = _mm_nt(lhs, q["rhs4"])
    tinvs = _tri_inv_multi([jnp.where(bd_strict, q["g"][0:2 * L, 0:2 * L], 0.0) for q in units], 2 * L, L)
    for q, tinv in zip(units, tinvs):
        q["tinv"] = tinv
        q["x2"] = _mm(jnp.where(bd_strict, q["g"][0:2 * L, 2 * L:4 * L], 0.0), q["v2"])
    for q in units:
        uw = _mm(q["tinv"], jnp.concatenate([q["x2"], q["a2"]], axis=1))
        q["u0"] = -fold(uw[:, 0:LANES])
        q["w"] = fold(uw[:, LANES:])
        q["y0"] = fold(_mm(jnp.where(bd_incl, q["g"][2 * L:4 * L, 2 * L:4 * L], 0.0), q["v2"]))
    for q in units:
        rb = jnp.where(bd_incl, q["g"][2 * L:4 * L, 0:2 * L], 0.0)
        ruw = _mm(rb, jnp.concatenate([both(q["u0"]), both(q["w"])], axis=1))
        q["yc"] = q["y0"] + fold(ruw[:, 0:LANES])
        q["ry"] = q["rhat"] - fold(ruw[:, LANES:])
        q["c1"] = _mm_tn(jnp.concatenate([q["u0"], q["v"]], axis=0), jnp.concatenate([q["btil"], q["ktil"]], axis=0))
        q["c2"] = _mm_tn(q["w"], q["btil"])

    hr, hc = _iota2((LANES, LANES))
    head_bd = (hr // RW_HEAD_DIM) == (hc // RW_HEAD_DIM)
    zts = [zt_ref[p] for p in range(npair)]
    for ci in range(nch):
        qs = units[ci * npair:(ci + 1) * npair]
        for q, zt in zip(qs, zts):
            q["y"] = q["yc"] + _mm_nt(q["ry"], zt)
        zts = [zt * q["gam_last"] + jnp.where(head_bd, q["c1"] - _mm(zt, q["c2"]), 0.0) for q, zt in zip(qs, zts)]
    for p in range(npair):
        zt_ref[p] = zts[p]

    for q in units:
        sl, rs, y = q["sl"], q["rs"], q["y"]
        mean = _mm_split(y, seg) * (1.0 / RW_HEAD_DIM)
        yc = y - mean
        var = _mm_split(yc * yc, seg) * (1.0 / RW_HEAD_DIM)
        yn = yc * lax.rsqrt(var + RW_LN_EPS) * lnw_ref[:, sl] + lnb_ref[:, sl]
        bonus = _mm_split(q["r"] * q["k2"] * rk_ref[:, sl], seg) * q["v"]
        o_ref[0, rs, sl] = (yn + bonus) * g_all[rs, sl]


def _rwkv7(p, mu, w0, w2, a0, a2, g2, k_k, k_a, r_k, ln_w, ln_b):
    bsz, t, _ = p.shape
    row = lambda z: z.reshape(1, -1).astype(F32)
    w2p = jnp.concatenate([w2, jnp.zeros_like(w2)], axis=0).astype(BF16)
    a2p = jnp.concatenate([jnp.zeros_like(a2), a2], axis=0).astype(BF16)
    params = [row(mu), row(w0), w2p, row(a0), a2p, g2.astype(BF16), row(k_k), row(k_a), row(r_k), row(ln_w), row(ln_b)]
    full = lambda z: pl.BlockSpec(z.shape, lambda b, c: (0, 0))
    rows = RW_NCH * CHUNK
    return pl.pallas_call(
        _rwkv7_kernel,
        grid=(bsz, t // rows),
        in_specs=[pl.BlockSpec((1, rows, RW_IN), lambda b, c: (b, c, 0))] + [full(z) for z in params],
        out_specs=pl.BlockSpec((1, rows, RW_DIM), lambda b, c: (b, c, 0)),
        out_shape=jax.ShapeDtypeStruct((bsz, t, RW_DIM), F32),
        scratch_shapes=[pltpu.VMEM((8, RW_IN), F32), pltpu.VMEM((RW_HEADS // 2, LANES, LANES), F32)],
        compiler_params=_cparams("arbitrary", "arbitrary"),
        name="rwkv7_mix",
    )(p, *params)


def _hgrn2_kernel(p_ref, lb_ref, ng_ref, o_ref, st_ref):
    L = HG_ROWS
    c = pl.program_id(1)

    @pl.when(c == 0)
    def _():
        st_ref[...] = jnp.zeros_like(st_ref)

    x = p_ref[0]
    lb = lb_ref[...]
    q_all = jax.nn.silu(x[:, 0:HG_KDIM])
    fg = lb + (1.0 - lb) * jax.nn.sigmoid(x[:, HG_KDIM:2 * HG_KDIM])
    k_all = 1.0 - fg
    logf = jnp.log(fg)
    v_all = x[:, 2 * HG_KDIM:3 * HG_KDIM]
    gate = x[:, 3 * HG_KDIM:]
    tr, tc = _iota2((L, L))
    blk_tril = jnp.where((tc <= tr) & ((tr // SUB) == (tc // SUB)), 1.0, 0.0).astype(BF16)
    bc_all = _cumsum_rows(blk_tril, logf)
    seg = _seg_ones(LANES, HG_DIM)
    br, bcc = _iota2((LANES, LANES))
    bd = (br // HG_DIM) == (bcc // HG_DIM)
    half = SUB // 2
    t_lo = lax.broadcasted_iota(I32, (SUB, LANES), 0)
    t_hi = lax.broadcasted_iota(I32, (half, LANES), 0) + half

    nsub = L // SUB
    npair = HG_HEADS // 2
    rows_per_unit = half * SUB + half * half
    sub_of_row = lax.broadcasted_iota(I32, (L, LANES), 0) // SUB

    units = []
    parts = []
    for p in range(npair):
        sl = slice(p * LANES, (p + 1) * LANES)
        for j in range(nsub):
            rs = slice(j * SUB, (j + 1) * SUB)
            q, k, v, bc = q_all[rs, sl], k_all[rs, sl], v_all[rs, sl], bc_all[rs, sl]
            q_hi, bc_hi = q[half:], bc[half:]
            for s in range(SUB):
                if s < half:
                    diff = jnp.where(t_lo >= s, bc - bc[s:s + 1, :], NEG)
                    parts.append(jnp.exp(diff) * q * k[s:s + 1, :])
                else:
                    diff = jnp.where(t_hi >= s, bc_hi - bc[s:s + 1, :], NEG)
                    parts.append(jnp.exp(diff) * q_hi * k[s:s + 1, :])
            bend = bc[SUB - 1:SUB, :]
            units.append(dict(v=v, qt=q * jnp.exp(bc), dec=jnp.exp(bend), kd=k * jnp.exp(bend - bc)))
    score_all = _mm(jnp.concatenate(parts, axis=0), seg)
    for ui, un in enumerate(units):
        score = score_all[ui * rows_per_unit:(ui + 1) * rows_per_unit]
        v = un["v"]
        acc_lo = jnp.zeros((half, LANES), F32)
        acc_hi = jnp.zeros((half, LANES), F32)
        off = 0
        for s in range(SUB):
            vs = v[s:s + 1, :]
            if s < half:
                acc_lo = acc_lo + score[off:off + half] * vs
                acc_hi = acc_hi + score[off + half:off + SUB] * vs
                off += SUB
            else:
                acc_hi = acc_hi + score[off:off + half] * vs
                off += half
        un["intra"] = jnp.concatenate([acc_lo, acc_hi], axis=0)
    for p in range(npair):
        us = units[p * nsub:(p + 1) * nsub]
        v_pair = jnp.concatenate([un["v"] for un in us], axis=0)
        kd_pair = jnp.concatenate([un["kd"] for un in us], axis=0)
        kd_wide = jnp.concatenate([jnp.where(sub_of_row == j, kd_pair, 0.0) for j in range(nsub)], axis=1)
        upd = _mm_tn(v_pair, kd_wide)
        for j, un in enumerate(us):
            un["upd"] = jnp.where(bd, upd[:, j * LANES:(j + 1) * LANES], 0.0)

    for p in range(npair):
        sl = slice(p * LANES, (p + 1) * LANES)
        st = st_ref[p]
        outs = []
        for un in units[p * nsub:(p + 1) * nsub]:
            outs.append(un["intra"] + _mm_nt(un["qt"], st))
            st = st * un["dec"] + un["upd"]
        st_ref[p] = st
        o = jnp.concatenate(outs, axis=0)
        ms = _mm_split(o * o, seg) * (1.0 / HG_DIM)
        o_ref[0, :, sl] = o * lax.rsqrt(ms + NORM_EPS) * ng_ref[:, sl] * jax.nn.silu(gate[:, sl])


def _hgrn2(p, lb, norm_g):
    bsz, t, _ = p.shape
    lb = lb.reshape(1, HG_KDIM).astype(F32)
    ng = jnp.tile(norm_g.astype(F32), HG_HEADS).reshape(1, HG_KDIM)
    return pl.pallas_call(
        _hgrn2_kernel,
        grid=(bsz, t // HG_ROWS),
        in_specs=[pl.BlockSpec((1, HG_ROWS, HG_IN), lambda b, c: (b, c, 0)),
                  pl.BlockSpec((1, HG_KDIM), lambda b, c: (0, 0)), pl.BlockSpec((1, HG_KDIM), lambda b, c: (0, 0))],
        out_specs=pl.BlockSpec((1, HG_ROWS, HG_KDIM), lambda b, c: (b, c, 0)),
        out_shape=jax.ShapeDtypeStruct((bsz, t, HG_KDIM), F32),
        scratch_shapes=[pltpu.VMEM((HG_HEADS // 2, LANES, LANES), F32)],
        compiler_params=_cparams("arbitrary", "arbitrary"),
        name="hgrn2_mix",
    )(p, lb, ng)


def _gdn_kernel(p_ref, pg_ref, cw_ref, gb_ref, nal_ref, nalc_ref, ng_ref, o_ref, prev_ref, s_ref):
    L = CHUNK
    c = pl.program_id(1)

    @pl.when(c == 0)
    def _():
        prev_ref[...] = jnp.zeros_like(prev_ref)
        s_ref[...] = jnp.zeros_like(s_ref)

    x = p_ref[0]
    R = x.shape[0]
    nch = R // L
    xq = x[:, 0:3 * GD_DIM]
    xcat = jnp.concatenate([prev_ref[...], xq], axis=0)
    conv = xq * cw_ref[CONV_K - 1:CONV_K, :]
    for j in range(1, CONV_K):
        conv = conv + xcat[8 - j:8 - j + R, :] * cw_ref[CONV_K - 1 - j:CONV_K - j, :]
    prev_ref[...] = xq[R - 8:R, :]
    qkv = jax.nn.silu(conv)
    z = x[:, 3 * GD_DIM:]

    gt = pg_ref[0] + gb_ref[...]
    beta_all = jax.nn.sigmoid(gt)
    g_all = nal_ref[...] * jax.nn.softplus(gt)
    g_t = nalc_ref[...] * jax.nn.softplus(gt.T)
    tr, tc = _iota2((R, R))
    chunk_tril = jnp.where((tc <= tr) & ((tr // L) == (tc // L)), 1.0, 0.0).astype(BF16)
    gam_all = _cumsum_rows(chunk_tril, g_all)
    ur, uc = _iota2((R, 2 * R))
    triu2 = jnp.where(((ur // L) == (uc // (2 * L))) & ((ur % L) <= (uc % L)), 1.0, 0.0).astype(BF16)
    gam_t2 = _cumsum_cols(g_t, triu2)
    lr, lc = _iota2((L, L))
    incl = lc <= lr
    br, bc = _iota2((2 * L, 2 * L))
    bd = (br // L) == (bc // L)
    bd_strict = bd & (bc < br)
    bd_incl = bd & (bc <= br)
    lane2 = lax.broadcasted_iota(I32, (1, 2 * L), 1)
    zero = jnp.zeros((L, LANES), F32)

    units = []
    for ci in range(nch):
        rs = slice(ci * L, (ci + 1) * L)
        for h in range(GD_HEADS):
            q = qkv[rs, h * LANES:(h + 1) * LANES]
            k = qkv[rs, GD_DIM + h * LANES:GD_DIM + (h + 1) * LANES]
            v = qkv[rs, 2 * GD_DIM + h * LANES:2 * GD_DIM + (h + 1) * LANES]
            q = q * lax.rsqrt(jnp.sum(q * q, axis=-1, keepdims=True) + 1e-6) * (GD_HEAD_DIM ** -0.5)
            k = k * lax.rsqrt(jnp.sum(k * k, axis=-1, keepdims=True) + 1e-6)
            beta = beta_all[rs, h:h + 1]
            gam = gam_all[rs, GD_HEADS + h:GD_HEADS + h + 1]
            gam_row2 = gam_t2[GD_HEADS + h:GD_HEADS + h + 1, ci * 2 * L:(ci + 1) * 2 * L]
            gam_last = gam[L - 1:L, :]
            kb = k * beta
            units.append(dict(h=h, rs=rs, q=q, k=k, kb=kb, vb=v * beta, gam=gam, gam_row2=gam_row2,
                              gam_last=gam_last, kg=kb * jnp.exp(gam), qg=q * jnp.exp(gam),
                              kd=k * jnp.exp(gam_last - gam),
                              decay=jnp.exp(jnp.where(incl, gam - gam_row2[:, 0:L], NEG))))
    pairs = [(units[i], units[i + 1]) for i in range(0, len(units), 2)]
    ms = []
    for h0, h1 in pairs:
        lhs = jnp.concatenate([jnp.concatenate([h0["kb"], zero], axis=1),
                               jnp.concatenate([zero, h1["kb"]], axis=1)], axis=0)
        rhs = jnp.concatenate([jnp.concatenate([h0["k"], zero], axis=1),
                               jnp.concatenate([zero, h1["k"]], axis=1)], axis=0)
        gam_col = jnp.concatenate([h0["gam"], h1["gam"]], axis=0)
        gam_row = jnp.where(lane2 < L, h0["gam_row2"], h1["gam_row2"])
        decay2 = jnp.exp(jnp.where(bd_incl, gam_col - gam_row, NEG))
        ms.append(jnp.where(bd_strict, _mm_nt(lhs, rhs) * decay2, 0.0))
    tinvs = _tri_inv_multi(ms, 2 * L, L)
    for (h0, h1), tinv in zip(pairs, tinvs):
        rhs = jnp.concatenate([jnp.concatenate([h0["vb"], h0["kg"]], axis=1),
                               jnp.concatenate([h1["vb"], h1["kg"]], axis=1)], axis=0)
        uw = _mm(tinv, rhs)
        h0["u"], h0["w"] = uw[0:L, 0:LANES], uw[0:L, LANES:]
        h1["u"], h1["w"] = uw[L:2 * L, 0:LANES], uw[L:2 * L, LANES:]
    for hd in units:
        attn = _mm_nt(hd["q"], hd["k"]) * hd["decay"]
        uw = jnp.concatenate([hd["u"], hd["w"]], axis=1)
        auw = _mm(attn, uw)
        hd["o0"] = auw[:, 0:LANES]
        hd["qs"] = hd["qg"] - auw[:, LANES:]
        cc = _mm_tn(hd["kd"], uw)
        hd["c1"], hd["c2"] = cc[:, 0:LANES], cc[:, LANES:]

    ss = [s_ref[h] for h in range(GD_HEADS)]
    for ci in range(nch):
        hds = units[ci * GD_HEADS:(ci + 1) * GD_HEADS]
        for hd, s in zip(hds, ss):
            hd["o"] = hd["o0"] + _mm(hd["qs"], s)
        ss = [s * jnp.exp(hd["gam_last"]) + hd["c1"] - _mm(hd["c2"], s) for hd, s in zip(hds, ss)]
    for h in range(GD_HEADS):
        s_ref[h] = ss[h]
    for hd in units:
        sl = slice(hd["h"] * LANES, (hd["h"] + 1) * LANES)
        o_ref[0, hd["rs"], sl] = _rms_rows(hd["o"], ng_ref[...]) * jax.nn.silu(z[hd["rs"], sl])


def _gate_row(lo, vals):
    return jnp.zeros((1, LANES), F32).at[0, lo:lo + vals.shape[0]].set(vals.astype(F32))


def _gdn(p, pg, conv_w, a_log, dt_bias, norm_g):
    bsz, t, _ = p.shape
    gbias = _gate_row(GD_HEADS, dt_bias)
    nal = _gate_row(GD_HEADS, -jnp.exp(a_log))
    nal_col = nal.reshape(LANES, 1)
    ng = norm_g.reshape(1, GD_HEAD_DIM).astype(F32)
    full = lambda z: pl.BlockSpec(z.shape, lambda b, c: (0, 0))
    rows = GD_NCH * CHUNK
    return pl.pallas_call(
        _gdn_kernel,
        grid=(bsz, t // rows),
        in_specs=[pl.BlockSpec((1, rows, GD_MAIN), lambda b, c: (b, c, 0)),
                  pl.BlockSpec((1, rows, LANES), lambda b, c: (b, c, 0)),
                  full(conv_w), full(gbias), full(nal), full(nal_col), full(ng)],
        out_specs=pl.BlockSpec((1, rows, GD_DIM), lambda b, c: (b, c, 0)),
        out_shape=jax.ShapeDtypeStruct((bsz, t, GD_DIM), F32),
        scratch_shapes=[pltpu.VMEM((8, 3 * GD_DIM), F32), pltpu.VMEM((GD_HEADS, LANES, LANES), F32)],
        compiler_params=_cparams("arbitrary", "arbitrary"),
        name="gdn_mix",
    )(p, pg, conv_w.astype(F32), gbias, nal, nal_col, ng)


def _mlstm_kernel(p_ref, pg_ref, gb_ref, ng_ref, o_ref, c_ref, n_ref, m_ref):
    L = ML_CHUNK
    ci = pl.program_id(1)

    @pl.when(ci == 0)
    def _():
        c_ref[...] = jnp.zeros_like(c_ref)
        n_ref[...] = jnp.zeros_like(n_ref)
        m_ref[...] = jnp.zeros_like(m_ref)

    x = p_ref[0]
    gt = pg_ref[0] + gb_ref[...]
    logf = jax.nn.log_sigmoid(gt)
    gt_t = gt.T
    logf_t = jax.nn.log_sigmoid(gt_t)
    tr, tc = _iota2((L, L))
    bc_all = _cumsum_rows(jnp.where(tc <= tr, 1.0, 0.0).astype(BF16), logf)
    bc_t = _cumsum_cols(logf_t, jnp.where(tr <= tc, 1.0, 0.0).astype(BF16))
    incl = tc <= tr
    i_lo, f_lo = 2 * GD_HEADS, 2 * GD_HEADS + ML_HEADS

    heads = []
    for h in range(ML_HEADS):
        sl = slice(h * LANES, (h + 1) * LANES)
        q = x[:, sl]
        k = x[:, ML_DIM + h * LANES:ML_DIM + (h + 1) * LANES] * (ML_HEAD_DIM ** -0.5)
        v = x[:, 2 * ML_DIM + h * LANES:2 * ML_DIM + (h + 1) * LANES]
        bc = bc_all[:, f_lo + h:f_lo + h + 1]
        bc_row = bc_t[f_lo + h:f_lo + h + 1, :]
        ig = gt[:, i_lo + h:i_lo + h + 1]
        ig_row = gt_t[i_lo + h:i_lo + h + 1, :]
        m_prev = m_ref[h][0:1, 0:1]
        b_last = bc[L - 1:L, :]
        d_log = jnp.where(incl, bc - bc_row + ig_row, NEG)
        inter_log = bc + m_prev
        m_t = jnp.maximum(inter_log, jnp.max(d_log, axis=-1, keepdims=True))
        upd_log = b_last - bc + ig
        m_new = jnp.maximum(b_last + m_prev, jnp.max(upd_log, axis=0, keepdims=True))
        wk = jnp.exp(upd_log - m_new) * k
        heads.append(dict(sl=sl, q=q, v=v, m_t=m_t, m_new=m_new, wk=wk, inter_w=jnp.exp(inter_log - m_t),
                          dec=jnp.exp(b_last + m_prev - m_new), sc=_mm_nt(q, k) * jnp.exp(d_log - m_t)))
    for hd in heads:
        hd["scv"] = _mm(hd["sc"], hd["v"])
        hd["kv"] = _mm_tn(hd["wk"], hd["v"])
    for h, hd in enumerate(heads):
        cmat = c_ref[h]
        nvec = n_ref[h][0:1, :]
        q, sc, inter_w, dec = hd["q"], hd["sc"], hd["inter_w"], hd["dec"]
        num = inter_w * _mm(q, cmat) + hd["scv"]
        den = inter_w * jnp.sum(q * nvec, axis=-1, keepdims=True) + jnp.sum(sc, axis=-1, keepdims=True)
        hd["hh"] = num / jnp.maximum(jnp.abs(den), jnp.exp(-hd["m_t"]))
        c_ref[h] = dec * cmat + hd["kv"]
        n_ref[h] = jnp.broadcast_to(dec * nvec + jnp.sum(hd["wk"], axis=0, keepdims=True), (8, LANES))
        m_ref[h] = jnp.broadcast_to(hd["m_new"], (8, LANES))
    for h, hd in enumerate(heads):
        sl = hd["sl"]
        og = jax.nn.sigmoid(x[:, 3 * ML_DIM + h * LANES:3 * ML_DIM + (h + 1) * LANES])
        o_ref[0, :, sl] = og * _rms_rows(hd["hh"], ng_ref[:, sl])


def _mlstm(p, pg, i_bias, f_bias, norm_g):
    bsz, t, _ = p.shape
    gbias = _gate_row(2 * GD_HEADS, jnp.concatenate([i_bias, f_bias]))
    ng = norm_g.reshape(1, ML_DIM).astype(F32)
    lc = min(ML_CHUNK, t)
    full = lambda z: pl.BlockSpec(z.shape, lambda b, c: (0, 0))
    return pl.pallas_call(
        _mlstm_kernel,
        grid=(bsz, t // lc),
        in_specs=[pl.BlockSpec((1, lc, ML_MAIN), lambda b, c: (b, c, 0)),
                  pl.BlockSpec((1, lc, LANES), lambda b, c: (b, c, 0)), full(gbias), full(ng)],
        out_specs=pl.BlockSpec((1, lc, ML_DIM), lambda b, c: (b, c, 0)),
        out_shape=jax.ShapeDtypeStruct((bsz, t, ML_DIM), F32),
        scratch_shapes=[pltpu.VMEM((ML_HEADS, LANES, LANES), F32), pltpu.VMEM((ML_HEADS, 8, LANES), F32),
                        pltpu.VMEM((ML_HEADS, 8, LANES), F32)],
        compiler_params=_cparams("arbitrary", "arbitrary"),
        name="mlstm_mix",
    )(p, pg, gbias, ng)


def _route(h, g_ref, wh_ref, wl_ref, b_ref, off_ref):
    tm = h.shape[0]
    xn = _rms_rows(h, g_ref[...])
    xh = xn.astype(BF16)
    xl = (xn - xh.astype(F32)).astype(BF16)
    dot = lambda a, b: jnp.dot(a, b, preferred_element_type=F32)
    logits = dot(xh, wh_ref[...]) + dot(xh, wl_ref[...]) + dot(xl, wh_ref[...]) + b_ref[...]
    lt = logits.T[0:ROUTE_ROWS, :]
    row = lax.broadcasted_iota(I32, lt.shape, 0)
    big = jnp.int32(1 << 20)
    is_grp = (row >= N_EXPERTS) & (row < N_EXPERTS + N_GROUPS)
    lg = jnp.where(is_grp, lt, NEG)
    gmax = jnp.max(lg, axis=0, keepdims=True)
    p_top = 1.0 / jnp.sum(jnp.exp(lg - gmax), axis=0, keepdims=True)
    g_idx = jnp.min(jnp.where(lg == gmax, row, big), axis=0, keepdims=True) - N_EXPERTS
    valid = (row < N_EXPERTS) & ((row // EXPERTS_PER_GROUP) == g_idx)
    v1 = jnp.where(valid, lt, NEG)
    m1 = jnp.max(v1, axis=0, keepdims=True)
    i1 = jnp.min(jnp.where(v1 == m1, row, big), axis=0, keepdims=True)
    v2 = jnp.where(row == i1, NEG, v1)
    m2 = jnp.max(v2, axis=0, keepdims=True)
    i2 = jnp.min(jnp.where(v2 == m2, row, big), axis=0, keepdims=True)
    e21 = jnp.exp(m2 - m1)
    gate1 = p_top / (1.0 + e21)
    gate2 = p_top * e21 / (1.0 + e21)

    sel1 = row == i1
    sel2 = row == i2
    onehot = jnp.where(sel1 | sel2, 1.0, 0.0)
    tr, tc = _iota2((tm, tm))
    earlier = jnp.where(tr < tc, 1.0, 0.0).astype(BF16)
    off = off_ref[0:ROUTE_ROWS, 0:1]
    prefix = jnp.dot(onehot.astype(BF16), earlier, preferred_element_type=F32) + off
    rank1 = jnp.sum(jnp.where(sel1, prefix, 0.0), axis=0, keepdims=True)
    rank2 = jnp.sum(jnp.where(sel2, prefix, 0.0), axis=0, keepdims=True)
    off_ref[0:ROUTE_ROWS, :] = jnp.broadcast_to(off + jnp.sum(onehot, axis=1, keepdims=True), (ROUTE_ROWS, LANES))

    packed = jnp.concatenate([i1.astype(F32), i2.astype(F32), rank1, rank2, gate1, gate2,
                              jnp.zeros((2, tm), F32)], axis=0)
    return xn, packed


def _out_proj_route_kernel(h_ref, ya_ref, yb_ref, wa_ref, wb_ref, g_ref, wh_ref, wl_ref, b_ref,
                           o_ref, xt_ref, rg_ref, cnt_ref, dest_ref, off_ref, rt_ref):
    phase = pl.program_id(0)
    s = pl.program_id(1)
    tm = h_ref.shape[0]

    @pl.when((phase == 0) & (s == 0))
    def _():
        off_ref[...] = jnp.zeros_like(off_ref)

    @pl.when(phase == 0)
    def _():
        out = h_ref[...] + _mm(ya_ref[...], wa_ref[...]) + _mm(yb_ref[...], wb_ref[...])
        o_ref[...] = out
        xn, packed = _route(out, g_ref, wh_ref, wl_ref, b_ref, off_ref)
        _store_row_tiles(xt_ref, xn)
        rt_ref[s] = packed
        wide = jnp.concatenate([packed, jnp.zeros((LANES - 8, tm), F32)], axis=0)
        rg_ref[...] = wide.T
        cnt_ref[...] = off_ref[...]

    @pl.when(phase == 1)
    def _():
        counts = off_ref[...]
        tiles_per = jnp.floor((counts + (MOE_TM - 1)) * (1.0 / MOE_TM))
        er, ec = _iota2((LANES, LANES))
        tile_end = _mm(jnp.where(ec <= er, 1.0, 0.0), tiles_per)
        seg_start = ((tile_end - tiles_per) * MOE_TM)[0:ROUTE_ROWS, 0:1]
        blk = rt_ref[s]
        row = lax.broadcasted_iota(I32, (ROUTE_ROWS, tm), 0).astype(F32)
        pick = lambda e: jnp.sum(jnp.where(row == e, seg_start, 0.0), axis=0, keepdims=True)
        d1 = pick(blk[0:1]) + blk[2:3]
        d2 = pick(blk[1:2]) + blk[3:4]
        dest_ref[...] = jnp.concatenate([d1, d2, jnp.zeros((6, tm), F32)], axis=0).astype(I32)


def _out_proj_route(h, ya, yb, w_out, g_ffn, w_group, b_group, w_router, b_router):
    n, d = h.shape
    da, db = ya.shape[1], yb.shape[1]
    tm = min(PROJ_TM, n)
    wa = w_out[:da].astype(BF16)
    wb = w_out[da:].astype(BF16)
    pad = LANES - N_EXPERTS - N_GROUPS
    w_cat = jnp.concatenate([w_router, w_group, jnp.zeros((d, pad), F32)], axis=1)
    b_cat = jnp.concatenate([b_router, b_group, jnp.zeros((pad,), F32)]).reshape(1, LANES)
    w_hi = w_cat.astype(BF16)
    w_lo = (w_cat - w_hi.astype(F32)).astype(BF16)
    steps = n // tm
    row = lambda p, s: (s * (1 - p) + (steps - 1) * p, 0)
    fixed = lambda p, s: (0, 0)
    return pl.pallas_call(
        _out_proj_route_kernel,
        grid=(2, steps),
        in_specs=[pl.BlockSpec((tm, d), row), pl.BlockSpec((tm, da), row), pl.BlockSpec((tm, db), row),
                  pl.BlockSpec((da, d), fixed), pl.BlockSpec((db, d), fixed), pl.BlockSpec((1, d), fixed),
                  pl.BlockSpec((d, LANES), fixed), pl.BlockSpec((d, LANES), fixed), pl.BlockSpec((1, LANES), fixed)],
        out_specs=[pl.BlockSpec((tm, d), row), pl.BlockSpec((tm * ROW_TILE, LANES), row),
                   pl.BlockSpec((tm, LANES), row), pl.BlockSpec((LANES, LANES), fixed),
                   pl.BlockSpec((8, tm), lambda p, s: (0, s * p))],
        out_shape=[jax.ShapeDtypeStruct((n, d), F32), jax.ShapeDtypeStruct((n * ROW_TILE, LANES), F32),
                   jax.ShapeDtypeStruct((n, LANES), F32), jax.ShapeDtypeStruct((LANES, LANES), F32),
                   jax.ShapeDtypeStruct((8, n), I32)],
        scratch_shapes=[pltpu.VMEM((LANES, LANES), F32), pltpu.VMEM((steps, 8, tm), F32)],
        compiler_params=_cparams("arbitrary", "arbitrary"),
        name="out_proj_route",
    )(h, ya, yb, wa, wb, g_ffn.reshape(1, d), w_hi, w_lo, b_cat)


def _invert_kernel(d1_ref, d2_ref, o_ref):
    def clear(i, carry):
        o_ref[i] = 0
        return carry

    def fill(t, carry):
        o_ref[d1_ref[t]] = t
        o_ref[d2_ref[t]] = t
        return carry

    lax.fori_loop(0, o_ref.shape[0], clear, 0, unroll=GATHER_UNROLL)
    lax.fori_loop(0, d1_ref.shape[0], fill, 0, unroll=GATHER_UNROLL)


def _invert(d1, d2, n_rows):
    grid_spec = pltpu.PrefetchScalarGridSpec(
        num_scalar_prefetch=2, grid=(1,), in_specs=[],
        out_specs=pl.BlockSpec(memory_space=pltpu.SMEM))
    return pl.pallas_call(
        _invert_kernel,
        grid_spec=grid_spec,
        out_shape=jax.ShapeDtypeStruct((n_rows,), I32),
        compiler_params=_cparams("arbitrary"),
        name="moe_invert",
    )(d1, d2)


def _gather_rows(src_hbm, idx_ref, base, dst, sem):
    def body(j, carry):
        r = pl.multiple_of(idx_ref[base + j] * ROW_TILE, ROW_TILE)
        pltpu.make_async_copy(src_hbm.at[pl.ds(r, ROW_TILE), :],
                              dst.at[pl.ds(pl.multiple_of(j * ROW_TILE, ROW_TILE), ROW_TILE), :],
                              sem).start()
        return carry

    lax.fori_loop(0, dst.shape[0] // ROW_TILE, body, 0, unroll=GATHER_UNROLL)


def _wait_rows(src_hbm, dst, sem):
    pltpu.make_async_copy(src_hbm.at[pl.ds(0, dst.shape[0]), :], dst, sem).wait()


def _expert_kernel(te_ref, nt_ref, src_ref, x_hbm, wg_ref, wu_ref, wd_ref, o_ref, xbuf, wgb, wub, wdb, sem):
    tm = xbuf.shape[1] // ROW_TILE
    i = pl.program_id(0)
    n_used = nt_ref[0]
    slot = i % EXPERT_BUFS

    @pl.when(i == 0)
    def _():
        for t in range(EXPERT_BUFS - 1):
            @pl.when(t < n_used)
            def _():
                _gather_rows(x_hbm, src_ref, t * tm, xbuf.at[t], sem.at[t])

    ahead = i + EXPERT_BUFS - 1

    @pl.when(ahead < n_used)
    def _():
        _gather_rows(x_hbm, src_ref, ahead * tm, xbuf.at[ahead % EXPERT_BUFS], sem.at[ahead % EXPERT_BUFS])

    @pl.when((i == 0) | (te_ref[i] != te_ref[jnp.maximum(i - 1, 0)]))
    def _():
        wgb[...] = wg_ref[0].astype(BF16)
        wub[...] = wu_ref[0].astype(BF16)
        wdb[...] = wd_ref[0].astype(BF16)

    @pl.when(i < n_used)
    def _():
        _wait_rows(x_hbm, xbuf.at[slot], sem.at[slot])
        xn = _load_row_tiles(xbuf.at[slot], tm).astype(BF16)
        gate = jnp.dot(xn, wgb[...], preferred_element_type=F32)
        up = jnp.dot(xn, wub[...], preferred_element_type=F32)
        hid = (jax.nn.silu(gate) * up).astype(BF16)
        _store_row_tiles(o_ref, jnp.dot(hid, wdb[...], preferred_element_type=F32))

    @pl.when(i >= n_used)
    def _():
        o_ref[...] = jnp.zeros_like(o_ref)


def _experts(xn_tiles, tile_expert, n_used, src, w_gate, w_up, w_down):
    d = D_MODEL
    n_tiles = tile_expert.shape[0]
    tm = src.shape[0] // n_tiles
    de = w_gate.shape[-1]
    grid_spec = pltpu.PrefetchScalarGridSpec(
        num_scalar_prefetch=3,
        grid=(n_tiles,),
        in_specs=[pl.BlockSpec(memory_space=pl.ANY),
                  pl.BlockSpec((1, d, de), lambda i, te, nt, s: (te[i], 0, 0)),
                  pl.BlockSpec((1, d, de), lambda i, te, nt, s: (te[i], 0, 0)),
                  pl.BlockSpec((1, de, d), lambda i, te, nt, s: (te[i], 0, 0))],
        out_specs=pl.BlockSpec((tm * ROW_TILE, LANES), lambda i, te, nt, s: (i, 0)),
        scratch_shapes=[pltpu.VMEM((EXPERT_BUFS, tm * ROW_TILE, LANES), F32), pltpu.VMEM((d, de), BF16),
                        pltpu.VMEM((d, de), BF16), pltpu.VMEM((de, d), BF16),
                        pltpu.SemaphoreType.DMA((EXPERT_BUFS,))],
    )
    return pl.pallas_call(
        _expert_kernel,
        grid_spec=grid_spec,
        out_shape=jax.ShapeDtypeStruct((n_tiles * tm * ROW_TILE, LANES), F32),
        compiler_params=_cparams("arbitrary"),
        name="moe_experts",
    )(tile_expert, n_used, src, xn_tiles, w_gate, w_up, w_down)


def _combine_kernel(d1_ref, d2_ref, ys_hbm, h_ref, rg_ref, gf_ref, o_ref, y1, y2, sem, *, final_norm):
    tm = h_ref.shape[0]
    i = pl.program_id(0)
    slot = i % 2

    def gather(tile, s):
        _gather_rows(ys_hbm, d1_ref, tile * tm, y1.at[s], sem.at[0, s])
        _gather_rows(ys_hbm, d2_ref, tile * tm, y2.at[s], sem.at[1, s])

    @pl.when(i == 0)
    def _():
        gather(0, 0)

    @pl.when(i + 1 < pl.num_programs(0))
    def _():
        gather(i + 1, 1 - slot)

    _wait_rows(ys_hbm, y1.at[slot], sem.at[0, slot])
    _wait_rows(ys_hbm, y2.at[slot], sem.at[1, slot])
    rg = rg_ref[...]
    out = (h_ref[...] + rg[:, 4:5] * _load_row_tiles(y1.at[slot], tm)
           + rg[:, 5:6] * _load_row_tiles(y2.at[slot], tm))
    if final_norm:
        out = _rms_rows(out, gf_ref[...])
    o_ref[...] = out


def _combine(h, ys, d1, d2, rg, g_final, final_norm):
    n, d = h.shape
    tm = min(PROJ_TM, n)
    grid_spec = pltpu.PrefetchScalarGridSpec(
        num_scalar_prefetch=2,
        grid=(n // tm,),
        in_specs=[pl.BlockSpec(memory_space=pl.ANY),
                  pl.BlockSpec((tm, d), lambda i, a, b: (i, 0)),
                  pl.BlockSpec((tm, LANES), lambda i, a, b: (i, 0)),
                  pl.BlockSpec((1, d), lambda i, a, b: (0, 0))],
        out_specs=pl.BlockSpec((tm, d), lambda i, a, b: (i, 0)),
        scratch_shapes=[pltpu.VMEM((2, tm * ROW_TILE, LANES), F32), pltpu.VMEM((2, tm * ROW_TILE, LANES), F32),
                        pltpu.SemaphoreType.DMA((2, 2))],
    )
    return pl.pallas_call(
        functools.partial(_combine_kernel, final_norm=final_norm),
        grid_spec=grid_spec,
        out_shape=jax.ShapeDtypeStruct((n, d), F32),
        compiler_params=_cparams("arbitrary"),
        name="moe_combine",
    )(d1, d2, ys, h, rg, g_final.reshape(1, d))


def _hmoe_residual(h, xn_tiles, rg, cnt, dest, w_gate, w_up, w_down, g_final, final_norm):
    n, d = h.shape

    counts = cnt[:N_EXPERTS, 0].astype(I32)
    n_tiles = (2 * n) // MOE_TM + N_EXPERTS
    tile_end = jnp.cumsum((counts + MOE_TM - 1) // MOE_TM)
    d1, d2 = dest[0], dest[1]
    tile_ids = jnp.arange(n_tiles, dtype=I32)
    tile_expert = jnp.minimum(jnp.sum((tile_end[None, :] <= tile_ids[:, None]).astype(I32), axis=1), N_EXPERTS - 1)
    src = _invert(d1, d2, n_tiles * MOE_TM)

    n_used = tile_end[N_EXPERTS - 1:].astype(I32)
    ys = _experts(xn_tiles, tile_expert, n_used, src, w_gate, w_up, w_down)
    return _combine(h, ys, d1, d2, rg, g_final, final_norm)


def kernel(x, norm_mix, norm_ffn, norm_final, ev_w_in, ev_mu, rw_w0, rw_w2, rw_a0, rw_a2, rw_g2, rw_k_k, rw_k_a, rw_r_k, rw_ln_w, rw_ln_b, hg_lb_logits, hg_norm, ev_w_out, od_w_in, gd_conv, gd_a_log, gd_dt_bias, gd_norm, ml_i_bias, ml_f_bias, ml_norm, od_w_out, moe_w_group, moe_b_group, moe_w_router, moe_b_router, moe_w_gate, moe_w_up, moe_w_down):
    bsz, t, d = x.shape
    n = bsz * t
    depth = norm_mix.shape[0]
    lb_table = jnp.cumsum(jax.nn.softmax(hg_lb_logits.astype(F32), axis=0), axis=0)
    h = x.reshape(n, d)
    for layer in range(depth):
        j = layer // 2
        if layer % 2 == 0:
            p_rw, p_hg = _rms_proj(h, norm_mix[layer],
                                   _pack_weights(ev_w_in[j], [(0, RW_IN), (RW_IN, RW_IN + HG_IN)]))
            ya = _rwkv7(p_rw.reshape(bsz, t, RW_IN), ev_mu[j], rw_w0[j], rw_w2[j], rw_a0[j], rw_a2[j], rw_g2[j],
                        rw_k_k[j], rw_k_a[j], rw_r_k[j], rw_ln_w[j], rw_ln_b[j])
            yb = _hgrn2(p_hg.reshape(bsz, t, HG_IN), lb_table[j], hg_norm[j])
            mix_a, mix_b, w_out = ya.reshape(n, RW_DIM), yb.reshape(n, HG_KDIM), ev_w_out[j]
        else:
            ws = _pack_weights(od_w_in[j], [(0, GD_MAIN), (GD_IN, GD_IN + ML_MAIN)],
                               gate_ranges=[(GD_MAIN, GD_IN), (GD_IN + ML_MAIN, GD_IN + ML_MAIN + 2 * ML_HEADS)])
            p_gd, p_ml, p_gt = _rms_proj(h, norm_mix[layer], ws)
            p_gt = p_gt.reshape(bsz, t, LANES)
            yc = _gdn(p_gd.reshape(bsz, t, GD_MAIN), p_gt, gd_conv[j], gd_a_log[j], gd_dt_bias[j], gd_norm[j])
            yd = _mlstm(p_ml.reshape(bsz, t, ML_MAIN), p_gt, ml_i_bias[j], ml_f_bias[j], ml_norm[j])
            mix_a, mix_b, w_out = yc.reshape(n, GD_DIM), yd.reshape(n, ML_DIM), od_w_out[j]
        h, xn_tiles, rg, cnt, dest = _out_proj_route(h, mix_a, mix_b, w_out, norm_ffn[layer], moe_w_group[layer],
                                                     moe_b_group[layer], moe_w_router[layer], moe_b_router[layer])
        h = _hmoe_residual(h, xn_tiles, rg, cnt, dest, moe_w_gate[layer], moe_w_up[layer], moe_w_down[layer],
                           norm_final, final_norm=(layer == depth - 1))
    return h.reshape(bsz, t, d)
```

```python
import functools
import math

import jax
import jax.numpy as jnp
from jax import lax
from jax.experimental import pallas as pl
from jax.experimental.pallas import tpu as pltpu

F32 = jnp.float32
BF16 = jnp.bfloat16
I32 = jnp.int32
HIGHEST = lax.Precision.HIGHEST

D_MODEL = 1024
NORM_EPS = 1e-6
RW_HEADS, RW_HEAD_DIM = 8, 64
RW_DIM = RW_HEADS * RW_HEAD_DIM
R_DECAY, R_AAA, R_GATE = 64, 64, 128
RW_IN = 3 * RW_DIM + R_DECAY + R_AAA + R_GATE
RW_LN_EPS = 64e-5
HG_HEADS, HG_DIM = 8, 64
HG_KDIM = HG_HEADS * HG_DIM
HG_IN = 4 * HG_KDIM
GD_HEADS, GD_HEAD_DIM = 4, 128
GD_DIM = GD_HEADS * GD_HEAD_DIM
CONV_K = 4
GD_MAIN = 4 * GD_DIM
GD_IN = GD_MAIN + 2 * GD_HEADS
ML_HEADS, ML_HEAD_DIM = 4, 128
ML_DIM = ML_HEADS * ML_HEAD_DIM
ML_MAIN = 4 * ML_DIM
N_GROUPS, EXPERTS_PER_GROUP = 4, 8
N_EXPERTS = N_GROUPS * EXPERTS_PER_GROUP
D_EXPERT = 256

LANES = 128
VMEM_LIMIT_BYTES = 48 * 1024 * 1024

PROJ_TM = 256
PACK_TM = 128
CHUNK = 64
SUB = 16
RW_NCH = 4
GD_NCH = 4
HG_ROWS = 128
ML_CHUNK = 128
MOE_TM = 256
ROUTE_ROWS = 40
EXPERT_BUFS = 3
GATHER_UNROLL = 8
NEG = -1e30


def _cparams(*sem):
    return pltpu.CompilerParams(dimension_semantics=sem, vmem_limit_bytes=VMEM_LIMIT_BYTES)


def _mm(a, b):
    return jnp.dot(a.astype(BF16), b.astype(BF16), preferred_element_type=F32)


def _mm_nt(a, b):
    return lax.dot_general(a.astype(BF16), b.astype(BF16), (((1,), (1,)), ((), ())), preferred_element_type=F32)


def _mm_tn(a, b):
    return lax.dot_general(a.astype(BF16), b.astype(BF16), (((0,), (0,)), ((), ())), preferred_element_type=F32)


def _mm_hi(a, b):
    return jnp.dot(a, b, precision=HIGHEST, preferred_element_type=F32)


def _mm_split(x, ones_bf16):
    hi = x.astype(BF16)
    lo = (x - hi.astype(F32)).astype(BF16)
    return (jnp.dot(hi, ones_bf16, preferred_element_type=F32) + jnp.dot(lo, ones_bf16, preferred_element_type=F32))


def _iota2(shape):
    return lax.broadcasted_iota(I32, shape, 0), lax.broadcasted_iota(I32, shape, 1)


def _rms_rows(x, g, eps=NORM_EPS):
    return x * lax.rsqrt(jnp.mean(x * x, axis=-1, keepdims=True) + eps) * g


def _seg_ones(width, seg):
    r, c = _iota2((width, width))
    return jnp.where((r // seg) == (c // seg), 1.0, 0.0).astype(BF16)


def _split3(x):
    x1 = x.astype(BF16)
    r1 = x - x1.astype(F32)
    x2 = r1.astype(BF16)
    return x1, x2, (r1 - x2.astype(F32)).astype(BF16)


def _cumsum_rows(tri_bf16, x):
    return sum(jnp.dot(tri_bf16, t, preferred_element_type=F32) for t in _split3(x))


def _cumsum_cols(x, tri_bf16):
    return sum(jnp.dot(t, tri_bf16, preferred_element_type=F32) for t in _split3(x))


def _mm3(a, b):
    ah = a.astype(BF16)
    al = (a - ah.astype(F32)).astype(BF16)
    bh = b.astype(BF16)
    bl = (b - bh.astype(F32)).astype(BF16)
    dot = lambda x, y: jnp.dot(x, y, preferred_element_type=F32)
    return dot(ah, bh) + dot(ah, bl) + dot(al, bh)


def _tri_inv_multi(ms, n, chain):
    assert chain // SUB <= 4
    r, c = _iota2((n, n))
    same = (r // SUB) == (c // SUB)
    eye = jnp.where(r == c, 1.0, 0.0).astype(F32)
    ds = [jnp.where(same, m, 0.0) for m in ms]
    offs = [m - d for m, d in zip(ms, ds)]
    xs = [eye - d for d in ds]
    ps = ds
    for _ in range(3):
        ps = [_mm(p, p) for p in ps]
        xs = [x + _mm(x, p) for x, p in zip(xs, ps)]
    es = [_mm(x, o) for x, o in zip(xs, offs)]
    imes = [eye - e for e in es]
    e2s = [_mm(e, e) for e in es]
    ys = [i + _mm(i, e2) for i, e2 in zip(imes, e2s)]
    xs = [_mm(y, x) for y, x in zip(ys, xs)]
    res = [eye - x - _mm3(m, x) for m, x in zip(ms, xs)]
    return [x + _mm(x, rr) for x, rr in zip(xs, res)]


def _pack_weights_kernel(w_ref, *o_refs, ranges, gate_ranges):
    w = w_ref[...]
    for (lo, hi), o_ref in zip(ranges, o_refs):
        o_ref[...] = w[:, lo:hi].astype(BF16)
    if gate_ranges:
        cols = [w[:, lo:hi] for lo, hi in gate_ranges]
        used = sum(hi - lo for lo, hi in gate_ranges)
        cols.append(jnp.zeros((w.shape[0], LANES - used), F32))
        o_refs[-1][...] = jnp.concatenate(cols, axis=1).astype(BF16)


def _pack_weights(w, ranges, gate_ranges=()):
    rows, cols = w.shape
    tm = PACK_TM
    widths = [hi - lo for lo, hi in ranges] + ([LANES] if gate_ranges else [])
    return pl.pallas_call(
        functools.partial(_pack_weights_kernel, ranges=tuple(ranges), gate_ranges=tuple(gate_ranges)),
        grid=(rows // tm,),
        in_specs=[pl.BlockSpec((tm, cols), lambda i: (i, 0))],
        out_specs=[pl.BlockSpec((tm, wd), lambda i: (i, 0)) for wd in widths],
        out_shape=[jax.ShapeDtypeStruct((rows, wd), BF16) for wd in widths],
        compiler_params=_cparams("parallel"),
        name="pack_weights",
    )(w)


def _rms_proj_kernel(x_ref, g_ref, *refs, n_out):
    y = _rms_rows(x_ref[...], g_ref[...]).astype(BF16)
    for w_ref, o_ref in zip(refs[:n_out], refs[n_out:]):
        o_ref[...] = jnp.dot(y, w_ref[...], preferred_element_type=F32)


def _rms_proj(x, g, ws):
    n, d = x.shape
    tm = min(PROJ_TM, n)
    in_specs = [pl.BlockSpec((tm, d), lambda i: (i, 0)), pl.BlockSpec((1, d), lambda i: (0, 0))]
    in_specs += [pl.BlockSpec(w.shape, lambda i: (0, 0)) for w in ws]
    return pl.pallas_call(
        functools.partial(_rms_proj_kernel, n_out=len(ws)),
        grid=(n // tm,),
        in_specs=in_specs,
        out_specs=[pl.BlockSpec((tm, w.shape[1]), lambda i: (i, 0)) for w in ws],
        out_shape=[jax.ShapeDtypeStruct((n, w.shape[1]), F32) for w in ws],
        compiler_params=_cparams("parallel"),
        name="rms_proj",
    )(x, g.reshape(1, d), *ws)


ROW_TILE = D_MODEL // LANES


def _store_row_tiles(ref, x):
    for j in range(ROW_TILE):
        ref[pl.ds(j, x.shape[0], stride=ROW_TILE), :] = x[:, j * LANES:(j + 1) * LANES]


def _load_row_tiles(ref, rows):
    return jnp.concatenate([ref[pl.ds(j, rows, stride=ROW_TILE), :] for j in range(ROW_TILE)], axis=1)


def _rwkv7_kernel(p_ref, mu_ref, w0_ref, w2_ref, a0_ref, a2_ref, g2_ref, kk_ref, ka_ref, rk_ref,
                  lnw_ref, lnb_ref, o_ref, prev_ref, zt_ref):
    L = CHUNK
    npair = RW_HEADS // 2
    c = pl.program_id(1)

    @pl.when(c == 0)
    def _():
        prev_ref[...] = jnp.zeros_like(prev_ref)
        zt_ref[...] = jnp.zeros_like(zt_ref)

    x = p_ref[0]
    R = x.shape[0]
    nch = R // L
    row = lax.broadcasted_iota(I32, x.shape, 0)
    xs = jnp.where(row == 0, prev_ref[7:8, :], pltpu.roll(x, 1, 0))
    prev_ref[...] = x[R - 8:R, :]
    pm = x + mu_ref[...] * (xs - x)
    r_all = pm[:, 0:RW_DIM]
    k_all = pm[:, RW_DIM:2 * RW_DIM]
    v_all = pm[:, 2 * RW_DIM:3 * RW_DIM]
    wa = pm[:, 3 * RW_DIM:3 * RW_DIM + LANES]
    gl = pm[:, 3 * RW_DIM + LANES:]
    wlog = -jax.nn.softplus(-(w0_ref[...] + _mm(jnp.tanh(wa), w2_ref[...]))) - 0.5
    ld = -jnp.exp(wlog)
    a_all = jax.nn.sigmoid(a0_ref[...] + _mm(wa, a2_ref[...]))
    g_all = _mm(jax.nn.sigmoid(gl), g2_ref[...])

    tr, tc = _iota2((R, R))
    chunk_tril = jnp.where((tc <= tr) & ((tr // L) == (tc // L)), 1.0, 0.0).astype(BF16)
    cs_all = _cumsum_rows(chunk_tril, ld)
    seg = _seg_ones(LANES, RW_HEAD_DIM)
    lane = lax.broadcasted_iota(I32, (L, LANES), 1)
    hm = (lane < RW_HEAD_DIM, lane >= RW_HEAD_DIM)
    br, bc = _iota2((2 * L, 2 * L))
    bd = (br // L) == (bc // L)
    bd_strict = bd & (bc < br)
    bd_incl = bd & (bc <= br)
    fold = lambda z: z[0:L] + z[L:2 * L]
    both = lambda z: jnp.concatenate([jnp.where(hm[0], z, 0.0), jnp.where(hm[1], z, 0.0)], axis=0)

    units = []
    for ci in range(nch):
        rs = slice(ci * L, (ci + 1) * L)
        for p in range(npair):
            sl = slice(p * LANES, (p + 1) * LANES)
            r, k, v, a = r_all[rs, sl], k_all[rs, sl], v_all[rs, sl], a_all[rs, sl]
            cs, ldp = cs_all[rs, sl], ld[rs, sl]
            kkr = k * kk_ref[:, sl]
            kk = kkr * lax.rsqrt(_mm_split(kkr * kkr, seg) + 1e-6)
            k2 = k * (1.0 + (a - 1.0) * ka_ref[:, sl])
            b = kk * a
            cs_last = cs[L - 1:L, :]
            e_neg = jnp.exp(-cs)
            e_rem = jnp.exp(cs_last - cs)
            bhat = b * e_neg
            khat = k2 * e_neg
            units.append(dict(p=p, rs=rs, sl=sl, r=r, v=v, k2=k2, rhat=r * jnp.exp(cs), gam_last=jnp.exp(cs_last),
                              btil=b * e_rem, ktil=k2 * e_rem, a2=both(kk * jnp.exp(cs - ldp)), v2=both(v),
                              rhs4=jnp.concatenate([bhat, bhat, khat, khat], axis=0)))
    for q in units:
        lhs = jnp.concatenate([q["a2"], both(q["rhat"])], axis=0)
        q["g"] = _mm_nt(lhs, q["rhs4"])
    tinvs = _tri_inv_multi([jnp.where(bd_strict, q["g"][0:2 * L, 0:2 * L], 0.0) for q in units], 2 * L, L)
    for q, tinv in zip(units, tinvs):
        q["tinv"] = tinv
        q["x2"] = _mm(jnp.where(bd_strict, q["g"][0:2 * L, 2 * L:4 * L], 0.0), q["v2"])
    for q in units:
        uw = _mm(q["tinv"], jnp.concatenate([q["x2"], q["a2"]], axis=1))
        q["u0"] = -fold(uw[:, 0:LANES])
        q["w"] = fold(uw[:, LANES:])
        q["y0"] = fold(_mm(jnp.where(bd_incl, q["g"][2 * L:4 * L, 2 * L:4 * L], 0.0), q["v2"]))
    for q in units:
        rb = jnp.where(bd_incl, q["g"][2 * L:4 * L, 0:2 * L], 0.0)
        ruw = _mm(rb, jnp.concatenate([both(q["u0"]), both(q["w"])], axis=1))
        q["yc"] = q["y0"] + fold(ruw[:, 0:LANES])
        q["ry"] = q["rhat"] - fold(ruw[:, LANES:])
        q["c1"] = _mm_tn(jnp.concatenate([q["u0"], q["v"]], axis=0), jnp.concatenate([q["btil"], q["ktil"]], axis=0))
        q["c2"] = _mm_tn(q["w"], q["btil"])

    hr, hc = _iota2((LANES, LANES))
    head_bd = (hr // RW_HEAD_DIM) == (hc // RW_HEAD_DIM)
    zts = [zt_ref[p] for p in range(npair)]
    for ci in range(nch):
        qs = units[ci * npair:(ci + 1) * npair]
        for q, zt in zip(qs, zts):
            q["y"] = q["yc"] + _mm_nt(q["ry"], zt)
        zts = [zt * q["gam_last"] + jnp.where(head_bd, q["c1"] - _mm(zt, q["c2"]), 0.0) for q, zt in zip(qs, zts)]
    for p in range(npair):
        zt_ref[p] = zts[p]

    for q in units:
        sl, rs, y = q["sl"], q["rs"], q["y"]
        mean = _mm_split(y, seg) * (1.0 / RW_HEAD_DIM)
        yc = y - mean
        var = _mm_split(yc * yc, seg) * (1.0 / RW_HEAD_DIM)
        yn = yc * lax.rsqrt(var + RW_LN_EPS) * lnw_ref[:, sl] + lnb_ref[:, sl]
        bonus = _mm_split(q["r"] * q["k2"] * rk_ref[:, sl], seg) * q["v"]
        o_ref[0, rs, sl] = (yn + bonus) * g_all[rs, sl]


def _rwkv7(p, mu, w0, w2, a0, a2, g2, k_k, k_a, r_k, ln_w, ln_b):
    bsz, t, _ = p.shape
    row = lambda z: z.reshape(1, -1).astype(F32)
    w2p = jnp.concatenate([w2, jnp.zeros_like(w2)], axis=0).astype(BF16)
    a2p = jnp.concatenate([jnp.zeros_like(a2), a2], axis=0).astype(BF16)
    params = [row(mu), row(w0), w2p, row(a0), a2p, g2.astype(BF16), row(k_k), row(k_a), row(r_k), row(ln_w), row(ln_b)]
    full = lambda z: pl.BlockSpec(z.shape, lambda b, c: (0, 0))
    rows = RW_NCH * CHUNK
    return pl.pallas_call(
        _rwkv7_kernel,
        grid=(bsz, t // rows),
        in_specs=[pl.BlockSpec((1, rows, RW_IN), lambda b, c: (b, c, 0))] + [full(z) for z in params],
        out_specs=pl.BlockSpec((1, rows, RW_DIM), lambda b, c: (b, c, 0)),
        out_shape=jax.ShapeDtypeStruct((bsz, t, RW_DIM), F32),
        scratch_shapes=[pltpu.VMEM((8, RW_IN), F32), pltpu.VMEM((RW_HEADS // 2, LANES, LANES), F32)],
        compiler_params=_cparams("arbitrary", "arbitrary"),
        name="rwkv7_mix",
    )(p, *params)


def _hgrn2_kernel(p_ref, lb_ref, ng_ref, o_ref, st_ref):
    L = HG_ROWS
    c = pl.program_id(1)

    @pl.when(c == 0)
    def _():
        st_ref[...] = jnp.zeros_like(st_ref)

    x = p_ref[0]
    lb = lb_ref[...]
    q_all = jax.nn.silu(x[:, 0:HG_KDIM])
    fg = lb + (1.0 - lb) * jax.nn.sigmoid(x[:, HG_KDIM:2 * HG_KDIM])
    k_all = 1.0 - fg
    logf = jnp.log(fg)
    v_all = x[:, 2 * HG_KDIM:3 * HG_KDIM]
    gate = x[:, 3 * HG_KDIM:]
    tr, tc = _iota2((L, L))
    blk_tril = jnp.where((tc <= tr) & ((tr // SUB) == (tc // SUB)), 1.0, 0.0).astype(BF16)
    bc_all = _cumsum_rows(blk_tril, logf)
    seg = _seg_ones(LANES, HG_DIM)
    br, bcc = _iota2((LANES, LANES))
    bd = (br // HG_DIM) == (bcc // HG_DIM)
    half = SUB // 2
    t_lo = lax.broadcasted_iota(I32, (SUB, LANES), 0)
    t_hi = lax.broadcasted_iota(I32, (half, LANES), 0) + half

    nsub = L // SUB
    npair = HG_HEADS // 2
    rows_per_unit = half * SUB + half * half
    sub_of_row = lax.broadcasted_iota(I32, (L, LANES), 0) // SUB

    units = []
    parts = []
    for p in range(npair):
        sl = slice(p * LANES, (p + 1) * LANES)
        for j in range(nsub):
            rs = slice(j * SUB, (j + 1) * SUB)
            q, k, v, bc = q_all[rs, sl], k_all[rs, sl], v_all[rs, sl], bc_all[rs, sl]
            q_hi, bc_hi = q[half:], bc[half:]
            for s in range(SUB):
                if s < half:
                    diff = jnp.where(t_lo >= s, bc - bc[s:s + 1, :], NEG)
                    parts.append(jnp.exp(diff) * q * k[s:s + 1, :])
                else:
                    diff = jnp.where(t_hi >= s, bc_hi - bc[s:s + 1, :], NEG)
                    parts.append(jnp.exp(diff) * q_hi * k[s:s + 1, :])
            bend = bc[SUB - 1:SUB, :]
            units.append(dict(v=v, qt=q * jnp.exp(bc), dec=jnp.exp(bend), kd=k * jnp.exp(bend - bc)))
    score_all = _mm(jnp.concatenate(parts, axis=0), seg)
    for ui, un in enumerate(units):
        score = score_all[ui * rows_per_unit:(ui + 1) * rows_per_unit]
        v = un["v"]
        acc_lo = jnp.zeros((half, LANES), F32)
        acc_hi = jnp.zeros((half, LANES), F32)
        off = 0
        for s in range(SUB):
            vs = v[s:s + 1, :]
            if s < half:
                acc_lo = acc_lo + score[off:off + half] * vs
                acc_hi = acc_hi + score[off + half:off + SUB] * vs
                off += SUB
            else:
                acc_hi = acc_hi + score[off:off + half] * vs
                off += half
        un["intra"] = jnp.concatenate([acc_lo, acc_hi], axis=0)
    for p in range(npair):
        us = units[p * nsub:(p + 1) * nsub]
        v_pair = jnp.concatenate([un["v"] for un in us], axis=0)
        kd_pair = jnp.concatenate([un["kd"] for un in us], axis=0)
        kd_wide = jnp.concatenate([jnp.where(sub_of_row == j, kd_pair, 0.0) for j in range(nsub)], axis=1)
        upd = _mm_tn(v_pair, kd_wide)
        for j, un in enumerate(us):
            un["upd"] = jnp.where(bd, upd[:, j * LANES:(j + 1) * LANES], 0.0)

    for p in range(npair):
        sl = slice(p * LANES, (p + 1) * LANES)
        st = st_ref[p]
        outs = []
        for un in units[p * nsub:(p + 1) * nsub]:
            outs.append(un["intra"] + _mm_nt(un["qt"], st))
            st = st * un["dec"] + un["upd"]
        st_ref[p] = st
        o = jnp.concatenate(outs, axis=0)
        ms = _mm_split(o * o, seg) * (1.0 / HG_DIM)
        o_ref[0, :, sl] = o * lax.rsqrt(ms + NORM_EPS) * ng_ref[:, sl] * jax.nn.silu(gate[:, sl])


def _hgrn2(p, lb, norm_g):
    bsz, t, _ = p.shape
    lb = lb.reshape(1, HG_KDIM).astype(F32)
    ng = jnp.tile(norm_g.astype(F32), HG_HEADS).reshape(1, HG_KDIM)
    return pl.pallas_call(
        _hgrn2_kernel,
        grid=(bsz, t // HG_ROWS),
        in_specs=[pl.BlockSpec((1, HG_ROWS, HG_IN), lambda b, c: (b, c, 0)),
                  pl.BlockSpec((1, HG_KDIM), lambda b, c: (0, 0)), pl.BlockSpec((1, HG_KDIM), lambda b, c: (0, 0))],
        out_specs=pl.BlockSpec((1, HG_ROWS, HG_KDIM), lambda b, c: (b, c, 0)),
        out_shape=jax.ShapeDtypeStruct((bsz, t, HG_KDIM), F32),
        scratch_shapes=[pltpu.VMEM((HG_HEADS // 2, LANES, LANES), F32)],
        compiler_params=_cparams("arbitrary", "arbitrary"),
        name="hgrn2_mix",
    )(p, lb, ng)


def _gdn_kernel(p_ref, pg_ref, cw_ref, gb_ref, nal_ref, nalc_ref, ng_ref, o_ref, prev_ref, s_ref):
    L = CHUNK
    c = pl.program_id(1)

    @pl.when(c == 0)
    def _():
        prev_ref[...] = jnp.zeros_like(prev_ref)
        s_ref[...] = jnp.zeros_like(s_ref)

    x = p_ref[0]
    R = x.shape[0]
    nch = R // L
    xq = x[:, 0:3 * GD_DIM]
    xcat = jnp.concatenate([prev_ref[...], xq], axis=0)
    conv = xq * cw_ref[CONV_K - 1:CONV_K, :]
    for j in range(1, CONV_K):
        conv = conv + xcat[8 - j:8 - j + R, :] * cw_ref[CONV_K - 1 - j:CONV_K - j, :]
    prev_ref[...] = xq[R - 8:R, :]
    qkv = jax.nn.silu(conv)
    z = x[:, 3 * GD_DIM:]

    gt = pg_ref[0] + gb_ref[...]
    beta_all = jax.nn.sigmoid(gt)
    g_all = nal_ref[...] * jax.nn.softplus(gt)
    g_t = nalc_ref[...] * jax.nn.softplus(gt.T)
    tr, tc = _iota2((R, R))
    chunk_tril = jnp.where((tc <= tr) & ((tr // L) == (tc // L)), 1.0, 0.0).astype(BF16)
    gam_all = _cumsum_rows(chunk_tril, g_all)
    ur, uc = _iota2((R, 2 * R))
    triu2 = jnp.where(((ur // L) == (uc // (2 * L))) & ((ur % L) <= (uc % L)), 1.0, 0.0).astype(BF16)
    gam_t2 = _cumsum_cols(g_t, triu2)
    lr, lc = _iota2((L, L))
    incl = lc <= lr
    br, bc = _iota2((2 * L, 2 * L))
    bd = (br // L) == (bc // L)
    bd_strict = bd & (bc < br)
    bd_incl = bd & (bc <= br)
    lane2 = lax.broadcasted_iota(I32, (1, 2 * L), 1)
    zero = jnp.zeros((L, LANES), F32)

    units = []
    for ci in range(nch):
        rs = slice(ci * L, (ci + 1) * L)
        for h in range(GD_HEADS):
            q = qkv[rs, h * LANES:(h + 1) * LANES]
            k = qkv[rs, GD_DIM + h * LANES:GD_DIM + (h + 1) * LANES]
            v = qkv[rs, 2 * GD_DIM + h * LANES:2 * GD_DIM + (h + 1) * LANES]
            q = q * lax.rsqrt(jnp.sum(q * q, axis=-1, keepdims=True) + 1e-6) * (GD_HEAD_DIM ** -0.5)
            k = k * lax.rsqrt(jnp.sum(k * k, axis=-1, keepdims=True) + 1e-6)
            beta = beta_all[rs, h:h + 1]
            gam = gam_all[rs, GD_HEADS + h:GD_HEADS + h + 1]
            gam_row2 = gam_t2[GD_HEADS + h:GD_HEADS + h + 1, ci * 2 * L:(ci + 1) * 2 * L]
            gam_last = gam[L - 1:L, :]
            kb = k * beta
            units.append(dict(h=h, rs=rs, q=q, k=k, kb=kb, vb=v * beta, gam=gam, gam_row2=gam_row2,
                              gam_last=gam_last, kg=kb * jnp.exp(gam), qg=q * jnp.exp(gam),
                              kd=k * jnp.exp(gam_last - gam),
                              decay=jnp.exp(jnp.where(incl, gam - gam_row2[:, 0:L], NEG))))
    pairs = [(units[i], units[i + 1]) for i in range(0, len(units), 2)]
    ms = []
    for h0, h1 in pairs:
        lhs = jnp.concatenate([jnp.concatenate([h0["kb"], zero], axis=1),
                               jnp.concatenate([zero, h1["kb"]], axis=1)], axis=0)
        rhs = jnp.concatenate([jnp.concatenate([h0["k"], zero], axis=1),
                               jnp.concatenate([zero, h1["k"]], axis=1)], axis=0)
        gam_col = jnp.concatenate([h0["gam"], h1["gam"]], axis=0)
        gam_row = jnp.where(lane2 < L, h0["gam_row2"], h1["gam_row2"])
        decay2 = jnp.exp(jnp.where(bd_incl, gam_col - gam_row, NEG))
        ms.append(jnp.where(bd_strict, _mm_nt(lhs, rhs) * decay2, 0.0))
    tinvs = _tri_inv_multi(ms, 2 * L, L)
    for (h0, h1), tinv in zip(pairs, tinvs):
        rhs = jnp.concatenate([jnp.concatenate([h0["vb"], h0["kg"]], axis=1),
                               jnp.concatenate([h1["vb"], h1["kg"]], axis=1)], axis=0)
        uw = _mm(tinv, rhs)
        h0["u"], h0["w"] = uw[0:L, 0:LANES], uw[0:L, LANES:]
        h1["u"], h1["w"] = uw[L:2 * L, 0:LANES], uw[L:2 * L, LANES:]
    for hd in units:
        attn = _mm_nt(hd["q"], hd["k"]) * hd["decay"]
        uw = jnp.concatenate([hd["u"], hd["w"]], axis=1)
        auw = _mm(attn, uw)
        hd["o0"] = auw[:, 0:LANES]
        hd["qs"] = hd["qg"] - auw[:, LANES:]
        cc = _mm_tn(hd["kd"], uw)
        hd["c1"], hd["c2"] = cc[:, 0:LANES], cc[:, LANES:]

    ss = [s_ref[h] for h in range(GD_HEADS)]
    for ci in range(nch):
        hds = units[ci * GD_HEADS:(ci + 1) * GD_HEADS]
        for hd, s in zip(hds, ss):
            hd["o"] = hd["o0"] + _mm(hd["qs"], s)
        ss = [s * jnp.exp(hd["gam_last"]) + hd["c1"] - _mm(hd["c2"], s) for hd, s in zip(hds, ss)]
    for h in range(GD_HEADS):
        s_ref[h] = ss[h]
    for hd in units:
        sl = slice(hd["h"] * LANES, (hd["h"] + 1) * LANES)
        o_ref[0, hd["rs"], sl] = _rms_rows(hd["o"], ng_ref[...]) * jax.nn.silu(z[hd["rs"], sl])


def _gate_row(lo, vals):
    return jnp.zeros((1, LANES), F32).at[0, lo:lo + vals.shape[0]].set(vals.astype(F32))


def _gdn(p, pg, conv_w, a_log, dt_bias, norm_g):
    bsz, t, _ = p.shape
    gbias = _gate_row(GD_HEADS, dt_bias)
    nal = _gate_row(GD_HEADS, -jnp.exp(a_log))
    nal_col = nal.reshape(LANES, 1)
    ng = norm_g.reshape(1, GD_HEAD_DIM).astype(F32)
    full = lambda z: pl.BlockSpec(z.shape, lambda b, c: (0, 0))
    rows = GD_NCH * CHUNK
    return pl.pallas_call(
        _gdn_kernel,
        grid=(bsz, t // rows),
        in_specs=[pl.BlockSpec((1, rows, GD_MAIN), lambda b, c: (b, c, 0)),
                  pl.BlockSpec((1, rows, LANES), lambda b, c: (b, c, 0)),
                  full(conv_w), full(gbias), full(nal), full(nal_col), full(ng)],
        out_specs=pl.BlockSpec((1, rows, GD_DIM), lambda b, c: (b, c, 0)),
        out_shape=jax.ShapeDtypeStruct((bsz, t, GD_DIM), F32),
        scratch_shapes=[pltpu.VMEM((8, 3 * GD_DIM), F32), pltpu.VMEM((GD_HEADS, LANES, LANES), F32)],
        compiler_params=_cparams("arbitrary", "arbitrary"),
        name="gdn_mix",
    )(p, pg, conv_w.astype(F32), gbias, nal, nal_col, ng)


def _mlstm_kernel(p_ref, pg_ref, gb_ref, ng_ref, o_ref, c_ref, n_ref, m_ref):
    L = ML_CHUNK
    ci = pl.program_id(1)

    @pl.when(ci == 0)
    def _():
        c_ref[...] = jnp.zeros_like(c_ref)
        n_ref[...] = jnp.zeros_like(n_ref)
        m_ref[...] = jnp.zeros_like(m_ref)

    x = p_ref[0]
    gt = pg_ref[0] + gb_ref[...]
    logf = jax.nn.log_sigmoid(gt)
    gt_t = gt.T
    logf_t = jax.nn.log_sigmoid(gt_t)
    tr, tc = _iota2((L, L))
    bc_all = _cumsum_rows(jnp.where(tc <= tr, 1.0, 0.0).astype(BF16), logf)
    bc_t = _cumsum_cols(logf_t, jnp.where(tr <= tc, 1.0, 0.0).astype(BF16))
    incl = tc <= tr
    i_lo, f_lo = 2 * GD_HEADS, 2 * GD_HEADS + ML_HEADS

    heads = []
    for h in range(ML_HEADS):
        sl = slice(h * LANES, (h + 1) * LANES)
        q = x[:, sl]
        k = x[:, ML_DIM + h * LANES:ML_DIM + (h + 1) * LANES] * (ML_HEAD_DIM ** -0.5)
        v = x[:, 2 * ML_DIM + h * LANES:2 * ML_DIM + (h + 1) * LANES]
        bc = bc_all[:, f_lo + h:f_lo + h + 1]
        bc_row = bc_t[f_lo + h:f_lo + h + 1, :]
        ig = gt[:, i_lo + h:i_lo + h + 1]
        ig_row = gt_t[i_lo + h:i_lo + h + 1, :]
        m_prev = m_ref[h][0:1, 0:1]
        b_last = bc[L - 1:L, :]
        d_log = jnp.where(incl, bc - bc_row + ig_row, NEG)
        inter_log = bc + m_prev
        m_t = jnp.maximum(inter_log, jnp.max(d_log, axis=-1, keepdims=True))
        upd_log = b_last - bc + ig
        m_new = jnp.maximum(b_last + m_prev, jnp.max(upd_log, axis=0, keepdims=True))
        wk = jnp.exp(upd_log - m_new) * k
        heads.append(dict(sl=sl, q=q, v=v, m_t=m_t, m_new=m_new, wk=wk, inter_w=jnp.exp(inter_log - m_t),
                          dec=jnp.exp(b_last + m_prev - m_new), sc=_mm_nt(q, k) * jnp.exp(d_log - m_t)))
    for hd in heads:
        hd["scv"] = _mm(hd["sc"], hd["v"])
        hd["kv"] = _mm_tn(hd["wk"], hd["v"])
    for h, hd in enumerate(heads):
        cmat = c_ref[h]
        nvec = n_ref[h][0:1, :]
        q, sc, inter_w, dec = hd["q"], hd["sc"], hd["inter_w"], hd["dec"]
        num = inter_w * _mm(q, cmat) + hd["scv"]
        den = inter_w * jnp.sum(q * nvec, axis=-1, keepdims=True) + jnp.sum(sc, axis=-1, keepdims=True)
        hd["hh"] = num / jnp.maximum(jnp.abs(den), jnp.exp(-hd["m_t"]))
        c_ref[h] = dec * cmat + hd["kv"]
        n_ref[h] = jnp.broadcast_to(dec * nvec + jnp.sum(hd["wk"], axis=0, keepdims=True), (8, LANES))
        m_ref[h] = jnp.broadcast_to(hd["m_new"], (8, LANES))
    for h, hd in enumerate(heads):
        sl = hd["sl"]
        og = jax.nn.sigmoid(x[:, 3 * ML_DIM + h * LANES:3 * ML_DIM + (h + 1) * LANES])
        o_ref[0, :, sl] = og * _rms_rows(hd["hh"], ng_ref[:, sl])


def _mlstm(p, pg, i_bias, f_bias, norm_g):
    bsz, t, _ = p.shape
    gbias = _gate_row(2 * GD_HEADS, jnp.concatenate([i_bias, f_bias]))
    ng = norm_g.reshape(1, ML_DIM).astype(F32)
    lc = min(ML_CHUNK, t)
    full = lambda z: pl.BlockSpec(z.shape, lambda b, c: (0, 0))
    return pl.pallas_call(
        _mlstm_kernel,
        grid=(bsz, t // lc),
        in_specs=[pl.BlockSpec((1, lc, ML_MAIN), lambda b, c: (b, c, 0)),
                  pl.BlockSpec((1, lc, LANES), lambda b, c: (b, c, 0)), full(gbias), full(ng)],
        out_specs=pl.BlockSpec((1, lc, ML_DIM), lambda b, c: (b, c, 0)),
        out_shape=jax.ShapeDtypeStruct((bsz, t, ML_DIM), F32),
        scratch_shapes=[pltpu.VMEM((ML_HEADS, LANES, LANES), F32), pltpu.VMEM((ML_HEADS, 8, LANES), F32),
                        pltpu.VMEM((ML_HEADS, 8, LANES), F32)],
        compiler_params=_cparams("arbitrary", "arbitrary"),
        name="mlstm_mix",
    )(p, pg, gbias, ng)


def _route(h, g_ref, wh_ref, wl_ref, b_ref, off_ref):
    tm = h.shape[0]
    xn = _rms_rows(h, g_ref[...])
    xh = xn.astype(BF16)
    xl = (xn - xh.astype(F32)).astype(BF16)
    dot = lambda a, b: jnp.dot(a, b, preferred_element_type=F32)
    logits = dot(xh, wh_ref[...]) + dot(xh, wl_ref[...]) + dot(xl, wh_ref[...]) + b_ref[...]
    lt = logits.T[0:ROUTE_ROWS, :]
    row = lax.broadcasted_iota(I32, lt.shape, 0)
    big = jnp.int32(1 << 20)
    is_grp = (row >= N_EXPERTS) & (row < N_EXPERTS + N_GROUPS)
    lg = jnp.where(is_grp, lt, NEG)
    gmax = jnp.max(lg, axis=0, keepdims=True)
    p_top = 1.0 / jnp.sum(jnp.exp(lg - gmax), axis=0, keepdims=True)
    g_idx = jnp.min(jnp.where(lg == gmax, row, big), axis=0, keepdims=True) - N_EXPERTS
    valid = (row < N_EXPERTS) & ((row // EXPERTS_PER_GROUP) == g_idx)
    v1 = jnp.where(valid, lt, NEG)
    m1 = jnp.max(v1, axis=0, keepdims=True)
    i1 = jnp.min(jnp.where(v1 == m1, row, big), axis=0, keepdims=True)
    v2 = jnp.where(row == i1, NEG, v1)
    m2 = jnp.max(v2, axis=0, keepdims=True)
    i2 = jnp.min(jnp.where(v2 == m2, row, big), axis=0, keepdims=True)
    e21 = jnp.exp(m2 - m1)
    gate1 = p_top / (1.0 + e21)
    gate2 = p_top * e21 / (1.0 + e21)

    sel1 = row == i1
    sel2 = row == i2
    onehot = jnp.where(sel1 | sel2, 1.0, 0.0)
    tr, tc = _iota2((tm, tm))
    earlier = jnp.where(tr < tc, 1.0, 0.0).astype(BF16)
    off = off_ref[0:ROUTE_ROWS, 0:1]
    prefix = jnp.dot(onehot.astype(BF16), earlier, preferred_element_type=F32) + off
    rank1 = jnp.sum(jnp.where(sel1, prefix, 0.0), axis=0, keepdims=True)
    rank2 = jnp.sum(jnp.where(sel2, prefix, 0.0), axis=0, keepdims=True)
    off_ref[0:ROUTE_ROWS, :] = jnp.broadcast_to(off + jnp.sum(onehot, axis=1, keepdims=True), (ROUTE_ROWS, LANES))

    packed = jnp.concatenate([i1.astype(F32), i2.astype(F32), rank1, rank2, gate1, gate2,
                              jnp.zeros((2, tm), F32)], axis=0)
    return xn, packed


def _out_proj_route_kernel(h_ref, ya_ref, yb_ref, wa_ref, wb_ref, g_ref, wh_ref, wl_ref, b_ref,
                           o_ref, xt_ref, rg_ref, cnt_ref, dest_ref, off_ref, rt_ref):
    phase = pl.program_id(0)
    s = pl.program_id(1)
    tm = h_ref.shape[0]

    @pl.when((phase == 0) & (s == 0))
    def _():
        off_ref[...] = jnp.zeros_like(off_ref)

    @pl.when(phase == 0)
    def _():
        out = h_ref[...] + _mm(ya_ref[...], wa_ref[...]) + _mm(yb_ref[...], wb_ref[...])
        o_ref[...] = out
        xn, packed = _route(out, g_ref, wh_ref, wl_ref, b_ref, off_ref)
        _store_row_tiles(xt_ref, xn)
        rt_ref[s] = packed
        wide = jnp.concatenate([packed, jnp.zeros((LANES - 8, tm), F32)], axis=0)
        rg_ref[...] = wide.T
        cnt_ref[...] = off_ref[...]

    @pl.when(phase == 1)
    def _():
        counts = off_ref[...]
        tiles_per = jnp.floor((counts + (MOE_TM - 1)) * (1.0 / MOE_TM))
        er, ec = _iota2((LANES, LANES))
        tile_end = _mm(jnp.where(ec <= er, 1.0, 0.0), tiles_per)
        seg_start = ((tile_end - tiles_per) * MOE_TM)[0:ROUTE_ROWS, 0:1]
        blk = rt_ref[s]
        row = lax.broadcasted_iota(I32, (ROUTE_ROWS, tm), 0).astype(F32)
        pick = lambda e: jnp.sum(jnp.where(row == e, seg_start, 0.0), axis=0, keepdims=True)
        d1 = pick(blk[0:1]) + blk[2:3]
        d2 = pick(blk[1:2]) + blk[3:4]
        dest_ref[...] = jnp.concatenate([d1, d2, jnp.zeros((6, tm), F32)], axis=0).astype(I32)


def _out_proj_route(h, ya, yb, w_out, g_ffn, w_group, b_group, w_router, b_router):
    n, d = h.shape
    da, db = ya.shape[1], yb.shape[1]
    tm = min(PROJ_TM, n)
    wa = w_out[:da].astype(BF16)
    wb = w_out[da:].astype(BF16)
    pad = LANES - N_EXPERTS - N_GROUPS
    w_cat = jnp.concatenate([w_router, w_group, jnp.zeros((d, pad), F32)], axis=1)
    b_cat = jnp.concatenate([b_router, b_group, jnp.zeros((pad,), F32)]).reshape(1, LANES)
    w_hi = w_cat.astype(BF16)
    w_lo = (w_cat - w_hi.astype(F32)).astype(BF16)
    steps = n // tm
    row = lambda p, s: (s * (1 - p) + (steps - 1) * p, 0)
    fixed = lambda p, s: (0, 0)
    return pl.pallas_call(
        _out_proj_route_kernel,
        grid=(2, steps),
        in_specs=[pl.BlockSpec((tm, d), row), pl.BlockSpec((tm, da), row), pl.BlockSpec((tm, db), row),
                  pl.BlockSpec((da, d), fixed), pl.BlockSpec((db, d), fixed), pl.BlockSpec((1, d), fixed),
                  pl.BlockSpec((d, LANES), fixed), pl.BlockSpec((d, LANES), fixed), pl.BlockSpec((1, LANES), fixed)],
        out_specs=[pl.BlockSpec((tm, d), row), pl.BlockSpec((tm * ROW_TILE, LANES), row),
                   pl.BlockSpec((tm, LANES), row), pl.BlockSpec((LANES, LANES), fixed),
                   pl.BlockSpec((8, tm), lambda p, s: (0, s * p))],
        out_shape=[jax.ShapeDtypeStruct((n, d), F32), jax.ShapeDtypeStruct((n * ROW_TILE, LANES), F32),
                   jax.ShapeDtypeStruct((n, LANES), F32), jax.ShapeDtypeStruct((LANES, LANES), F32),
                   jax.ShapeDtypeStruct((8, n), I32)],
        scratch_shapes=[pltpu.VMEM((LANES, LANES), F32), pltpu.VMEM((steps, 8, tm), F32)],
        compiler_params=_cparams("arbitrary", "arbitrary"),
        name="out_proj_route",
    )(h, ya, yb, wa, wb, g_ffn.reshape(1, d), w_hi, w_lo, b_cat)


def _invert_kernel(d1_ref, d2_ref, o_ref):
    def clear(i, carry):
        o_ref[i] = 0
        return carry

    def fill(t, carry):
        o_ref[d1_ref[t]] = t
        o_ref[d2_ref[t]] = t
        return carry

    lax.fori_loop(0, o_ref.shape[0], clear, 0, unroll=GATHER_UNROLL)
    lax.fori_loop(0, d1_ref.shape[0], fill, 0, unroll=GATHER_UNROLL)


def _invert(d1, d2, n_rows):
    grid_spec = pltpu.PrefetchScalarGridSpec(
        num_scalar_prefetch=2, grid=(1,), in_specs=[],
        out_specs=pl.BlockSpec(memory_space=pltpu.SMEM))
    return pl.pallas_call(
        _invert_kernel,
        grid_spec=grid_spec,
        out_shape=jax.ShapeDtypeStruct((n_rows,), I32),
        compiler_params=_cparams("arbitrary"),
        name="moe_invert",
    )(d1, d2)


def _gather_rows(src_hbm, idx_ref, base, dst, sem):
    def body(j, carry):
        r = pl.multiple_of(idx_ref[base + j] * ROW_TILE, ROW_TILE)
        pltpu.make_async_copy(src_hbm.at[pl.ds(r, ROW_TILE), :],
                              dst.at[pl.ds(pl.multiple_of(j * ROW_TILE, ROW_TILE), ROW_TILE), :],
                              sem).start()
        return carry

    lax.fori_loop(0, dst.shape[0] // ROW_TILE, body, 0, unroll=GATHER_UNROLL)


def _wait_rows(src_hbm, dst, sem):
    pltpu.make_async_copy(src_hbm.at[pl.ds(0, dst.shape[0]), :], dst, sem).wait()


def _expert_kernel(te_ref, nt_ref, src_ref, x_hbm, wg_ref, wu_ref, wd_ref, o_ref, xbuf, wgb, wub, wdb, sem):
    tm = xbuf.shape[1] // ROW_TILE
    i = pl.program_id(0)
    n_used = nt_ref[0]
    slot = i % EXPERT_BUFS

    @pl.when(i == 0)
    def _():
        for t in range(EXPERT_BUFS - 1):
            @pl.when(t < n_used)
            def _():
                _gather_rows(x_hbm, src_ref, t * tm, xbuf.at[t], sem.at[t])

    ahead = i + EXPERT_BUFS - 1

    @pl.when(ahead < n_used)
    def _():
        _gather_rows(x_hbm, src_ref, ahead * tm, xbuf.at[ahead % EXPERT_BUFS], sem.at[ahead % EXPERT_BUFS])

    @pl.when((i == 0) | (te_ref[i] != te_ref[jnp.maximum(i - 1, 0)]))
    def _():
        wgb[...] = wg_ref[0].astype(BF16)
        wub[...] = wu_ref[0].astype(BF16)
        wdb[...] = wd_ref[0].astype(BF16)

    @pl.when(i < n_used)
    def _():
        _wait_rows(x_hbm, xbuf.at[slot], sem.at[slot])
        xn = _load_row_tiles(xbuf.at[slot], tm).astype(BF16)
        gate = jnp.dot(xn, wgb[...], preferred_element_type=F32)
        up = jnp.dot(xn, wub[...], preferred_element_type=F32)
        hid = (jax.nn.silu(gate) * up).astype(BF16)
        _store_row_tiles(o_ref, jnp.dot(hid, wdb[...], preferred_element_type=F32))

    @pl.when(i >= n_used)
    def _():
        o_ref[...] = jnp.zeros_like(o_ref)


def _experts(xn_tiles, tile_expert, n_used, src, w_gate, w_up, w_down):
    d = D_MODEL
    n_tiles = tile_expert.shape[0]
    tm = src.shape[0] // n_tiles
    de = w_gate.shape[-1]
    grid_spec = pltpu.PrefetchScalarGridSpec(
        num_scalar_prefetch=3,
        grid=(n_tiles,),
        in_specs=[pl.BlockSpec(memory_space=pl.ANY),
                  pl.BlockSpec((1, d, de), lambda i, te, nt, s: (te[i], 0, 0)),
                  pl.BlockSpec((1, d, de), lambda i, te, nt, s: (te[i], 0, 0)),
                  pl.BlockSpec((1, de, d), lambda i, te, nt, s: (te[i], 0, 0))],
        out_specs=pl.BlockSpec((tm * ROW_TILE, LANES), lambda i, te, nt, s: (i, 0)),
        scratch_shapes=[pltpu.VMEM((EXPERT_BUFS, tm * ROW_TILE, LANES), F32), pltpu.VMEM((d, de), BF16),
                        pltpu.VMEM((d, de), BF16), pltpu.VMEM((de, d), BF16),
                        pltpu.SemaphoreType.DMA((EXPERT_BUFS,))],
    )
    return pl.pallas_call(
        _expert_kernel,
        grid_spec=grid_spec,
        out_shape=jax.ShapeDtypeStruct((n_tiles * tm * ROW_TILE, LANES), F32),
        compiler_params=_cparams("arbitrary"),
        name="moe_experts",
    )(tile_expert, n_used, src, xn_tiles, w_gate, w_up, w_down)


def _combine_kernel(d1_ref, d2_ref, ys_hbm, h_ref, rg_ref, gf_ref, o_ref, y1, y2, sem, *, final_norm):
    tm = h_ref.shape[0]
    i = pl.program_id(0)
    slot = i % 2

    def gather(tile, s):
        _gather_rows(ys_hbm, d1_ref, tile * tm, y1.at[s], sem.at[0, s])
        _gather_rows(ys_hbm, d2_ref, tile * tm, y2.at[s], sem.at[1, s])

    @pl.when(i == 0)
    def _():
        gather(0, 0)

    @pl.when(i + 1 < pl.num_programs(0))
    def _():
        gather(i + 1, 1 - slot)

    _wait_rows(ys_hbm, y1.at[slot], sem.at[0, slot])
    _wait_rows(ys_hbm, y2.at[slot], sem.at[1, slot])
    rg = rg_ref[...]
    out = (h_ref[...] + rg[:, 4:5] * _load_row_tiles(y1.at[slot], tm)
           + rg[:, 5:6] * _load_row_tiles(y2.at[slot], tm))
    if final_norm:
        out = _rms_rows(out, gf_ref[...])
    o_ref[...] = out


def _combine(h, ys, d1, d2, rg, g_final, final_norm):
    n, d = h.shape
    tm = min(PROJ_TM, n)
    grid_spec = pltpu.PrefetchScalarGridSpec(
        num_scalar_prefetch=2,
        grid=(n // tm,),
        in_specs=[pl.BlockSpec(memory_space=pl.ANY),
                  pl.BlockSpec((tm, d), lambda i, a, b: (i, 0)),
                  pl.BlockSpec((tm, LANES), lambda i, a, b: (i, 0)),
                  pl.BlockSpec((1, d), lambda i, a, b: (0, 0))],
        out_specs=pl.BlockSpec((tm, d), lambda i, a, b: (i, 0)),
        scratch_shapes=[pltpu.VMEM((2, tm * ROW_TILE, LANES), F32), pltpu.VMEM((2, tm * ROW_TILE, LANES), F32),
                        pltpu.SemaphoreType.DMA((2, 2))],
    )
    return pl.pallas_call(
        functools.partial(_combine_kernel, final_norm=final_norm),
        grid_spec=grid_spec,
        out_shape=jax.ShapeDtypeStruct((n, d), F32),
        compiler_params=_cparams("arbitrary"),
        name="moe_combine",
    )(d1, d2, ys, h, rg, g_final.reshape(1, d))


def _hmoe_residual(h, xn_tiles, rg, cnt, dest, layer, w_gate, w_up, w_down, g_final, final_norm):
    n, d = h.shape
    stack = lambda w: w.reshape((-1,) + w.shape[2:])

    counts = cnt[:N_EXPERTS, 0].astype(I32)
    n_tiles = (2 * n) // MOE_TM + N_EXPERTS
    tile_end = jnp.cumsum((counts + MOE_TM - 1) // MOE_TM)
    d1, d2 = dest[0], dest[1]
    tile_ids = jnp.arange(n_tiles, dtype=I32)
    tile_expert = jnp.minimum(jnp.sum((tile_end[None, :] <= tile_ids[:, None]).astype(I32), axis=1), N_EXPERTS - 1)
    tile_expert = tile_expert + layer * N_EXPERTS
    src = _invert(d1, d2, n_tiles * MOE_TM)

    n_used = tile_end[N_EXPERTS - 1:].astype(I32)
    ys = _experts(xn_tiles, tile_expert, n_used, src, stack(w_gate), stack(w_up), stack(w_down))
    return _combine(h, ys, d1, d2, rg, g_final, final_norm)


def kernel(x, norm_mix, norm_ffn, norm_final, ev_w_in, ev_mu, rw_w0, rw_w2, rw_a0, rw_a2, rw_g2, rw_k_k, rw_k_a, rw_r_k, rw_ln_w, rw_ln_b, hg_lb_logits, hg_norm, ev_w_out, od_w_in, gd_conv, gd_a_log, gd_dt_bias, gd_norm, ml_i_bias, ml_f_bias, ml_norm, od_w_out, moe_w_group, moe_b_group, moe_w_router, moe_b_router, moe_w_gate, moe_w_up, moe_w_down):
    bsz, t, d = x.shape
    n = bsz * t
    depth = norm_mix.shape[0]
    lb_table = jnp.cumsum(jax.nn.softmax(hg_lb_logits.astype(F32), axis=0), axis=0)
    h = x.reshape(n, d)
    for layer in range(depth):
        j = layer // 2
        if layer % 2 == 0:
            p_rw, p_hg = _rms_proj(h, norm_mix[layer],
                                   _pack_weights(ev_w_in[j], [(0, RW_IN), (RW_IN, RW_IN + HG_IN)]))
            ya = _rwkv7(p_rw.reshape(bsz, t, RW_IN), ev_mu[j], rw_w0[j], rw_w2[j], rw_a0[j], rw_a2[j], rw_g2[j],
                        rw_k_k[j], rw_k_a[j], rw_r_k[j], rw_ln_w[j], rw_ln_b[j])
            yb = _hgrn2(p_hg.reshape(bsz, t, HG_IN), lb_table[j], hg_norm[j])
            mix_a, mix_b, w_out = ya.reshape(n, RW_DIM), yb.reshape(n, HG_KDIM), ev_w_out[j]
        else:
            ws = _pack_weights(od_w_in[j], [(0, GD_MAIN), (GD_IN, GD_IN + ML_MAIN)],
                               gate_ranges=[(GD_MAIN, GD_IN), (GD_IN + ML_MAIN, GD_IN + ML_MAIN + 2 * ML_HEADS)])
            p_gd, p_ml, p_gt = _rms_proj(h, norm_mix[layer], ws)
            p_gt = p_gt.reshape(bsz, t, LANES)
            yc = _gdn(p_gd.reshape(bsz, t, GD_MAIN), p_gt, gd_conv[j], gd_a_log[j], gd_dt_bias[j], gd_norm[j])
            yd = _mlstm(p_ml.reshape(bsz, t, ML_MAIN), p_gt, ml_i_bias[j], ml_f_bias[j], ml_norm[j])
            mix_a, mix_b, w_out = yc.reshape(n, GD_DIM), yd.reshape(n, ML_DIM), od_w_out[j]
        h, xn_tiles, rg, cnt, dest = _out_proj_route(h, mix_a, mix_b, w_out, norm_ffn[layer], moe_w_group[layer],
                                                     moe_b_group[layer], moe_w_router[layer], moe_b_router[layer])
        h = _hmoe_residual(h, xn_tiles, rg, cnt, dest, layer, moe_w_gate, moe_w_up, moe_w_down,
                           norm_final, final_norm=(layer == depth - 1))
    return h.reshape(bsz, t, d)
```

```python
import functools
import math

import jax
import jax.numpy as jnp
from jax import lax
from jax.experimental import pallas as pl
from jax.experimental.pallas import tpu as pltpu

F32 = jnp.float32
BF16 = jnp.bfloat16
I32 = jnp.int32
HIGHEST = lax.Precision.HIGHEST

D_MODEL = 1024
NORM_EPS = 1e-6
RW_HEADS, RW_HEAD_DIM = 8, 64
RW_DIM = RW_HEADS * RW_HEAD_DIM
R_DECAY, R_AAA, R_GATE = 64, 64, 128
RW_IN = 3 * RW_DIM + R_DECAY + R_AAA + R_GATE
RW_LN_EPS = 64e-5
HG_HEADS, HG_DIM = 8, 64
HG_KDIM = HG_HEADS * HG_DIM
HG_IN = 4 * HG_KDIM
GD_HEADS, GD_HEAD_DIM = 4, 128
GD_DIM = GD_HEADS * GD_HEAD_DIM
CONV_K = 4
GD_MAIN = 4 * GD_DIM
GD_IN = GD_MAIN + 2 * GD_HEADS
ML_HEADS, ML_HEAD_DIM = 4, 128
ML_DIM = ML_HEADS * ML_HEAD_DIM
ML_MAIN = 4 * ML_DIM
N_GROUPS, EXPERTS_PER_GROUP = 4, 8
N_EXPERTS = N_GROUPS * EXPERTS_PER_GROUP
D_EXPERT = 256

LANES = 128
VMEM_LIMIT_BYTES = 48 * 1024 * 1024

PROJ_TM = 256
PACK_TM = 128
CHUNK = 64
SUB = 16
RW_NCH = 4
GD_NCH = 4
HG_ROWS = 128
ML_CHUNK = 128
MOE_TM = 256
ROUTE_ROWS = 40
EXPERT_BUFS = 3
GATHER_UNROLL = 8
NEG = -1e30


def _cparams(*sem):
    return pltpu.CompilerParams(dimension_semantics=sem, vmem_limit_bytes=VMEM_LIMIT_BYTES)


def _mm(a, b):
    return jnp.dot(a.astype(BF16), b.astype(BF16), preferred_element_type=F32)


def _mm_nt(a, b):
    return lax.dot_general(a.astype(BF16), b.astype(BF16), (((1,), (1,)), ((), ())), preferred_element_type=F32)


def _mm_tn(a, b):
    return lax.dot_general(a.astype(BF16), b.astype(BF16), (((0,), (0,)), ((), ())), preferred_element_type=F32)


def _mm_hi(a, b):
    return jnp.dot(a, b, precision=HIGHEST, preferred_element_type=F32)


def _mm_split(x, ones_bf16):
    hi = x.astype(BF16)
    lo = (x - hi.astype(F32)).astype(BF16)
    return (jnp.dot(hi, ones_bf16, preferred_element_type=F32) + jnp.dot(lo, ones_bf16, preferred_element_type=F32))


def _iota2(shape):
    return lax.broadcasted_iota(I32, shape, 0), lax.broadcasted_iota(I32, shape, 1)


def _rms_rows(x, g, eps=NORM_EPS):
    return x * lax.rsqrt(jnp.mean(x * x, axis=-1, keepdims=True) + eps) * g


def _seg_ones(width, seg):
    r, c = _iota2((width, width))
    return jnp.where((r // seg) == (c // seg), 1.0, 0.0).astype(BF16)


def _split3(x):
    x1 = x.astype(BF16)
    r1 = x - x1.astype(F32)
    x2 = r1.astype(BF16)
    return x1, x2, (r1 - x2.astype(F32)).astype(BF16)


def _cumsum_rows(tri_bf16, x):
    return sum(jnp.dot(tri_bf16, t, preferred_element_type=F32) for t in _split3(x))


def _cumsum_cols(x, tri_bf16):
    return sum(jnp.dot(t, tri_bf16, preferred_element_type=F32) for t in _split3(x))


def _mm3(a, b):
    ah = a.astype(BF16)
    al = (a - ah.astype(F32)).astype(BF16)
    bh = b.astype(BF16)
    bl = (b - bh.astype(F32)).astype(BF16)
    dot = lambda x, y: jnp.dot(x, y, preferred_element_type=F32)
    return dot(ah, bh) + dot(ah, bl) + dot(al, bh)


def _tri_inv_multi(ms, n, chain):
    assert chain // SUB <= 4
    r, c = _iota2((n, n))
    same = (r // SUB) == (c // SUB)
    eye = jnp.where(r == c, 1.0, 0.0).astype(F32)
    ds = [jnp.where(same, m, 0.0) for m in ms]
    offs = [m - d for m, d in zip(ms, ds)]
    xs = [eye - d for d in ds]
    ps = ds
    for _ in range(3):
        ps = [_mm(p, p) for p in ps]
        xs = [x + _mm(x, p) for x, p in zip(xs, ps)]
    es = [_mm(x, o) for x, o in zip(xs, offs)]
    imes = [eye - e for e in es]
    e2s = [_mm(e, e) for e in es]
    ys = [i + _mm(i, e2) for i, e2 in zip(imes, e2s)]
    xs = [_mm(y, x) for y, x in zip(ys, xs)]
    res = [eye - x - _mm3(m, x) for m, x in zip(ms, xs)]
    return [x + _mm(x, rr) for x, rr in zip(xs, res)]


def _pack_weights_kernel(w_ref, *o_refs, ranges, gate_ranges):
    w = w_ref[...]
    for (lo, hi), o_ref in zip(ranges, o_refs):
        o_ref[...] = w[:, lo:hi].astype(BF16)
    if gate_ranges:
        cols = [w[:, lo:hi] for lo, hi in gate_ranges]
        used = sum(hi - lo for lo, hi in gate_ranges)
        cols.append(jnp.zeros((w.shape[0], LANES - used), F32))
        o_refs[-1][...] = jnp.concatenate(cols, axis=1).astype(BF16)


def _pack_weights(w, ranges, gate_ranges=()):
    rows, cols = w.shape
    tm = PACK_TM
    widths = [hi - lo for lo, hi in ranges] + ([LANES] if gate_ranges else [])
    return pl.pallas_call(
        functools.partial(_pack_weights_kernel, ranges=tuple(ranges), gate_ranges=tuple(gate_ranges)),
        grid=(rows // tm,),
        in_specs=[pl.BlockSpec((tm, cols), lambda i: (i, 0))],
        out_specs=[pl.BlockSpec((tm, wd), lambda i: (i, 0)) for wd in widths],
        out_shape=[jax.ShapeDtypeStruct((rows, wd), BF16) for wd in widths],
        compiler_params=_cparams("parallel"),
        name="pack_weights",
    )(w)


def _rms_proj_kernel(x_ref, g_ref, *refs, n_out):
    y = _rms_rows(x_ref[...], g_ref[...]).astype(BF16)
    for w_ref, o_ref in zip(refs[:n_out], refs[n_out:]):
        o_ref[...] = jnp.dot(y, w_ref[...], preferred_element_type=F32)


def _rms_proj(x, g, ws):
    n, d = x.shape
    tm = min(PROJ_TM, n)
    in_specs = [pl.BlockSpec((tm, d), lambda i: (i, 0)), pl.BlockSpec((1, d), lambda i: (0, 0))]
    in_specs += [pl.BlockSpec(w.shape, lambda i: (0, 0)) for w in ws]
    return pl.pallas_call(
        functools.partial(_rms_proj_kernel, n_out=len(ws)),
        grid=(n // tm,),
        in_specs=in_specs,
        out_specs=[pl.BlockSpec((tm, w.shape[1]), lambda i: (i, 0)) for w in ws],
        out_shape=[jax.ShapeDtypeStruct((n, w.shape[1]), F32) for w in ws],
        compiler_params=_cparams("parallel"),
        name="rms_proj",
    )(x, g.reshape(1, d), *ws)


ROW_TILE = D_MODEL // LANES


def _store_row_tiles(ref, x):
    for j in range(ROW_TILE):
        ref[pl.ds(j, x.shape[0], stride=ROW_TILE), :] = x[:, j * LANES:(j + 1) * LANES]


def _load_row_tiles(ref, rows):
    return jnp.concatenate([ref[pl.ds(j, rows, stride=ROW_TILE), :] for j in range(ROW_TILE)], axis=1)


def _rwkv7_kernel(p_ref, mu_ref, w0_ref, w2_ref, a0_ref, a2_ref, g2_ref, kk_ref, ka_ref, rk_ref,
                  lnw_ref, lnb_ref, o_ref, prev_ref, zt_ref):
    L = CHUNK
    npair = RW_HEADS // 2
    c = pl.program_id(1)

    @pl.when(c == 0)
    def _():
        prev_ref[...] = jnp.zeros_like(prev_ref)
        zt_ref[...] = jnp.zeros_like(zt_ref)

    x = p_ref[0]
    R = x.shape[0]
    nch = R // L
    row = lax.broadcasted_iota(I32, x.shape, 0)
    xs = jnp.where(row == 0, prev_ref[7:8, :], pltpu.roll(x, 1, 0))
    prev_ref[...] = x[R - 8:R, :]
    pm = x + mu_ref[...] * (xs - x)
    r_all = pm[:, 0:RW_DIM]
    k_all = pm[:, RW_DIM:2 * RW_DIM]
    v_all = pm[:, 2 * RW_DIM:3 * RW_DIM]
    wa = pm[:, 3 * RW_DIM:3 * RW_DIM + LANES]
    gl = pm[:, 3 * RW_DIM + LANES:]
    wlog = -jax.nn.softplus(-(w0_ref[...] + _mm(jnp.tanh(wa), w2_ref[...]))) - 0.5
    ld = -jnp.exp(wlog)
    a_all = jax.nn.sigmoid(a0_ref[...] + _mm(wa, a2_ref[...]))
    g_all = _mm(jax.nn.sigmoid(gl), g2_ref[...])

    tr, tc = _iota2((R, R))
    chunk_tril = jnp.where((tc <= tr) & ((tr // L) == (tc // L)), 1.0, 0.0).astype(BF16)
    cs_all = _cumsum_rows(chunk_tril, ld)
    seg = _seg_ones(LANES, RW_HEAD_DIM)
    lane = lax.broadcasted_iota(I32, (L, LANES), 1)
    hm = (lane < RW_HEAD_DIM, lane >= RW_HEAD_DIM)
    br, bc = _iota2((2 * L, 2 * L))
    bd = (br // L) == (bc // L)
    bd_strict = bd & (bc < br)
    bd_incl = bd & (bc <= br)
    fold = lambda z: z[0:L] + z[L:2 * L]
    both = lambda z: jnp.concatenate([jnp.where(hm[0], z, 0.0), jnp.where(hm[1], z, 0.0)], axis=0)

    units = []
    for ci in range(nch):
        rs = slice(ci * L, (ci + 1) * L)
        for p in range(npair):
            sl = slice(p * LANES, (p + 1) * LANES)
            r, k, v, a = r_all[rs, sl], k_all[rs, sl], v_all[rs, sl], a_all[rs, sl]
            cs, ldp = cs_all[rs, sl], ld[rs, sl]
            kkr = k * kk_ref[:, sl]
            kk = kkr * lax.rsqrt(_mm_split(kkr * kkr, seg) + 1e-6)
            k2 = k * (1.0 + (a - 1.0) * ka_ref[:, sl])
            b = kk * a
            cs_last = cs[L - 1:L, :]
            e_neg = jnp.exp(-cs)
            e_rem = jnp.exp(cs_last - cs)
            bhat = b * e_neg
            khat = k2 * e_neg
            units.append(dict(p=p, rs=rs, sl=sl, r=r, v=v, k2=k2, rhat=r * jnp.exp(cs), gam_last=jnp.exp(cs_last),
                              btil=b * e_rem, ktil=k2 * e_rem, a2=both(kk * jnp.exp(cs - ldp)), v2=both(v),
                              rhs4=jnp.concatenate([bhat, bhat, khat, khat], axis=0)))
    for q in units:
        lhs = jnp.concatenate([q["a2"], both(q["rhat"])], axis=0)
        q["g"] = _mm_nt(lhs, q["rhs4"])
    tinvs = _tri_inv_multi([jnp.where(bd_strict, q["g"][0:2 * L, 0:2 * L], 0.0) for q in units], 2 * L, L)
    for q, tinv in zip(units, tinvs):
        q["tinv"] = tinv
        q["x2"] = _mm(jnp.where(bd_strict, q["g"][0:2 * L, 2 * L:4 * L], 0.0), q["v2"])
    for q in units:
        uw = _mm(q["tinv"], jnp.concatenate([q["x2"], q["a2"]], axis=1))
        q["u0"] = -fold(uw[:, 0:LANES])
        q["w"] = fold(uw[:, LANES:])
        q["y0"] = fold(_mm(jnp.where(bd_incl, q["g"][2 * L:4 * L, 2 * L:4 * L], 0.0), q["v2"]))
    for q in units:
        rb = jnp.where(bd_incl, q["g"][2 * L:4 * L, 0:2 * L], 0.0)
        ruw = _mm(rb, jnp.concatenate([both(q["u0"]), both(q["w"])], axis=1))
        q["yc"] = q["y0"] + fold(ruw[:, 0:LANES])
        q["ry"] = q["rhat"] - fold(ruw[:, LANES:])
        q["c1"] = _mm_tn(jnp.concatenate([q["u0"], q["v"]], axis=0), jnp.concatenate([q["btil"], q["ktil"]], axis=0))
        q["c2"] = _mm_tn(q["w"], q["btil"])

    hr, hc = _iota2((LANES, LANES))
    head_bd = (hr // RW_HEAD_DIM) == (hc // RW_HEAD_DIM)
    zts = [zt_ref[p] for p in range(npair)]
    for ci in range(nch):
        qs = units[ci * npair:(ci + 1) * npair]
        for q, zt in zip(qs, zts):
            q["y"] = q["yc"] + _mm_nt(q["ry"], zt)
        zts = [zt * q["gam_last"] + jnp.where(head_bd, q["c1"] - _mm(zt, q["c2"]), 0.0) for q, zt in zip(qs, zts)]
    for p in range(npair):
        zt_ref[p] = zts[p]

    for q in units:
        sl, rs, y = q["sl"], q["rs"], q["y"]
        mean = _mm_split(y, seg) * (1.0 / RW_HEAD_DIM)
        yc = y - mean
        var = _mm_split(yc * yc, seg) * (1.0 / RW_HEAD_DIM)
        yn = yc * lax.rsqrt(var + RW_LN_EPS) * lnw_ref[:, sl] + lnb_ref[:, sl]
        bonus = _mm_split(q["r"] * q["k2"] * rk_ref[:, sl], seg) * q["v"]
        o_ref[0, rs, sl] = (yn + bonus) * g_all[rs, sl]


def _rwkv7(p, mu, w0, w2, a0, a2, g2, k_k, k_a, r_k, ln_w, ln_b):
    bsz, t, _ = p.shape
    row = lambda z: z.reshape(1, -1).astype(F32)
    w2p = jnp.concatenate([w2, jnp.zeros_like(w2)], axis=0).astype(BF16)
    a2p = jnp.concatenate([jnp.zeros_like(a2), a2], axis=0).astype(BF16)
    params = [row(mu), row(w0), w2p, row(a0), a2p, g2.astype(BF16), row(k_k), row(k_a), row(r_k), row(ln_w), row(ln_b)]
    full = lambda z: pl.BlockSpec(z.shape, lambda b, c: (0, 0))
    rows = RW_NCH * CHUNK
    return pl.pallas_call(
        _rwkv7_kernel,
        grid=(bsz, t // rows),
        in_specs=[pl.BlockSpec((1, rows, RW_IN), lambda b, c: (b, c, 0))] + [full(z) for z in params],
        out_specs=pl.BlockSpec((1, rows, RW_DIM), lambda b, c: (b, c, 0)),
        out_shape=jax.ShapeDtypeStruct((bsz, t, RW_DIM), F32),
        scratch_shapes=[pltpu.VMEM((8, RW_IN), F32), pltpu.VMEM((RW_HEADS // 2, LANES, LANES), F32)],
        compiler_params=_cparams("arbitrary", "arbitrary"),
        name="rwkv7_mix",
    )(p, *params)


def _hgrn2_kernel(p_ref, lb_ref, ng_ref, o_ref, st_ref):
    L = HG_ROWS
    c = pl.program_id(1)

    @pl.when(c == 0)
    def _():
        st_ref[...] = jnp.zeros_like(st_ref)

    x = p_ref[0]
    lb = lb_ref[...]
    q_all = jax.nn.silu(x[:, 0:HG_KDIM])
    fg = lb + (1.0 - lb) * jax.nn.sigmoid(x[:, HG_KDIM:2 * HG_KDIM])
    k_all = 1.0 - fg
    logf = jnp.log(fg)
    v_all = x[:, 2 * HG_KDIM:3 * HG_KDIM]
    gate = x[:, 3 * HG_KDIM:]
    tr, tc = _iota2((L, L))
    blk_tril = jnp.where((tc <= tr) & ((tr // SUB) == (tc // SUB)), 1.0, 0.0).astype(BF16)
    bc_all = _cumsum_rows(blk_tril, logf)
    seg = _seg_ones(LANES, HG_DIM)
    br, bcc = _iota2((LANES, LANES))
    bd = (br // HG_DIM) == (bcc // HG_DIM)
    half = SUB // 2
    t_lo = lax.broadcasted_iota(I32, (SUB, LANES), 0)
    t_hi = lax.broadcasted_iota(I32, (half, LANES), 0) + half

    nsub = L // SUB
    npair = HG_HEADS // 2
    rows_per_unit = half * SUB + half * half
    sub_of_row = lax.broadcasted_iota(I32, (L, LANES), 0) // SUB

    units = []
    parts = []
    for p in range(npair):
        sl = slice(p * LANES, (p + 1) * LANES)
        for j in range(nsub):
            rs = slice(j * SUB, (j + 1) * SUB)
            q, k, v, bc = q_all[rs, sl], k_all[rs, sl], v_all[rs, sl], bc_all[rs, sl]
            q_hi, bc_hi = q[half:], bc[half:]
            for s in range(SUB):
                if s < half:
                    diff = jnp.where(t_lo >= s, bc - bc[s:s + 1, :], NEG)
                    parts.append(jnp.exp(diff) * q * k[s:s + 1, :])
                else:
                    diff = jnp.where(t_hi >= s, bc_hi - bc[s:s + 1, :], NEG)
                    parts.append(jnp.exp(diff) * q_hi * k[s:s + 1, :])
            bend = bc[SUB - 1:SUB, :]
            units.append(dict(v=v, qt=q * jnp.exp(bc), dec=jnp.exp(bend), kd=k * jnp.exp(bend - bc)))
    score_all = _mm(jnp.concatenate(parts, axis=0), seg)
    for ui, un in enumerate(units):
        score = score_all[ui * rows_per_unit:(ui + 1) * rows_per_unit]
        v = un["v"]
        acc_lo = jnp.zeros((half, LANES), F32)
        acc_hi = jnp.zeros((half, LANES), F32)
        off = 0
        for s in range(SUB):
            vs = v[s:s + 1, :]
            if s < half:
                acc_lo = acc_lo + score[off:off + half] * vs
                acc_hi = acc_hi + score[off + half:off + SUB] * vs
                off += SUB
            else:
                acc_hi = acc_hi + score[off:off + half] * vs
                off += half
        un["intra"] = jnp.concatenate([acc_lo, acc_hi], axis=0)
    for p in range(npair):
        us = units[p * nsub:(p + 1) * nsub]
        v_pair = jnp.concatenate([un["v"] for un in us], axis=0)
        kd_pair = jnp.concatenate([un["kd"] for un in us], axis=0)
        kd_wide = jnp.concatenate([jnp.where(sub_of_row == j, kd_pair, 0.0) for j in range(nsub)], axis=1)
        upd = _mm_tn(v_pair, kd_wide)
        for j, un in enumerate(us):
            un["upd"] = jnp.where(bd, upd[:, j * LANES:(j + 1) * LANES], 0.0)

    for p in range(npair):
        sl = slice(p * LANES, (p + 1) * LANES)
        st = st_ref[p]
        outs = []
        for un in units[p * nsub:(p + 1) * nsub]:
            outs.append(un["intra"] + _mm_nt(un["qt"], st))
            st = st * un["dec"] + un["upd"]
        st_ref[p] = st
        o = jnp.concatenate(outs, axis=0)
        ms = _mm_split(o * o, seg) * (1.0 / HG_DIM)
        o_ref[0, :, sl] = o * lax.rsqrt(ms + NORM_EPS) * ng_ref[:, sl] * jax.nn.silu(gate[:, sl])


def _hgrn2(p, lb, norm_g):
    bsz, t, _ = p.shape
    lb = lb.reshape(1, HG_KDIM).astype(F32)
    ng = jnp.tile(norm_g.astype(F32), HG_HEADS).reshape(1, HG_KDIM)
    return pl.pallas_call(
        _hgrn2_kernel,
        grid=(bsz, t // HG_ROWS),
        in_specs=[pl.BlockSpec((1, HG_ROWS, HG_IN), lambda b, c: (b, c, 0)),
                  pl.BlockSpec((1, HG_KDIM), lambda b, c: (0, 0)), pl.BlockSpec((1, HG_KDIM), lambda b, c: (0, 0))],
        out_specs=pl.BlockSpec((1, HG_ROWS, HG_KDIM), lambda b, c: (b, c, 0)),
        out_shape=jax.ShapeDtypeStruct((bsz, t, HG_KDIM), F32),
        scratch_shapes=[pltpu.VMEM((HG_HEADS // 2, LANES, LANES), F32)],
        compiler_params=_cparams("arbitrary", "arbitrary"),
        name="hgrn2_mix",
    )(p, lb, ng)


def _gdn_kernel(p_ref, pg_ref, cw_ref, gb_ref, nal_ref, nalc_ref, ng_ref, o_ref, prev_ref, s_ref):
    L = CHUNK
    c = pl.program_id(1)

    @pl.when(c == 0)
    def _():
        prev_ref[...] = jnp.zeros_like(prev_ref)
        s_ref[...] = jnp.zeros_like(s_ref)

    x = p_ref[0]
    R = x.shape[0]
    nch = R // L
    xq = x[:, 0:3 * GD_DIM]
    xcat = jnp.concatenate([prev_ref[...], xq], axis=0)
    conv = xq * cw_ref[CONV_K - 1:CONV_K, :]
    for j in range(1, CONV_K):
        conv = conv + xcat[8 - j:8 - j + R, :] * cw_ref[CONV_K - 1 - j:CONV_K - j, :]
    prev_ref[...] = xq[R - 8:R, :]
    qkv = jax.nn.silu(conv)
    z = x[:, 3 * GD_DIM:]

    gt = pg_ref[0] + gb_ref[...]
    beta_all = jax.nn.sigmoid(gt)
    g_all = nal_ref[...] * jax.nn.softplus(gt)
    g_t = nalc_ref[...] * jax.nn.softplus(gt.T)
    tr, tc = _iota2((R, R))
    chunk_tril = jnp.where((tc <= tr) & ((tr // L) == (tc // L)), 1.0, 0.0).astype(BF16)
    gam_all = _cumsum_rows(chunk_tril, g_all)
    ur, uc = _iota2((R, 2 * R))
    triu2 = jnp.where(((ur // L) == (uc // (2 * L))) & ((ur % L) <= (uc % L)), 1.0, 0.0).astype(BF16)
    gam_t2 = _cumsum_cols(g_t, triu2)
    lr, lc = _iota2((L, L))
    incl = lc <= lr
    br, bc = _iota2((2 * L, 2 * L))
    bd = (br // L) == (bc // L)
    bd_strict = bd & (bc < br)
    bd_incl = bd & (bc <= br)
    lane2 = lax.broadcasted_iota(I32, (1, 2 * L), 1)
    zero = jnp.zeros((L, LANES), F32)

    units = []
    for ci in range(nch):
        rs = slice(ci * L, (ci + 1) * L)
        for h in range(GD_HEADS):
            q = qkv[rs, h * LANES:(h + 1) * LANES]
            k = qkv[rs, GD_DIM + h * LANES:GD_DIM + (h + 1) * LANES]
            v = qkv[rs, 2 * GD_DIM + h * LANES:2 * GD_DIM + (h + 1) * LANES]
            q = q * lax.rsqrt(jnp.sum(q * q, axis=-1, keepdims=True) + 1e-6) * (GD_HEAD_DIM ** -0.5)
            k = k * lax.rsqrt(jnp.sum(k * k, axis=-1, keepdims=True) + 1e-6)
            beta = beta_all[rs, h:h + 1]
            gam = gam_all[rs, GD_HEADS + h:GD_HEADS + h + 1]
            gam_row2 = gam_t2[GD_HEADS + h:GD_HEADS + h + 1, ci * 2 * L:(ci + 1) * 2 * L]
            gam_last = gam[L - 1:L, :]
            kb = k * beta
            units.append(dict(h=h, rs=rs, q=q, k=k, kb=kb, vb=v * beta, gam=gam, gam_row2=gam_row2,
                              gam_last=gam_last, kg=kb * jnp.exp(gam), qg=q * jnp.exp(gam),
                              kd=k * jnp.exp(gam_last - gam),
                              decay=jnp.exp(jnp.where(incl, gam - gam_row2[:, 0:L], NEG))))
    pairs = [(units[i], units[i + 1]) for i in range(0, len(units), 2)]
    ms = []
    for h0, h1 in pairs:
        lhs = jnp.concatenate([jnp.concatenate([h0["kb"], zero], axis=1),
                               jnp.concatenate([zero, h1["kb"]], axis=1)], axis=0)
        rhs = jnp.concatenate([jnp.concatenate([h0["k"], zero], axis=1),
                               jnp.concatenate([zero, h1["k"]], axis=1)], axis=0)
        gam_col = jnp.concatenate([h0["gam"], h1["gam"]], axis=0)
        gam_row = jnp.where(lane2 < L, h0["gam_row2"], h1["gam_row2"])
        decay2 = jnp.exp(jnp.where(bd_incl, gam_col - gam_row, NEG))
        ms.append(jnp.where(bd_strict, _mm_nt(lhs, rhs) * decay2, 0.0))
    tinvs = _tri_inv_multi(ms, 2 * L, L)
    for (h0, h1), tinv in zip(pairs, tinvs):
        rhs = jnp.concatenate([jnp.concatenate([h0["vb"], h0["kg"]], axis=1),
                               jnp.concatenate([h1["vb"], h1["kg"]], axis=1)], axis=0)
        uw = _mm(tinv, rhs)
        h0["u"], h0["w"] = uw[0:L, 0:LANES], uw[0:L, LANES:]
        h1["u"], h1["w"] = uw[L:2 * L, 0:LANES], uw[L:2 * L, LANES:]
    for hd in units:
        attn = _mm_nt(hd["q"], hd["k"]) * hd["decay"]
        uw = jnp.concatenate([hd["u"], hd["w"]], axis=1)
        auw = _mm(attn, uw)
        hd["o0"] = auw[:, 0:LANES]
        hd["qs"] = hd["qg"] - auw[:, LANES:]
        cc = _mm_tn(hd["kd"], uw)
        hd["c1"], hd["c2"] = cc[:, 0:LANES], cc[:, LANES:]

    ss = [s_ref[h] for h in range(GD_HEADS)]
    for ci in range(nch):
        hds = units[ci * GD_HEADS:(ci + 1) * GD_HEADS]
        for hd, s in zip(hds, ss):
            hd["o"] = hd["o0"] + _mm(hd["qs"], s)
        ss = [s * jnp.exp(hd["gam_last"]) + hd["c1"] - _mm(hd["c2"], s) for hd, s in zip(hds, ss)]
    for h in range(GD_HEADS):
        s_ref[h] = ss[h]
    for hd in units:
        sl = slice(hd["h"] * LANES, (hd["h"] + 1) * LANES)
        o_ref[0, hd["rs"], sl] = _rms_rows(hd["o"], ng_ref[...]) * jax.nn.silu(z[hd["rs"], sl])


def _gate_row(lo, vals):
    return jnp.zeros((1, LANES), F32).at[0, lo:lo + vals.shape[0]].set(vals.astype(F32))


def _gdn(p, pg, conv_w, a_log, dt_bias, norm_g):
    bsz, t, _ = p.shape
    gbias = _gate_row(GD_HEADS, dt_bias)
    nal = _gate_row(GD_HEADS, -jnp.exp(a_log))
    nal_col = nal.reshape(LANES, 1)
    ng = norm_g.reshape(1, GD_HEAD_DIM).astype(F32)
    full = lambda z: pl.BlockSpec(z.shape, lambda b, c: (0, 0))
    rows = GD_NCH * CHUNK
    return pl.pallas_call(
        _gdn_kernel,
        grid=(bsz, t // rows),
        in_specs=[pl.BlockSpec((1, rows, GD_MAIN), lambda b, c: (b, c, 0)),
                  pl.BlockSpec((1, rows, LANES), lambda b, c: (b, c, 0)),
                  full(conv_w), full(gbias), full(nal), full(nal_col), full(ng)],
        out_specs=pl.BlockSpec((1, rows, GD_DIM), lambda b, c: (b, c, 0)),
        out_shape=jax.ShapeDtypeStruct((bsz, t, GD_DIM), F32),
        scratch_shapes=[pltpu.VMEM((8, 3 * GD_DIM), F32), pltpu.VMEM((GD_HEADS, LANES, LANES), F32)],
        compiler_params=_cparams("arbitrary", "arbitrary"),
        name="gdn_mix",
    )(p, pg, conv_w.astype(F32), gbias, nal, nal_col, ng)


def _mlstm_kernel(p_ref, pg_ref, gb_ref, ng_ref, o_ref, c_ref, n_ref, m_ref):
    L = ML_CHUNK
    ci = pl.program_id(1)

    @pl.when(ci == 0)
    def _():
        c_ref[...] = jnp.zeros_like(c_ref)
        n_ref[...] = jnp.zeros_like(n_ref)
        m_ref[...] = jnp.zeros_like(m_ref)

    x = p_ref[0]
    gt = pg_ref[0] + gb_ref[...]
    logf = jax.nn.log_sigmoid(gt)
    gt_t = gt.T
    logf_t = jax.nn.log_sigmoid(gt_t)
    tr, tc = _iota2((L, L))
    bc_all = _cumsum_rows(jnp.where(tc <= tr, 1.0, 0.0).astype(BF16), logf)
    bc_t = _cumsum_cols(logf_t, jnp.where(tr <= tc, 1.0, 0.0).astype(BF16))
    incl = tc <= tr
    i_lo, f_lo = 2 * GD_HEADS, 2 * GD_HEADS + ML_HEADS

    heads = []
    for h in range(ML_HEADS):
        sl = slice(h * LANES, (h + 1) * LANES)
        q = x[:, sl]
        k = x[:, ML_DIM + h * LANES:ML_DIM + (h + 1) * LANES] * (ML_HEAD_DIM ** -0.5)
        v = x[:, 2 * ML_DIM + h * LANES:2 * ML_DIM + (h + 1) * LANES]
        bc = bc_all[:, f_lo + h:f_lo + h + 1]
        bc_row = bc_t[f_lo + h:f_lo + h + 1, :]
        ig = gt[:, i_lo + h:i_lo + h + 1]
        ig_row = gt_t[i_lo + h:i_lo + h + 1, :]
        m_prev = m_ref[h][0:1, 0:1]
        b_last = bc[L - 1:L, :]
        d_log = jnp.where(incl, bc - bc_row + ig_row, NEG)
        inter_log = bc + m_prev
        m_t = jnp.maximum(inter_log, jnp.max(d_log, axis=-1, keepdims=True))
        upd_log = b_last - bc + ig
        m_new = jnp.maximum(b_last + m_prev, jnp.max(upd_log, axis=0, keepdims=True))
        wk = jnp.exp(upd_log - m_new) * k
        heads.append(dict(sl=sl, q=q, v=v, m_t=m_t, m_new=m_new, wk=wk, inter_w=jnp.exp(inter_log - m_t),
                          dec=jnp.exp(b_last + m_prev - m_new), sc=_mm_nt(q, k) * jnp.exp(d_log - m_t)))
    for hd in heads:
        hd["scv"] = _mm(hd["sc"], hd["v"])
        hd["kv"] = _mm_tn(hd["wk"], hd["v"])
    for h, hd in enumerate(heads):
        cmat = c_ref[h]
        nvec = n_ref[h][0:1, :]
        q, sc, inter_w, dec = hd["q"], hd["sc"], hd["inter_w"], hd["dec"]
        num = inter_w * _mm(q, cmat) + hd["scv"]
        den = inter_w * jnp.sum(q * nvec, axis=-1, keepdims=True) + jnp.sum(sc, axis=-1, keepdims=True)
        hd["hh"] = num / jnp.maximum(jnp.abs(den), jnp.exp(-hd["m_t"]))
        c_ref[h] = dec * cmat + hd["kv"]
        n_ref[h] = jnp.broadcast_to(dec * nvec + jnp.sum(hd["wk"], axis=0, keepdims=True), (8, LANES))
        m_ref[h] = jnp.broadcast_to(hd["m_new"], (8, LANES))
    for h, hd in enumerate(heads):
        sl = hd["sl"]
        og = jax.nn.sigmoid(x[:, 3 * ML_DIM + h * LANES:3 * ML_DIM + (h + 1) * LANES])
        o_ref[0, :, sl] = og * _rms_rows(hd["hh"], ng_ref[:, sl])


def _mlstm(p, pg, i_bias, f_bias, norm_g):
    bsz, t, _ = p.shape
    gbias = _gate_row(2 * GD_HEADS, jnp.concatenate([i_bias, f_bias]))
    ng = norm_g.reshape(1, ML_DIM).astype(F32)
    lc = min(ML_CHUNK, t)
    full = lambda z: pl.BlockSpec(z.shape, lambda b, c: (0, 0))
    return pl.pallas_call(
        _mlstm_kernel,
        grid=(bsz, t // lc),
        in_specs=[pl.BlockSpec((1, lc, ML_MAIN), lambda b, c: (b, c, 0)),
                  pl.BlockSpec((1, lc, LANES), lambda b, c: (b, c, 0)), full(gbias), full(ng)],
        out_specs=pl.BlockSpec((1, lc, ML_DIM), lambda b, c: (b, c, 0)),
        out_shape=jax.ShapeDtypeStruct((bsz, t, ML_DIM), F32),
        scratch_shapes=[pltpu.VMEM((ML_HEADS, LANES, LANES), F32), pltpu.VMEM((ML_HEADS, 8, LANES), F32),
                        pltpu.VMEM((ML_HEADS, 8, LANES), F32)],
        compiler_params=_cparams("arbitrary", "arbitrary"),
        name="mlstm_mix",
    )(p, pg, gbias, ng)


def _route(h, g_ref, wh_ref, wl_ref, b_ref, off_ref):
    tm = h.shape[0]
    xn = _rms_rows(h, g_ref[...])
    xh = xn.astype(BF16)
    xl = (xn - xh.astype(F32)).astype(BF16)
    dot = lambda a, b: jnp.dot(a, b, preferred_element_type=F32)
    logits = dot(xh, wh_ref[...]) + dot(xh, wl_ref[...]) + dot(xl, wh_ref[...]) + b_ref[...]
    lt = logits.T[0:ROUTE_ROWS, :]
    row = lax.broadcasted_iota(I32, lt.shape, 0)
    big = jnp.int32(1 << 20)
    is_grp = (row >= N_EXPERTS) & (row < N_EXPERTS + N_GROUPS)
    lg = jnp.where(is_grp, lt, NEG)
    gmax = jnp.max(lg, axis=0, keepdims=True)
    p_top = 1.0 / jnp.sum(jnp.exp(lg - gmax), axis=0, keepdims=True)
    g_idx = jnp.min(jnp.where(lg == gmax, row, big), axis=0, keepdims=True) - N_EXPERTS
    valid = (row < N_EXPERTS) & ((row // EXPERTS_PER_GROUP) == g_idx)
    v1 = jnp.where(valid, lt, NEG)
    m1 = jnp.max(v1, axis=0, keepdims=True)
    i1 = jnp.min(jnp.where(v1 == m1, row, big), axis=0, keepdims=True)
    v2 = jnp.where(row == i1, NEG, v1)
    m2 = jnp.max(v2, axis=0, keepdims=True)
    i2 = jnp.min(jnp.where(v2 == m2, row, big), axis=0, keepdims=True)
    e21 = jnp.exp(m2 - m1)
    gate1 = p_top / (1.0 + e21)
    gate2 = p_top * e21 / (1.0 + e21)

    sel1 = row == i1
    sel2 = row == i2
    onehot = jnp.where(sel1 | sel2, 1.0, 0.0)
    tr, tc = _iota2((tm, tm))
    earlier = jnp.where(tr < tc, 1.0, 0.0).astype(BF16)
    off = off_ref[0:ROUTE_ROWS, 0:1]
    prefix = jnp.dot(onehot.astype(BF16), earlier, preferred_element_type=F32) + off
    rank1 = jnp.sum(jnp.where(sel1, prefix, 0.0), axis=0, keepdims=True)
    rank2 = jnp.sum(jnp.where(sel2, prefix, 0.0), axis=0, keepdims=True)
    off_ref[0:ROUTE_ROWS, :] = jnp.broadcast_to(off + jnp.sum(onehot, axis=1, keepdims=True), (ROUTE_ROWS, LANES))

    packed = jnp.concatenate([i1.astype(F32), i2.astype(F32), rank1, rank2, gate1, gate2,
                              jnp.zeros((2, tm), F32)], axis=0)
    return xn, packed


def _out_proj_route_kernel(h_ref, ya_ref, yb_ref, wa_ref, wb_ref, g_ref, wh_ref, wl_ref, b_ref,
                           o_ref, xt_ref, rg_ref, cnt_ref, dest_ref, off_ref, rt_ref):
    phase = pl.program_id(0)
    s = pl.program_id(1)
    tm = h_ref.shape[0]

    @pl.when((phase == 0) & (s == 0))
    def _():
        off_ref[...] = jnp.zeros_like(off_ref)

    @pl.when(phase == 0)
    def _():
        out = h_ref[...] + _mm(ya_ref[...], wa_ref[...]) + _mm(yb_ref[...], wb_ref[...])
        o_ref[...] = out
        xn, packed = _route(out, g_ref, wh_ref, wl_ref, b_ref, off_ref)
        _store_row_tiles(xt_ref, xn)
        rt_ref[s] = packed
        wide = jnp.concatenate([packed, jnp.zeros((LANES - 8, tm), F32)], axis=0)
        rg_ref[...] = wide.T
        cnt_ref[...] = off_ref[...]

    @pl.when(phase == 1)
    def _():
        counts = off_ref[...]
        tiles_per = jnp.floor((counts + (MOE_TM - 1)) * (1.0 / MOE_TM))
        er, ec = _iota2((LANES, LANES))
        tile_end = _mm(jnp.where(ec <= er, 1.0, 0.0), tiles_per)
        seg_start = ((tile_end - tiles_per) * MOE_TM)[0:ROUTE_ROWS, 0:1]
        blk = rt_ref[s]
        row = lax.broadcasted_iota(I32, (ROUTE_ROWS, tm), 0).astype(F32)
        pick = lambda e: jnp.sum(jnp.where(row == e, seg_start, 0.0), axis=0, keepdims=True)
        d1 = pick(blk[0:1]) + blk[2:3]
        d2 = pick(blk[1:2]) + blk[3:4]
        dest_ref[...] = jnp.concatenate([d1, d2, jnp.zeros((6, tm), F32)], axis=0).astype(I32)


def _out_proj_route(h, ya, yb, w_out, g_ffn, w_group, b_group, w_router, b_router):
    n, d = h.shape
    da, db = ya.shape[1], yb.shape[1]
    tm = min(PROJ_TM, n)
    wa = w_out[:da].astype(BF16)
    wb = w_out[da:].astype(BF16)
    pad = LANES - N_EXPERTS - N_GROUPS
    w_cat = jnp.concatenate([w_router, w_group, jnp.zeros((d, pad), F32)], axis=1)
    b_cat = jnp.concatenate([b_router, b_group, jnp.zeros((pad,), F32)]).reshape(1, LANES)
    w_hi = w_cat.astype(BF16)
    w_lo = (w_cat - w_hi.astype(F32)).astype(BF16)
    steps = n // tm
    row = lambda p, s: (s * (1 - p) + (steps - 1) * p, 0)
    fixed = lambda p, s: (0, 0)
    return pl.pallas_call(
        _out_proj_route_kernel,
        grid=(2, steps),
        in_specs=[pl.BlockSpec((tm, d), row), pl.BlockSpec((tm, da), row), pl.BlockSpec((tm, db), row),
                  pl.BlockSpec((da, d), fixed), pl.BlockSpec((db, d), fixed), pl.BlockSpec((1, d), fixed),
                  pl.BlockSpec((d, LANES), fixed), pl.BlockSpec((d, LANES), fixed), pl.BlockSpec((1, LANES), fixed)],
        out_specs=[pl.BlockSpec((tm, d), row), pl.BlockSpec((tm * ROW_TILE, LANES), row),
                   pl.BlockSpec((tm, LANES), row), pl.BlockSpec((LANES, LANES), fixed),
                   pl.BlockSpec((8, tm), lambda p, s: (0, s * p))],
        out_shape=[jax.ShapeDtypeStruct((n, d), F32), jax.ShapeDtypeStruct((n * ROW_TILE, LANES), F32),
                   jax.ShapeDtypeStruct((n, LANES), F32), jax.ShapeDtypeStruct((LANES, LANES), F32),
                   jax.ShapeDtypeStruct((8, n), I32)],
        scratch_shapes=[pltpu.VMEM((LANES, LANES), F32), pltpu.VMEM((steps, 8, tm), F32)],
        compiler_params=_cparams("arbitrary", "arbitrary"),
        name="out_proj_route",
    )(h, ya, yb, wa, wb, g_ffn.reshape(1, d), w_hi, w_lo, b_cat)


def _invert_kernel(d1_ref, d2_ref, o_ref):
    def clear(i, carry):
        o_ref[i] = 0
        return carry

    def fill(t, carry):
        o_ref[d1_ref[t]] = t
        o_ref[d2_ref[t]] = t
        return carry

    lax.fori_loop(0, o_ref.shape[0], clear, 0, unroll=GATHER_UNROLL)
    lax.fori_loop(0, d1_ref.shape[0], fill, 0, unroll=GATHER_UNROLL)


def _invert(d1, d2, n_rows):
    grid_spec = pltpu.PrefetchScalarGridSpec(
        num_scalar_prefetch=2, grid=(1,), in_specs=[],
        out_specs=pl.BlockSpec(memory_space=pltpu.SMEM))
    return pl.pallas_call(
        _invert_kernel,
        grid_spec=grid_spec,
        out_shape=jax.ShapeDtypeStruct((n_rows,), I32),
        compiler_params=_cparams("arbitrary"),
        name="moe_invert",
    )(d1, d2)


def _gather_rows(src_hbm, idx_ref, base, dst, sem):
    def body(j, carry):
        r = pl.multiple_of(idx_ref[base + j] * ROW_TILE, ROW_TILE)
        pltpu.make_async_copy(src_hbm.at[pl.ds(r, ROW_TILE), :],
                              dst.at[pl.ds(pl.multiple_of(j * ROW_TILE, ROW_TILE), ROW_TILE), :],
                              sem).start()
        return carry

    lax.fori_loop(0, dst.shape[0] // ROW_TILE, body, 0, unroll=GATHER_UNROLL)


def _wait_rows(src_hbm, dst, sem):
    pltpu.make_async_copy(src_hbm.at[pl.ds(0, dst.shape[0]), :], dst, sem).wait()


def _expert_kernel(te_ref, nt_ref, src_ref, x_hbm, wg_ref, wu_ref, wd_ref, o_ref, xbuf, wgb, wub, wdb, sem):
    tm = xbuf.shape[1] // ROW_TILE
    i = pl.program_id(0)
    n_used = nt_ref[0]
    slot = i % EXPERT_BUFS

    @pl.when(i == 0)
    def _():
        for t in range(EXPERT_BUFS - 1):
            @pl.when(t < n_used)
            def _():
                _gather_rows(x_hbm, src_ref, t * tm, xbuf.at[t], sem.at[t])

    ahead = i + EXPERT_BUFS - 1

    @pl.when(ahead < n_used)
    def _():
        _gather_rows(x_hbm, src_ref, ahead * tm, xbuf.at[ahead % EXPERT_BUFS], sem.at[ahead % EXPERT_BUFS])

    @pl.when((i == 0) | (te_ref[i] != te_ref[jnp.maximum(i - 1, 0)]))
    def _():
        wgb[...] = wg_ref[0].astype(BF16)
        wub[...] = wu_ref[0].astype(BF16)
        wdb[...] = wd_ref[0].astype(BF16)

    @pl.when(i < n_used)
    def _():
        _wait_rows(x_hbm, xbuf.at[slot], sem.at[slot])
        xn = _load_row_tiles(xbuf.at[slot], tm).astype(BF16)
        gate = jnp.dot(xn, wgb[...], preferred_element_type=F32)
        up = jnp.dot(xn, wub[...], preferred_element_type=F32)
        hid = (jax.nn.silu(gate) * up).astype(BF16)
        _store_row_tiles(o_ref, jnp.dot(hid, wdb[...], preferred_element_type=F32))

    @pl.when(i >= n_used)
    def _():
        o_ref[...] = jnp.zeros_like(o_ref)


def _experts(xn_tiles, tile_expert, n_used, src, w_gate, w_up, w_down):
    d = D_MODEL
    n_tiles = tile_expert.shape[0]
    tm = src.shape[0] // n_tiles
    de = w_gate.shape[-1]
    grid_spec = pltpu.PrefetchScalarGridSpec(
        num_scalar_prefetch=3,
        grid=(n_tiles,),
        in_specs=[pl.BlockSpec(memory_space=pl.ANY),
                  pl.BlockSpec((1, d, de), lambda i, te, nt, s: (te[i], 0, 0)),
                  pl.BlockSpec((1, d, de), lambda i, te, nt, s: (te[i], 0, 0)),
                  pl.BlockSpec((1, de, d), lambda i, te, nt, s: (te[i], 0, 0))],
        out_specs=pl.BlockSpec((tm * ROW_TILE, LANES), lambda i, te, nt, s: (i, 0)),
        scratch_shapes=[pltpu.VMEM((EXPERT_BUFS, tm * ROW_TILE, LANES), F32), pltpu.VMEM((d, de), BF16),
                        pltpu.VMEM((d, de), BF16), pltpu.VMEM((de, d), BF16),
                        pltpu.SemaphoreType.DMA((EXPERT_BUFS,))],
    )
    return pl.pallas_call(
        _expert_kernel,
        grid_spec=grid_spec,
        out_shape=jax.ShapeDtypeStruct((n_tiles * tm * ROW_TILE, LANES), F32),
        compiler_params=_cparams("arbitrary"),
        name="moe_experts",
    )(tile_expert, n_used, src, xn_tiles, w_gate, w_up, w_down)


def _combine_kernel(d1_ref, d2_ref, ys_hbm, h_ref, rg_ref, g_ref, *refs, n_proj, final_norm):
    w_refs, o_ref, p_refs = refs[:n_proj], refs[n_proj], refs[n_proj + 1:2 * n_proj + 1]
    y1, y2, sem = refs[2 * n_proj + 1:]
    tm = h_ref.shape[0]
    i = pl.program_id(0)
    slot = i % 2

    def gather(tile, s):
        _gather_rows(ys_hbm, d1_ref, tile * tm, y1.at[s], sem.at[0, s])
        _gather_rows(ys_hbm, d2_ref, tile * tm, y2.at[s], sem.at[1, s])

    @pl.when(i == 0)
    def _():
        gather(0, 0)

    @pl.when(i + 1 < pl.num_programs(0))
    def _():
        gather(i + 1, 1 - slot)

    _wait_rows(ys_hbm, y1.at[slot], sem.at[0, slot])
    _wait_rows(ys_hbm, y2.at[slot], sem.at[1, slot])
    rg = rg_ref[...]
    out = (h_ref[...] + rg[:, 4:5] * _load_row_tiles(y1.at[slot], tm)
           + rg[:, 5:6] * _load_row_tiles(y2.at[slot], tm))
    if final_norm:
        out = _rms_rows(out, g_ref[...])
    o_ref[...] = out
    if n_proj:
        y = _rms_rows(out, g_ref[...]).astype(BF16)
        for w_ref, p_ref in zip(w_refs, p_refs):
            p_ref[...] = jnp.dot(y, w_ref[...], preferred_element_type=F32)


def _combine(h, ys, d1, d2, rg, g, ws=(), final_norm=False):
    n, d = h.shape
    tm = min(PROJ_TM, n)
    row = lambda i, a, b: (i, 0)
    fixed = lambda i, a, b: (0, 0)
    grid_spec = pltpu.PrefetchScalarGridSpec(
        num_scalar_prefetch=2,
        grid=(n // tm,),
        in_specs=[pl.BlockSpec(memory_space=pl.ANY), pl.BlockSpec((tm, d), row), pl.BlockSpec((tm, LANES), row),
                  pl.BlockSpec((1, d), fixed)] + [pl.BlockSpec(w.shape, fixed) for w in ws],
        out_specs=[pl.BlockSpec((tm, d), row)] + [pl.BlockSpec((tm, w.shape[1]), row) for w in ws],
        scratch_shapes=[pltpu.VMEM((2, tm * ROW_TILE, LANES), F32), pltpu.VMEM((2, tm * ROW_TILE, LANES), F32),
                        pltpu.SemaphoreType.DMA((2, 2))],
    )
    return pl.pallas_call(
        functools.partial(_combine_kernel, n_proj=len(ws), final_norm=final_norm),
        grid_spec=grid_spec,
        out_shape=[jax.ShapeDtypeStruct((n, d), F32)] + [jax.ShapeDtypeStruct((n, w.shape[1]), F32) for w in ws],
        compiler_params=_cparams("arbitrary"),
        name="moe_combine",
    )(d1, d2, ys, h, rg, g.reshape(1, d), *ws)


def _hmoe_residual(h, xn_tiles, rg, cnt, dest, layer, w_gate, w_up, w_down, g, ws=(), final_norm=False):
    n, d = h.shape
    stack = lambda w: w.reshape((-1,) + w.shape[2:])

    counts = cnt[:N_EXPERTS, 0].astype(I32)
    n_tiles = (2 * n) // MOE_TM + N_EXPERTS
    tile_end = jnp.cumsum((counts + MOE_TM - 1) // MOE_TM)
    d1, d2 = dest[0], dest[1]
    tile_ids = jnp.arange(n_tiles, dtype=I32)
    tile_expert = jnp.minimum(jnp.sum((tile_end[None, :] <= tile_ids[:, None]).astype(I32), axis=1), N_EXPERTS - 1)
    tile_expert = tile_expert + layer * N_EXPERTS
    src = _invert(d1, d2, n_tiles * MOE_TM)

    n_used = tile_end[N_EXPERTS - 1:].astype(I32)
    ys = _experts(xn_tiles, tile_expert, n_used, src, stack(w_gate), stack(w_up), stack(w_down))
    return _combine(h, ys, d1, d2, rg, g, ws, final_norm)


def kernel(x, norm_mix, norm_ffn, norm_final, ev_w_in, ev_mu, rw_w0, rw_w2, rw_a0, rw_a2, rw_g2, rw_k_k, rw_k_a, rw_r_k, rw_ln_w, rw_ln_b, hg_lb_logits, hg_norm, ev_w_out, od_w_in, gd_conv, gd_a_log, gd_dt_bias, gd_norm, ml_i_bias, ml_f_bias, ml_norm, od_w_out, moe_w_group, moe_b_group, moe_w_router, moe_b_router, moe_w_gate, moe_w_up, moe_w_down):
    bsz, t, d = x.shape
    n = bsz * t
    depth = norm_mix.shape[0]
    lb_table = jnp.cumsum(jax.nn.softmax(hg_lb_logits.astype(F32), axis=0), axis=0)
    def in_proj_weights(layer):
        j = layer // 2
        if layer % 2 == 0:
            return _pack_weights(ev_w_in[j], [(0, RW_IN), (RW_IN, RW_IN + HG_IN)])
        return _pack_weights(od_w_in[j], [(0, GD_MAIN), (GD_IN, GD_IN + ML_MAIN)],
                             gate_ranges=[(GD_MAIN, GD_IN), (GD_IN + ML_MAIN, GD_IN + ML_MAIN + 2 * ML_HEADS)])

    h = x.reshape(n, d)
    proj = _rms_proj(h, norm_mix[0], in_proj_weights(0))
    for layer in range(depth):
        j = layer // 2
        if layer % 2 == 0:
            p_rw, p_hg = proj
            ya = _rwkv7(p_rw.reshape(bsz, t, RW_IN), ev_mu[j], rw_w0[j], rw_w2[j], rw_a0[j], rw_a2[j], rw_g2[j],
                        rw_k_k[j], rw_k_a[j], rw_r_k[j], rw_ln_w[j], rw_ln_b[j])
            yb = _hgrn2(p_hg.reshape(bsz, t, HG_IN), lb_table[j], hg_norm[j])
            mix_a, mix_b, w_out = ya.reshape(n, RW_DIM), yb.reshape(n, HG_KDIM), ev_w_out[j]
        else:
            p_gd, p_ml, p_gt = proj
            p_gt = p_gt.reshape(bsz, t, LANES)
            yc = _gdn(p_gd.reshape(bsz, t, GD_MAIN), p_gt, gd_conv[j], gd_a_log[j], gd_dt_bias[j], gd_norm[j])
            yd = _mlstm(p_ml.reshape(bsz, t, ML_MAIN), p_gt, ml_i_bias[j], ml_f_bias[j], ml_norm[j])
            mix_a, mix_b, w_out = yc.reshape(n, GD_DIM), yd.reshape(n, ML_DIM), od_w_out[j]
        h, xn_tiles, rg, cnt, dest = _out_proj_route(h, mix_a, mix_b, w_out, norm_ffn[layer], moe_w_group[layer],
                                                     moe_b_group[layer], moe_w_router[layer], moe_b_router[layer])
        if layer == depth - 1:
            (h,) = _hmoe_residual(h, xn_tiles, rg, cnt, dest, layer, moe_w_gate, moe_w_up, moe_w_down,
                                  norm_final, final_norm=True)
        else:
            h, *proj = _hmoe_residual(h, xn_tiles, rg, cnt, dest, layer, moe_w_gate, moe_w_up, moe_w_down,
                                      norm_mix[layer + 1], ws=in_proj_weights(layer + 1))
    return h.reshape(bsz, t, d)
```

```python
import functools
import math

import jax
import jax.numpy as jnp
from jax import lax
from jax.experimental import pallas as pl
from jax.experimental.pallas import tpu as pltpu

F32 = jnp.float32
BF16 = jnp.bfloat16
I32 = jnp.int32
HIGHEST = lax.Precision.HIGHEST

D_MODEL = 1024
NORM_EPS = 1e-6
RW_HEADS, RW_HEAD_DIM = 8, 64
RW_DIM = RW_HEADS * RW_HEAD_DIM
R_DECAY, R_AAA, R_GATE = 64, 64, 128
RW_IN = 3 * RW_DIM + R_DECAY + R_AAA + R_GATE
RW_LN_EPS = 64e-5
HG_HEADS, HG_DIM = 8, 64
HG_KDIM = HG_HEADS * HG_DIM
HG_IN = 4 * HG_KDIM
GD_HEADS, GD_HEAD_DIM = 4, 128
GD_DIM = GD_HEADS * GD_HEAD_DIM
CONV_K = 4
GD_MAIN = 4 * GD_DIM
GD_IN = GD_MAIN + 2 * GD_HEADS
ML_HEADS, ML_HEAD_DIM = 4, 128
ML_DIM = ML_HEADS * ML_HEAD_DIM
ML_MAIN = 4 * ML_DIM
N_GROUPS, EXPERTS_PER_GROUP = 4, 8
N_EXPERTS = N_GROUPS * EXPERTS_PER_GROUP
D_EXPERT = 256

LANES = 128
VMEM_LIMIT_BYTES = 48 * 1024 * 1024

PROJ_TM = 256
PACK_TM = 128
CHUNK = 64
SUB = 16
RW_NCH = 4
GD_NCH = 8
HG_ROWS = 128
ML_CHUNK = 128
ML_NCH = 2
MOE_TM = 256
ROUTE_ROWS = 40
EXPERT_BUFS = 3
GATHER_UNROLL = 8
NEG = -1e30


def _cparams(*sem):
    return pltpu.CompilerParams(dimension_semantics=sem, vmem_limit_bytes=VMEM_LIMIT_BYTES)


def _mm(a, b):
    return jnp.dot(a.astype(BF16), b.astype(BF16), preferred_element_type=F32)


def _mm_nt(a, b):
    return lax.dot_general(a.astype(BF16), b.astype(BF16), (((1,), (1,)), ((), ())), preferred_element_type=F32)


def _mm_tn(a, b):
    return lax.dot_general(a.astype(BF16), b.astype(BF16), (((0,), (0,)), ((), ())), preferred_element_type=F32)


def _mm_hi(a, b):
    return jnp.dot(a, b, precision=HIGHEST, preferred_element_type=F32)


def _mm_split(x, ones_bf16):
    hi = x.astype(BF16)
    lo = (x - hi.astype(F32)).astype(BF16)
    return (jnp.dot(hi, ones_bf16, preferred_element_type=F32) + jnp.dot(lo, ones_bf16, preferred_element_type=F32))


def _iota2(shape):
    return lax.broadcasted_iota(I32, shape, 0), lax.broadcasted_iota(I32, shape, 1)


def _rms_rows(x, g, eps=NORM_EPS):
    return x * lax.rsqrt(jnp.mean(x * x, axis=-1, keepdims=True) + eps) * g


def _seg_ones(width, seg):
    r, c = _iota2((width, width))
    return jnp.where((r // seg) == (c // seg), 1.0, 0.0).astype(BF16)


def _split3(x):
    x1 = x.astype(BF16)
    r1 = x - x1.astype(F32)
    x2 = r1.astype(BF16)
    return x1, x2, (r1 - x2.astype(F32)).astype(BF16)


def _cumsum_rows(tri_bf16, x):
    return sum(jnp.dot(tri_bf16, t, preferred_element_type=F32) for t in _split3(x))


def _cumsum_cols(x, tri_bf16):
    return sum(jnp.dot(t, tri_bf16, preferred_element_type=F32) for t in _split3(x))


def _mm3(a, b):
    ah = a.astype(BF16)
    al = (a - ah.astype(F32)).astype(BF16)
    bh = b.astype(BF16)
    bl = (b - bh.astype(F32)).astype(BF16)
    dot = lambda x, y: jnp.dot(x, y, preferred_element_type=F32)
    return dot(ah, bh) + dot(ah, bl) + dot(al, bh)


def _tri_inv_multi(ms, n, chain):
    assert chain // SUB <= 4
    r, c = _iota2((n, n))
    same = (r // SUB) == (c // SUB)
    eye = jnp.where(r == c, 1.0, 0.0).astype(F32)
    ds = [jnp.where(same, m, 0.0) for m in ms]
    offs = [m - d for m, d in zip(ms, ds)]
    xs = [eye - d for d in ds]
    ps = ds
    for _ in range(3):
        ps = [_mm(p, p) for p in ps]
        xs = [x + _mm(x, p) for x, p in zip(xs, ps)]
    es = [_mm(x, o) for x, o in zip(xs, offs)]
    imes = [eye - e for e in es]
    e2s = [_mm(e, e) for e in es]
    ys = [i + _mm(i, e2) for i, e2 in zip(imes, e2s)]
    xs = [_mm(y, x) for y, x in zip(ys, xs)]
    res = [eye - x - _mm3(m, x) for m, x in zip(ms, xs)]
    return [x + _mm(x, rr) for x, rr in zip(xs, res)]


def _pack_weights_kernel(w_ref, *o_refs, ranges, gate_ranges):
    w = w_ref[...]
    for (lo, hi), o_ref in zip(ranges, o_refs):
        o_ref[...] = w[:, lo:hi].astype(BF16)
    if gate_ranges:
        cols = [w[:, lo:hi] for lo, hi in gate_ranges]
        used = sum(hi - lo for lo, hi in gate_ranges)
        cols.append(jnp.zeros((w.shape[0], LANES - used), F32))
        o_refs[-1][...] = jnp.concatenate(cols, axis=1).astype(BF16)


def _pack_weights(w, ranges, gate_ranges=()):
    rows, cols = w.shape
    tm = PACK_TM
    widths = [hi - lo for lo, hi in ranges] + ([LANES] if gate_ranges else [])
    return pl.pallas_call(
        functools.partial(_pack_weights_kernel, ranges=tuple(ranges), gate_ranges=tuple(gate_ranges)),
        grid=(rows // tm,),
        in_specs=[pl.BlockSpec((tm, cols), lambda i: (i, 0))],
        out_specs=[pl.BlockSpec((tm, wd), lambda i: (i, 0)) for wd in widths],
        out_shape=[jax.ShapeDtypeStruct((rows, wd), BF16) for wd in widths],
        compiler_params=_cparams("parallel"),
        name="pack_weights",
    )(w)


def _rms_proj_kernel(x_ref, g_ref, *refs, n_out):
    y = _rms_rows(x_ref[...], g_ref[...]).astype(BF16)
    for w_ref, o_ref in zip(refs[:n_out], refs[n_out:]):
        o_ref[...] = jnp.dot(y, w_ref[...], preferred_element_type=F32)


def _rms_proj(x, g, ws):
    n, d = x.shape
    tm = min(PROJ_TM, n)
    in_specs = [pl.BlockSpec((tm, d), lambda i: (i, 0)), pl.BlockSpec((1, d), lambda i: (0, 0))]
    in_specs += [pl.BlockSpec(w.shape, lambda i: (0, 0)) for w in ws]
    return pl.pallas_call(
        functools.partial(_rms_proj_kernel, n_out=len(ws)),
        grid=(n // tm,),
        in_specs=in_specs,
        out_specs=[pl.BlockSpec((tm, w.shape[1]), lambda i: (i, 0)) for w in ws],
        out_shape=[jax.ShapeDtypeStruct((n, w.shape[1]), F32) for w in ws],
        compiler_params=_cparams("parallel"),
        name="rms_proj",
    )(x, g.reshape(1, d), *ws)


ROW_TILE = D_MODEL // LANES


def _store_row_tiles(ref, x):
    for j in range(ROW_TILE):
        ref[pl.ds(j, x.shape[0], stride=ROW_TILE), :] = x[:, j * LANES:(j + 1) * LANES]


def _load_row_tiles(ref, rows):
    return jnp.concatenate([ref[pl.ds(j, rows, stride=ROW_TILE), :] for j in range(ROW_TILE)], axis=1)


def _rwkv7_kernel(p_ref, mu_ref, w0_ref, w2_ref, a0_ref, a2_ref, g2_ref, kk_ref, ka_ref, rk_ref,
                  lnw_ref, lnb_ref, o_ref, prev_ref, zt_ref):
    L = CHUNK
    npair = RW_HEADS // 2
    c = pl.program_id(1)

    @pl.when(c == 0)
    def _():
        prev_ref[...] = jnp.zeros_like(prev_ref)
        zt_ref[...] = jnp.zeros_like(zt_ref)

    x = p_ref[0]
    R = x.shape[0]
    nch = R // L
    row = lax.broadcasted_iota(I32, x.shape, 0)
    xs = jnp.where(row == 0, prev_ref[7:8, :], pltpu.roll(x, 1, 0))
    prev_ref[...] = x[R - 8:R, :]
    pm = x + mu_ref[...] * (xs - x)
    r_all = pm[:, 0:RW_DIM]
    k_all = pm[:, RW_DIM:2 * RW_DIM]
    v_all = pm[:, 2 * RW_DIM:3 * RW_DIM]
    wa = pm[:, 3 * RW_DIM:3 * RW_DIM + LANES]
    gl = pm[:, 3 * RW_DIM + LANES:]
    wlog = -jax.nn.softplus(-(w0_ref[...] + _mm(jnp.tanh(wa), w2_ref[...]))) - 0.5
    ld = -jnp.exp(wlog)
    a_all = jax.nn.sigmoid(a0_ref[...] + _mm(wa, a2_ref[...]))
    g_all = _mm(jax.nn.sigmoid(gl), g2_ref[...])

    tr, tc = _iota2((R, R))
    chunk_tril = jnp.where((tc <= tr) & ((tr // L) == (tc // L)), 1.0, 0.0).astype(BF16)
    cs_all = _cumsum_rows(chunk_tril, ld)
    seg = _seg_ones(LANES, RW_HEAD_DIM)
    lane = lax.broadcasted_iota(I32, (L, LANES), 1)
    hm = (lane < RW_HEAD_DIM, lane >= RW_HEAD_DIM)
    br, bc = _iota2((2 * L, 2 * L))
    bd = (br // L) == (bc // L)
    bd_strict = bd & (bc < br)
    bd_incl = bd & (bc <= br)
    fold = lambda z: z[0:L] + z[L:2 * L]
    both = lambda z: jnp.concatenate([jnp.where(hm[0], z, 0.0), jnp.where(hm[1], z, 0.0)], axis=0)

    units = []
    for ci in range(nch):
        rs = slice(ci * L, (ci + 1) * L)
        for p in range(npair):
            sl = slice(p * LANES, (p + 1) * LANES)
            r, k, v, a = r_all[rs, sl], k_all[rs, sl], v_all[rs, sl], a_all[rs, sl]
            cs, ldp = cs_all[rs, sl], ld[rs, sl]
            kkr = k * kk_ref[:, sl]
            kk = kkr * lax.rsqrt(_mm_split(kkr * kkr, seg) + 1e-6)
            k2 = k * (1.0 + (a - 1.0) * ka_ref[:, sl])
            b = kk * a
            cs_last = cs[L - 1:L, :]
            e_neg = jnp.exp(-cs)
            e_rem = jnp.exp(cs_last - cs)
            bhat = b * e_neg
            khat = k2 * e_neg
            units.append(dict(p=p, rs=rs, sl=sl, r=r, v=v, k2=k2, rhat=r * jnp.exp(cs), gam_last=jnp.exp(cs_last),
                              btil=b * e_rem, ktil=k2 * e_rem, a2=both(kk * jnp.exp(cs - ldp)), v2=both(v),
                              rhs4=jnp.concatenate([bhat, bhat, khat, khat], axis=0)))
    for q in units:
        lhs = jnp.concatenate([q["a2"], both(q["rhat"])], axis=0)
        q["g"] = _mm_nt(lhs, q["rhs4"])
    tinvs = _tri_inv_multi([jnp.where(bd_strict, q["g"][0:2 * L, 0:2 * L], 0.0) for q in units], 2 * L, L)
    for q, tinv in zip(units, tinvs):
        q["tinv"] = tinv
        q["x2"] = _mm(jnp.where(bd_strict, q["g"][0:2 * L, 2 * L:4 * L], 0.0), q["v2"])
    for q in units:
        uw = _mm(q["tinv"], jnp.concatenate([q["x2"], q["a2"]], axis=1))
        q["u0"] = -fold(uw[:, 0:LANES])
        q["w"] = fold(uw[:, LANES:])
        q["y0"] = fold(_mm(jnp.where(bd_incl, q["g"][2 * L:4 * L, 2 * L:4 * L], 0.0), q["v2"]))
    for q in units:
        rb = jnp.where(bd_incl, q["g"][2 * L:4 * L, 0:2 * L], 0.0)
        ruw = _mm(rb, jnp.concatenate([both(q["u0"]), both(q["w"])], axis=1))
        q["yc"] = q["y0"] + fold(ruw[:, 0:LANES])
        q["ry"] = q["rhat"] - fold(ruw[:, LANES:])
        q["c1"] = _mm_tn(jnp.concatenate([q["u0"], q["v"]], axis=0), jnp.concatenate([q["btil"], q["ktil"]], axis=0))
        q["c2"] = _mm_tn(q["w"], q["btil"])

    hr, hc = _iota2((LANES, LANES))
    head_bd = (hr // RW_HEAD_DIM) == (hc // RW_HEAD_DIM)
    zts = [zt_ref[p] for p in range(npair)]
    for ci in range(nch):
        qs = units[ci * npair:(ci + 1) * npair]
        for q, zt in zip(qs, zts):
            q["y"] = q["yc"] + _mm_nt(q["ry"], zt)
        zts = [zt * q["gam_last"] + jnp.where(head_bd, q["c1"] - _mm(zt, q["c2"]), 0.0) for q, zt in zip(qs, zts)]
    for p in range(npair):
        zt_ref[p] = zts[p]

    for q in units:
        sl, rs, y = q["sl"], q["rs"], q["y"]
        mean = _mm_split(y, seg) * (1.0 / RW_HEAD_DIM)
        yc = y - mean
        var = _mm_split(yc * yc, seg) * (1.0 / RW_HEAD_DIM)
        yn = yc * lax.rsqrt(var + RW_LN_EPS) * lnw_ref[:, sl] + lnb_ref[:, sl]
        bonus = _mm_split(q["r"] * q["k2"] * rk_ref[:, sl], seg) * q["v"]
        o_ref[0, rs, sl] = (yn + bonus) * g_all[rs, sl]


def _rwkv7(p, mu, w0, w2, a0, a2, g2, k_k, k_a, r_k, ln_w, ln_b):
    bsz, t, _ = p.shape
    row = lambda z: z.reshape(1, -1).astype(F32)
    w2p = jnp.concatenate([w2, jnp.zeros_like(w2)], axis=0).astype(BF16)
    a2p = jnp.concatenate([jnp.zeros_like(a2), a2], axis=0).astype(BF16)
    params = [row(mu), row(w0), w2p, row(a0), a2p, g2.astype(BF16), row(k_k), row(k_a), row(r_k), row(ln_w), row(ln_b)]
    full = lambda z: pl.BlockSpec(z.shape, lambda b, c: (0, 0))
    rows = RW_NCH * CHUNK
    return pl.pallas_call(
        _rwkv7_kernel,
        grid=(bsz, t // rows),
        in_specs=[pl.BlockSpec((1, rows, RW_IN), lambda b, c: (b, c, 0))] + [full(z) for z in params],
        out_specs=pl.BlockSpec((1, rows, RW_DIM), lambda b, c: (b, c, 0)),
        out_shape=jax.ShapeDtypeStruct((bsz, t, RW_DIM), F32),
        scratch_shapes=[pltpu.VMEM((8, RW_IN), F32), pltpu.VMEM((RW_HEADS // 2, LANES, LANES), F32)],
        compiler_params=_cparams("arbitrary", "arbitrary"),
        name="rwkv7_mix",
    )(p, *params)


def _hgrn2_kernel(p_ref, lb_ref, ng_ref, o_ref, st_ref):
    L = HG_ROWS
    c = pl.program_id(1)

    @pl.when(c == 0)
    def _():
        st_ref[...] = jnp.zeros_like(st_ref)

    x = p_ref[0]
    lb = lb_ref[...]
    q_all = jax.nn.silu(x[:, 0:HG_KDIM])
    fg = lb + (1.0 - lb) * jax.nn.sigmoid(x[:, HG_KDIM:2 * HG_KDIM])
    k_all = 1.0 - fg
    logf = jnp.log(fg)
    v_all = x[:, 2 * HG_KDIM:3 * HG_KDIM]
    gate = x[:, 3 * HG_KDIM:]
    tr, tc = _iota2((L, L))
    blk_tril = jnp.where((tc <= tr) & ((tr // SUB) == (tc // SUB)), 1.0, 0.0).astype(BF16)
    bc_all = _cumsum_rows(blk_tril, logf)
    seg = _seg_ones(LANES, HG_DIM)
    br, bcc = _iota2((LANES, LANES))
    bd = (br // HG_DIM) == (bcc // HG_DIM)
    half = SUB // 2
    t_lo = lax.broadcasted_iota(I32, (SUB, LANES), 0)
    t_hi = lax.broadcasted_iota(I32, (half, LANES), 0) + half

    nsub = L // SUB
    npair = HG_HEADS // 2
    rows_per_unit = half * SUB + half * half
    sub_of_row = lax.broadcasted_iota(I32, (L, LANES), 0) // SUB

    units = []
    parts = []
    for p in range(npair):
        sl = slice(p * LANES, (p + 1) * LANES)
        for j in range(nsub):
            rs = slice(j * SUB, (j + 1) * SUB)
            q, k, v, bc = q_all[rs, sl], k_all[rs, sl], v_all[rs, sl], bc_all[rs, sl]
            q_hi, bc_hi = q[half:], bc[half:]
            for s in range(SUB):
                if s < half:
                    diff = jnp.where(t_lo >= s, bc - bc[s:s + 1, :], NEG)
                    parts.append(jnp.exp(diff) * q * k[s:s + 1, :])
                else:
                    diff = jnp.where(t_hi >= s, bc_hi - bc[s:s + 1, :], NEG)
                    parts.append(jnp.exp(diff) * q_hi * k[s:s + 1, :])
            bend = bc[SUB - 1:SUB, :]
            units.append(dict(v=v, qt=q * jnp.exp(bc), dec=jnp.exp(bend), kd=k * jnp.exp(bend - bc)))
    score_all = _mm(jnp.concatenate(parts, axis=0), seg)
    for ui, un in enumerate(units):
        score = score_all[ui * rows_per_unit:(ui + 1) * rows_per_unit]
        v = un["v"]
        acc_lo = jnp.zeros((half, LANES), F32)
        acc_hi = jnp.zeros((half, LANES), F32)
        off = 0
        for s in range(SUB):
            vs = v[s:s + 1, :]
            if s < half:
                acc_lo = acc_lo + score[off:off + half] * vs
                acc_hi = acc_hi + score[off + half:off + SUB] * vs
                off += SUB
            else:
                acc_hi = acc_hi + score[off:off + half] * vs
                off += half
        un["intra"] = jnp.concatenate([acc_lo, acc_hi], axis=0)
    for p in range(npair):
        us = units[p * nsub:(p + 1) * nsub]
        v_pair = jnp.concatenate([un["v"] for un in us], axis=0)
        kd_pair = jnp.concatenate([un["kd"] for un in us], axis=0)
        kd_wide = jnp.concatenate([jnp.where(sub_of_row == j, kd_pair, 0.0) for j in range(nsub)], axis=1)
        upd = _mm_tn(v_pair, kd_wide)
        for j, un in enumerate(us):
            un["upd"] = jnp.where(bd, upd[:, j * LANES:(j + 1) * LANES], 0.0)

    for p in range(npair):
        sl = slice(p * LANES, (p + 1) * LANES)
        st = st_ref[p]
        outs = []
        for un in units[p * nsub:(p + 1) * nsub]:
            outs.append(un["intra"] + _mm_nt(un["qt"], st))
            st = st * un["dec"] + un["upd"]
        st_ref[p] = st
        o = jnp.concatenate(outs, axis=0)
        ms = _mm_split(o * o, seg) * (1.0 / HG_DIM)
        o_ref[0, :, sl] = o * lax.rsqrt(ms + NORM_EPS) * ng_ref[:, sl] * jax.nn.silu(gate[:, sl])


def _hgrn2(p, lb, norm_g):
    bsz, t, _ = p.shape
    lb = lb.reshape(1, HG_KDIM).astype(F32)
    ng = jnp.tile(norm_g.astype(F32), HG_HEADS).reshape(1, HG_KDIM)
    return pl.pallas_call(
        _hgrn2_kernel,
        grid=(bsz, t // HG_ROWS),
        in_specs=[pl.BlockSpec((1, HG_ROWS, HG_IN), lambda b, c: (b, c, 0)),
                  pl.BlockSpec((1, HG_KDIM), lambda b, c: (0, 0)), pl.BlockSpec((1, HG_KDIM), lambda b, c: (0, 0))],
        out_specs=pl.BlockSpec((1, HG_ROWS, HG_KDIM), lambda b, c: (b, c, 0)),
        out_shape=jax.ShapeDtypeStruct((bsz, t, HG_KDIM), F32),
        scratch_shapes=[pltpu.VMEM((HG_HEADS // 2, LANES, LANES), F32)],
        compiler_params=_cparams("arbitrary", "arbitrary"),
        name="hgrn2_mix",
    )(p, lb, ng)


def _gdn_kernel(p_ref, pg_ref, cw_ref, gb_ref, nal_ref, nalc_ref, ng_ref, o_ref, prev_ref, s_ref):
    L = CHUNK
    c = pl.program_id(1)

    @pl.when(c == 0)
    def _():
        prev_ref[...] = jnp.zeros_like(prev_ref)
        s_ref[...] = jnp.zeros_like(s_ref)

    x = p_ref[0]
    R = x.shape[0]
    nch = R // L
    xq = x[:, 0:3 * GD_DIM]
    xcat = jnp.concatenate([prev_ref[...], xq], axis=0)
    conv = xq * cw_ref[CONV_K - 1:CONV_K, :]
    for j in range(1, CONV_K):
        conv = conv + xcat[8 - j:8 - j + R, :] * cw_ref[CONV_K - 1 - j:CONV_K - j, :]
    prev_ref[...] = xq[R - 8:R, :]
    qkv = jax.nn.silu(conv)
    z = x[:, 3 * GD_DIM:]

    gt = pg_ref[0] + gb_ref[...]
    beta_all = jax.nn.sigmoid(gt)
    g_all = nal_ref[...] * jax.nn.softplus(gt)
    g_t = nalc_ref[...] * jax.nn.softplus(gt.T)
    tr, tc = _iota2((R, R))
    chunk_tril = jnp.where((tc <= tr) & ((tr // L) == (tc // L)), 1.0, 0.0).astype(BF16)
    gam_all = _cumsum_rows(chunk_tril, g_all)
    ur, uc = _iota2((R, 2 * R))
    triu2 = jnp.where(((ur // L) == (uc // (2 * L))) & ((ur % L) <= (uc % L)), 1.0, 0.0).astype(BF16)
    gam_t2 = _cumsum_cols(g_t, triu2)
    lr, lc = _iota2((L, L))
    incl = lc <= lr
    br, bc = _iota2((2 * L, 2 * L))
    bd = (br // L) == (bc // L)
    bd_strict = bd & (bc < br)
    bd_incl = bd & (bc <= br)
    lane2 = lax.broadcasted_iota(I32, (1, 2 * L), 1)
    zero = jnp.zeros((L, LANES), F32)

    units = []
    for ci in range(nch):
        rs = slice(ci * L, (ci + 1) * L)
        for h in range(GD_HEADS):
            q = qkv[rs, h * LANES:(h + 1) * LANES]
            k = qkv[rs, GD_DIM + h * LANES:GD_DIM + (h + 1) * LANES]
            v = qkv[rs, 2 * GD_DIM + h * LANES:2 * GD_DIM + (h + 1) * LANES]
            q = q * lax.rsqrt(jnp.sum(q * q, axis=-1, keepdims=True) + 1e-6) * (GD_HEAD_DIM ** -0.5)
            k = k * lax.rsqrt(jnp.sum(k * k, axis=-1, keepdims=True) + 1e-6)
            beta = beta_all[rs, h:h + 1]
            gam = gam_all[rs, GD_HEADS + h:GD_HEADS + h + 1]
            gam_row2 = gam_t2[GD_HEADS + h:GD_HEADS + h + 1, ci * 2 * L:(ci + 1) * 2 * L]
            gam_last = gam[L - 1:L, :]
            kb = k * beta
            units.append(dict(h=h, rs=rs, q=q, k=k, kb=kb, vb=v * beta, gam=gam, gam_row2=gam_row2,
                              gam_last=gam_last, kg=kb * jnp.exp(gam), qg=q * jnp.exp(gam),
                              kd=k * jnp.exp(gam_last - gam),
                              decay=jnp.exp(jnp.where(incl, gam - gam_row2[:, 0:L], NEG))))
    pairs = [(units[i], units[i + 1]) for i in range(0, len(units), 2)]
    ms = []
    for h0, h1 in pairs:
        lhs = jnp.concatenate([jnp.concatenate([h0["kb"], zero], axis=1),
                               jnp.concatenate([zero, h1["kb"]], axis=1)], axis=0)
        rhs = jnp.concatenate([jnp.concatenate([h0["k"], zero], axis=1),
                               jnp.concatenate([zero, h1["k"]], axis=1)], axis=0)
        gam_col = jnp.concatenate([h0["gam"], h1["gam"]], axis=0)
        gam_row = jnp.where(lane2 < L, h0["gam_row2"], h1["gam_row2"])
        decay2 = jnp.exp(jnp.where(bd_incl, gam_col - gam_row, NEG))
        ms.append(jnp.where(bd_strict, _mm_nt(lhs, rhs) * decay2, 0.0))
    tinvs = _tri_inv_multi(ms, 2 * L, L)
    for (h0, h1), tinv in zip(pairs, tinvs):
        rhs = jnp.concatenate([jnp.concatenate([h0["vb"], h0["kg"]], axis=1),
                               jnp.concatenate([h1["vb"], h1["kg"]], axis=1)], axis=0)
        uw = _mm(tinv, rhs)
        h0["u"], h0["w"] = uw[0:L, 0:LANES], uw[0:L, LANES:]
        h1["u"], h1["w"] = uw[L:2 * L, 0:LANES], uw[L:2 * L, LANES:]
    for hd in units:
        attn = _mm_nt(hd["q"], hd["k"]) * hd["decay"]
        uw = jnp.concatenate([hd["u"], hd["w"]], axis=1)
        auw = _mm(attn, uw)
        hd["o0"] = auw[:, 0:LANES]
        hd["qs"] = hd["qg"] - auw[:, LANES:]
        cc = _mm_tn(hd["kd"], uw)
        hd["c1"], hd["c2"] = cc[:, 0:LANES], cc[:, LANES:]

    ss = [s_ref[h] for h in range(GD_HEADS)]
    for ci in range(nch):
        hds = units[ci * GD_HEADS:(ci + 1) * GD_HEADS]
        for hd, s in zip(hds, ss):
            hd["o"] = hd["o0"] + _mm(hd["qs"], s)
        ss = [s * jnp.exp(hd["gam_last"]) + hd["c1"] - _mm(hd["c2"], s) for hd, s in zip(hds, ss)]
    for h in range(GD_HEADS):
        s_ref[h] = ss[h]
    for hd in units:
        sl = slice(hd["h"] * LANES, (hd["h"] + 1) * LANES)
        o_ref[0, hd["rs"], sl] = _rms_rows(hd["o"], ng_ref[...]) * jax.nn.silu(z[hd["rs"], sl])


def _gate_row(lo, vals):
    return jnp.zeros((1, LANES), F32).at[0, lo:lo + vals.shape[0]].set(vals.astype(F32))


def _gdn(p, pg, conv_w, a_log, dt_bias, norm_g):
    bsz, t, _ = p.shape
    gbias = _gate_row(GD_HEADS, dt_bias)
    nal = _gate_row(GD_HEADS, -jnp.exp(a_log))
    nal_col = nal.reshape(LANES, 1)
    ng = norm_g.reshape(1, GD_HEAD_DIM).astype(F32)
    full = lambda z: pl.BlockSpec(z.shape, lambda b, c: (0, 0))
    rows = GD_NCH * CHUNK
    return pl.pallas_call(
        _gdn_kernel,
        grid=(bsz, t // rows),
        in_specs=[pl.BlockSpec((1, rows, GD_MAIN), lambda b, c: (b, c, 0)),
                  pl.BlockSpec((1, rows, LANES), lambda b, c: (b, c, 0)),
                  full(conv_w), full(gbias), full(nal), full(nal_col), full(ng)],
        out_specs=pl.BlockSpec((1, rows, GD_DIM), lambda b, c: (b, c, 0)),
        out_shape=jax.ShapeDtypeStruct((bsz, t, GD_DIM), F32),
        scratch_shapes=[pltpu.VMEM((8, 3 * GD_DIM), F32), pltpu.VMEM((GD_HEADS, LANES, LANES), F32)],
        compiler_params=_cparams("arbitrary", "arbitrary"),
        name="gdn_mix",
    )(p, pg, conv_w.astype(F32), gbias, nal, nal_col, ng)


def _mlstm_kernel(p_ref, pg_ref, gb_ref, ng_ref, o_ref, c_ref, n_ref, m_ref):
    L = ML_CHUNK
    ci = pl.program_id(1)

    @pl.when(ci == 0)
    def _():
        c_ref[...] = jnp.zeros_like(c_ref)
        n_ref[...] = jnp.zeros_like(n_ref)
        m_ref[...] = jnp.zeros_like(m_ref)

    x = p_ref[0]
    R = x.shape[0]
    nch = R // L
    gt = pg_ref[0] + gb_ref[...]
    logf = jax.nn.log_sigmoid(gt)
    gt_t = gt.T
    logf_t = jax.nn.log_sigmoid(gt_t)
    tr, tc = _iota2((R, R))
    same = (tr // L) == (tc // L)
    bc_all = _cumsum_rows(jnp.where(same & (tc <= tr), 1.0, 0.0).astype(BF16), logf)
    bc_t = _cumsum_cols(logf_t, jnp.where(same & (tr <= tc), 1.0, 0.0).astype(BF16))
    lr, lc = _iota2((L, L))
    incl = lc <= lr
    i_lo, f_lo = 2 * GD_HEADS, 2 * GD_HEADS + ML_HEADS

    units = []
    m_run = [m_ref[h][0:1, 0:1] for h in range(ML_HEADS)]
    for ci in range(nch):
        rs = slice(ci * L, (ci + 1) * L)
        for h in range(ML_HEADS):
            sl = slice(h * LANES, (h + 1) * LANES)
            q = x[rs, sl]
            k = x[rs, ML_DIM + h * LANES:ML_DIM + (h + 1) * LANES] * (ML_HEAD_DIM ** -0.5)
            v = x[rs, 2 * ML_DIM + h * LANES:2 * ML_DIM + (h + 1) * LANES]
            bc = bc_all[rs, f_lo + h:f_lo + h + 1]
            bc_row = bc_t[f_lo + h:f_lo + h + 1, rs]
            ig = gt[rs, i_lo + h:i_lo + h + 1]
            ig_row = gt_t[i_lo + h:i_lo + h + 1, rs]
            m_prev = m_run[h]
            b_last = bc[L - 1:L, :]
            d_log = jnp.where(incl, bc - bc_row + ig_row, NEG)
            inter_log = bc + m_prev
            m_t = jnp.maximum(inter_log, jnp.max(d_log, axis=-1, keepdims=True))
            upd_log = b_last - bc + ig
            m_new = jnp.maximum(b_last + m_prev, jnp.max(upd_log, axis=0, keepdims=True))
            m_run[h] = m_new
            wk = jnp.exp(upd_log - m_new) * k
            units.append(dict(h=h, rs=rs, sl=sl, q=q, v=v, m_t=m_t, wk=wk, inter_w=jnp.exp(inter_log - m_t),
                              dec=jnp.exp(b_last + m_prev - m_new), sc=_mm_nt(q, k) * jnp.exp(d_log - m_t)))
    for hd in units:
        hd["scv"] = _mm(hd["sc"], hd["v"])
        hd["kv"] = _mm_tn(hd["wk"], hd["v"])
    cs = [c_ref[h] for h in range(ML_HEADS)]
    ns = [n_ref[h][0:1, :] for h in range(ML_HEADS)]
    for hd in units:
        h = hd["h"]
        q, sc, inter_w, dec = hd["q"], hd["sc"], hd["inter_w"], hd["dec"]
        num = inter_w * _mm(q, cs[h]) + hd["scv"]
        den = inter_w * jnp.sum(q * ns[h], axis=-1, keepdims=True) + jnp.sum(sc, axis=-1, keepdims=True)
        hd["hh"] = num / jnp.maximum(jnp.abs(den), jnp.exp(-hd["m_t"]))
        cs[h] = dec * cs[h] + hd["kv"]
        ns[h] = dec * ns[h] + jnp.sum(hd["wk"], axis=0, keepdims=True)
    for h in range(ML_HEADS):
        c_ref[h] = cs[h]
        n_ref[h] = jnp.broadcast_to(ns[h], (8, LANES))
        m_ref[h] = jnp.broadcast_to(m_run[h], (8, LANES))
    for hd in units:
        h, rs, sl = hd["h"], hd["rs"], hd["sl"]
        og = jax.nn.sigmoid(x[rs, 3 * ML_DIM + h * LANES:3 * ML_DIM + (h + 1) * LANES])
        o_ref[0, rs, sl] = og * _rms_rows(hd["hh"], ng_ref[:, sl])


def _mlstm(p, pg, i_bias, f_bias, norm_g):
    bsz, t, _ = p.shape
    gbias = _gate_row(2 * GD_HEADS, jnp.concatenate([i_bias, f_bias]))
    ng = norm_g.reshape(1, ML_DIM).astype(F32)
    lc = ML_NCH * ML_CHUNK
    full = lambda z: pl.BlockSpec(z.shape, lambda b, c: (0, 0))
    return pl.pallas_call(
        _mlstm_kernel,
        grid=(bsz, t // lc),
        in_specs=[pl.BlockSpec((1, lc, ML_MAIN), lambda b, c: (b, c, 0)),
                  pl.BlockSpec((1, lc, LANES), lambda b, c: (b, c, 0)), full(gbias), full(ng)],
        out_specs=pl.BlockSpec((1, lc, ML_DIM), lambda b, c: (b, c, 0)),
        out_shape=jax.ShapeDtypeStruct((bsz, t, ML_DIM), F32),
        scratch_shapes=[pltpu.VMEM((ML_HEADS, LANES, LANES), F32), pltpu.VMEM((ML_HEADS, 8, LANES), F32),
                        pltpu.VMEM((ML_HEADS, 8, LANES), F32)],
        compiler_params=_cparams("arbitrary", "arbitrary"),
        name="mlstm_mix",
    )(p, pg, gbias, ng)


def _route(h, g_ref, wh_ref, wl_ref, b_ref, off_ref):
    tm = h.shape[0]
    xn = _rms_rows(h, g_ref[...])
    xh = xn.astype(BF16)
    xl = (xn - xh.astype(F32)).astype(BF16)
    dot = lambda a, b: jnp.dot(a, b, preferred_element_type=F32)
    logits = dot(xh, wh_ref[...]) + dot(xh, wl_ref[...]) + dot(xl, wh_ref[...]) + b_ref[...]
    lt = logits.T[0:ROUTE_ROWS, :]
    row = lax.broadcasted_iota(I32, lt.shape, 0)
    big = jnp.int32(1 << 20)
    is_grp = (row >= N_EXPERTS) & (row < N_EXPERTS + N_GROUPS)
    lg = jnp.where(is_grp, lt, NEG)
    gmax = jnp.max(lg, axis=0, keepdims=True)
    p_top = 1.0 / jnp.sum(jnp.exp(lg - gmax), axis=0, keepdims=True)
    g_idx = jnp.min(jnp.where(lg == gmax, row, big), axis=0, keepdims=True) - N_EXPERTS
    valid = (row < N_EXPERTS) & ((row // EXPERTS_PER_GROUP) == g_idx)
    v1 = jnp.where(valid, lt, NEG)
    m1 = jnp.max(v1, axis=0, keepdims=True)
    i1 = jnp.min(jnp.where(v1 == m1, row, big), axis=0, keepdims=True)
    v2 = jnp.where(row == i1, NEG, v1)
    m2 = jnp.max(v2, axis=0, keepdims=True)
    i2 = jnp.min(jnp.where(v2 == m2, row, big), axis=0, keepdims=True)
    e21 = jnp.exp(m2 - m1)
    gate1 = p_top / (1.0 + e21)
    gate2 = p_top * e21 / (1.0 + e21)

    sel1 = row == i1
    sel2 = row == i2
    onehot = jnp.where(sel1 | sel2, 1.0, 0.0)
    tr, tc = _iota2((tm, tm))
    earlier = jnp.where(tr < tc, 1.0, 0.0).astype(BF16)
    off = off_ref[0:ROUTE_ROWS, 0:1]
    prefix = jnp.dot(onehot.astype(BF16), earlier, preferred_element_type=F32) + off
    rank1 = jnp.sum(jnp.where(sel1, prefix, 0.0), axis=0, keepdims=True)
    rank2 = jnp.sum(jnp.where(sel2, prefix, 0.0), axis=0, keepdims=True)
    off_ref[0:ROUTE_ROWS, :] = jnp.broadcast_to(off + jnp.sum(onehot, axis=1, keepdims=True), (ROUTE_ROWS, LANES))

    packed = jnp.concatenate([i1.astype(F32), i2.astype(F32), rank1, rank2, gate1, gate2,
                              jnp.zeros((2, tm), F32)], axis=0)
    return xn, packed


def _out_proj_route_kernel(h_ref, ya_ref, yb_ref, wa_ref, wb_ref, g_ref, wh_ref, wl_ref, b_ref,
                           o_ref, xt_ref, rg_ref, cnt_ref, dest_ref, off_ref, rt_ref):
    phase = pl.program_id(0)
    s = pl.program_id(1)
    tm = h_ref.shape[0]

    @pl.when((phase == 0) & (s == 0))
    def _():
        off_ref[...] = jnp.zeros_like(off_ref)

    @pl.when(phase == 0)
    def _():
        out = h_ref[...] + _mm(ya_ref[...], wa_ref[...]) + _mm(yb_ref[...], wb_ref[...])
        o_ref[...] = out
        xn, packed = _route(out, g_ref, wh_ref, wl_ref, b_ref, off_ref)
        _store_row_tiles(xt_ref, xn)
        rt_ref[s] = packed
        wide = jnp.concatenate([packed, jnp.zeros((LANES - 8, tm), F32)], axis=0)
        rg_ref[...] = wide.T
        cnt_ref[...] = off_ref[...]

    @pl.when(phase == 1)
    def _():
        counts = off_ref[...]
        tiles_per = jnp.floor((counts + (MOE_TM - 1)) * (1.0 / MOE_TM))
        er, ec = _iota2((LANES, LANES))
        tile_end = _mm(jnp.where(ec <= er, 1.0, 0.0), tiles_per)
        seg_start = ((tile_end - tiles_per) * MOE_TM)[0:ROUTE_ROWS, 0:1]
        blk = rt_ref[s]
        row = lax.broadcasted_iota(I32, (ROUTE_ROWS, tm), 0).astype(F32)
        pick = lambda e: jnp.sum(jnp.where(row == e, seg_start, 0.0), axis=0, keepdims=True)
        d1 = pick(blk[0:1]) + blk[2:3]
        d2 = pick(blk[1:2]) + blk[3:4]
        dest_ref[...] = jnp.concatenate([d1, d2, jnp.zeros((6, tm), F32)], axis=0).astype(I32)


def _out_proj_route(h, ya, yb, w_out, g_ffn, w_group, b_group, w_router, b_router):
    n, d = h.shape
    da, db = ya.shape[1], yb.shape[1]
    tm = min(PROJ_TM, n)
    wa = w_out[:da].astype(BF16)
    wb = w_out[da:].astype(BF16)
    pad = LANES - N_EXPERTS - N_GROUPS
    w_cat = jnp.concatenate([w_router, w_group, jnp.zeros((d, pad), F32)], axis=1)
    b_cat = jnp.concatenate([b_router, b_group, jnp.zeros((pad,), F32)]).reshape(1, LANES)
    w_hi = w_cat.astype(BF16)
    w_lo = (w_cat - w_hi.astype(F32)).astype(BF16)
    steps = n // tm
    row = lambda p, s: (s * (1 - p) + (steps - 1) * p, 0)
    fixed = lambda p, s: (0, 0)
    return pl.pallas_call(
        _out_proj_route_kernel,
        grid=(2, steps),
        in_specs=[pl.BlockSpec((tm, d), row), pl.BlockSpec((tm, da), row), pl.BlockSpec((tm, db), row),
                  pl.BlockSpec((da, d), fixed), pl.BlockSpec((db, d), fixed), pl.BlockSpec((1, d), fixed),
                  pl.BlockSpec((d, LANES), fixed), pl.BlockSpec((d, LANES), fixed), pl.BlockSpec((1, LANES), fixed)],
        out_specs=[pl.BlockSpec((tm, d), row), pl.BlockSpec((tm * ROW_TILE, LANES), row),
                   pl.BlockSpec((tm, LANES), row), pl.BlockSpec((LANES, LANES), fixed),
                   pl.BlockSpec((8, tm), lambda p, s: (0, s * p))],
        out_shape=[jax.ShapeDtypeStruct((n, d), F32), jax.ShapeDtypeStruct((n * ROW_TILE, LANES), F32),
                   jax.ShapeDtypeStruct((n, LANES), F32), jax.ShapeDtypeStruct((LANES, LANES), F32),
                   jax.ShapeDtypeStruct((8, n), I32)],
        scratch_shapes=[pltpu.VMEM((LANES, LANES), F32), pltpu.VMEM((steps, 8, tm), F32)],
        compiler_params=_cparams("arbitrary", "arbitrary"),
        name="out_proj_route",
    )(h, ya, yb, wa, wb, g_ffn.reshape(1, d), w_hi, w_lo, b_cat)


def _invert_kernel(d1_ref, d2_ref, lo_ref, hi_ref, o_ref):
    def clear(i, carry):
        o_ref[i] = 0
        return carry

    def fill(t, carry):
        o_ref[d1_ref[t]] = t
        o_ref[d2_ref[t]] = t
        return carry

    for e in range(lo_ref.shape[0]):
        lax.fori_loop(lo_ref[e], hi_ref[e], clear, 0)
    lax.fori_loop(0, d1_ref.shape[0], fill, 0, unroll=GATHER_UNROLL)


def _invert(d1, d2, pad_lo, pad_hi, n_rows):
    grid_spec = pltpu.PrefetchScalarGridSpec(
        num_scalar_prefetch=4, grid=(1,), in_specs=[],
        out_specs=pl.BlockSpec(memory_space=pltpu.SMEM))
    return pl.pallas_call(
        _invert_kernel,
        grid_spec=grid_spec,
        out_shape=jax.ShapeDtypeStruct((n_rows,), I32),
        compiler_params=_cparams("arbitrary"),
        name="moe_invert",
    )(d1, d2, pad_lo, pad_hi)


def _gather_rows(src_hbm, idx_ref, base, dst, sem):
    def body(j, carry):
        r = pl.multiple_of(idx_ref[base + j] * ROW_TILE, ROW_TILE)
        pltpu.make_async_copy(src_hbm.at[pl.ds(r, ROW_TILE), :],
                              dst.at[pl.ds(pl.multiple_of(j * ROW_TILE, ROW_TILE), ROW_TILE), :],
                              sem).start()
        return carry

    lax.fori_loop(0, dst.shape[0] // ROW_TILE, body, 0, unroll=GATHER_UNROLL)


def _wait_rows(src_hbm, dst, sem):
    pltpu.make_async_copy(src_hbm.at[pl.ds(0, dst.shape[0]), :], dst, sem).wait()


def _expert_kernel(te_ref, nt_ref, src_ref, x_hbm, wg_ref, wu_ref, wd_ref, o_ref, xbuf, wgb, wub, wdb, sem):
    tm = xbuf.shape[1] // ROW_TILE
    i = pl.program_id(0)
    n_used = nt_ref[0]
    slot = i % EXPERT_BUFS

    @pl.when(i == 0)
    def _():
        for t in range(EXPERT_BUFS - 1):
            @pl.when(t < n_used)
            def _():
                _gather_rows(x_hbm, src_ref, t * tm, xbuf.at[t], sem.at[t])

    ahead = i + EXPERT_BUFS - 1

    @pl.when(ahead < n_used)
    def _():
        _gather_rows(x_hbm, src_ref, ahead * tm, xbuf.at[ahead % EXPERT_BUFS], sem.at[ahead % EXPERT_BUFS])

    @pl.when((i == 0) | (te_ref[i] != te_ref[jnp.maximum(i - 1, 0)]))
    def _():
        wgb[...] = wg_ref[0].astype(BF16)
        wub[...] = wu_ref[0].astype(BF16)
        wdb[...] = wd_ref[0].astype(BF16)

    @pl.when(i < n_used)
    def _():
        _wait_rows(x_hbm, xbuf.at[slot], sem.at[slot])
        xn = _load_row_tiles(xbuf.at[slot], tm).astype(BF16)
        gate = jnp.dot(xn, wgb[...], preferred_element_type=F32)
        up = jnp.dot(xn, wub[...], preferred_element_type=F32)
        hid = (jax.nn.silu(gate) * up).astype(BF16)
        _store_row_tiles(o_ref, jnp.dot(hid, wdb[...], preferred_element_type=F32))

    @pl.when(i >= n_used)
    def _():
        o_ref[...] = jnp.zeros_like(o_ref)


def _experts(xn_tiles, tile_expert, n_used, src, w_gate, w_up, w_down):
    d = D_MODEL
    n_tiles = tile_expert.shape[0]
    tm = src.shape[0] // n_tiles
    de = w_gate.shape[-1]
    grid_spec = pltpu.PrefetchScalarGridSpec(
        num_scalar_prefetch=3,
        grid=(n_tiles,),
        in_specs=[pl.BlockSpec(memory_space=pl.ANY),
                  pl.BlockSpec((1, d, de), lambda i, te, nt, s: (te[i], 0, 0)),
                  pl.BlockSpec((1, d, de), lambda i, te, nt, s: (te[i], 0, 0)),
                  pl.BlockSpec((1, de, d), lambda i, te, nt, s: (te[i], 0, 0))],
        out_specs=pl.BlockSpec((tm * ROW_TILE, LANES), lambda i, te, nt, s: (i, 0)),
        scratch_shapes=[pltpu.VMEM((EXPERT_BUFS, tm * ROW_TILE, LANES), F32), pltpu.VMEM((d, de), BF16),
                        pltpu.VMEM((d, de), BF16), pltpu.VMEM((de, d), BF16),
                        pltpu.SemaphoreType.DMA((EXPERT_BUFS,))],
    )
    return pl.pallas_call(
        _expert_kernel,
        grid_spec=grid_spec,
        out_shape=jax.ShapeDtypeStruct((n_tiles * tm * ROW_TILE, LANES), F32),
        compiler_params=_cparams("arbitrary"),
        name="moe_experts",
    )(tile_expert, n_used, src, xn_tiles, w_gate, w_up, w_down)


def _combine_kernel(d1_ref, d2_ref, ys_hbm, h_ref, rg_ref, g_ref, *refs, n_proj, final_norm):
    w_refs, o_ref, p_refs = refs[:n_proj], refs[n_proj], refs[n_proj + 1:2 * n_proj + 1]
    y1, y2, sem = refs[2 * n_proj + 1:]
    tm = h_ref.shape[0]
    i = pl.program_id(0)
    slot = i % 2

    def gather(tile, s):
        _gather_rows(ys_hbm, d1_ref, tile * tm, y1.at[s], sem.at[0, s])
        _gather_rows(ys_hbm, d2_ref, tile * tm, y2.at[s], sem.at[1, s])

    @pl.when(i == 0)
    def _():
        gather(0, 0)

    @pl.when(i + 1 < pl.num_programs(0))
    def _():
        gather(i + 1, 1 - slot)

    _wait_rows(ys_hbm, y1.at[slot], sem.at[0, slot])
    _wait_rows(ys_hbm, y2.at[slot], sem.at[1, slot])
    rg = rg_ref[...]
    out = (h_ref[...] + rg[:, 4:5] * _load_row_tiles(y1.at[slot], tm)
           + rg[:, 5:6] * _load_row_tiles(y2.at[slot], tm))
    if final_norm:
        out = _rms_rows(out, g_ref[...])
    o_ref[...] = out
    if n_proj:
        y = _rms_rows(out, g_ref[...]).astype(BF16)
        for w_ref, p_ref in zip(w_refs, p_refs):
            p_ref[...] = jnp.dot(y, w_ref[...], preferred_element_type=F32)


def _combine(h, ys, d1, d2, rg, g, ws=(), final_norm=False):
    n, d = h.shape
    tm = min(PROJ_TM, n)
    row = lambda i, a, b: (i, 0)
    fixed = lambda i, a, b: (0, 0)
    grid_spec = pltpu.PrefetchScalarGridSpec(
        num_scalar_prefetch=2,
        grid=(n // tm,),
        in_specs=[pl.BlockSpec(memory_space=pl.ANY), pl.BlockSpec((tm, d), row), pl.BlockSpec((tm, LANES), row),
                  pl.BlockSpec((1, d), fixed)] + [pl.BlockSpec(w.shape, fixed) for w in ws],
        out_specs=[pl.BlockSpec((tm, d), row)] + [pl.BlockSpec((tm, w.shape[1]), row) for w in ws],
        scratch_shapes=[pltpu.VMEM((2, tm * ROW_TILE, LANES), F32), pltpu.VMEM((2, tm * ROW_TILE, LANES), F32),
                        pltpu.SemaphoreType.DMA((2, 2))],
    )
    return pl.pallas_call(
        functools.partial(_combine_kernel, n_proj=len(ws), final_norm=final_norm),
        grid_spec=grid_spec,
        out_shape=[jax.ShapeDtypeStruct((n, d), F32)] + [jax.ShapeDtypeStruct((n, w.shape[1]), F32) for w in ws],
        compiler_params=_cparams("arbitrary"),
        name="moe_combine",
    )(d1, d2, ys, h, rg, g.reshape(1, d), *ws)


def _hmoe_residual(h, xn_tiles, rg, cnt, dest, layer, w_gate, w_up, w_down, g, ws=(), final_norm=False):
    n, d = h.shape
    stack = lambda w: w.reshape((-1,) + w.shape[2:])

    counts = cnt[:N_EXPERTS, 0].astype(I32)
    n_tiles = (2 * n) // MOE_TM + N_EXPERTS
    tile_end = jnp.cumsum((counts + MOE_TM - 1) // MOE_TM)
    d1, d2 = dest[0], dest[1]
    tile_ids = jnp.arange(n_tiles, dtype=I32)
    tile_expert = jnp.minimum(jnp.sum((tile_end[None, :] <= tile_ids[:, None]).astype(I32), axis=1), N_EXPERTS - 1)
    tile_expert = tile_expert + layer * N_EXPERTS
    seg_end = tile_end * MOE_TM
    seg_start = seg_end - ((counts + MOE_TM - 1) // MOE_TM) * MOE_TM
    pad_lo = jnp.concatenate([seg_start + counts, seg_end[N_EXPERTS - 1:]]).astype(I32)
    pad_hi = jnp.concatenate([seg_end, jnp.full((1,), n_tiles * MOE_TM, I32)]).astype(I32)
    src = _invert(d1, d2, pad_lo, pad_hi, n_tiles * MOE_TM)

    n_used = tile_end[N_EXPERTS - 1:].astype(I32)
    ys = _experts(xn_tiles, tile_expert, n_used, src, stack(w_gate), stack(w_up), stack(w_down))
    return _combine(h, ys, d1, d2, rg, g, ws, final_norm)


def kernel(x, norm_mix, norm_ffn, norm_final, ev_w_in, ev_mu, rw_w0, rw_w2, rw_a0, rw_a2, rw_g2, rw_k_k, rw_k_a, rw_r_k, rw_ln_w, rw_ln_b, hg_lb_logits, hg_norm, ev_w_out, od_w_in, gd_conv, gd_a_log, gd_dt_bias, gd_norm, ml_i_bias, ml_f_bias, ml_norm, od_w_out, moe_w_group, moe_b_group, moe_w_router, moe_b_router, moe_w_gate, moe_w_up, moe_w_down):
    bsz, t, d = x.shape
    n = bsz * t
    depth = norm_mix.shape[0]
    lb_table = jnp.cumsum(jax.nn.softmax(hg_lb_logits.astype(F32), axis=0), axis=0)
    def in_proj_weights(layer):
        j = layer // 2
        if layer % 2 == 0:
            return _pack_weights(ev_w_in[j], [(0, RW_IN), (RW_IN, RW_IN + HG_IN)])
        return _pack_weights(od_w_in[j], [(0, GD_MAIN), (GD_IN, GD_IN + ML_MAIN)],
                             gate_ranges=[(GD_MAIN, GD_IN), (GD_IN + ML_MAIN, GD_IN + ML_MAIN + 2 * ML_HEADS)])

    h = x.reshape(n, d)
    proj = _rms_proj(h, norm_mix[0], in_proj_weights(0))
    for layer in range(depth):
        j = layer // 2
        if layer % 2 == 0:
            p_rw, p_hg = proj
            ya = _rwkv7(p_rw.reshape(bsz, t, RW_IN), ev_mu[j], rw_w0[j], rw_w2[j], rw_a0[j], rw_a2[j], rw_g2[j],
                        rw_k_k[j], rw_k_a[j], rw_r_k[j], rw_ln_w[j], rw_ln_b[j])
            yb = _hgrn2(p_hg.reshape(bsz, t, HG_IN), lb_table[j], hg_norm[j])
            mix_a, mix_b, w_out = ya.reshape(n, RW_DIM), yb.reshape(n, HG_KDIM), ev_w_out[j]
        else:
            p_gd, p_ml, p_gt = proj
            p_gt = p_gt.reshape(bsz, t, LANES)
            yc = _gdn(p_gd.reshape(bsz, t, GD_MAIN), p_gt, gd_conv[j], gd_a_log[j], gd_dt_bias[j], gd_norm[j])
            yd = _mlstm(p_ml.reshape(bsz, t, ML_MAIN), p_gt, ml_i_bias[j], ml_f_bias[j], ml_norm[j])
            mix_a, mix_b, w_out = yc.reshape(n, GD_DIM), yd.reshape(n, ML_DIM), od_w_out[j]
        h, xn_tiles, rg, cnt, dest = _out_proj_route(h, mix_a, mix_b, w_out, norm_ffn[layer], moe_w_group[layer],
                                                     moe_b_group[layer], moe_w_router[layer], moe_b_router[layer])
        if layer == depth - 1:
            (h,) = _hmoe_residual(h, xn_tiles, rg, cnt, dest, layer, moe_w_gate, moe_w_up, moe_w_down,
                                  norm_final, final_norm=True)
        else:
            h, *proj = _hmoe_residual(h, xn_tiles, rg, cnt, dest, layer, moe_w_gate, moe_w_up, moe_w_down,
                                      norm_mix[layer + 1], ws=in_proj_weights(layer + 1))
    return h.reshape(bsz, t, d)
```

```python
import functools
import math

import jax
import jax.numpy as jnp
from jax import lax
from jax.experimental import pallas as pl
from jax.experimental.pallas import tpu as pltpu

F32 = jnp.float32
BF16 = jnp.bfloat16
I32 = jnp.int32
HIGHEST = lax.Precision.HIGHEST

D_MODEL = 1024
NORM_EPS = 1e-6
RW_HEADS, RW_HEAD_DIM = 8, 64
RW_DIM = RW_HEADS * RW_HEAD_DIM
R_DECAY, R_AAA, R_GATE = 64, 64, 128
RW_IN = 3 * RW_DIM + R_DECAY + R_AAA + R_GATE
RW_LN_EPS = 64e-5
HG_HEADS, HG_DIM = 8, 64
HG_KDIM = HG_HEADS * HG_DIM
HG_IN = 4 * HG_KDIM
GD_HEADS, GD_HEAD_DIM = 4, 128
GD_DIM = GD_HEADS * GD_HEAD_DIM
CONV_K = 4
GD_MAIN = 4 * GD_DIM
GD_IN = GD_MAIN + 2 * GD_HEADS
ML_HEADS, ML_HEAD_DIM = 4, 128
ML_DIM = ML_HEADS * ML_HEAD_DIM
ML_MAIN = 4 * ML_DIM
N_GROUPS, EXPERTS_PER_GROUP = 4, 8
N_EXPERTS = N_GROUPS * EXPERTS_PER_GROUP
D_EXPERT = 256

LANES = 128
VMEM_LIMIT_BYTES = 48 * 1024 * 1024

PROJ_TM = 256
PACK_TM = 128
CHUNK = 64
SUB = 16
RW_NCH = 4
GD_NCH = 8
HG_ROWS = 128
ML_CHUNK = 128
ML_NCH = 2
MOE_TM = 256
ROUTE_ROWS = 40
EXPERT_BUFS = 3
GATHER_UNROLL = 8
NEG = -1e30


def _cparams(*sem):
    return pltpu.CompilerParams(dimension_semantics=sem, vmem_limit_bytes=VMEM_LIMIT_BYTES)


def _mm(a, b):
    return jnp.dot(a.astype(BF16), b.astype(BF16), preferred_element_type=F32)


def _mm_nt(a, b):
    return lax.dot_general(a.astype(BF16), b.astype(BF16), (((1,), (1,)), ((), ())), preferred_element_type=F32)


def _mm_tn(a, b):
    return lax.dot_general(a.astype(BF16), b.astype(BF16), (((0,), (0,)), ((), ())), preferred_element_type=F32)


def _mm_hi(a, b):
    return jnp.dot(a, b, precision=HIGHEST, preferred_element_type=F32)


def _mm_split(x, ones_bf16):
    hi = x.astype(BF16)
    lo = (x - hi.astype(F32)).astype(BF16)
    return (jnp.dot(hi, ones_bf16, preferred_element_type=F32) + jnp.dot(lo, ones_bf16, preferred_element_type=F32))


def _iota2(shape):
    return lax.broadcasted_iota(I32, shape, 0), lax.broadcasted_iota(I32, shape, 1)


def _rms_rows(x, g, eps=NORM_EPS):
    return x * lax.rsqrt(jnp.mean(x * x, axis=-1, keepdims=True) + eps) * g


def _seg_ones(width, seg):
    r, c = _iota2((width, width))
    return jnp.where((r // seg) == (c // seg), 1.0, 0.0).astype(BF16)


def _split3(x):
    x1 = x.astype(BF16)
    r1 = x - x1.astype(F32)
    x2 = r1.astype(BF16)
    return x1, x2, (r1 - x2.astype(F32)).astype(BF16)


def _cumsum_rows(tri_bf16, x):
    return sum(jnp.dot(tri_bf16, t, preferred_element_type=F32) for t in _split3(x))


def _cumsum_cols(x, tri_bf16):
    return sum(jnp.dot(t, tri_bf16, preferred_element_type=F32) for t in _split3(x))


def _mm3(a, b):
    ah = a.astype(BF16)
    al = (a - ah.astype(F32)).astype(BF16)
    bh = b.astype(BF16)
    bl = (b - bh.astype(F32)).astype(BF16)
    dot = lambda x, y: jnp.dot(x, y, preferred_element_type=F32)
    return dot(ah, bh) + dot(ah, bl) + dot(al, bh)


def _tri_inv_multi(ms, n, chain):
    assert chain // SUB <= 4
    r, c = _iota2((n, n))
    same = (r // SUB) == (c // SUB)
    eye = jnp.where(r == c, 1.0, 0.0).astype(F32)
    ds = [jnp.where(same, m, 0.0) for m in ms]
    offs = [m - d for m, d in zip(ms, ds)]
    xs = [eye - d for d in ds]
    ps = ds
    for _ in range(3):
        ps = [_mm(p, p) for p in ps]
        xs = [x + _mm(x, p) for x, p in zip(xs, ps)]
    es = [_mm(x, o) for x, o in zip(xs, offs)]
    imes = [eye - e for e in es]
    e2s = [_mm(e, e) for e in es]
    ys = [i + _mm(i, e2) for i, e2 in zip(imes, e2s)]
    xs = [_mm(y, x) for y, x in zip(ys, xs)]
    res = [eye - x - _mm3(m, x) for m, x in zip(ms, xs)]
    return [x + _mm(x, rr) for x, rr in zip(xs, res)]


def _pack_weights_kernel(w_ref, *o_refs, ranges, gate_ranges):
    w = w_ref[...]
    for (lo, hi), o_ref in zip(ranges, o_refs):
        o_ref[...] = w[:, lo:hi].astype(BF16)
    if gate_ranges:
        cols = [w[:, lo:hi] for lo, hi in gate_ranges]
        used = sum(hi - lo for lo, hi in gate_ranges)
        cols.append(jnp.zeros((w.shape[0], LANES - used), F32))
        o_refs[-1][...] = jnp.concatenate(cols, axis=1).astype(BF16)


def _pack_weights(w, ranges, gate_ranges=()):
    rows, cols = w.shape
    tm = PACK_TM
    widths = [hi - lo for lo, hi in ranges] + ([LANES] if gate_ranges else [])
    return pl.pallas_call(
        functools.partial(_pack_weights_kernel, ranges=tuple(ranges), gate_ranges=tuple(gate_ranges)),
        grid=(rows // tm,),
        in_specs=[pl.BlockSpec((tm, cols), lambda i: (i, 0))],
        out_specs=[pl.BlockSpec((tm, wd), lambda i: (i, 0)) for wd in widths],
        out_shape=[jax.ShapeDtypeStruct((rows, wd), BF16) for wd in widths],
        compiler_params=_cparams("parallel"),
        name="pack_weights",
    )(w)


def _rms_proj_kernel(x_ref, g_ref, *refs, n_out):
    y = _rms_rows(x_ref[...], g_ref[...]).astype(BF16)
    for w_ref, o_ref in zip(refs[:n_out], refs[n_out:]):
        o_ref[...] = jnp.dot(y, w_ref[...], preferred_element_type=F32)


def _rms_proj(x, g, ws):
    n, d = x.shape
    tm = min(PROJ_TM, n)
    in_specs = [pl.BlockSpec((tm, d), lambda i: (i, 0)), pl.BlockSpec((1, d), lambda i: (0, 0))]
    in_specs += [pl.BlockSpec(w.shape, lambda i: (0, 0)) for w in ws]
    return pl.pallas_call(
        functools.partial(_rms_proj_kernel, n_out=len(ws)),
        grid=(n // tm,),
        in_specs=in_specs,
        out_specs=[pl.BlockSpec((tm, w.shape[1]), lambda i: (i, 0)) for w in ws],
        out_shape=[jax.ShapeDtypeStruct((n, w.shape[1]), F32) for w in ws],
        compiler_params=_cparams("parallel"),
        name="rms_proj",
    )(x, g.reshape(1, d), *ws)


ROW_TILE = D_MODEL // LANES


def _store_row_tiles(ref, x):
    for j in range(ROW_TILE):
        ref[pl.ds(j, x.shape[0], stride=ROW_TILE), :] = x[:, j * LANES:(j + 1) * LANES]


def _load_row_tiles(ref, rows):
    return jnp.concatenate([ref[pl.ds(j, rows, stride=ROW_TILE), :] for j in range(ROW_TILE)], axis=1)


def _rwkv7_kernel(p_ref, mu_ref, w0_ref, w2_ref, a0_ref, a2_ref, g2_ref, kk_ref, ka_ref, rk_ref,
                  lnw_ref, lnb_ref, o_ref, prev_ref, zt_ref):
    L = CHUNK
    npair = RW_HEADS // 2
    c = pl.program_id(1)

    @pl.when(c == 0)
    def _():
        prev_ref[...] = jnp.zeros_like(prev_ref)
        zt_ref[...] = jnp.zeros_like(zt_ref)

    x = p_ref[0]
    R = x.shape[0]
    nch = R // L
    row = lax.broadcasted_iota(I32, x.shape, 0)
    xs = jnp.where(row == 0, prev_ref[7:8, :], pltpu.roll(x, 1, 0))
    prev_ref[...] = x[R - 8:R, :]
    pm = x + mu_ref[...] * (xs - x)
    r_all = pm[:, 0:RW_DIM]
    k_all = pm[:, RW_DIM:2 * RW_DIM]
    v_all = pm[:, 2 * RW_DIM:3 * RW_DIM]
    wa = pm[:, 3 * RW_DIM:3 * RW_DIM + LANES]
    gl = pm[:, 3 * RW_DIM + LANES:]
    wlog = -jax.nn.softplus(-(w0_ref[...] + _mm(jnp.tanh(wa), w2_ref[...]))) - 0.5
    ld = -jnp.exp(wlog)
    a_all = jax.nn.sigmoid(a0_ref[...] + _mm(wa, a2_ref[...]))
    g_all = _mm(jax.nn.sigmoid(gl), g2_ref[...])

    tr, tc = _iota2((R, R))
    chunk_tril = jnp.where((tc <= tr) & ((tr // L) == (tc // L)), 1.0, 0.0).astype(BF16)
    cs_all = _cumsum_rows(chunk_tril, ld)
    seg = _seg_ones(LANES, RW_HEAD_DIM)
    lane = lax.broadcasted_iota(I32, (L, LANES), 1)
    hm = (lane < RW_HEAD_DIM, lane >= RW_HEAD_DIM)
    br, bc = _iota2((2 * L, 2 * L))
    bd = (br // L) == (bc // L)
    bd_strict = bd & (bc < br)
    bd_incl = bd & (bc <= br)
    fold = lambda z: z[0:L] + z[L:2 * L]
    both = lambda z: jnp.concatenate([jnp.where(hm[0], z, 0.0), jnp.where(hm[1], z, 0.0)], axis=0)

    units = []
    for ci in range(nch):
        rs = slice(ci * L, (ci + 1) * L)
        for p in range(npair):
            sl = slice(p * LANES, (p + 1) * LANES)
            r, k, v, a = r_all[rs, sl], k_all[rs, sl], v_all[rs, sl], a_all[rs, sl]
            cs, ldp = cs_all[rs, sl], ld[rs, sl]
            kkr = k * kk_ref[:, sl]
            kk = kkr * lax.rsqrt(_mm_split(kkr * kkr, seg) + 1e-6)
            k2 = k * (1.0 + (a - 1.0) * ka_ref[:, sl])
            b = kk * a
            cs_last = cs[L - 1:L, :]
            e_neg = jnp.exp(-cs)
            e_rem = jnp.exp(cs_last - cs)
            bhat = b * e_neg
            khat = k2 * e_neg
            units.append(dict(p=p, rs=rs, sl=sl, r=r, v=v, k2=k2, rhat=r * jnp.exp(cs), gam_last=jnp.exp(cs_last),
                              btil=b * e_rem, ktil=k2 * e_rem, a2=both(kk * jnp.exp(cs - ldp)), v2=both(v),
                              rhs4=jnp.concatenate([bhat, bhat, khat, khat], axis=0)))
    for q in units:
        lhs = jnp.concatenate([q["a2"], both(q["rhat"])], axis=0)
        q["g"] = _mm_nt(lhs, q["rhs4"])
    tinvs = _tri_inv_multi([jnp.where(bd_strict, q["g"][0:2 * L, 0:2 * L], 0.0) for q in units], 2 * L, L)
    for q, tinv in zip(units, tinvs):
        q["tinv"] = tinv
        q["x2"] = _mm(jnp.where(bd_strict, q["g"][0:2 * L, 2 * L:4 * L], 0.0), q["v2"])
    for q in units:
        uw = _mm(q["tinv"], jnp.concatenate([q["x2"], q["a2"]], axis=1))
        q["u0"] = -fold(uw[:, 0:LANES])
        q["w"] = fold(uw[:, LANES:])
        q["y0"] = fold(_mm(jnp.where(bd_incl, q["g"][2 * L:4 * L, 2 * L:4 * L], 0.0), q["v2"]))
    for q in units:
        rb = jnp.where(bd_incl, q["g"][2 * L:4 * L, 0:2 * L], 0.0)
        ruw = _mm(rb, jnp.concatenate([both(q["u0"]), both(q["w"])], axis=1))
        q["yc"] = q["y0"] + fold(ruw[:, 0:LANES])
        q["ry"] = q["rhat"] - fold(ruw[:, LANES:])
        q["c1"] = _mm_tn(jnp.concatenate([q["u0"], q["v"]], axis=0), jnp.concatenate([q["btil"], q["ktil"]], axis=0))
        q["c2"] = _mm_tn(q["w"], q["btil"])

    hr, hc = _iota2((LANES, LANES))
    head_bd = (hr // RW_HEAD_DIM) == (hc // RW_HEAD_DIM)
    zts = [zt_ref[p] for p in range(npair)]
    for ci in range(nch):
        qs = units[ci * npair:(ci + 1) * npair]
        for q, zt in zip(qs, zts):
            q["y"] = q["yc"] + _mm_nt(q["ry"], zt)
        zts = [zt * q["gam_last"] + jnp.where(head_bd, q["c1"] - _mm(zt, q["c2"]), 0.0) for q, zt in zip(qs, zts)]
    for p in range(npair):
        zt_ref[p] = zts[p]

    for q in units:
        sl, rs, y = q["sl"], q["rs"], q["y"]
        mean = _mm_split(y, seg) * (1.0 / RW_HEAD_DIM)
        yc = y - mean
        var = _mm_split(yc * yc, seg) * (1.0 / RW_HEAD_DIM)
        yn = yc * lax.rsqrt(var + RW_LN_EPS) * lnw_ref[:, sl] + lnb_ref[:, sl]
        bonus = _mm_split(q["r"] * q["k2"] * rk_ref[:, sl], seg) * q["v"]
        o_ref[0, rs, sl] = (yn + bonus) * g_all[rs, sl]


def _rwkv7(p, mu, w0, w2, a0, a2, g2, k_k, k_a, r_k, ln_w, ln_b):
    bsz, t, _ = p.shape
    row = lambda z: z.reshape(1, -1).astype(F32)
    w2p = jnp.concatenate([w2, jnp.zeros_like(w2)], axis=0).astype(BF16)
    a2p = jnp.concatenate([jnp.zeros_like(a2), a2], axis=0).astype(BF16)
    params = [row(mu), row(w0), w2p, row(a0), a2p, g2.astype(BF16), row(k_k), row(k_a), row(r_k), row(ln_w), row(ln_b)]
    full = lambda z: pl.BlockSpec(z.shape, lambda b, c: (0, 0))
    rows = RW_NCH * CHUNK
    return pl.pallas_call(
        _rwkv7_kernel,
        grid=(bsz, t // rows),
        in_specs=[pl.BlockSpec((1, rows, RW_IN), lambda b, c: (b, c, 0))] + [full(z) for z in params],
        out_specs=pl.BlockSpec((1, rows, RW_DIM), lambda b, c: (b, c, 0)),
        out_shape=jax.ShapeDtypeStruct((bsz, t, RW_DIM), F32),
        scratch_shapes=[pltpu.VMEM((8, RW_IN), F32), pltpu.VMEM((RW_HEADS // 2, LANES, LANES), F32)],
        compiler_params=_cparams("arbitrary", "arbitrary"),
        name="rwkv7_mix",
    )(p, *params)


def _hgrn2_kernel(p_ref, lb_ref, ng_ref, o_ref, st_ref):
    L = HG_ROWS
    c = pl.program_id(1)

    @pl.when(c == 0)
    def _():
        st_ref[...] = jnp.zeros_like(st_ref)

    x = p_ref[0]
    lb = lb_ref[...]
    q_all = jax.nn.silu(x[:, 0:HG_KDIM])
    fg = lb + (1.0 - lb) * jax.nn.sigmoid(x[:, HG_KDIM:2 * HG_KDIM])
    k_all = 1.0 - fg
    logf = jnp.log(fg)
    v_all = x[:, 2 * HG_KDIM:3 * HG_KDIM]
    gate = x[:, 3 * HG_KDIM:]
    tr, tc = _iota2((L, L))
    blk_tril = jnp.where((tc <= tr) & ((tr // SUB) == (tc // SUB)), 1.0, 0.0).astype(BF16)
    bc_all = _cumsum_rows(blk_tril, logf)
    seg = _seg_ones(LANES, HG_DIM)
    br, bcc = _iota2((LANES, LANES))
    bd = (br // HG_DIM) == (bcc // HG_DIM)
    half = SUB // 2
    t_lo = lax.broadcasted_iota(I32, (SUB, LANES), 0)
    t_hi = lax.broadcasted_iota(I32, (half, LANES), 0) + half

    nsub = L // SUB
    npair = HG_HEADS // 2
    rows_per_unit = half * SUB + half * half
    sub_of_row = lax.broadcasted_iota(I32, (L, LANES), 0) // SUB

    units = []
    parts = []
    for p in range(npair):
        sl = slice(p * LANES, (p + 1) * LANES)
        for j in range(nsub):
            rs = slice(j * SUB, (j + 1) * SUB)
            q, k, v, bc = q_all[rs, sl], k_all[rs, sl], v_all[rs, sl], bc_all[rs, sl]
            q_hi, bc_hi = q[half:], bc[half:]
            for s in range(SUB):
                if s < half:
                    diff = jnp.where(t_lo >= s, bc - bc[s:s + 1, :], NEG)
                    parts.append(jnp.exp(diff) * q * k[s:s + 1, :])
                else:
                    diff = jnp.where(t_hi >= s, bc_hi - bc[s:s + 1, :], NEG)
                    parts.append(jnp.exp(diff) * q_hi * k[s:s + 1, :])
            bend = bc[SUB - 1:SUB, :]
            units.append(dict(v=v, qt=q * jnp.exp(bc), dec=jnp.exp(bend), kd=k * jnp.exp(bend - bc)))
    score_all = _mm(jnp.concatenate(parts, axis=0), seg)
    for ui, un in enumerate(units):
        score = score_all[ui * rows_per_unit:(ui + 1) * rows_per_unit]
        v = un["v"]
        acc_lo = jnp.zeros((half, LANES), F32)
        acc_hi = jnp.zeros((half, LANES), F32)
        off = 0
        for s in range(SUB):
            vs = v[s:s + 1, :]
            if s < half:
                acc_lo = acc_lo + score[off:off + half] * vs
                acc_hi = acc_hi + score[off + half:off + SUB] * vs
                off += SUB
            else:
                acc_hi = acc_hi + score[off:off + half] * vs
                off += half
        un["intra"] = jnp.concatenate([acc_lo, acc_hi], axis=0)
    for p in range(npair):
        us = units[p * nsub:(p + 1) * nsub]
        v_pair = jnp.concatenate([un["v"] for un in us], axis=0)
        kd_pair = jnp.concatenate([un["kd"] for un in us], axis=0)
        kd_wide = jnp.concatenate([jnp.where(sub_of_row == j, kd_pair, 0.0) for j in range(nsub)], axis=1)
        upd = _mm_tn(v_pair, kd_wide)
        for j, un in enumerate(us):
            un["upd"] = jnp.where(bd, upd[:, j * LANES:(j + 1) * LANES], 0.0)

    for p in range(npair):
        sl = slice(p * LANES, (p + 1) * LANES)
        st = st_ref[p]
        outs = []
        for un in units[p * nsub:(p + 1) * nsub]:
            outs.append(un["intra"] + _mm_nt(un["qt"], st))
            st = st * un["dec"] + un["upd"]
        st_ref[p] = st
        o = jnp.concatenate(outs, axis=0)
        ms = _mm_split(o * o, seg) * (1.0 / HG_DIM)
        o_ref[0, :, sl] = o * lax.rsqrt(ms + NORM_EPS) * ng_ref[:, sl] * jax.nn.silu(gate[:, sl])


def _hgrn2(p, lb, norm_g):
    bsz, t, _ = p.shape
    lb = lb.reshape(1, HG_KDIM).astype(F32)
    ng = jnp.tile(norm_g.astype(F32), HG_HEADS).reshape(1, HG_KDIM)
    return pl.pallas_call(
        _hgrn2_kernel,
        grid=(bsz, t // HG_ROWS),
        in_specs=[pl.BlockSpec((1, HG_ROWS, HG_IN), lambda b, c: (b, c, 0)),
                  pl.BlockSpec((1, HG_KDIM), lambda b, c: (0, 0)), pl.BlockSpec((1, HG_KDIM), lambda b, c: (0, 0))],
        out_specs=pl.BlockSpec((1, HG_ROWS, HG_KDIM), lambda b, c: (b, c, 0)),
        out_shape=jax.ShapeDtypeStruct((bsz, t, HG_KDIM), F32),
        scratch_shapes=[pltpu.VMEM((HG_HEADS // 2, LANES, LANES), F32)],
        compiler_params=_cparams("arbitrary", "arbitrary"),
        name="hgrn2_mix",
    )(p, lb, ng)


def _gdn_kernel(p_ref, pg_ref, cw_ref, gb_ref, nal_ref, nalc_ref, ng_ref, o_ref, prev_ref, s_ref):
    L = CHUNK
    c = pl.program_id(1)

    @pl.when(c == 0)
    def _():
        prev_ref[...] = jnp.zeros_like(prev_ref)
        s_ref[...] = jnp.zeros_like(s_ref)

    x = p_ref[0]
    R = x.shape[0]
    nch = R // L
    xq = x[:, 0:3 * GD_DIM]
    xcat = jnp.concatenate([prev_ref[...], xq], axis=0)
    conv = xq * cw_ref[CONV_K - 1:CONV_K, :]
    for j in range(1, CONV_K):
        conv = conv + xcat[8 - j:8 - j + R, :] * cw_ref[CONV_K - 1 - j:CONV_K - j, :]
    prev_ref[...] = xq[R - 8:R, :]
    qkv = jax.nn.silu(conv)
    z = x[:, 3 * GD_DIM:]

    gt = pg_ref[0] + gb_ref[...]
    beta_all = jax.nn.sigmoid(gt)
    g_all = nal_ref[...] * jax.nn.softplus(gt)
    g_t = nalc_ref[...] * jax.nn.softplus(gt.T)
    tr, tc = _iota2((R, R))
    chunk_tril = jnp.where((tc <= tr) & ((tr // L) == (tc // L)), 1.0, 0.0).astype(BF16)
    gam_all = _cumsum_rows(chunk_tril, g_all)
    ur, uc = _iota2((R, 2 * R))
    triu2 = jnp.where(((ur // L) == (uc // (2 * L))) & ((ur % L) <= (uc % L)), 1.0, 0.0).astype(BF16)
    gam_t2 = _cumsum_cols(g_t, triu2)
    lr, lc = _iota2((L, L))
    incl = lc <= lr
    br, bc = _iota2((2 * L, 2 * L))
    bd = (br // L) == (bc // L)
    bd_strict = bd & (bc < br)
    bd_incl = bd & (bc <= br)
    lane2 = lax.broadcasted_iota(I32, (1, 2 * L), 1)
    zero = jnp.zeros((L, LANES), F32)

    units = []
    for ci in range(nch):
        rs = slice(ci * L, (ci + 1) * L)
        for h in range(GD_HEADS):
            q = qkv[rs, h * LANES:(h + 1) * LANES]
            k = qkv[rs, GD_DIM + h * LANES:GD_DIM + (h + 1) * LANES]
            v = qkv[rs, 2 * GD_DIM + h * LANES:2 * GD_DIM + (h + 1) * LANES]
            q = q * lax.rsqrt(jnp.sum(q * q, axis=-1, keepdims=True) + 1e-6) * (GD_HEAD_DIM ** -0.5)
            k = k * lax.rsqrt(jnp.sum(k * k, axis=-1, keepdims=True) + 1e-6)
            beta = beta_all[rs, h:h + 1]
            gam = gam_all[rs, GD_HEADS + h:GD_HEADS + h + 1]
            gam_row2 = gam_t2[GD_HEADS + h:GD_HEADS + h + 1, ci * 2 * L:(ci + 1) * 2 * L]
            gam_last = gam[L - 1:L, :]
            kb = k * beta
            units.append(dict(h=h, rs=rs, q=q, k=k, kb=kb, vb=v * beta, gam=gam, gam_row2=gam_row2,
                              gam_last=gam_last, kg=kb * jnp.exp(gam), qg=q * jnp.exp(gam),
                              kd=k * jnp.exp(gam_last - gam),
                              decay=jnp.exp(jnp.where(incl, gam - gam_row2[:, 0:L], NEG))))
    pairs = [(units[i], units[i + 1]) for i in range(0, len(units), 2)]
    ms = []
    for h0, h1 in pairs:
        lhs = jnp.concatenate([jnp.concatenate([h0["kb"], zero], axis=1),
                               jnp.concatenate([zero, h1["kb"]], axis=1)], axis=0)
        rhs = jnp.concatenate([jnp.concatenate([h0["k"], zero], axis=1),
                               jnp.concatenate([zero, h1["k"]], axis=1)], axis=0)
        gam_col = jnp.concatenate([h0["gam"], h1["gam"]], axis=0)
        gam_row = jnp.where(lane2 < L, h0["gam_row2"], h1["gam_row2"])
        decay2 = jnp.exp(jnp.where(bd_incl, gam_col - gam_row, NEG))
        ms.append(jnp.where(bd_strict, _mm_nt(lhs, rhs) * decay2, 0.0))
    tinvs = _tri_inv_multi(ms, 2 * L, L)
    for (h0, h1), tinv in zip(pairs, tinvs):
        rhs = jnp.concatenate([jnp.concatenate([h0["vb"], h0["kg"]], axis=1),
                               jnp.concatenate([h1["vb"], h1["kg"]], axis=1)], axis=0)
        uw = _mm(tinv, rhs)
        h0["u"], h0["w"] = uw[0:L, 0:LANES], uw[0:L, LANES:]
        h1["u"], h1["w"] = uw[L:2 * L, 0:LANES], uw[L:2 * L, LANES:]
    for hd in units:
        attn = _mm_nt(hd["q"], hd["k"]) * hd["decay"]
        uw = jnp.concatenate([hd["u"], hd["w"]], axis=1)
        auw = _mm(attn, uw)
        hd["o0"] = auw[:, 0:LANES]
        hd["qs"] = hd["qg"] - auw[:, LANES:]
        cc = _mm_tn(hd["kd"], uw)
        hd["c1"], hd["c2"] = cc[:, 0:LANES], cc[:, LANES:]

    ss = [s_ref[h] for h in range(GD_HEADS)]
    for ci in range(nch):
        hds = units[ci * GD_HEADS:(ci + 1) * GD_HEADS]
        for hd, s in zip(hds, ss):
            hd["o"] = hd["o0"] + _mm(hd["qs"], s)
        ss = [s * jnp.exp(hd["gam_last"]) + hd["c1"] - _mm(hd["c2"], s) for hd, s in zip(hds, ss)]
    for h in range(GD_HEADS):
        s_ref[h] = ss[h]
    for hd in units:
        sl = slice(hd["h"] * LANES, (hd["h"] + 1) * LANES)
        o_ref[0, hd["rs"], sl] = _rms_rows(hd["o"], ng_ref[...]) * jax.nn.silu(z[hd["rs"], sl])


def _gate_row(lo, vals):
    return jnp.zeros((1, LANES), F32).at[0, lo:lo + vals.shape[0]].set(vals.astype(F32))


def _gdn(p, pg, conv_w, a_log, dt_bias, norm_g):
    bsz, t, _ = p.shape
    gbias = _gate_row(GD_HEADS, dt_bias)
    nal = _gate_row(GD_HEADS, -jnp.exp(a_log))
    nal_col = nal.reshape(LANES, 1)
    ng = norm_g.reshape(1, GD_HEAD_DIM).astype(F32)
    full = lambda z: pl.BlockSpec(z.shape, lambda b, c: (0, 0))
    rows = GD_NCH * CHUNK
    return pl.pallas_call(
        _gdn_kernel,
        grid=(bsz, t // rows),
        in_specs=[pl.BlockSpec((1, rows, GD_MAIN), lambda b, c: (b, c, 0)),
                  pl.BlockSpec((1, rows, LANES), lambda b, c: (b, c, 0)),
                  full(conv_w), full(gbias), full(nal), full(nal_col), full(ng)],
        out_specs=pl.BlockSpec((1, rows, GD_DIM), lambda b, c: (b, c, 0)),
        out_shape=jax.ShapeDtypeStruct((bsz, t, GD_DIM), F32),
        scratch_shapes=[pltpu.VMEM((8, 3 * GD_DIM), F32), pltpu.VMEM((GD_HEADS, LANES, LANES), F32)],
        compiler_params=_cparams("arbitrary", "arbitrary"),
        name="gdn_mix",
    )(p, pg, conv_w.astype(F32), gbias, nal, nal_col, ng)


def _mlstm_kernel(p_ref, pg_ref, gb_ref, ng_ref, o_ref, c_ref, n_ref, m_ref):
    L = ML_CHUNK
    ci = pl.program_id(1)

    @pl.when(ci == 0)
    def _():
        c_ref[...] = jnp.zeros_like(c_ref)
        n_ref[...] = jnp.zeros_like(n_ref)
        m_ref[...] = jnp.zeros_like(m_ref)

    x = p_ref[0]
    R = x.shape[0]
    nch = R // L
    gt = pg_ref[0] + gb_ref[...]
    logf = jax.nn.log_sigmoid(gt)
    gt_t = gt.T
    logf_t = jax.nn.log_sigmoid(gt_t)
    tr, tc = _iota2((R, R))
    same = (tr // L) == (tc // L)
    bc_all = _cumsum_rows(jnp.where(same & (tc <= tr), 1.0, 0.0).astype(BF16), logf)
    bc_t = _cumsum_cols(logf_t, jnp.where(same & (tr <= tc), 1.0, 0.0).astype(BF16))
    lr, lc = _iota2((L, L))
    incl = lc <= lr
    i_lo, f_lo = 2 * GD_HEADS, 2 * GD_HEADS + ML_HEADS

    units = []
    m_run = [m_ref[h][0:1, 0:1] for h in range(ML_HEADS)]
    for ci in range(nch):
        rs = slice(ci * L, (ci + 1) * L)
        for h in range(ML_HEADS):
            sl = slice(h * LANES, (h + 1) * LANES)
            q = x[rs, sl]
            k = x[rs, ML_DIM + h * LANES:ML_DIM + (h + 1) * LANES] * (ML_HEAD_DIM ** -0.5)
            v = x[rs, 2 * ML_DIM + h * LANES:2 * ML_DIM + (h + 1) * LANES]
            bc = bc_all[rs, f_lo + h:f_lo + h + 1]
            bc_row = bc_t[f_lo + h:f_lo + h + 1, rs]
            ig = gt[rs, i_lo + h:i_lo + h + 1]
            ig_row = gt_t[i_lo + h:i_lo + h + 1, rs]
            m_prev = m_run[h]
            b_last = bc[L - 1:L, :]
            d_log = jnp.where(incl, bc - bc_row + ig_row, NEG)
            inter_log = bc + m_prev
            m_t = jnp.maximum(inter_log, jnp.max(d_log, axis=-1, keepdims=True))
            upd_log = b_last - bc + ig
            m_new = jnp.maximum(b_last + m_prev, jnp.max(upd_log, axis=0, keepdims=True))
            m_run[h] = m_new
            wk = jnp.exp(upd_log - m_new) * k
            units.append(dict(h=h, rs=rs, sl=sl, q=q, v=v, m_t=m_t, wk=wk, inter_w=jnp.exp(inter_log - m_t),
                              dec=jnp.exp(b_last + m_prev - m_new), sc=_mm_nt(q, k) * jnp.exp(d_log - m_t)))
    for hd in units:
        hd["scv"] = _mm(hd["sc"], hd["v"])
        hd["kv"] = _mm_tn(hd["wk"], hd["v"])
    cs = [c_ref[h] for h in range(ML_HEADS)]
    ns = [n_ref[h][0:1, :] for h in range(ML_HEADS)]
    for hd in units:
        h = hd["h"]
        q, sc, inter_w, dec = hd["q"], hd["sc"], hd["inter_w"], hd["dec"]
        num = inter_w * _mm(q, cs[h]) + hd["scv"]
        den = inter_w * jnp.sum(q * ns[h], axis=-1, keepdims=True) + jnp.sum(sc, axis=-1, keepdims=True)
        hd["hh"] = num / jnp.maximum(jnp.abs(den), jnp.exp(-hd["m_t"]))
        cs[h] = dec * cs[h] + hd["kv"]
        ns[h] = dec * ns[h] + jnp.sum(hd["wk"], axis=0, keepdims=True)
    for h in range(ML_HEADS):
        c_ref[h] = cs[h]
        n_ref[h] = jnp.broadcast_to(ns[h], (8, LANES))
        m_ref[h] = jnp.broadcast_to(m_run[h], (8, LANES))
    for hd in units:
        h, rs, sl = hd["h"], hd["rs"], hd["sl"]
        og = jax.nn.sigmoid(x[rs, 3 * ML_DIM + h * LANES:3 * ML_DIM + (h + 1) * LANES])
        o_ref[0, rs, sl] = og * _rms_rows(hd["hh"], ng_ref[:, sl])


def _mlstm(p, pg, i_bias, f_bias, norm_g):
    bsz, t, _ = p.shape
    gbias = _gate_row(2 * GD_HEADS, jnp.concatenate([i_bias, f_bias]))
    ng = norm_g.reshape(1, ML_DIM).astype(F32)
    lc = ML_NCH * ML_CHUNK
    full = lambda z: pl.BlockSpec(z.shape, lambda b, c: (0, 0))
    return pl.pallas_call(
        _mlstm_kernel,
        grid=(bsz, t // lc),
        in_specs=[pl.BlockSpec((1, lc, ML_MAIN), lambda b, c: (b, c, 0)),
                  pl.BlockSpec((1, lc, LANES), lambda b, c: (b, c, 0)), full(gbias), full(ng)],
        out_specs=pl.BlockSpec((1, lc, ML_DIM), lambda b, c: (b, c, 0)),
        out_shape=jax.ShapeDtypeStruct((bsz, t, ML_DIM), F32),
        scratch_shapes=[pltpu.VMEM((ML_HEADS, LANES, LANES), F32), pltpu.VMEM((ML_HEADS, 8, LANES), F32),
                        pltpu.VMEM((ML_HEADS, 8, LANES), F32)],
        compiler_params=_cparams("arbitrary", "arbitrary"),
        name="mlstm_mix",
    )(p, pg, gbias, ng)


def _route(h, g_ref, wh_ref, wl_ref, b_ref, off_ref):
    tm = h.shape[0]
    xn = _rms_rows(h, g_ref[...])
    xh = xn.astype(BF16)
    xl = (xn - xh.astype(F32)).astype(BF16)
    dot = lambda a, b: jnp.dot(a, b, preferred_element_type=F32)
    logits = dot(xh, wh_ref[...]) + dot(xh, wl_ref[...]) + dot(xl, wh_ref[...]) + b_ref[...]
    lt = logits.T[0:ROUTE_ROWS, :]
    row = lax.broadcasted_iota(I32, lt.shape, 0)
    big = jnp.int32(1 << 20)
    is_grp = (row >= N_EXPERTS) & (row < N_EXPERTS + N_GROUPS)
    lg = jnp.where(is_grp, lt, NEG)
    gmax = jnp.max(lg, axis=0, keepdims=True)
    p_top = 1.0 / jnp.sum(jnp.exp(lg - gmax), axis=0, keepdims=True)
    g_idx = jnp.min(jnp.where(lg == gmax, row, big), axis=0, keepdims=True) - N_EXPERTS
    valid = (row < N_EXPERTS) & ((row // EXPERTS_PER_GROUP) == g_idx)
    v1 = jnp.where(valid, lt, NEG)
    m1 = jnp.max(v1, axis=0, keepdims=True)
    i1 = jnp.min(jnp.where(v1 == m1, row, big), axis=0, keepdims=True)
    v2 = jnp.where(row == i1, NEG, v1)
    m2 = jnp.max(v2, axis=0, keepdims=True)
    i2 = jnp.min(jnp.where(v2 == m2, row, big), axis=0, keepdims=True)
    e21 = jnp.exp(m2 - m1)
    gate1 = p_top / (1.0 + e21)
    gate2 = p_top * e21 / (1.0 + e21)

    sel1 = row == i1
    sel2 = row == i2
    onehot = jnp.where(sel1 | sel2, 1.0, 0.0)
    tr, tc = _iota2((tm, tm))
    earlier = jnp.where(tr < tc, 1.0, 0.0).astype(BF16)
    off = off_ref[0:ROUTE_ROWS, 0:1]
    prefix = jnp.dot(onehot.astype(BF16), earlier, preferred_element_type=F32) + off
    rank1 = jnp.sum(jnp.where(sel1, prefix, 0.0), axis=0, keepdims=True)
    rank2 = jnp.sum(jnp.where(sel2, prefix, 0.0), axis=0, keepdims=True)
    off_ref[0:ROUTE_ROWS, :] = jnp.broadcast_to(off + jnp.sum(onehot, axis=1, keepdims=True), (ROUTE_ROWS, LANES))

    packed = jnp.concatenate([i1.astype(F32), i2.astype(F32), rank1, rank2, gate1, gate2,
                              jnp.zeros((2, tm), F32)], axis=0)
    return xn, packed


def _out_proj_route_kernel(h_ref, ya_ref, yb_ref, wa_ref, wb_ref, g_ref, wh_ref, wl_ref, b_ref,
                           o_ref, xt_ref, rg_ref, cnt_ref, dest_ref, off_ref, rt_ref):
    phase = pl.program_id(0)
    s = pl.program_id(1)
    tm = h_ref.shape[0]

    @pl.when((phase == 0) & (s == 0))
    def _():
        off_ref[...] = jnp.zeros_like(off_ref)

    @pl.when(phase == 0)
    def _():
        out = h_ref[...] + _mm(ya_ref[...], wa_ref[...]) + _mm(yb_ref[...], wb_ref[...])
        o_ref[...] = out
        xn, packed = _route(out, g_ref, wh_ref, wl_ref, b_ref, off_ref)
        _store_row_tiles(xt_ref, xn)
        rt_ref[s] = packed
        wide = jnp.concatenate([packed, jnp.zeros((LANES - 8, tm), F32)], axis=0)
        rg_ref[...] = wide.T
        cnt_ref[...] = off_ref[...]

    @pl.when(phase == 1)
    def _():
        counts = off_ref[...]
        tiles_per = jnp.floor((counts + (MOE_TM - 1)) * (1.0 / MOE_TM))
        er, ec = _iota2((LANES, LANES))
        tile_end = _mm(jnp.where(ec <= er, 1.0, 0.0), tiles_per)
        seg_start = ((tile_end - tiles_per) * MOE_TM)[0:ROUTE_ROWS, 0:1]
        blk = rt_ref[s]
        row = lax.broadcasted_iota(I32, (ROUTE_ROWS, tm), 0).astype(F32)
        pick = lambda e: jnp.sum(jnp.where(row == e, seg_start, 0.0), axis=0, keepdims=True)
        d1 = pick(blk[0:1]) + blk[2:3]
        d2 = pick(blk[1:2]) + blk[3:4]
        dest_ref[...] = jnp.concatenate([d1, d2, jnp.zeros((6, tm), F32)], axis=0).astype(I32)


def _out_proj_route(h, ya, yb, w_out, g_ffn, w_group, b_group, w_router, b_router):
    n, d = h.shape
    da, db = ya.shape[1], yb.shape[1]
    tm = min(PROJ_TM, n)
    wa = w_out[:da].astype(BF16)
    wb = w_out[da:].astype(BF16)
    pad = LANES - N_EXPERTS - N_GROUPS
    w_cat = jnp.concatenate([w_router, w_group, jnp.zeros((d, pad), F32)], axis=1)
    b_cat = jnp.concatenate([b_router, b_group, jnp.zeros((pad,), F32)]).reshape(1, LANES)
    w_hi = w_cat.astype(BF16)
    w_lo = (w_cat - w_hi.astype(F32)).astype(BF16)
    steps = n // tm
    row = lambda p, s: (s * (1 - p) + (steps - 1) * p, 0)
    fixed = lambda p, s: (0, 0)
    return pl.pallas_call(
        _out_proj_route_kernel,
        grid=(2, steps),
        in_specs=[pl.BlockSpec((tm, d), row), pl.BlockSpec((tm, da), row), pl.BlockSpec((tm, db), row),
                  pl.BlockSpec((da, d), fixed), pl.BlockSpec((db, d), fixed), pl.BlockSpec((1, d), fixed),
                  pl.BlockSpec((d, LANES), fixed), pl.BlockSpec((d, LANES), fixed), pl.BlockSpec((1, LANES), fixed)],
        out_specs=[pl.BlockSpec((tm, d), row), pl.BlockSpec((tm * ROW_TILE, LANES), row),
                   pl.BlockSpec((tm, LANES), row), pl.BlockSpec((LANES, LANES), fixed),
                   pl.BlockSpec((8, tm), lambda p, s: (0, s * p))],
        out_shape=[jax.ShapeDtypeStruct((n, d), F32), jax.ShapeDtypeStruct((n * ROW_TILE, LANES), F32),
                   jax.ShapeDtypeStruct((n, LANES), F32), jax.ShapeDtypeStruct((LANES, LANES), F32),
                   jax.ShapeDtypeStruct((8, n), I32)],
        scratch_shapes=[pltpu.VMEM((LANES, LANES), F32), pltpu.VMEM((steps, 8, tm), F32)],
        compiler_params=_cparams("arbitrary", "arbitrary"),
        name="out_proj_route",
    )(h, ya, yb, wa, wb, g_ffn.reshape(1, d), w_hi, w_lo, b_cat)


def _gather_rows(src_hbm, idx_ref, base, dst, sem):
    def body(j, carry):
        r = pl.multiple_of(idx_ref[base + j] * ROW_TILE, ROW_TILE)
        pltpu.make_async_copy(src_hbm.at[pl.ds(r, ROW_TILE), :],
                              dst.at[pl.ds(pl.multiple_of(j * ROW_TILE, ROW_TILE), ROW_TILE), :],
                              sem).start()
        return carry

    lax.fori_loop(0, dst.shape[0] // ROW_TILE, body, 0, unroll=GATHER_UNROLL)


def _wait_rows(src_hbm, dst, sem):
    pltpu.make_async_copy(src_hbm.at[pl.ds(0, dst.shape[0]), :], dst, sem).wait()


def _expert_kernel(te_ref, nt_ref, d1_ref, d2_ref, lo_ref, hi_ref, x_hbm, wg_ref, wu_ref, wd_ref, o_ref,
                   xbuf, wgb, wub, wdb, src_ref, sem):
    tm = xbuf.shape[1] // ROW_TILE
    i = pl.program_id(0)
    n_used = nt_ref[0]
    slot = i % EXPERT_BUFS

    @pl.when(i == 0)
    def _():
        def clear(p, carry):
            src_ref[p] = 0
            return carry

        def fill(t, carry):
            src_ref[d1_ref[t]] = t
            src_ref[d2_ref[t]] = t
            return carry

        for e in range(lo_ref.shape[0]):
            lax.fori_loop(lo_ref[e], hi_ref[e], clear, 0)
        lax.fori_loop(0, d1_ref.shape[0], fill, 0, unroll=GATHER_UNROLL)
        for t in range(EXPERT_BUFS - 1):
            @pl.when(t < n_used)
            def _():
                _gather_rows(x_hbm, src_ref, t * tm, xbuf.at[t], sem.at[t])

    ahead = i + EXPERT_BUFS - 1

    @pl.when(ahead < n_used)
    def _():
        _gather_rows(x_hbm, src_ref, ahead * tm, xbuf.at[ahead % EXPERT_BUFS], sem.at[ahead % EXPERT_BUFS])

    @pl.when((i == 0) | (te_ref[i] != te_ref[jnp.maximum(i - 1, 0)]))
    def _():
        wgb[...] = wg_ref[0].astype(BF16)
        wub[...] = wu_ref[0].astype(BF16)
        wdb[...] = wd_ref[0].astype(BF16)

    @pl.when(i < n_used)
    def _():
        _wait_rows(x_hbm, xbuf.at[slot], sem.at[slot])
        xn = _load_row_tiles(xbuf.at[slot], tm).astype(BF16)
        gate = jnp.dot(xn, wgb[...], preferred_element_type=F32)
        up = jnp.dot(xn, wub[...], preferred_element_type=F32)
        hid = (jax.nn.silu(gate) * up).astype(BF16)
        _store_row_tiles(o_ref, jnp.dot(hid, wdb[...], preferred_element_type=F32))

    @pl.when(i >= n_used)
    def _():
        o_ref[...] = jnp.zeros_like(o_ref)


def _experts(xn_tiles, tile_expert, n_used, d1, d2, pad_lo, pad_hi, w_gate, w_up, w_down):
    d = D_MODEL
    n_tiles = tile_expert.shape[0]
    tm = MOE_TM
    de = w_gate.shape[-1]
    wspec = lambda rows, cols: pl.BlockSpec((1, rows, cols), lambda i, te, *_: (te[i], 0, 0))
    grid_spec = pltpu.PrefetchScalarGridSpec(
        num_scalar_prefetch=6,
        grid=(n_tiles,),
        in_specs=[pl.BlockSpec(memory_space=pl.ANY), wspec(d, de), wspec(d, de), wspec(de, d)],
        out_specs=pl.BlockSpec((tm * ROW_TILE, LANES), lambda i, *_: (i, 0)),
        scratch_shapes=[pltpu.VMEM((EXPERT_BUFS, tm * ROW_TILE, LANES), F32), pltpu.VMEM((d, de), BF16),
                        pltpu.VMEM((d, de), BF16), pltpu.VMEM((de, d), BF16),
                        pltpu.SMEM((n_tiles * tm,), I32), pltpu.SemaphoreType.DMA((EXPERT_BUFS,))],
    )
    return pl.pallas_call(
        _expert_kernel,
        grid_spec=grid_spec,
        out_shape=jax.ShapeDtypeStruct((n_tiles * tm * ROW_TILE, LANES), F32),
        compiler_params=_cparams("arbitrary"),
        name="moe_experts",
    )(tile_expert, n_used, d1, d2, pad_lo, pad_hi, xn_tiles, w_gate, w_up, w_down)


def _combine_kernel(d1_ref, d2_ref, ys_hbm, h_ref, rg_ref, g_ref, *refs, n_proj, final_norm):
    w_refs, o_ref, p_refs = refs[:n_proj], refs[n_proj], refs[n_proj + 1:2 * n_proj + 1]
    y1, y2, sem = refs[2 * n_proj + 1:]
    tm = h_ref.shape[0]
    i = pl.program_id(0)
    slot = i % 2

    def gather(tile, s):
        _gather_rows(ys_hbm, d1_ref, tile * tm, y1.at[s], sem.at[0, s])
        _gather_rows(ys_hbm, d2_ref, tile * tm, y2.at[s], sem.at[1, s])

    @pl.when(i == 0)
    def _():
        gather(0, 0)

    @pl.when(i + 1 < pl.num_programs(0))
    def _():
        gather(i + 1, 1 - slot)

    _wait_rows(ys_hbm, y1.at[slot], sem.at[0, slot])
    _wait_rows(ys_hbm, y2.at[slot], sem.at[1, slot])
    rg = rg_ref[...]
    out = (h_ref[...] + rg[:, 4:5] * _load_row_tiles(y1.at[slot], tm)
           + rg[:, 5:6] * _load_row_tiles(y2.at[slot], tm))
    if final_norm:
        out = _rms_rows(out, g_ref[...])
    o_ref[...] = out
    if n_proj:
        y = _rms_rows(out, g_ref[...]).astype(BF16)
        for w_ref, p_ref in zip(w_refs, p_refs):
            p_ref[...] = jnp.dot(y, w_ref[...], preferred_element_type=F32)


def _combine(h, ys, d1, d2, rg, g, ws=(), final_norm=False):
    n, d = h.shape
    tm = min(PROJ_TM, n)
    row = lambda i, a, b: (i, 0)
    fixed = lambda i, a, b: (0, 0)
    grid_spec = pltpu.PrefetchScalarGridSpec(
        num_scalar_prefetch=2,
        grid=(n // tm,),
        in_specs=[pl.BlockSpec(memory_space=pl.ANY), pl.BlockSpec((tm, d), row), pl.BlockSpec((tm, LANES), row),
                  pl.BlockSpec((1, d), fixed)] + [pl.BlockSpec(w.shape, fixed) for w in ws],
        out_specs=[pl.BlockSpec((tm, d), row)] + [pl.BlockSpec((tm, w.shape[1]), row) for w in ws],
        scratch_shapes=[pltpu.VMEM((2, tm * ROW_TILE, LANES), F32), pltpu.VMEM((2, tm * ROW_TILE, LANES), F32),
                        pltpu.SemaphoreType.DMA((2, 2))],
    )
    return pl.pallas_call(
        functools.partial(_combine_kernel, n_proj=len(ws), final_norm=final_norm),
        grid_spec=grid_spec,
        out_shape=[jax.ShapeDtypeStruct((n, d), F32)] + [jax.ShapeDtypeStruct((n, w.shape[1]), F32) for w in ws],
        compiler_params=_cparams("arbitrary"),
        name="moe_combine",
    )(d1, d2, ys, h, rg, g.reshape(1, d), *ws)


def _hmoe_residual(h, xn_tiles, rg, cnt, dest, layer, w_gate, w_up, w_down, g, ws=(), final_norm=False):
    n, d = h.shape
    stack = lambda w: w.reshape((-1,) + w.shape[2:])

    counts = cnt[:N_EXPERTS, 0].astype(I32)
    n_tiles = (2 * n) // MOE_TM + N_EXPERTS
    tile_end = jnp.cumsum((counts + MOE_TM - 1) // MOE_TM)
    d1, d2 = dest[0], dest[1]
    tile_ids = jnp.arange(n_tiles, dtype=I32)
    tile_expert = jnp.minimum(jnp.sum((tile_end[None, :] <= tile_ids[:, None]).astype(I32), axis=1), N_EXPERTS - 1)
    tile_expert = tile_expert + layer * N_EXPERTS
    seg_end = tile_end * MOE_TM
    seg_start = seg_end - ((counts + MOE_TM - 1) // MOE_TM) * MOE_TM
    pad_lo = jnp.concatenate([seg_start + counts, seg_end[N_EXPERTS - 1:]]).astype(I32)
    pad_hi = jnp.concatenate([seg_end, jnp.full((1,), n_tiles * MOE_TM, I32)]).astype(I32)
    n_used = tile_end[N_EXPERTS - 1:].astype(I32)
    ys = _experts(xn_tiles, tile_expert, n_used, d1, d2, pad_lo, pad_hi, stack(w_gate), stack(w_up), stack(w_down))
    return _combine(h, ys, d1, d2, rg, g, ws, final_norm)


def kernel(x, norm_mix, norm_ffn, norm_final, ev_w_in, ev_mu, rw_w0, rw_w2, rw_a0, rw_a2, rw_g2, rw_k_k, rw_k_a, rw_r_k, rw_ln_w, rw_ln_b, hg_lb_logits, hg_norm, ev_w_out, od_w_in, gd_conv, gd_a_log, gd_dt_bias, gd_norm, ml_i_bias, ml_f_bias, ml_norm, od_w_out, moe_w_group, moe_b_group, moe_w_router, moe_b_router, moe_w_gate, moe_w_up, moe_w_down):
    bsz, t, d = x.shape
    n = bsz * t
    depth = norm_mix.shape[0]
    lb_table = jnp.cumsum(jax.nn.softmax(hg_lb_logits.astype(F32), axis=0), axis=0)
    def in_proj_weights(layer):
        j = layer // 2
        if layer % 2 == 0:
            return _pack_weights(ev_w_in[j], [(0, RW_IN), (RW_IN, RW_IN + HG_IN)])
        return _pack_weights(od_w_in[j], [(0, GD_MAIN), (GD_IN, GD_IN + ML_MAIN)],
                             gate_ranges=[(GD_MAIN, GD_IN), (GD_IN + ML_MAIN, GD_IN + ML_MAIN + 2 * ML_HEADS)])

    h = x.reshape(n, d)
    proj = _rms_proj(h, norm_mix[0], in_proj_weights(0))
    for layer in range(depth):
        j = layer // 2
        if layer % 2 == 0:
            p_rw, p_hg = proj
            ya = _rwkv7(p_rw.reshape(bsz, t, RW_IN), ev_mu[j], rw_w0[j], rw_w2[j], rw_a0[j], rw_a2[j], rw_g2[j],
                        rw_k_k[j], rw_k_a[j], rw_r_k[j], rw_ln_w[j], rw_ln_b[j])
            yb = _hgrn2(p_hg.reshape(bsz, t, HG_IN), lb_table[j], hg_norm[j])
            mix_a, mix_b, w_out = ya.reshape(n, RW_DIM), yb.reshape(n, HG_KDIM), ev_w_out[j]
        else:
            p_gd, p_ml, p_gt = proj
            p_gt = p_gt.reshape(bsz, t, LANES)
            yc = _gdn(p_gd.reshape(bsz, t, GD_MAIN), p_gt, gd_conv[j], gd_a_log[j], gd_dt_bias[j], gd_norm[j])
            yd = _mlstm(p_ml.reshape(bsz, t, ML_MAIN), p_gt, ml_i_bias[j], ml_f_bias[j], ml_norm[j])
            mix_a, mix_b, w_out = yc.reshape(n, GD_DIM), yd.reshape(n, ML_DIM), od_w_out[j]
        h, xn_tiles, rg, cnt, dest = _out_proj_route(h, mix_a, mix_b, w_out, norm_ffn[layer], moe_w_group[layer],
                                                     moe_b_group[layer], moe_w_router[layer], moe_b_router[layer])
        if layer == depth - 1:
            (h,) = _hmoe_residual(h, xn_tiles, rg, cnt, dest, layer, moe_w_gate, moe_w_up, moe_w_down,
                                  norm_final, final_norm=True)
        else:
            h, *proj = _hmoe_residual(h, xn_tiles, rg, cnt, dest, layer, moe_w_gate, moe_w_up, moe_w_down,
                                      norm_mix[layer + 1], ws=in_proj_weights(layer + 1))
    return h.reshape(bsz, t, d)
```

```python
import functools
import math

import jax
import jax.numpy as jnp
from jax import lax
from jax.experimental import pallas as pl
from jax.experimental.pallas import tpu as pltpu

F32 = jnp.float32
BF16 = jnp.bfloat16
I32 = jnp.int32
HIGHEST = lax.Precision.HIGHEST

D_MODEL = 1024
NORM_EPS = 1e-6
RW_HEADS, RW_HEAD_DIM = 8, 64
RW_DIM = RW_HEADS * RW_HEAD_DIM
R_DECAY, R_AAA, R_GATE = 64, 64, 128
RW_IN = 3 * RW_DIM + R_DECAY + R_AAA + R_GATE
RW_LN_EPS = 64e-5
HG_HEADS, HG_DIM = 8, 64
HG_KDIM = HG_HEADS * HG_DIM
HG_IN = 4 * HG_KDIM
GD_HEADS, GD_HEAD_DIM = 4, 128
GD_DIM = GD_HEADS * GD_HEAD_DIM
CONV_K = 4
GD_MAIN = 4 * GD_DIM
GD_IN = GD_MAIN + 2 * GD_HEADS
ML_HEADS, ML_HEAD_DIM = 4, 128
ML_DIM = ML_HEADS * ML_HEAD_DIM
ML_MAIN = 4 * ML_DIM
N_GROUPS, EXPERTS_PER_GROUP = 4, 8
N_EXPERTS = N_GROUPS * EXPERTS_PER_GROUP
D_EXPERT = 256

LANES = 128
VMEM_LIMIT_BYTES = 48 * 1024 * 1024

PROJ_TM = 256
PACK_TM = 128
CHUNK = 64
SUB = 16
RW_NCH = 4
GD_NCH = 8
HG_ROWS = 128
ML_CHUNK = 128
ML_NCH = 2
MOE_TM = 256
ROUTE_ROWS = 40
GATHER_UNROLL = 8
NEG = -1e30


def _cparams(*sem):
    return pltpu.CompilerParams(dimension_semantics=sem, vmem_limit_bytes=VMEM_LIMIT_BYTES)


def _mm(a, b):
    return jnp.dot(a.astype(BF16), b.astype(BF16), preferred_element_type=F32)


def _mm_nt(a, b):
    return lax.dot_general(a.astype(BF16), b.astype(BF16), (((1,), (1,)), ((), ())), preferred_element_type=F32)


def _mm_tn(a, b):
    return lax.dot_general(a.astype(BF16), b.astype(BF16), (((0,), (0,)), ((), ())), preferred_element_type=F32)


def _mm_hi(a, b):
    return jnp.dot(a, b, precision=HIGHEST, preferred_element_type=F32)


def _mm_split(x, ones_bf16):
    hi = x.astype(BF16)
    lo = (x - hi.astype(F32)).astype(BF16)
    return (jnp.dot(hi, ones_bf16, preferred_element_type=F32) + jnp.dot(lo, ones_bf16, preferred_element_type=F32))


def _iota2(shape):
    return lax.broadcasted_iota(I32, shape, 0), lax.broadcasted_iota(I32, shape, 1)


def _rms_rows(x, g, eps=NORM_EPS):
    return x * lax.rsqrt(jnp.mean(x * x, axis=-1, keepdims=True) + eps) * g


def _seg_ones(width, seg):
    r, c = _iota2((width, width))
    return jnp.where((r // seg) == (c // seg), 1.0, 0.0).astype(BF16)


def _split3(x):
    x1 = x.astype(BF16)
    r1 = x - x1.astype(F32)
    x2 = r1.astype(BF16)
    return x1, x2, (r1 - x2.astype(F32)).astype(BF16)


def _cumsum_rows(tri_bf16, x):
    return sum(jnp.dot(tri_bf16, t, preferred_element_type=F32) for t in _split3(x))


def _cumsum_cols(x, tri_bf16):
    return sum(jnp.dot(t, tri_bf16, preferred_element_type=F32) for t in _split3(x))


def _mm3(a, b):
    ah = a.astype(BF16)
    al = (a - ah.astype(F32)).astype(BF16)
    bh = b.astype(BF16)
    bl = (b - bh.astype(F32)).astype(BF16)
    dot = lambda x, y: jnp.dot(x, y, preferred_element_type=F32)
    return dot(ah, bh) + dot(ah, bl) + dot(al, bh)


def _tri_inv_multi(ms, n, chain):
    assert chain // SUB <= 4
    r, c = _iota2((n, n))
    same = (r // SUB) == (c // SUB)
    eye = jnp.where(r == c, 1.0, 0.0).astype(F32)
    ds = [jnp.where(same, m, 0.0) for m in ms]
    offs = [m - d for m, d in zip(ms, ds)]
    xs = [eye - d for d in ds]
    ps = ds
    for _ in range(3):
        ps = [_mm(p, p) for p in ps]
        xs = [x + _mm(x, p) for x, p in zip(xs, ps)]
    es = [_mm(x, o) for x, o in zip(xs, offs)]
    imes = [eye - e for e in es]
    e2s = [_mm(e, e) for e in es]
    ys = [i + _mm(i, e2) for i, e2 in zip(imes, e2s)]
    xs = [_mm(y, x) for y, x in zip(ys, xs)]
    res = [eye - x - _mm3(m, x) for m, x in zip(ms, xs)]
    return [x + _mm(x, rr) for x, rr in zip(xs, res)]


def _pack_weights_kernel(w_ref, *o_refs, ranges, gate_ranges):
    w = w_ref[...]
    for (lo, hi), o_ref in zip(ranges, o_refs):
        o_ref[...] = w[:, lo:hi].astype(BF16)
    if gate_ranges:
        cols = [w[:, lo:hi] for lo, hi in gate_ranges]
        used = sum(hi - lo for lo, hi in gate_ranges)
        cols.append(jnp.zeros((w.shape[0], LANES - used), F32))
        o_refs[-1][...] = jnp.concatenate(cols, axis=1).astype(BF16)


def _pack_weights(w, ranges, gate_ranges=()):
    rows, cols = w.shape
    tm = PACK_TM
    widths = [hi - lo for lo, hi in ranges] + ([LANES] if gate_ranges else [])
    return pl.pallas_call(
        functools.partial(_pack_weights_kernel, ranges=tuple(ranges), gate_ranges=tuple(gate_ranges)),
        grid=(rows // tm,),
        in_specs=[pl.BlockSpec((tm, cols), lambda i: (i, 0))],
        out_specs=[pl.BlockSpec((tm, wd), lambda i: (i, 0)) for wd in widths],
        out_shape=[jax.ShapeDtypeStruct((rows, wd), BF16) for wd in widths],
        compiler_params=_cparams("parallel"),
        name="pack_weights",
    )(w)


def _rms_proj_kernel(x_ref, g_ref, *refs, n_out):
    y = _rms_rows(x_ref[...], g_ref[...]).astype(BF16)
    for w_ref, o_ref in zip(refs[:n_out], refs[n_out:]):
        o_ref[...] = jnp.dot(y, w_ref[...], preferred_element_type=F32)


def _rms_proj(x, g, ws):
    n, d = x.shape
    tm = min(PROJ_TM, n)
    in_specs = [pl.BlockSpec((tm, d), lambda i: (i, 0)), pl.BlockSpec((1, d), lambda i: (0, 0))]
    in_specs += [pl.BlockSpec(w.shape, lambda i: (0, 0)) for w in ws]
    return pl.pallas_call(
        functools.partial(_rms_proj_kernel, n_out=len(ws)),
        grid=(n // tm,),
        in_specs=in_specs,
        out_specs=[pl.BlockSpec((tm, w.shape[1]), lambda i: (i, 0)) for w in ws],
        out_shape=[jax.ShapeDtypeStruct((n, w.shape[1]), F32) for w in ws],
        compiler_params=_cparams("parallel"),
        name="rms_proj",
    )(x, g.reshape(1, d), *ws)


ROW_TILE = D_MODEL // LANES


def _store_row_tiles(ref, x):
    for j in range(ROW_TILE):
        ref[pl.ds(j, x.shape[0], stride=ROW_TILE), :] = x[:, j * LANES:(j + 1) * LANES]


def _load_row_tiles(ref, rows):
    return jnp.concatenate([ref[pl.ds(j, rows, stride=ROW_TILE), :] for j in range(ROW_TILE)], axis=1)


def _rwkv7_kernel(p_ref, mu_ref, w0_ref, w2_ref, a0_ref, a2_ref, g2_ref, kk_ref, ka_ref, rk_ref,
                  lnw_ref, lnb_ref, o_ref, prev_ref, zt_ref):
    L = CHUNK
    npair = RW_HEADS // 2
    c = pl.program_id(1)

    @pl.when(c == 0)
    def _():
        prev_ref[...] = jnp.zeros_like(prev_ref)
        zt_ref[...] = jnp.zeros_like(zt_ref)

    x = p_ref[0]
    R = x.shape[0]
    nch = R // L
    row = lax.broadcasted_iota(I32, x.shape, 0)
    xs = jnp.where(row == 0, prev_ref[7:8, :], pltpu.roll(x, 1, 0))
    prev_ref[...] = x[R - 8:R, :]
    pm = x + mu_ref[...] * (xs - x)
    r_all = pm[:, 0:RW_DIM]
    k_all = pm[:, RW_DIM:2 * RW_DIM]
    v_all = pm[:, 2 * RW_DIM:3 * RW_DIM]
    wa = pm[:, 3 * RW_DIM:3 * RW_DIM + LANES]
    gl = pm[:, 3 * RW_DIM + LANES:]
    wlog = -jax.nn.softplus(-(w0_ref[...] + _mm(jnp.tanh(wa), w2_ref[...]))) - 0.5
    ld = -jnp.exp(wlog)
    a_all = jax.nn.sigmoid(a0_ref[...] + _mm(wa, a2_ref[...]))
    g_all = _mm(jax.nn.sigmoid(gl), g2_ref[...])

    tr, tc = _iota2((R, R))
    chunk_tril = jnp.where((tc <= tr) & ((tr // L) == (tc // L)), 1.0, 0.0).astype(BF16)
    cs_all = _cumsum_rows(chunk_tril, ld)
    seg = _seg_ones(LANES, RW_HEAD_DIM)
    lane = lax.broadcasted_iota(I32, (L, LANES), 1)
    hm = (lane < RW_HEAD_DIM, lane >= RW_HEAD_DIM)
    br, bc = _iota2((2 * L, 2 * L))
    bd = (br // L) == (bc // L)
    bd_strict = bd & (bc < br)
    bd_incl = bd & (bc <= br)
    fold = lambda z: z[0:L] + z[L:2 * L]
    both = lambda z: jnp.concatenate([jnp.where(hm[0], z, 0.0), jnp.where(hm[1], z, 0.0)], axis=0)

    units = []
    for ci in range(nch):
        rs = slice(ci * L, (ci + 1) * L)
        for p in range(npair):
            sl = slice(p * LANES, (p + 1) * LANES)
            r, k, v, a = r_all[rs, sl], k_all[rs, sl], v_all[rs, sl], a_all[rs, sl]
            cs, ldp = cs_all[rs, sl], ld[rs, sl]
            kkr = k * kk_ref[:, sl]
            kk = kkr * lax.rsqrt(_mm_split(kkr * kkr, seg) + 1e-6)
            k2 = k * (1.0 + (a - 1.0) * ka_ref[:, sl])
            b = kk * a
            cs_last = cs[L - 1:L, :]
            e_neg = jnp.exp(-cs)
            e_rem = jnp.exp(cs_last - cs)
            bhat = b * e_neg
            khat = k2 * e_neg
            units.append(dict(p=p, rs=rs, sl=sl, r=r, v=v, k2=k2, rhat=r * jnp.exp(cs), gam_last=jnp.exp(cs_last),
                              btil=b * e_rem, ktil=k2 * e_rem, a2=both(kk * jnp.exp(cs - ldp)), v2=both(v),
                              rhs4=jnp.concatenate([bhat, bhat, khat, khat], axis=0)))
    for q in units:
        lhs = jnp.concatenate([q["a2"], both(q["rhat"])], axis=0)
        q["g"] = _mm_nt(lhs, q["rhs4"])
    tinvs = _tri_inv_multi([jnp.where(bd_strict, q["g"][0:2 * L, 0:2 * L], 0.0) for q in units], 2 * L, L)
    for q, tinv in zip(units, tinvs):
        q["tinv"] = tinv
        q["x2"] = _mm(jnp.where(bd_strict, q["g"][0:2 * L, 2 * L:4 * L], 0.0), q["v2"])
    for q in units:
        uw = _mm(q["tinv"], jnp.concatenate([q["x2"], q["a2"]], axis=1))
        q["u0"] = -fold(uw[:, 0:LANES])
        q["w"] = fold(uw[:, LANES:])
        q["y0"] = fold(_mm(jnp.where(bd_incl, q["g"][2 * L:4 * L, 2 * L:4 * L], 0.0), q["v2"]))
    for q in units:
        rb = jnp.where(bd_incl, q["g"][2 * L:4 * L, 0:2 * L], 0.0)
        ruw = _mm(rb, jnp.concatenate([both(q["u0"]), both(q["w"])], axis=1))
        q["yc"] = q["y0"] + fold(ruw[:, 0:LANES])
        q["ry"] = q["rhat"] - fold(ruw[:, LANES:])
        q["c1"] = _mm_tn(jnp.concatenate([q["u0"], q["v"]], axis=0), jnp.concatenate([q["btil"], q["ktil"]], axis=0))
        q["c2"] = _mm_tn(q["w"], q["btil"])

    hr, hc = _iota2((LANES, LANES))
    head_bd = (hr // RW_HEAD_DIM) == (hc // RW_HEAD_DIM)
    zts = [zt_ref[p] for p in range(npair)]
    for ci in range(nch):
        qs = units[ci * npair:(ci + 1) * npair]
        for q, zt in zip(qs, zts):
            q["y"] = q["yc"] + _mm_nt(q["ry"], zt)
        zts = [zt * q["gam_last"] + jnp.where(head_bd, q["c1"] - _mm(zt, q["c2"]), 0.0) for q, zt in zip(qs, zts)]
    for p in range(npair):
        zt_ref[p] = zts[p]

    for q in units:
        sl, rs, y = q["sl"], q["rs"], q["y"]
        mean = _mm_split(y, seg) * (1.0 / RW_HEAD_DIM)
        yc = y - mean
        var = _mm_split(yc * yc, seg) * (1.0 / RW_HEAD_DIM)
        yn = yc * lax.rsqrt(var + RW_LN_EPS) * lnw_ref[:, sl] + lnb_ref[:, sl]
        bonus = _mm_split(q["r"] * q["k2"] * rk_ref[:, sl], seg) * q["v"]
        o_ref[0, rs, sl] = (yn + bonus) * g_all[rs, sl]


def _rwkv7(p, mu, w0, w2, a0, a2, g2, k_k, k_a, r_k, ln_w, ln_b):
    bsz, t, _ = p.shape
    row = lambda z: z.reshape(1, -1).astype(F32)
    w2p = jnp.concatenate([w2, jnp.zeros_like(w2)], axis=0).astype(BF16)
    a2p = jnp.concatenate([jnp.zeros_like(a2), a2], axis=0).astype(BF16)
    params = [row(mu), row(w0), w2p, row(a0), a2p, g2.astype(BF16), row(k_k), row(k_a), row(r_k), row(ln_w), row(ln_b)]
    full = lambda z: pl.BlockSpec(z.shape, lambda b, c: (0, 0))
    rows = RW_NCH * CHUNK
    return pl.pallas_call(
        _rwkv7_kernel,
        grid=(bsz, t // rows),
        in_specs=[pl.BlockSpec((1, rows, RW_IN), lambda b, c: (b, c, 0))] + [full(z) for z in params],
        out_specs=pl.BlockSpec((1, rows, RW_DIM), lambda b, c: (b, c, 0)),
        out_shape=jax.ShapeDtypeStruct((bsz, t, RW_DIM), F32),
        scratch_shapes=[pltpu.VMEM((8, RW_IN), F32), pltpu.VMEM((RW_HEADS // 2, LANES, LANES), F32)],
        compiler_params=_cparams("arbitrary", "arbitrary"),
        name="rwkv7_mix",
    )(p, *params)


def _hgrn2_kernel(p_ref, lb_ref, ng_ref, o_ref, st_ref):
    L = HG_ROWS
    c = pl.program_id(1)

    @pl.when(c == 0)
    def _():
        st_ref[...] = jnp.zeros_like(st_ref)

    x = p_ref[0]
    lb = lb_ref[...]
    q_all = jax.nn.silu(x[:, 0:HG_KDIM])
    fg = lb + (1.0 - lb) * jax.nn.sigmoid(x[:, HG_KDIM:2 * HG_KDIM])
    k_all = 1.0 - fg
    logf = jnp.log(fg)
    v_all = x[:, 2 * HG_KDIM:3 * HG_KDIM]
    gate = x[:, 3 * HG_KDIM:]
    tr, tc = _iota2((L, L))
    blk_tril = jnp.where((tc <= tr) & ((tr // SUB) == (tc // SUB)), 1.0, 0.0).astype(BF16)
    bc_all = _cumsum_rows(blk_tril, logf)
    seg = _seg_ones(LANES, HG_DIM)
    br, bcc = _iota2((LANES, LANES))
    bd = (br // HG_DIM) == (bcc // HG_DIM)
    half = SUB // 2
    t_lo = lax.broadcasted_iota(I32, (SUB, LANES), 0)
    t_hi = lax.broadcasted_iota(I32, (half, LANES), 0) + half

    nsub = L // SUB
    npair = HG_HEADS // 2
    rows_per_unit = half * SUB + half * half
    sub_of_row = lax.broadcasted_iota(I32, (L, LANES), 0) // SUB

    units = []
    parts = []
    for p in range(npair):
        sl = slice(p * LANES, (p + 1) * LANES)
        for j in range(nsub):
            rs = slice(j * SUB, (j + 1) * SUB)
            q, k, v, bc = q_all[rs, sl], k_all[rs, sl], v_all[rs, sl], bc_all[rs, sl]
            q_hi, bc_hi = q[half:], bc[half:]
            for s in range(SUB):
                if s < half:
                    diff = jnp.where(t_lo >= s, bc - bc[s:s + 1, :], NEG)
                    parts.append(jnp.exp(diff) * q * k[s:s + 1, :])
                else:
                    diff = jnp.where(t_hi >= s, bc_hi - bc[s:s + 1, :], NEG)
                    parts.append(jnp.exp(diff) * q_hi * k[s:s + 1, :])
            bend = bc[SUB - 1:SUB, :]
            units.append(dict(v=v, qt=q * jnp.exp(bc), dec=jnp.exp(bend), kd=k * jnp.exp(bend - bc)))
    score_all = _mm(jnp.concatenate(parts, axis=0), seg)
    for ui, un in enumerate(units):
        score = score_all[ui * rows_per_unit:(ui + 1) * rows_per_unit]
        v = un["v"]
        acc_lo = jnp.zeros((half, LANES), F32)
        acc_hi = jnp.zeros((half, LANES), F32)
        off = 0
        for s in range(SUB):
            vs = v[s:s + 1, :]
            if s < half:
                acc_lo = acc_lo + score[off:off + half] * vs
                acc_hi = acc_hi + score[off + half:off + SUB] * vs
                off += SUB
            else:
                acc_hi = acc_hi + score[off:off + half] * vs
                off += half
        un["intra"] = jnp.concatenate([acc_lo, acc_hi], axis=0)
    for p in range(npair):
        us = units[p * nsub:(p + 1) * nsub]
        v_pair = jnp.concatenate([un["v"] for un in us], axis=0)
        kd_pair = jnp.concatenate([un["kd"] for un in us], axis=0)
        kd_wide = jnp.concatenate([jnp.where(sub_of_row == j, kd_pair, 0.0) for j in range(nsub)], axis=1)
        upd = _mm_tn(v_pair, kd_wide)
        for j, un in enumerate(us):
            un["upd"] = jnp.where(bd, upd[:, j * LANES:(j + 1) * LANES], 0.0)

    for p in range(npair):
        sl = slice(p * LANES, (p + 1) * LANES)
        st = st_ref[p]
        outs = []
        for un in units[p * nsub:(p + 1) * nsub]:
            outs.append(un["intra"] + _mm_nt(un["qt"], st))
            st = st * un["dec"] + un["upd"]
        st_ref[p] = st
        o = jnp.concatenate(outs, axis=0)
        ms = _mm_split(o * o, seg) * (1.0 / HG_DIM)
        o_ref[0, :, sl] = o * lax.rsqrt(ms + NORM_EPS) * ng_ref[:, sl] * jax.nn.silu(gate[:, sl])


def _hgrn2(p, lb, norm_g):
    bsz, t, _ = p.shape
    lb = lb.reshape(1, HG_KDIM).astype(F32)
    ng = jnp.tile(norm_g.astype(F32), HG_HEADS).reshape(1, HG_KDIM)
    return pl.pallas_call(
        _hgrn2_kernel,
        grid=(bsz, t // HG_ROWS),
        in_specs=[pl.BlockSpec((1, HG_ROWS, HG_IN), lambda b, c: (b, c, 0)),
                  pl.BlockSpec((1, HG_KDIM), lambda b, c: (0, 0)), pl.BlockSpec((1, HG_KDIM), lambda b, c: (0, 0))],
        out_specs=pl.BlockSpec((1, HG_ROWS, HG_KDIM), lambda b, c: (b, c, 0)),
        out_shape=jax.ShapeDtypeStruct((bsz, t, HG_KDIM), F32),
        scratch_shapes=[pltpu.VMEM((HG_HEADS // 2, LANES, LANES), F32)],
        compiler_params=_cparams("arbitrary", "arbitrary"),
        name="hgrn2_mix",
    )(p, lb, ng)


def _gdn_kernel(p_ref, pg_ref, cw_ref, gb_ref, nal_ref, nalc_ref, ng_ref, o_ref, prev_ref, s_ref):
    L = CHUNK
    c = pl.program_id(1)

    @pl.when(c == 0)
    def _():
        prev_ref[...] = jnp.zeros_like(prev_ref)
        s_ref[...] = jnp.zeros_like(s_ref)

    x = p_ref[0]
    R = x.shape[0]
    nch = R // L
    xq = x[:, 0:3 * GD_DIM]
    xcat = jnp.concatenate([prev_ref[...], xq], axis=0)
    conv = xq * cw_ref[CONV_K - 1:CONV_K, :]
    for j in range(1, CONV_K):
        conv = conv + xcat[8 - j:8 - j + R, :] * cw_ref[CONV_K - 1 - j:CONV_K - j, :]
    prev_ref[...] = xq[R - 8:R, :]
    qkv = jax.nn.silu(conv)
    z = x[:, 3 * GD_DIM:]

    gt = pg_ref[0] + gb_ref[...]
    beta_all = jax.nn.sigmoid(gt)
    g_all = nal_ref[...] * jax.nn.softplus(gt)
    g_t = nalc_ref[...] * jax.nn.softplus(gt.T)
    tr, tc = _iota2((R, R))
    chunk_tril = jnp.where((tc <= tr) & ((tr // L) == (tc // L)), 1.0, 0.0).astype(BF16)
    gam_all = _cumsum_rows(chunk_tril, g_all)
    ur, uc = _iota2((R, 2 * R))
    triu2 = jnp.where(((ur // L) == (uc // (2 * L))) & ((ur % L) <= (uc % L)), 1.0, 0.0).astype(BF16)
    gam_t2 = _cumsum_cols(g_t, triu2)
    lr, lc = _iota2((L, L))
    incl = lc <= lr
    br, bc = _iota2((2 * L, 2 * L))
    bd = (br // L) == (bc // L)
    bd_strict = bd & (bc < br)
    bd_incl = bd & (bc <= br)
    lane2 = lax.broadcasted_iota(I32, (1, 2 * L), 1)
    zero = jnp.zeros((L, LANES), F32)

    units = []
    for ci in range(nch):
        rs = slice(ci * L, (ci + 1) * L)
        for h in range(GD_HEADS):
            q = qkv[rs, h * LANES:(h + 1) * LANES]
            k = qkv[rs, GD_DIM + h * LANES:GD_DIM + (h + 1) * LANES]
            v = qkv[rs, 2 * GD_DIM + h * LANES:2 * GD_DIM + (h + 1) * LANES]
            q = q * lax.rsqrt(jnp.sum(q * q, axis=-1, keepdims=True) + 1e-6) * (GD_HEAD_DIM ** -0.5)
            k = k * lax.rsqrt(jnp.sum(k * k, axis=-1, keepdims=True) + 1e-6)
            beta = beta_all[rs, h:h + 1]
            gam = gam_all[rs, GD_HEADS + h:GD_HEADS + h + 1]
            gam_row2 = gam_t2[GD_HEADS + h:GD_HEADS + h + 1, ci * 2 * L:(ci + 1) * 2 * L]
            gam_last = gam[L - 1:L, :]
            kb = k * beta
            units.append(dict(h=h, rs=rs, q=q, k=k, kb=kb, vb=v * beta, gam=gam, gam_row2=gam_row2,
                              gam_last=gam_last, kg=kb * jnp.exp(gam), qg=q * jnp.exp(gam),
                              kd=k * jnp.exp(gam_last - gam),
                              decay=jnp.exp(jnp.where(incl, gam - gam_row2[:, 0:L], NEG))))
    pairs = [(units[i], units[i + 1]) for i in range(0, len(units), 2)]
    ms = []
    for h0, h1 in pairs:
        lhs = jnp.concatenate([jnp.concatenate([h0["kb"], zero], axis=1),
                               jnp.concatenate([zero, h1["kb"]], axis=1)], axis=0)
        rhs = jnp.concatenate([jnp.concatenate([h0["k"], zero], axis=1),
                               jnp.concatenate([zero, h1["k"]], axis=1)], axis=0)
        gam_col = jnp.concatenate([h0["gam"], h1["gam"]], axis=0)
        gam_row = jnp.where(lane2 < L, h0["gam_row2"], h1["gam_row2"])
        decay2 = jnp.exp(jnp.where(bd_incl, gam_col - gam_row, NEG))
        ms.append(jnp.where(bd_strict, _mm_nt(lhs, rhs) * decay2, 0.0))
    tinvs = _tri_inv_multi(ms, 2 * L, L)
    for (h0, h1), tinv in zip(pairs, tinvs):
        rhs = jnp.concatenate([jnp.concatenate([h0["vb"], h0["kg"]], axis=1),
                               jnp.concatenate([h1["vb"], h1["kg"]], axis=1)], axis=0)
        uw = _mm(tinv, rhs)
        h0["u"], h0["w"] = uw[0:L, 0:LANES], uw[0:L, LANES:]
        h1["u"], h1["w"] = uw[L:2 * L, 0:LANES], uw[L:2 * L, LANES:]
    for hd in units:
        attn = _mm_nt(hd["q"], hd["k"]) * hd["decay"]
        uw = jnp.concatenate([hd["u"], hd["w"]], axis=1)
        auw = _mm(attn, uw)
        hd["o0"] = auw[:, 0:LANES]
        hd["qs"] = hd["qg"] - auw[:, LANES:]
        cc = _mm_tn(hd["kd"], uw)
        hd["c1"], hd["c2"] = cc[:, 0:LANES], cc[:, LANES:]

    ss = [s_ref[h] for h in range(GD_HEADS)]
    for ci in range(nch):
        hds = units[ci * GD_HEADS:(ci + 1) * GD_HEADS]
        for hd, s in zip(hds, ss):
            hd["o"] = hd["o0"] + _mm(hd["qs"], s)
        ss = [s * jnp.exp(hd["gam_last"]) + hd["c1"] - _mm(hd["c2"], s) for hd, s in zip(hds, ss)]
    for h in range(GD_HEADS):
        s_ref[h] = ss[h]
    for hd in units:
        sl = slice(hd["h"] * LANES, (hd["h"] + 1) * LANES)
        o_ref[0, hd["rs"], sl] = _rms_rows(hd["o"], ng_ref[...]) * jax.nn.silu(z[hd["rs"], sl])


def _gate_row(lo, vals):
    return jnp.zeros((1, LANES), F32).at[0, lo:lo + vals.shape[0]].set(vals.astype(F32))


def _gdn(p, pg, conv_w, a_log, dt_bias, norm_g):
    bsz, t, _ = p.shape
    gbias = _gate_row(GD_HEADS, dt_bias)
    nal = _gate_row(GD_HEADS, -jnp.exp(a_log))
    nal_col = nal.reshape(LANES, 1)
    ng = norm_g.reshape(1, GD_HEAD_DIM).astype(F32)
    full = lambda z: pl.BlockSpec(z.shape, lambda b, c: (0, 0))
    rows = GD_NCH * CHUNK
    return pl.pallas_call(
        _gdn_kernel,
        grid=(bsz, t // rows),
        in_specs=[pl.BlockSpec((1, rows, GD_MAIN), lambda b, c: (b, c, 0)),
                  pl.BlockSpec((1, rows, LANES), lambda b, c: (b, c, 0)),
                  full(conv_w), full(gbias), full(nal), full(nal_col), full(ng)],
        out_specs=pl.BlockSpec((1, rows, GD_DIM), lambda b, c: (b, c, 0)),
        out_shape=jax.ShapeDtypeStruct((bsz, t, GD_DIM), F32),
        scratch_shapes=[pltpu.VMEM((8, 3 * GD_DIM), F32), pltpu.VMEM((GD_HEADS, LANES, LANES), F32)],
        compiler_params=_cparams("arbitrary", "arbitrary"),
        name="gdn_mix",
    )(p, pg, conv_w.astype(F32), gbias, nal, nal_col, ng)


def _mlstm_kernel(p_ref, pg_ref, gb_ref, ng_ref, o_ref, c_ref, n_ref, m_ref):
    L = ML_CHUNK
    ci = pl.program_id(1)

    @pl.when(ci == 0)
    def _():
        c_ref[...] = jnp.zeros_like(c_ref)
        n_ref[...] = jnp.zeros_like(n_ref)
        m_ref[...] = jnp.zeros_like(m_ref)

    x = p_ref[0]
    R = x.shape[0]
    nch = R // L
    gt = pg_ref[0] + gb_ref[...]
    logf = jax.nn.log_sigmoid(gt)
    gt_t = gt.T
    logf_t = jax.nn.log_sigmoid(gt_t)
    tr, tc = _iota2((R, R))
    same = (tr // L) == (tc // L)
    bc_all = _cumsum_rows(jnp.where(same & (tc <= tr), 1.0, 0.0).astype(BF16), logf)
    bc_t = _cumsum_cols(logf_t, jnp.where(same & (tr <= tc), 1.0, 0.0).astype(BF16))
    lr, lc = _iota2((L, L))
    incl = lc <= lr
    i_lo, f_lo = 2 * GD_HEADS, 2 * GD_HEADS + ML_HEADS

    units = []
    m_run = [m_ref[h][0:1, 0:1] for h in range(ML_HEADS)]
    for ci in range(nch):
        rs = slice(ci * L, (ci + 1) * L)
        for h in range(ML_HEADS):
            sl = slice(h * LANES, (h + 1) * LANES)
            q = x[rs, sl]
            k = x[rs, ML_DIM + h * LANES:ML_DIM + (h + 1) * LANES] * (ML_HEAD_DIM ** -0.5)
            v = x[rs, 2 * ML_DIM + h * LANES:2 * ML_DIM + (h + 1) * LANES]
            bc = bc_all[rs, f_lo + h:f_lo + h + 1]
            bc_row = bc_t[f_lo + h:f_lo + h + 1, rs]
            ig = gt[rs, i_lo + h:i_lo + h + 1]
            ig_row = gt_t[i_lo + h:i_lo + h + 1, rs]
            m_prev = m_run[h]
            b_last = bc[L - 1:L, :]
            d_log = jnp.where(incl, bc - bc_row + ig_row, NEG)
            inter_log = bc + m_prev
            m_t = jnp.maximum(inter_log, jnp.max(d_log, axis=-1, keepdims=True))
            upd_log = b_last - bc + ig
            m_new = jnp.maximum(b_last + m_prev, jnp.max(upd_log, axis=0, keepdims=True))
            m_run[h] = m_new
            wk = jnp.exp(upd_log - m_new) * k
            units.append(dict(h=h, rs=rs, sl=sl, q=q, v=v, m_t=m_t, wk=wk, inter_w=jnp.exp(inter_log - m_t),
                              dec=jnp.exp(b_last + m_prev - m_new), sc=_mm_nt(q, k) * jnp.exp(d_log - m_t)))
    for hd in units:
        hd["scv"] = _mm(hd["sc"], hd["v"])
        hd["kv"] = _mm_tn(hd["wk"], hd["v"])
    cs = [c_ref[h] for h in range(ML_HEADS)]
    ns = [n_ref[h][0:1, :] for h in range(ML_HEADS)]
    for hd in units:
        h = hd["h"]
        q, sc, inter_w, dec = hd["q"], hd["sc"], hd["inter_w"], hd["dec"]
        num = inter_w * _mm(q, cs[h]) + hd["scv"]
        den = inter_w * jnp.sum(q * ns[h], axis=-1, keepdims=True) + jnp.sum(sc, axis=-1, keepdims=True)
        hd["hh"] = num / jnp.maximum(jnp.abs(den), jnp.exp(-hd["m_t"]))
        cs[h] = dec * cs[h] + hd["kv"]
        ns[h] = dec * ns[h] + jnp.sum(hd["wk"], axis=0, keepdims=True)
    for h in range(ML_HEADS):
        c_ref[h] = cs[h]
        n_ref[h] = jnp.broadcast_to(ns[h], (8, LANES))
        m_ref[h] = jnp.broadcast_to(m_run[h], (8, LANES))
    for hd in units:
        h, rs, sl = hd["h"], hd["rs"], hd["sl"]
        og = jax.nn.sigmoid(x[rs, 3 * ML_DIM + h * LANES:3 * ML_DIM + (h + 1) * LANES])
        o_ref[0, rs, sl] = og * _rms_rows(hd["hh"], ng_ref[:, sl])


def _mlstm(p, pg, i_bias, f_bias, norm_g):
    bsz, t, _ = p.shape
    gbias = _gate_row(2 * GD_HEADS, jnp.concatenate([i_bias, f_bias]))
    ng = norm_g.reshape(1, ML_DIM).astype(F32)
    lc = ML_NCH * ML_CHUNK
    full = lambda z: pl.BlockSpec(z.shape, lambda b, c: (0, 0))
    return pl.pallas_call(
        _mlstm_kernel,
        grid=(bsz, t // lc),
        in_specs=[pl.BlockSpec((1, lc, ML_MAIN), lambda b, c: (b, c, 0)),
                  pl.BlockSpec((1, lc, LANES), lambda b, c: (b, c, 0)), full(gbias), full(ng)],
        out_specs=pl.BlockSpec((1, lc, ML_DIM), lambda b, c: (b, c, 0)),
        out_shape=jax.ShapeDtypeStruct((bsz, t, ML_DIM), F32),
        scratch_shapes=[pltpu.VMEM((ML_HEADS, LANES, LANES), F32), pltpu.VMEM((ML_HEADS, 8, LANES), F32),
                        pltpu.VMEM((ML_HEADS, 8, LANES), F32)],
        compiler_params=_cparams("arbitrary", "arbitrary"),
        name="mlstm_mix",
    )(p, pg, gbias, ng)


def _route(h, g_ref, wh_ref, wl_ref, b_ref, off_ref):
    tm = h.shape[0]
    xn = _rms_rows(h, g_ref[...])
    xh = xn.astype(BF16)
    xl = (xn - xh.astype(F32)).astype(BF16)
    dot = lambda a, b: jnp.dot(a, b, preferred_element_type=F32)
    logits = dot(xh, wh_ref[...]) + dot(xh, wl_ref[...]) + dot(xl, wh_ref[...]) + b_ref[...]
    lt = logits.T[0:ROUTE_ROWS, :]
    row = lax.broadcasted_iota(I32, lt.shape, 0)
    big = jnp.int32(1 << 20)
    is_grp = (row >= N_EXPERTS) & (row < N_EXPERTS + N_GROUPS)
    lg = jnp.where(is_grp, lt, NEG)
    gmax = jnp.max(lg, axis=0, keepdims=True)
    p_top = 1.0 / jnp.sum(jnp.exp(lg - gmax), axis=0, keepdims=True)
    g_idx = jnp.min(jnp.where(lg == gmax, row, big), axis=0, keepdims=True) - N_EXPERTS
    valid = (row < N_EXPERTS) & ((row // EXPERTS_PER_GROUP) == g_idx)
    v1 = jnp.where(valid, lt, NEG)
    m1 = jnp.max(v1, axis=0, keepdims=True)
    i1 = jnp.min(jnp.where(v1 == m1, row, big), axis=0, keepdims=True)
    v2 = jnp.where(row == i1, NEG, v1)
    m2 = jnp.max(v2, axis=0, keepdims=True)
    i2 = jnp.min(jnp.where(v2 == m2, row, big), axis=0, keepdims=True)
    e21 = jnp.exp(m2 - m1)
    gate1 = p_top / (1.0 + e21)
    gate2 = p_top * e21 / (1.0 + e21)

    sel1 = row == i1
    sel2 = row == i2
    onehot = jnp.where(sel1 | sel2, 1.0, 0.0)
    tr, tc = _iota2((tm, tm))
    earlier = jnp.where(tr < tc, 1.0, 0.0).astype(BF16)
    off = off_ref[0:ROUTE_ROWS, 0:1]
    prefix = jnp.dot(onehot.astype(BF16), earlier, preferred_element_type=F32) + off
    rank1 = jnp.sum(jnp.where(sel1, prefix, 0.0), axis=0, keepdims=True)
    rank2 = jnp.sum(jnp.where(sel2, prefix, 0.0), axis=0, keepdims=True)
    off_ref[0:ROUTE_ROWS, :] = jnp.broadcast_to(off + jnp.sum(onehot, axis=1, keepdims=True), (ROUTE_ROWS, LANES))

    packed = jnp.concatenate([i1.astype(F32), i2.astype(F32), rank1, rank2, gate1, gate2,
                              jnp.zeros((2, tm), F32)], axis=0)
    return xn, packed


def _out_proj_route_kernel(h_ref, ya_ref, yb_ref, wa_ref, wb_ref, g_ref, wh_ref, wl_ref, b_ref,
                           o_ref, xt_ref, rg_ref, cnt_ref, dest_ref, off_ref, rt_ref):
    phase = pl.program_id(0)
    s = pl.program_id(1)
    tm = h_ref.shape[0]

    @pl.when((phase == 0) & (s == 0))
    def _():
        off_ref[...] = jnp.zeros_like(off_ref)

    @pl.when(phase == 0)
    def _():
        out = h_ref[...] + _mm(ya_ref[...], wa_ref[...]) + _mm(yb_ref[...], wb_ref[...])
        o_ref[...] = out
        xn, packed = _route(out, g_ref, wh_ref, wl_ref, b_ref, off_ref)
        _store_row_tiles(xt_ref, xn)
        rt_ref[s] = packed
        wide = jnp.concatenate([packed, jnp.zeros((LANES - 8, tm), F32)], axis=0)
        rg_ref[...] = wide.T
        cnt_ref[...] = off_ref[...]

    @pl.when(phase == 1)
    def _():
        counts = off_ref[...]
        tiles_per = jnp.floor((counts + (MOE_TM - 1)) * (1.0 / MOE_TM))
        er, ec = _iota2((LANES, LANES))
        tile_end = _mm(jnp.where(ec <= er, 1.0, 0.0), tiles_per)
        seg_start = ((tile_end - tiles_per) * MOE_TM)[0:ROUTE_ROWS, 0:1]
        blk = rt_ref[s]
        row = lax.broadcasted_iota(I32, (ROUTE_ROWS, tm), 0).astype(F32)
        pick = lambda e: jnp.sum(jnp.where(row == e, seg_start, 0.0), axis=0, keepdims=True)
        d1 = pick(blk[0:1]) + blk[2:3]
        d2 = pick(blk[1:2]) + blk[3:4]
        dest_ref[...] = jnp.concatenate([d1, d2, jnp.zeros((6, tm), F32)], axis=0).astype(I32)


def _out_proj_route(h, ya, yb, w_out, g_ffn, w_group, b_group, w_router, b_router):
    n, d = h.shape
    da, db = ya.shape[1], yb.shape[1]
    tm = min(PROJ_TM, n)
    wa = w_out[:da].astype(BF16)
    wb = w_out[da:].astype(BF16)
    pad = LANES - N_EXPERTS - N_GROUPS
    w_cat = jnp.concatenate([w_router, w_group, jnp.zeros((d, pad), F32)], axis=1)
    b_cat = jnp.concatenate([b_router, b_group, jnp.zeros((pad,), F32)]).reshape(1, LANES)
    w_hi = w_cat.astype(BF16)
    w_lo = (w_cat - w_hi.astype(F32)).astype(BF16)
    steps = n // tm
    row = lambda p, s: (s * (1 - p) + (steps - 1) * p, 0)
    fixed = lambda p, s: (0, 0)
    return pl.pallas_call(
        _out_proj_route_kernel,
        grid=(2, steps),
        in_specs=[pl.BlockSpec((tm, d), row), pl.BlockSpec((tm, da), row), pl.BlockSpec((tm, db), row),
                  pl.BlockSpec((da, d), fixed), pl.BlockSpec((db, d), fixed), pl.BlockSpec((1, d), fixed),
                  pl.BlockSpec((d, LANES), fixed), pl.BlockSpec((d, LANES), fixed), pl.BlockSpec((1, LANES), fixed)],
        out_specs=[pl.BlockSpec((tm, d), row), pl.BlockSpec((tm * ROW_TILE, LANES), row),
                   pl.BlockSpec((tm, LANES), row), pl.BlockSpec((LANES, LANES), fixed),
                   pl.BlockSpec((8, tm), lambda p, s: (0, s * p))],
        out_shape=[jax.ShapeDtypeStruct((n, d), F32), jax.ShapeDtypeStruct((n * ROW_TILE, LANES), F32),
                   jax.ShapeDtypeStruct((n, LANES), F32), jax.ShapeDtypeStruct((LANES, LANES), F32),
                   jax.ShapeDtypeStruct((8, n), I32)],
        scratch_shapes=[pltpu.VMEM((LANES, LANES), F32), pltpu.VMEM((steps, 8, tm), F32)],
        compiler_params=_cparams("arbitrary", "arbitrary"),
        name="out_proj_route",
    )(h, ya, yb, wa, wb, g_ffn.reshape(1, d), w_hi, w_lo, b_cat)


def _gather_rows(src_hbm, idx_ref, base, dst, sem):
    def body(j, carry):
        r = pl.multiple_of(idx_ref[base + j] * ROW_TILE, ROW_TILE)
        pltpu.make_async_copy(src_hbm.at[pl.ds(r, ROW_TILE), :],
                              dst.at[pl.ds(pl.multiple_of(j * ROW_TILE, ROW_TILE), ROW_TILE), :],
                              sem).start()
        return carry

    lax.fori_loop(0, dst.shape[0] // ROW_TILE, body, 0, unroll=GATHER_UNROLL)


def _wait_rows(src_hbm, dst, sem):
    pltpu.make_async_copy(src_hbm.at[pl.ds(0, dst.shape[0]), :], dst, sem).wait()


def _dispatch_kernel(d1_ref, d2_ref, lo_ref, hi_ref, x_ref, xs_hbm, zero_ref, sem):
    rows = x_ref.shape[0]
    base = pl.program_id(0) * (rows // ROW_TILE)

    def pad_copy(p):
        return pltpu.make_async_copy(zero_ref, xs_hbm.at[pl.ds(pl.multiple_of(p * ROW_TILE, ROW_TILE), ROW_TILE), :],
                                     sem.at[1])

    @pl.when(pl.program_id(0) == 0)
    def _():
        zero_ref[...] = jnp.zeros_like(zero_ref)

        def start(p, carry):
            pad_copy(p).start()
            return carry

        def wait(p, carry):
            pad_copy(p).wait()
            return carry

        for e in range(lo_ref.shape[0]):
            lax.fori_loop(lo_ref[e], hi_ref[e], start, 0)
        for e in range(lo_ref.shape[0]):
            lax.fori_loop(lo_ref[e], hi_ref[e], wait, 0)

    for idx_ref in (d1_ref, d2_ref):
        def body(j, carry, idx_ref=idx_ref):
            r = pl.multiple_of(idx_ref[base + j] * ROW_TILE, ROW_TILE)
            pltpu.make_async_copy(x_ref.at[pl.ds(pl.multiple_of(j * ROW_TILE, ROW_TILE), ROW_TILE), :],
                                  xs_hbm.at[pl.ds(r, ROW_TILE), :], sem.at[0]).start()
            return carry

        lax.fori_loop(0, rows // ROW_TILE, body, 0, unroll=GATHER_UNROLL)
    for _ in range(2):
        pltpu.make_async_copy(x_ref, xs_hbm.at[pl.ds(0, rows), :], sem.at[0]).wait()


def _dispatch(xn_tiles, d1, d2, pad_lo, pad_hi, n_rows):
    n = d1.shape[0]
    tm = min(PROJ_TM, n)
    grid_spec = pltpu.PrefetchScalarGridSpec(
        num_scalar_prefetch=4,
        grid=(n // tm,),
        in_specs=[pl.BlockSpec((tm * ROW_TILE, LANES), lambda i, *_: (i, 0))],
        out_specs=pl.BlockSpec(memory_space=pl.ANY),
        scratch_shapes=[pltpu.VMEM((ROW_TILE, LANES), F32), pltpu.SemaphoreType.DMA((2,))],
    )
    return pl.pallas_call(
        _dispatch_kernel,
        grid_spec=grid_spec,
        out_shape=jax.ShapeDtypeStruct((n_rows * ROW_TILE, LANES), F32),
        compiler_params=_cparams("arbitrary"),
        name="moe_dispatch",
    )(d1, d2, pad_lo, pad_hi, xn_tiles)


def _expert_kernel(te_ref, nt_ref, x_ref, wg_ref, wu_ref, wd_ref, o_ref, wgb, wub, wdb):
    tm = x_ref.shape[0] // ROW_TILE
    i = pl.program_id(0)

    @pl.when((i == 0) | (te_ref[i] != te_ref[jnp.maximum(i - 1, 0)]))
    def _():
        wgb[...] = wg_ref[0].astype(BF16)
        wub[...] = wu_ref[0].astype(BF16)
        wdb[...] = wd_ref[0].astype(BF16)

    @pl.when(i < nt_ref[0])
    def _():
        xn = _load_row_tiles(x_ref, tm).astype(BF16)
        gate = jnp.dot(xn, wgb[...], preferred_element_type=F32)
        up = jnp.dot(xn, wub[...], preferred_element_type=F32)
        hid = (jax.nn.silu(gate) * up).astype(BF16)
        _store_row_tiles(o_ref, jnp.dot(hid, wdb[...], preferred_element_type=F32))

    @pl.when(i >= nt_ref[0])
    def _():
        o_ref[...] = jnp.zeros_like(o_ref)


def _experts(xs, tile_expert, n_used, w_gate, w_up, w_down):
    d = D_MODEL
    n_tiles = tile_expert.shape[0]
    tm = MOE_TM
    de = w_gate.shape[-1]
    wspec = lambda rows, cols: pl.BlockSpec((1, rows, cols), lambda i, te, nt: (te[i], 0, 0))
    grid_spec = pltpu.PrefetchScalarGridSpec(
        num_scalar_prefetch=2,
        grid=(n_tiles,),
        in_specs=[pl.BlockSpec((tm * ROW_TILE, LANES), lambda i, te, nt: (jnp.minimum(i, nt[0] - 1), 0)),
                  wspec(d, de), wspec(d, de), wspec(de, d)],
        out_specs=pl.BlockSpec((tm * ROW_TILE, LANES), lambda i, te, nt: (i, 0)),
        scratch_shapes=[pltpu.VMEM((d, de), BF16), pltpu.VMEM((d, de), BF16), pltpu.VMEM((de, d), BF16)],
    )
    return pl.pallas_call(
        _expert_kernel,
        grid_spec=grid_spec,
        out_shape=jax.ShapeDtypeStruct((n_tiles * tm * ROW_TILE, LANES), F32),
        compiler_params=_cparams("arbitrary"),
        name="moe_experts",
    )(tile_expert, n_used, xs, w_gate, w_up, w_down)


def _combine_kernel(d1_ref, d2_ref, ys_hbm, h_ref, rg_ref, g_ref, *refs, n_proj, final_norm):
    w_refs, o_ref, p_refs = refs[:n_proj], refs[n_proj], refs[n_proj + 1:2 * n_proj + 1]
    y1, y2, sem = refs[2 * n_proj + 1:]
    tm = h_ref.shape[0]
    i = pl.program_id(0)
    slot = i % 2

    def gather(tile, s):
        _gather_rows(ys_hbm, d1_ref, tile * tm, y1.at[s], sem.at[0, s])
        _gather_rows(ys_hbm, d2_ref, tile * tm, y2.at[s], sem.at[1, s])

    @pl.when(i == 0)
    def _():
        gather(0, 0)

    @pl.when(i + 1 < pl.num_programs(0))
    def _():
        gather(i + 1, 1 - slot)

    _wait_rows(ys_hbm, y1.at[slot], sem.at[0, slot])
    _wait_rows(ys_hbm, y2.at[slot], sem.at[1, slot])
    rg = rg_ref[...]
    out = (h_ref[...] + rg[:, 4:5] * _load_row_tiles(y1.at[slot], tm)
           + rg[:, 5:6] * _load_row_tiles(y2.at[slot], tm))
    if final_norm:
        out = _rms_rows(out, g_ref[...])
    o_ref[...] = out
    if n_proj:
        y = _rms_rows(out, g_ref[...]).astype(BF16)
        for w_ref, p_ref in zip(w_refs, p_refs):
            p_ref[...] = jnp.dot(y, w_ref[...], preferred_element_type=F32)


def _combine(h, ys, d1, d2, rg, g, ws=(), final_norm=False):
    n, d = h.shape
    tm = min(PROJ_TM, n)
    row = lambda i, a, b: (i, 0)
    fixed = lambda i, a, b: (0, 0)
    grid_spec = pltpu.PrefetchScalarGridSpec(
        num_scalar_prefetch=2,
        grid=(n // tm,),
        in_specs=[pl.BlockSpec(memory_space=pl.ANY), pl.BlockSpec((tm, d), row), pl.BlockSpec((tm, LANES), row),
                  pl.BlockSpec((1, d), fixed)] + [pl.BlockSpec(w.shape, fixed) for w in ws],
        out_specs=[pl.BlockSpec((tm, d), row)] + [pl.BlockSpec((tm, w.shape[1]), row) for w in ws],
        scratch_shapes=[pltpu.VMEM((2, tm * ROW_TILE, LANES), F32), pltpu.VMEM((2, tm * ROW_TILE, LANES), F32),
                        pltpu.SemaphoreType.DMA((2, 2))],
    )
    return pl.pallas_call(
        functools.partial(_combine_kernel, n_proj=len(ws), final_norm=final_norm),
        grid_spec=grid_spec,
        out_shape=[jax.ShapeDtypeStruct((n, d), F32)] + [jax.ShapeDtypeStruct((n, w.shape[1]), F32) for w in ws],
        compiler_params=_cparams("arbitrary"),
        name="moe_combine",
    )(d1, d2, ys, h, rg, g.reshape(1, d), *ws)


def _hmoe_residual(h, xn_tiles, rg, cnt, dest, layer, w_gate, w_up, w_down, g, ws=(), final_norm=False):
    n, d = h.shape
    stack = lambda w: w.reshape((-1,) + w.shape[2:])

    counts = cnt[:N_EXPERTS, 0].astype(I32)
    n_tiles = (2 * n) // MOE_TM + N_EXPERTS
    tile_end = jnp.cumsum((counts + MOE_TM - 1) // MOE_TM)
    d1, d2 = dest[0], dest[1]
    tile_ids = jnp.arange(n_tiles, dtype=I32)
    tile_expert = jnp.minimum(jnp.sum((tile_end[None, :] <= tile_ids[:, None]).astype(I32), axis=1), N_EXPERTS - 1)
    tile_expert = tile_expert + layer * N_EXPERTS
    seg_end = tile_end * MOE_TM
    seg_start = seg_end - ((counts + MOE_TM - 1) // MOE_TM) * MOE_TM
    pad_lo = (seg_start + counts).astype(I32)
    pad_hi = seg_end.astype(I32)
    xs = _dispatch(xn_tiles, d1, d2, pad_lo, pad_hi, n_tiles * MOE_TM)
    n_used = tile_end[N_EXPERTS - 1:].astype(I32)
    ys = _experts(xs, tile_expert, n_used, stack(w_gate), stack(w_up), stack(w_down))
    return _combine(h, ys, d1, d2, rg, g, ws, final_norm)


def kernel(x, norm_mix, norm_ffn, norm_final, ev_w_in, ev_mu, rw_w0, rw_w2, rw_a0, rw_a2, rw_g2, rw_k_k, rw_k_a, rw_r_k, rw_ln_w, rw_ln_b, hg_lb_logits, hg_norm, ev_w_out, od_w_in, gd_conv, gd_a_log, gd_dt_bias, gd_norm, ml_i_bias, ml_f_bias, ml_norm, od_w_out, moe_w_group, moe_b_group, moe_w_router, moe_b_router, moe_w_gate, moe_w_up, moe_w_down):
    bsz, t, d = x.shape
    n = bsz * t
    depth = norm_mix.shape[0]
    lb_table = jnp.cumsum(jax.nn.softmax(hg_lb_logits.astype(F32), axis=0), axis=0)
    def in_proj_weights(layer):
        j = layer // 2
        if layer % 2 == 0:
            return _pack_weights(ev_w_in[j], [(0, RW_IN), (RW_IN, RW_IN + HG_IN)])
        return _pack_weights(od_w_in[j], [(0, GD_MAIN), (GD_IN, GD_IN + ML_MAIN)],
                             gate_ranges=[(GD_MAIN, GD_IN), (GD_IN + ML_MAIN, GD_IN + ML_MAIN + 2 * ML_HEADS)])

    h = x.reshape(n, d)
    proj = _rms_proj(h, norm_mix[0], in_proj_weights(0))
    for layer in range(depth):
        j = layer // 2
        if layer % 2 == 0:
            p_rw, p_hg = proj
            ya = _rwkv7(p_rw.reshape(bsz, t, RW_IN), ev_mu[j], rw_w0[j], rw_w2[j], rw_a0[j], rw_a2[j], rw_g2[j],
                        rw_k_k[j], rw_k_a[j], rw_r_k[j], rw_ln_w[j], rw_ln_b[j])
            yb = _hgrn2(p_hg.reshape(bsz, t, HG_IN), lb_table[j], hg_norm[j])
            mix_a, mix_b, w_out = ya.reshape(n, RW_DIM), yb.reshape(n, HG_KDIM), ev_w_out[j]
        else:
            p_gd, p_ml, p_gt = proj
            p_gt = p_gt.reshape(bsz, t, LANES)
            yc = _gdn(p_gd.reshape(bsz, t, GD_MAIN), p_gt, gd_conv[j], gd_a_log[j], gd_dt_bias[j], gd_norm[j])
            yd = _mlstm(p_ml.reshape(bsz, t, ML_MAIN), p_gt, ml_i_bias[j], ml_f_bias[j], ml_norm[j])
            mix_a, mix_b, w_out = yc.reshape(n, GD_DIM), yd.reshape(n, ML_DIM), od_w_out[j]
        h, xn_tiles, rg, cnt, dest = _out_proj_route(h, mix_a, mix_b, w_out, norm_ffn[layer], moe_w_group[layer],
                                                     moe_b_group[layer], moe_w_router[layer], moe_b_router[layer])
        if layer == depth - 1:
            (h,) = _hmoe_residual(h, xn_tiles, rg, cnt, dest, layer, moe_w_gate, moe_w_up, moe_w_down,
                                  norm_final, final_norm=True)
        else:
            h, *proj = _hmoe_residual(h, xn_tiles, rg, cnt, dest, layer, moe_w_gate, moe_w_up, moe_w_down,
                                      norm_mix[layer + 1], ws=in_proj_weights(layer + 1))
    return h.reshape(bsz, t, d)
```

```python
import functools
import math

import jax
import jax.numpy as jnp
from jax import lax
from jax.experimental import pallas as pl
from jax.experimental.pallas import tpu as pltpu

F32 = jnp.float32
BF16 = jnp.bfloat16
I32 = jnp.int32
HIGHEST = lax.Precision.HIGHEST

D_MODEL = 1024
NORM_EPS = 1e-6
RW_HEADS, RW_HEAD_DIM = 8, 64
RW_DIM = RW_HEADS * RW_HEAD_DIM
R_DECAY, R_AAA, R_GATE = 64, 64, 128
RW_IN = 3 * RW_DIM + R_DECAY + R_AAA + R_GATE
RW_LN_EPS = 64e-5
HG_HEADS, HG_DIM = 8, 64
HG_KDIM = HG_HEADS * HG_DIM
HG_IN = 4 * HG_KDIM
GD_HEADS, GD_HEAD_DIM = 4, 128
GD_DIM = GD_HEADS * GD_HEAD_DIM
CONV_K = 4
GD_MAIN = 4 * GD_DIM
GD_IN = GD_MAIN + 2 * GD_HEADS
ML_HEADS, ML_HEAD_DIM = 4, 128
ML_DIM = ML_HEADS * ML_HEAD_DIM
ML_MAIN = 4 * ML_DIM
N_GROUPS, EXPERTS_PER_GROUP = 4, 8
N_EXPERTS = N_GROUPS * EXPERTS_PER_GROUP
D_EXPERT = 256

LANES = 128
VMEM_LIMIT_BYTES = 48 * 1024 * 1024

PROJ_TM = 256
PACK_TM = 128
CHUNK = 64
SUB = 16
RW_NCH = 4
GD_NCH = 8
HG_ROWS = 128
ML_CHUNK = 128
ML_NCH = 2
MOE_TM = 256
ROUTE_ROWS = 40
GATHER_UNROLL = 8
NEG = -1e30


def _cparams(*sem):
    return pltpu.CompilerParams(dimension_semantics=sem, vmem_limit_bytes=VMEM_LIMIT_BYTES)


def _mm(a, b):
    return jnp.dot(a.astype(BF16), b.astype(BF16), preferred_element_type=F32)


def _mm_nt(a, b):
    return lax.dot_general(a.astype(BF16), b.astype(BF16), (((1,), (1,)), ((), ())), preferred_element_type=F32)


def _mm_tn(a, b):
    return lax.dot_general(a.astype(BF16), b.astype(BF16), (((0,), (0,)), ((), ())), preferred_element_type=F32)


def _mm_hi(a, b):
    return jnp.dot(a, b, precision=HIGHEST, preferred_element_type=F32)


def _mm_split(x, ones_bf16):
    hi = x.astype(BF16)
    lo = (x - hi.astype(F32)).astype(BF16)
    return (jnp.dot(hi, ones_bf16, preferred_element_type=F32) + jnp.dot(lo, ones_bf16, preferred_element_type=F32))


def _iota2(shape):
    return lax.broadcasted_iota(I32, shape, 0), lax.broadcasted_iota(I32, shape, 1)


def _rms_rows(x, g, eps=NORM_EPS):
    return x * lax.rsqrt(jnp.mean(x * x, axis=-1, keepdims=True) + eps) * g


def _seg_ones(width, seg):
    r, c = _iota2((width, width))
    return jnp.where((r // seg) == (c // seg), 1.0, 0.0).astype(BF16)


def _split3(x):
    x1 = x.astype(BF16)
    r1 = x - x1.astype(F32)
    x2 = r1.astype(BF16)
    return x1, x2, (r1 - x2.astype(F32)).astype(BF16)


def _cumsum_rows(tri_bf16, x):
    return sum(jnp.dot(tri_bf16, t, preferred_element_type=F32) for t in _split3(x))


def _cumsum_cols(x, tri_bf16):
    return sum(jnp.dot(t, tri_bf16, preferred_element_type=F32) for t in _split3(x))


def _mm3(a, b):
    ah = a.astype(BF16)
    al = (a - ah.astype(F32)).astype(BF16)
    bh = b.astype(BF16)
    bl = (b - bh.astype(F32)).astype(BF16)
    dot = lambda x, y: jnp.dot(x, y, preferred_element_type=F32)
    return dot(ah, bh) + dot(ah, bl) + dot(al, bh)


def _tri_inv_multi(ms, n, chain):
    assert chain // SUB <= 4
    r, c = _iota2((n, n))
    same = (r // SUB) == (c // SUB)
    eye = jnp.where(r == c, 1.0, 0.0).astype(F32)
    ds = [jnp.where(same, m, 0.0) for m in ms]
    offs = [m - d for m, d in zip(ms, ds)]
    xs = [eye - d for d in ds]
    ps = ds
    for _ in range(3):
        ps = [_mm(p, p) for p in ps]
        xs = [x + _mm(x, p) for x, p in zip(xs, ps)]
    es = [_mm(x, o) for x, o in zip(xs, offs)]
    imes = [eye - e for e in es]
    e2s = [_mm(e, e) for e in es]
    ys = [i + _mm(i, e2) for i, e2 in zip(imes, e2s)]
    xs = [_mm(y, x) for y, x in zip(ys, xs)]
    res = [eye - x - _mm3(m, x) for m, x in zip(ms, xs)]
    return [x + _mm(x, rr) for x, rr in zip(xs, res)]


def _pack_weights_kernel(w_ref, *o_refs, ranges, gate_ranges):
    w = w_ref[...]
    for (lo, hi), o_ref in zip(ranges, o_refs):
        o_ref[...] = w[:, lo:hi].astype(BF16)
    if gate_ranges:
        cols = [w[:, lo:hi] for lo, hi in gate_ranges]
        used = sum(hi - lo for lo, hi in gate_ranges)
        cols.append(jnp.zeros((w.shape[0], LANES - used), F32))
        o_refs[-1][...] = jnp.concatenate(cols, axis=1).astype(BF16)


def _pack_weights(w, ranges, gate_ranges=()):
    rows, cols = w.shape
    tm = PACK_TM
    widths = [hi - lo for lo, hi in ranges] + ([LANES] if gate_ranges else [])
    return pl.pallas_call(
        functools.partial(_pack_weights_kernel, ranges=tuple(ranges), gate_ranges=tuple(gate_ranges)),
        grid=(rows // tm,),
        in_specs=[pl.BlockSpec((tm, cols), lambda i: (i, 0))],
        out_specs=[pl.BlockSpec((tm, wd), lambda i: (i, 0)) for wd in widths],
        out_shape=[jax.ShapeDtypeStruct((rows, wd), BF16) for wd in widths],
        compiler_params=_cparams("parallel"),
        name="pack_weights",
    )(w)


def _rms_proj_kernel(x_ref, g_ref, *refs, n_out):
    y = _rms_rows(x_ref[...], g_ref[...]).astype(BF16)
    for w_ref, o_ref in zip(refs[:n_out], refs[n_out:]):
        o_ref[...] = jnp.dot(y, w_ref[...], preferred_element_type=F32)


def _rms_proj(x, g, ws):
    n, d = x.shape
    tm = min(PROJ_TM, n)
    in_specs = [pl.BlockSpec((tm, d), lambda i: (i, 0)), pl.BlockSpec((1, d), lambda i: (0, 0))]
    in_specs += [pl.BlockSpec(w.shape, lambda i: (0, 0)) for w in ws]
    return pl.pallas_call(
        functools.partial(_rms_proj_kernel, n_out=len(ws)),
        grid=(n // tm,),
        in_specs=in_specs,
        out_specs=[pl.BlockSpec((tm, w.shape[1]), lambda i: (i, 0)) for w in ws],
        out_shape=[jax.ShapeDtypeStruct((n, w.shape[1]), F32) for w in ws],
        compiler_params=_cparams("parallel"),
        name="rms_proj",
    )(x, g.reshape(1, d), *ws)


ROW_TILE = D_MODEL // LANES


def _store_row_tiles(ref, x):
    for j in range(ROW_TILE):
        ref[pl.ds(j, x.shape[0], stride=ROW_TILE), :] = x[:, j * LANES:(j + 1) * LANES]


def _load_row_tiles(ref, rows):
    return jnp.concatenate([ref[pl.ds(j, rows, stride=ROW_TILE), :] for j in range(ROW_TILE)], axis=1)


def _rwkv7_kernel(p_ref, mu_ref, w0_ref, w2_ref, a0_ref, a2_ref, g2_ref, kk_ref, ka_ref, rk_ref,
                  lnw_ref, lnb_ref, o_ref, prev_ref, zt_ref):
    L = CHUNK
    npair = RW_HEADS // 2
    c = pl.program_id(1)

    @pl.when(c == 0)
    def _():
        prev_ref[...] = jnp.zeros_like(prev_ref)
        zt_ref[...] = jnp.zeros_like(zt_ref)

    x = p_ref[0]
    R = x.shape[0]
    nch = R // L
    row = lax.broadcasted_iota(I32, x.shape, 0)
    xs = jnp.where(row == 0, prev_ref[7:8, :], pltpu.roll(x, 1, 0))
    prev_ref[...] = x[R - 8:R, :]
    pm = x + mu_ref[...] * (xs - x)
    r_all = pm[:, 0:RW_DIM]
    k_all = pm[:, RW_DIM:2 * RW_DIM]
    v_all = pm[:, 2 * RW_DIM:3 * RW_DIM]
    wa = pm[:, 3 * RW_DIM:3 * RW_DIM + LANES]
    gl = pm[:, 3 * RW_DIM + LANES:]
    wlog = -jax.nn.softplus(-(w0_ref[...] + _mm(jnp.tanh(wa), w2_ref[...]))) - 0.5
    ld = -jnp.exp(wlog)
    a_all = jax.nn.sigmoid(a0_ref[...] + _mm(wa, a2_ref[...]))
    g_all = _mm(jax.nn.sigmoid(gl), g2_ref[...])

    tr, tc = _iota2((R, R))
    chunk_tril = jnp.where((tc <= tr) & ((tr // L) == (tc // L)), 1.0, 0.0).astype(BF16)
    cs_all = _cumsum_rows(chunk_tril, ld)
    seg = _seg_ones(LANES, RW_HEAD_DIM)
    lane = lax.broadcasted_iota(I32, (L, LANES), 1)
    hm = (lane < RW_HEAD_DIM, lane >= RW_HEAD_DIM)
    br, bc = _iota2((2 * L, 2 * L))
    bd = (br // L) == (bc // L)
    bd_strict = bd & (bc < br)
    bd_incl = bd & (bc <= br)
    fold = lambda z: z[0:L] + z[L:2 * L]
    both = lambda z: jnp.concatenate([jnp.where(hm[0], z, 0.0), jnp.where(hm[1], z, 0.0)], axis=0)

    units = []
    for ci in range(nch):
        rs = slice(ci * L, (ci + 1) * L)
        for p in range(npair):
            sl = slice(p * LANES, (p + 1) * LANES)
            r, k, v, a = r_all[rs, sl], k_all[rs, sl], v_all[rs, sl], a_all[rs, sl]
            cs, ldp = cs_all[rs, sl], ld[rs, sl]
            kkr = k * kk_ref[:, sl]
            kk = kkr * lax.rsqrt(_mm_split(kkr * kkr, seg) + 1e-6)
            k2 = k * (1.0 + (a - 1.0) * ka_ref[:, sl])
            b = kk * a
            cs_last = cs[L - 1:L, :]
            e_neg = jnp.exp(-cs)
            e_rem = jnp.exp(cs_last - cs)
            bhat = b * e_neg
            khat = k2 * e_neg
            units.append(dict(p=p, rs=rs, sl=sl, r=r, v=v, k2=k2, rhat=r * jnp.exp(cs), gam_last=jnp.exp(cs_last),
                              btil=b * e_rem, ktil=k2 * e_rem, a2=both(kk * jnp.exp(cs - ldp)), v2=both(v),
                              rhs4=jnp.concatenate([bhat, bhat, khat, khat], axis=0)))
    for q in units:
        lhs = jnp.concatenate([q["a2"], both(q["rhat"])], axis=0)
        q["g"] = _mm_nt(lhs, q["rhs4"])
    tinvs = _tri_inv_multi([jnp.where(bd_strict, q["g"][0:2 * L, 0:2 * L], 0.0) for q in units], 2 * L, L)
    for q, tinv in zip(units, tinvs):
        q["tinv"] = tinv
        q["x2"] = _mm(jnp.where(bd_strict, q["g"][0:2 * L, 2 * L:4 * L], 0.0), q["v2"])
    for q in units:
        uw = _mm(q["tinv"], jnp.concatenate([q["x2"], q["a2"]], axis=1))
        q["u0"] = -fold(uw[:, 0:LANES])
        q["w"] = fold(uw[:, LANES:])
        q["y0"] = fold(_mm(jnp.where(bd_incl, q["g"][2 * L:4 * L, 2 * L:4 * L], 0.0), q["v2"]))
    for q in units:
        rb = jnp.where(bd_incl, q["g"][2 * L:4 * L, 0:2 * L], 0.0)
        ruw = _mm(rb, jnp.concatenate([both(q["u0"]), both(q["w"])], axis=1))
        q["yc"] = q["y0"] + fold(ruw[:, 0:LANES])
        q["ry"] = q["rhat"] - fold(ruw[:, LANES:])
        q["c1"] = _mm_tn(jnp.concatenate([q["u0"], q["v"]], axis=0), jnp.concatenate([q["btil"], q["ktil"]], axis=0))
        q["c2"] = _mm_tn(q["w"], q["btil"])

    hr, hc = _iota2((LANES, LANES))
    head_bd = (hr // RW_HEAD_DIM) == (hc // RW_HEAD_DIM)
    zts = [zt_ref[p] for p in range(npair)]
    for ci in range(nch):
        qs = units[ci * npair:(ci + 1) * npair]
        for q, zt in zip(qs, zts):
            q["y"] = q["yc"] + _mm_nt(q["ry"], zt)
        zts = [zt * q["gam_last"] + jnp.where(head_bd, q["c1"] - _mm(zt, q["c2"]), 0.0) for q, zt in zip(qs, zts)]
    for p in range(npair):
        zt_ref[p] = zts[p]

    for q in units:
        sl, rs, y = q["sl"], q["rs"], q["y"]
        mean = _mm_split(y, seg) * (1.0 / RW_HEAD_DIM)
        yc = y - mean
        var = _mm_split(yc * yc, seg) * (1.0 / RW_HEAD_DIM)
        yn = yc * lax.rsqrt(var + RW_LN_EPS) * lnw_ref[:, sl] + lnb_ref[:, sl]
        bonus = _mm_split(q["r"] * q["k2"] * rk_ref[:, sl], seg) * q["v"]
        o_ref[0, rs, sl] = (yn + bonus) * g_all[rs, sl]


def _rwkv7(p, mu, w0, w2, a0, a2, g2, k_k, k_a, r_k, ln_w, ln_b):
    bsz, t, _ = p.shape
    row = lambda z: z.reshape(1, -1).astype(F32)
    w2p = jnp.concatenate([w2, jnp.zeros_like(w2)], axis=0).astype(BF16)
    a2p = jnp.concatenate([jnp.zeros_like(a2), a2], axis=0).astype(BF16)
    params = [row(mu), row(w0), w2p, row(a0), a2p, g2.astype(BF16), row(k_k), row(k_a), row(r_k), row(ln_w), row(ln_b)]
    full = lambda z: pl.BlockSpec(z.shape, lambda b, c: (0, 0))
    rows = RW_NCH * CHUNK
    return pl.pallas_call(
        _rwkv7_kernel,
        grid=(bsz, t // rows),
        in_specs=[pl.BlockSpec((1, rows, RW_IN), lambda b, c: (b, c, 0))] + [full(z) for z in params],
        out_specs=pl.BlockSpec((1, rows, RW_DIM), lambda b, c: (b, c, 0)),
        out_shape=jax.ShapeDtypeStruct((bsz, t, RW_DIM), F32),
        scratch_shapes=[pltpu.VMEM((8, RW_IN), F32), pltpu.VMEM((RW_HEADS // 2, LANES, LANES), F32)],
        compiler_params=_cparams("arbitrary", "arbitrary"),
        name="rwkv7_mix",
    )(p, *params)


def _hgrn2_kernel(p_ref, lb_ref, ng_ref, o_ref, st_ref):
    L = HG_ROWS
    c = pl.program_id(1)

    @pl.when(c == 0)
    def _():
        st_ref[...] = jnp.zeros_like(st_ref)

    x = p_ref[0]
    lb = lb_ref[...]
    q_all = jax.nn.silu(x[:, 0:HG_KDIM])
    fg = lb + (1.0 - lb) * jax.nn.sigmoid(x[:, HG_KDIM:2 * HG_KDIM])
    k_all = 1.0 - fg
    logf = jnp.log(fg)
    v_all = x[:, 2 * HG_KDIM:3 * HG_KDIM]
    gate = x[:, 3 * HG_KDIM:]
    tr, tc = _iota2((L, L))
    blk_tril = jnp.where((tc <= tr) & ((tr // SUB) == (tc // SUB)), 1.0, 0.0).astype(BF16)
    bc_all = _cumsum_rows(blk_tril, logf)
    seg = _seg_ones(LANES, HG_DIM)
    br, bcc = _iota2((LANES, LANES))
    bd = (br // HG_DIM) == (bcc // HG_DIM)
    half = SUB // 2
    t_lo = lax.broadcasted_iota(I32, (SUB, LANES), 0)
    t_hi = lax.broadcasted_iota(I32, (half, LANES), 0) + half

    nsub = L // SUB
    npair = HG_HEADS // 2
    rows_per_unit = half * SUB + half * half
    sub_of_row = lax.broadcasted_iota(I32, (L, LANES), 0) // SUB

    units = []
    parts = []
    for p in range(npair):
        sl = slice(p * LANES, (p + 1) * LANES)
        for j in range(nsub):
            rs = slice(j * SUB, (j + 1) * SUB)
            q, k, v, bc = q_all[rs, sl], k_all[rs, sl], v_all[rs, sl], bc_all[rs, sl]
            q_hi, bc_hi = q[half:], bc[half:]
            for s in range(SUB):
                if s < half:
                    diff = jnp.where(t_lo >= s, bc - bc[s:s + 1, :], NEG)
                    parts.append(jnp.exp(diff) * q * k[s:s + 1, :])
                else:
                    diff = jnp.where(t_hi >= s, bc_hi - bc[s:s + 1, :], NEG)
                    parts.append(jnp.exp(diff) * q_hi * k[s:s + 1, :])
            bend = bc[SUB - 1:SUB, :]
            units.append(dict(v=v, qt=q * jnp.exp(bc), dec=jnp.exp(bend), kd=k * jnp.exp(bend - bc)))
    score_all = _mm(jnp.concatenate(parts, axis=0), seg)
    for ui, un in enumerate(units):
        score = score_all[ui * rows_per_unit:(ui + 1) * rows_per_unit]
        v = un["v"]
        acc_lo = jnp.zeros((half, LANES), F32)
        acc_hi = jnp.zeros((half, LANES), F32)
        off = 0
        for s in range(SUB):
            vs = v[s:s + 1, :]
            if s < half:
                acc_lo = acc_lo + score[off:off + half] * vs
                acc_hi = acc_hi + score[off + half:off + SUB] * vs
                off += SUB
            else:
                acc_hi = acc_hi + score[off:off + half] * vs
                off += half
        un["intra"] = jnp.concatenate([acc_lo, acc_hi], axis=0)
    for p in range(npair):
        us = units[p * nsub:(p + 1) * nsub]
        v_pair = jnp.concatenate([un["v"] for un in us], axis=0)
        kd_pair = jnp.concatenate([un["kd"] for un in us], axis=0)
        kd_wide = jnp.concatenate([jnp.where(sub_of_row == j, kd_pair, 0.0) for j in range(nsub)], axis=1)
        upd = _mm_tn(v_pair, kd_wide)
        for j, un in enumerate(us):
            un["upd"] = jnp.where(bd, upd[:, j * LANES:(j + 1) * LANES], 0.0)

    for p in range(npair):
        sl = slice(p * LANES, (p + 1) * LANES)
        st = st_ref[p]
        outs = []
        for un in units[p * nsub:(p + 1) * nsub]:
            outs.append(un["intra"] + _mm_nt(un["qt"], st))
            st = st * un["dec"] + un["upd"]
        st_ref[p] = st
        o = jnp.concatenate(outs, axis=0)
        ms = _mm_split(o * o, seg) * (1.0 / HG_DIM)
        o_ref[0, :, sl] = o * lax.rsqrt(ms + NORM_EPS) * ng_ref[:, sl] * jax.nn.silu(gate[:, sl])


def _hgrn2(p, lb, norm_g):
    bsz, t, _ = p.shape
    lb = lb.reshape(1, HG_KDIM).astype(F32)
    ng = jnp.tile(norm_g.astype(F32), HG_HEADS).reshape(1, HG_KDIM)
    return pl.pallas_call(
        _hgrn2_kernel,
        grid=(bsz, t // HG_ROWS),
        in_specs=[pl.BlockSpec((1, HG_ROWS, HG_IN), lambda b, c: (b, c, 0)),
                  pl.BlockSpec((1, HG_KDIM), lambda b, c: (0, 0)), pl.BlockSpec((1, HG_KDIM), lambda b, c: (0, 0))],
        out_specs=pl.BlockSpec((1, HG_ROWS, HG_KDIM), lambda b, c: (b, c, 0)),
        out_shape=jax.ShapeDtypeStruct((bsz, t, HG_KDIM), F32),
        scratch_shapes=[pltpu.VMEM((HG_HEADS // 2, LANES, LANES), F32)],
        compiler_params=_cparams("arbitrary", "arbitrary"),
        name="hgrn2_mix",
    )(p, lb, ng)


def _gdn_kernel(p_ref, pg_ref, cw_ref, gb_ref, nal_ref, nalc_ref, ng_ref, o_ref, prev_ref, s_ref):
    L = CHUNK
    c = pl.program_id(1)

    @pl.when(c == 0)
    def _():
        prev_ref[...] = jnp.zeros_like(prev_ref)
        s_ref[...] = jnp.zeros_like(s_ref)

    x = p_ref[0]
    R = x.shape[0]
    nch = R // L
    xq = x[:, 0:3 * GD_DIM]
    xcat = jnp.concatenate([prev_ref[...], xq], axis=0)
    conv = xq * cw_ref[CONV_K - 1:CONV_K, :]
    for j in range(1, CONV_K):
        conv = conv + xcat[8 - j:8 - j + R, :] * cw_ref[CONV_K - 1 - j:CONV_K - j, :]
    prev_ref[...] = xq[R - 8:R, :]
    qkv = jax.nn.silu(conv)
    z = x[:, 3 * GD_DIM:]

    gt = pg_ref[0] + gb_ref[...]
    beta_all = jax.nn.sigmoid(gt)
    g_all = nal_ref[...] * jax.nn.softplus(gt)
    g_t = nalc_ref[...] * jax.nn.softplus(gt.T)
    tr, tc = _iota2((R, R))
    chunk_tril = jnp.where((tc <= tr) & ((tr // L) == (tc // L)), 1.0, 0.0).astype(BF16)
    gam_all = _cumsum_rows(chunk_tril, g_all)
    ur, uc = _iota2((R, 2 * R))
    triu2 = jnp.where(((ur // L) == (uc // (2 * L))) & ((ur % L) <= (uc % L)), 1.0, 0.0).astype(BF16)
    gam_t2 = _cumsum_cols(g_t, triu2)
    lr, lc = _iota2((L, L))
    incl = lc <= lr
    br, bc = _iota2((2 * L, 2 * L))
    bd = (br // L) == (bc // L)
    bd_strict = bd & (bc < br)
    bd_incl = bd & (bc <= br)
    lane2 = lax.broadcasted_iota(I32, (1, 2 * L), 1)
    zero = jnp.zeros((L, LANES), F32)

    units = []
    for ci in range(nch):
        rs = slice(ci * L, (ci + 1) * L)
        for h in range(GD_HEADS):
            q = qkv[rs, h * LANES:(h + 1) * LANES]
            k = qkv[rs, GD_DIM + h * LANES:GD_DIM + (h + 1) * LANES]
            v = qkv[rs, 2 * GD_DIM + h * LANES:2 * GD_DIM + (h + 1) * LANES]
            q = q * lax.rsqrt(jnp.sum(q * q, axis=-1, keepdims=True) + 1e-6) * (GD_HEAD_DIM ** -0.5)
            k = k * lax.rsqrt(jnp.sum(k * k, axis=-1, keepdims=True) + 1e-6)
            beta = beta_all[rs, h:h + 1]
            gam = gam_all[rs, GD_HEADS + h:GD_HEADS + h + 1]
            gam_row2 = gam_t2[GD_HEADS + h:GD_HEADS + h + 1, ci * 2 * L:(ci + 1) * 2 * L]
            gam_last = gam[L - 1:L, :]
            kb = k * beta
            units.append(dict(h=h, rs=rs, q=q, k=k, kb=kb, vb=v * beta, gam=gam, gam_row2=gam_row2,
                              gam_last=gam_last, kg=kb * jnp.exp(gam), qg=q * jnp.exp(gam),
                              kd=k * jnp.exp(gam_last - gam),
                              decay=jnp.exp(jnp.where(incl, gam - gam_row2[:, 0:L], NEG))))
    pairs = [(units[i], units[i + 1]) for i in range(0, len(units), 2)]
    ms = []
    for h0, h1 in pairs:
        lhs = jnp.concatenate([jnp.concatenate([h0["kb"], zero], axis=1),
                               jnp.concatenate([zero, h1["kb"]], axis=1)], axis=0)
        rhs = jnp.concatenate([jnp.concatenate([h0["k"], zero], axis=1),
                               jnp.concatenate([zero, h1["k"]], axis=1)], axis=0)
        gam_col = jnp.concatenate([h0["gam"], h1["gam"]], axis=0)
        gam_row = jnp.where(lane2 < L, h0["gam_row2"], h1["gam_row2"])
        decay2 = jnp.exp(jnp.where(bd_incl, gam_col - gam_row, NEG))
        ms.append(jnp.where(bd_strict, _mm_nt(lhs, rhs) * decay2, 0.0))
    tinvs = _tri_inv_multi(ms, 2 * L, L)
    for (h0, h1), tinv in zip(pairs, tinvs):
        rhs = jnp.concatenate([jnp.concatenate([h0["vb"], h0["kg"]], axis=1),
                               jnp.concatenate([h1["vb"], h1["kg"]], axis=1)], axis=0)
        uw = _mm(tinv, rhs)
        h0["u"], h0["w"] = uw[0:L, 0:LANES], uw[0:L, LANES:]
        h1["u"], h1["w"] = uw[L:2 * L, 0:LANES], uw[L:2 * L, LANES:]
    for hd in units:
        attn = _mm_nt(hd["q"], hd["k"]) * hd["decay"]
        uw = jnp.concatenate([hd["u"], hd["w"]], axis=1)
        auw = _mm(attn, uw)
        hd["o0"] = auw[:, 0:LANES]
        hd["qs"] = hd["qg"] - auw[:, LANES:]
        cc = _mm_tn(hd["kd"], uw)
        hd["c1"], hd["c2"] = cc[:, 0:LANES], cc[:, LANES:]

    ss = [s_ref[h] for h in range(GD_HEADS)]
    for ci in range(nch):
        hds = units[ci * GD_HEADS:(ci + 1) * GD_HEADS]
        for hd, s in zip(hds, ss):
            hd["o"] = hd["o0"] + _mm(hd["qs"], s)
        ss = [s * jnp.exp(hd["gam_last"]) + hd["c1"] - _mm(hd["c2"], s) for hd, s in zip(hds, ss)]
    for h in range(GD_HEADS):
        s_ref[h] = ss[h]
    for hd in units:
        sl = slice(hd["h"] * LANES, (hd["h"] + 1) * LANES)
        o_ref[0, hd["rs"], sl] = _rms_rows(hd["o"], ng_ref[...]) * jax.nn.silu(z[hd["rs"], sl])


def _gate_row(lo, vals):
    return jnp.zeros((1, LANES), F32).at[0, lo:lo + vals.shape[0]].set(vals.astype(F32))


def _gdn(p, pg, conv_w, a_log, dt_bias, norm_g):
    bsz, t, _ = p.shape
    gbias = _gate_row(GD_HEADS, dt_bias)
    nal = _gate_row(GD_HEADS, -jnp.exp(a_log))
    nal_col = nal.reshape(LANES, 1)
    ng = norm_g.reshape(1, GD_HEAD_DIM).astype(F32)
    full = lambda z: pl.BlockSpec(z.shape, lambda b, c: (0, 0))
    rows = GD_NCH * CHUNK
    return pl.pallas_call(
        _gdn_kernel,
        grid=(bsz, t // rows),
        in_specs=[pl.BlockSpec((1, rows, GD_MAIN), lambda b, c: (b, c, 0)),
                  pl.BlockSpec((1, rows, LANES), lambda b, c: (b, c, 0)),
                  full(conv_w), full(gbias), full(nal), full(nal_col), full(ng)],
        out_specs=pl.BlockSpec((1, rows, GD_DIM), lambda b, c: (b, c, 0)),
        out_shape=jax.ShapeDtypeStruct((bsz, t, GD_DIM), F32),
        scratch_shapes=[pltpu.VMEM((8, 3 * GD_DIM), F32), pltpu.VMEM((GD_HEADS, LANES, LANES), F32)],
        compiler_params=_cparams("arbitrary", "arbitrary"),
        name="gdn_mix",
    )(p, pg, conv_w.astype(F32), gbias, nal, nal_col, ng)


def _mlstm_kernel(p_ref, pg_ref, gb_ref, ng_ref, o_ref, c_ref, n_ref, m_ref):
    L = ML_CHUNK
    ci = pl.program_id(1)

    @pl.when(ci == 0)
    def _():
        c_ref[...] = jnp.zeros_like(c_ref)
        n_ref[...] = jnp.zeros_like(n_ref)
        m_ref[...] = jnp.zeros_like(m_ref)

    x = p_ref[0]
    R = x.shape[0]
    nch = R // L
    gt = pg_ref[0] + gb_ref[...]
    logf = jax.nn.log_sigmoid(gt)
    gt_t = gt.T
    logf_t = jax.nn.log_sigmoid(gt_t)
    tr, tc = _iota2((R, R))
    same = (tr // L) == (tc // L)
    bc_all = _cumsum_rows(jnp.where(same & (tc <= tr), 1.0, 0.0).astype(BF16), logf)
    bc_t = _cumsum_cols(logf_t, jnp.where(same & (tr <= tc), 1.0, 0.0).astype(BF16))
    lr, lc = _iota2((L, L))
    incl = lc <= lr
    i_lo, f_lo = 2 * GD_HEADS, 2 * GD_HEADS + ML_HEADS

    units = []
    m_run = [m_ref[h][0:1, 0:1] for h in range(ML_HEADS)]
    for ci in range(nch):
        rs = slice(ci * L, (ci + 1) * L)
        for h in range(ML_HEADS):
            sl = slice(h * LANES, (h + 1) * LANES)
            q = x[rs, sl]
            k = x[rs, ML_DIM + h * LANES:ML_DIM + (h + 1) * LANES] * (ML_HEAD_DIM ** -0.5)
            v = x[rs, 2 * ML_DIM + h * LANES:2 * ML_DIM + (h + 1) * LANES]
            bc = bc_all[rs, f_lo + h:f_lo + h + 1]
            bc_row = bc_t[f_lo + h:f_lo + h + 1, rs]
            ig = gt[rs, i_lo + h:i_lo + h + 1]
            ig_row = gt_t[i_lo + h:i_lo + h + 1, rs]
            m_prev = m_run[h]
            b_last = bc[L - 1:L, :]
            d_log = jnp.where(incl, bc - bc_row + ig_row, NEG)
            inter_log = bc + m_prev
            m_t = jnp.maximum(inter_log, jnp.max(d_log, axis=-1, keepdims=True))
            upd_log = b_last - bc + ig
            m_new = jnp.maximum(b_last + m_prev, jnp.max(upd_log, axis=0, keepdims=True))
            m_run[h] = m_new
            wk = jnp.exp(upd_log - m_new) * k
            units.append(dict(h=h, rs=rs, sl=sl, q=q, v=v, m_t=m_t, wk=wk, inter_w=jnp.exp(inter_log - m_t),
                              dec=jnp.exp(b_last + m_prev - m_new), sc=_mm_nt(q, k) * jnp.exp(d_log - m_t)))
    for hd in units:
        hd["scv"] = _mm(hd["sc"], hd["v"])
        hd["kv"] = _mm_tn(hd["wk"], hd["v"])
    cs = [c_ref[h] for h in range(ML_HEADS)]
    ns = [n_ref[h][0:1, :] for h in range(ML_HEADS)]
    for hd in units:
        h = hd["h"]
        q, sc, inter_w, dec = hd["q"], hd["sc"], hd["inter_w"], hd["dec"]
        num = inter_w * _mm(q, cs[h]) + hd["scv"]
        den = inter_w * jnp.sum(q * ns[h], axis=-1, keepdims=True) + jnp.sum(sc, axis=-1, keepdims=True)
        hd["hh"] = num / jnp.maximum(jnp.abs(den), jnp.exp(-hd["m_t"]))
        cs[h] = dec * cs[h] + hd["kv"]
        ns[h] = dec * ns[h] + jnp.sum(hd["wk"], axis=0, keepdims=True)
    for h in range(ML_HEADS):
        c_ref[h] = cs[h]
        n_ref[h] = jnp.broadcast_to(ns[h], (8, LANES))
        m_ref[h] = jnp.broadcast_to(m_run[h], (8, LANES))
    for hd in units:
        h, rs, sl = hd["h"], hd["rs"], hd["sl"]
        og = jax.nn.sigmoid(x[rs, 3 * ML_DIM + h * LANES:3 * ML_DIM + (h + 1) * LANES])
        o_ref[0, rs, sl] = og * _rms_rows(hd["hh"], ng_ref[:, sl])


def _mlstm(p, pg, i_bias, f_bias, norm_g):
    bsz, t, _ = p.shape
    gbias = _gate_row(2 * GD_HEADS, jnp.concatenate([i_bias, f_bias]))
    ng = norm_g.reshape(1, ML_DIM).astype(F32)
    lc = ML_NCH * ML_CHUNK
    full = lambda z: pl.BlockSpec(z.shape, lambda b, c: (0, 0))
    return pl.pallas_call(
        _mlstm_kernel,
        grid=(bsz, t // lc),
        in_specs=[pl.BlockSpec((1, lc, ML_MAIN), lambda b, c: (b, c, 0)),
                  pl.BlockSpec((1, lc, LANES), lambda b, c: (b, c, 0)), full(gbias), full(ng)],
        out_specs=pl.BlockSpec((1, lc, ML_DIM), lambda b, c: (b, c, 0)),
        out_shape=jax.ShapeDtypeStruct((bsz, t, ML_DIM), F32),
        scratch_shapes=[pltpu.VMEM((ML_HEADS, LANES, LANES), F32), pltpu.VMEM((ML_HEADS, 8, LANES), F32),
                        pltpu.VMEM((ML_HEADS, 8, LANES), F32)],
        compiler_params=_cparams("arbitrary", "arbitrary"),
        name="mlstm_mix",
    )(p, pg, gbias, ng)


def _route(h, g_ref, wh_ref, wl_ref, b_ref, off_ref):
    tm = h.shape[0]
    xn = _rms_rows(h, g_ref[...])
    xh = xn.astype(BF16)
    xl = (xn - xh.astype(F32)).astype(BF16)
    dot = lambda a, b: jnp.dot(a, b, preferred_element_type=F32)
    logits = dot(xh, wh_ref[...]) + dot(xh, wl_ref[...]) + dot(xl, wh_ref[...]) + b_ref[...]
    lt = logits.T[0:ROUTE_ROWS, :]
    row = lax.broadcasted_iota(I32, lt.shape, 0)
    big = jnp.int32(1 << 20)
    is_grp = (row >= N_EXPERTS) & (row < N_EXPERTS + N_GROUPS)
    lg = jnp.where(is_grp, lt, NEG)
    gmax = jnp.max(lg, axis=0, keepdims=True)
    p_top = 1.0 / jnp.sum(jnp.exp(lg - gmax), axis=0, keepdims=True)
    g_idx = jnp.min(jnp.where(lg == gmax, row, big), axis=0, keepdims=True) - N_EXPERTS
    valid = (row < N_EXPERTS) & ((row // EXPERTS_PER_GROUP) == g_idx)
    v1 = jnp.where(valid, lt, NEG)
    m1 = jnp.max(v1, axis=0, keepdims=True)
    i1 = jnp.min(jnp.where(v1 == m1, row, big), axis=0, keepdims=True)
    v2 = jnp.where(row == i1, NEG, v1)
    m2 = jnp.max(v2, axis=0, keepdims=True)
    i2 = jnp.min(jnp.where(v2 == m2, row, big), axis=0, keepdims=True)
    e21 = jnp.exp(m2 - m1)
    gate1 = p_top / (1.0 + e21)
    gate2 = p_top * e21 / (1.0 + e21)

    sel1 = row == i1
    sel2 = row == i2
    onehot = jnp.where(sel1 | sel2, 1.0, 0.0)
    tr, tc = _iota2((tm, tm))
    earlier = jnp.where(tr < tc, 1.0, 0.0).astype(BF16)
    off = off_ref[0:ROUTE_ROWS, 0:1]
    prefix = jnp.dot(onehot.astype(BF16), earlier, preferred_element_type=F32) + off
    rank1 = jnp.sum(jnp.where(sel1, prefix, 0.0), axis=0, keepdims=True)
    rank2 = jnp.sum(jnp.where(sel2, prefix, 0.0), axis=0, keepdims=True)
    off_ref[0:ROUTE_ROWS, :] = jnp.broadcast_to(off + jnp.sum(onehot, axis=1, keepdims=True), (ROUTE_ROWS, LANES))

    packed = jnp.concatenate([i1.astype(F32), i2.astype(F32), rank1, rank2, gate1, gate2,
                              jnp.zeros((2, tm), F32)], axis=0)
    return xn, packed


def _out_proj_route_kernel(h_ref, ya_ref, yb_ref, wa_ref, wb_ref, g_ref, wh_ref, wl_ref, b_ref,
                           o_ref, xt_ref, rg_ref, cnt_ref, dest_ref, off_ref, rt_ref):
    phase = pl.program_id(0)
    s = pl.program_id(1)
    tm = h_ref.shape[0]

    @pl.when((phase == 0) & (s == 0))
    def _():
        off_ref[...] = jnp.zeros_like(off_ref)

    @pl.when(phase == 0)
    def _():
        out = h_ref[...] + _mm(ya_ref[...], wa_ref[...]) + _mm(yb_ref[...], wb_ref[...])
        o_ref[...] = out
        xn, packed = _route(out, g_ref, wh_ref, wl_ref, b_ref, off_ref)
        _store_row_tiles(xt_ref, xn)
        rt_ref[s] = packed
        wide = jnp.concatenate([packed, jnp.zeros((LANES - 8, tm), F32)], axis=0)
        rg_ref[...] = wide.T
        cnt_ref[...] = off_ref[...]

    @pl.when(phase == 1)
    def _():
        counts = off_ref[...]
        tiles_per = jnp.floor((counts + (MOE_TM - 1)) * (1.0 / MOE_TM))
        er, ec = _iota2((LANES, LANES))
        tile_end = _mm(jnp.where(ec <= er, 1.0, 0.0), tiles_per)
        seg_start = ((tile_end - tiles_per) * MOE_TM)[0:ROUTE_ROWS, 0:1]
        blk = rt_ref[s]
        row = lax.broadcasted_iota(I32, (ROUTE_ROWS, tm), 0).astype(F32)
        pick = lambda e: jnp.sum(jnp.where(row == e, seg_start, 0.0), axis=0, keepdims=True)
        d1 = pick(blk[0:1]) + blk[2:3]
        d2 = pick(blk[1:2]) + blk[3:4]
        dest_ref[...] = jnp.concatenate([d1, d2, jnp.zeros((6, tm), F32)], axis=0).astype(I32)


def _out_proj_route(h, ya, yb, w_out, g_ffn, w_group, b_group, w_router, b_router):
    n, d = h.shape
    da, db = ya.shape[1], yb.shape[1]
    tm = min(PROJ_TM, n)
    wa = w_out[:da].astype(BF16)
    wb = w_out[da:].astype(BF16)
    pad = LANES - N_EXPERTS - N_GROUPS
    w_cat = jnp.concatenate([w_router, w_group, jnp.zeros((d, pad), F32)], axis=1)
    b_cat = jnp.concatenate([b_router, b_group, jnp.zeros((pad,), F32)]).reshape(1, LANES)
    w_hi = w_cat.astype(BF16)
    w_lo = (w_cat - w_hi.astype(F32)).astype(BF16)
    steps = n // tm
    row = lambda p, s: (s * (1 - p) + (steps - 1) * p, 0)
    fixed = lambda p, s: (0, 0)
    return pl.pallas_call(
        _out_proj_route_kernel,
        grid=(2, steps),
        in_specs=[pl.BlockSpec((tm, d), row), pl.BlockSpec((tm, da), row), pl.BlockSpec((tm, db), row),
                  pl.BlockSpec((da, d), fixed), pl.BlockSpec((db, d), fixed), pl.BlockSpec((1, d), fixed),
                  pl.BlockSpec((d, LANES), fixed), pl.BlockSpec((d, LANES), fixed), pl.BlockSpec((1, LANES), fixed)],
        out_specs=[pl.BlockSpec((tm, d), row), pl.BlockSpec((tm * ROW_TILE, LANES), row),
                   pl.BlockSpec((tm, LANES), row), pl.BlockSpec((LANES, LANES), fixed),
                   pl.BlockSpec((8, tm), lambda p, s: (0, s * p))],
        out_shape=[jax.ShapeDtypeStruct((n, d), F32), jax.ShapeDtypeStruct((n * ROW_TILE, LANES), F32),
                   jax.ShapeDtypeStruct((n, LANES), F32), jax.ShapeDtypeStruct((LANES, LANES), F32),
                   jax.ShapeDtypeStruct((8, n), I32)],
        scratch_shapes=[pltpu.VMEM((LANES, LANES), F32), pltpu.VMEM((steps, 8, tm), F32)],
        compiler_params=_cparams("arbitrary", "arbitrary"),
        name="out_proj_route",
    )(h, ya, yb, wa, wb, g_ffn.reshape(1, d), w_hi, w_lo, b_cat)


def _gather_rows(src_hbm, idx_ref, base, dst, sem):
    def body(j, carry):
        r = pl.multiple_of(idx_ref[base + j] * ROW_TILE, ROW_TILE)
        pltpu.make_async_copy(src_hbm.at[pl.ds(r, ROW_TILE), :],
                              dst.at[pl.ds(pl.multiple_of(j * ROW_TILE, ROW_TILE), ROW_TILE), :],
                              sem).start()
        return carry

    lax.fori_loop(0, dst.shape[0] // ROW_TILE, body, 0, unroll=GATHER_UNROLL)


def _wait_rows(src_hbm, dst, sem):
    pltpu.make_async_copy(src_hbm.at[pl.ds(0, dst.shape[0]), :], dst, sem).wait()


def _dispatch_kernel(d1_ref, d2_ref, lo_ref, hi_ref, nt_ref, x_ref, xs_hbm, zero_ref, sem):
    rows = x_ref.shape[0]
    base = pl.program_id(0) * (rows // ROW_TILE)
    tile_rows = MOE_TM * ROW_TILE

    def pad_copy(p):
        return pltpu.make_async_copy(zero_ref.at[pl.ds(0, ROW_TILE), :],
                                     xs_hbm.at[pl.ds(pl.multiple_of(p * ROW_TILE, ROW_TILE), ROW_TILE), :], sem.at[1])

    def tile_copy(t):
        return pltpu.make_async_copy(zero_ref, xs_hbm.at[pl.ds(pl.multiple_of(t * tile_rows, tile_rows), tile_rows), :],
                                     sem.at[1])

    @pl.when(pl.program_id(0) == 0)
    def _():
        zero_ref[...] = jnp.zeros_like(zero_ref)

        def loop(lo, hi, make, wait):
            def body(p, carry):
                if wait:
                    make(p).wait()
                else:
                    make(p).start()
                return carry

            lax.fori_loop(lo, hi, body, 0)

        n_tiles = xs_hbm.shape[0] // tile_rows
        for wait in (False, True):
            for e in range(lo_ref.shape[0]):
                loop(lo_ref[e], hi_ref[e], pad_copy, wait)
            loop(nt_ref[0], n_tiles, tile_copy, wait)

    for idx_ref in (d1_ref, d2_ref):
        def body(j, carry, idx_ref=idx_ref):
            r = pl.multiple_of(idx_ref[base + j] * ROW_TILE, ROW_TILE)
            pltpu.make_async_copy(x_ref.at[pl.ds(pl.multiple_of(j * ROW_TILE, ROW_TILE), ROW_TILE), :],
                                  xs_hbm.at[pl.ds(r, ROW_TILE), :], sem.at[0]).start()
            return carry

        lax.fori_loop(0, rows // ROW_TILE, body, 0, unroll=GATHER_UNROLL)
    for _ in range(2):
        pltpu.make_async_copy(x_ref, xs_hbm.at[pl.ds(0, rows), :], sem.at[0]).wait()


def _dispatch(xn_tiles, d1, d2, pad_lo, pad_hi, n_used, n_rows):
    n = d1.shape[0]
    tm = min(PROJ_TM, n)
    grid_spec = pltpu.PrefetchScalarGridSpec(
        num_scalar_prefetch=5,
        grid=(n // tm,),
        in_specs=[pl.BlockSpec((tm * ROW_TILE, LANES), lambda i, *_: (i, 0))],
        out_specs=pl.BlockSpec(memory_space=pl.ANY),
        scratch_shapes=[pltpu.VMEM((MOE_TM * ROW_TILE, LANES), F32), pltpu.SemaphoreType.DMA((2,))],
    )
    return pl.pallas_call(
        _dispatch_kernel,
        grid_spec=grid_spec,
        out_shape=jax.ShapeDtypeStruct((n_rows * ROW_TILE, LANES), F32),
        compiler_params=_cparams("arbitrary"),
        name="moe_dispatch",
    )(d1, d2, pad_lo, pad_hi, n_used, xn_tiles)


def _expert_kernel(te_ref, nt_ref, x_ref, wg_ref, wu_ref, wd_ref, o_ref, wgb, wub, wdb):
    tm = x_ref.shape[0] // ROW_TILE
    i = pl.program_id(0)

    @pl.when((i == 0) | (te_ref[i] != te_ref[jnp.maximum(i - 1, 0)]))
    def _():
        wgb[...] = wg_ref[0].astype(BF16)
        wub[...] = wu_ref[0].astype(BF16)
        wdb[...] = wd_ref[0].astype(BF16)

    @pl.when(i < nt_ref[0])
    def _():
        xn = _load_row_tiles(x_ref, tm).astype(BF16)
        gate = jnp.dot(xn, wgb[...], preferred_element_type=F32)
        up = jnp.dot(xn, wub[...], preferred_element_type=F32)
        hid = (jax.nn.silu(gate) * up).astype(BF16)
        _store_row_tiles(o_ref, jnp.dot(hid, wdb[...], preferred_element_type=F32))

    @pl.when(i >= nt_ref[0])
    def _():
        o_ref[...] = jnp.zeros_like(o_ref)


def _experts(xs, tile_expert, n_used, w_gate, w_up, w_down):
    d = D_MODEL
    n_tiles = tile_expert.shape[0]
    tm = MOE_TM
    de = w_gate.shape[-1]
    wspec = lambda rows, cols: pl.BlockSpec((1, rows, cols), lambda i, te, nt: (te[i], 0, 0))
    grid_spec = pltpu.PrefetchScalarGridSpec(
        num_scalar_prefetch=2,
        grid=(n_tiles,),
        in_specs=[pl.BlockSpec((tm * ROW_TILE, LANES), lambda i, te, nt: (jnp.minimum(i, nt[0] - 1), 0)),
                  wspec(d, de), wspec(d, de), wspec(de, d)],
        out_specs=pl.BlockSpec((tm * ROW_TILE, LANES), lambda i, te, nt: (i, 0)),
        scratch_shapes=[pltpu.VMEM((d, de), BF16), pltpu.VMEM((d, de), BF16), pltpu.VMEM((de, d), BF16)],
    )
    return pl.pallas_call(
        _expert_kernel,
        grid_spec=grid_spec,
        out_shape=jax.ShapeDtypeStruct((n_tiles * tm * ROW_TILE, LANES), F32),
        compiler_params=_cparams("arbitrary"),
        name="moe_experts",
    )(tile_expert, n_used, xs, w_gate, w_up, w_down)


def _combine_kernel(d1_ref, d2_ref, ys_hbm, h_ref, rg_ref, g_ref, *refs, n_proj, final_norm):
    w_refs, o_ref, p_refs = refs[:n_proj], refs[n_proj], refs[n_proj + 1:2 * n_proj + 1]
    y1, y2, sem = refs[2 * n_proj + 1:]
    tm = h_ref.shape[0]
    i = pl.program_id(0)
    slot = i % 2

    def gather(tile, s):
        _gather_rows(ys_hbm, d1_ref, tile * tm, y1.at[s], sem.at[0, s])
        _gather_rows(ys_hbm, d2_ref, tile * tm, y2.at[s], sem.at[1, s])

    @pl.when(i == 0)
    def _():
        gather(0, 0)

    @pl.when(i + 1 < pl.num_programs(0))
    def _():
        gather(i + 1, 1 - slot)

    _wait_rows(ys_hbm, y1.at[slot], sem.at[0, slot])
    _wait_rows(ys_hbm, y2.at[slot], sem.at[1, slot])
    rg = rg_ref[...]
    out = (h_ref[...] + rg[:, 4:5] * _load_row_tiles(y1.at[slot], tm)
           + rg[:, 5:6] * _load_row_tiles(y2.at[slot], tm))
    if final_norm:
        out = _rms_rows(out, g_ref[...])
    o_ref[...] = out
    if n_proj:
        y = _rms_rows(out, g_ref[...]).astype(BF16)
        for w_ref, p_ref in zip(w_refs, p_refs):
            p_ref[...] = jnp.dot(y, w_ref[...], preferred_element_type=F32)


def _combine(h, ys, d1, d2, rg, g, ws=(), final_norm=False):
    n, d = h.shape
    tm = min(PROJ_TM, n)
    row = lambda i, a, b: (i, 0)
    fixed = lambda i, a, b: (0, 0)
    grid_spec = pltpu.PrefetchScalarGridSpec(
        num_scalar_prefetch=2,
        grid=(n // tm,),
        in_specs=[pl.BlockSpec(memory_space=pl.ANY), pl.BlockSpec((tm, d), row), pl.BlockSpec((tm, LANES), row),
                  pl.BlockSpec((1, d), fixed)] + [pl.BlockSpec(w.shape, fixed) for w in ws],
        out_specs=[pl.BlockSpec((tm, d), row)] + [pl.BlockSpec((tm, w.shape[1]), row) for w in ws],
        scratch_shapes=[pltpu.VMEM((2, tm * ROW_TILE, LANES), F32), pltpu.VMEM((2, tm * ROW_TILE, LANES), F32),
                        pltpu.SemaphoreType.DMA((2, 2))],
    )
    return pl.pallas_call(
        functools.partial(_combine_kernel, n_proj=len(ws), final_norm=final_norm),
        grid_spec=grid_spec,
        out_shape=[jax.ShapeDtypeStruct((n, d), F32)] + [jax.ShapeDtypeStruct((n, w.shape[1]), F32) for w in ws],
        compiler_params=_cparams("arbitrary"),
        name="moe_combine",
    )(d1, d2, ys, h, rg, g.reshape(1, d), *ws)


def _hmoe_residual(h, xn_tiles, rg, cnt, dest, layer, w_gate, w_up, w_down, g, ws=(), final_norm=False):
    n, d = h.shape
    stack = lambda w: w.reshape((-1,) + w.shape[2:])

    counts = cnt[:N_EXPERTS, 0].astype(I32)
    n_tiles = (2 * n) // MOE_TM + N_EXPERTS
    tile_end = jnp.cumsum((counts + MOE_TM - 1) // MOE_TM)
    d1, d2 = dest[0], dest[1]
    tile_ids = jnp.arange(n_tiles, dtype=I32)
    tile_expert = jnp.minimum(jnp.sum((tile_end[None, :] <= tile_ids[:, None]).astype(I32), axis=1), N_EXPERTS - 1)
    tile_expert = tile_expert + layer * N_EXPERTS
    seg_end = tile_end * MOE_TM
    seg_start = seg_end - ((counts + MOE_TM - 1) // MOE_TM) * MOE_TM
    pad_lo = (seg_start + counts).astype(I32)
    pad_hi = seg_end.astype(I32)
    n_used = tile_end[N_EXPERTS - 1:].astype(I32)
    xs = _dispatch(xn_tiles, d1, d2, pad_lo, pad_hi, n_used, n_tiles * MOE_TM)
    ys = _experts(xs, tile_expert, n_used, stack(w_gate), stack(w_up), stack(w_down))
    return _combine(h, ys, d1, d2, rg, g, ws, final_norm)


def kernel(x, norm_mix, norm_ffn, norm_final, ev_w_in, ev_mu, rw_w0, rw_w2, rw_a0, rw_a2, rw_g2, rw_k_k, rw_k_a, rw_r_k, rw_ln_w, rw_ln_b, hg_lb_logits, hg_norm, ev_w_out, od_w_in, gd_conv, gd_a_log, gd_dt_bias, gd_norm, ml_i_bias, ml_f_bias, ml_norm, od_w_out, moe_w_group, moe_b_group, moe_w_router, moe_b_router, moe_w_gate, moe_w_up, moe_w_down):
    bsz, t, d = x.shape
    n = bsz * t
    depth = norm_mix.shape[0]
    lb_table = jnp.cumsum(jax.nn.softmax(hg_lb_logits.astype(F32), axis=0), axis=0)
    def in_proj_weights(layer):
        j = layer // 2
        if layer % 2 == 0:
            return _pack_weights(ev_w_in[j], [(0, RW_IN), (RW_IN, RW_IN + HG_IN)])
        return _pack_weights(od_w_in[j], [(0, GD_MAIN), (GD_IN, GD_IN + ML_MAIN)],
                             gate_ranges=[(GD_MAIN, GD_IN), (GD_IN + ML_MAIN, GD_IN + ML_MAIN + 2 * ML_HEADS)])

    h = x.reshape(n, d)
    proj = _rms_proj(h, norm_mix[0], in_proj_weights(0))
    for layer in range(depth):
        j = layer // 2
        if layer % 2 == 0:
            p_rw, p_hg = proj
            ya = _rwkv7(p_rw.reshape(bsz, t, RW_IN), ev_mu[j], rw_w0[j], rw_w2[j], rw_a0[j], rw_a2[j], rw_g2[j],
                        rw_k_k[j], rw_k_a[j], rw_r_k[j], rw_ln_w[j], rw_ln_b[j])
            yb = _hgrn2(p_hg.reshape(bsz, t, HG_IN), lb_table[j], hg_norm[j])
            mix_a, mix_b, w_out = ya.reshape(n, RW_DIM), yb.reshape(n, HG_KDIM), ev_w_out[j]
        else:
            p_gd, p_ml, p_gt = proj
            p_gt = p_gt.reshape(bsz, t, LANES)
            yc = _gdn(p_gd.reshape(bsz, t, GD_MAIN), p_gt, gd_conv[j], gd_a_log[j], gd_dt_bias[j], gd_norm[j])
            yd = _mlstm(p_ml.reshape(bsz, t, ML_MAIN), p_gt, ml_i_bias[j], ml_f_bias[j], ml_norm[j])
            mix_a, mix_b, w_out = yc.reshape(n, GD_DIM), yd.reshape(n, ML_DIM), od_w_out[j]
        h, xn_tiles, rg, cnt, dest = _out_proj_route(h, mix_a, mix_b, w_out, norm_ffn[layer], moe_w_group[layer],
                                                     moe_b_group[layer], moe_w_router[layer], moe_b_router[layer])
        if layer == depth - 1:
            (h,) = _hmoe_residual(h, xn_tiles, rg, cnt, dest, layer, moe_w_gate, moe_w_up, moe_w_down,
                                  norm_final, final_norm=True)
        else:
            h, *proj = _hmoe_residual(h, xn_tiles, rg, cnt, dest, layer, moe_w_gate, moe_w_up, moe_w_down,
                                      norm_mix[layer + 1], ws=in_proj_weights(layer + 1))
    return h.reshape(bsz, t, d)
```

```python
import functools
import math

import jax
import jax.numpy as jnp
from jax import lax
from jax.experimental import pallas as pl
from jax.experimental.pallas import tpu as pltpu

F32 = jnp.float32
BF16 = jnp.bfloat16
I32 = jnp.int32
HIGHEST = lax.Precision.HIGHEST

D_MODEL = 1024
NORM_EPS = 1e-6
RW_HEADS, RW_HEAD_DIM = 8, 64
RW_DIM = RW_HEADS * RW_HEAD_DIM
R_DECAY, R_AAA, R_GATE = 64, 64, 128
RW_IN = 3 * RW_DIM + R_DECAY + R_AAA + R_GATE
RW_LN_EPS = 64e-5
HG_HEADS, HG_DIM = 8, 64
HG_KDIM = HG_HEADS * HG_DIM
HG_IN = 4 * HG_KDIM
GD_HEADS, GD_HEAD_DIM = 4, 128
GD_DIM = GD_HEADS * GD_HEAD_DIM
CONV_K = 4
GD_MAIN = 4 * GD_DIM
GD_IN = GD_MAIN + 2 * GD_HEADS
ML_HEADS, ML_HEAD_DIM = 4, 128
ML_DIM = ML_HEADS * ML_HEAD_DIM
ML_MAIN = 4 * ML_DIM
N_GROUPS, EXPERTS_PER_GROUP = 4, 8
N_EXPERTS = N_GROUPS * EXPERTS_PER_GROUP
D_EXPERT = 256

LANES = 128
VMEM_LIMIT_BYTES = 48 * 1024 * 1024

PROJ_TM = 256
PACK_TM = 128
CHUNK = 64
SUB = 16
RW_NCH = 4
GD_NCH = 8
HG_ROWS = 256
ML_CHUNK = 128
ML_NCH = 2
MOE_TM = 256
ROUTE_ROWS = 40
GATHER_UNROLL = 8
NEG = -1e30


def _cparams(*sem):
    return pltpu.CompilerParams(dimension_semantics=sem, vmem_limit_bytes=VMEM_LIMIT_BYTES)


def _mm(a, b):
    return jnp.dot(a.astype(BF16), b.astype(BF16), preferred_element_type=F32)


def _mm_nt(a, b):
    return lax.dot_general(a.astype(BF16), b.astype(BF16), (((1,), (1,)), ((), ())), preferred_element_type=F32)


def _mm_tn(a, b):
    return lax.dot_general(a.astype(BF16), b.astype(BF16), (((0,), (0,)), ((), ())), preferred_element_type=F32)


def _mm_hi(a, b):
    return jnp.dot(a, b, precision=HIGHEST, preferred_element_type=F32)


def _mm_split(x, ones_bf16):
    hi = x.astype(BF16)
    lo = (x - hi.astype(F32)).astype(BF16)
    return (jnp.dot(hi, ones_bf16, preferred_element_type=F32) + jnp.dot(lo, ones_bf16, preferred_element_type=F32))


def _iota2(shape):
    return lax.broadcasted_iota(I32, shape, 0), lax.broadcasted_iota(I32, shape, 1)


def _rms_rows(x, g, eps=NORM_EPS):
    return x * lax.rsqrt(jnp.mean(x * x, axis=-1, keepdims=True) + eps) * g


def _seg_ones(width, seg):
    r, c = _iota2((width, width))
    return jnp.where((r // seg) == (c // seg), 1.0, 0.0).astype(BF16)


def _split3(x):
    x1 = x.astype(BF16)
    r1 = x - x1.astype(F32)
    x2 = r1.astype(BF16)
    return x1, x2, (r1 - x2.astype(F32)).astype(BF16)


def _cumsum_rows(tri_bf16, x):
    return sum(jnp.dot(tri_bf16, t, preferred_element_type=F32) for t in _split3(x))


def _cumsum_cols(x, tri_bf16):
    return sum(jnp.dot(t, tri_bf16, preferred_element_type=F32) for t in _split3(x))


def _mm3(a, b):
    ah = a.astype(BF16)
    al = (a - ah.astype(F32)).astype(BF16)
    bh = b.astype(BF16)
    bl = (b - bh.astype(F32)).astype(BF16)
    dot = lambda x, y: jnp.dot(x, y, preferred_element_type=F32)
    return dot(ah, bh) + dot(ah, bl) + dot(al, bh)


def _tri_inv_multi(ms, n, chain):
    assert chain // SUB <= 4
    r, c = _iota2((n, n))
    same = (r // SUB) == (c // SUB)
    eye = jnp.where(r == c, 1.0, 0.0).astype(F32)
    ds = [jnp.where(same, m, 0.0) for m in ms]
    offs = [m - d for m, d in zip(ms, ds)]
    xs = [eye - d for d in ds]
    ps = ds
    for _ in range(3):
        ps = [_mm(p, p) for p in ps]
        xs = [x + _mm(x, p) for x, p in zip(xs, ps)]
    es = [_mm(x, o) for x, o in zip(xs, offs)]
    imes = [eye - e for e in es]
    e2s = [_mm(e, e) for e in es]
    ys = [i + _mm(i, e2) for i, e2 in zip(imes, e2s)]
    xs = [_mm(y, x) for y, x in zip(ys, xs)]
    res = [eye - x - _mm3(m, x) for m, x in zip(ms, xs)]
    return [x + _mm(x, rr) for x, rr in zip(xs, res)]


def _pack_weights_kernel(w_ref, *o_refs, ranges, gate_ranges):
    w = w_ref[...]
    for (lo, hi), o_ref in zip(ranges, o_refs):
        o_ref[...] = w[:, lo:hi].astype(BF16)
    if gate_ranges:
        cols = [w[:, lo:hi] for lo, hi in gate_ranges]
        used = sum(hi - lo for lo, hi in gate_ranges)
        cols.append(jnp.zeros((w.shape[0], LANES - used), F32))
        o_refs[-1][...] = jnp.concatenate(cols, axis=1).astype(BF16)


def _pack_weights(w, ranges, gate_ranges=()):
    rows, cols = w.shape
    tm = PACK_TM
    widths = [hi - lo for lo, hi in ranges] + ([LANES] if gate_ranges else [])
    return pl.pallas_call(
        functools.partial(_pack_weights_kernel, ranges=tuple(ranges), gate_ranges=tuple(gate_ranges)),
        grid=(rows // tm,),
        in_specs=[pl.BlockSpec((tm, cols), lambda i: (i, 0))],
        out_specs=[pl.BlockSpec((tm, wd), lambda i: (i, 0)) for wd in widths],
        out_shape=[jax.ShapeDtypeStruct((rows, wd), BF16) for wd in widths],
        compiler_params=_cparams("parallel"),
        name="pack_weights",
    )(w)


def _rms_proj_kernel(x_ref, g_ref, *refs, n_out):
    y = _rms_rows(x_ref[...], g_ref[...]).astype(BF16)
    for w_ref, o_ref in zip(refs[:n_out], refs[n_out:]):
        o_ref[...] = jnp.dot(y, w_ref[...], preferred_element_type=F32)


def _rms_proj(x, g, ws):
    n, d = x.shape
    tm = min(PROJ_TM, n)
    in_specs = [pl.BlockSpec((tm, d), lambda i: (i, 0)), pl.BlockSpec((1, d), lambda i: (0, 0))]
    in_specs += [pl.BlockSpec(w.shape, lambda i: (0, 0)) for w in ws]
    return pl.pallas_call(
        functools.partial(_rms_proj_kernel, n_out=len(ws)),
        grid=(n // tm,),
        in_specs=in_specs,
        out_specs=[pl.BlockSpec((tm, w.shape[1]), lambda i: (i, 0)) for w in ws],
        out_shape=[jax.ShapeDtypeStruct((n, w.shape[1]), F32) for w in ws],
        compiler_params=_cparams("parallel"),
        name="rms_proj",
    )(x, g.reshape(1, d), *ws)


ROW_TILE = D_MODEL // LANES


def _store_row_tiles(ref, x):
    for j in range(ROW_TILE):
        ref[pl.ds(j, x.shape[0], stride=ROW_TILE), :] = x[:, j * LANES:(j + 1) * LANES]


def _load_row_tiles(ref, rows):
    return jnp.concatenate([ref[pl.ds(j, rows, stride=ROW_TILE), :] for j in range(ROW_TILE)], axis=1)


def _rwkv7_kernel(p_ref, mu_ref, w0_ref, w2_ref, a0_ref, a2_ref, g2_ref, kk_ref, ka_ref, rk_ref,
                  lnw_ref, lnb_ref, o_ref, prev_ref, zt_ref):
    L = CHUNK
    npair = RW_HEADS // 2
    c = pl.program_id(1)

    @pl.when(c == 0)
    def _():
        prev_ref[...] = jnp.zeros_like(prev_ref)
        zt_ref[...] = jnp.zeros_like(zt_ref)

    x = p_ref[0]
    R = x.shape[0]
    nch = R // L
    row = lax.broadcasted_iota(I32, x.shape, 0)
    xs = jnp.where(row == 0, prev_ref[7:8, :], pltpu.roll(x, 1, 0))
    prev_ref[...] = x[R - 8:R, :]
    pm = x + mu_ref[...] * (xs - x)
    r_all = pm[:, 0:RW_DIM]
    k_all = pm[:, RW_DIM:2 * RW_DIM]
    v_all = pm[:, 2 * RW_DIM:3 * RW_DIM]
    wa = pm[:, 3 * RW_DIM:3 * RW_DIM + LANES]
    gl = pm[:, 3 * RW_DIM + LANES:]
    wlog = -jax.nn.softplus(-(w0_ref[...] + _mm(jnp.tanh(wa), w2_ref[...]))) - 0.5
    ld = -jnp.exp(wlog)
    a_all = jax.nn.sigmoid(a0_ref[...] + _mm(wa, a2_ref[...]))
    g_all = _mm(jax.nn.sigmoid(gl), g2_ref[...])

    tr, tc = _iota2((R, R))
    chunk_tril = jnp.where((tc <= tr) & ((tr // L) == (tc // L)), 1.0, 0.0).astype(BF16)
    cs_all = _cumsum_rows(chunk_tril, ld)
    seg = _seg_ones(LANES, RW_HEAD_DIM)
    lane = lax.broadcasted_iota(I32, (L, LANES), 1)
    hm = (lane < RW_HEAD_DIM, lane >= RW_HEAD_DIM)
    br, bc = _iota2((2 * L, 2 * L))
    bd = (br // L) == (bc // L)
    bd_strict = bd & (bc < br)
    bd_incl = bd & (bc <= br)
    fold = lambda z: z[0:L] + z[L:2 * L]
    both = lambda z: jnp.concatenate([jnp.where(hm[0], z, 0.0), jnp.where(hm[1], z, 0.0)], axis=0)

    units = []
    for ci in range(nch):
        rs = slice(ci * L, (ci + 1) * L)
        for p in range(npair):
            sl = slice(p * LANES, (p + 1) * LANES)
            r, k, v, a = r_all[rs, sl], k_all[rs, sl], v_all[rs, sl], a_all[rs, sl]
            cs, ldp = cs_all[rs, sl], ld[rs, sl]
            kkr = k * kk_ref[:, sl]
            kk = kkr * lax.rsqrt(_mm_split(kkr * kkr, seg) + 1e-6)
            k2 = k * (1.0 + (a - 1.0) * ka_ref[:, sl])
            b = kk * a
            cs_last = cs[L - 1:L, :]
            e_neg = jnp.exp(-cs)
            e_rem = jnp.exp(cs_last - cs)
            bhat = b * e_neg
            khat = k2 * e_neg
            units.append(dict(p=p, rs=rs, sl=sl, r=r, v=v, k2=k2, rhat=r * jnp.exp(cs), gam_last=jnp.exp(cs_last),
                              btil=b * e_rem, ktil=k2 * e_rem, a2=both(kk * jnp.exp(cs - ldp)), v2=both(v),
                              rhs4=jnp.concatenate([bhat, bhat, khat, khat], axis=0)))
    for q in units:
        lhs = jnp.concatenate([q["a2"], both(q["rhat"])], axis=0)
        q["g"] = _mm_nt(lhs, q["rhs4"])
    tinvs = _tri_inv_multi([jnp.where(bd_strict, q["g"][0:2 * L, 0:2 * L], 0.0) for q in units], 2 * L, L)
    for q, tinv in zip(units, tinvs):
        q["tinv"] = tinv
        q["x2"] = _mm(jnp.where(bd_strict, q["g"][0:2 * L, 2 * L:4 * L], 0.0), q["v2"])
    for q in units:
        uw = _mm(q["tinv"], jnp.concatenate([q["x2"], q["a2"]], axis=1))
        q["u0"] = -fold(uw[:, 0:LANES])
        q["w"] = fold(uw[:, LANES:])
        q["y0"] = fold(_mm(jnp.where(bd_incl, q["g"][2 * L:4 * L, 2 * L:4 * L], 0.0), q["v2"]))
    for q in units:
        rb = jnp.where(bd_incl, q["g"][2 * L:4 * L, 0:2 * L], 0.0)
        ruw = _mm(rb, jnp.concatenate([both(q["u0"]), both(q["w"])], axis=1))
        q["yc"] = q["y0"] + fold(ruw[:, 0:LANES])
        q["ry"] = q["rhat"] - fold(ruw[:, LANES:])
        q["c1"] = _mm_tn(jnp.concatenate([q["u0"], q["v"]], axis=0), jnp.concatenate([q["btil"], q["ktil"]], axis=0))
        q["c2"] = _mm_tn(q["w"], q["btil"])

    hr, hc = _iota2((LANES, LANES))
    head_bd = (hr // RW_HEAD_DIM) == (hc // RW_HEAD_DIM)
    zts = [zt_ref[p] for p in range(npair)]
    for ci in range(nch):
        qs = units[ci * npair:(ci + 1) * npair]
        for q, zt in zip(qs, zts):
            q["y"] = q["yc"] + _mm_nt(q["ry"], zt)
        zts = [zt * q["gam_last"] + jnp.where(head_bd, q["c1"] - _mm(zt, q["c2"]), 0.0) for q, zt in zip(qs, zts)]
    for p in range(npair):
        zt_ref[p] = zts[p]

    for q in units:
        sl, rs, y = q["sl"], q["rs"], q["y"]
        mean = _mm_split(y, seg) * (1.0 / RW_HEAD_DIM)
        yc = y - mean
        var = _mm_split(yc * yc, seg) * (1.0 / RW_HEAD_DIM)
        yn = yc * lax.rsqrt(var + RW_LN_EPS) * lnw_ref[:, sl] + lnb_ref[:, sl]
        bonus = _mm_split(q["r"] * q["k2"] * rk_ref[:, sl], seg) * q["v"]
        o_ref[0, rs, sl] = (yn + bonus) * g_all[rs, sl]


def _rwkv7(p, mu, w0, w2, a0, a2, g2, k_k, k_a, r_k, ln_w, ln_b):
    bsz, t, _ = p.shape
    row = lambda z: z.reshape(1, -1).astype(F32)
    w2p = jnp.concatenate([w2, jnp.zeros_like(w2)], axis=0).astype(BF16)
    a2p = jnp.concatenate([jnp.zeros_like(a2), a2], axis=0).astype(BF16)
    params = [row(mu), row(w0), w2p, row(a0), a2p, g2.astype(BF16), row(k_k), row(k_a), row(r_k), row(ln_w), row(ln_b)]
    full = lambda z: pl.BlockSpec(z.shape, lambda b, c: (0, 0))
    rows = RW_NCH * CHUNK
    return pl.pallas_call(
        _rwkv7_kernel,
        grid=(bsz, t // rows),
        in_specs=[pl.BlockSpec((1, rows, RW_IN), lambda b, c: (b, c, 0))] + [full(z) for z in params],
        out_specs=pl.BlockSpec((1, rows, RW_DIM), lambda b, c: (b, c, 0)),
        out_shape=jax.ShapeDtypeStruct((bsz, t, RW_DIM), F32),
        scratch_shapes=[pltpu.VMEM((8, RW_IN), F32), pltpu.VMEM((RW_HEADS // 2, LANES, LANES), F32)],
        compiler_params=_cparams("arbitrary", "arbitrary"),
        name="rwkv7_mix",
    )(p, *params)


def _hgrn2_kernel(p_ref, lb_ref, ng_ref, o_ref, st_ref):
    L = HG_ROWS
    c = pl.program_id(1)

    @pl.when(c == 0)
    def _():
        st_ref[...] = jnp.zeros_like(st_ref)

    x = p_ref[0]
    lb = lb_ref[...]
    q_all = jax.nn.silu(x[:, 0:HG_KDIM])
    fg = lb + (1.0 - lb) * jax.nn.sigmoid(x[:, HG_KDIM:2 * HG_KDIM])
    k_all = 1.0 - fg
    logf = jnp.log(fg)
    v_all = x[:, 2 * HG_KDIM:3 * HG_KDIM]
    gate = x[:, 3 * HG_KDIM:]
    tr, tc = _iota2((L, L))
    blk_tril = jnp.where((tc <= tr) & ((tr // SUB) == (tc // SUB)), 1.0, 0.0).astype(BF16)
    bc_all = _cumsum_rows(blk_tril, logf)
    seg = _seg_ones(LANES, HG_DIM)
    br, bcc = _iota2((LANES, LANES))
    bd = (br // HG_DIM) == (bcc // HG_DIM)
    half = SUB // 2
    t_lo = lax.broadcasted_iota(I32, (SUB, LANES), 0)
    t_hi = lax.broadcasted_iota(I32, (half, LANES), 0) + half

    nsub = L // SUB
    npair = HG_HEADS // 2
    rows_per_unit = half * SUB + half * half
    sub_of_row = lax.broadcasted_iota(I32, (L, LANES), 0) // SUB

    units = []
    parts = []
    for p in range(npair):
        sl = slice(p * LANES, (p + 1) * LANES)
        for j in range(nsub):
            rs = slice(j * SUB, (j + 1) * SUB)
            q, k, v, bc = q_all[rs, sl], k_all[rs, sl], v_all[rs, sl], bc_all[rs, sl]
            q_hi, bc_hi = q[half:], bc[half:]
            for s in range(SUB):
                if s < half:
                    diff = jnp.where(t_lo >= s, bc - bc[s:s + 1, :], NEG)
                    parts.append(jnp.exp(diff) * q * k[s:s + 1, :])
                else:
                    diff = jnp.where(t_hi >= s, bc_hi - bc[s:s + 1, :], NEG)
                    parts.append(jnp.exp(diff) * q_hi * k[s:s + 1, :])
            bend = bc[SUB - 1:SUB, :]
            units.append(dict(v=v, qt=q * jnp.exp(bc), dec=jnp.exp(bend), kd=k * jnp.exp(bend - bc)))
    score_all = _mm(jnp.concatenate(parts, axis=0), seg)
    for ui, un in enumerate(units):
        score = score_all[ui * rows_per_unit:(ui + 1) * rows_per_unit]
        v = un["v"]
        acc_lo = jnp.zeros((half, LANES), F32)
        acc_hi = jnp.zeros((half, LANES), F32)
        off = 0
        for s in range(SUB):
            vs = v[s:s + 1, :]
            if s < half:
                acc_lo = acc_lo + score[off:off + half] * vs
                acc_hi = acc_hi + score[off + half:off + SUB] * vs
                off += SUB
            else:
                acc_hi = acc_hi + score[off:off + half] * vs
                off += half
        un["intra"] = jnp.concatenate([acc_lo, acc_hi], axis=0)
    for p in range(npair):
        us = units[p * nsub:(p + 1) * nsub]
        v_pair = jnp.concatenate([un["v"] for un in us], axis=0)
        kd_pair = jnp.concatenate([un["kd"] for un in us], axis=0)
        kd_wide = jnp.concatenate([jnp.where(sub_of_row == j, kd_pair, 0.0) for j in range(nsub)], axis=1)
        upd = _mm_tn(v_pair, kd_wide)
        for j, un in enumerate(us):
            un["upd"] = jnp.where(bd, upd[:, j * LANES:(j + 1) * LANES], 0.0)

    for p in range(npair):
        sl = slice(p * LANES, (p + 1) * LANES)
        st = st_ref[p]
        outs = []
        for un in units[p * nsub:(p + 1) * nsub]:
            outs.append(un["intra"] + _mm_nt(un["qt"], st))
            st = st * un["dec"] + un["upd"]
        st_ref[p] = st
        o = jnp.concatenate(outs, axis=0)
        ms = _mm_split(o * o, seg) * (1.0 / HG_DIM)
        o_ref[0, :, sl] = o * lax.rsqrt(ms + NORM_EPS) * ng_ref[:, sl] * jax.nn.silu(gate[:, sl])


def _hgrn2(p, lb, norm_g):
    bsz, t, _ = p.shape
    lb = lb.reshape(1, HG_KDIM).astype(F32)
    ng = jnp.tile(norm_g.astype(F32), HG_HEADS).reshape(1, HG_KDIM)
    return pl.pallas_call(
        _hgrn2_kernel,
        grid=(bsz, t // HG_ROWS),
        in_specs=[pl.BlockSpec((1, HG_ROWS, HG_IN), lambda b, c: (b, c, 0)),
                  pl.BlockSpec((1, HG_KDIM), lambda b, c: (0, 0)), pl.BlockSpec((1, HG_KDIM), lambda b, c: (0, 0))],
        out_specs=pl.BlockSpec((1, HG_ROWS, HG_KDIM), lambda b, c: (b, c, 0)),
        out_shape=jax.ShapeDtypeStruct((bsz, t, HG_KDIM), F32),
        scratch_shapes=[pltpu.VMEM((HG_HEADS // 2, LANES, LANES), F32)],
        compiler_params=_cparams("arbitrary", "arbitrary"),
        name="hgrn2_mix",
    )(p, lb, ng)


def _gdn_kernel(p_ref, pg_ref, cw_ref, gb_ref, nal_ref, nalc_ref, ng_ref, o_ref, prev_ref, s_ref):
    L = CHUNK
    c = pl.program_id(1)

    @pl.when(c == 0)
    def _():
        prev_ref[...] = jnp.zeros_like(prev_ref)
        s_ref[...] = jnp.zeros_like(s_ref)

    x = p_ref[0]
    R = x.shape[0]
    nch = R // L
    xq = x[:, 0:3 * GD_DIM]
    xcat = jnp.concatenate([prev_ref[...], xq], axis=0)
    conv = xq * cw_ref[CONV_K - 1:CONV_K, :]
    for j in range(1, CONV_K):
        conv = conv + xcat[8 - j:8 - j + R, :] * cw_ref[CONV_K - 1 - j:CONV_K - j, :]
    prev_ref[...] = xq[R - 8:R, :]
    qkv = jax.nn.silu(conv)
    z = x[:, 3 * GD_DIM:]

    gt = pg_ref[0] + gb_ref[...]
    beta_all = jax.nn.sigmoid(gt)
    g_all = nal_ref[...] * jax.nn.softplus(gt)
    g_t = nalc_ref[...] * jax.nn.softplus(gt.T)
    tr, tc = _iota2((R, R))
    chunk_tril = jnp.where((tc <= tr) & ((tr // L) == (tc // L)), 1.0, 0.0).astype(BF16)
    gam_all = _cumsum_rows(chunk_tril, g_all)
    ur, uc = _iota2((R, 2 * R))
    triu2 = jnp.where(((ur // L) == (uc // (2 * L))) & ((ur % L) <= (uc % L)), 1.0, 0.0).astype(BF16)
    gam_t2 = _cumsum_cols(g_t, triu2)
    lr, lc = _iota2((L, L))
    incl = lc <= lr
    br, bc = _iota2((2 * L, 2 * L))
    bd = (br // L) == (bc // L)
    bd_strict = bd & (bc < br)
    bd_incl = bd & (bc <= br)
    lane2 = lax.broadcasted_iota(I32, (1, 2 * L), 1)
    zero = jnp.zeros((L, LANES), F32)

    units = []
    for ci in range(nch):
        rs = slice(ci * L, (ci + 1) * L)
        for h in range(GD_HEADS):
            q = qkv[rs, h * LANES:(h + 1) * LANES]
            k = qkv[rs, GD_DIM + h * LANES:GD_DIM + (h + 1) * LANES]
            v = qkv[rs, 2 * GD_DIM + h * LANES:2 * GD_DIM + (h + 1) * LANES]
            q = q * lax.rsqrt(jnp.sum(q * q, axis=-1, keepdims=True) + 1e-6) * (GD_HEAD_DIM ** -0.5)
            k = k * lax.rsqrt(jnp.sum(k * k, axis=-1, keepdims=True) + 1e-6)
            beta = beta_all[rs, h:h + 1]
            gam = gam_all[rs, GD_HEADS + h:GD_HEADS + h + 1]
            gam_row2 = gam_t2[GD_HEADS + h:GD_HEADS + h + 1, ci * 2 * L:(ci + 1) * 2 * L]
            gam_last = gam[L - 1:L, :]
            kb = k * beta
            units.append(dict(h=h, rs=rs, q=q, k=k, kb=kb, vb=v * beta, gam=gam, gam_row2=gam_row2,
                              gam_last=gam_last, kg=kb * jnp.exp(gam), qg=q * jnp.exp(gam),
                              kd=k * jnp.exp(gam_last - gam),
                              decay=jnp.exp(jnp.where(incl, gam - gam_row2[:, 0:L], NEG))))
    pairs = [(units[i], units[i + 1]) for i in range(0, len(units), 2)]
    ms = []
    for h0, h1 in pairs:
        lhs = jnp.concatenate([jnp.concatenate([h0["kb"], zero], axis=1),
                               jnp.concatenate([zero, h1["kb"]], axis=1)], axis=0)
        rhs = jnp.concatenate([jnp.concatenate([h0["k"], zero], axis=1),
                               jnp.concatenate([zero, h1["k"]], axis=1)], axis=0)
        gam_col = jnp.concatenate([h0["gam"], h1["gam"]], axis=0)
        gam_row = jnp.where(lane2 < L, h0["gam_row2"], h1["gam_row2"])
        decay2 = jnp.exp(jnp.where(bd_incl, gam_col - gam_row, NEG))
        ms.append(jnp.where(bd_strict, _mm_nt(lhs, rhs) * decay2, 0.0))
    tinvs = _tri_inv_multi(ms, 2 * L, L)
    for (h0, h1), tinv in zip(pairs, tinvs):
        rhs = jnp.concatenate([jnp.concatenate([h0["vb"], h0["kg"]], axis=1),
                               jnp.concatenate([h1["vb"], h1["kg"]], axis=1)], axis=0)
        uw = _mm(tinv, rhs)
        h0["u"], h0["w"] = uw[0:L, 0:LANES], uw[0:L, LANES:]
        h1["u"], h1["w"] = uw[L:2 * L, 0:LANES], uw[L:2 * L, LANES:]
    for hd in units:
        attn = _mm_nt(hd["q"], hd["k"]) * hd["decay"]
        uw = jnp.concatenate([hd["u"], hd["w"]], axis=1)
        auw = _mm(attn, uw)
        hd["o0"] = auw[:, 0:LANES]
        hd["qs"] = hd["qg"] - auw[:, LANES:]
        cc = _mm_tn(hd["kd"], uw)
        hd["c1"], hd["c2"] = cc[:, 0:LANES], cc[:, LANES:]

    ss = [s_ref[h] for h in range(GD_HEADS)]
    for ci in range(nch):
        hds = units[ci * GD_HEADS:(ci + 1) * GD_HEADS]
        for hd, s in zip(hds, ss):
            hd["o"] = hd["o0"] + _mm(hd["qs"], s)
        ss = [s * jnp.exp(hd["gam_last"]) + hd["c1"] - _mm(hd["c2"], s) for hd, s in zip(hds, ss)]
    for h in range(GD_HEADS):
        s_ref[h] = ss[h]
    for hd in units:
        sl = slice(hd["h"] * LANES, (hd["h"] + 1) * LANES)
        o_ref[0, hd["rs"], sl] = _rms_rows(hd["o"], ng_ref[...]) * jax.nn.silu(z[hd["rs"], sl])


def _gate_row(lo, vals):
    return jnp.zeros((1, LANES), F32).at[0, lo:lo + vals.shape[0]].set(vals.astype(F32))


def _gdn(p, pg, conv_w, a_log, dt_bias, norm_g):
    bsz, t, _ = p.shape
    gbias = _gate_row(GD_HEADS, dt_bias)
    nal = _gate_row(GD_HEADS, -jnp.exp(a_log))
    nal_col = nal.reshape(LANES, 1)
    ng = norm_g.reshape(1, GD_HEAD_DIM).astype(F32)
    full = lambda z: pl.BlockSpec(z.shape, lambda b, c: (0, 0))
    rows = GD_NCH * CHUNK
    return pl.pallas_call(
        _gdn_kernel,
        grid=(bsz, t // rows),
        in_specs=[pl.BlockSpec((1, rows, GD_MAIN), lambda b, c: (b, c, 0)),
                  pl.BlockSpec((1, rows, LANES), lambda b, c: (b, c, 0)),
                  full(conv_w), full(gbias), full(nal), full(nal_col), full(ng)],
        out_specs=pl.BlockSpec((1, rows, GD_DIM), lambda b, c: (b, c, 0)),
        out_shape=jax.ShapeDtypeStruct((bsz, t, GD_DIM), F32),
        scratch_shapes=[pltpu.VMEM((8, 3 * GD_DIM), F32), pltpu.VMEM((GD_HEADS, LANES, LANES), F32)],
        compiler_params=_cparams("arbitrary", "arbitrary"),
        name="gdn_mix",
    )(p, pg, conv_w.astype(F32), gbias, nal, nal_col, ng)


def _mlstm_kernel(p_ref, pg_ref, gb_ref, ng_ref, o_ref, c_ref, n_ref, m_ref):
    L = ML_CHUNK
    ci = pl.program_id(1)

    @pl.when(ci == 0)
    def _():
        c_ref[...] = jnp.zeros_like(c_ref)
        n_ref[...] = jnp.zeros_like(n_ref)
        m_ref[...] = jnp.zeros_like(m_ref)

    x = p_ref[0]
    R = x.shape[0]
    nch = R // L
    gt = pg_ref[0] + gb_ref[...]
    logf = jax.nn.log_sigmoid(gt)
    gt_t = gt.T
    logf_t = jax.nn.log_sigmoid(gt_t)
    tr, tc = _iota2((R, R))
    same = (tr // L) == (tc // L)
    bc_all = _cumsum_rows(jnp.where(same & (tc <= tr), 1.0, 0.0).astype(BF16), logf)
    bc_t = _cumsum_cols(logf_t, jnp.where(same & (tr <= tc), 1.0, 0.0).astype(BF16))
    lr, lc = _iota2((L, L))
    incl = lc <= lr
    i_lo, f_lo = 2 * GD_HEADS, 2 * GD_HEADS + ML_HEADS

    units = []
    m_run = [m_ref[h][0:1, 0:1] for h in range(ML_HEADS)]
    for ci in range(nch):
        rs = slice(ci * L, (ci + 1) * L)
        for h in range(ML_HEADS):
            sl = slice(h * LANES, (h + 1) * LANES)
            q = x[rs, sl]
            k = x[rs, ML_DIM + h * LANES:ML_DIM + (h + 1) * LANES] * (ML_HEAD_DIM ** -0.5)
            v = x[rs, 2 * ML_DIM + h * LANES:2 * ML_DIM + (h + 1) * LANES]
            bc = bc_all[rs, f_lo + h:f_lo + h + 1]
            bc_row = bc_t[f_lo + h:f_lo + h + 1, rs]
            ig = gt[rs, i_lo + h:i_lo + h + 1]
            ig_row = gt_t[i_lo + h:i_lo + h + 1, rs]
            m_prev = m_run[h]
            b_last = bc[L - 1:L, :]
            d_log = jnp.where(incl, bc - bc_row + ig_row, NEG)
            inter_log = bc + m_prev
            m_t = jnp.maximum(inter_log, jnp.max(d_log, axis=-1, keepdims=True))
            upd_log = b_last - bc + ig
            m_new = jnp.maximum(b_last + m_prev, jnp.max(upd_log, axis=0, keepdims=True))
            m_run[h] = m_new
            wk = jnp.exp(upd_log - m_new) * k
            units.append(dict(h=h, rs=rs, sl=sl, q=q, v=v, m_t=m_t, wk=wk, inter_w=jnp.exp(inter_log - m_t),
                              dec=jnp.exp(b_last + m_prev - m_new), sc=_mm_nt(q, k) * jnp.exp(d_log - m_t)))
    for hd in units:
        hd["scv"] = _mm(hd["sc"], hd["v"])
        hd["kv"] = _mm_tn(hd["wk"], hd["v"])
    cs = [c_ref[h] for h in range(ML_HEADS)]
    ns = [n_ref[h][0:1, :] for h in range(ML_HEADS)]
    for hd in units:
        h = hd["h"]
        q, sc, inter_w, dec = hd["q"], hd["sc"], hd["inter_w"], hd["dec"]
        num = inter_w * _mm(q, cs[h]) + hd["scv"]
        den = inter_w * jnp.sum(q * ns[h], axis=-1, keepdims=True) + jnp.sum(sc, axis=-1, keepdims=True)
        hd["hh"] = num / jnp.maximum(jnp.abs(den), jnp.exp(-hd["m_t"]))
        cs[h] = dec * cs[h] + hd["kv"]
        ns[h] = dec * ns[h] + jnp.sum(hd["wk"], axis=0, keepdims=True)
    for h in range(ML_HEADS):
        c_ref[h] = cs[h]
        n_ref[h] = jnp.broadcast_to(ns[h], (8, LANES))
        m_ref[h] = jnp.broadcast_to(m_run[h], (8, LANES))
    for hd in units:
        h, rs, sl = hd["h"], hd["rs"], hd["sl"]
        og = jax.nn.sigmoid(x[rs, 3 * ML_DIM + h * LANES:3 * ML_DIM + (h + 1) * LANES])
        o_ref[0, rs, sl] = og * _rms_rows(hd["hh"], ng_ref[:, sl])


def _mlstm(p, pg, i_bias, f_bias, norm_g):
    bsz, t, _ = p.shape
    gbias = _gate_row(2 * GD_HEADS, jnp.concatenate([i_bias, f_bias]))
    ng = norm_g.reshape(1, ML_DIM).astype(F32)
    lc = ML_NCH * ML_CHUNK
    full = lambda z: pl.BlockSpec(z.shape, lambda b, c: (0, 0))
    return pl.pallas_call(
        _mlstm_kernel,
        grid=(bsz, t // lc),
        in_specs=[pl.BlockSpec((1, lc, ML_MAIN), lambda b, c: (b, c, 0)),
                  pl.BlockSpec((1, lc, LANES), lambda b, c: (b, c, 0)), full(gbias), full(ng)],
        out_specs=pl.BlockSpec((1, lc, ML_DIM), lambda b, c: (b, c, 0)),
        out_shape=jax.ShapeDtypeStruct((bsz, t, ML_DIM), F32),
        scratch_shapes=[pltpu.VMEM((ML_HEADS, LANES, LANES), F32), pltpu.VMEM((ML_HEADS, 8, LANES), F32),
                        pltpu.VMEM((ML_HEADS, 8, LANES), F32)],
        compiler_params=_cparams("arbitrary", "arbitrary"),
        name="mlstm_mix",
    )(p, pg, gbias, ng)


def _route(h, g_ref, wh_ref, wl_ref, b_ref, off_ref):
    tm = h.shape[0]
    xn = _rms_rows(h, g_ref[...])
    xh = xn.astype(BF16)
    xl = (xn - xh.astype(F32)).astype(BF16)
    dot = lambda a, b: jnp.dot(a, b, preferred_element_type=F32)
    logits = dot(xh, wh_ref[...]) + dot(xh, wl_ref[...]) + dot(xl, wh_ref[...]) + b_ref[...]
    lt = logits.T[0:ROUTE_ROWS, :]
    row = lax.broadcasted_iota(I32, lt.shape, 0)
    big = jnp.int32(1 << 20)
    is_grp = (row >= N_EXPERTS) & (row < N_EXPERTS + N_GROUPS)
    lg = jnp.where(is_grp, lt, NEG)
    gmax = jnp.max(lg, axis=0, keepdims=True)
    p_top = 1.0 / jnp.sum(jnp.exp(lg - gmax), axis=0, keepdims=True)
    g_idx = jnp.min(jnp.where(lg == gmax, row, big), axis=0, keepdims=True) - N_EXPERTS
    valid = (row < N_EXPERTS) & ((row // EXPERTS_PER_GROUP) == g_idx)
    v1 = jnp.where(valid, lt, NEG)
    m1 = jnp.max(v1, axis=0, keepdims=True)
    i1 = jnp.min(jnp.where(v1 == m1, row, big), axis=0, keepdims=True)
    v2 = jnp.where(row == i1, NEG, v1)
    m2 = jnp.max(v2, axis=0, keepdims=True)
    i2 = jnp.min(jnp.where(v2 == m2, row, big), axis=0, keepdims=True)
    e21 = jnp.exp(m2 - m1)
    gate1 = p_top / (1.0 + e21)
    gate2 = p_top * e21 / (1.0 + e21)

    sel1 = row == i1
    sel2 = row == i2
    onehot = jnp.where(sel1 | sel2, 1.0, 0.0)
    tr, tc = _iota2((tm, tm))
    earlier = jnp.where(tr < tc, 1.0, 0.0).astype(BF16)
    off = off_ref[0:ROUTE_ROWS, 0:1]
    prefix = jnp.dot(onehot.astype(BF16), earlier, preferred_element_type=F32) + off
    rank1 = jnp.sum(jnp.where(sel1, prefix, 0.0), axis=0, keepdims=True)
    rank2 = jnp.sum(jnp.where(sel2, prefix, 0.0), axis=0, keepdims=True)
    off_ref[0:ROUTE_ROWS, :] = jnp.broadcast_to(off + jnp.sum(onehot, axis=1, keepdims=True), (ROUTE_ROWS, LANES))

    packed = jnp.concatenate([i1.astype(F32), i2.astype(F32), rank1, rank2, gate1, gate2,
                              jnp.zeros((2, tm), F32)], axis=0)
    return xn, packed


def _out_proj_route_kernel(h_ref, ya_ref, yb_ref, wa_ref, wb_ref, g_ref, wh_ref, wl_ref, b_ref,
                           o_ref, xt_ref, rg_ref, cnt_ref, dest_ref, off_ref, rt_ref):
    phase = pl.program_id(0)
    s = pl.program_id(1)
    tm = h_ref.shape[0]

    @pl.when((phase == 0) & (s == 0))
    def _():
        off_ref[...] = jnp.zeros_like(off_ref)

    @pl.when(phase == 0)
    def _():
        out = h_ref[...] + _mm(ya_ref[...], wa_ref[...]) + _mm(yb_ref[...], wb_ref[...])
        o_ref[...] = out
        xn, packed = _route(out, g_ref, wh_ref, wl_ref, b_ref, off_ref)
        _store_row_tiles(xt_ref, xn)
        rt_ref[s] = packed
        wide = jnp.concatenate([packed, jnp.zeros((LANES - 8, tm), F32)], axis=0)
        rg_ref[...] = wide.T
        cnt_ref[...] = off_ref[...]

    @pl.when(phase == 1)
    def _():
        counts = off_ref[...]
        tiles_per = jnp.floor((counts + (MOE_TM - 1)) * (1.0 / MOE_TM))
        er, ec = _iota2((LANES, LANES))
        tile_end = _mm(jnp.where(ec <= er, 1.0, 0.0), tiles_per)
        seg_start = ((tile_end - tiles_per) * MOE_TM)[0:ROUTE_ROWS, 0:1]
        blk = rt_ref[s]
        row = lax.broadcasted_iota(I32, (ROUTE_ROWS, tm), 0).astype(F32)
        pick = lambda e: jnp.sum(jnp.where(row == e, seg_start, 0.0), axis=0, keepdims=True)
        d1 = pick(blk[0:1]) + blk[2:3]
        d2 = pick(blk[1:2]) + blk[3:4]
        dest_ref[...] = jnp.concatenate([d1, d2, jnp.zeros((6, tm), F32)], axis=0).astype(I32)


def _out_proj_route(h, ya, yb, w_out, g_ffn, w_group, b_group, w_router, b_router):
    n, d = h.shape
    da, db = ya.shape[1], yb.shape[1]
    tm = min(PROJ_TM, n)
    wa = w_out[:da].astype(BF16)
    wb = w_out[da:].astype(BF16)
    pad = LANES - N_EXPERTS - N_GROUPS
    w_cat = jnp.concatenate([w_router, w_group, jnp.zeros((d, pad), F32)], axis=1)
    b_cat = jnp.concatenate([b_router, b_group, jnp.zeros((pad,), F32)]).reshape(1, LANES)
    w_hi = w_cat.astype(BF16)
    w_lo = (w_cat - w_hi.astype(F32)).astype(BF16)
    steps = n // tm
    row = lambda p, s: (s * (1 - p) + (steps - 1) * p, 0)
    fixed = lambda p, s: (0, 0)
    return pl.pallas_call(
        _out_proj_route_kernel,
        grid=(2, steps),
        in_specs=[pl.BlockSpec((tm, d), row), pl.BlockSpec((tm, da), row), pl.BlockSpec((tm, db), row),
                  pl.BlockSpec((da, d), fixed), pl.BlockSpec((db, d), fixed), pl.BlockSpec((1, d), fixed),
                  pl.BlockSpec((d, LANES), fixed), pl.BlockSpec((d, LANES), fixed), pl.BlockSpec((1, LANES), fixed)],
        out_specs=[pl.BlockSpec((tm, d), row), pl.BlockSpec((tm * ROW_TILE, LANES), row),
                   pl.BlockSpec((tm, LANES), row), pl.BlockSpec((LANES, LANES), fixed),
                   pl.BlockSpec((8, tm), lambda p, s: (0, s * p))],
        out_shape=[jax.ShapeDtypeStruct((n, d), F32), jax.ShapeDtypeStruct((n * ROW_TILE, LANES), F32),
                   jax.ShapeDtypeStruct((n, LANES), F32), jax.ShapeDtypeStruct((LANES, LANES), F32),
                   jax.ShapeDtypeStruct((8, n), I32)],
        scratch_shapes=[pltpu.VMEM((LANES, LANES), F32), pltpu.VMEM((steps, 8, tm), F32)],
        compiler_params=_cparams("arbitrary", "arbitrary"),
        name="out_proj_route",
    )(h, ya, yb, wa, wb, g_ffn.reshape(1, d), w_hi, w_lo, b_cat)


def _gather_rows(src_hbm, idx_ref, base, dst, sem):
    def body(j, carry):
        r = pl.multiple_of(idx_ref[base + j] * ROW_TILE, ROW_TILE)
        pltpu.make_async_copy(src_hbm.at[pl.ds(r, ROW_TILE), :],
                              dst.at[pl.ds(pl.multiple_of(j * ROW_TILE, ROW_TILE), ROW_TILE), :],
                              sem).start()
        return carry

    lax.fori_loop(0, dst.shape[0] // ROW_TILE, body, 0, unroll=GATHER_UNROLL)


def _wait_rows(src_hbm, dst, sem):
    pltpu.make_async_copy(src_hbm.at[pl.ds(0, dst.shape[0]), :], dst, sem).wait()


def _dispatch_kernel(d1_ref, d2_ref, lo_ref, hi_ref, nt_ref, x_ref, xs_hbm, zero_ref, sem):
    rows = x_ref.shape[0]
    base = pl.program_id(0) * (rows // ROW_TILE)
    tile_rows = MOE_TM * ROW_TILE

    def pad_copy(p):
        return pltpu.make_async_copy(zero_ref.at[pl.ds(0, ROW_TILE), :],
                                     xs_hbm.at[pl.ds(pl.multiple_of(p * ROW_TILE, ROW_TILE), ROW_TILE), :], sem.at[1])

    def tile_copy(t):
        return pltpu.make_async_copy(zero_ref, xs_hbm.at[pl.ds(pl.multiple_of(t * tile_rows, tile_rows), tile_rows), :],
                                     sem.at[1])

    @pl.when(pl.program_id(0) == 0)
    def _():
        zero_ref[...] = jnp.zeros_like(zero_ref)

        def loop(lo, hi, make, wait):
            def body(p, carry):
                if wait:
                    make(p).wait()
                else:
                    make(p).start()
                return carry

            lax.fori_loop(lo, hi, body, 0)

        n_tiles = xs_hbm.shape[0] // tile_rows
        for wait in (False, True):
            for e in range(lo_ref.shape[0]):
                loop(lo_ref[e], hi_ref[e], pad_copy, wait)
            loop(nt_ref[0], n_tiles, tile_copy, wait)

    for idx_ref in (d1_ref, d2_ref):
        def body(j, carry, idx_ref=idx_ref):
            r = pl.multiple_of(idx_ref[base + j] * ROW_TILE, ROW_TILE)
            pltpu.make_async_copy(x_ref.at[pl.ds(pl.multiple_of(j * ROW_TILE, ROW_TILE), ROW_TILE), :],
                                  xs_hbm.at[pl.ds(r, ROW_TILE), :], sem.at[0]).start()
            return carry

        lax.fori_loop(0, rows // ROW_TILE, body, 0, unroll=GATHER_UNROLL)
    for _ in range(2):
        pltpu.make_async_copy(x_ref, xs_hbm.at[pl.ds(0, rows), :], sem.at[0]).wait()


def _dispatch(xn_tiles, d1, d2, pad_lo, pad_hi, n_used, n_rows):
    n = d1.shape[0]
    tm = min(PROJ_TM, n)
    grid_spec = pltpu.PrefetchScalarGridSpec(
        num_scalar_prefetch=5,
        grid=(n // tm,),
        in_specs=[pl.BlockSpec((tm * ROW_TILE, LANES), lambda i, *_: (i, 0))],
        out_specs=pl.BlockSpec(memory_space=pl.ANY),
        scratch_shapes=[pltpu.VMEM((MOE_TM * ROW_TILE, LANES), F32), pltpu.SemaphoreType.DMA((2,))],
    )
    return pl.pallas_call(
        _dispatch_kernel,
        grid_spec=grid_spec,
        out_shape=jax.ShapeDtypeStruct((n_rows * ROW_TILE, LANES), F32),
        compiler_params=_cparams("arbitrary"),
        name="moe_dispatch",
    )(d1, d2, pad_lo, pad_hi, n_used, xn_tiles)


def _expert_kernel(te_ref, nt_ref, x_ref, wg_ref, wu_ref, wd_ref, o_ref, wgb, wub, wdb):
    tm = x_ref.shape[0] // ROW_TILE
    i = pl.program_id(0)

    @pl.when((i == 0) | (te_ref[i] != te_ref[jnp.maximum(i - 1, 0)]))
    def _():
        wgb[...] = wg_ref[0].astype(BF16)
        wub[...] = wu_ref[0].astype(BF16)
        wdb[...] = wd_ref[0].astype(BF16)

    @pl.when(i < nt_ref[0])
    def _():
        xn = _load_row_tiles(x_ref, tm).astype(BF16)
        gate = jnp.dot(xn, wgb[...], preferred_element_type=F32)
        up = jnp.dot(xn, wub[...], preferred_element_type=F32)
        hid = (jax.nn.silu(gate) * up).astype(BF16)
        _store_row_tiles(o_ref, jnp.dot(hid, wdb[...], preferred_element_type=F32))

    @pl.when(i >= nt_ref[0])
    def _():
        o_ref[...] = jnp.zeros_like(o_ref)


def _experts(xs, tile_expert, n_used, w_gate, w_up, w_down):
    d = D_MODEL
    n_tiles = tile_expert.shape[0]
    tm = MOE_TM
    de = w_gate.shape[-1]
    wspec = lambda rows, cols: pl.BlockSpec((1, rows, cols), lambda i, te, nt: (te[i], 0, 0))
    grid_spec = pltpu.PrefetchScalarGridSpec(
        num_scalar_prefetch=2,
        grid=(n_tiles,),
        in_specs=[pl.BlockSpec((tm * ROW_TILE, LANES), lambda i, te, nt: (jnp.minimum(i, nt[0] - 1), 0)),
                  wspec(d, de), wspec(d, de), wspec(de, d)],
        out_specs=pl.BlockSpec((tm * ROW_TILE, LANES), lambda i, te, nt: (i, 0)),
        scratch_shapes=[pltpu.VMEM((d, de), BF16), pltpu.VMEM((d, de), BF16), pltpu.VMEM((de, d), BF16)],
    )
    return pl.pallas_call(
        _expert_kernel,
        grid_spec=grid_spec,
        out_shape=jax.ShapeDtypeStruct((n_tiles * tm * ROW_TILE, LANES), F32),
        compiler_params=_cparams("arbitrary"),
        name="moe_experts",
    )(tile_expert, n_used, xs, w_gate, w_up, w_down)


def _combine_kernel(d1_ref, d2_ref, ys_hbm, h_ref, rg_ref, g_ref, *refs, n_proj, final_norm):
    w_refs, o_ref, p_refs = refs[:n_proj], refs[n_proj], refs[n_proj + 1:2 * n_proj + 1]
    y1, y2, sem = refs[2 * n_proj + 1:]
    tm = h_ref.shape[0]
    i = pl.program_id(0)
    slot = i % 2

    def gather(tile, s):
        _gather_rows(ys_hbm, d1_ref, tile * tm, y1.at[s], sem.at[0, s])
        _gather_rows(ys_hbm, d2_ref, tile * tm, y2.at[s], sem.at[1, s])

    @pl.when(i == 0)
    def _():
        gather(0, 0)

    @pl.when(i + 1 < pl.num_programs(0))
    def _():
        gather(i + 1, 1 - slot)

    _wait_rows(ys_hbm, y1.at[slot], sem.at[0, slot])
    _wait_rows(ys_hbm, y2.at[slot], sem.at[1, slot])
    rg = rg_ref[...]
    out = (h_ref[...] + rg[:, 4:5] * _load_row_tiles(y1.at[slot], tm)
           + rg[:, 5:6] * _load_row_tiles(y2.at[slot], tm))
    if final_norm:
        out = _rms_rows(out, g_ref[...])
    o_ref[...] = out
    if n_proj:
        y = _rms_rows(out, g_ref[...]).astype(BF16)
        for w_ref, p_ref in zip(w_refs, p_refs):
            p_ref[...] = jnp.dot(y, w_ref[...], preferred_element_type=F32)


def _combine(h, ys, d1, d2, rg, g, ws=(), final_norm=False):
    n, d = h.shape
    tm = min(PROJ_TM, n)
    row = lambda i, a, b: (i, 0)
    fixed = lambda i, a, b: (0, 0)
    grid_spec = pltpu.PrefetchScalarGridSpec(
        num_scalar_prefetch=2,
        grid=(n // tm,),
        in_specs=[pl.BlockSpec(memory_space=pl.ANY), pl.BlockSpec((tm, d), row), pl.BlockSpec((tm, LANES), row),
                  pl.BlockSpec((1, d), fixed)] + [pl.BlockSpec(w.shape, fixed) for w in ws],
        out_specs=[pl.BlockSpec((tm, d), row)] + [pl.BlockSpec((tm, w.shape[1]), row) for w in ws],
        scratch_shapes=[pltpu.VMEM((2, tm * ROW_TILE, LANES), F32), pltpu.VMEM((2, tm * ROW_TILE, LANES), F32),
                        pltpu.SemaphoreType.DMA((2, 2))],
    )
    return pl.pallas_call(
        functools.partial(_combine_kernel, n_proj=len(ws), final_norm=final_norm),
        grid_spec=grid_spec,
        out_shape=[jax.ShapeDtypeStruct((n, d), F32)] + [jax.ShapeDtypeStruct((n, w.shape[1]), F32) for w in ws],
        compiler_params=_cparams("arbitrary"),
        name="moe_combine",
    )(d1, d2, ys, h, rg, g.reshape(1, d), *ws)


def _hmoe_residual(h, xn_tiles, rg, cnt, dest, layer, w_gate, w_up, w_down, g, ws=(), final_norm=False):
    n, d = h.shape
    stack = lambda w: w.reshape((-1,) + w.shape[2:])

    counts = cnt[:N_EXPERTS, 0].astype(I32)
    n_tiles = (2 * n) // MOE_TM + N_EXPERTS
    tile_end = jnp.cumsum((counts + MOE_TM - 1) // MOE_TM)
    d1, d2 = dest[0], dest[1]
    tile_ids = jnp.arange(n_tiles, dtype=I32)
    tile_expert = jnp.minimum(jnp.sum((tile_end[None, :] <= tile_ids[:, None]).astype(I32), axis=1), N_EXPERTS - 1)
    tile_expert = tile_expert + layer * N_EXPERTS
    seg_end = tile_end * MOE_TM
    seg_start = seg_end - ((counts + MOE_TM - 1) // MOE_TM) * MOE_TM
    pad_lo = (seg_start + counts).astype(I32)
    pad_hi = seg_end.astype(I32)
    n_used = tile_end[N_EXPERTS - 1:].astype(I32)
    xs = _dispatch(xn_tiles, d1, d2, pad_lo, pad_hi, n_used, n_tiles * MOE_TM)
    ys = _experts(xs, tile_expert, n_used, stack(w_gate), stack(w_up), stack(w_down))
    return _combine(h, ys, d1, d2, rg, g, ws, final_norm)


def kernel(x, norm_mix, norm_ffn, norm_final, ev_w_in, ev_mu, rw_w0, rw_w2, rw_a0, rw_a2, rw_g2, rw_k_k, rw_k_a, rw_r_k, rw_ln_w, rw_ln_b, hg_lb_logits, hg_norm, ev_w_out, od_w_in, gd_conv, gd_a_log, gd_dt_bias, gd_norm, ml_i_bias, ml_f_bias, ml_norm, od_w_out, moe_w_group, moe_b_group, moe_w_router, moe_b_router, moe_w_gate, moe_w_up, moe_w_down):
    bsz, t, d = x.shape
    n = bsz * t
    depth = norm_mix.shape[0]
    lb_table = jnp.cumsum(jax.nn.softmax(hg_lb_logits.astype(F32), axis=0), axis=0)
    def in_proj_weights(layer):
        j = layer // 2
        if layer % 2 == 0:
            return _pack_weights(ev_w_in[j], [(0, RW_IN), (RW_IN, RW_IN + HG_IN)])
        return _pack_weights(od_w_in[j], [(0, GD_MAIN), (GD_IN, GD_IN + ML_MAIN)],
                             gate_ranges=[(GD_MAIN, GD_IN), (GD_IN + ML_MAIN, GD_IN + ML_MAIN + 2 * ML_HEADS)])

    h = x.reshape(n, d)
    proj = _rms_proj(h, norm_mix[0], in_proj_weights(0))
    for layer in range(depth):
        j = layer // 2
        if layer % 2 == 0:
            p_rw, p_hg = proj
            ya = _rwkv7(p_rw.reshape(bsz, t, RW_IN), ev_mu[j], rw_w0[j], rw_w2[j], rw_a0[j], rw_a2[j], rw_g2[j],
                        rw_k_k[j], rw_k_a[j], rw_r_k[j], rw_ln_w[j], rw_ln_b[j])
            yb = _hgrn2(p_hg.reshape(bsz, t, HG_IN), lb_table[j], hg_norm[j])
            mix_a, mix_b, w_out = ya.reshape(n, RW_DIM), yb.reshape(n, HG_KDIM), ev_w_out[j]
        else:
            p_gd, p_ml, p_gt = proj
            p_gt = p_gt.reshape(bsz, t, LANES)
            yc = _gdn(p_gd.reshape(bsz, t, GD_MAIN), p_gt, gd_conv[j], gd_a_log[j], gd_dt_bias[j], gd_norm[j])
            yd = _mlstm(p_ml.reshape(bsz, t, ML_MAIN), p_gt, ml_i_bias[j], ml_f_bias[j], ml_norm[j])
            mix_a, mix_b, w_out = yc.reshape(n, GD_DIM), yd.reshape(n, ML_DIM), od_w_out[j]
        h, xn_tiles, rg, cnt, dest = _out_proj_route(h, mix_a, mix_b, w_out, norm_ffn[layer], moe_w_group[layer],
                                                     moe_b_group[layer], moe_w_router[layer], moe_b_router[layer])
        if layer == depth - 1:
            (h,) = _hmoe_residual(h, xn_tiles, rg, cnt, dest, layer, moe_w_gate, moe_w_up, moe_w_down,
                                  norm_final, final_norm=True)
        else:
            h, *proj = _hmoe_residual(h, xn_tiles, rg, cnt, dest, layer, moe_w_gate, moe_w_up, moe_w_down,
                                      norm_mix[layer + 1], ws=in_proj_weights(layer + 1))
    return h.reshape(bsz, t, d)
```

```python
import functools

import jax
import jax.numpy as jnp
from jax import lax
from jax.experimental import pallas as pl
from jax.experimental.pallas import tpu as pltpu

F32 = jnp.float32
BF16 = jnp.bfloat16
I32 = jnp.int32

D_MODEL = 1024
NORM_EPS = 1e-6
RW_HEADS, RW_HEAD_DIM = 8, 64
RW_DIM = RW_HEADS * RW_HEAD_DIM
R_DECAY, R_AAA, R_GATE = 64, 64, 128
RW_IN = 3 * RW_DIM + R_DECAY + R_AAA + R_GATE
RW_LN_EPS = 64e-5
HG_HEADS, HG_DIM = 8, 64
HG_KDIM = HG_HEADS * HG_DIM
HG_IN = 4 * HG_KDIM
GD_HEADS, GD_HEAD_DIM = 4, 128
GD_DIM = GD_HEADS * GD_HEAD_DIM
CONV_K = 4
GD_MAIN = 4 * GD_DIM
GD_IN = GD_MAIN + 2 * GD_HEADS
ML_HEADS, ML_HEAD_DIM = 4, 128
ML_DIM = ML_HEADS * ML_HEAD_DIM
ML_MAIN = 4 * ML_DIM
N_GROUPS, EXPERTS_PER_GROUP = 4, 8
N_EXPERTS = N_GROUPS * EXPERTS_PER_GROUP
D_EXPERT = 256

LANES = 128
VMEM_LIMIT_BYTES = 48 * 1024 * 1024

PROJ_TM = 256
PACK_TM = 128
CHUNK = 64
SUB = 16
RW_NCH = 4
GD_NCH = 8
HG_ROWS = 256
ML_CHUNK = 128
ML_NCH = 2
MOE_TM = 256
ROUTE_ROWS = 40
GATHER_UNROLL = 8
NEG = -1e30


def _cparams(*sem):
    return pltpu.CompilerParams(dimension_semantics=sem, vmem_limit_bytes=VMEM_LIMIT_BYTES)


def _mm(a, b):
    return jnp.dot(a.astype(BF16), b.astype(BF16), preferred_element_type=F32)


def _mm_nt(a, b):
    return lax.dot_general(a.astype(BF16), b.astype(BF16), (((1,), (1,)), ((), ())), preferred_element_type=F32)


def _mm_tn(a, b):
    return lax.dot_general(a.astype(BF16), b.astype(BF16), (((0,), (0,)), ((), ())), preferred_element_type=F32)


def _mm_split(x, ones_bf16):
    hi = x.astype(BF16)
    lo = (x - hi.astype(F32)).astype(BF16)
    return (jnp.dot(hi, ones_bf16, preferred_element_type=F32) + jnp.dot(lo, ones_bf16, preferred_element_type=F32))


def _iota2(shape):
    return lax.broadcasted_iota(I32, shape, 0), lax.broadcasted_iota(I32, shape, 1)


def _rms_rows(x, g, eps=NORM_EPS):
    return x * lax.rsqrt(jnp.mean(x * x, axis=-1, keepdims=True) + eps) * g


def _seg_ones(width, seg):
    r, c = _iota2((width, width))
    return jnp.where((r // seg) == (c // seg), 1.0, 0.0).astype(BF16)


def _split3(x):
    x1 = x.astype(BF16)
    r1 = x - x1.astype(F32)
    x2 = r1.astype(BF16)
    return x1, x2, (r1 - x2.astype(F32)).astype(BF16)


def _cumsum_rows(tri_bf16, x):
    return sum(jnp.dot(tri_bf16, t, preferred_element_type=F32) for t in _split3(x))


def _cumsum_cols(x, tri_bf16):
    return sum(jnp.dot(t, tri_bf16, preferred_element_type=F32) for t in _split3(x))


def _mm3(a, b):
    ah = a.astype(BF16)
    al = (a - ah.astype(F32)).astype(BF16)
    bh = b.astype(BF16)
    bl = (b - bh.astype(F32)).astype(BF16)
    dot = lambda x, y: jnp.dot(x, y, preferred_element_type=F32)
    return dot(ah, bh) + dot(ah, bl) + dot(al, bh)


def _tri_inv_multi(ms, n, chain):
    assert chain // SUB <= 4
    r, c = _iota2((n, n))
    same = (r // SUB) == (c // SUB)
    eye = jnp.where(r == c, 1.0, 0.0).astype(F32)
    ds = [jnp.where(same, m, 0.0) for m in ms]
    offs = [m - d for m, d in zip(ms, ds)]
    xs = [eye - d for d in ds]
    ps = ds
    for _ in range(3):
        ps = [_mm(p, p) for p in ps]
        xs = [x + _mm(x, p) for x, p in zip(xs, ps)]
    es = [_mm(x, o) for x, o in zip(xs, offs)]
    imes = [eye - e for e in es]
    e2s = [_mm(e, e) for e in es]
    ys = [i + _mm(i, e2) for i, e2 in zip(imes, e2s)]
    xs = [_mm(y, x) for y, x in zip(ys, xs)]
    res = [eye - x - _mm3(m, x) for m, x in zip(ms, xs)]
    return [x + _mm(x, rr) for x, rr in zip(xs, res)]


def _pack_weights_kernel(w_ref, *o_refs, ranges, gate_ranges):
    w = w_ref[...]
    for (lo, hi), o_ref in zip(ranges, o_refs):
        o_ref[...] = w[:, lo:hi].astype(BF16)
    if gate_ranges:
        cols = [w[:, lo:hi] for lo, hi in gate_ranges]
        used = sum(hi - lo for lo, hi in gate_ranges)
        cols.append(jnp.zeros((w.shape[0], LANES - used), F32))
        o_refs[-1][...] = jnp.concatenate(cols, axis=1).astype(BF16)


def _pack_weights(w, ranges, gate_ranges=()):
    rows, cols = w.shape
    tm = PACK_TM
    widths = [hi - lo for lo, hi in ranges] + ([LANES] if gate_ranges else [])
    return pl.pallas_call(
        functools.partial(_pack_weights_kernel, ranges=tuple(ranges), gate_ranges=tuple(gate_ranges)),
        grid=(rows // tm,),
        in_specs=[pl.BlockSpec((tm, cols), lambda i: (i, 0))],
        out_specs=[pl.BlockSpec((tm, wd), lambda i: (i, 0)) for wd in widths],
        out_shape=[jax.ShapeDtypeStruct((rows, wd), BF16) for wd in widths],
        compiler_params=_cparams("parallel"),
        name="pack_weights",
    )(w)


def _rms_proj_kernel(x_ref, g_ref, *refs, n_out):
    y = _rms_rows(x_ref[...], g_ref[...]).astype(BF16)
    for w_ref, o_ref in zip(refs[:n_out], refs[n_out:]):
        o_ref[...] = jnp.dot(y, w_ref[...], preferred_element_type=F32)


def _rms_proj(x, g, ws):
    n, d = x.shape
    tm = min(PROJ_TM, n)
    in_specs = [pl.BlockSpec((tm, d), lambda i: (i, 0)), pl.BlockSpec((1, d), lambda i: (0, 0))]
    in_specs += [pl.BlockSpec(w.shape, lambda i: (0, 0)) for w in ws]
    return pl.pallas_call(
        functools.partial(_rms_proj_kernel, n_out=len(ws)),
        grid=(n // tm,),
        in_specs=in_specs,
        out_specs=[pl.BlockSpec((tm, w.shape[1]), lambda i: (i, 0)) for w in ws],
        out_shape=[jax.ShapeDtypeStruct((n, w.shape[1]), F32) for w in ws],
        compiler_params=_cparams("parallel"),
        name="rms_proj",
    )(x, g.reshape(1, d), *ws)


ROW_TILE = D_MODEL // LANES


def _store_row_tiles(ref, x):
    for j in range(ROW_TILE):
        ref[pl.ds(j, x.shape[0], stride=ROW_TILE), :] = x[:, j * LANES:(j + 1) * LANES]


def _load_row_tiles(ref, rows):
    return jnp.concatenate([ref[pl.ds(j, rows, stride=ROW_TILE), :] for j in range(ROW_TILE)], axis=1)


def _rwkv7_kernel(p_ref, mu_ref, w0_ref, w2_ref, a0_ref, a2_ref, g2_ref, kk_ref, ka_ref, rk_ref,
                  lnw_ref, lnb_ref, o_ref, prev_ref, zt_ref):
    L = CHUNK
    npair = RW_HEADS // 2
    c = pl.program_id(1)

    @pl.when(c == 0)
    def _():
        prev_ref[...] = jnp.zeros_like(prev_ref)
        zt_ref[...] = jnp.zeros_like(zt_ref)

    x = p_ref[0]
    R = x.shape[0]
    nch = R // L
    row = lax.broadcasted_iota(I32, x.shape, 0)
    xs = jnp.where(row == 0, prev_ref[7:8, :], pltpu.roll(x, 1, 0))
    prev_ref[...] = x[R - 8:R, :]
    pm = x + mu_ref[...] * (xs - x)
    r_all = pm[:, 0:RW_DIM]
    k_all = pm[:, RW_DIM:2 * RW_DIM]
    v_all = pm[:, 2 * RW_DIM:3 * RW_DIM]
    wa = pm[:, 3 * RW_DIM:3 * RW_DIM + LANES]
    gl = pm[:, 3 * RW_DIM + LANES:]
    wlog = -jax.nn.softplus(-(w0_ref[...] + _mm(jnp.tanh(wa), w2_ref[...]))) - 0.5
    ld = -jnp.exp(wlog)
    a_all = jax.nn.sigmoid(a0_ref[...] + _mm(wa, a2_ref[...]))
    g_all = _mm(jax.nn.sigmoid(gl), g2_ref[...])

    tr, tc = _iota2((R, R))
    chunk_tril = jnp.where((tc <= tr) & ((tr // L) == (tc // L)), 1.0, 0.0).astype(BF16)
    cs_all = _cumsum_rows(chunk_tril, ld)
    seg = _seg_ones(LANES, RW_HEAD_DIM)
    lane = lax.broadcasted_iota(I32, (L, LANES), 1)
    hm = (lane < RW_HEAD_DIM, lane >= RW_HEAD_DIM)
    br, bc = _iota2((2 * L, 2 * L))
    bd = (br // L) == (bc // L)
    bd_strict = bd & (bc < br)
    bd_incl = bd & (bc <= br)
    fold = lambda z: z[0:L] + z[L:2 * L]
    both = lambda z: jnp.concatenate([jnp.where(hm[0], z, 0.0), jnp.where(hm[1], z, 0.0)], axis=0)

    units = []
    for ci in range(nch):
        rs = slice(ci * L, (ci + 1) * L)
        for p in range(npair):
            sl = slice(p * LANES, (p + 1) * LANES)
            r, k, v, a = r_all[rs, sl], k_all[rs, sl], v_all[rs, sl], a_all[rs, sl]
            cs, ldp = cs_all[rs, sl], ld[rs, sl]
            kkr = k * kk_ref[:, sl]
            kk = kkr * lax.rsqrt(_mm_split(kkr * kkr, seg) + 1e-6)
            k2 = k * (1.0 + (a - 1.0) * ka_ref[:, sl])
            b = kk * a
            cs_last = cs[L - 1:L, :]
            e_neg = jnp.exp(-cs)
            e_rem = jnp.exp(cs_last - cs)
            bhat = b * e_neg
            khat = k2 * e_neg
            units.append(dict(p=p, rs=rs, sl=sl, r=r, v=v, k2=k2, rhat=r * jnp.exp(cs), gam_last=jnp.exp(cs_last),
                              btil=b * e_rem, ktil=k2 * e_rem, a2=both(kk * jnp.exp(cs - ldp)), v2=both(v),
                              rhs4=jnp.concatenate([bhat, bhat, khat, khat], axis=0)))
    for q in units:
        lhs = jnp.concatenate([q["a2"], both(q["rhat"])], axis=0)
        q["g"] = _mm_nt(lhs, q["rhs4"])
    tinvs = _tri_inv_multi([jnp.where(bd_strict, q["g"][0:2 * L, 0:2 * L], 0.0) for q in units], 2 * L, L)
    for q, tinv in zip(units, tinvs):
        q["tinv"] = tinv
        q["x2"] = _mm(jnp.where(bd_strict, q["g"][0:2 * L, 2 * L:4 * L], 0.0), q["v2"])
    for q in units:
        uw = _mm(q["tinv"], jnp.concatenate([q["x2"], q["a2"]], axis=1))
        q["u0"] = -fold(uw[:, 0:LANES])
        q["w"] = fold(uw[:, LANES:])
        q["y0"] = fold(_mm(jnp.where(bd_incl, q["g"][2 * L:4 * L, 2 * L:4 * L], 0.0), q["v2"]))
    for q in units:
        rb = jnp.where(bd_incl, q["g"][2 * L:4 * L, 0:2 * L], 0.0)
        ruw = _mm(rb, jnp.concatenate([both(q["u0"]), both(q["w"])], axis=1))
        q["yc"] = q["y0"] + fold(ruw[:, 0:LANES])
        q["ry"] = q["rhat"] - fold(ruw[:, LANES:])
        q["c1"] = _mm_tn(jnp.concatenate([q["u0"], q["v"]], axis=0), jnp.concatenate([q["btil"], q["ktil"]], axis=0))
        q["c2"] = _mm_tn(q["w"], q["btil"])

    hr, hc = _iota2((LANES, LANES))
    head_bd = (hr // RW_HEAD_DIM) == (hc // RW_HEAD_DIM)
    zts = [zt_ref[p] for p in range(npair)]
    for ci in range(nch):
        qs = units[ci * npair:(ci + 1) * npair]
        for q, zt in zip(qs, zts):
            q["y"] = q["yc"] + _mm_nt(q["ry"], zt)
        zts = [zt * q["gam_last"] + jnp.where(head_bd, q["c1"] - _mm(zt, q["c2"]), 0.0) for q, zt in zip(qs, zts)]
    for p in range(npair):
        zt_ref[p] = zts[p]

    for q in units:
        sl, rs, y = q["sl"], q["rs"], q["y"]
        mean = _mm_split(y, seg) * (1.0 / RW_HEAD_DIM)
        yc = y - mean
        var = _mm_split(yc * yc, seg) * (1.0 / RW_HEAD_DIM)
        yn = yc * lax.rsqrt(var + RW_LN_EPS) * lnw_ref[:, sl] + lnb_ref[:, sl]
        bonus = _mm_split(q["r"] * q["k2"] * rk_ref[:, sl], seg) * q["v"]
        o_ref[0, rs, sl] = (yn + bonus) * g_all[rs, sl]


def _rwkv7(p, mu, w0, w2, a0, a2, g2, k_k, k_a, r_k, ln_w, ln_b):
    bsz, t, _ = p.shape
    row = lambda z: z.reshape(1, -1).astype(F32)
    w2p = jnp.concatenate([w2, jnp.zeros_like(w2)], axis=0).astype(BF16)
    a2p = jnp.concatenate([jnp.zeros_like(a2), a2], axis=0).astype(BF16)
    params = [row(mu), row(w0), w2p, row(a0), a2p, g2.astype(BF16), row(k_k), row(k_a), row(r_k), row(ln_w), row(ln_b)]
    full = lambda z: pl.BlockSpec(z.shape, lambda b, c: (0, 0))
    rows = RW_NCH * CHUNK
    return pl.pallas_call(
        _rwkv7_kernel,
        grid=(bsz, t // rows),
        in_specs=[pl.BlockSpec((1, rows, RW_IN), lambda b, c: (b, c, 0))] + [full(z) for z in params],
        out_specs=pl.BlockSpec((1, rows, RW_DIM), lambda b, c: (b, c, 0)),
        out_shape=jax.ShapeDtypeStruct((bsz, t, RW_DIM), F32),
        scratch_shapes=[pltpu.VMEM((8, RW_IN), F32), pltpu.VMEM((RW_HEADS // 2, LANES, LANES), F32)],
        compiler_params=_cparams("arbitrary", "arbitrary"),
        name="rwkv7_mix",
    )(p, *params)


def _hgrn2_kernel(p_ref, lb_ref, ng_ref, o_ref, st_ref):
    L = HG_ROWS
    c = pl.program_id(1)

    @pl.when(c == 0)
    def _():
        st_ref[...] = jnp.zeros_like(st_ref)

    x = p_ref[0]
    lb = lb_ref[...]
    q_all = jax.nn.silu(x[:, 0:HG_KDIM])
    fg = lb + (1.0 - lb) * jax.nn.sigmoid(x[:, HG_KDIM:2 * HG_KDIM])
    k_all = 1.0 - fg
    logf = jnp.log(fg)
    v_all = x[:, 2 * HG_KDIM:3 * HG_KDIM]
    gate = x[:, 3 * HG_KDIM:]
    tr, tc = _iota2((L, L))
    blk_tril = jnp.where((tc <= tr) & ((tr // SUB) == (tc // SUB)), 1.0, 0.0).astype(BF16)
    bc_all = _cumsum_rows(blk_tril, logf)
    seg = _seg_ones(LANES, HG_DIM)
    br, bcc = _iota2((LANES, LANES))
    bd = (br // HG_DIM) == (bcc // HG_DIM)
    half = SUB // 2
    t_lo = lax.broadcasted_iota(I32, (SUB, LANES), 0)
    t_hi = lax.broadcasted_iota(I32, (half, LANES), 0) + half

    nsub = L // SUB
    npair = HG_HEADS // 2
    rows_per_unit = half * SUB + half * half
    sub_of_row = lax.broadcasted_iota(I32, (L, LANES), 0) // SUB

    units = []
    parts = []
    for p in range(npair):
        sl = slice(p * LANES, (p + 1) * LANES)
        for j in range(nsub):
            rs = slice(j * SUB, (j + 1) * SUB)
            q, k, v, bc = q_all[rs, sl], k_all[rs, sl], v_all[rs, sl], bc_all[rs, sl]
            q_hi, bc_hi = q[half:], bc[half:]
            for s in range(SUB):
                if s < half:
                    diff = jnp.where(t_lo >= s, bc - bc[s:s + 1, :], NEG)
                    parts.append(jnp.exp(diff) * q * k[s:s + 1, :])
                else:
                    diff = jnp.where(t_hi >= s, bc_hi - bc[s:s + 1, :], NEG)
                    parts.append(jnp.exp(diff) * q_hi * k[s:s + 1, :])
            bend = bc[SUB - 1:SUB, :]
            units.append(dict(v=v, qt=q * jnp.exp(bc), dec=jnp.exp(bend), kd=k * jnp.exp(bend - bc)))
    score_all = _mm(jnp.concatenate(parts, axis=0), seg)
    for ui, un in enumerate(units):
        score = score_all[ui * rows_per_unit:(ui + 1) * rows_per_unit]
        v = un["v"]
        acc_lo = jnp.zeros((half, LANES), F32)
        acc_hi = jnp.zeros((half, LANES), F32)
        off = 0
        for s in range(SUB):
            vs = v[s:s + 1, :]
            if s < half:
                acc_lo = acc_lo + score[off:off + half] * vs
                acc_hi = acc_hi + score[off + half:off + SUB] * vs
                off += SUB
            else:
                acc_hi = acc_hi + score[off:off + half] * vs
                off += half
        un["intra"] = jnp.concatenate([acc_lo, acc_hi], axis=0)
    for p in range(npair):
        us = units[p * nsub:(p + 1) * nsub]
        v_pair = jnp.concatenate([un["v"] for un in us], axis=0)
        kd_pair = jnp.concatenate([un["kd"] for un in us], axis=0)
        kd_wide = jnp.concatenate([jnp.where(sub_of_row == j, kd_pair, 0.0) for j in range(nsub)], axis=1)
        upd = _mm_tn(v_pair, kd_wide)
        for j, un in enumerate(us):
            un["upd"] = jnp.where(bd, upd[:, j * LANES:(j + 1) * LANES], 0.0)

    for p in range(npair):
        sl = slice(p * LANES, (p + 1) * LANES)
        st = st_ref[p]
        outs = []
        for un in units[p * nsub:(p + 1) * nsub]:
            outs.append(un["intra"] + _mm_nt(un["qt"], st))
            st = st * un["dec"] + un["upd"]
        st_ref[p] = st
        o = jnp.concatenate(outs, axis=0)
        ms = _mm_split(o * o, seg) * (1.0 / HG_DIM)
        o_ref[0, :, sl] = o * lax.rsqrt(ms + NORM_EPS) * ng_ref[:, sl] * jax.nn.silu(gate[:, sl])


def _hgrn2(p, lb, norm_g):
    bsz, t, _ = p.shape
    lb = lb.reshape(1, HG_KDIM).astype(F32)
    ng = jnp.tile(norm_g.astype(F32), HG_HEADS).reshape(1, HG_KDIM)
    return pl.pallas_call(
        _hgrn2_kernel,
        grid=(bsz, t // HG_ROWS),
        in_specs=[pl.BlockSpec((1, HG_ROWS, HG_IN), lambda b, c: (b, c, 0)),
                  pl.BlockSpec((1, HG_KDIM), lambda b, c: (0, 0)), pl.BlockSpec((1, HG_KDIM), lambda b, c: (0, 0))],
        out_specs=pl.BlockSpec((1, HG_ROWS, HG_KDIM), lambda b, c: (b, c, 0)),
        out_shape=jax.ShapeDtypeStruct((bsz, t, HG_KDIM), F32),
        scratch_shapes=[pltpu.VMEM((HG_HEADS // 2, LANES, LANES), F32)],
        compiler_params=_cparams("arbitrary", "arbitrary"),
        name="hgrn2_mix",
    )(p, lb, ng)


def _gdn_kernel(p_ref, pg_ref, cw_ref, gb_ref, nal_ref, nalc_ref, ng_ref, o_ref, prev_ref, s_ref):
    L = CHUNK
    c = pl.program_id(1)

    @pl.when(c == 0)
    def _():
        prev_ref[...] = jnp.zeros_like(prev_ref)
        s_ref[...] = jnp.zeros_like(s_ref)

    x = p_ref[0]
    R = x.shape[0]
    nch = R // L
    xq = x[:, 0:3 * GD_DIM]
    xcat = jnp.concatenate([prev_ref[...], xq], axis=0)
    conv = xq * cw_ref[CONV_K - 1:CONV_K, :]
    for j in range(1, CONV_K):
        conv = conv + xcat[8 - j:8 - j + R, :] * cw_ref[CONV_K - 1 - j:CONV_K - j, :]
    prev_ref[...] = xq[R - 8:R, :]
    qkv = jax.nn.silu(conv)
    z = x[:, 3 * GD_DIM:]

    gt = pg_ref[0] + gb_ref[...]
    beta_all = jax.nn.sigmoid(gt)
    g_all = nal_ref[...] * jax.nn.softplus(gt)
    g_t = nalc_ref[...] * jax.nn.softplus(gt.T)
    tr, tc = _iota2((R, R))
    chunk_tril = jnp.where((tc <= tr) & ((tr // L) == (tc // L)), 1.0, 0.0).astype(BF16)
    gam_all = _cumsum_rows(chunk_tril, g_all)
    ur, uc = _iota2((R, 2 * R))
    triu2 = jnp.where(((ur // L) == (uc // (2 * L))) & ((ur % L) <= (uc % L)), 1.0, 0.0).astype(BF16)
    gam_t2 = _cumsum_cols(g_t, triu2)
    lr, lc = _iota2((L, L))
    incl = lc <= lr
    br, bc = _iota2((2 * L, 2 * L))
    bd = (br // L) == (bc // L)
    bd_strict = bd & (bc < br)
    bd_incl = bd & (bc <= br)
    lane2 = lax.broadcasted_iota(I32, (1, 2 * L), 1)
    zero = jnp.zeros((L, LANES), F32)

    units = []
    for ci in range(nch):
        rs = slice(ci * L, (ci + 1) * L)
        for h in range(GD_HEADS):
            q = qkv[rs, h * LANES:(h + 1) * LANES]
            k = qkv[rs, GD_DIM + h * LANES:GD_DIM + (h + 1) * LANES]
            v = qkv[rs, 2 * GD_DIM + h * LANES:2 * GD_DIM + (h + 1) * LANES]
            q = q * lax.rsqrt(jnp.sum(q * q, axis=-1, keepdims=True) + 1e-6) * (GD_HEAD_DIM ** -0.5)
            k = k * lax.rsqrt(jnp.sum(k * k, axis=-1, keepdims=True) + 1e-6)
            beta = beta_all[rs, h:h + 1]
            gam = gam_all[rs, GD_HEADS + h:GD_HEADS + h + 1]
            gam_row2 = gam_t2[GD_HEADS + h:GD_HEADS + h + 1, ci * 2 * L:(ci + 1) * 2 * L]
            gam_last = gam[L - 1:L, :]
            kb = k * beta
            units.append(dict(h=h, rs=rs, q=q, k=k, kb=kb, vb=v * beta, gam=gam, gam_row2=gam_row2,
                              gam_last=gam_last, kg=kb * jnp.exp(gam), qg=q * jnp.exp(gam),
                              kd=k * jnp.exp(gam_last - gam),
                              decay=jnp.exp(jnp.where(incl, gam - gam_row2[:, 0:L], NEG))))
    pairs = [(units[i], units[i + 1]) for i in range(0, len(units), 2)]
    ms = []
    for h0, h1 in pairs:
        lhs = jnp.concatenate([jnp.concatenate([h0["kb"], zero], axis=1),
                               jnp.concatenate([zero, h1["kb"]], axis=1)], axis=0)
        rhs = jnp.concatenate([jnp.concatenate([h0["k"], zero], axis=1),
                               jnp.concatenate([zero, h1["k"]], axis=1)], axis=0)
        gam_col = jnp.concatenate([h0["gam"], h1["gam"]], axis=0)
        gam_row = jnp.where(lane2 < L, h0["gam_row2"], h1["gam_row2"])
        decay2 = jnp.exp(jnp.where(bd_incl, gam_col - gam_row, NEG))
        ms.append(jnp.where(bd_strict, _mm_nt(lhs, rhs) * decay2, 0.0))
    tinvs = _tri_inv_multi(ms, 2 * L, L)
    for (h0, h1), tinv in zip(pairs, tinvs):
        rhs = jnp.concatenate([jnp.concatenate([h0["vb"], h0["kg"]], axis=1),
                               jnp.concatenate([h1["vb"], h1["kg"]], axis=1)], axis=0)
        uw = _mm(tinv, rhs)
        h0["u"], h0["w"] = uw[0:L, 0:LANES], uw[0:L, LANES:]
        h1["u"], h1["w"] = uw[L:2 * L, 0:LANES], uw[L:2 * L, LANES:]
    for hd in units:
        attn = _mm_nt(hd["q"], hd["k"]) * hd["decay"]
        uw = jnp.concatenate([hd["u"], hd["w"]], axis=1)
        auw = _mm(attn, uw)
        hd["o0"] = auw[:, 0:LANES]
        hd["qs"] = hd["qg"] - auw[:, LANES:]
        cc = _mm_tn(hd["kd"], uw)
        hd["c1"], hd["c2"] = cc[:, 0:LANES], cc[:, LANES:]

    ss = [s_ref[h] for h in range(GD_HEADS)]
    for ci in range(nch):
        hds = units[ci * GD_HEADS:(ci + 1) * GD_HEADS]
        for hd, s in zip(hds, ss):
            hd["o"] = hd["o0"] + _mm(hd["qs"], s)
        ss = [s * jnp.exp(hd["gam_last"]) + hd["c1"] - _mm(hd["c2"], s) for hd, s in zip(hds, ss)]
    for h in range(GD_HEADS):
        s_ref[h] = ss[h]
    for hd in units:
        sl = slice(hd["h"] * LANES, (hd["h"] + 1) * LANES)
        o_ref[0, hd["rs"], sl] = _rms_rows(hd["o"], ng_ref[...]) * jax.nn.silu(z[hd["rs"], sl])


def _gate_row(lo, vals):
    return jnp.zeros((1, LANES), F32).at[0, lo:lo + vals.shape[0]].set(vals.astype(F32))


def _gdn(p, pg, conv_w, a_log, dt_bias, norm_g):
    bsz, t, _ = p.shape
    gbias = _gate_row(GD_HEADS, dt_bias)
    nal = _gate_row(GD_HEADS, -jnp.exp(a_log))
    nal_col = nal.reshape(LANES, 1)
    ng = norm_g.reshape(1, GD_HEAD_DIM).astype(F32)
    full = lambda z: pl.BlockSpec(z.shape, lambda b, c: (0, 0))
    rows = GD_NCH * CHUNK
    return pl.pallas_call(
        _gdn_kernel,
        grid=(bsz, t // rows),
        in_specs=[pl.BlockSpec((1, rows, GD_MAIN), lambda b, c: (b, c, 0)),
                  pl.BlockSpec((1, rows, LANES), lambda b, c: (b, c, 0)),
                  full(conv_w), full(gbias), full(nal), full(nal_col), full(ng)],
        out_specs=pl.BlockSpec((1, rows, GD_DIM), lambda b, c: (b, c, 0)),
        out_shape=jax.ShapeDtypeStruct((bsz, t, GD_DIM), F32),
        scratch_shapes=[pltpu.VMEM((8, 3 * GD_DIM), F32), pltpu.VMEM((GD_HEADS, LANES, LANES), F32)],
        compiler_params=_cparams("arbitrary", "arbitrary"),
        name="gdn_mix",
    )(p, pg, conv_w.astype(F32), gbias, nal, nal_col, ng)


def _mlstm_kernel(p_ref, pg_ref, gb_ref, ng_ref, o_ref, c_ref, n_ref, m_ref):
    L = ML_CHUNK
    ci = pl.program_id(1)

    @pl.when(ci == 0)
    def _():
        c_ref[...] = jnp.zeros_like(c_ref)
        n_ref[...] = jnp.zeros_like(n_ref)
        m_ref[...] = jnp.zeros_like(m_ref)

    x = p_ref[0]
    R = x.shape[0]
    nch = R // L
    gt = pg_ref[0] + gb_ref[...]
    logf = jax.nn.log_sigmoid(gt)
    gt_t = gt.T
    logf_t = jax.nn.log_sigmoid(gt_t)
    tr, tc = _iota2((R, R))
    same = (tr // L) == (tc // L)
    bc_all = _cumsum_rows(jnp.where(same & (tc <= tr), 1.0, 0.0).astype(BF16), logf)
    bc_t = _cumsum_cols(logf_t, jnp.where(same & (tr <= tc), 1.0, 0.0).astype(BF16))
    lr, lc = _iota2((L, L))
    incl = lc <= lr
    i_lo, f_lo = 2 * GD_HEADS, 2 * GD_HEADS + ML_HEADS

    units = []
    m_run = [m_ref[h][0:1, 0:1] for h in range(ML_HEADS)]
    for ci in range(nch):
        rs = slice(ci * L, (ci + 1) * L)
        for h in range(ML_HEADS):
            sl = slice(h * LANES, (h + 1) * LANES)
            q = x[rs, sl]
            k = x[rs, ML_DIM + h * LANES:ML_DIM + (h + 1) * LANES] * (ML_HEAD_DIM ** -0.5)
            v = x[rs, 2 * ML_DIM + h * LANES:2 * ML_DIM + (h + 1) * LANES]
            bc = bc_all[rs, f_lo + h:f_lo + h + 1]
            bc_row = bc_t[f_lo + h:f_lo + h + 1, rs]
            ig = gt[rs, i_lo + h:i_lo + h + 1]
            ig_row = gt_t[i_lo + h:i_lo + h + 1, rs]
            m_prev = m_run[h]
            b_last = bc[L - 1:L, :]
            d_log = jnp.where(incl, bc - bc_row + ig_row, NEG)
            inter_log = bc + m_prev
            m_t = jnp.maximum(inter_log, jnp.max(d_log, axis=-1, keepdims=True))
            upd_log = b_last - bc + ig
            m_new = jnp.maximum(b_last + m_prev, jnp.max(upd_log, axis=0, keepdims=True))
            m_run[h] = m_new
            wk = jnp.exp(upd_log - m_new) * k
            units.append(dict(h=h, rs=rs, sl=sl, q=q, v=v, m_t=m_t, wk=wk, inter_w=jnp.exp(inter_log - m_t),
                              dec=jnp.exp(b_last + m_prev - m_new), sc=_mm_nt(q, k) * jnp.exp(d_log - m_t)))
    for hd in units:
        hd["scv"] = _mm(hd["sc"], hd["v"])
        hd["kv"] = _mm_tn(hd["wk"], hd["v"])
    cs = [c_ref[h] for h in range(ML_HEADS)]
    ns = [n_ref[h][0:1, :] for h in range(ML_HEADS)]
    for hd in units:
        h = hd["h"]
        q, sc, inter_w, dec = hd["q"], hd["sc"], hd["inter_w"], hd["dec"]
        num = inter_w * _mm(q, cs[h]) + hd["scv"]
        den = inter_w * jnp.sum(q * ns[h], axis=-1, keepdims=True) + jnp.sum(sc, axis=-1, keepdims=True)
        hd["hh"] = num / jnp.maximum(jnp.abs(den), jnp.exp(-hd["m_t"]))
        cs[h] = dec * cs[h] + hd["kv"]
        ns[h] = dec * ns[h] + jnp.sum(hd["wk"], axis=0, keepdims=True)
    for h in range(ML_HEADS):
        c_ref[h] = cs[h]
        n_ref[h] = jnp.broadcast_to(ns[h], (8, LANES))
        m_ref[h] = jnp.broadcast_to(m_run[h], (8, LANES))
    for hd in units:
        h, rs, sl = hd["h"], hd["rs"], hd["sl"]
        og = jax.nn.sigmoid(x[rs, 3 * ML_DIM + h * LANES:3 * ML_DIM + (h + 1) * LANES])
        o_ref[0, rs, sl] = og * _rms_rows(hd["hh"], ng_ref[:, sl])


def _mlstm(p, pg, i_bias, f_bias, norm_g):
    bsz, t, _ = p.shape
    gbias = _gate_row(2 * GD_HEADS, jnp.concatenate([i_bias, f_bias]))
    ng = norm_g.reshape(1, ML_DIM).astype(F32)
    lc = ML_NCH * ML_CHUNK
    full = lambda z: pl.BlockSpec(z.shape, lambda b, c: (0, 0))
    return pl.pallas_call(
        _mlstm_kernel,
        grid=(bsz, t // lc),
        in_specs=[pl.BlockSpec((1, lc, ML_MAIN), lambda b, c: (b, c, 0)),
                  pl.BlockSpec((1, lc, LANES), lambda b, c: (b, c, 0)), full(gbias), full(ng)],
        out_specs=pl.BlockSpec((1, lc, ML_DIM), lambda b, c: (b, c, 0)),
        out_shape=jax.ShapeDtypeStruct((bsz, t, ML_DIM), F32),
        scratch_shapes=[pltpu.VMEM((ML_HEADS, LANES, LANES), F32), pltpu.VMEM((ML_HEADS, 8, LANES), F32),
                        pltpu.VMEM((ML_HEADS, 8, LANES), F32)],
        compiler_params=_cparams("arbitrary", "arbitrary"),
        name="mlstm_mix",
    )(p, pg, gbias, ng)


def _route(h, g_ref, wh_ref, wl_ref, b_ref, off_ref):
    tm = h.shape[0]
    xn = _rms_rows(h, g_ref[...])
    xh = xn.astype(BF16)
    xl = (xn - xh.astype(F32)).astype(BF16)
    dot = lambda a, b: jnp.dot(a, b, preferred_element_type=F32)
    logits = dot(xh, wh_ref[...]) + dot(xh, wl_ref[...]) + dot(xl, wh_ref[...]) + b_ref[...]
    lt = logits.T[0:ROUTE_ROWS, :]
    row = lax.broadcasted_iota(I32, lt.shape, 0)
    big = jnp.int32(1 << 20)
    is_grp = (row >= N_EXPERTS) & (row < N_EXPERTS + N_GROUPS)
    lg = jnp.where(is_grp, lt, NEG)
    gmax = jnp.max(lg, axis=0, keepdims=True)
    p_top = 1.0 / jnp.sum(jnp.exp(lg - gmax), axis=0, keepdims=True)
    g_idx = jnp.min(jnp.where(lg == gmax, row, big), axis=0, keepdims=True) - N_EXPERTS
    valid = (row < N_EXPERTS) & ((row // EXPERTS_PER_GROUP) == g_idx)
    v1 = jnp.where(valid, lt, NEG)
    m1 = jnp.max(v1, axis=0, keepdims=True)
    i1 = jnp.min(jnp.where(v1 == m1, row, big), axis=0, keepdims=True)
    v2 = jnp.where(row == i1, NEG, v1)
    m2 = jnp.max(v2, axis=0, keepdims=True)
    i2 = jnp.min(jnp.where(v2 == m2, row, big), axis=0, keepdims=True)
    e21 = jnp.exp(m2 - m1)
    gate1 = p_top / (1.0 + e21)
    gate2 = p_top * e21 / (1.0 + e21)

    sel1 = row == i1
    sel2 = row == i2
    onehot = jnp.where(sel1 | sel2, 1.0, 0.0)
    tr, tc = _iota2((tm, tm))
    earlier = jnp.where(tr < tc, 1.0, 0.0).astype(BF16)
    off = off_ref[0:ROUTE_ROWS, 0:1]
    prefix = jnp.dot(onehot.astype(BF16), earlier, preferred_element_type=F32) + off
    rank1 = jnp.sum(jnp.where(sel1, prefix, 0.0), axis=0, keepdims=True)
    rank2 = jnp.sum(jnp.where(sel2, prefix, 0.0), axis=0, keepdims=True)
    off_ref[0:ROUTE_ROWS, :] = jnp.broadcast_to(off + jnp.sum(onehot, axis=1, keepdims=True), (ROUTE_ROWS, LANES))

    packed = jnp.concatenate([i1.astype(F32), i2.astype(F32), rank1, rank2, gate1, gate2,
                              jnp.zeros((2, tm), F32)], axis=0)
    return xn, packed


def _out_proj_route_kernel(h_ref, ya_ref, yb_ref, wa_ref, wb_ref, g_ref, wh_ref, wl_ref, b_ref,
                           o_ref, xt_ref, rg_ref, cnt_ref, dest_ref, off_ref, rt_ref):
    phase = pl.program_id(0)
    s = pl.program_id(1)
    tm = h_ref.shape[0]

    @pl.when((phase == 0) & (s == 0))
    def _():
        off_ref[...] = jnp.zeros_like(off_ref)

    @pl.when(phase == 0)
    def _():
        out = h_ref[...] + _mm(ya_ref[...], wa_ref[...]) + _mm(yb_ref[...], wb_ref[...])
        o_ref[...] = out
        xn, packed = _route(out, g_ref, wh_ref, wl_ref, b_ref, off_ref)
        _store_row_tiles(xt_ref, xn)
        rt_ref[s] = packed
        wide = jnp.concatenate([packed, jnp.zeros((LANES - 8, tm), F32)], axis=0)
        rg_ref[...] = wide.T
        cnt_ref[...] = off_ref[...]

    @pl.when(phase == 1)
    def _():
        counts = off_ref[...]
        tiles_per = jnp.floor((counts + (MOE_TM - 1)) * (1.0 / MOE_TM))
        er, ec = _iota2((LANES, LANES))
        tile_end = _mm(jnp.where(ec <= er, 1.0, 0.0), tiles_per)
        seg_start = ((tile_end - tiles_per) * MOE_TM)[0:ROUTE_ROWS, 0:1]
        blk = rt_ref[s]
        row = lax.broadcasted_iota(I32, (ROUTE_ROWS, tm), 0).astype(F32)
        pick = lambda e: jnp.sum(jnp.where(row == e, seg_start, 0.0), axis=0, keepdims=True)
        d1 = pick(blk[0:1]) + blk[2:3]
        d2 = pick(blk[1:2]) + blk[3:4]
        dest_ref[...] = jnp.concatenate([d1, d2, jnp.zeros((6, tm), F32)], axis=0).astype(I32)


def _out_proj_route(h, ya, yb, w_out, g_ffn, w_group, b_group, w_router, b_router):
    n, d = h.shape
    da, db = ya.shape[1], yb.shape[1]
    tm = min(PROJ_TM, n)
    wa = w_out[:da].astype(BF16)
    wb = w_out[da:].astype(BF16)
    pad = LANES - N_EXPERTS - N_GROUPS
    w_cat = jnp.concatenate([w_router, w_group, jnp.zeros((d, pad), F32)], axis=1)
    b_cat = jnp.concatenate([b_router, b_group, jnp.zeros((pad,), F32)]).reshape(1, LANES)
    w_hi = w_cat.astype(BF16)
    w_lo = (w_cat - w_hi.astype(F32)).astype(BF16)
    steps = n // tm
    row = lambda p, s: (s * (1 - p) + (steps - 1) * p, 0)
    fixed = lambda p, s: (0, 0)
    return pl.pallas_call(
        _out_proj_route_kernel,
        grid=(2, steps),
        in_specs=[pl.BlockSpec((tm, d), row), pl.BlockSpec((tm, da), row), pl.BlockSpec((tm, db), row),
                  pl.BlockSpec((da, d), fixed), pl.BlockSpec((db, d), fixed), pl.BlockSpec((1, d), fixed),
                  pl.BlockSpec((d, LANES), fixed), pl.BlockSpec((d, LANES), fixed), pl.BlockSpec((1, LANES), fixed)],
        out_specs=[pl.BlockSpec((tm, d), row), pl.BlockSpec((tm * ROW_TILE, LANES), row),
                   pl.BlockSpec((tm, LANES), row), pl.BlockSpec((LANES, LANES), fixed),
                   pl.BlockSpec((8, tm), lambda p, s: (0, s * p))],
        out_shape=[jax.ShapeDtypeStruct((n, d), F32), jax.ShapeDtypeStruct((n * ROW_TILE, LANES), F32),
                   jax.ShapeDtypeStruct((n, LANES), F32), jax.ShapeDtypeStruct((LANES, LANES), F32),
                   jax.ShapeDtypeStruct((8, n), I32)],
        scratch_shapes=[pltpu.VMEM((LANES, LANES), F32), pltpu.VMEM((steps, 8, tm), F32)],
        compiler_params=_cparams("arbitrary", "arbitrary"),
        name="out_proj_route",
    )(h, ya, yb, wa, wb, g_ffn.reshape(1, d), w_hi, w_lo, b_cat)


def _gather_rows(src_hbm, idx_ref, base, dst, sem):
    def body(j, carry):
        r = pl.multiple_of(idx_ref[base + j] * ROW_TILE, ROW_TILE)
        pltpu.make_async_copy(src_hbm.at[pl.ds(r, ROW_TILE), :],
                              dst.at[pl.ds(pl.multiple_of(j * ROW_TILE, ROW_TILE), ROW_TILE), :],
                              sem).start()
        return carry

    lax.fori_loop(0, dst.shape[0] // ROW_TILE, body, 0, unroll=GATHER_UNROLL)


def _wait_rows(src_hbm, dst, sem):
    pltpu.make_async_copy(src_hbm.at[pl.ds(0, dst.shape[0]), :], dst, sem).wait()


def _dispatch_kernel(d1_ref, d2_ref, lo_ref, hi_ref, nt_ref, x_ref, xs_hbm, zero_ref, sem):
    rows = x_ref.shape[0]
    base = pl.program_id(0) * (rows // ROW_TILE)
    tile_rows = MOE_TM * ROW_TILE

    def pad_copy(p):
        return pltpu.make_async_copy(zero_ref.at[pl.ds(0, ROW_TILE), :],
                                     xs_hbm.at[pl.ds(pl.multiple_of(p * ROW_TILE, ROW_TILE), ROW_TILE), :], sem.at[1])

    def tile_copy(t):
        return pltpu.make_async_copy(zero_ref, xs_hbm.at[pl.ds(pl.multiple_of(t * tile_rows, tile_rows), tile_rows), :],
                                     sem.at[1])

    @pl.when(pl.program_id(0) == 0)
    def _():
        zero_ref[...] = jnp.zeros_like(zero_ref)

        def loop(lo, hi, make, wait):
            def body(p, carry):
                if wait:
                    make(p).wait()
                else:
                    make(p).start()
                return carry

            lax.fori_loop(lo, hi, body, 0)

        n_tiles = xs_hbm.shape[0] // tile_rows
        for wait in (False, True):
            for e in range(lo_ref.shape[0]):
                loop(lo_ref[e], hi_ref[e], pad_copy, wait)
            loop(nt_ref[0], n_tiles, tile_copy, wait)

    for idx_ref in (d1_ref, d2_ref):
        def body(j, carry, idx_ref=idx_ref):
            r = pl.multiple_of(idx_ref[base + j] * ROW_TILE, ROW_TILE)
            pltpu.make_async_copy(x_ref.at[pl.ds(pl.multiple_of(j * ROW_TILE, ROW_TILE), ROW_TILE), :],
                                  xs_hbm.at[pl.ds(r, ROW_TILE), :], sem.at[0]).start()
            return carry

        lax.fori_loop(0, rows // ROW_TILE, body, 0, unroll=GATHER_UNROLL)
    for _ in range(2):
        pltpu.make_async_copy(x_ref, xs_hbm.at[pl.ds(0, rows), :], sem.at[0]).wait()


def _dispatch(xn_tiles, d1, d2, pad_lo, pad_hi, n_used, n_rows):
    n = d1.shape[0]
    tm = min(PROJ_TM, n)
    grid_spec = pltpu.PrefetchScalarGridSpec(
        num_scalar_prefetch=5,
        grid=(n // tm,),
        in_specs=[pl.BlockSpec((tm * ROW_TILE, LANES), lambda i, *_: (i, 0))],
        out_specs=pl.BlockSpec(memory_space=pl.ANY),
        scratch_shapes=[pltpu.VMEM((MOE_TM * ROW_TILE, LANES), F32), pltpu.SemaphoreType.DMA((2,))],
    )
    return pl.pallas_call(
        _dispatch_kernel,
        grid_spec=grid_spec,
        out_shape=jax.ShapeDtypeStruct((n_rows * ROW_TILE, LANES), F32),
        compiler_params=_cparams("arbitrary"),
        name="moe_dispatch",
    )(d1, d2, pad_lo, pad_hi, n_used, xn_tiles)


def _expert_kernel(te_ref, nt_ref, x_ref, wg_ref, wu_ref, wd_ref, o_ref, wgb, wub, wdb):
    tm = x_ref.shape[0] // ROW_TILE
    i = pl.program_id(0)

    @pl.when((i == 0) | (te_ref[i] != te_ref[jnp.maximum(i - 1, 0)]))
    def _():
        wgb[...] = wg_ref[0].astype(BF16)
        wub[...] = wu_ref[0].astype(BF16)
        wdb[...] = wd_ref[0].astype(BF16)

    @pl.when(i < nt_ref[0])
    def _():
        xn = _load_row_tiles(x_ref, tm).astype(BF16)
        gate = jnp.dot(xn, wgb[...], preferred_element_type=F32)
        up = jnp.dot(xn, wub[...], preferred_element_type=F32)
        hid = (jax.nn.silu(gate) * up).astype(BF16)
        _store_row_tiles(o_ref, jnp.dot(hid, wdb[...], preferred_element_type=F32))

    @pl.when(i >= nt_ref[0])
    def _():
        o_ref[...] = jnp.zeros_like(o_ref)


def _experts(xs, tile_expert, n_used, w_gate, w_up, w_down):
    d = D_MODEL
    n_tiles = tile_expert.shape[0]
    tm = MOE_TM
    de = w_gate.shape[-1]
    wspec = lambda rows, cols: pl.BlockSpec((1, rows, cols), lambda i, te, nt: (te[i], 0, 0))
    grid_spec = pltpu.PrefetchScalarGridSpec(
        num_scalar_prefetch=2,
        grid=(n_tiles,),
        in_specs=[pl.BlockSpec((tm * ROW_TILE, LANES), lambda i, te, nt: (jnp.minimum(i, nt[0] - 1), 0)),
                  wspec(d, de), wspec(d, de), wspec(de, d)],
        out_specs=pl.BlockSpec((tm * ROW_TILE, LANES), lambda i, te, nt: (i, 0)),
        scratch_shapes=[pltpu.VMEM((d, de), BF16), pltpu.VMEM((d, de), BF16), pltpu.VMEM((de, d), BF16)],
    )
    return pl.pallas_call(
        _expert_kernel,
        grid_spec=grid_spec,
        out_shape=jax.ShapeDtypeStruct((n_tiles * tm * ROW_TILE, LANES), F32),
        compiler_params=_cparams("arbitrary"),
        name="moe_experts",
    )(tile_expert, n_used, xs, w_gate, w_up, w_down)


def _combine_kernel(d1_ref, d2_ref, ys_hbm, h_ref, rg_ref, g_ref, *refs, n_proj, final_norm):
    w_refs, o_ref, p_refs = refs[:n_proj], refs[n_proj], refs[n_proj + 1:2 * n_proj + 1]
    y1, y2, sem = refs[2 * n_proj + 1:]
    tm = h_ref.shape[0]
    i = pl.program_id(0)
    slot = i % 2

    def gather(tile, s):
        _gather_rows(ys_hbm, d1_ref, tile * tm, y1.at[s], sem.at[0, s])
        _gather_rows(ys_hbm, d2_ref, tile * tm, y2.at[s], sem.at[1, s])

    @pl.when(i == 0)
    def _():
        gather(0, 0)

    @pl.when(i + 1 < pl.num_programs(0))
    def _():
        gather(i + 1, 1 - slot)

    _wait_rows(ys_hbm, y1.at[slot], sem.at[0, slot])
    _wait_rows(ys_hbm, y2.at[slot], sem.at[1, slot])
    rg = rg_ref[...]
    out = (h_ref[...] + rg[:, 4:5] * _load_row_tiles(y1.at[slot], tm)
           + rg[:, 5:6] * _load_row_tiles(y2.at[slot], tm))
    if final_norm:
        out = _rms_rows(out, g_ref[...])
    o_ref[...] = out
    if n_proj:
        y = _rms_rows(out, g_ref[...]).astype(BF16)
        for w_ref, p_ref in zip(w_refs, p_refs):
            p_ref[...] = jnp.dot(y, w_ref[...], preferred_element_type=F32)


def _combine(h, ys, d1, d2, rg, g, ws=(), final_norm=False):
    n, d = h.shape
    tm = min(PROJ_TM, n)
    row = lambda i, a, b: (i, 0)
    fixed = lambda i, a, b: (0, 0)
    grid_spec = pltpu.PrefetchScalarGridSpec(
        num_scalar_prefetch=2,
        grid=(n // tm,),
        in_specs=[pl.BlockSpec(memory_space=pl.ANY), pl.BlockSpec((tm, d), row), pl.BlockSpec((tm, LANES), row),
                  pl.BlockSpec((1, d), fixed)] + [pl.BlockSpec(w.shape, fixed) for w in ws],
        out_specs=[pl.BlockSpec((tm, d), row)] + [pl.BlockSpec((tm, w.shape[1]), row) for w in ws],
        scratch_shapes=[pltpu.VMEM((2, tm * ROW_TILE, LANES), F32), pltpu.VMEM((2, tm * ROW_TILE, LANES), F32),
                        pltpu.SemaphoreType.DMA((2, 2))],
    )
    return pl.pallas_call(
        functools.partial(_combine_kernel, n_proj=len(ws), final_norm=final_norm),
        grid_spec=grid_spec,
        out_shape=[jax.ShapeDtypeStruct((n, d), F32)] + [jax.ShapeDtypeStruct((n, w.shape[1]), F32) for w in ws],
        compiler_params=_cparams("arbitrary"),
        name="moe_combine",
    )(d1, d2, ys, h, rg, g.reshape(1, d), *ws)


def _hmoe_residual(h, xn_tiles, rg, cnt, dest, layer, w_gate, w_up, w_down, g, ws=(), final_norm=False):
    n, d = h.shape
    stack = lambda w: w.reshape((-1,) + w.shape[2:])

    counts = cnt[:N_EXPERTS, 0].astype(I32)
    n_tiles = (2 * n) // MOE_TM + N_EXPERTS
    tile_end = jnp.cumsum((counts + MOE_TM - 1) // MOE_TM)
    d1, d2 = dest[0], dest[1]
    tile_ids = jnp.arange(n_tiles, dtype=I32)
    tile_expert = jnp.minimum(jnp.sum((tile_end[None, :] <= tile_ids[:, None]).astype(I32), axis=1), N_EXPERTS - 1)
    tile_expert = tile_expert + layer * N_EXPERTS
    seg_end = tile_end * MOE_TM
    seg_start = seg_end - ((counts + MOE_TM - 1) // MOE_TM) * MOE_TM
    pad_lo = (seg_start + counts).astype(I32)
    pad_hi = seg_end.astype(I32)
    n_used = tile_end[N_EXPERTS - 1:].astype(I32)
    xs = _dispatch(xn_tiles, d1, d2, pad_lo, pad_hi, n_used, n_tiles * MOE_TM)
    ys = _experts(xs, tile_expert, n_used, stack(w_gate), stack(w_up), stack(w_down))
    return _combine(h, ys, d1, d2, rg, g, ws, final_norm)


def kernel(x, norm_mix, norm_ffn, norm_final, ev_w_in, ev_mu, rw_w0, rw_w2, rw_a0, rw_a2, rw_g2, rw_k_k, rw_k_a, rw_r_k, rw_ln_w, rw_ln_b, hg_lb_logits, hg_norm, ev_w_out, od_w_in, gd_conv, gd_a_log, gd_dt_bias, gd_norm, ml_i_bias, ml_f_bias, ml_norm, od_w_out, moe_w_group, moe_b_group, moe_w_router, moe_b_router, moe_w_gate, moe_w_up, moe_w_down):
    bsz, t, d = x.shape
    n = bsz * t
    depth = norm_mix.shape[0]
    lb_table = jnp.cumsum(jax.nn.softmax(hg_lb_logits.astype(F32), axis=0), axis=0)
    def in_proj_weights(layer):
        j = layer // 2
        if layer % 2 == 0:
            return _pack_weights(ev_w_in[j], [(0, RW_IN), (RW_IN, RW_IN + HG_IN)])
        return _pack_weights(od_w_in[j], [(0, GD_MAIN), (GD_IN, GD_IN + ML_MAIN)],
                             gate_ranges=[(GD_MAIN, GD_IN), (GD_IN + ML_MAIN, GD_IN + ML_MAIN + 2 * ML_HEADS)])

    h = x.reshape(n, d)
    proj = _rms_proj(h, norm_mix[0], in_proj_weights(0))
    for layer in range(depth):
        j = layer // 2
        if layer % 2 == 0:
            p_rw, p_hg = proj
            ya = _rwkv7(p_rw.reshape(bsz, t, RW_IN), ev_mu[j], rw_w0[j], rw_w2[j], rw_a0[j], rw_a2[j], rw_g2[j],
                        rw_k_k[j], rw_k_a[j], rw_r_k[j], rw_ln_w[j], rw_ln_b[j])
            yb = _hgrn2(p_hg.reshape(bsz, t, HG_IN), lb_table[j], hg_norm[j])
            mix_a, mix_b, w_out = ya.reshape(n, RW_DIM), yb.reshape(n, HG_KDIM), ev_w_out[j]
        else:
            p_gd, p_ml, p_gt = proj
            p_gt = p_gt.reshape(bsz, t, LANES)
            yc = _gdn(p_gd.reshape(bsz, t, GD_MAIN), p_gt, gd_conv[j], gd_a_log[j], gd_dt_bias[j], gd_norm[j])
            yd = _mlstm(p_ml.reshape(bsz, t, ML_MAIN), p_gt, ml_i_bias[j], ml_f_bias[j], ml_norm[j])
            mix_a, mix_b, w_out = yc.reshape(n, GD_DIM), yd.reshape(n, ML_DIM), od_w_out[j]
        h, xn_tiles, rg, cnt, dest = _out_proj_route(h, mix_a, mix_b, w_out, norm_ffn[layer], moe_w_group[layer],
                                                     moe_b_group[layer], moe_w_router[layer], moe_b_router[layer])
        if layer == depth - 1:
            (h,) = _hmoe_residual(h, xn_tiles, rg, cnt, dest, layer, moe_w_gate, moe_w_up, moe_w_down,
                                  norm_final, final_norm=True)
        else:
            h, *proj = _hmoe_residual(h, xn_tiles, rg, cnt, dest, layer, moe_w_gate, moe_w_up, moe_w_down,
                                      norm_mix[layer + 1], ws=in_proj_weights(layer + 1))
    return h.reshape(bsz, t, d)
```

```python
import functools

import jax
import jax.numpy as jnp
from jax import lax
from jax.experimental import pallas as pl
from jax.experimental.pallas import tpu as pltpu

F32 = jnp.float32
BF16 = jnp.bfloat16
I32 = jnp.int32

D_MODEL = 1024
NORM_EPS = 1e-6
RW_HEADS, RW_HEAD_DIM = 8, 64
RW_DIM = RW_HEADS * RW_HEAD_DIM
R_DECAY, R_AAA, R_GATE = 64, 64, 128
RW_IN = 3 * RW_DIM + R_DECAY + R_AAA + R_GATE
RW_LN_EPS = 64e-5
HG_HEADS, HG_DIM = 8, 64
HG_KDIM = HG_HEADS * HG_DIM
HG_IN = 4 * HG_KDIM
GD_HEADS, GD_HEAD_DIM = 4, 128
GD_DIM = GD_HEADS * GD_HEAD_DIM
CONV_K = 4
GD_MAIN = 4 * GD_DIM
GD_IN = GD_MAIN + 2 * GD_HEADS
ML_HEADS, ML_HEAD_DIM = 4, 128
ML_DIM = ML_HEADS * ML_HEAD_DIM
ML_MAIN = 4 * ML_DIM
N_GROUPS, EXPERTS_PER_GROUP = 4, 8
N_EXPERTS = N_GROUPS * EXPERTS_PER_GROUP
D_EXPERT = 256

LANES = 128
VMEM_LIMIT_BYTES = 48 * 1024 * 1024

PROJ_TM = 256
PACK_TM = 128
CHUNK = 64
SUB = 16
RW_NCH = 4
GD_NCH = 8
HG_ROWS = 256
ML_CHUNK = 128
ML_NCH = 2
MOE_TM = 256
ROUTE_ROWS = 40
GATHER_UNROLL = 8
NEG = -1e30


def _cparams(*sem):
    return pltpu.CompilerParams(dimension_semantics=sem, vmem_limit_bytes=VMEM_LIMIT_BYTES)


def _mm(a, b):
    return jnp.dot(a.astype(BF16), b.astype(BF16), preferred_element_type=F32)


def _mm_nt(a, b):
    return lax.dot_general(a.astype(BF16), b.astype(BF16), (((1,), (1,)), ((), ())), preferred_element_type=F32)


def _mm_tn(a, b):
    return lax.dot_general(a.astype(BF16), b.astype(BF16), (((0,), (0,)), ((), ())), preferred_element_type=F32)


def _mm_split(x, ones_bf16):
    hi = x.astype(BF16)
    lo = (x - hi.astype(F32)).astype(BF16)
    return (jnp.dot(hi, ones_bf16, preferred_element_type=F32) + jnp.dot(lo, ones_bf16, preferred_element_type=F32))


def _iota2(shape):
    return lax.broadcasted_iota(I32, shape, 0), lax.broadcasted_iota(I32, shape, 1)


def _rms_rows(x, g, eps=NORM_EPS):
    return x * lax.rsqrt(jnp.mean(x * x, axis=-1, keepdims=True) + eps) * g


def _seg_ones(width, seg):
    r, c = _iota2((width, width))
    return jnp.where((r // seg) == (c // seg), 1.0, 0.0).astype(BF16)


def _split3(x):
    x1 = x.astype(BF16)
    r1 = x - x1.astype(F32)
    x2 = r1.astype(BF16)
    return x1, x2, (r1 - x2.astype(F32)).astype(BF16)


def _cumsum_rows(tri_bf16, x):
    return sum(jnp.dot(tri_bf16, t, preferred_element_type=F32) for t in _split3(x))


def _cumsum_cols(x, tri_bf16):
    return sum(jnp.dot(t, tri_bf16, preferred_element_type=F32) for t in _split3(x))


def _mm3(a, b):
    ah = a.astype(BF16)
    al = (a - ah.astype(F32)).astype(BF16)
    bh = b.astype(BF16)
    bl = (b - bh.astype(F32)).astype(BF16)
    dot = lambda x, y: jnp.dot(x, y, preferred_element_type=F32)
    return dot(ah, bh) + dot(ah, bl) + dot(al, bh)


def _tri_inv_multi(ms, n, chain):
    assert chain // SUB <= 4
    r, c = _iota2((n, n))
    same = (r // SUB) == (c // SUB)
    eye = jnp.where(r == c, 1.0, 0.0).astype(F32)
    ds = [jnp.where(same, m, 0.0) for m in ms]
    offs = [m - d for m, d in zip(ms, ds)]
    xs = [eye - d for d in ds]
    ps = ds
    for _ in range(3):
        ps = [_mm(p, p) for p in ps]
        xs = [x + _mm(x, p) for x, p in zip(xs, ps)]
    es = [_mm(x, o) for x, o in zip(xs, offs)]
    imes = [eye - e for e in es]
    e2s = [_mm(e, e) for e in es]
    ys = [i + _mm(i, e2) for i, e2 in zip(imes, e2s)]
    xs = [_mm(y, x) for y, x in zip(ys, xs)]
    res = [eye - x - _mm3(m, x) for m, x in zip(ms, xs)]
    return [x + _mm(x, rr) for x, rr in zip(xs, res)]


def _pack_weights_kernel(w_ref, *o_refs, ranges, gate_ranges):
    w = w_ref[...]
    for (lo, hi), o_ref in zip(ranges, o_refs):
        o_ref[...] = w[:, lo:hi].astype(BF16)
    if gate_ranges:
        cols = [w[:, lo:hi] for lo, hi in gate_ranges]
        used = sum(hi - lo for lo, hi in gate_ranges)
        cols.append(jnp.zeros((w.shape[0], LANES - used), F32))
        o_refs[-1][...] = jnp.concatenate(cols, axis=1).astype(BF16)


def _pack_weights(w, ranges, gate_ranges=()):
    rows, cols = w.shape
    tm = PACK_TM
    widths = [hi - lo for lo, hi in ranges] + ([LANES] if gate_ranges else [])
    return pl.pallas_call(
        functools.partial(_pack_weights_kernel, ranges=tuple(ranges), gate_ranges=tuple(gate_ranges)),
        grid=(rows // tm,),
        in_specs=[pl.BlockSpec((tm, cols), lambda i: (i, 0))],
        out_specs=[pl.BlockSpec((tm, wd), lambda i: (i, 0)) for wd in widths],
        out_shape=[jax.ShapeDtypeStruct((rows, wd), BF16) for wd in widths],
        compiler_params=_cparams("parallel"),
        name="pack_weights",
    )(w)


def _rms_proj_kernel(x_ref, g_ref, *refs, n_out):
    y = _rms_rows(x_ref[...], g_ref[...]).astype(BF16)
    for w_ref, o_ref in zip(refs[:n_out], refs[n_out:]):
        o_ref[...] = jnp.dot(y, w_ref[...], preferred_element_type=F32)


def _rms_proj(x, g, ws):
    n, d = x.shape
    tm = min(PROJ_TM, n)
    in_specs = [pl.BlockSpec((tm, d), lambda i: (i, 0)), pl.BlockSpec((1, d), lambda i: (0, 0))]
    in_specs += [pl.BlockSpec(w.shape, lambda i: (0, 0)) for w in ws]
    return pl.pallas_call(
        functools.partial(_rms_proj_kernel, n_out=len(ws)),
        grid=(n // tm,),
        in_specs=in_specs,
        out_specs=[pl.BlockSpec((tm, w.shape[1]), lambda i: (i, 0)) for w in ws],
        out_shape=[jax.ShapeDtypeStruct((n, w.shape[1]), F32) for w in ws],
        compiler_params=_cparams("parallel"),
        name="rms_proj",
    )(x, g.reshape(1, d), *ws)


ROW_TILE = D_MODEL // LANES


def _store_row_tiles(ref, x):
    for j in range(ROW_TILE):
        ref[pl.ds(j, x.shape[0], stride=ROW_TILE), :] = x[:, j * LANES:(j + 1) * LANES]


def _load_row_tiles(ref, rows):
    return jnp.concatenate([ref[pl.ds(j, rows, stride=ROW_TILE), :] for j in range(ROW_TILE)], axis=1)


def _rwkv7_kernel(p_ref, mu_ref, w0_ref, w2_ref, a0_ref, a2_ref, g2_ref, kk_ref, ka_ref, rk_ref,
                  lnw_ref, lnb_ref, o_ref, prev_ref, zt_ref):
    L = CHUNK
    npair = RW_HEADS // 2
    c = pl.program_id(1)

    @pl.when(c == 0)
    def _():
        prev_ref[...] = jnp.zeros_like(prev_ref)
        zt_ref[...] = jnp.zeros_like(zt_ref)

    x = p_ref[0]
    R = x.shape[0]
    nch = R // L
    row = lax.broadcasted_iota(I32, x.shape, 0)
    xs = jnp.where(row == 0, prev_ref[7:8, :], pltpu.roll(x, 1, 0))
    prev_ref[...] = x[R - 8:R, :]
    pm = x + mu_ref[...] * (xs - x)
    r_all = pm[:, 0:RW_DIM]
    k_all = pm[:, RW_DIM:2 * RW_DIM]
    v_all = pm[:, 2 * RW_DIM:3 * RW_DIM]
    wa = pm[:, 3 * RW_DIM:3 * RW_DIM + LANES]
    gl = pm[:, 3 * RW_DIM + LANES:]
    wlog = -jax.nn.softplus(-(w0_ref[...] + _mm(jnp.tanh(wa), w2_ref[...]))) - 0.5
    ld = -jnp.exp(wlog)
    a_all = jax.nn.sigmoid(a0_ref[...] + _mm(wa, a2_ref[...]))
    g_all = _mm(jax.nn.sigmoid(gl), g2_ref[...])

    tr, tc = _iota2((R, R))
    chunk_tril = jnp.where((tc <= tr) & ((tr // L) == (tc // L)), 1.0, 0.0).astype(BF16)
    cs_all = _cumsum_rows(chunk_tril, ld)
    seg = _seg_ones(LANES, RW_HEAD_DIM)
    lane = lax.broadcasted_iota(I32, (L, LANES), 1)
    hm = (lane < RW_HEAD_DIM, lane >= RW_HEAD_DIM)
    br, bc = _iota2((2 * L, 2 * L))
    bd = (br // L) == (bc // L)
    bd_strict = bd & (bc < br)
    bd_incl = bd & (bc <= br)
    fold = lambda z: z[0:L] + z[L:2 * L]
    both = lambda z: jnp.concatenate([jnp.where(hm[0], z, 0.0), jnp.where(hm[1], z, 0.0)], axis=0)

    units = []
    for ci in range(nch):
        rs = slice(ci * L, (ci + 1) * L)
        for p in range(npair):
            sl = slice(p * LANES, (p + 1) * LANES)
            r, k, v, a = r_all[rs, sl], k_all[rs, sl], v_all[rs, sl], a_all[rs, sl]
            cs, ldp = cs_all[rs, sl], ld[rs, sl]
            kkr = k * kk_ref[:, sl]
            kk = kkr * lax.rsqrt(_mm_split(kkr * kkr, seg) + 1e-6)
            k2 = k * (1.0 + (a - 1.0) * ka_ref[:, sl])
            b = kk * a
            cs_last = cs[L - 1:L, :]
            e_neg = jnp.exp(-cs)
            e_rem = jnp.exp(cs_last - cs)
            bhat = b * e_neg
            khat = k2 * e_neg
            units.append(dict(p=p, rs=rs, sl=sl, r=r, v=v, k2=k2, rhat=r * jnp.exp(cs), gam_last=jnp.exp(cs_last),
                              btil=b * e_rem, ktil=k2 * e_rem, a2=both(kk * jnp.exp(cs - ldp)), v2=both(v),
                              rhs4=jnp.concatenate([bhat, bhat, khat, khat], axis=0)))
    for q in units:
        lhs = jnp.concatenate([q["a2"], both(q["rhat"])], axis=0)
        q["g"] = _mm_nt(lhs, q["rhs4"])
    tinvs = _tri_inv_multi([jnp.where(bd_strict, q["g"][0:2 * L, 0:2 * L], 0.0) for q in units], 2 * L, L)
    for q, tinv in zip(units, tinvs):
        q["tinv"] = tinv
        q["x2"] = _mm(jnp.where(bd_strict, q["g"][0:2 * L, 2 * L:4 * L], 0.0), q["v2"])
    for q in units:
        uw = _mm(q["tinv"], jnp.concatenate([q["x2"], q["a2"]], axis=1))
        q["u0"] = -fold(uw[:, 0:LANES])
        q["w"] = fold(uw[:, LANES:])
        q["y0"] = fold(_mm(jnp.where(bd_incl, q["g"][2 * L:4 * L, 2 * L:4 * L], 0.0), q["v2"]))
    for q in units:
        rb = jnp.where(bd_incl, q["g"][2 * L:4 * L, 0:2 * L], 0.0)
        ruw = _mm(rb, jnp.concatenate([both(q["u0"]), both(q["w"])], axis=1))
        q["yc"] = q["y0"] + fold(ruw[:, 0:LANES])
        q["ry"] = q["rhat"] - fold(ruw[:, LANES:])
        q["c1"] = _mm_tn(jnp.concatenate([q["u0"], q["v"]], axis=0), jnp.concatenate([q["btil"], q["ktil"]], axis=0))
        q["c2"] = _mm_tn(q["w"], q["btil"])

    hr, hc = _iota2((LANES, LANES))
    head_bd = (hr // RW_HEAD_DIM) == (hc // RW_HEAD_DIM)
    zts = [zt_ref[p] for p in range(npair)]
    for ci in range(nch):
        qs = units[ci * npair:(ci + 1) * npair]
        for q, zt in zip(qs, zts):
            q["y"] = q["yc"] + _mm_nt(q["ry"], zt)
        zts = [zt * q["gam_last"] + jnp.where(head_bd, q["c1"] - _mm(zt, q["c2"]), 0.0) for q, zt in zip(qs, zts)]
    for p in range(npair):
        zt_ref[p] = zts[p]

    for q in units:
        sl, rs, y = q["sl"], q["rs"], q["y"]
        mean = _mm_split(y, seg) * (1.0 / RW_HEAD_DIM)
        yc = y - mean
        var = _mm_split(yc * yc, seg) * (1.0 / RW_HEAD_DIM)
        yn = yc * lax.rsqrt(var + RW_LN_EPS) * lnw_ref[:, sl] + lnb_ref[:, sl]
        bonus = _mm_split(q["r"] * q["k2"] * rk_ref[:, sl], seg) * q["v"]
        o_ref[0, rs, sl] = (yn + bonus) * g_all[rs, sl]


def _rwkv7(p, mu, w0, w2, a0, a2, g2, k_k, k_a, r_k, ln_w, ln_b):
    bsz, t, _ = p.shape
    row = lambda z: z.reshape(1, -1).astype(F32)
    w2p = jnp.concatenate([w2, jnp.zeros_like(w2)], axis=0).astype(BF16)
    a2p = jnp.concatenate([jnp.zeros_like(a2), a2], axis=0).astype(BF16)
    params = [row(mu), row(w0), w2p, row(a0), a2p, g2.astype(BF16), row(k_k), row(k_a), row(r_k), row(ln_w), row(ln_b)]
    full = lambda z: pl.BlockSpec(z.shape, lambda b, c: (0, 0))
    rows = RW_NCH * CHUNK
    return pl.pallas_call(
        _rwkv7_kernel,
        grid=(bsz, t // rows),
        in_specs=[pl.BlockSpec((1, rows, RW_IN), lambda b, c: (b, c, 0))] + [full(z) for z in params],
        out_specs=pl.BlockSpec((1, rows, RW_DIM), lambda b, c: (b, c, 0)),
        out_shape=jax.ShapeDtypeStruct((bsz, t, RW_DIM), F32),
        scratch_shapes=[pltpu.VMEM((8, RW_IN), F32), pltpu.VMEM((RW_HEADS // 2, LANES, LANES), F32)],
        compiler_params=_cparams("arbitrary", "arbitrary"),
        name="rwkv7_mix",
    )(p, *params)


def _hgrn2_kernel(p_ref, lb_ref, ng_ref, o_ref, st_ref):
    L = HG_ROWS
    c = pl.program_id(1)

    @pl.when(c == 0)
    def _():
        st_ref[...] = jnp.zeros_like(st_ref)

    x = p_ref[0]
    lb = lb_ref[...]
    q_all = jax.nn.silu(x[:, 0:HG_KDIM])
    fg = lb + (1.0 - lb) * jax.nn.sigmoid(x[:, HG_KDIM:2 * HG_KDIM])
    k_all = 1.0 - fg
    logf = jnp.log(fg)
    v_all = x[:, 2 * HG_KDIM:3 * HG_KDIM]
    gate = x[:, 3 * HG_KDIM:]
    tr, tc = _iota2((L, L))
    blk_tril = jnp.where((tc <= tr) & ((tr // SUB) == (tc // SUB)), 1.0, 0.0).astype(BF16)
    bc_all = _cumsum_rows(blk_tril, logf)
    seg = _seg_ones(LANES, HG_DIM)
    br, bcc = _iota2((LANES, LANES))
    bd = (br // HG_DIM) == (bcc // HG_DIM)
    half = SUB // 2
    t_lo = lax.broadcasted_iota(I32, (SUB, LANES), 0)
    t_hi = lax.broadcasted_iota(I32, (half, LANES), 0) + half

    nsub = L // SUB
    npair = HG_HEADS // 2
    rows_per_unit = half * SUB + half * half
    sub_of_row = lax.broadcasted_iota(I32, (L, LANES), 0) // SUB

    units = []
    parts = []
    for p in range(npair):
        sl = slice(p * LANES, (p + 1) * LANES)
        for j in range(nsub):
            rs = slice(j * SUB, (j + 1) * SUB)
            q, k, v, bc = q_all[rs, sl], k_all[rs, sl], v_all[rs, sl], bc_all[rs, sl]
            q_hi, bc_hi = q[half:], bc[half:]
            for s in range(SUB):
                if s < half:
                    diff = jnp.where(t_lo >= s, bc - bc[s:s + 1, :], NEG)
                    parts.append(jnp.exp(diff) * q * k[s:s + 1, :])
                else:
                    diff = jnp.where(t_hi >= s, bc_hi - bc[s:s + 1, :], NEG)
                    parts.append(jnp.exp(diff) * q_hi * k[s:s + 1, :])
            bend = bc[SUB - 1:SUB, :]
            units.append(dict(v=v, qt=q * jnp.exp(bc), dec=jnp.exp(bend), kd=k * jnp.exp(bend - bc)))
    score_all = _mm(jnp.concatenate(parts, axis=0), seg)
    for ui, un in enumerate(units):
        score = score_all[ui * rows_per_unit:(ui + 1) * rows_per_unit]
        v = un["v"]
        acc_lo = jnp.zeros((half, LANES), F32)
        acc_hi = jnp.zeros((half, LANES), F32)
        off = 0
        for s in range(SUB):
            vs = v[s:s + 1, :]
            if s < half:
                acc_lo = acc_lo + score[off:off + half] * vs
                acc_hi = acc_hi + score[off + half:off + SUB] * vs
                off += SUB
            else:
                acc_hi = acc_hi + score[off:off + half] * vs
                off += half
        un["intra"] = jnp.concatenate([acc_lo, acc_hi], axis=0)
    for p in range(npair):
        us = units[p * nsub:(p + 1) * nsub]
        v_pair = jnp.concatenate([un["v"] for un in us], axis=0)
        kd_pair = jnp.concatenate([un["kd"] for un in us], axis=0)
        kd_wide = jnp.concatenate([jnp.where(sub_of_row == j, kd_pair, 0.0) for j in range(nsub)], axis=1)
        upd = _mm_tn(v_pair, kd_wide)
        for j, un in enumerate(us):
            un["upd"] = jnp.where(bd, upd[:, j * LANES:(j + 1) * LANES], 0.0)

    for p in range(npair):
        sl = slice(p * LANES, (p + 1) * LANES)
        st = st_ref[p]
        outs = []
        for un in units[p * nsub:(p + 1) * nsub]:
            outs.append(un["intra"] + _mm_nt(un["qt"], st))
            st = st * un["dec"] + un["upd"]
        st_ref[p] = st
        o = jnp.concatenate(outs, axis=0)
        ms = _mm_split(o * o, seg) * (1.0 / HG_DIM)
        o_ref[0, :, sl] = o * lax.rsqrt(ms + NORM_EPS) * ng_ref[:, sl] * jax.nn.silu(gate[:, sl])


def _hgrn2(p, lb, norm_g):
    bsz, t, _ = p.shape
    lb = lb.reshape(1, HG_KDIM).astype(F32)
    ng = jnp.tile(norm_g.astype(F32), HG_HEADS).reshape(1, HG_KDIM)
    return pl.pallas_call(
        _hgrn2_kernel,
        grid=(bsz, t // HG_ROWS),
        in_specs=[pl.BlockSpec((1, HG_ROWS, HG_IN), lambda b, c: (b, c, 0)),
                  pl.BlockSpec((1, HG_KDIM), lambda b, c: (0, 0)), pl.BlockSpec((1, HG_KDIM), lambda b, c: (0, 0))],
        out_specs=pl.BlockSpec((1, HG_ROWS, HG_KDIM), lambda b, c: (b, c, 0)),
        out_shape=jax.ShapeDtypeStruct((bsz, t, HG_KDIM), F32),
        scratch_shapes=[pltpu.VMEM((HG_HEADS // 2, LANES, LANES), F32)],
        compiler_params=_cparams("arbitrary", "arbitrary"),
        name="hgrn2_mix",
    )(p, lb, ng)


def _gdn_kernel(p_ref, pg_ref, cw_ref, gb_ref, nal_ref, nalc_ref, ng_ref, o_ref, prev_ref, s_ref):
    L = CHUNK
    c = pl.program_id(1)

    @pl.when(c == 0)
    def _():
        prev_ref[...] = jnp.zeros_like(prev_ref)
        s_ref[...] = jnp.zeros_like(s_ref)

    x = p_ref[0]
    R = x.shape[0]
    nch = R // L
    xq = x[:, 0:3 * GD_DIM]
    xcat = jnp.concatenate([prev_ref[...], xq], axis=0)
    conv = xq * cw_ref[CONV_K - 1:CONV_K, :]
    for j in range(1, CONV_K):
        conv = conv + xcat[8 - j:8 - j + R, :] * cw_ref[CONV_K - 1 - j:CONV_K - j, :]
    prev_ref[...] = xq[R - 8:R, :]
    qkv = jax.nn.silu(conv)
    z = x[:, 3 * GD_DIM:]

    gt = pg_ref[0] + gb_ref[...]
    beta_all = jax.nn.sigmoid(gt)
    g_all = nal_ref[...] * jax.nn.softplus(gt)
    g_t = nalc_ref[...] * jax.nn.softplus(gt.T)
    tr, tc = _iota2((R, R))
    chunk_tril = jnp.where((tc <= tr) & ((tr // L) == (tc // L)), 1.0, 0.0).astype(BF16)
    gam_all = _cumsum_rows(chunk_tril, g_all)
    ur, uc = _iota2((R, 2 * R))
    triu2 = jnp.where(((ur // L) == (uc // (2 * L))) & ((ur % L) <= (uc % L)), 1.0, 0.0).astype(BF16)
    gam_t2 = _cumsum_cols(g_t, triu2)
    lr, lc = _iota2((L, L))
    incl = lc <= lr
    br, bc = _iota2((2 * L, 2 * L))
    bd = (br // L) == (bc // L)
    bd_strict = bd & (bc < br)
    bd_incl = bd & (bc <= br)
    lane2 = lax.broadcasted_iota(I32, (1, 2 * L), 1)
    zero = jnp.zeros((L, LANES), F32)

    units = []
    for ci in range(nch):
        rs = slice(ci * L, (ci + 1) * L)
        for h in range(GD_HEADS):
            q = qkv[rs, h * LANES:(h + 1) * LANES]
            k = qkv[rs, GD_DIM + h * LANES:GD_DIM + (h + 1) * LANES]
            v = qkv[rs, 2 * GD_DIM + h * LANES:2 * GD_DIM + (h + 1) * LANES]
            q = q * lax.rsqrt(jnp.sum(q * q, axis=-1, keepdims=True) + 1e-6) * (GD_HEAD_DIM ** -0.5)
            k = k * lax.rsqrt(jnp.sum(k * k, axis=-1, keepdims=True) + 1e-6)
            beta = beta_all[rs, h:h + 1]
            gam = gam_all[rs, GD_HEADS + h:GD_HEADS + h + 1]
            gam_row2 = gam_t2[GD_HEADS + h:GD_HEADS + h + 1, ci * 2 * L:(ci + 1) * 2 * L]
            gam_last = gam[L - 1:L, :]
            kb = k * beta
            units.append(dict(h=h, rs=rs, q=q, k=k, kb=kb, vb=v * beta, gam=gam, gam_row2=gam_row2,
                              gam_last=gam_last, kg=kb * jnp.exp(gam), qg=q * jnp.exp(gam),
                              kd=k * jnp.exp(gam_last - gam),
                              decay=jnp.exp(jnp.where(incl, gam - gam_row2[:, 0:L], NEG))))
    pairs = [(units[i], units[i + 1]) for i in range(0, len(units), 2)]
    ms = []
    for h0, h1 in pairs:
        lhs = jnp.concatenate([jnp.concatenate([h0["kb"], zero], axis=1),
                               jnp.concatenate([zero, h1["kb"]], axis=1)], axis=0)
        rhs = jnp.concatenate([jnp.concatenate([h0["k"], zero], axis=1),
                               jnp.concatenate([zero, h1["k"]], axis=1)], axis=0)
        gam_col = jnp.concatenate([h0["gam"], h1["gam"]], axis=0)
        gam_row = jnp.where(lane2 < L, h0["gam_row2"], h1["gam_row2"])
        decay2 = jnp.exp(jnp.where(bd_incl, gam_col - gam_row, NEG))
        ms.append(jnp.where(bd_strict, _mm_nt(lhs, rhs) * decay2, 0.0))
    tinvs = _tri_inv_multi(ms, 2 * L, L)
    for (h0, h1), tinv in zip(pairs, tinvs):
        rhs = jnp.concatenate([jnp.concatenate([h0["vb"], h0["kg"]], axis=1),
                               jnp.concatenate([h1["vb"], h1["kg"]], axis=1)], axis=0)
        uw = _mm(tinv, rhs)
        h0["u"], h0["w"] = uw[0:L, 0:LANES], uw[0:L, LANES:]
        h1["u"], h1["w"] = uw[L:2 * L, 0:LANES], uw[L:2 * L, LANES:]
    for hd in units:
        attn = _mm_nt(hd["q"], hd["k"]) * hd["decay"]
        uw = jnp.concatenate([hd["u"], hd["w"]], axis=1)
        auw = _mm(attn, uw)
        hd["o0"] = auw[:, 0:LANES]
        hd["qs"] = hd["qg"] - auw[:, LANES:]
        cc = _mm_tn(hd["kd"], uw)
        hd["c1"], hd["c2"] = cc[:, 0:LANES], cc[:, LANES:]

    ss = [s_ref[h] for h in range(GD_HEADS)]
    for ci in range(nch):
        hds = units[ci * GD_HEADS:(ci + 1) * GD_HEADS]
        for hd, s in zip(hds, ss):
            hd["o"] = hd["o0"] + _mm(hd["qs"], s)
        ss = [s * jnp.exp(hd["gam_last"]) + hd["c1"] - _mm(hd["c2"], s) for hd, s in zip(hds, ss)]
    for h in range(GD_HEADS):
        s_ref[h] = ss[h]
    for hd in units:
        sl = slice(hd["h"] * LANES, (hd["h"] + 1) * LANES)
        o_ref[0, hd["rs"], sl] = _rms_rows(hd["o"], ng_ref[...]) * jax.nn.silu(z[hd["rs"], sl])


def _gate_row(lo, vals):
    return jnp.zeros((1, LANES), F32).at[0, lo:lo + vals.shape[0]].set(vals.astype(F32))


def _gdn(p, pg, conv_w, a_log, dt_bias, norm_g):
    bsz, t, _ = p.shape
    gbias = _gate_row(GD_HEADS, dt_bias)
    nal = _gate_row(GD_HEADS, -jnp.exp(a_log))
    nal_col = nal.reshape(LANES, 1)
    ng = norm_g.reshape(1, GD_HEAD_DIM).astype(F32)
    full = lambda z: pl.BlockSpec(z.shape, lambda b, c: (0, 0))
    rows = GD_NCH * CHUNK
    return pl.pallas_call(
        _gdn_kernel,
        grid=(bsz, t // rows),
        in_specs=[pl.BlockSpec((1, rows, GD_MAIN), lambda b, c: (b, c, 0)),
                  pl.BlockSpec((1, rows, LANES), lambda b, c: (b, c, 0)),
                  full(conv_w), full(gbias), full(nal), full(nal_col), full(ng)],
        out_specs=pl.BlockSpec((1, rows, GD_DIM), lambda b, c: (b, c, 0)),
        out_shape=jax.ShapeDtypeStruct((bsz, t, GD_DIM), F32),
        scratch_shapes=[pltpu.VMEM((8, 3 * GD_DIM), F32), pltpu.VMEM((GD_HEADS, LANES, LANES), F32)],
        compiler_params=_cparams("arbitrary", "arbitrary"),
        name="gdn_mix",
    )(p, pg, conv_w.astype(F32), gbias, nal, nal_col, ng)


def _mlstm_kernel(p_ref, pg_ref, gb_ref, ng_ref, o_ref, c_ref, n_ref, m_ref):
    L = ML_CHUNK
    ci = pl.program_id(1)

    @pl.when(ci == 0)
    def _():
        c_ref[...] = jnp.zeros_like(c_ref)
        n_ref[...] = jnp.zeros_like(n_ref)
        m_ref[...] = jnp.zeros_like(m_ref)

    x = p_ref[0]
    R = x.shape[0]
    nch = R // L
    gt = pg_ref[0] + gb_ref[...]
    logf = jax.nn.log_sigmoid(gt)
    gt_t = gt.T
    logf_t = jax.nn.log_sigmoid(gt_t)
    tr, tc = _iota2((R, R))
    same = (tr // L) == (tc // L)
    bc_all = _cumsum_rows(jnp.where(same & (tc <= tr), 1.0, 0.0).astype(BF16), logf)
    bc_t = _cumsum_cols(logf_t, jnp.where(same & (tr <= tc), 1.0, 0.0).astype(BF16))
    lr, lc = _iota2((L, L))
    incl = lc <= lr
    i_lo, f_lo = 2 * GD_HEADS, 2 * GD_HEADS + ML_HEADS

    units = []
    m_run = [m_ref[h][0:1, 0:1] for h in range(ML_HEADS)]
    for ci in range(nch):
        rs = slice(ci * L, (ci + 1) * L)
        for h in range(ML_HEADS):
            sl = slice(h * LANES, (h + 1) * LANES)
            q = x[rs, sl]
            k = x[rs, ML_DIM + h * LANES:ML_DIM + (h + 1) * LANES] * (ML_HEAD_DIM ** -0.5)
            v = x[rs, 2 * ML_DIM + h * LANES:2 * ML_DIM + (h + 1) * LANES]
            bc = bc_all[rs, f_lo + h:f_lo + h + 1]
            bc_row = bc_t[f_lo + h:f_lo + h + 1, rs]
            ig = gt[rs, i_lo + h:i_lo + h + 1]
            ig_row = gt_t[i_lo + h:i_lo + h + 1, rs]
            m_prev = m_run[h]
            b_last = bc[L - 1:L, :]
            d_log = jnp.where(incl, bc - bc_row + ig_row, NEG)
            inter_log = bc + m_prev
            m_t = jnp.maximum(inter_log, jnp.max(d_log, axis=-1, keepdims=True))
            upd_log = b_last - bc + ig
            m_new = jnp.maximum(b_last + m_prev, jnp.max(upd_log, axis=0, keepdims=True))
            m_run[h] = m_new
            wk = jnp.exp(upd_log - m_new) * k
            units.append(dict(h=h, rs=rs, sl=sl, q=q, v=v, m_t=m_t, wk=wk, inter_w=jnp.exp(inter_log - m_t),
                              dec=jnp.exp(b_last + m_prev - m_new), sc=_mm_nt(q, k) * jnp.exp(d_log - m_t)))
    for hd in units:
        hd["scv"] = _mm(hd["sc"], hd["v"])
        hd["kv"] = _mm_tn(hd["wk"], hd["v"])
    cs = [c_ref[h] for h in range(ML_HEADS)]
    ns = [n_ref[h][0:1, :] for h in range(ML_HEADS)]
    for hd in units:
        h = hd["h"]
        q, sc, inter_w, dec = hd["q"], hd["sc"], hd["inter_w"], hd["dec"]
        num = inter_w * _mm(q, cs[h]) + hd["scv"]
        den = inter_w * jnp.sum(q * ns[h], axis=-1, keepdims=True) + jnp.sum(sc, axis=-1, keepdims=True)
        hd["hh"] = num / jnp.maximum(jnp.abs(den), jnp.exp(-hd["m_t"]))
        cs[h] = dec * cs[h] + hd["kv"]
        ns[h] = dec * ns[h] + jnp.sum(hd["wk"], axis=0, keepdims=True)
    for h in range(ML_HEADS):
        c_ref[h] = cs[h]
        n_ref[h] = jnp.broadcast_to(ns[h], (8, LANES))
        m_ref[h] = jnp.broadcast_to(m_run[h], (8, LANES))
    for hd in units:
        h, rs, sl = hd["h"], hd["rs"], hd["sl"]
        og = jax.nn.sigmoid(x[rs, 3 * ML_DIM + h * LANES:3 * ML_DIM + (h + 1) * LANES])
        o_ref[0, rs, sl] = og * _rms_rows(hd["hh"], ng_ref[:, sl])


def _mlstm(p, pg, i_bias, f_bias, norm_g):
    bsz, t, _ = p.shape
    gbias = _gate_row(2 * GD_HEADS, jnp.concatenate([i_bias, f_bias]))
    ng = norm_g.reshape(1, ML_DIM).astype(F32)
    lc = ML_NCH * ML_CHUNK
    full = lambda z: pl.BlockSpec(z.shape, lambda b, c: (0, 0))
    return pl.pallas_call(
        _mlstm_kernel,
        grid=(bsz, t // lc),
        in_specs=[pl.BlockSpec((1, lc, ML_MAIN), lambda b, c: (b, c, 0)),
                  pl.BlockSpec((1, lc, LANES), lambda b, c: (b, c, 0)), full(gbias), full(ng)],
        out_specs=pl.BlockSpec((1, lc, ML_DIM), lambda b, c: (b, c, 0)),
        out_shape=jax.ShapeDtypeStruct((bsz, t, ML_DIM), F32),
        scratch_shapes=[pltpu.VMEM((ML_HEADS, LANES, LANES), F32), pltpu.VMEM((ML_HEADS, 8, LANES), F32),
                        pltpu.VMEM((ML_HEADS, 8, LANES), F32)],
        compiler_params=_cparams("arbitrary", "arbitrary"),
        name="mlstm_mix",
    )(p, pg, gbias, ng)


def _route(h, g_ref, wh_ref, wl_ref, b_ref, off_ref):
    tm = h.shape[0]
    xn = _rms_rows(h, g_ref[...])
    xh = xn.astype(BF16)
    xl = (xn - xh.astype(F32)).astype(BF16)
    dot = lambda a, b: jnp.dot(a, b, preferred_element_type=F32)
    logits = dot(xh, wh_ref[...]) + dot(xh, wl_ref[...]) + dot(xl, wh_ref[...]) + b_ref[...]
    lt = logits.T[0:ROUTE_ROWS, :]
    row = lax.broadcasted_iota(I32, lt.shape, 0)
    big = jnp.int32(1 << 20)
    is_grp = (row >= N_EXPERTS) & (row < N_EXPERTS + N_GROUPS)
    lg = jnp.where(is_grp, lt, NEG)
    gmax = jnp.max(lg, axis=0, keepdims=True)
    p_top = 1.0 / jnp.sum(jnp.exp(lg - gmax), axis=0, keepdims=True)
    g_idx = jnp.min(jnp.where(lg == gmax, row, big), axis=0, keepdims=True) - N_EXPERTS
    valid = (row < N_EXPERTS) & ((row // EXPERTS_PER_GROUP) == g_idx)
    v1 = jnp.where(valid, lt, NEG)
    m1 = jnp.max(v1, axis=0, keepdims=True)
    i1 = jnp.min(jnp.where(v1 == m1, row, big), axis=0, keepdims=True)
    v2 = jnp.where(row == i1, NEG, v1)
    m2 = jnp.max(v2, axis=0, keepdims=True)
    i2 = jnp.min(jnp.where(v2 == m2, row, big), axis=0, keepdims=True)
    e21 = jnp.exp(m2 - m1)
    gate1 = p_top / (1.0 + e21)
    gate2 = p_top * e21 / (1.0 + e21)

    sel1 = row == i1
    sel2 = row == i2
    onehot = jnp.where(sel1 | sel2, 1.0, 0.0)
    tr, tc = _iota2((tm, tm))
    earlier = jnp.where(tr < tc, 1.0, 0.0).astype(BF16)
    off = off_ref[0:ROUTE_ROWS, 0:1]
    prefix = jnp.dot(onehot.astype(BF16), earlier, preferred_element_type=F32) + off
    rank1 = jnp.sum(jnp.where(sel1, prefix, 0.0), axis=0, keepdims=True)
    rank2 = jnp.sum(jnp.where(sel2, prefix, 0.0), axis=0, keepdims=True)
    off_ref[0:ROUTE_ROWS, :] = jnp.broadcast_to(off + jnp.sum(onehot, axis=1, keepdims=True), (ROUTE_ROWS, LANES))

    packed = jnp.concatenate([i1.astype(F32), i2.astype(F32), rank1, rank2, gate1, gate2,
                              jnp.zeros((2, tm), F32)], axis=0)
    return xn, packed


def _out_proj_route_kernel(h_ref, ya_ref, yb_ref, wa_ref, wb_ref, g_ref, wh_ref, wl_ref, b_ref,
                           o_ref, xt_ref, rg_ref, cnt_ref, dest_ref, off_ref, rt_ref):
    phase = pl.program_id(0)
    s = pl.program_id(1)
    tm = h_ref.shape[0]

    @pl.when((phase == 0) & (s == 0))
    def _():
        off_ref[...] = jnp.zeros_like(off_ref)

    @pl.when(phase == 0)
    def _():
        out = h_ref[...] + _mm(ya_ref[...], wa_ref[...]) + _mm(yb_ref[...], wb_ref[...])
        o_ref[...] = out
        xn, packed = _route(out, g_ref, wh_ref, wl_ref, b_ref, off_ref)
        _store_row_tiles(xt_ref, xn)
        rt_ref[s] = packed
        wide = jnp.concatenate([packed, jnp.zeros((LANES - 8, tm), F32)], axis=0)
        rg_ref[...] = wide.T
        cnt_ref[...] = off_ref[...]

    @pl.when(phase == 1)
    def _():
        counts = off_ref[...]
        tiles_per = jnp.floor((counts + (MOE_TM - 1)) * (1.0 / MOE_TM))
        er, ec = _iota2((LANES, LANES))
        tile_end = _mm(jnp.where(ec <= er, 1.0, 0.0), tiles_per)
        seg_start = ((tile_end - tiles_per) * MOE_TM)[0:ROUTE_ROWS, 0:1]
        blk = rt_ref[s]
        row = lax.broadcasted_iota(I32, (ROUTE_ROWS, tm), 0).astype(F32)
        pick = lambda e: jnp.sum(jnp.where(row == e, seg_start, 0.0), axis=0, keepdims=True)
        d1 = pick(blk[0:1]) + blk[2:3]
        d2 = pick(blk[1:2]) + blk[3:4]
        dest_ref[...] = (jnp.concatenate([d1, d2, jnp.zeros((6, tm), F32)], axis=0) * ROW_TILE).astype(I32)


def _out_proj_route(h, ya, yb, w_out, g_ffn, w_group, b_group, w_router, b_router):
    n, d = h.shape
    da, db = ya.shape[1], yb.shape[1]
    tm = min(PROJ_TM, n)
    wa = w_out[:da].astype(BF16)
    wb = w_out[da:].astype(BF16)
    pad = LANES - N_EXPERTS - N_GROUPS
    w_cat = jnp.concatenate([w_router, w_group, jnp.zeros((d, pad), F32)], axis=1)
    b_cat = jnp.concatenate([b_router, b_group, jnp.zeros((pad,), F32)]).reshape(1, LANES)
    w_hi = w_cat.astype(BF16)
    w_lo = (w_cat - w_hi.astype(F32)).astype(BF16)
    steps = n // tm
    row = lambda p, s: (s * (1 - p) + (steps - 1) * p, 0)
    fixed = lambda p, s: (0, 0)
    return pl.pallas_call(
        _out_proj_route_kernel,
        grid=(2, steps),
        in_specs=[pl.BlockSpec((tm, d), row), pl.BlockSpec((tm, da), row), pl.BlockSpec((tm, db), row),
                  pl.BlockSpec((da, d), fixed), pl.BlockSpec((db, d), fixed), pl.BlockSpec((1, d), fixed),
                  pl.BlockSpec((d, LANES), fixed), pl.BlockSpec((d, LANES), fixed), pl.BlockSpec((1, LANES), fixed)],
        out_specs=[pl.BlockSpec((tm, d), row), pl.BlockSpec((tm * ROW_TILE, LANES), row),
                   pl.BlockSpec((tm, LANES), row), pl.BlockSpec((LANES, LANES), fixed),
                   pl.BlockSpec((8, tm), lambda p, s: (0, s * p))],
        out_shape=[jax.ShapeDtypeStruct((n, d), F32), jax.ShapeDtypeStruct((n * ROW_TILE, LANES), F32),
                   jax.ShapeDtypeStruct((n, LANES), F32), jax.ShapeDtypeStruct((LANES, LANES), F32),
                   jax.ShapeDtypeStruct((8, n), I32)],
        scratch_shapes=[pltpu.VMEM((LANES, LANES), F32), pltpu.VMEM((steps, 8, tm), F32)],
        compiler_params=_cparams("arbitrary", "arbitrary"),
        name="out_proj_route",
    )(h, ya, yb, wa, wb, g_ffn.reshape(1, d), w_hi, w_lo, b_cat)


def _gather_rows(src_hbm, idx_ref, base, dst, sem):
    def body(j, carry):
        r = pl.multiple_of(idx_ref[base + j], ROW_TILE)
        pltpu.make_async_copy(src_hbm.at[pl.ds(r, ROW_TILE), :],
                              dst.at[pl.ds(pl.multiple_of(j * ROW_TILE, ROW_TILE), ROW_TILE), :],
                              sem).start()
        return carry

    lax.fori_loop(0, dst.shape[0] // ROW_TILE, body, 0, unroll=GATHER_UNROLL)


def _wait_rows(src_hbm, dst, sem):
    pltpu.make_async_copy(src_hbm.at[pl.ds(0, dst.shape[0]), :], dst, sem).wait()


def _dispatch_kernel(d1_ref, d2_ref, lo_ref, hi_ref, nt_ref, x_ref, xs_hbm, zero_ref, sem):
    rows = x_ref.shape[0]
    base = pl.program_id(0) * (rows // ROW_TILE)
    tile_rows = MOE_TM * ROW_TILE

    def pad_copy(p):
        return pltpu.make_async_copy(zero_ref.at[pl.ds(0, ROW_TILE), :],
                                     xs_hbm.at[pl.ds(pl.multiple_of(p * ROW_TILE, ROW_TILE), ROW_TILE), :], sem.at[1])

    def tile_copy(t):
        return pltpu.make_async_copy(zero_ref, xs_hbm.at[pl.ds(pl.multiple_of(t * tile_rows, tile_rows), tile_rows), :],
                                     sem.at[1])

    @pl.when(pl.program_id(0) == 0)
    def _():
        zero_ref[...] = jnp.zeros_like(zero_ref)

        def loop(lo, hi, make, wait):
            def body(p, carry):
                if wait:
                    make(p).wait()
                else:
                    make(p).start()
                return carry

            lax.fori_loop(lo, hi, body, 0)

        n_tiles = xs_hbm.shape[0] // tile_rows
        for wait in (False, True):
            for e in range(lo_ref.shape[0]):
                loop(lo_ref[e], hi_ref[e], pad_copy, wait)
            loop(nt_ref[0], n_tiles, tile_copy, wait)

    for idx_ref in (d1_ref, d2_ref):
        def body(j, carry, idx_ref=idx_ref):
            r = pl.multiple_of(idx_ref[base + j], ROW_TILE)
            pltpu.make_async_copy(x_ref.at[pl.ds(pl.multiple_of(j * ROW_TILE, ROW_TILE), ROW_TILE), :],
                                  xs_hbm.at[pl.ds(r, ROW_TILE), :], sem.at[0]).start()
            return carry

        lax.fori_loop(0, rows // ROW_TILE, body, 0, unroll=GATHER_UNROLL)
    for _ in range(2):
        pltpu.make_async_copy(x_ref, xs_hbm.at[pl.ds(0, rows), :], sem.at[0]).wait()


def _dispatch(xn_tiles, d1, d2, pad_lo, pad_hi, n_used, n_rows):
    n = d1.shape[0]
    tm = min(PROJ_TM, n)
    grid_spec = pltpu.PrefetchScalarGridSpec(
        num_scalar_prefetch=5,
        grid=(n // tm,),
        in_specs=[pl.BlockSpec((tm * ROW_TILE, LANES), lambda i, *_: (i, 0))],
        out_specs=pl.BlockSpec(memory_space=pl.ANY),
        scratch_shapes=[pltpu.VMEM((MOE_TM * ROW_TILE, LANES), F32), pltpu.SemaphoreType.DMA((2,))],
    )
    return pl.pallas_call(
        _dispatch_kernel,
        grid_spec=grid_spec,
        out_shape=jax.ShapeDtypeStruct((n_rows * ROW_TILE, LANES), F32),
        compiler_params=_cparams("arbitrary"),
        name="moe_dispatch",
    )(d1, d2, pad_lo, pad_hi, n_used, xn_tiles)


def _expert_kernel(te_ref, nt_ref, x_ref, wg_ref, wu_ref, wd_ref, o_ref, wgb, wub, wdb):
    tm = x_ref.shape[0] // ROW_TILE
    i = pl.program_id(0)

    @pl.when((i == 0) | (te_ref[i] != te_ref[jnp.maximum(i - 1, 0)]))
    def _():
        wgb[...] = wg_ref[0].astype(BF16)
        wub[...] = wu_ref[0].astype(BF16)
        wdb[...] = wd_ref[0].astype(BF16)

    @pl.when(i < nt_ref[0])
    def _():
        xn = _load_row_tiles(x_ref, tm).astype(BF16)
        gate = jnp.dot(xn, wgb[...], preferred_element_type=F32)
        up = jnp.dot(xn, wub[...], preferred_element_type=F32)
        hid = (jax.nn.silu(gate) * up).astype(BF16)
        _store_row_tiles(o_ref, jnp.dot(hid, wdb[...], preferred_element_type=F32))

    @pl.when(i >= nt_ref[0])
    def _():
        o_ref[...] = jnp.zeros_like(o_ref)


def _experts(xs, tile_expert, n_used, w_gate, w_up, w_down):
    d = D_MODEL
    n_tiles = tile_expert.shape[0]
    tm = MOE_TM
    de = w_gate.shape[-1]
    wspec = lambda rows, cols: pl.BlockSpec((1, rows, cols), lambda i, te, nt: (te[i], 0, 0))
    grid_spec = pltpu.PrefetchScalarGridSpec(
        num_scalar_prefetch=2,
        grid=(n_tiles,),
        in_specs=[pl.BlockSpec((tm * ROW_TILE, LANES), lambda i, te, nt: (jnp.minimum(i, nt[0] - 1), 0)),
                  wspec(d, de), wspec(d, de), wspec(de, d)],
        out_specs=pl.BlockSpec((tm * ROW_TILE, LANES), lambda i, te, nt: (i, 0)),
        scratch_shapes=[pltpu.VMEM((d, de), BF16), pltpu.VMEM((d, de), BF16), pltpu.VMEM((de, d), BF16)],
    )
    return pl.pallas_call(
        _expert_kernel,
        grid_spec=grid_spec,
        out_shape=jax.ShapeDtypeStruct((n_tiles * tm * ROW_TILE, LANES), F32),
        compiler_params=_cparams("arbitrary"),
        name="moe_experts",
    )(tile_expert, n_used, xs, w_gate, w_up, w_down)


def _combine_kernel(d1_ref, d2_ref, ys_hbm, h_ref, rg_ref, g_ref, *refs, n_proj, final_norm):
    w_refs, o_ref, p_refs = refs[:n_proj], refs[n_proj], refs[n_proj + 1:2 * n_proj + 1]
    y1, y2, sem = refs[2 * n_proj + 1:]
    tm = h_ref.shape[0]
    i = pl.program_id(0)
    slot = i % 2

    def gather(tile, s):
        _gather_rows(ys_hbm, d1_ref, tile * tm, y1.at[s], sem.at[0, s])
        _gather_rows(ys_hbm, d2_ref, tile * tm, y2.at[s], sem.at[1, s])

    @pl.when(i == 0)
    def _():
        gather(0, 0)

    @pl.when(i + 1 < pl.num_programs(0))
    def _():
        gather(i + 1, 1 - slot)

    _wait_rows(ys_hbm, y1.at[slot], sem.at[0, slot])
    _wait_rows(ys_hbm, y2.at[slot], sem.at[1, slot])
    rg = rg_ref[...]
    out = (h_ref[...] + rg[:, 4:5] * _load_row_tiles(y1.at[slot], tm)
           + rg[:, 5:6] * _load_row_tiles(y2.at[slot], tm))
    if final_norm:
        out = _rms_rows(out, g_ref[...])
    o_ref[...] = out
    if n_proj:
        y = _rms_rows(out, g_ref[...]).astype(BF16)
        for w_ref, p_ref in zip(w_refs, p_refs):
            p_ref[...] = jnp.dot(y, w_ref[...], preferred_element_type=F32)


def _combine(h, ys, d1, d2, rg, g, ws=(), final_norm=False):
    n, d = h.shape
    tm = min(PROJ_TM, n)
    row = lambda i, a, b: (i, 0)
    fixed = lambda i, a, b: (0, 0)
    grid_spec = pltpu.PrefetchScalarGridSpec(
        num_scalar_prefetch=2,
        grid=(n // tm,),
        in_specs=[pl.BlockSpec(memory_space=pl.ANY), pl.BlockSpec((tm, d), row), pl.BlockSpec((tm, LANES), row),
                  pl.BlockSpec((1, d), fixed)] + [pl.BlockSpec(w.shape, fixed) for w in ws],
        out_specs=[pl.BlockSpec((tm, d), row)] + [pl.BlockSpec((tm, w.shape[1]), row) for w in ws],
        scratch_shapes=[pltpu.VMEM((2, tm * ROW_TILE, LANES), F32), pltpu.VMEM((2, tm * ROW_TILE, LANES), F32),
                        pltpu.SemaphoreType.DMA((2, 2))],
    )
    return pl.pallas_call(
        functools.partial(_combine_kernel, n_proj=len(ws), final_norm=final_norm),
        grid_spec=grid_spec,
        out_shape=[jax.ShapeDtypeStruct((n, d), F32)] + [jax.ShapeDtypeStruct((n, w.shape[1]), F32) for w in ws],
        compiler_params=_cparams("arbitrary"),
        name="moe_combine",
    )(d1, d2, ys, h, rg, g.reshape(1, d), *ws)


def _hmoe_residual(h, xn_tiles, rg, cnt, dest, layer, w_gate, w_up, w_down, g, ws=(), final_norm=False):
    n, d = h.shape
    stack = lambda w: w.reshape((-1,) + w.shape[2:])

    counts = cnt[:N_EXPERTS, 0].astype(I32)
    n_tiles = (2 * n) // MOE_TM + N_EXPERTS
    tile_end = jnp.cumsum((counts + MOE_TM - 1) // MOE_TM)
    d1, d2 = dest[0], dest[1]
    tile_ids = jnp.arange(n_tiles, dtype=I32)
    tile_expert = jnp.minimum(jnp.sum((tile_end[None, :] <= tile_ids[:, None]).astype(I32), axis=1), N_EXPERTS - 1)
    tile_expert = tile_expert + layer * N_EXPERTS
    seg_end = tile_end * MOE_TM
    seg_start = seg_end - ((counts + MOE_TM - 1) // MOE_TM) * MOE_TM
    pad_lo = (seg_start + counts).astype(I32)
    pad_hi = seg_end.astype(I32)
    n_used = tile_end[N_EXPERTS - 1:].astype(I32)
    xs = _dispatch(xn_tiles, d1, d2, pad_lo, pad_hi, n_used, n_tiles * MOE_TM)
    ys = _experts(xs, tile_expert, n_used, stack(w_gate), stack(w_up), stack(w_down))
    return _combine(h, ys, d1, d2, rg, g, ws, final_norm)


def kernel(x, norm_mix, norm_ffn, norm_final, ev_w_in, ev_mu, rw_w0, rw_w2, rw_a0, rw_a2, rw_g2, rw_k_k, rw_k_a, rw_r_k, rw_ln_w, rw_ln_b, hg_lb_logits, hg_norm, ev_w_out, od_w_in, gd_conv, gd_a_log, gd_dt_bias, gd_norm, ml_i_bias, ml_f_bias, ml_norm, od_w_out, moe_w_group, moe_b_group, moe_w_router, moe_b_router, moe_w_gate, moe_w_up, moe_w_down):
    bsz, t, d = x.shape
    n = bsz * t
    depth = norm_mix.shape[0]
    lb_table = jnp.cumsum(jax.nn.softmax(hg_lb_logits.astype(F32), axis=0), axis=0)
    def in_proj_weights(layer):
        j = layer // 2
        if layer % 2 == 0:
            return _pack_weights(ev_w_in[j], [(0, RW_IN), (RW_IN, RW_IN + HG_IN)])
        return _pack_weights(od_w_in[j], [(0, GD_MAIN), (GD_IN, GD_IN + ML_MAIN)],
                             gate_ranges=[(GD_MAIN, GD_IN), (GD_IN + ML_MAIN, GD_IN + ML_MAIN + 2 * ML_HEADS)])

    h = x.reshape(n, d)
    proj = _rms_proj(h, norm_mix[0], in_proj_weights(0))
    for layer in range(depth):
        j = layer // 2
        if layer % 2 == 0:
            p_rw, p_hg = proj
            ya = _rwkv7(p_rw.reshape(bsz, t, RW_IN), ev_mu[j], rw_w0[j], rw_w2[j], rw_a0[j], rw_a2[j], rw_g2[j],
                        rw_k_k[j], rw_k_a[j], rw_r_k[j], rw_ln_w[j], rw_ln_b[j])
            yb = _hgrn2(p_hg.reshape(bsz, t, HG_IN), lb_table[j], hg_norm[j])
            mix_a, mix_b, w_out = ya.reshape(n, RW_DIM), yb.reshape(n, HG_KDIM), ev_w_out[j]
        else:
            p_gd, p_ml, p_gt = proj
            p_gt = p_gt.reshape(bsz, t, LANES)
            yc = _gdn(p_gd.reshape(bsz, t, GD_MAIN), p_gt, gd_conv[j], gd_a_log[j], gd_dt_bias[j], gd_norm[j])
            yd = _mlstm(p_ml.reshape(bsz, t, ML_MAIN), p_gt, ml_i_bias[j], ml_f_bias[j], ml_norm[j])
            mix_a, mix_b, w_out = yc.reshape(n, GD_DIM), yd.reshape(n, ML_DIM), od_w_out[j]
        h, xn_tiles, rg, cnt, dest = _out_proj_route(h, mix_a, mix_b, w_out, norm_ffn[layer], moe_w_group[layer],
                                                     moe_b_group[layer], moe_w_router[layer], moe_b_router[layer])
        if layer == depth - 1:
            (h,) = _hmoe_residual(h, xn_tiles, rg, cnt, dest, layer, moe_w_gate, moe_w_up, moe_w_down,
                                  norm_final, final_norm=True)
        else:
            h, *proj = _hmoe_residual(h, xn_tiles, rg, cnt, dest, layer, moe_w_gate, moe_w_up, moe_w_down,
                                      norm_mix[layer + 1], ws=in_proj_weights(layer + 1))
    return h.reshape(bsz, t, d)
```
